```python
import jax, jax.numpy as jnp
from jax import lax
import numpy as np

D_MODEL = 1024
BATCH = 16
SEQ = 2048
DEPTH = 4

D_CONV = D_MODEL
CONV_A_WIDTH = 31
D_RNN = 1536
N_RNN_HEADS = 16
RNN_HEAD_DIM = D_RNN // N_RNN_HEADS
CONV_B_WIDTH = 4
LRU_C = 8.0
D_FF = 4 * D_MODEL
EPS = 1e-6

SPLITS = (D_CONV, D_CONV, D_RNN, D_RNN, D_MODEL, D_MODEL)
D_IN = sum(SPLITS)

kernel_name = "hybrid_conformer_conv_rglru_gated_parallel"


def rms_norm(x, g):
    xf = x.astype(jnp.float32)
    y = xf * lax.rsqrt(jnp.mean(xf * xf, axis=-1, keepdims=True) + EPS)
    return (y * g.astype(jnp.float32)).astype(x.dtype)


def layer_norm(x, g, b):
    xf = x.astype(jnp.float32)
    mu = jnp.mean(xf, axis=-1, keepdims=True)
    xc = xf - mu
    y = xc * lax.rsqrt(jnp.mean(xc * xc, axis=-1, keepdims=True) + EPS)
    return (y * g.astype(jnp.float32) + b.astype(jnp.float32)).astype(x.dtype)


def causal_depthwise_conv(u, w, b):
    k = w.shape[0]
    y = lax.conv_general_dilated(
        u, w[:, None, :].astype(u.dtype), window_strides=(1,), padding=[(k - 1, 0)],
        dimension_numbers=("NWC", "WIO", "NWC"), feature_group_count=u.shape[-1])
    return y + b


def block_diag_linear(x, w, b):
    bsz, s, c = x.shape
    xh = x.reshape(bsz, s, N_RNN_HEADS, RNN_HEAD_DIM)
    y = jnp.einsum("bshi,hij->bshj", xh, w).reshape(bsz, s, c)
    return y + b


def rg_lru(x, w_a, b_a, w_x, b_x, lam):
    s = x.shape[1]
    r = jax.nn.sigmoid(block_diag_linear(x, w_a, b_a).astype(jnp.float32))
    i = jax.nn.sigmoid(block_diag_linear(x, w_x, b_x).astype(jnp.float32))
    log_a = -LRU_C * r * jax.nn.softplus(-lam.astype(jnp.float32))
    a = jnp.exp(log_a)
    mult = jnp.sqrt(-jnp.expm1(2.0 * log_a))
    is_start = (jnp.arange(s) == 0)[None, :, None]
    mult = jnp.where(is_start, 1.0, mult)
    bterm = mult * i * x.astype(jnp.float32)

    def combine(left, right):
        a_l, b_l = left
        a_r, b_r = right
        return a_r * a_l, a_r * b_l + b_r

    _, h = lax.associative_scan(combine, (a, bterm), axis=1)
    return h.astype(x.dtype)


def _fwd_setup_inputs(seed: int = 0) -> dict:
    key = jax.random.key(seed)
    ks = jax.random.split(key, 32)
    f32 = jnp.float32
    L = DEPTH

    def nrm(k, shape, scale):
        return jax.random.normal(k, shape, f32) * scale

    x = jax.random.normal(ks[0], (BATCH, SEQ, D_MODEL), f32)
    g_mix = 1.0 + nrm(ks[1], (L, D_MODEL), 0.02)
    w_in = nrm(ks[2], (L, D_MODEL, D_IN), D_MODEL ** -0.5)
    b_in = nrm(ks[3], (L, D_IN), 0.02)
    conv_a_w = nrm(ks[4], (L, CONV_A_WIDTH, D_CONV), CONV_A_WIDTH ** -0.5)
    conv_a_b = nrm(ks[5], (L, D_CONV), 0.02)
    ln_g = 1.0 + nrm(ks[6], (L, D_CONV), 0.02)
    ln_b = nrm(ks[7], (L, D_CONV), 0.02)
    w_a_out = nrm(ks[8], (L, D_CONV, D_MODEL), D_CONV ** -0.5)
    conv_b_w = nrm(ks[9], (L, CONV_B_WIDTH, D_RNN), CONV_B_WIDTH ** -0.5)
    conv_b_b = nrm(ks[10], (L, D_RNN), 0.02)
    w_rg_a = nrm(ks[11], (L, N_RNN_HEADS, RNN_HEAD_DIM, RNN_HEAD_DIM), RNN_HEAD_DIM ** -0.5)
    b_rg_a = nrm(ks[12], (L, D_RNN), 0.02)
    w_rg_x = nrm(ks[13], (L, N_RNN_HEADS, RNN_HEAD_DIM, RNN_HEAD_DIM), RNN_HEAD_DIM ** -0.5)
    b_rg_x = nrm(ks[14], (L, D_RNN), 0.02)
    a0 = jax.random.uniform(ks[15], (L, D_RNN), f32, 0.9, 0.999)
    s0 = a0 ** (1.0 / LRU_C)
    lam = jnp.log(s0) - jnp.log1p(-s0)
    w_b_out = nrm(ks[16], (L, D_RNN, D_MODEL), D_RNN ** -0.5)
    w_o = nrm(ks[17], (L, D_MODEL, D_MODEL), D_MODEL ** -0.5)
    g_mlp = 1.0 + nrm(ks[18], (L, D_MODEL), 0.02)
    w_1 = nrm(ks[19], (L, D_MODEL, D_FF), D_MODEL ** -0.5)
    w_2 = nrm(ks[20], (L, D_FF, D_MODEL), D_FF ** -0.5)
    g_final = 1.0 + nrm(ks[21], (D_MODEL,), 0.02)
    return {"x": x, "g_mix": g_mix, "w_in": w_in, "b_in": b_in,
            "conv_a_w": conv_a_w, "conv_a_b": conv_a_b, "ln_g": ln_g, "ln_b": ln_b,
            "w_a_out": w_a_out, "conv_b_w": conv_b_w, "conv_b_b": conv_b_b,
            "w_rg_a": w_rg_a, "b_rg_a": b_rg_a, "w_rg_x": w_rg_x, "b_rg_x": b_rg_x,
            "lam": lam, "w_b_out": w_b_out, "w_o": w_o, "g_mlp": g_mlp,
            "w_1": w_1, "w_2": w_2, "g_final": g_final}


def _fwd_reference(x, g_mix, w_in, b_in, conv_a_w, conv_a_b, ln_g, ln_b, w_a_out,
              conv_b_w, conv_b_b, w_rg_a, b_rg_a, w_rg_x, b_rg_x, lam, w_b_out,
              w_o, g_mlp, w_1, w_2, g_final):
    cuts = np.cumsum(SPLITS)[:-1].tolist()
    for l in range(DEPTH):
        h = rms_norm(x, g_mix[l])
        z = jnp.einsum("bsd,de->bse", h, w_in[l]) + b_in[l]
        va, ga, xb, gb, sa, sb = jnp.split(z, cuts, axis=-1)

        u = va * jax.nn.sigmoid(ga)
        u = causal_depthwise_conv(u, conv_a_w[l], conv_a_b[l])
        u = jax.nn.silu(layer_norm(u, ln_g[l], ln_b[l]))
        y_a = jnp.einsum("bsc,cd->bsd", u, w_a_out[l])

        v = causal_depthwise_conv(xb, conv_b_w[l], conv_b_b[l])
        v = rg_lru(v, w_rg_a[l], b_rg_a[l], w_rg_x[l], b_rg_x[l], lam[l])
        y_b = jnp.einsum("bsc,cd->bsd", v * jax.nn.gelu(gb), w_b_out[l])

        m = jax.nn.sigmoid(sa) * y_a + jax.nn.sigmoid(sb) * y_b
        x = x + jnp.einsum("bsd,de->bse", m, w_o[l])

        h = rms_norm(x, g_mlp[l])
        f = jnp.square(jax.nn.relu(jnp.einsum("bsd,df->bsf", h, w_1[l])))
        x = x + jnp.einsum("bsf,fd->bsd", f, w_2[l])
    return rms_norm(x, g_final)


import jax as _jax
import jax.numpy as _jnp

TWIN_FORMAT = 'train_step'
FWD_PARAMS = ['x', 'g_mix', 'w_in', 'b_in', 'conv_a_w', 'conv_a_b', 'ln_g', 'ln_b', 'w_a_out', 'conv_b_w', 'conv_b_b', 'w_rg_a', 'b_rg_a', 'w_rg_x', 'b_rg_x', 'lam', 'w_b_out', 'w_o', 'g_mlp', 'w_1', 'w_2', 'g_final']
TWIN_WEIGHTS = ['g_mix', 'w_in', 'b_in', 'conv_a_w', 'conv_a_b', 'ln_g', 'ln_b', 'w_a_out', 'conv_b_w', 'conv_b_b', 'w_rg_a', 'b_rg_a', 'w_rg_x', 'b_rg_x', 'lam', 'w_b_out', 'w_o', 'g_mlp', 'w_1', 'w_2', 'g_final']
TWIN_DIFF_INPUT = 'x'
TWIN_INPUTS = ['x', 'g_mix', 'w_in', 'b_in', 'conv_a_w', 'conv_a_b', 'ln_g', 'ln_b', 'w_a_out', 'conv_b_w', 'conv_b_b', 'w_rg_a', 'b_rg_a', 'w_rg_x', 'b_rg_x', 'lam', 'w_b_out', 'w_o', 'g_mlp', 'w_1', 'w_2', 'g_final', 'loss_target', 'm_g_mix', 'm_w_in', 'm_b_in', 'm_conv_a_w', 'm_conv_a_b', 'm_ln_g', 'm_ln_b', 'm_w_a_out', 'm_conv_b_w', 'm_conv_b_b', 'm_w_rg_a', 'm_b_rg_a', 'm_w_rg_x', 'm_b_rg_x', 'm_lam', 'm_w_b_out', 'm_w_o', 'm_g_mlp', 'm_w_1', 'm_w_2', 'm_g_final', 'v_g_mix', 'v_w_in', 'v_b_in', 'v_conv_a_w', 'v_conv_a_b', 'v_ln_g', 'v_ln_b', 'v_w_a_out', 'v_conv_b_w', 'v_conv_b_b', 'v_w_rg_a', 'v_b_rg_a', 'v_w_rg_x', 'v_b_rg_x', 'v_lam', 'v_w_b_out', 'v_w_o', 'v_g_mlp', 'v_w_1', 'v_w_2', 'v_g_final']
TWIN_OUTPUTS = ['loss', 'grad_x', 'grad_g_mix', 'grad_w_in', 'grad_b_in', 'grad_conv_a_w', 'grad_conv_a_b', 'grad_ln_g', 'grad_ln_b', 'grad_w_a_out', 'grad_conv_b_w', 'grad_conv_b_b', 'grad_w_rg_a', 'grad_b_rg_a', 'grad_w_rg_x', 'grad_b_rg_x', 'grad_lam', 'grad_w_b_out', 'grad_w_o', 'grad_g_mlp', 'grad_w_1', 'grad_w_2', 'grad_g_final', 'delta_g_mix', 'delta_w_in', 'delta_b_in', 'delta_conv_a_w', 'delta_conv_a_b', 'delta_ln_g', 'delta_ln_b', 'delta_w_a_out', 'delta_conv_b_w', 'delta_conv_b_b', 'delta_w_rg_a', 'delta_b_rg_a', 'delta_w_rg_x', 'delta_b_rg_x', 'delta_lam', 'delta_w_b_out', 'delta_w_o', 'delta_g_mlp', 'delta_w_1', 'delta_w_2', 'delta_g_final', 'new_m_g_mix', 'new_m_w_in', 'new_m_b_in', 'new_m_conv_a_w', 'new_m_conv_a_b', 'new_m_ln_g', 'new_m_ln_b', 'new_m_w_a_out', 'new_m_conv_b_w', 'new_m_conv_b_b', 'new_m_w_rg_a', 'new_m_b_rg_a', 'new_m_w_rg_x', 'new_m_b_rg_x', 'new_m_lam', 'new_m_w_b_out', 'new_m_w_o', 'new_m_g_mlp', 'new_m_w_1', 'new_m_w_2', 'new_m_g_final', 'new_v_g_mix', 'new_v_w_in', 'new_v_b_in', 'new_v_conv_a_w', 'new_v_conv_a_b', 'new_v_ln_g', 'new_v_ln_b', 'new_v_w_a_out', 'new_v_conv_b_w', 'new_v_conv_b_b', 'new_v_w_rg_a', 'new_v_b_rg_a', 'new_v_w_rg_x', 'new_v_b_rg_x', 'new_v_lam', 'new_v_w_b_out', 'new_v_w_o', 'new_v_g_mlp', 'new_v_w_1', 'new_v_w_2', 'new_v_g_final']
TWIN_LEAF_KINDS = {'loss': 'loss', 'grad_x': 'grad_x', 'grad_g_mix': 'grad_w', 'grad_w_in': 'grad_w', 'grad_b_in': 'grad_w', 'grad_conv_a_w': 'grad_w', 'grad_conv_a_b': 'grad_w', 'grad_ln_g': 'grad_w', 'grad_ln_b': 'grad_w', 'grad_w_a_out': 'grad_w', 'grad_conv_b_w': 'grad_w', 'grad_conv_b_b': 'grad_w', 'grad_w_rg_a': 'grad_w', 'grad_b_rg_a': 'grad_w', 'grad_w_rg_x': 'grad_w', 'grad_b_rg_x': 'grad_w', 'grad_lam': 'grad_w', 'grad_w_b_out': 'grad_w', 'grad_w_o': 'grad_w', 'grad_g_mlp': 'grad_w', 'grad_w_1': 'grad_w', 'grad_w_2': 'grad_w', 'grad_g_final': 'grad_w', 'delta_g_mix': 'delta_w', 'delta_w_in': 'delta_w', 'delta_b_in': 'delta_w', 'delta_conv_a_w': 'delta_w', 'delta_conv_a_b': 'delta_w', 'delta_ln_g': 'delta_w', 'delta_ln_b': 'delta_w', 'delta_w_a_out': 'delta_w', 'delta_conv_b_w': 'delta_w', 'delta_conv_b_b': 'delta_w', 'delta_w_rg_a': 'delta_w', 'delta_b_rg_a': 'delta_w', 'delta_w_rg_x': 'delta_w', 'delta_b_rg_x': 'delta_w', 'delta_lam': 'delta_w', 'delta_w_b_out': 'delta_w', 'delta_w_o': 'delta_w', 'delta_g_mlp': 'delta_w', 'delta_w_1': 'delta_w', 'delta_w_2': 'delta_w', 'delta_g_final': 'delta_w', 'new_m_g_mix': 'new_m', 'new_m_w_in': 'new_m', 'new_m_b_in': 'new_m', 'new_m_conv_a_w': 'new_m', 'new_m_conv_a_b': 'new_m', 'new_m_ln_g': 'new_m', 'new_m_ln_b': 'new_m', 'new_m_w_a_out': 'new_m', 'new_m_conv_b_w': 'new_m', 'new_m_conv_b_b': 'new_m', 'new_m_w_rg_a': 'new_m', 'new_m_b_rg_a': 'new_m', 'new_m_w_rg_x': 'new_m', 'new_m_b_rg_x': 'new_m', 'new_m_lam': 'new_m', 'new_m_w_b_out': 'new_m', 'new_m_w_o': 'new_m', 'new_m_g_mlp': 'new_m', 'new_m_w_1': 'new_m', 'new_m_w_2': 'new_m', 'new_m_g_final': 'new_m', 'new_v_g_mix': 'new_v', 'new_v_w_in': 'new_v', 'new_v_b_in': 'new_v', 'new_v_conv_a_w': 'new_v', 'new_v_conv_a_b': 'new_v', 'new_v_ln_g': 'new_v', 'new_v_ln_b': 'new_v', 'new_v_w_a_out': 'new_v', 'new_v_conv_b_w': 'new_v', 'new_v_conv_b_b': 'new_v', 'new_v_w_rg_a': 'new_v', 'new_v_b_rg_a': 'new_v', 'new_v_w_rg_x': 'new_v', 'new_v_b_rg_x': 'new_v', 'new_v_lam': 'new_v', 'new_v_w_b_out': 'new_v', 'new_v_w_o': 'new_v', 'new_v_g_mlp': 'new_v', 'new_v_w_1': 'new_v', 'new_v_w_2': 'new_v', 'new_v_g_final': 'new_v'}


def _forward(args):
    return _fwd_reference(*[args[k] for k in FWD_PARAMS])


def _output_shape():
    out = _jax.eval_shape(lambda: _forward(_fwd_setup_inputs(0)))
    return out.shape, out.dtype

N_MICROBATCH = 1
ADAM_LR = 0.001
ADAM_B1 = 0.9
ADAM_B2 = 0.999
ADAM_EPS = 1e-08
ADAM_WD = 0.01
ADAM_STEP = 10
PER_EXAMPLE_BATCH_AXIS = {'x': 0, 'loss_target': 0}
SHARED_INPUTS = []
_WEIGHT_DTYPES = {'g_mix': _jnp.float32, 'w_in': _jnp.float32, 'b_in': _jnp.float32, 'conv_a_w': _jnp.float32, 'conv_a_b': _jnp.float32, 'ln_g': _jnp.float32, 'ln_b': _jnp.float32, 'w_a_out': _jnp.float32, 'conv_b_w': _jnp.float32, 'conv_b_b': _jnp.float32, 'w_rg_a': _jnp.float32, 'b_rg_a': _jnp.float32, 'w_rg_x': _jnp.float32, 'b_rg_x': _jnp.float32, 'lam': _jnp.float32, 'w_b_out': _jnp.float32, 'w_o': _jnp.float32, 'g_mlp': _jnp.float32, 'w_1': _jnp.float32, 'w_2': _jnp.float32, 'g_final': _jnp.float32}
MOMENT_SCALE = {'g_mix': 1.482539e-01, 'w_in': 5.667868e-02, 'b_in': 1.981032e-01, 'conv_a_w': 4.548773e-02, 'conv_a_b': 1.281244e-01, 'ln_g': 6.579137e-02, 'ln_b': 7.807261e-02, 'w_a_out': 5.024982e-02, 'conv_b_w': 8.915048e-02, 'conv_b_b': 3.899976e-01, 'w_rg_a': 1.229781e-02, 'b_rg_a': 1.767501e-02, 'w_rg_x': 2.538934e-02, 'b_rg_x': 3.698960e-02, 'lam': 4.443283e-02, 'w_b_out': 1.064940e-01, 'w_o': 1.141158e-01, 'g_mlp': 1.336823e-01, 'w_1': 6.673462e-02, 'w_2': 1.569315e-01, 'g_final': 3.317452e+01}


def _to_microbatches(a, axis):
    t = _jnp.moveaxis(a, axis, 0)
    t = t.reshape((N_MICROBATCH, t.shape[0] // N_MICROBATCH) + t.shape[1:])
    return _jnp.moveaxis(t, 1, axis + 1)


def setup_inputs(seed: int = 0) -> dict:
    inp = _fwd_setup_inputs(seed)
    key = _jax.random.fold_in(_jax.random.key(seed), 7919)
    shape, _ = _output_shape()
    out = dict(inp)
    out["loss_target"] = _jax.random.normal(_jax.random.fold_in(key, 0), shape, _jnp.float32)
    for i, name in enumerate(TWIN_WEIGHTS):
        w = inp[name].astype(_jnp.float32)
        if MOMENT_SCALE is None:
            s = _jnp.sqrt(_jnp.mean(_jnp.square(w)) + 1e-30)
        else:
            s = MOMENT_SCALE[name]
        km, kv = _jax.random.split(_jax.random.fold_in(key, i + 1))
        out[name] = w
        out["m_" + name] = s * _jax.random.normal(km, w.shape, _jnp.float32)
        out["v_" + name] = (s * s) * _jax.random.uniform(kv, w.shape, _jnp.float32, 0.5, 1.5)
    if N_MICROBATCH > 1:
        for name, axis in PER_EXAMPLE_BATCH_AXIS.items():
            out[name] = _to_microbatches(out[name], axis)
    return {'x': out['x'], 'g_mix': out['g_mix'], 'w_in': out['w_in'], 'b_in': out['b_in'], 'conv_a_w': out['conv_a_w'], 'conv_a_b': out['conv_a_b'], 'ln_g': out['ln_g'], 'ln_b': out['ln_b'], 'w_a_out': out['w_a_out'], 'conv_b_w': out['conv_b_w'], 'conv_b_b': out['conv_b_b'], 'w_rg_a': out['w_rg_a'], 'b_rg_a': out['b_rg_a'], 'w_rg_x': out['w_rg_x'], 'b_rg_x': out['b_rg_x'], 'lam': out['lam'], 'w_b_out': out['w_b_out'], 'w_o': out['w_o'], 'g_mlp': out['g_mlp'], 'w_1': out['w_1'], 'w_2': out['w_2'], 'g_final': out['g_final'], 'loss_target': out['loss_target'], 'm_g_mix': out['m_g_mix'], 'm_w_in': out['m_w_in'], 'm_b_in': out['m_b_in'], 'm_conv_a_w': out['m_conv_a_w'], 'm_conv_a_b': out['m_conv_a_b'], 'm_ln_g': out['m_ln_g'], 'm_ln_b': out['m_ln_b'], 'm_w_a_out': out['m_w_a_out'], 'm_conv_b_w': out['m_conv_b_w'], 'm_conv_b_b': out['m_conv_b_b'], 'm_w_rg_a': out['m_w_rg_a'], 'm_b_rg_a': out['m_b_rg_a'], 'm_w_rg_x': out['m_w_rg_x'], 'm_b_rg_x': out['m_b_rg_x'], 'm_lam': out['m_lam'], 'm_w_b_out': out['m_w_b_out'], 'm_w_o': out['m_w_o'], 'm_g_mlp': out['m_g_mlp'], 'm_w_1': out['m_w_1'], 'm_w_2': out['m_w_2'], 'm_g_final': out['m_g_final'], 'v_g_mix': out['v_g_mix'], 'v_w_in': out['v_w_in'], 'v_b_in': out['v_b_in'], 'v_conv_a_w': out['v_conv_a_w'], 'v_conv_a_b': out['v_conv_a_b'], 'v_ln_g': out['v_ln_g'], 'v_ln_b': out['v_ln_b'], 'v_w_a_out': out['v_w_a_out'], 'v_conv_b_w': out['v_conv_b_w'], 'v_conv_b_b': out['v_conv_b_b'], 'v_w_rg_a': out['v_w_rg_a'], 'v_b_rg_a': out['v_b_rg_a'], 'v_w_rg_x': out['v_w_rg_x'], 'v_b_rg_x': out['v_b_rg_x'], 'v_lam': out['v_lam'], 'v_w_b_out': out['v_w_b_out'], 'v_w_o': out['v_w_o'], 'v_g_mlp': out['v_g_mlp'], 'v_w_1': out['v_w_1'], 'v_w_2': out['v_w_2'], 'v_g_final': out['v_g_final']}


def _loss(weights, diff, rest, loss_target):
    with _jax.named_scope("forward"):
        args = {**rest, TWIN_DIFF_INPUT: diff, **{k: w.astype(_WEIGHT_DTYPES[k]) for k, w in weights.items()}}
        y = _forward(args)
    with _jax.named_scope("loss_head"):
        err = _jnp.square(y.astype(_jnp.float32) - loss_target)
        return 0.5 * _jnp.sum(_jnp.mean(err, axis=-1)) if err.ndim else 0.5 * err


def _adamw(w, g, m, v):
    m = ADAM_B1 * m + (1.0 - ADAM_B1) * g
    v = ADAM_B2 * v + (1.0 - ADAM_B2) * _jnp.square(g)
    m_hat = m / (1.0 - ADAM_B1 ** ADAM_STEP)
    v_hat = v / (1.0 - ADAM_B2 ** ADAM_STEP)
    delta = -ADAM_LR * (m_hat / (_jnp.sqrt(v_hat) + ADAM_EPS) + ADAM_WD * w)
    return delta, m, v


def reference(x, g_mix, w_in, b_in, conv_a_w, conv_a_b, ln_g, ln_b, w_a_out, conv_b_w, conv_b_b, w_rg_a, b_rg_a, w_rg_x, b_rg_x, lam, w_b_out, w_o, g_mlp, w_1, w_2, g_final, loss_target, m_g_mix, m_w_in, m_b_in, m_conv_a_w, m_conv_a_b, m_ln_g, m_ln_b, m_w_a_out, m_conv_b_w, m_conv_b_b, m_w_rg_a, m_b_rg_a, m_w_rg_x, m_b_rg_x, m_lam, m_w_b_out, m_w_o, m_g_mlp, m_w_1, m_w_2, m_g_final, v_g_mix, v_w_in, v_b_in, v_conv_a_w, v_conv_a_b, v_ln_g, v_ln_b, v_w_a_out, v_conv_b_w, v_conv_b_b, v_w_rg_a, v_b_rg_a, v_w_rg_x, v_b_rg_x, v_lam, v_w_b_out, v_w_o, v_g_mlp, v_w_1, v_w_2, v_g_final):
    given = dict(x=x, g_mix=g_mix, w_in=w_in, b_in=b_in, conv_a_w=conv_a_w, conv_a_b=conv_a_b, ln_g=ln_g, ln_b=ln_b, w_a_out=w_a_out, conv_b_w=conv_b_w, conv_b_b=conv_b_b, w_rg_a=w_rg_a, b_rg_a=b_rg_a, w_rg_x=w_rg_x, b_rg_x=b_rg_x, lam=lam, w_b_out=w_b_out, w_o=w_o, g_mlp=g_mlp, w_1=w_1, w_2=w_2, g_final=g_final, loss_target=loss_target, m_g_mix=m_g_mix, m_w_in=m_w_in, m_b_in=m_b_in, m_conv_a_w=m_conv_a_w, m_conv_a_b=m_conv_a_b, m_ln_g=m_ln_g, m_ln_b=m_ln_b, m_w_a_out=m_w_a_out, m_conv_b_w=m_conv_b_w, m_conv_b_b=m_conv_b_b, m_w_rg_a=m_w_rg_a, m_b_rg_a=m_b_rg_a, m_w_rg_x=m_w_rg_x, m_b_rg_x=m_b_rg_x, m_lam=m_lam, m_w_b_out=m_w_b_out, m_w_o=m_w_o, m_g_mlp=m_g_mlp, m_w_1=m_w_1, m_w_2=m_w_2, m_g_final=m_g_final, v_g_mix=v_g_mix, v_w_in=v_w_in, v_b_in=v_b_in, v_conv_a_w=v_conv_a_w, v_conv_a_b=v_conv_a_b, v_ln_g=v_ln_g, v_ln_b=v_ln_b, v_w_a_out=v_w_a_out, v_conv_b_w=v_conv_b_w, v_conv_b_b=v_conv_b_b, v_w_rg_a=v_w_rg_a, v_b_rg_a=v_b_rg_a, v_w_rg_x=v_w_rg_x, v_b_rg_x=v_b_rg_x, v_lam=v_lam, v_w_b_out=v_w_b_out, v_w_o=v_w_o, v_g_mlp=v_g_mlp, v_w_1=v_w_1, v_w_2=v_w_2, v_g_final=v_g_final)
    weights = {n: given[n] for n in TWIN_WEIGHTS}
    shared = {n: given[n] for n in SHARED_INPUTS}
    per_example = {n: given[n] for n in ['x']}
    grad_fn = _jax.value_and_grad(_loss, argnums=(0, 1))

    def one_microbatch(ex, loss_target):
        ex = dict(ex)
        diff = ex.pop(TWIN_DIFF_INPUT)
        return grad_fn(weights, diff, {**shared, **ex}, loss_target)

    if N_MICROBATCH == 1:
        loss, (grad_w, grad_x) = one_microbatch(per_example, given["loss_target"])
    else:
        def body(carry, xs):
            loss_sum, grad_sum = carry
            l_k, (gw_k, gx_k) = one_microbatch(xs[0], xs[1])
            with _jax.named_scope("update"):
                return (loss_sum + l_k, _jax.tree.map(_jnp.add, grad_sum, gw_k)), gx_k

        init = (_jnp.zeros((), _jnp.float32), _jax.tree.map(_jnp.zeros_like, weights))
        (loss, grad_w), grad_x = _jax.lax.scan(body, init, (per_example, given["loss_target"]))
    with _jax.named_scope("update"):
        delta_w, new_m, new_v = {}, {}, {}
        for n in TWIN_WEIGHTS:
            delta_w[n], new_m[n], new_v[n] = _adamw(weights[n], grad_w[n], given["m_" + n], given["v_" + n])
    return (loss, grad_x, *[grad_w[n] for n in TWIN_WEIGHTS], *[delta_w[n] for n in TWIN_WEIGHTS],
            *[new_m[n] for n in TWIN_WEIGHTS], *[new_v[n] for n in TWIN_WEIGHTS])
```

```python
import functools

import jax
import jax.numpy as jnp
from jax import lax
from jax.experimental import pallas as pl
from jax.experimental.pallas import tpu as pltpu

F32 = jnp.float32
BF16 = jnp.bfloat16

EPS = 1e-6
LRU_C = 8.0
N_RNN_HEADS = 16
HEADS_PER_GROUP = 4
N_DEV = 8
ADAM_LR, ADAM_B1, ADAM_B2, ADAM_EPS, ADAM_WD, ADAM_STEP = 0.001, 0.9, 0.999, 1e-08, 0.01, 10

VMEM_LIMIT_BYTES = 48 * 1024 * 1024
CONV_PAD = 32
CONV_CHUNK = 128
SUBLANES = 8


def _cp(*sem):
    return pltpu.CompilerParams(dimension_semantics=sem, vmem_limit_bytes=VMEM_LIMIT_BYTES)


def _sig(x):
    return 1.0 / (1.0 + jnp.exp(-x))


def _gelu(x):
    c = 0.7978845608028654
    return 0.5 * x * (1.0 + jnp.tanh(c * (x + 0.044715 * x * x * x)))


def _gelu_grad(x):
    c = 0.7978845608028654
    th = jnp.tanh(c * (x + 0.044715 * x * x * x))
    return 0.5 * (1.0 + th) + 0.5 * x * (1.0 - th * th) * c * (1.0 + 3.0 * 0.044715 * x * x)


def _mm_nn(a, b, *, tm, tn, name, bias=None, resid=None, relu2=False, out_dtype=F32):
    M, K = a.shape
    blocked = b.ndim == 3
    N = b.shape[0] * b.shape[2] if blocked else b.shape[1]
    tm = min(tm, M)
    tn = min(tn, N)
    if blocked:
        assert tn == b.shape[2]
    n_extra = (bias is not None) + (resid is not None)

    def body(*refs):
        acc = jnp.dot(refs[0][...].astype(BF16), refs[1][...].astype(BF16), preferred_element_type=F32)
        k = 2
        if bias is not None:
            acc = acc + refs[k][...]
            k += 1
        if resid is not None:
            acc = acc + refs[k][...]
            k += 1
        if relu2:
            refs[k][...] = acc
            p = jnp.maximum(acc, 0.0)
            refs[k + 1][...] = (p * p).astype(BF16)
        else:
            refs[k][...] = acc.astype(out_dtype)

    in_specs = [pl.BlockSpec((tm, K), lambda j, i: (i, 0))]
    if blocked:
        in_specs.append(pl.BlockSpec((None, K, tn), lambda j, i: (j, 0, 0)))
    else:
        in_specs.append(pl.BlockSpec((K, tn), lambda j, i: (0, j)))
    args = [a, b]
    if bias is not None:
        in_specs.append(pl.BlockSpec((1, tn), lambda j, i: (0, j)))
        args.append(bias)
    if resid is not None:
        in_specs.append(pl.BlockSpec((tm, tn), lambda j, i: (i, j)))
        args.append(resid)
    o_spec = pl.BlockSpec((tm, tn), lambda j, i: (i, j))
    if relu2:
        out_shape = (jax.ShapeDtypeStruct((M, N), F32), jax.ShapeDtypeStruct((M, N), BF16))
        out_specs = (o_spec, o_spec)
    else:
        out_shape = jax.ShapeDtypeStruct((M, N), out_dtype)
        out_specs = o_spec
    del n_extra
    return pl.pallas_call(body, grid=(N // tn, M // tm), in_specs=in_specs, out_specs=out_specs,
                          out_shape=out_shape, compiler_params=_cp("parallel", "parallel"), name=name)(*args)


def _mm_nt(a, b, *, tm, tn, tk, name, mul_relu=None, resid=None, out_dtype=F32):
    M, N = a.shape
    blocked = b.ndim == 3
    Kout = b.shape[1] if blocked else b.shape[0]
    tm = min(tm, M)
    tn = min(tn, Kout)
    tk = b.shape[2] if blocked else min(tk, N)
    nk = N // tk

    def body(*refs):
        acc_ref = refs[-1]
        kk = pl.program_id(2)
        part = lax.dot_general(refs[0][...].astype(BF16), refs[1][...].astype(BF16),
                               (((1,), (1,)), ((), ())), preferred_element_type=F32)

        @pl.when(kk == 0)
        def _():
            acc_ref[...] = part

        @pl.when(kk > 0)
        def _():
            acc_ref[...] += part

        @pl.when(kk == nk - 1)
        def _():
            acc = acc_ref[...]
            k = 2
            if mul_relu is not None:
                acc = acc * (2.0 * jnp.maximum(refs[k][...], 0.0))
                k += 1
            if resid is not None:
                acc = acc + refs[k][...]
                k += 1
            refs[k][...] = acc.astype(out_dtype)

    in_specs = [pl.BlockSpec((tm, tk), lambda i, j, k: (i, k))]
    if blocked:
        in_specs.append(pl.BlockSpec((None, tn, tk), lambda i, j, k: (k, j, 0)))
    else:
        in_specs.append(pl.BlockSpec((tn, tk), lambda i, j, k: (j, k)))
    args = [a, b]
    for extra in (mul_relu, resid):
        if extra is not None:
            in_specs.append(pl.BlockSpec((tm, tn), lambda i, j, k: (i, j)))
            args.append(extra)
    return pl.pallas_call(body, grid=(M // tm, Kout // tn, nk), in_specs=in_specs,
                          out_specs=pl.BlockSpec((tm, tn), lambda i, j, k: (i, j)),
                          out_shape=jax.ShapeDtypeStruct((M, Kout), out_dtype),
                          scratch_shapes=[pltpu.VMEM((tm, tn), F32)],
                          compiler_params=_cp("parallel", "parallel", "arbitrary"), name=name)(*args)


def _mm_tn(a, b, *, tm, tn, tk, name, out_blocks=None, colsum=False):
    T, M = a.shape
    N = b.shape[1]
    tm = min(tm, M)
    tk = min(tk, T)
    if out_blocks is not None:
        tn = N // out_blocks
        tm = M
    tn = min(tn, N)
    nk = T // tk
    if colsum:
        assert tm == M

    def body(*refs):
        a_ref, b_ref, o_ref = refs[0], refs[1], refs[2]
        kk = pl.program_id(2)
        bv = b_ref[...]
        part = lax.dot_general(a_ref[...].astype(BF16), bv.astype(BF16),
                               (((0,), (0,)), ((), ())), preferred_element_type=F32)

        if colsum:
            csum = jnp.broadcast_to(jnp.sum(bv.astype(F32), axis=0, keepdims=True), (SUBLANES, tn))

        @pl.when(kk == 0)
        def _():
            o_ref[...] = part
            if colsum:
                refs[3][...] = csum

        @pl.when(kk > 0)
        def _():
            o_ref[...] += part
            if colsum:
                refs[3][...] += csum

    in_specs = [pl.BlockSpec((tk, tm), lambda i, j, k: (k, i)), pl.BlockSpec((tk, tn), lambda i, j, k: (k, j))]
    if out_blocks is not None:
        o_shape = jax.ShapeDtypeStruct((out_blocks, M, tn), F32)
        o_spec = pl.BlockSpec((None, M, tn), lambda i, j, k: (j, 0, 0))
    else:
        o_shape = jax.ShapeDtypeStruct((M, N), F32)
        o_spec = pl.BlockSpec((tm, tn), lambda i, j, k: (i, j))
    if colsum:
        out_shape = (o_shape, jax.ShapeDtypeStruct((SUBLANES, N), F32))
        out_specs = (o_spec, pl.BlockSpec((SUBLANES, tn), lambda i, j, k: (0, j)))
    else:
        out_shape, out_specs = o_shape, o_spec
    return pl.pallas_call(body, grid=(M // tm, N // tn, nk), in_specs=in_specs, out_specs=out_specs,
                          out_shape=out_shape, compiler_params=_cp("parallel", "parallel", "arbitrary"),
                          name=name)(a, b)


def _rms_fwd(x, g, *, name, tr=512):
    T, D = x.shape
    tr = min(tr, T)

    def body(x_ref, g_ref, h_ref):
        xv = x_ref[...]
        r = lax.rsqrt(jnp.mean(xv * xv, axis=-1, keepdims=True) + EPS)
        h_ref[...] = (xv * r * g_ref[...]).astype(BF16)

    return pl.pallas_call(body, grid=(T // tr,),
                          in_specs=[pl.BlockSpec((tr, D), lambda i: (i, 0)), pl.BlockSpec((1, D), lambda i: (0, 0))],
                          out_specs=pl.BlockSpec((tr, D), lambda i: (i, 0)),
                          out_shape=jax.ShapeDtypeStruct((T, D), BF16), compiler_params=_cp("parallel"), name=name)(x, g)


def _rms_bwd(x, g, dh, dres, *, name, tr=512):
    T, D = x.shape
    tr = min(tr, T)

    def body(x_ref, g_ref, dh_ref, dres_ref, dx_ref, dg_ref):
        xv = x_ref[...]
        r = lax.rsqrt(jnp.mean(xv * xv, axis=-1, keepdims=True) + EPS)
        n = xv * r
        dh = dh_ref[...]
        dn = dh * g_ref[...]
        dx_ref[...] = dres_ref[...] + r * (dn - n * jnp.mean(dn * n, axis=-1, keepdims=True))
        part = jnp.sum(dh * n, axis=0, keepdims=True)

        @pl.when(pl.program_id(0) == 0)
        def _():
            dg_ref[...] = part

        @pl.when(pl.program_id(0) > 0)
        def _():
            dg_ref[...] += part

    row = pl.BlockSpec((tr, D), lambda i: (i, 0))
    vec = pl.BlockSpec((1, D), lambda i: (0, 0))
    return pl.pallas_call(body, grid=(T // tr,), in_specs=[row, vec, row, row], out_specs=(row, vec),
                          out_shape=(jax.ShapeDtypeStruct((T, D), F32), jax.ShapeDtypeStruct((1, D), F32)),
                          compiler_params=_cp("arbitrary"), name=name)(x, g, dh, dres)


def _final_loss(x, g, tgt, *, name, tr=512):
    T, D = x.shape
    tr = min(tr, T)

    def body(x_ref, g_ref, t_ref, loss_ref, dx_ref, dg_ref):
        xv = x_ref[...]
        gv = g_ref[...]
        r = lax.rsqrt(jnp.mean(xv * xv, axis=-1, keepdims=True) + EPS)
        n = xv * r
        e = n * gv - t_ref[...]
        lpart = 0.5 * jnp.sum(jnp.mean(e * e, axis=-1, keepdims=True), axis=0, keepdims=True)
        dy = e * (1.0 / D)
        dn = dy * gv
        dx_ref[...] = r * (dn - n * jnp.mean(dn * n, axis=-1, keepdims=True))
        gpart = jnp.sum(dy * n, axis=0, keepdims=True)

        @pl.when(pl.program_id(0) == 0)
        def _():
            dg_ref[...] = gpart
            loss_ref[...] = jnp.broadcast_to(lpart, (1, 128))

        @pl.when(pl.program_id(0) > 0)
        def _():
            dg_ref[...] += gpart
            loss_ref[...] += jnp.broadcast_to(lpart, (1, 128))

    row = pl.BlockSpec((tr, D), lambda i: (i, 0))
    vec = pl.BlockSpec((1, D), lambda i: (0, 0))
    return pl.pallas_call(body, grid=(T // tr,), in_specs=[row, vec, row],
                          out_specs=(pl.BlockSpec((1, 128), lambda i: (0, 0)), row, vec),
                          out_shape=(jax.ShapeDtypeStruct((1, 128), F32), jax.ShapeDtypeStruct((T, D), F32),
                                     jax.ShapeDtypeStruct((1, D), F32)),
                          compiler_params=_cp("arbitrary"), name=name)(x, g, tgt)


def _ln_silu_fwd(u, g, b, *, name, tr=512):
    T, C = u.shape
    tr = min(tr, T)

    def body(u_ref, g_ref, b_ref, o_ref):
        uv = u_ref[...]
        mu = jnp.mean(uv, axis=-1, keepdims=True)
        xc = uv - mu
        r = lax.rsqrt(jnp.mean(xc * xc, axis=-1, keepdims=True) + EPS)
        y = xc * r * g_ref[...] + b_ref[...]
        o_ref[...] = (y * _sig(y)).astype(BF16)

    row = pl.BlockSpec((tr, C), lambda i: (i, 0))
    vec = pl.BlockSpec((1, C), lambda i: (0, 0))
    return pl.pallas_call(body, grid=(T // tr,), in_specs=[row, vec, vec], out_specs=row,
                          out_shape=jax.ShapeDtypeStruct((T, C), BF16), compiler_params=_cp("parallel"),
                          name=name)(u, g, b)


def _ln_silu_bwd(u, g, b, do, *, name, tr=512):
    T, C = u.shape
    tr = min(tr, T)

    def body(u_ref, g_ref, b_ref, do_ref, du_ref, dg_ref, db_ref):
        uv = u_ref[...]
        gv = g_ref[...]
        mu = jnp.mean(uv, axis=-1, keepdims=True)
        xc = uv - mu
        r = lax.rsqrt(jnp.mean(xc * xc, axis=-1, keepdims=True) + EPS)
        n = xc * r
        y = n * gv + b_ref[...]
        s = _sig(y)
        dy = do_ref[...] * (s * (1.0 + y * (1.0 - s)))
        dn = dy * gv
        du_ref[...] = r * (dn - jnp.mean(dn, axis=-1, keepdims=True) - n * jnp.mean(dn * n, axis=-1, keepdims=True))
        gpart = jnp.sum(dy * n, axis=0, keepdims=True)
        bpart = jnp.sum(dy, axis=0, keepdims=True)

        @pl.when(pl.program_id(0) == 0)
        def _():
            dg_ref[...] = gpart
            db_ref[...] = bpart

        @pl.when(pl.program_id(0) > 0)
        def _():
            dg_ref[...] += gpart
            db_ref[...] += bpart

    row = pl.BlockSpec((tr, C), lambda i: (i, 0))
    vec = pl.BlockSpec((1, C), lambda i: (0, 0))
    return pl.pallas_call(body, grid=(T // tr,), in_specs=[row, vec, vec, row], out_specs=(row, vec, vec),
                          out_shape=(jax.ShapeDtypeStruct((T, C), F32), jax.ShapeDtypeStruct((1, C), F32),
                                     jax.ShapeDtypeStruct((1, C), F32)),
                          compiler_params=_cp("arbitrary"), name=name)(u, g, b, do)


def _merge_fwd(z, ya, yb, *, off_sa, off_sb, name, tr=512):
    T, D = ya.shape
    tr = min(tr, T)
    assert off_sa % D == 0 and off_sb % D == 0

    def body(sa_ref, sb_ref, ya_ref, yb_ref, m_ref):
        m_ref[...] = (_sig(sa_ref[...]) * ya_ref[...] + _sig(sb_ref[...]) * yb_ref[...]).astype(BF16)

    row = pl.BlockSpec((tr, D), lambda i: (i, 0))
    return pl.pallas_call(body, grid=(T // tr,),
                          in_specs=[pl.BlockSpec((tr, D), lambda i: (i, off_sa // D)),
                                    pl.BlockSpec((tr, D), lambda i: (i, off_sb // D)), row, row],
                          out_specs=row, out_shape=jax.ShapeDtypeStruct((T, D), BF16),
                          compiler_params=_cp("parallel"), name=name)(z, z, ya, yb)


def _merge_bwd(z, ya, yb, dm, *, off_sa, off_sb, name, tr=512):
    T, D = ya.shape
    tr = min(tr, T)

    def body(sa_ref, sb_ref, ya_ref, yb_ref, dm_ref, dya_ref, dyb_ref, dsa_ref, dsb_ref):
        dm = dm_ref[...]
        ga = _sig(sa_ref[...])
        gb = _sig(sb_ref[...])
        dya_ref[...] = (dm * ga).astype(BF16)
        dyb_ref[...] = (dm * gb).astype(BF16)
        dsa_ref[...] = (dm * ya_ref[...] * ga * (1.0 - ga)).astype(BF16)
        dsb_ref[...] = (dm * yb_ref[...] * gb * (1.0 - gb)).astype(BF16)

    row = pl.BlockSpec((tr, D), lambda i: (i, 0))
    o = jax.ShapeDtypeStruct((T, D), BF16)
    return pl.pallas_call(body, grid=(T // tr,),
                          in_specs=[pl.BlockSpec((tr, D), lambda i: (i, off_sa // D)),
                                    pl.BlockSpec((tr, D), lambda i: (i, off_sb // D)), row, row, row],
                          out_specs=(row, row, row, row), out_shape=(o, o, o, o),
                          compiler_params=_cp("parallel"), name=name)(z, z, ya, yb, dm)


def _conv_taps(pad_ref, w_ref, K, base, rows, lanes, reverse):
    acc = None
    for s in range(K):
        st = base + s if reverse else base - s
        term = w_ref[K - 1 - s:K - s, lanes] * pad_ref[st:st + rows, lanes]
        acc = term if acc is None else acc + term
    return acc


def _conv_fwd(z, w, b, *, S, off_v, off_g, name, ct=256):
    T = z.shape[0]
    K, C = w.shape
    ct = min(ct, C)
    ch = min(CONV_CHUNK, S)
    glu = off_g is not None
    assert off_v % ct == 0 and (not glu or off_g % ct == 0) and K - 1 <= CONV_PAD

    def body(*refs):
        if glu:
            v_ref, g_ref, w_ref, b_ref, o_ref, pad_ref = refs
        else:
            v_ref, w_ref, b_ref, o_ref, pad_ref = refs
        pad_ref[0:CONV_PAD, :] = jnp.zeros((CONV_PAD, ct), F32)
        if glu:
            pad_ref[CONV_PAD:CONV_PAD + S, :] = v_ref[...] * _sig(g_ref[...])
        else:
            pad_ref[CONV_PAD:CONV_PAD + S, :] = v_ref[...]
        for l0 in range(0, ct, 128):
            lanes = slice(l0, l0 + 128)
            for c in range(S // ch):
                acc = _conv_taps(pad_ref, w_ref, K, CONV_PAD + c * ch, ch, lanes, False)
                o_ref[c * ch:(c + 1) * ch, lanes] = acc + b_ref[:, lanes]

    in_specs = [pl.BlockSpec((S, ct), lambda j, bb: (bb, off_v // ct + j))]
    args = [z]
    if glu:
        in_specs.append(pl.BlockSpec((S, ct), lambda j, bb: (bb, off_g // ct + j)))
        args.append(z)
    in_specs += [pl.BlockSpec((K, ct), lambda j, bb: (0, j)), pl.BlockSpec((1, ct), lambda j, bb: (0, j))]
    args += [w, b]
    return pl.pallas_call(body, grid=(C // ct, T // S), in_specs=in_specs,
                          out_specs=pl.BlockSpec((S, ct), lambda j, bb: (bb, j)),
                          out_shape=jax.ShapeDtypeStruct((T, C), F32),
                          scratch_shapes=[pltpu.VMEM((CONV_PAD + S, ct), F32)],
                          compiler_params=_cp("parallel", "parallel"), name=name)(*args)


def _conv_bwd(z, w, dy, *, S, off_v, off_g, name, ct=256):
    T = z.shape[0]
    K, C = w.shape
    KP = -(-K // SUBLANES) * SUBLANES
    ct = min(ct, C)
    ch = min(CONV_CHUNK, S)
    glu = off_g is not None
    n_seq = T // S

    def body(*refs):
        if glu:
            v_ref, g_ref, w_ref, dy_ref, dv_ref, dgo_ref, dw_ref, db_ref, pad_ref, padb_ref = refs
        else:
            v_ref, w_ref, dy_ref, dv_ref, dw_ref, db_ref, pad_ref, padb_ref = refs
        bb = pl.program_id(1)
        pad_ref[0:CONV_PAD, :] = jnp.zeros((CONV_PAD, ct), F32)
        if glu:
            pad_ref[CONV_PAD:CONV_PAD + S, :] = v_ref[...] * _sig(g_ref[...])
        else:
            pad_ref[CONV_PAD:CONV_PAD + S, :] = v_ref[...]
        padb_ref[0:S, :] = dy_ref[...]
        padb_ref[S:S + CONV_PAD, :] = jnp.zeros((CONV_PAD, ct), F32)

        @pl.when(bb == 0)
        def _():
            dw_ref[...] = jnp.zeros((KP, ct), F32)
            db_ref[...] = jnp.zeros((1, ct), F32)

        for l0 in range(0, ct, 128):
            lanes = slice(l0, l0 + 128)
            for c in range(S // ch):
                du = _conv_taps(padb_ref, w_ref, K, c * ch, ch, lanes, True)
                rows = slice(c * ch, (c + 1) * ch)
                if glu:
                    sg = _sig(g_ref[rows, lanes])
                    dv_ref[rows, lanes] = (du * sg).astype(BF16)
                    dgo_ref[rows, lanes] = (du * v_ref[rows, lanes] * sg * (1.0 - sg)).astype(BF16)
                else:
                    dv_ref[rows, lanes] = du.astype(BF16)
            for s in range(K):
                acc = None
                for c in range(S // ch):
                    prod = padb_ref[c * ch:(c + 1) * ch, lanes] * pad_ref[CONV_PAD + c * ch - s:CONV_PAD + c * ch - s + ch, lanes]
                    acc = prod if acc is None else acc + prod
                dw_ref[K - 1 - s:K - s, lanes] += jnp.sum(acc, axis=0, keepdims=True)
            db_ref[:, lanes] += jnp.sum(dy_ref[:, lanes], axis=0, keepdims=True)

    blk = lambda off: pl.BlockSpec((S, ct), lambda j, bb: (bb, off // ct + j))
    in_specs = [blk(off_v)]
    args = [z]
    if glu:
        in_specs.append(blk(off_g))
        args.append(z)
    in_specs += [pl.BlockSpec((K, ct), lambda j, bb: (0, j)), blk(0)]
    args += [w, dy]
    o_bf = jax.ShapeDtypeStruct((T, C), BF16)
    out_shape = [o_bf] + ([o_bf] if glu else []) + [jax.ShapeDtypeStruct((KP, C), F32), jax.ShapeDtypeStruct((1, C), F32)]
    out_specs = [blk(0)] + ([blk(0)] if glu else []) + [pl.BlockSpec((KP, ct), lambda j, bb: (0, j)),
                                                        pl.BlockSpec((1, ct), lambda j, bb: (0, j))]
    del n_seq
    return pl.pallas_call(body, grid=(C // ct, T // S), in_specs=in_specs, out_specs=tuple(out_specs),
                          out_shape=tuple(out_shape),
                          scratch_shapes=[pltpu.VMEM((CONV_PAD + S, ct), F32), pltpu.VMEM((S + CONV_PAD, ct), F32)],
                          compiler_params=_cp("parallel", "arbitrary"), name=name)(*args)


def _softplus_neg(lam):
    return jnp.maximum(-lam, 0.0) + jnp.log1p(jnp.exp(-jnp.abs(lam)))


def _neg_expm1(x):
    u = jnp.exp(x)
    um1 = u - 1.0
    lg = jnp.log(u)
    safe = jnp.where(lg == 0.0, 1.0, lg)
    em1 = jnp.where(um1 == 0.0, x, jnp.where(um1 == -1.0, -1.0, um1 * x / safe))
    return -em1


def _gates_fwd(v, wa, wx, ba, bx, lam, *, S, name, tm=512):
    T, C = v.shape
    G, gw, _ = wa.shape
    tm = min(tm, T)

    def body(v_ref, wa_ref, wx_ref, ba_ref, bx_ref, lam_ref, r_ref, i_ref, a_ref, bt_ref):
        vv = v_ref[...]
        vb = vv.astype(BF16)
        r = _sig(jnp.dot(vb, wa_ref[...], preferred_element_type=F32) + ba_ref[...])
        ig = _sig(jnp.dot(vb, wx_ref[...], preferred_element_type=F32) + bx_ref[...])
        log_a = -LRU_C * r * _softplus_neg(lam_ref[...])
        a = jnp.exp(log_a)
        mult = jnp.sqrt(_neg_expm1(2.0 * log_a))
        row = pl.program_id(0) * tm + lax.broadcasted_iota(jnp.int32, (tm, gw), 0)
        mult = jnp.where(row % S == 0, 1.0, mult)
        r_ref[...] = r
        i_ref[...] = ig
        a_ref[...] = a
        bt_ref[...] = mult * ig * vv

    blk = pl.BlockSpec((tm, gw), lambda i, g: (i, g))
    wsp = pl.BlockSpec((None, gw, gw), lambda i, g: (g, 0, 0))
    vec = pl.BlockSpec((1, gw), lambda i, g: (0, g))
    o = jax.ShapeDtypeStruct((T, C), F32)
    return pl.pallas_call(body, grid=(T // tm, G), in_specs=[blk, wsp, wsp, vec, vec, vec],
                          out_specs=(blk, blk, blk, blk), out_shape=(o, o, o, o),
                          compiler_params=_cp("parallel", "parallel"), name=name)(v, wa, wx, ba, bx, lam)


def _scan_fwd(a, bt, z, *, S, off_gate, name, ct=256):
    T, C = a.shape
    ct = min(ct, C)
    assert off_gate % ct == 0

    def body(a_ref, b_ref, g_ref, h_ref, hb_ref):
        row = lax.broadcasted_iota(jnp.int32, (SUBLANES, ct), 0)

        def step(i, carry):
            st = pl.multiple_of(i * SUBLANES, SUBLANES)
            A = a_ref[pl.ds(st, SUBLANES), :]
            B = b_ref[pl.ds(st, SUBLANES), :]
            for d in (1, 2, 4):
                m = row >= d
                Bn = jnp.where(m, A * pltpu.roll(B, d, 0) + B, B)
                A = jnp.where(m, A * pltpu.roll(A, d, 0), A)
                B = Bn
            h = B + A * carry
            h_ref[pl.ds(st, SUBLANES), :] = h
            hb_ref[pl.ds(st, SUBLANES), :] = (h * _gelu(g_ref[pl.ds(st, SUBLANES), :])).astype(BF16)
            return jnp.broadcast_to(h[SUBLANES - 1:SUBLANES, :], (SUBLANES, ct))

        lax.fori_loop(0, S // SUBLANES, step, jnp.zeros((SUBLANES, ct), F32))

    blk = pl.BlockSpec((S, ct), lambda j, bb: (bb, j))
    return pl.pallas_call(body, grid=(C // ct, T // S),
                          in_specs=[blk, blk, pl.BlockSpec((S, ct), lambda j, bb: (bb, off_gate // ct + j))],
                          out_specs=(blk, blk),
                          out_shape=(jax.ShapeDtypeStruct((T, C), F32), jax.ShapeDtypeStruct((T, C), BF16)),
                          compiler_params=_cp("parallel", "parallel"), name=name)(a, bt, z)


def _scan_bwd(a, h, z, dhb, *, S, off_gate, name, ct=256):
    T, C = a.shape
    ct = min(ct, C)
    n_tiles = S // SUBLANES

    def body(a_ref, h_ref, g_ref, dhb_ref, G_ref, da_ref, dgate_ref):
        row = lax.broadcasted_iota(jnp.int32, (SUBLANES, ct), 0)

        def step(k, qcarry):
            i = n_tiles - 1 - k
            st = pl.multiple_of(i * SUBLANES, SUBLANES)
            stp = pl.multiple_of(jnp.maximum(i - 1, 0) * SUBLANES, SUBLANES)
            A = a_ref[pl.ds(st, SUBLANES), :]
            hv = h_ref[pl.ds(st, SUBLANES), :]
            hprev_tile = h_ref[pl.ds(stp, SUBLANES), :]
            gate = g_ref[pl.ds(st, SUBLANES), :]
            dhb = dhb_ref[pl.ds(st, SUBLANES), :]
            dh = dhb * _gelu(gate)
            dgate_ref[pl.ds(st, SUBLANES), :] = (dhb * hv * _gelu_grad(gate)).astype(BF16)
            Aq = A
            Bq = A * dh
            for d in (1, 2, 4):
                m = row < SUBLANES - d
                Bn = jnp.where(m, Aq * pltpu.roll(Bq, SUBLANES - d, 0) + Bq, Bq)
                Aq = jnp.where(m, Aq * pltpu.roll(Aq, SUBLANES - d, 0), Aq)
                Bq = Bn
            q = Bq + Aq * qcarry
            qnext = jnp.where(row == SUBLANES - 1, qcarry, pltpu.roll(q, SUBLANES - 1, 0))
            g = dh + qnext
            hlast = jnp.where(i > 0, jnp.broadcast_to(hprev_tile[SUBLANES - 1:SUBLANES, :], (SUBLANES, ct)), 0.0)
            hprev = jnp.where(row == 0, hlast, pltpu.roll(hv, 1, 0))
            G_ref[pl.ds(st, SUBLANES), :] = g
            da_ref[pl.ds(st, SUBLANES), :] = g * hprev
            return jnp.broadcast_to(q[0:1, :], (SUBLANES, ct))

        lax.fori_loop(0, n_tiles, step, jnp.zeros((SUBLANES, ct), F32))

    blk = pl.BlockSpec((S, ct), lambda j, bb: (bb, j))
    o = jax.ShapeDtypeStruct((T, C), F32)
    return pl.pallas_call(body, grid=(C // ct, T // S),
                          in_specs=[blk, blk, pl.BlockSpec((S, ct), lambda j, bb: (bb, off_gate // ct + j)), blk],
                          out_specs=(blk, blk, blk), out_shape=(o, o, jax.ShapeDtypeStruct((T, C), BF16)),
                          compiler_params=_cp("parallel", "parallel"), name=name)(a, h, z, dhb)


def _gates_bwd(v, r, ig, Gb, da, lam, *, S, name, tr=256):
    T, C = v.shape
    tr = min(tr, T)

    def body(v_ref, r_ref, i_ref, G_ref, da_ref, lam_ref, dpr_ref, dpi_ref, dv_ref, dsp_ref, dba_ref, dbx_ref):
        vv, r, ig, Gv = v_ref[...], r_ref[...], i_ref[...], G_ref[...]
        sp = _softplus_neg(lam_ref[...])
        log_a = -LRU_C * r * sp
        a = jnp.exp(log_a)
        mult_raw = jnp.sqrt(_neg_expm1(2.0 * log_a))
        row = pl.program_id(0) * tr + lax.broadcasted_iota(jnp.int32, (tr, C), 0)
        start = row % S == 0
        mult = jnp.where(start, 1.0, mult_raw)
        dmult = jnp.where(start, 0.0, Gv * ig * vv)
        di = Gv * mult * vv
        dv_ref[...] = Gv * mult * ig
        dla = da_ref[...] * a - dmult * (a * a) / jnp.where(start, 1.0, mult_raw)
        dr = dla * (-LRU_C) * sp
        dpr = dr * r * (1.0 - r)
        dpi = di * ig * (1.0 - ig)
        dpr_ref[...] = dpr.astype(BF16)
        dpi_ref[...] = dpi.astype(BF16)
        p_sp = jnp.sum(dla * (-LRU_C) * r, axis=0, keepdims=True) * (-_sig(-lam_ref[...]))
        p_a = jnp.sum(dpr, axis=0, keepdims=True)
        p_x = jnp.sum(dpi, axis=0, keepdims=True)

        @pl.when(pl.program_id(0) == 0)
        def _():
            dsp_ref[...] = p_sp
            dba_ref[...] = p_a
            dbx_ref[...] = p_x

        @pl.when(pl.program_id(0) > 0)
        def _():
            dsp_ref[...] += p_sp
            dba_ref[...] += p_a
            dbx_ref[...] += p_x

    rowb = pl.BlockSpec((tr, C), lambda i: (i, 0))
    vec = pl.BlockSpec((1, C), lambda i: (0, 0))
    ob = jax.ShapeDtypeStruct((T, C), BF16)
    ov = jax.ShapeDtypeStruct((1, C), F32)
    return pl.pallas_call(body, grid=(T // tr,), in_specs=[rowb, rowb, rowb, rowb, rowb, vec],
                          out_specs=(rowb, rowb, rowb, vec, vec, vec),
                          out_shape=(ob, ob, jax.ShapeDtypeStruct((T, C), F32), ov, ov, ov),
                          compiler_params=_cp("arbitrary"), name=name)(v, r, ig, Gb, da, lam)


def _gates_dgrad(dpr, dpi, wa, wx, dv_direct, *, name, tm=512):
    T, C = dpr.shape
    G, gw, _ = wa.shape
    tm = min(tm, T)

    def body(dpr_ref, dpi_ref, wa_ref, wx_ref, dvd_ref, dv_ref):
        dn = (((1,), (1,)), ((), ()))
        dv_ref[...] = (dvd_ref[...]
                       + lax.dot_general(dpr_ref[...], wa_ref[...], dn, preferred_element_type=F32)
                       + lax.dot_general(dpi_ref[...], wx_ref[...], dn, preferred_element_type=F32))

    blk = pl.BlockSpec((tm, gw), lambda i, g: (i, g))
    wsp = pl.BlockSpec((None, gw, gw), lambda i, g: (g, 0, 0))
    return pl.pallas_call(body, grid=(T // tm, G), in_specs=[blk, blk, wsp, wsp, blk], out_specs=blk,
                          out_shape=jax.ShapeDtypeStruct((T, C), F32),
                          compiler_params=_cp("parallel", "parallel"), name=name)(dpr, dpi, wa, wx, dv_direct)


def _gates_wgrad(v, dpr, dpi, *, G, name, tk=512):
    T, C = v.shape
    gw = C // G
    tk = min(tk, T)

    def body(v_ref, dpr_ref, dpi_ref, dwa_ref, dwx_ref):
        dn = (((0,), (0,)), ((), ()))
        vb = v_ref[...].astype(BF16)
        pa = lax.dot_general(vb, dpr_ref[...], dn, preferred_element_type=F32)
        px = lax.dot_general(vb, dpi_ref[...], dn, preferred_element_type=F32)

        @pl.when(pl.program_id(1) == 0)
        def _():
            dwa_ref[...] = pa
            dwx_ref[...] = px

        @pl.when(pl.program_id(1) > 0)
        def _():
            dwa_ref[...] += pa
            dwx_ref[...] += px

    blk = pl.BlockSpec((tk, gw), lambda g, k: (k, g))
    wsp = pl.BlockSpec((None, gw, gw), lambda g, k: (g, 0, 0))
    o = jax.ShapeDtypeStruct((G, gw, gw), F32)
    return pl.pallas_call(body, grid=(G, T // tk), in_specs=[blk, blk, blk], out_specs=(wsp, wsp), out_shape=(o, o),
                          compiler_params=_cp("parallel", "arbitrary"), name=name)(v, dpr, dpi)


def _group_weights(w):
    H, hd, _ = w.shape
    G = H // HEADS_PER_GROUP
    eye = jnp.eye(HEADS_PER_GROUP, dtype=w.dtype)
    wg = jnp.einsum("ghij,hk->ghikj", w.reshape(G, HEADS_PER_GROUP, hd, hd), eye)
    return wg.reshape(G, HEADS_PER_GROUP * hd, HEADS_PER_GROUP * hd).astype(BF16)


def _ungroup_grads(dwg, hd):
    G = dwg.shape[0]
    d5 = dwg.reshape(G, HEADS_PER_GROUP, hd, HEADS_PER_GROUP, hd)
    parts = [d5[:, k, :, k, :] for k in range(HEADS_PER_GROUP)]
    return jnp.stack(parts, axis=1).reshape(G * HEADS_PER_GROUP, hd, hd)


def _layer_fwd(x, p, *, S):
    D = x.shape[1]
    Dc = p["conv_a_w"].shape[1]
    Dr = p["conv_b_w"].shape[1]
    offs = dict(va=0, ga=Dc, xb=2 * Dc, gb=2 * Dc + Dr, sa=2 * Dc + 2 * Dr, sb=2 * Dc + 2 * Dr + D)
    h = _rms_fwd(x, p["g_mix"], name="rms_mix_fwd")
    z = _mm_nn(h, p["w_in"], tm=512, tn=p["w_in"].shape[2], bias=p["b_in"], name="mm_in_fwd")
    u1 = _conv_fwd(z, p["conv_a_w"], p["conv_a_b"], S=S, off_v=offs["va"], off_g=offs["ga"], name="conv_a_fwd")
    u2 = _ln_silu_fwd(u1, p["ln_g"], p["ln_b"], name="ln_silu_fwd")
    ya = _mm_nn(u2, p["w_a_out"], tm=512, tn=512, name="mm_a_out_fwd")
    v0 = _conv_fwd(z, p["conv_b_w"], p["conv_b_b"], S=S, off_v=offs["xb"], off_g=None, name="conv_b_fwd")
    r, ig, a, bt = _gates_fwd(v0, p["wg_a"], p["wg_x"], p["b_rg_a"], p["b_rg_x"], p["lam"], S=S, name="gates_fwd")
    hs, hb = _scan_fwd(a, bt, z, S=S, off_gate=offs["gb"], name="scan_fwd")
    yb = _mm_nn(hb, p["w_b_out"], tm=512, tn=512, name="mm_b_out_fwd")
    m = _merge_fwd(z, ya, yb, off_sa=offs["sa"], off_sb=offs["sb"], name="merge_fwd")
    x_mid = _mm_nn(m, p["w_o"], tm=512, tn=512, resid=x, name="mm_o_fwd")
    h2 = _rms_fwd(x_mid, p["g_mlp"], name="rms_mlp_fwd")
    f_pre, f = _mm_nn(h2, p["w_1"], tm=512, tn=p["w_1"].shape[2], relu2=True, name="mm_1_fwd")
    x_next = _mm_nn(f, p["w_2"], tm=512, tn=512, resid=x_mid, name="mm_2_fwd")
    saved = dict(x=x, h=h, z=z, u1=u1, u2=u2, ya=ya, v0=v0, r=r, ig=ig, a=a, hs=hs, hb=hb, yb=yb, m=m,
                 x_mid=x_mid, h2=h2, f_pre=f_pre, f=f, offs=offs)
    return x_next, saved


def _layer_bwd(dx, p, sv, *, S):
    offs = sv["offs"]
    g = {}
    g["w_2"] = _mm_tn(sv["f"], dx, tm=512, tn=1024, tk=512, name="mm_2_wgrad")
    dfp = _mm_nt(dx, p["w_2"], tm=512, tn=512, tk=1024, mul_relu=sv["f_pre"], out_dtype=BF16, name="mm_2_dgrad")
    g["w_1"] = _mm_tn(sv["h2"], dfp, tm=1024, tn=512, tk=512, out_blocks=p["w_1"].shape[0], name="mm_1_wgrad")
    dh2 = _mm_nt(dfp, p["w_1"], tm=512, tn=1024, tk=512, name="mm_1_dgrad")
    dx_mid, g["g_mlp"] = _rms_bwd(sv["x_mid"], p["g_mlp"], dh2, dx, name="rms_mlp_bwd")
    g["w_o"] = _mm_tn(sv["m"], dx_mid, tm=512, tn=1024, tk=512, name="mm_o_wgrad")
    dm = _mm_nt(dx_mid, p["w_o"], tm=512, tn=1024, tk=1024, name="mm_o_dgrad")
    dya, dyb, dsa, dsb = _merge_bwd(sv["z"], sv["ya"], sv["yb"], dm, off_sa=offs["sa"], off_sb=offs["sb"], name="merge_bwd")
    g["w_b_out"] = _mm_tn(sv["hb"], dyb, tm=512, tn=1024, tk=512, name="mm_b_out_wgrad")
    dhb = _mm_nt(dyb, p["w_b_out"], tm=512, tn=512, tk=1024, name="mm_b_out_dgrad")
    Gb, da, dgb = _scan_bwd(sv["a"], sv["hs"], sv["z"], dhb, S=S, off_gate=offs["gb"], name="scan_bwd")
    dpr, dpi, dv_direct, g["lam"], g["b_rg_a"], g["b_rg_x"] = _gates_bwd(sv["v0"], sv["r"], sv["ig"], Gb, da, p["lam"], S=S,
                                                                    name="gates_bwd")
    g["wg_a"], g["wg_x"] = _gates_wgrad(sv["v0"], dpr, dpi, G=p["wg_a"].shape[0], name="gates_wgrad")
    dv0 = _gates_dgrad(dpr, dpi, p["wg_a"], p["wg_x"], dv_direct, name="gates_dgrad")
    dxb, g["conv_b_w"], g["conv_b_b"] = _conv_bwd(sv["z"], p["conv_b_w"], dv0, S=S, off_v=offs["xb"], off_g=None,
                                                  name="conv_b_bwd")
    g["w_a_out"] = _mm_tn(sv["u2"], dya, tm=512, tn=1024, tk=512, name="mm_a_out_wgrad")
    du2 = _mm_nt(dya, p["w_a_out"], tm=512, tn=1024, tk=1024, name="mm_a_out_dgrad")
    du1, g["ln_g"], g["ln_b"] = _ln_silu_bwd(sv["u1"], p["ln_g"], p["ln_b"], du2, name="ln_silu_bwd")
    dva, dga, g["conv_a_w"], g["conv_a_b"] = _conv_bwd(sv["z"], p["conv_a_w"], du1, S=S, off_v=offs["va"],
                                                       off_g=offs["ga"], name="conv_a_bwd")
    dz = jnp.concatenate([dva, dga, dxb, dgb, dsa, dsb], axis=1)
    g["w_in"], db_in = _mm_tn(sv["h"], dz, tm=1024, tn=512, tk=512, out_blocks=p["w_in"].shape[0], colsum=True,
                              name="mm_in_wgrad")
    g["b_in"] = db_in[:1]
    dh = _mm_nt(dz, p["w_in"], tm=512, tn=1024, tk=512, name="mm_in_dgrad")
    dx_in, g["g_mix"] = _rms_bwd(sv["x"], p["g_mix"], dh, dx_mid, name="rms_mix_bwd")
    return dx_in, g


def _local_step(x, tgt, layers, g_final, *, S):
    saved = []
    for p in layers:
        x, sv = _layer_fwd(x, p, S=S)
        saved.append(sv)
    loss, dx, dg_final = _final_loss(x, g_final, tgt, name="final_loss")
    grads = [None] * len(layers)
    for l in reversed(range(len(layers))):
        dx, grads[l] = _layer_bwd(dx, layers[l], saved[l], S=S)
    return loss, dx, grads, dg_final


_HBM = pl.BlockSpec(memory_space=pltpu.HBM)
_MESH = pl.DeviceIdType.MESH


def _all_gather(shards, *, name):
    n = len(shards)

    def body(*refs):
        ins, outs = refs[:n], refs[n:2 * n]
        send_sems, recv_sems, local_sems = refs[2 * n:]
        x, y, c = lax.axis_index("x"), lax.axis_index("y"), lax.axis_index("c")
        me, sibling = (x, y, c), (x, y, 1 - c)
        chips = [(1 - x, y), (x, 1 - y), (1 - x, 1 - y)]

        def copy(k, a, block, to, from_input=False):
            dst = outs[a].at[4 * block[0] + 2 * block[1] + block[2]]
            return pltpu.make_async_remote_copy(src_ref=ins[a] if from_input else dst, dst_ref=dst,
                                                send_sem=send_sems.at[k, a], recv_sem=recv_sems.at[k, a],
                                                device_id=to, device_id_type=_MESH)

        mine = [pltpu.make_async_copy(ins[a], outs[a].at[4 * x + 2 * y + c], local_sems.at[a]) for a in range(n)]
        for cp in mine:
            cp.start()
        first = [copy(0, a, me, sibling, True) for a in range(n)]
        first += [copy(1 + j, a, me, (*chip, c), True) for j, chip in enumerate(chips) for a in range(n)]
        for cp in first:
            cp.start()
        passed = []
        for j, chip in enumerate(chips):
            for a in range(n):
                copy(1 + j, a, (*chip, c), me).wait_recv()
                fwd = copy(4 + j, a, (*chip, c), sibling)
                fwd.start()
                passed.append(fwd)
        for a in range(n):
            copy(0, a, sibling, me).wait_recv()
        for j, chip in enumerate(chips):
            for a in range(n):
                copy(4 + j, a, (*chip, 1 - c), me).wait_recv()
        for cp in first + passed:
            cp.wait_send()
        for cp in mine:
            cp.wait()

    return pl.pallas_call(
        body, in_specs=[_HBM] * n, out_specs=[_HBM] * n,
        out_shape=[jax.ShapeDtypeStruct((N_DEV,) + s.shape, s.dtype) for s in shards],
        scratch_shapes=[pltpu.SemaphoreType.DMA((7, n)), pltpu.SemaphoreType.DMA((7, n)), pltpu.SemaphoreType.DMA((n,))],
        name=name)(*shards)


def _exchange_siblings(grads, *, name):
    n = len(grads)

    def body(*refs):
        ins, outs = refs[:n], refs[n:2 * n]
        send_sems, recv_sems = refs[2 * n:]
        x, y, c = lax.axis_index("x"), lax.axis_index("y"), lax.axis_index("c")
        sibling = (x, y, 1 - c)
        copies = []
        for q in range(4):
            for a in range(n):
                cp = pltpu.make_async_remote_copy(src_ref=ins[a].at[2 * q + 1 - c], dst_ref=outs[a].at[q],
                                                  send_sem=send_sems.at[q, a], recv_sem=recv_sems.at[q, a],
                                                  device_id=sibling, device_id_type=_MESH)
                cp.start()
                copies.append(cp)
        for cp in copies:
            cp.wait()

    return pl.pallas_call(
        body, in_specs=[_HBM] * n, out_specs=[_HBM] * n,
        out_shape=[jax.ShapeDtypeStruct((4,) + g.shape[1:], g.dtype) for g in grads],
        scratch_shapes=[pltpu.SemaphoreType.DMA((4, n)), pltpu.SemaphoreType.DMA((4, n))], name=name)(*grads)


def _add_pair(g, r1, c_idx, *, name):
    _, r, cc = g.shape
    tr = 512 if r % 512 == 0 else r
    g4 = g.reshape(4, 2, r, cc)

    def body(c_ref, g_ref, r_ref, o_ref):
        del c_ref
        o_ref[...] = g_ref[...] + r_ref[...]

    grid_spec = pltpu.PrefetchScalarGridSpec(
        num_scalar_prefetch=1, grid=(4, r // tr),
        in_specs=[pl.BlockSpec((None, None, tr, cc), lambda q, i, c_ref: (q, c_ref[0], i, 0)),
                  pl.BlockSpec((None, tr, cc), lambda q, i, c_ref: (q, i, 0))],
        out_specs=pl.BlockSpec((None, tr, cc), lambda q, i, c_ref: (q, i, 0)))
    return pl.pallas_call(body, grid_spec=grid_spec, out_shape=jax.ShapeDtypeStruct((4, r, cc), F32),
                          compiler_params=_cp("parallel", "parallel"), name=name)(c_idx, g4, r1)


def _exchange_chips(parts, *, name):
    n = len(parts)

    def body(*refs):
        ins, outs = refs[:n], refs[n:2 * n]
        send_sems, recv_sems, local_sems = refs[2 * n:]
        x, y, c = lax.axis_index("x"), lax.axis_index("y"), lax.axis_index("c")
        my_q = 2 * x + y
        chips = [(1 - x, y), (x, 1 - y), (1 - x, 1 - y)]
        mine = [pltpu.make_async_copy(ins[a].at[my_q], outs[a].at[my_q], local_sems.at[a]) for a in range(n)]
        for cp in mine:
            cp.start()
        copies = []
        for j, chip in enumerate(chips):
            for a in range(n):
                cp = pltpu.make_async_remote_copy(src_ref=ins[a].at[2 * chip[0] + chip[1]], dst_ref=outs[a].at[my_q],
                                                  send_sem=send_sems.at[j, a], recv_sem=recv_sems.at[j, a],
                                                  device_id=(*chip, c), device_id_type=_MESH)
                cp.start()
                copies.append((cp, j, a, chip))
        for cp, j, a, chip in copies:
            pltpu.make_async_remote_copy(src_ref=ins[a].at[my_q], dst_ref=outs[a].at[2 * chip[0] + chip[1]],
                                         send_sem=send_sems.at[j, a], recv_sem=recv_sems.at[j, a],
                                         device_id=(*chip, c), device_id_type=_MESH).wait_recv()
        for cp, _, _, _ in copies:
            cp.wait_send()
        for cp in mine:
            cp.wait()

    return pl.pallas_call(
        body, in_specs=[_HBM] * n, out_specs=[_HBM] * n,
        out_shape=[jax.ShapeDtypeStruct(p.shape, p.dtype) for p in parts],
        scratch_shapes=[pltpu.SemaphoreType.DMA((3, n)), pltpu.SemaphoreType.DMA((3, n)), pltpu.SemaphoreType.DMA((n,))],
        name=name)(*parts)


def _reduce_scatter(grads, c_idx, *, tag):
    r1 = _exchange_siblings(grads, name="rs_siblings_" + tag)
    parts = [_add_pair(g, r, c_idx, name="rs_add_" + tag) for g, r in zip(grads, r1)]
    return _exchange_chips(parts, name="rs_chips_" + tag)


def _adamw_math(w, g, m, v):
    m = ADAM_B1 * m + (1.0 - ADAM_B1) * g
    v = ADAM_B2 * v + (1.0 - ADAM_B2) * (g * g)
    m_hat = m / (1.0 - ADAM_B1 ** ADAM_STEP)
    v_hat = v / (1.0 - ADAM_B2 ** ADAM_STEP)
    delta = -ADAM_LR * (m_hat / (jnp.sqrt(v_hat) + ADAM_EPS) + ADAM_WD * w)
    return delta, m, v


def _adamw(w, parts, m, v, *, name):
    r, cc = w.shape
    P = parts.shape[0]
    tr = 512 if r % 512 == 0 else r

    def body(w_ref, p_ref, m_ref, v_ref, g_ref, d_ref, nm_ref, nv_ref):
        g = p_ref[0]
        for q in range(1, P):
            g = g + p_ref[q]
        d, nm, nv = _adamw_math(w_ref[...], g, m_ref[...], v_ref[...])
        g_ref[...] = g
        d_ref[...] = d
        nm_ref[...] = nm
        nv_ref[...] = nv

    blk = pl.BlockSpec((tr, cc), lambda i: (i, 0))
    o = jax.ShapeDtypeStruct((r, cc), F32)
    return pl.pallas_call(body, grid=(r // tr,),
                          in_specs=[blk, pl.BlockSpec((P, tr, cc), lambda i: (0, i, 0)), blk, blk],
                          out_specs=(blk, blk, blk, blk), out_shape=(o, o, o, o),
                          compiler_params=_cp("parallel"), name=name)(w, parts, m, v)


_SHARDED = ("w_in", "conv_a_w", "w_a_out", "conv_b_w", "w_b_out", "w_o", "w_1", "w_2")
_COL_SHARDED = ("w_in", "conv_a_w", "conv_b_w", "w_1")
_REPLICATED = ("g_mix", "b_in", "conv_a_b", "ln_g", "ln_b", "conv_b_b", "w_rg_a", "b_rg_a", "w_rg_x", "b_rg_x", "lam",
               "g_mlp")
_WEIGHTS = ("g_mix", "w_in", "b_in", "conv_a_w", "conv_a_b", "ln_g", "ln_b", "w_a_out", "conv_b_w", "conv_b_b", "w_rg_a",
            "b_rg_a", "w_rg_x", "b_rg_x", "lam", "w_b_out", "w_o", "g_mlp", "w_1", "w_2", "g_final")
_LANES = 128


def _cols_from_blocks(b):
    nb, K, n = b.shape
    return b.transpose(1, 0, 2).reshape(K, nb * n)


def _blocks_from_cols(w, K):
    n = w.shape[1] // N_DEV
    return w[:K].reshape(K, N_DEV, n).transpose(1, 0, 2)


def kernel(x, g_mix, w_in, b_in, conv_a_w, conv_a_b, ln_g, ln_b, w_a_out, conv_b_w, conv_b_b, w_rg_a, b_rg_a, w_rg_x, b_rg_x, lam, w_b_out, w_o, g_mlp, w_1, w_2, g_final, loss_target, m_g_mix, m_w_in, m_b_in, m_conv_a_w, m_conv_a_b, m_ln_g, m_ln_b, m_w_a_out, m_conv_b_w, m_conv_b_b, m_w_rg_a, m_b_rg_a, m_w_rg_x, m_b_rg_x, m_lam, m_w_b_out, m_w_o, m_g_mlp, m_w_1, m_w_2, m_g_final, v_g_mix, v_w_in, v_b_in, v_conv_a_w, v_conv_a_b, v_ln_g, v_ln_b, v_w_a_out, v_conv_b_w, v_conv_b_b, v_w_rg_a, v_b_rg_a, v_w_rg_x, v_b_rg_x, v_lam, v_w_b_out, v_w_o, v_g_mlp, v_w_1, v_w_2, v_g_final):
    W = dict(g_mix=g_mix, w_in=w_in, b_in=b_in, conv_a_w=conv_a_w, conv_a_b=conv_a_b, ln_g=ln_g, ln_b=ln_b,
             w_a_out=w_a_out, conv_b_w=conv_b_w, conv_b_b=conv_b_b, w_rg_a=w_rg_a, b_rg_a=b_rg_a, w_rg_x=w_rg_x,
             b_rg_x=b_rg_x, lam=lam, w_b_out=w_b_out, w_o=w_o, g_mlp=g_mlp, w_1=w_1, w_2=w_2, g_final=g_final)
    M = dict(g_mix=m_g_mix, w_in=m_w_in, b_in=m_b_in, conv_a_w=m_conv_a_w, conv_a_b=m_conv_a_b, ln_g=m_ln_g, ln_b=m_ln_b,
             w_a_out=m_w_a_out, conv_b_w=m_conv_b_w, conv_b_b=m_conv_b_b, w_rg_a=m_w_rg_a, b_rg_a=m_b_rg_a,
             w_rg_x=m_w_rg_x, b_rg_x=m_b_rg_x, lam=m_lam, w_b_out=m_w_b_out, w_o=m_w_o, g_mlp=m_g_mlp, w_1=m_w_1,
             w_2=m_w_2, g_final=m_g_final)
    V = dict(g_mix=v_g_mix, w_in=v_w_in, b_in=v_b_in, conv_a_w=v_conv_a_w, conv_a_b=v_conv_a_b, ln_g=v_ln_g, ln_b=v_ln_b,
             w_a_out=v_w_a_out, conv_b_w=v_conv_b_w, conv_b_b=v_conv_b_b, w_rg_a=v_w_rg_a, b_rg_a=v_b_rg_a,
             w_rg_x=v_w_rg_x, b_rg_x=v_b_rg_x, lam=v_lam, w_b_out=v_w_b_out, w_o=v_w_o, g_mlp=v_g_mlp, w_1=v_w_1,
             w_2=v_w_2, g_final=v_g_final)
    NB, S, D = x.shape
    L = g_mix.shape[0]
    hd = w_rg_a.shape[-1]
    c_idx = lax.axis_index("c").astype(jnp.int32).reshape(1)

    layers = []
    for l in range(L):
        shards = [W[k][l].astype(BF16) if not k.startswith("conv") else W[k][l] for k in _SHARDED]
        full = dict(zip(_SHARDED, _all_gather(shards, name="all_gather_weights")))
        p = dict(w_in=full["w_in"], w_1=full["w_1"],
                 conv_a_w=_cols_from_blocks(full["conv_a_w"]), conv_b_w=_cols_from_blocks(full["conv_b_w"]))
        for k in ("w_a_out", "w_b_out", "w_o", "w_2"):
            p[k] = full[k].reshape(-1, full[k].shape[-1])
        for k in ("g_mix", "b_in", "conv_a_b", "ln_g", "ln_b", "conv_b_b", "b_rg_a", "b_rg_x", "lam", "g_mlp"):
            p[k] = W[k][l][None]
        p["wg_a"] = _group_weights(w_rg_a[l])
        p["wg_x"] = _group_weights(w_rg_x[l])
        layers.append(p)

    loss, dx, grads, dg_final = _local_step(x.reshape(NB * S, D), loss_target.reshape(NB * S, D), layers, g_final[None], S=S)

    out_g, out_d, out_m, out_v = {}, {}, {}, {}
    per_layer = {k: [] for k in _SHARDED}
    for l in range(L):
        g = grads[l]
        blocks = []
        for k in _SHARDED:
            shard_shape = W[k].shape[1:]
            if k in ("w_in", "w_1"):
                blocks.append(g[k])
            elif k in _COL_SHARDED:
                blocks.append(_blocks_from_cols(g[k], shard_shape[0]))
            else:
                blocks.append(g[k].reshape((N_DEV,) + shard_shape))
        parts = _reduce_scatter(blocks, c_idx, tag="grads")
        for k, pk in zip(_SHARDED, parts):
            per_layer[k].append(_adamw(W[k][l], pk, M[k][l], V[k][l], name="adamw_" + k))
    for k in _SHARDED:
        out_g[k], out_d[k], out_m[k], out_v[k] = (jnp.stack([per_layer[k][l][i] for l in range(L)]) for i in range(4))

    def pack(get, with_loss):
        pieces = [get(k).reshape(-1) for k in _REPLICATED] + [get("g_final").reshape(-1), with_loss.reshape(-1)]
        flat = jnp.concatenate(pieces)
        pad = -flat.shape[0] % (512 * _LANES)
        return jnp.pad(flat, (0, pad)).reshape(-1, _LANES)

    def small_grad(k):
        if k == "g_final":
            return dg_final
        if k in ("w_rg_a", "w_rg_x"):
            return jnp.stack([_ungroup_grads(grads[l]["wg_a" if k == "w_rg_a" else "wg_x"], hd) for l in range(L)])
        return jnp.stack([grads[l][k][0] for l in range(L)])

    zero = jnp.zeros((1,), F32)
    g_packed = pack(small_grad, loss[0, :1])
    (g_all,) = _all_gather([g_packed], name="all_gather_small")
    sg, sd, sm, sv = _adamw(pack(W.get, zero), g_all, pack(M.get, zero), pack(V.get, zero), name="adamw_small")
    off = 0
    for k in _REPLICATED + ("g_final",):
        size = W[k].size
        for res, arr in ((out_g, sg), (out_d, sd), (out_m, sm), (out_v, sv)):
            res[k] = arr.reshape(-1)[off:off + size].reshape(W[k].shape)
        off += size
    loss_out = sg.reshape(-1)[off]

    return (loss_out, dx.reshape(NB, S, D), *[out_g[k] for k in _WEIGHTS], *[out_d[k] for k in _WEIGHTS],
            *[out_m[k] for k in _WEIGHTS], *[out_v[k] for k in _WEIGHTS])
```

```python
import functools

import jax
import jax.numpy as jnp
from jax import lax
from jax.experimental import pallas as pl
from jax.experimental.pallas import tpu as pltpu

F32 = jnp.float32
BF16 = jnp.bfloat16

EPS = 1e-6
LRU_C = 8.0
N_RNN_HEADS = 16
HEADS_PER_GROUP = 4
N_DEV = 8
ADAM_LR, ADAM_B1, ADAM_B2, ADAM_EPS, ADAM_WD, ADAM_STEP = 0.001, 0.9, 0.999, 1e-08, 0.01, 10

VMEM_LIMIT_BYTES = 48 * 1024 * 1024
CONV_PAD = 32
CONV_CHUNK = 128
SUBLANES = 8


def _cp(*sem):
    return pltpu.CompilerParams(dimension_semantics=sem, vmem_limit_bytes=VMEM_LIMIT_BYTES)


def _sig(x):
    return 1.0 / (1.0 + jnp.exp(-x))


def _gelu(x):
    c = 0.7978845608028654
    return 0.5 * x * (1.0 + jnp.tanh(c * (x + 0.044715 * x * x * x)))


def _gelu_grad(x):
    c = 0.7978845608028654
    th = jnp.tanh(c * (x + 0.044715 * x * x * x))
    return 0.5 * (1.0 + th) + 0.5 * x * (1.0 - th * th) * c * (1.0 + 3.0 * 0.044715 * x * x)


def _mm_nn(a, b, *, tm, tn, name, bias=None, resid=None, relu2=False, out_dtype=F32):
    M, K = a.shape
    blocked = b.ndim == 3
    N = b.shape[0] * b.shape[2] if blocked else b.shape[1]
    tm = min(tm, M)
    tn = min(tn, N)
    if blocked:
        assert tn == b.shape[2]
    n_extra = (bias is not None) + (resid is not None)

    def body(*refs):
        acc = jnp.dot(refs[0][...].astype(BF16), refs[1][...].astype(BF16), preferred_element_type=F32)
        k = 2
        if bias is not None:
            acc = acc + refs[k][...]
            k += 1
        if resid is not None:
            acc = acc + refs[k][...]
            k += 1
        if relu2:
            refs[k][...] = acc
            p = jnp.maximum(acc, 0.0)
            refs[k + 1][...] = (p * p).astype(BF16)
        else:
            refs[k][...] = acc.astype(out_dtype)

    in_specs = [pl.BlockSpec((tm, K), lambda j, i: (i, 0))]
    if blocked:
        in_specs.append(pl.BlockSpec((None, K, tn), lambda j, i: (j, 0, 0)))
    else:
        in_specs.append(pl.BlockSpec((K, tn), lambda j, i: (0, j)))
    args = [a, b]
    if bias is not None:
        in_specs.append(pl.BlockSpec((1, tn), lambda j, i: (0, j)))
        args.append(bias)
    if resid is not None:
        in_specs.append(pl.BlockSpec((tm, tn), lambda j, i: (i, j)))
        args.append(resid)
    o_spec = pl.BlockSpec((tm, tn), lambda j, i: (i, j))
    if relu2:
        out_shape = (jax.ShapeDtypeStruct((M, N), F32), jax.ShapeDtypeStruct((M, N), BF16))
        out_specs = (o_spec, o_spec)
    else:
        out_shape = jax.ShapeDtypeStruct((M, N), out_dtype)
        out_specs = o_spec
    del n_extra
    return pl.pallas_call(body, grid=(N // tn, M // tm), in_specs=in_specs, out_specs=out_specs,
                          out_shape=out_shape, compiler_params=_cp("parallel", "parallel"), name=name)(*args)


def _mm_nt(a, b, *, tm, tn, tk, name, mul_relu=None, resid=None, out_dtype=F32):
    M, N = a.shape
    blocked = b.ndim == 3
    Kout = b.shape[1] if blocked else b.shape[0]
    tm = min(tm, M)
    tn = min(tn, Kout)
    tk = b.shape[2] if blocked else min(tk, N)
    nk = N // tk

    def body(*refs):
        acc_ref = refs[-1]
        kk = pl.program_id(2)
        part = lax.dot_general(refs[0][...].astype(BF16), refs[1][...].astype(BF16),
                               (((1,), (1,)), ((), ())), preferred_element_type=F32)

        @pl.when(kk == 0)
        def _():
            acc_ref[...] = part

        @pl.when(kk > 0)
        def _():
            acc_ref[...] += part

        @pl.when(kk == nk - 1)
        def _():
            acc = acc_ref[...]
            k = 2
            if mul_relu is not None:
                acc = acc * (2.0 * jnp.maximum(refs[k][...], 0.0))
                k += 1
            if resid is not None:
                acc = acc + refs[k][...]
                k += 1
            refs[k][...] = acc.astype(out_dtype)

    in_specs = [pl.BlockSpec((tm, tk), lambda i, j, k: (i, k))]
    if blocked:
        in_specs.append(pl.BlockSpec((None, tn, tk), lambda i, j, k: (k, j, 0)))
    else:
        in_specs.append(pl.BlockSpec((tn, tk), lambda i, j, k: (j, k)))
    args = [a, b]
    for extra in (mul_relu, resid):
        if extra is not None:
            in_specs.append(pl.BlockSpec((tm, tn), lambda i, j, k: (i, j)))
            args.append(extra)
    return pl.pallas_call(body, grid=(M // tm, Kout // tn, nk), in_specs=in_specs,
                          out_specs=pl.BlockSpec((tm, tn), lambda i, j, k: (i, j)),
                          out_shape=jax.ShapeDtypeStruct((M, Kout), out_dtype),
                          scratch_shapes=[pltpu.VMEM((tm, tn), F32)],
                          compiler_params=_cp("parallel", "parallel", "arbitrary"), name=name)(*args)


def _mm_tn(a, b, *, tm, tn, tk, name, out_blocks=None, colsum=False, out_dtype=F32):
    T, M = a.shape
    N = b.shape[1]
    tm = min(tm, M)
    tk = min(tk, T)
    if out_blocks is not None:
        tn = N // out_blocks
        tm = M
    tn = min(tn, N)
    nk = T // tk
    if colsum:
        assert tm == M

    def body(*refs):
        a_ref, b_ref, o_ref, acc_ref = refs[0], refs[1], refs[2], refs[-1]
        kk = pl.program_id(2)
        bv = b_ref[...]
        part = lax.dot_general(a_ref[...].astype(BF16), bv.astype(BF16),
                               (((0,), (0,)), ((), ())), preferred_element_type=F32)

        if colsum:
            csum = jnp.broadcast_to(jnp.sum(bv.astype(F32), axis=0, keepdims=True), (SUBLANES, tn))

        @pl.when(kk == 0)
        def _():
            acc_ref[...] = part
            if colsum:
                refs[3][...] = csum

        @pl.when(kk > 0)
        def _():
            acc_ref[...] += part
            if colsum:
                refs[3][...] += csum

        @pl.when(kk == nk - 1)
        def _():
            o_ref[...] = acc_ref[...].astype(out_dtype)

    in_specs = [pl.BlockSpec((tk, tm), lambda i, j, k: (k, i)), pl.BlockSpec((tk, tn), lambda i, j, k: (k, j))]
    if out_blocks is not None:
        o_shape = jax.ShapeDtypeStruct((out_blocks, M, tn), out_dtype)
        o_spec = pl.BlockSpec((None, M, tn), lambda i, j, k: (j, 0, 0))
    else:
        o_shape = jax.ShapeDtypeStruct((M, N), out_dtype)
        o_spec = pl.BlockSpec((tm, tn), lambda i, j, k: (i, j))
    if colsum:
        out_shape = (o_shape, jax.ShapeDtypeStruct((SUBLANES, N), F32))
        out_specs = (o_spec, pl.BlockSpec((SUBLANES, tn), lambda i, j, k: (0, j)))
    else:
        out_shape, out_specs = o_shape, o_spec
    return pl.pallas_call(body, grid=(M // tm, N // tn, nk), in_specs=in_specs, out_specs=out_specs,
                          out_shape=out_shape, scratch_shapes=[pltpu.VMEM((tm, tn), F32)],
                          compiler_params=_cp("parallel", "parallel", "arbitrary"), name=name)(a, b)


def _rms_fwd(x, g, *, name, tr=512):
    T, D = x.shape
    tr = min(tr, T)

    def body(x_ref, g_ref, h_ref):
        xv = x_ref[...]
        r = lax.rsqrt(jnp.mean(xv * xv, axis=-1, keepdims=True) + EPS)
        h_ref[...] = (xv * r * g_ref[...]).astype(BF16)

    return pl.pallas_call(body, grid=(T // tr,),
                          in_specs=[pl.BlockSpec((tr, D), lambda i: (i, 0)), pl.BlockSpec((1, D), lambda i: (0, 0))],
                          out_specs=pl.BlockSpec((tr, D), lambda i: (i, 0)),
                          out_shape=jax.ShapeDtypeStruct((T, D), BF16), compiler_params=_cp("parallel"), name=name)(x, g)


def _rms_bwd(x, g, dh, dres, *, name, tr=512):
    T, D = x.shape
    tr = min(tr, T)

    def body(x_ref, g_ref, dh_ref, dres_ref, dx_ref, dg_ref):
        xv = x_ref[...]
        r = lax.rsqrt(jnp.mean(xv * xv, axis=-1, keepdims=True) + EPS)
        n = xv * r
        dh = dh_ref[...]
        dn = dh * g_ref[...]
        dx_ref[...] = dres_ref[...] + r * (dn - n * jnp.mean(dn * n, axis=-1, keepdims=True))
        part = jnp.sum(dh * n, axis=0, keepdims=True)

        @pl.when(pl.program_id(0) == 0)
        def _():
            dg_ref[...] = part

        @pl.when(pl.program_id(0) > 0)
        def _():
            dg_ref[...] += part

    row = pl.BlockSpec((tr, D), lambda i: (i, 0))
    vec = pl.BlockSpec((1, D), lambda i: (0, 0))
    return pl.pallas_call(body, grid=(T // tr,), in_specs=[row, vec, row, row], out_specs=(row, vec),
                          out_shape=(jax.ShapeDtypeStruct((T, D), F32), jax.ShapeDtypeStruct((1, D), F32)),
                          compiler_params=_cp("arbitrary"), name=name)(x, g, dh, dres)


def _final_loss(x, g, tgt, *, name, tr=512):
    T, D = x.shape
    tr = min(tr, T)

    def body(x_ref, g_ref, t_ref, loss_ref, dx_ref, dg_ref):
        xv = x_ref[...]
        gv = g_ref[...]
        r = lax.rsqrt(jnp.mean(xv * xv, axis=-1, keepdims=True) + EPS)
        n = xv * r
        e = n * gv - t_ref[...]
        lpart = 0.5 * jnp.sum(jnp.mean(e * e, axis=-1, keepdims=True), axis=0, keepdims=True)
        dy = e * (1.0 / D)
        dn = dy * gv
        dx_ref[...] = r * (dn - n * jnp.mean(dn * n, axis=-1, keepdims=True))
        gpart = jnp.sum(dy * n, axis=0, keepdims=True)

        @pl.when(pl.program_id(0) == 0)
        def _():
            dg_ref[...] = gpart
            loss_ref[...] = jnp.broadcast_to(lpart, (1, 128))

        @pl.when(pl.program_id(0) > 0)
        def _():
            dg_ref[...] += gpart
            loss_ref[...] += jnp.broadcast_to(lpart, (1, 128))

    row = pl.BlockSpec((tr, D), lambda i: (i, 0))
    vec = pl.BlockSpec((1, D), lambda i: (0, 0))
    return pl.pallas_call(body, grid=(T // tr,), in_specs=[row, vec, row],
                          out_specs=(pl.BlockSpec((1, 128), lambda i: (0, 0)), row, vec),
                          out_shape=(jax.ShapeDtypeStruct((1, 128), F32), jax.ShapeDtypeStruct((T, D), F32),
                                     jax.ShapeDtypeStruct((1, D), F32)),
                          compiler_params=_cp("arbitrary"), name=name)(x, g, tgt)


def _ln_silu_fwd(u, g, b, *, name, tr=512):
    T, C = u.shape
    tr = min(tr, T)

    def body(u_ref, g_ref, b_ref, o_ref):
        uv = u_ref[...]
        mu = jnp.mean(uv, axis=-1, keepdims=True)
        xc = uv - mu
        r = lax.rsqrt(jnp.mean(xc * xc, axis=-1, keepdims=True) + EPS)
        y = xc * r * g_ref[...] + b_ref[...]
        o_ref[...] = (y * _sig(y)).astype(BF16)

    row = pl.BlockSpec((tr, C), lambda i: (i, 0))
    vec = pl.BlockSpec((1, C), lambda i: (0, 0))
    return pl.pallas_call(body, grid=(T // tr,), in_specs=[row, vec, vec], out_specs=row,
                          out_shape=jax.ShapeDtypeStruct((T, C), BF16), compiler_params=_cp("parallel"),
                          name=name)(u, g, b)


def _ln_silu_bwd(u, g, b, do, *, name, tr=512):
    T, C = u.shape
    tr = min(tr, T)

    def body(u_ref, g_ref, b_ref, do_ref, du_ref, dg_ref, db_ref):
        uv = u_ref[...]
        gv = g_ref[...]
        mu = jnp.mean(uv, axis=-1, keepdims=True)
        xc = uv - mu
        r = lax.rsqrt(jnp.mean(xc * xc, axis=-1, keepdims=True) + EPS)
        n = xc * r
        y = n * gv + b_ref[...]
        s = _sig(y)
        dy = do_ref[...] * (s * (1.0 + y * (1.0 - s)))
        dn = dy * gv
        du_ref[...] = r * (dn - jnp.mean(dn, axis=-1, keepdims=True) - n * jnp.mean(dn * n, axis=-1, keepdims=True))
        gpart = jnp.sum(dy * n, axis=0, keepdims=True)
        bpart = jnp.sum(dy, axis=0, keepdims=True)

        @pl.when(pl.program_id(0) == 0)
        def _():
            dg_ref[...] = gpart
            db_ref[...] = bpart

        @pl.when(pl.program_id(0) > 0)
        def _():
            dg_ref[...] += gpart
            db_ref[...] += bpart

    row = pl.BlockSpec((tr, C), lambda i: (i, 0))
    vec = pl.BlockSpec((1, C), lambda i: (0, 0))
    return pl.pallas_call(body, grid=(T // tr,), in_specs=[row, vec, vec, row], out_specs=(row, vec, vec),
                          out_shape=(jax.ShapeDtypeStruct((T, C), F32), jax.ShapeDtypeStruct((1, C), F32),
                                     jax.ShapeDtypeStruct((1, C), F32)),
                          compiler_params=_cp("arbitrary"), name=name)(u, g, b, do)


def _merge_fwd(z, ya, yb, *, off_sa, off_sb, name, tr=512):
    T, D = ya.shape
    tr = min(tr, T)
    assert off_sa % D == 0 and off_sb % D == 0

    def body(sa_ref, sb_ref, ya_ref, yb_ref, m_ref):
        m_ref[...] = (_sig(sa_ref[...]) * ya_ref[...] + _sig(sb_ref[...]) * yb_ref[...]).astype(BF16)

    row = pl.BlockSpec((tr, D), lambda i: (i, 0))
    return pl.pallas_call(body, grid=(T // tr,),
                          in_specs=[pl.BlockSpec((tr, D), lambda i: (i, off_sa // D)),
                                    pl.BlockSpec((tr, D), lambda i: (i, off_sb // D)), row, row],
                          out_specs=row, out_shape=jax.ShapeDtypeStruct((T, D), BF16),
                          compiler_params=_cp("parallel"), name=name)(z, z, ya, yb)


def _merge_bwd(z, ya, yb, dm, *, off_sa, off_sb, name, tr=512):
    T, D = ya.shape
    tr = min(tr, T)

    def body(sa_ref, sb_ref, ya_ref, yb_ref, dm_ref, dya_ref, dyb_ref, dsa_ref, dsb_ref):
        dm = dm_ref[...]
        ga = _sig(sa_ref[...])
        gb = _sig(sb_ref[...])
        dya_ref[...] = (dm * ga).astype(BF16)
        dyb_ref[...] = (dm * gb).astype(BF16)
        dsa_ref[...] = (dm * ya_ref[...] * ga * (1.0 - ga)).astype(BF16)
        dsb_ref[...] = (dm * yb_ref[...] * gb * (1.0 - gb)).astype(BF16)

    row = pl.BlockSpec((tr, D), lambda i: (i, 0))
    o = jax.ShapeDtypeStruct((T, D), BF16)
    return pl.pallas_call(body, grid=(T // tr,),
                          in_specs=[pl.BlockSpec((tr, D), lambda i: (i, off_sa // D)),
                                    pl.BlockSpec((tr, D), lambda i: (i, off_sb // D)), row, row, row],
                          out_specs=(row, row, row, row), out_shape=(o, o, o, o),
                          compiler_params=_cp("parallel"), name=name)(z, z, ya, yb, dm)


def _conv_taps(pad_ref, w_ref, K, base, rows, lanes, reverse):
    acc = None
    for s in range(K):
        st = base + s if reverse else base - s
        term = w_ref[K - 1 - s:K - s, lanes] * pad_ref[st:st + rows, lanes]
        acc = term if acc is None else acc + term
    return acc


def _conv_fwd(z, w, b, *, S, off_v, off_g, name, ct=256):
    T = z.shape[0]
    K, C = w.shape
    ct = min(ct, C)
    ch = min(CONV_CHUNK, S)
    glu = off_g is not None
    assert off_v % ct == 0 and (not glu or off_g % ct == 0) and K - 1 <= CONV_PAD

    def body(*refs):
        if glu:
            v_ref, g_ref, w_ref, b_ref, o_ref, pad_ref = refs
        else:
            v_ref, w_ref, b_ref, o_ref, pad_ref = refs
        pad_ref[0:CONV_PAD, :] = jnp.zeros((CONV_PAD, ct), F32)
        if glu:
            pad_ref[CONV_PAD:CONV_PAD + S, :] = v_ref[...] * _sig(g_ref[...])
        else:
            pad_ref[CONV_PAD:CONV_PAD + S, :] = v_ref[...]
        for l0 in range(0, ct, 128):
            lanes = slice(l0, l0 + 128)
            for c in range(S // ch):
                acc = _conv_taps(pad_ref, w_ref, K, CONV_PAD + c * ch, ch, lanes, False)
                o_ref[c * ch:(c + 1) * ch, lanes] = acc + b_ref[:, lanes]

    in_specs = [pl.BlockSpec((S, ct), lambda j, bb: (bb, off_v // ct + j))]
    args = [z]
    if glu:
        in_specs.append(pl.BlockSpec((S, ct), lambda j, bb: (bb, off_g // ct + j)))
        args.append(z)
    in_specs += [pl.BlockSpec((K, ct), lambda j, bb: (0, j)), pl.BlockSpec((1, ct), lambda j, bb: (0, j))]
    args += [w, b]
    return pl.pallas_call(body, grid=(C // ct, T // S), in_specs=in_specs,
                          out_specs=pl.BlockSpec((S, ct), lambda j, bb: (bb, j)),
                          out_shape=jax.ShapeDtypeStruct((T, C), F32),
                          scratch_shapes=[pltpu.VMEM((CONV_PAD + S, ct), F32)],
                          compiler_params=_cp("parallel", "parallel"), name=name)(*args)


def _conv_bwd(z, w, dy, *, S, off_v, off_g, name, ct=256):
    T = z.shape[0]
    K, C = w.shape
    KP = -(-K // SUBLANES) * SUBLANES
    ct = min(ct, C)
    ch = min(CONV_CHUNK, S)
    glu = off_g is not None
    n_seq = T // S

    def body(*refs):
        if glu:
            v_ref, g_ref, w_ref, dy_ref, dv_ref, dgo_ref, dw_ref, db_ref, pad_ref, padb_ref = refs
        else:
            v_ref, w_ref, dy_ref, dv_ref, dw_ref, db_ref, pad_ref, padb_ref = refs
        bb = pl.program_id(1)
        pad_ref[0:CONV_PAD, :] = jnp.zeros((CONV_PAD, ct), F32)
        if glu:
            pad_ref[CONV_PAD:CONV_PAD + S, :] = v_ref[...] * _sig(g_ref[...])
        else:
            pad_ref[CONV_PAD:CONV_PAD + S, :] = v_ref[...]
        padb_ref[0:S, :] = dy_ref[...]
        padb_ref[S:S + CONV_PAD, :] = jnp.zeros((CONV_PAD, ct), F32)

        @pl.when(bb == 0)
        def _():
            dw_ref[...] = jnp.zeros((KP, ct), F32)
            db_ref[...] = jnp.zeros((1, ct), F32)

        for l0 in range(0, ct, 128):
            lanes = slice(l0, l0 + 128)
            for c in range(S // ch):
                du = _conv_taps(padb_ref, w_ref, K, c * ch, ch, lanes, True)
                rows = slice(c * ch, (c + 1) * ch)
                if glu:
                    sg = _sig(g_ref[rows, lanes])
                    dv_ref[rows, lanes] = (du * sg).astype(BF16)
                    dgo_ref[rows, lanes] = (du * v_ref[rows, lanes] * sg * (1.0 - sg)).astype(BF16)
                else:
                    dv_ref[rows, lanes] = du.astype(BF16)
            for s in range(K):
                acc = None
                for c in range(S // ch):
                    prod = padb_ref[c * ch:(c + 1) * ch, lanes] * pad_ref[CONV_PAD + c * ch - s:CONV_PAD + c * ch - s + ch, lanes]
                    acc = prod if acc is None else acc + prod
                dw_ref[K - 1 - s:K - s, lanes] += jnp.sum(acc, axis=0, keepdims=True)
            db_ref[:, lanes] += jnp.sum(dy_ref[:, lanes], axis=0, keepdims=True)

    blk = lambda off: pl.BlockSpec((S, ct), lambda j, bb: (bb, off // ct + j))
    in_specs = [blk(off_v)]
    args = [z]
    if glu:
        in_specs.append(blk(off_g))
        args.append(z)
    in_specs += [pl.BlockSpec((K, ct), lambda j, bb: (0, j)), blk(0)]
    args += [w, dy]
    o_bf = jax.ShapeDtypeStruct((T, C), BF16)
    out_shape = [o_bf] + ([o_bf] if glu else []) + [jax.ShapeDtypeStruct((KP, C), F32), jax.ShapeDtypeStruct((1, C), F32)]
    out_specs = [blk(0)] + ([blk(0)] if glu else []) + [pl.BlockSpec((KP, ct), lambda j, bb: (0, j)),
                                                        pl.BlockSpec((1, ct), lambda j, bb: (0, j))]
    del n_seq
    return pl.pallas_call(body, grid=(C // ct, T // S), in_specs=in_specs, out_specs=tuple(out_specs),
                          out_shape=tuple(out_shape),
                          scratch_shapes=[pltpu.VMEM((CONV_PAD + S, ct), F32), pltpu.VMEM((S + CONV_PAD, ct), F32)],
                          compiler_params=_cp("parallel", "arbitrary"), name=name)(*args)


def _softplus_neg(lam):
    return jnp.maximum(-lam, 0.0) + jnp.log1p(jnp.exp(-jnp.abs(lam)))


def _neg_expm1(x):
    u = jnp.exp(x)
    um1 = u - 1.0
    lg = jnp.log(u)
    safe = jnp.where(lg == 0.0, 1.0, lg)
    em1 = jnp.where(um1 == 0.0, x, jnp.where(um1 == -1.0, -1.0, um1 * x / safe))
    return -em1


def _gates_fwd(v, wa, wx, ba, bx, lam, *, S, name, tm=512):
    T, C = v.shape
    G, gw, _ = wa.shape
    tm = min(tm, T)

    def body(v_ref, wa_ref, wx_ref, ba_ref, bx_ref, lam_ref, r_ref, i_ref, a_ref, bt_ref):
        vv = v_ref[...]
        vb = vv.astype(BF16)
        r = _sig(jnp.dot(vb, wa_ref[...], preferred_element_type=F32) + ba_ref[...])
        ig = _sig(jnp.dot(vb, wx_ref[...], preferred_element_type=F32) + bx_ref[...])
        log_a = -LRU_C * r * _softplus_neg(lam_ref[...])
        a = jnp.exp(log_a)
        mult = jnp.sqrt(_neg_expm1(2.0 * log_a))
        row = pl.program_id(0) * tm + lax.broadcasted_iota(jnp.int32, (tm, gw), 0)
        mult = jnp.where(row % S == 0, 1.0, mult)
        r_ref[...] = r
        i_ref[...] = ig
        a_ref[...] = a
        bt_ref[...] = mult * ig * vv

    blk = pl.BlockSpec((tm, gw), lambda i, g: (i, g))
    wsp = pl.BlockSpec((None, gw, gw), lambda i, g: (g, 0, 0))
    vec = pl.BlockSpec((1, gw), lambda i, g: (0, g))
    o = jax.ShapeDtypeStruct((T, C), F32)
    return pl.pallas_call(body, grid=(T // tm, G), in_specs=[blk, wsp, wsp, vec, vec, vec],
                          out_specs=(blk, blk, blk, blk), out_shape=(o, o, o, o),
                          compiler_params=_cp("parallel", "parallel"), name=name)(v, wa, wx, ba, bx, lam)


def _scan_fwd(a, bt, z, *, S, off_gate, name, ct=256):
    T, C = a.shape
    ct = min(ct, C)
    assert off_gate % ct == 0

    def body(a_ref, b_ref, g_ref, h_ref, hb_ref):
        row = lax.broadcasted_iota(jnp.int32, (SUBLANES, ct), 0)

        def step(i, carry):
            st = pl.multiple_of(i * SUBLANES, SUBLANES)
            A = a_ref[pl.ds(st, SUBLANES), :]
            B = b_ref[pl.ds(st, SUBLANES), :]
            for d in (1, 2, 4):
                m = row >= d
                Bn = jnp.where(m, A * pltpu.roll(B, d, 0) + B, B)
                A = jnp.where(m, A * pltpu.roll(A, d, 0), A)
                B = Bn
            h = B + A * carry
            h_ref[pl.ds(st, SUBLANES), :] = h
            hb_ref[pl.ds(st, SUBLANES), :] = (h * _gelu(g_ref[pl.ds(st, SUBLANES), :])).astype(BF16)
            return jnp.broadcast_to(h[SUBLANES - 1:SUBLANES, :], (SUBLANES, ct))

        lax.fori_loop(0, S // SUBLANES, step, jnp.zeros((SUBLANES, ct), F32))

    blk = pl.BlockSpec((S, ct), lambda j, bb: (bb, j))
    return pl.pallas_call(body, grid=(C // ct, T // S),
                          in_specs=[blk, blk, pl.BlockSpec((S, ct), lambda j, bb: (bb, off_gate // ct + j))],
                          out_specs=(blk, blk),
                          out_shape=(jax.ShapeDtypeStruct((T, C), F32), jax.ShapeDtypeStruct((T, C), BF16)),
                          compiler_params=_cp("parallel", "parallel"), name=name)(a, bt, z)


def _scan_bwd(a, h, z, dhb, *, S, off_gate, name, ct=256):
    T, C = a.shape
    ct = min(ct, C)
    n_tiles = S // SUBLANES

    def body(a_ref, h_ref, g_ref, dhb_ref, G_ref, da_ref, dgate_ref):
        row = lax.broadcasted_iota(jnp.int32, (SUBLANES, ct), 0)

        def step(k, qcarry):
            i = n_tiles - 1 - k
            st = pl.multiple_of(i * SUBLANES, SUBLANES)
            stp = pl.multiple_of(jnp.maximum(i - 1, 0) * SUBLANES, SUBLANES)
            A = a_ref[pl.ds(st, SUBLANES), :]
            hv = h_ref[pl.ds(st, SUBLANES), :]
            hprev_tile = h_ref[pl.ds(stp, SUBLANES), :]
            gate = g_ref[pl.ds(st, SUBLANES), :]
            dhb = dhb_ref[pl.ds(st, SUBLANES), :]
            dh = dhb * _gelu(gate)
            dgate_ref[pl.ds(st, SUBLANES), :] = (dhb * hv * _gelu_grad(gate)).astype(BF16)
            Aq = A
            Bq = A * dh
            for d in (1, 2, 4):
                m = row < SUBLANES - d
                Bn = jnp.where(m, Aq * pltpu.roll(Bq, SUBLANES - d, 0) + Bq, Bq)
                Aq = jnp.where(m, Aq * pltpu.roll(Aq, SUBLANES - d, 0), Aq)
                Bq = Bn
            q = Bq + Aq * qcarry
            qnext = jnp.where(row == SUBLANES - 1, qcarry, pltpu.roll(q, SUBLANES - 1, 0))
            g = dh + qnext
            hlast = jnp.where(i > 0, jnp.broadcast_to(hprev_tile[SUBLANES - 1:SUBLANES, :], (SUBLANES, ct)), 0.0)
            hprev = jnp.where(row == 0, hlast, pltpu.roll(hv, 1, 0))
            G_ref[pl.ds(st, SUBLANES), :] = g
            da_ref[pl.ds(st, SUBLANES), :] = g * hprev
            return jnp.broadcast_to(q[0:1, :], (SUBLANES, ct))

        lax.fori_loop(0, n_tiles, step, jnp.zeros((SUBLANES, ct), F32))

    blk = pl.BlockSpec((S, ct), lambda j, bb: (bb, j))
    o = jax.ShapeDtypeStruct((T, C), F32)
    return pl.pallas_call(body, grid=(C // ct, T // S),
                          in_specs=[blk, blk, pl.BlockSpec((S, ct), lambda j, bb: (bb, off_gate // ct + j)), blk],
                          out_specs=(blk, blk, blk), out_shape=(o, o, jax.ShapeDtypeStruct((T, C), BF16)),
                          compiler_params=_cp("parallel", "parallel"), name=name)(a, h, z, dhb)


def _gates_bwd(v, r, ig, Gb, da, lam, *, S, name, tr=256):
    T, C = v.shape
    tr = min(tr, T)

    def body(v_ref, r_ref, i_ref, G_ref, da_ref, lam_ref, dpr_ref, dpi_ref, dv_ref, dsp_ref, dba_ref, dbx_ref):
        vv, r, ig, Gv = v_ref[...], r_ref[...], i_ref[...], G_ref[...]
        sp = _softplus_neg(lam_ref[...])
        log_a = -LRU_C * r * sp
        a = jnp.exp(log_a)
        mult_raw = jnp.sqrt(_neg_expm1(2.0 * log_a))
        row = pl.program_id(0) * tr + lax.broadcasted_iota(jnp.int32, (tr, C), 0)
        start = row % S == 0
        mult = jnp.where(start, 1.0, mult_raw)
        dmult = jnp.where(start, 0.0, Gv * ig * vv)
        di = Gv * mult * vv
        dv_ref[...] = Gv * mult * ig
        dla = da_ref[...] * a - dmult * (a * a) / jnp.where(start, 1.0, mult_raw)
        dr = dla * (-LRU_C) * sp
        dpr = dr * r * (1.0 - r)
        dpi = di * ig * (1.0 - ig)
        dpr_ref[...] = dpr.astype(BF16)
        dpi_ref[...] = dpi.astype(BF16)
        p_sp = jnp.sum(dla * (-LRU_C) * r, axis=0, keepdims=True) * (-_sig(-lam_ref[...]))
        p_a = jnp.sum(dpr, axis=0, keepdims=True)
        p_x = jnp.sum(dpi, axis=0, keepdims=True)

        @pl.when(pl.program_id(0) == 0)
        def _():
            dsp_ref[...] = p_sp
            dba_ref[...] = p_a
            dbx_ref[...] = p_x

        @pl.when(pl.program_id(0) > 0)
        def _():
            dsp_ref[...] += p_sp
            dba_ref[...] += p_a
            dbx_ref[...] += p_x

    rowb = pl.BlockSpec((tr, C), lambda i: (i, 0))
    vec = pl.BlockSpec((1, C), lambda i: (0, 0))
    ob = jax.ShapeDtypeStruct((T, C), BF16)
    ov = jax.ShapeDtypeStruct((1, C), F32)
    return pl.pallas_call(body, grid=(T // tr,), in_specs=[rowb, rowb, rowb, rowb, rowb, vec],
                          out_specs=(rowb, rowb, rowb, vec, vec, vec),
                          out_shape=(ob, ob, jax.ShapeDtypeStruct((T, C), F32), ov, ov, ov),
                          compiler_params=_cp("arbitrary"), name=name)(v, r, ig, Gb, da, lam)


def _gates_dgrad(dpr, dpi, wa, wx, dv_direct, *, name, tm=512):
    T, C = dpr.shape
    G, gw, _ = wa.shape
    tm = min(tm, T)

    def body(dpr_ref, dpi_ref, wa_ref, wx_ref, dvd_ref, dv_ref):
        dn = (((1,), (1,)), ((), ()))
        dv_ref[...] = (dvd_ref[...]
                       + lax.dot_general(dpr_ref[...], wa_ref[...], dn, preferred_element_type=F32)
                       + lax.dot_general(dpi_ref[...], wx_ref[...], dn, preferred_element_type=F32))

    blk = pl.BlockSpec((tm, gw), lambda i, g: (i, g))
    wsp = pl.BlockSpec((None, gw, gw), lambda i, g: (g, 0, 0))
    return pl.pallas_call(body, grid=(T // tm, G), in_specs=[blk, blk, wsp, wsp, blk], out_specs=blk,
                          out_shape=jax.ShapeDtypeStruct((T, C), F32),
                          compiler_params=_cp("parallel", "parallel"), name=name)(dpr, dpi, wa, wx, dv_direct)


def _gates_wgrad(v, dpr, dpi, *, G, name, tk=512):
    T, C = v.shape
    gw = C // G
    tk = min(tk, T)

    def body(v_ref, dpr_ref, dpi_ref, dwa_ref, dwx_ref):
        dn = (((0,), (0,)), ((), ()))
        vb = v_ref[...].astype(BF16)
        pa = lax.dot_general(vb, dpr_ref[...], dn, preferred_element_type=F32)
        px = lax.dot_general(vb, dpi_ref[...], dn, preferred_element_type=F32)

        @pl.when(pl.program_id(1) == 0)
        def _():
            dwa_ref[...] = pa
            dwx_ref[...] = px

        @pl.when(pl.program_id(1) > 0)
        def _():
            dwa_ref[...] += pa
            dwx_ref[...] += px

    blk = pl.BlockSpec((tk, gw), lambda g, k: (k, g))
    wsp = pl.BlockSpec((None, gw, gw), lambda g, k: (g, 0, 0))
    o = jax.ShapeDtypeStruct((G, gw, gw), F32)
    return pl.pallas_call(body, grid=(G, T // tk), in_specs=[blk, blk, blk], out_specs=(wsp, wsp), out_shape=(o, o),
                          compiler_params=_cp("parallel", "arbitrary"), name=name)(v, dpr, dpi)


def _group_weights(w):
    H, hd, _ = w.shape
    G = H // HEADS_PER_GROUP
    eye = jnp.eye(HEADS_PER_GROUP, dtype=w.dtype)
    wg = jnp.einsum("ghij,hk->ghikj", w.reshape(G, HEADS_PER_GROUP, hd, hd), eye)
    return wg.reshape(G, HEADS_PER_GROUP * hd, HEADS_PER_GROUP * hd).astype(BF16)


def _ungroup_grads(dwg, hd):
    G = dwg.shape[0]
    d5 = dwg.reshape(G, HEADS_PER_GROUP, hd, HEADS_PER_GROUP, hd)
    parts = [d5[:, k, :, k, :] for k in range(HEADS_PER_GROUP)]
    return jnp.stack(parts, axis=1).reshape(G * HEADS_PER_GROUP, hd, hd)


def _layer_fwd(x, p, *, S):
    D = x.shape[1]
    Dc = p["conv_a_w"].shape[1]
    Dr = p["conv_b_w"].shape[1]
    offs = dict(va=0, ga=Dc, xb=2 * Dc, gb=2 * Dc + Dr, sa=2 * Dc + 2 * Dr, sb=2 * Dc + 2 * Dr + D)
    h = _rms_fwd(x, p["g_mix"], name="rms_mix_fwd")
    z = _mm_nn(h, p["w_in"], tm=512, tn=p["w_in"].shape[2], bias=p["b_in"], name="mm_in_fwd")
    u1 = _conv_fwd(z, p["conv_a_w"], p["conv_a_b"], S=S, off_v=offs["va"], off_g=offs["ga"], name="conv_a_fwd")
    u2 = _ln_silu_fwd(u1, p["ln_g"], p["ln_b"], name="ln_silu_fwd")
    ya = _mm_nn(u2, p["w_a_out"], tm=512, tn=512, name="mm_a_out_fwd")
    v0 = _conv_fwd(z, p["conv_b_w"], p["conv_b_b"], S=S, off_v=offs["xb"], off_g=None, name="conv_b_fwd")
    r, ig, a, bt = _gates_fwd(v0, p["wg_a"], p["wg_x"], p["b_rg_a"], p["b_rg_x"], p["lam"], S=S, name="gates_fwd")
    hs, hb = _scan_fwd(a, bt, z, S=S, off_gate=offs["gb"], name="scan_fwd")
    yb = _mm_nn(hb, p["w_b_out"], tm=512, tn=512, name="mm_b_out_fwd")
    m = _merge_fwd(z, ya, yb, off_sa=offs["sa"], off_sb=offs["sb"], name="merge_fwd")
    x_mid = _mm_nn(m, p["w_o"], tm=512, tn=512, resid=x, name="mm_o_fwd")
    h2 = _rms_fwd(x_mid, p["g_mlp"], name="rms_mlp_fwd")
    f_pre, f = _mm_nn(h2, p["w_1"], tm=512, tn=p["w_1"].shape[2], relu2=True, name="mm_1_fwd")
    x_next = _mm_nn(f, p["w_2"], tm=512, tn=512, resid=x_mid, name="mm_2_fwd")
    saved = dict(x=x, h=h, z=z, u1=u1, u2=u2, ya=ya, v0=v0, r=r, ig=ig, a=a, hs=hs, hb=hb, yb=yb, m=m,
                 x_mid=x_mid, h2=h2, f_pre=f_pre, f=f, offs=offs)
    return x_next, saved


def _layer_bwd_mlp(dx, p, sv, *, wdt):
    g = {}
    g["w_2"] = _mm_tn(sv["f"], dx, tm=512, tn=1024, tk=512, out_dtype=wdt, name="mm_2_wgrad")
    dfp = _mm_nt(dx, p["w_2"], tm=512, tn=512, tk=1024, mul_relu=sv["f_pre"], out_dtype=BF16, name="mm_2_dgrad")
    g["w_1"] = _mm_tn(sv["h2"], dfp, tm=1024, tn=512, tk=512, out_blocks=p["w_1"].shape[0], out_dtype=wdt,
                      name="mm_1_wgrad")
    dh2 = _mm_nt(dfp, p["w_1"], tm=512, tn=1024, tk=512, name="mm_1_dgrad")
    dx_mid, g["g_mlp"] = _rms_bwd(sv["x_mid"], p["g_mlp"], dh2, dx, name="rms_mlp_bwd")
    return dx_mid, g


def _layer_bwd_mix(dx_mid, p, sv, *, S, wdt):
    offs = sv["offs"]
    g = {}
    g["w_o"] = _mm_tn(sv["m"], dx_mid, tm=512, tn=1024, tk=512, out_dtype=wdt, name="mm_o_wgrad")
    dm = _mm_nt(dx_mid, p["w_o"], tm=512, tn=1024, tk=1024, name="mm_o_dgrad")
    dya, dyb, dsa, dsb = _merge_bwd(sv["z"], sv["ya"], sv["yb"], dm, off_sa=offs["sa"], off_sb=offs["sb"], name="merge_bwd")
    g["w_b_out"] = _mm_tn(sv["hb"], dyb, tm=512, tn=1024, tk=512, out_dtype=wdt, name="mm_b_out_wgrad")
    dhb = _mm_nt(dyb, p["w_b_out"], tm=512, tn=512, tk=1024, name="mm_b_out_dgrad")
    Gb, da, dgb = _scan_bwd(sv["a"], sv["hs"], sv["z"], dhb, S=S, off_gate=offs["gb"], name="scan_bwd")
    dpr, dpi, dv_direct, g["lam"], g["b_rg_a"], g["b_rg_x"] = _gates_bwd(sv["v0"], sv["r"], sv["ig"], Gb, da, p["lam"], S=S,
                                                                    name="gates_bwd")
    g["wg_a"], g["wg_x"] = _gates_wgrad(sv["v0"], dpr, dpi, G=p["wg_a"].shape[0], name="gates_wgrad")
    dv0 = _gates_dgrad(dpr, dpi, p["wg_a"], p["wg_x"], dv_direct, name="gates_dgrad")
    dxb, g["conv_b_w"], g["conv_b_b"] = _conv_bwd(sv["z"], p["conv_b_w"], dv0, S=S, off_v=offs["xb"], off_g=None,
                                                  name="conv_b_bwd")
    g["w_a_out"] = _mm_tn(sv["u2"], dya, tm=512, tn=1024, tk=512, out_dtype=wdt, name="mm_a_out_wgrad")
    du2 = _mm_nt(dya, p["w_a_out"], tm=512, tn=1024, tk=1024, name="mm_a_out_dgrad")
    du1, g["ln_g"], g["ln_b"] = _ln_silu_bwd(sv["u1"], p["ln_g"], p["ln_b"], du2, name="ln_silu_bwd")
    dva, dga, g["conv_a_w"], g["conv_a_b"] = _conv_bwd(sv["z"], p["conv_a_w"], du1, S=S, off_v=offs["va"],
                                                       off_g=offs["ga"], name="conv_a_bwd")
    dz = jnp.concatenate([dva, dga, dxb, dgb, dsa, dsb], axis=1)
    g["w_in"], db_in = _mm_tn(sv["h"], dz, tm=1024, tn=512, tk=512, out_blocks=p["w_in"].shape[0], colsum=True,
                              out_dtype=wdt, name="mm_in_wgrad")
    g["b_in"] = db_in[:1]
    dh = _mm_nt(dz, p["w_in"], tm=512, tn=1024, tk=512, name="mm_in_dgrad")
    dx_in, g["g_mix"] = _rms_bwd(sv["x"], p["g_mix"], dh, dx_mid, name="rms_mix_bwd")
    return dx_in, g


def _layer_bwd(dx, p, sv, *, S, wdt=F32):
    dx_mid, g = _layer_bwd_mlp(dx, p, sv, wdt=wdt)
    dx_in, g2 = _layer_bwd_mix(dx_mid, p, sv, S=S, wdt=wdt)
    g.update(g2)
    return dx_in, g


def _local_step(x, tgt, layers, g_final, *, S):
    saved = []
    for p in layers:
        x, sv = _layer_fwd(x, p, S=S)
        saved.append(sv)
    loss, dx, dg_final = _final_loss(x, g_final, tgt, name="final_loss")
    grads = [None] * len(layers)
    for l in reversed(range(len(layers))):
        dx, grads[l] = _layer_bwd(dx, layers[l], saved[l], S=S)
    return loss, dx, grads, dg_final


_HBM = pl.BlockSpec(memory_space=pltpu.HBM)
_MESH = pl.DeviceIdType.MESH


def _all_gather(shards, *, name):
    n = len(shards)

    def body(*refs):
        ins, outs = refs[:n], refs[n:2 * n]
        send_sems, recv_sems, local_sems = refs[2 * n:]
        x, y, c = lax.axis_index("x"), lax.axis_index("y"), lax.axis_index("c")
        me, sibling = (x, y, c), (x, y, 1 - c)
        chips = [(1 - x, y), (x, 1 - y), (1 - x, 1 - y)]

        def copy(k, a, block, to, from_input=False):
            dst = outs[a].at[4 * block[0] + 2 * block[1] + block[2]]
            return pltpu.make_async_remote_copy(src_ref=ins[a] if from_input else dst, dst_ref=dst,
                                                send_sem=send_sems.at[k, a], recv_sem=recv_sems.at[k, a],
                                                device_id=to, device_id_type=_MESH)

        mine = [pltpu.make_async_copy(ins[a], outs[a].at[4 * x + 2 * y + c], local_sems.at[a]) for a in range(n)]
        for cp in mine:
            cp.start()
        first = [copy(0, a, me, sibling, True) for a in range(n)]
        first += [copy(1 + j, a, me, (*chip, c), True) for j, chip in enumerate(chips) for a in range(n)]
        for cp in first:
            cp.start()
        passed = []
        for j, chip in enumerate(chips):
            for a in range(n):
                copy(1 + j, a, (*chip, c), me).wait_recv()
                fwd = copy(4 + j, a, (*chip, c), sibling)
                fwd.start()
                passed.append(fwd)
        for a in range(n):
            copy(0, a, sibling, me).wait_recv()
        for j, chip in enumerate(chips):
            for a in range(n):
                copy(4 + j, a, (*chip, 1 - c), me).wait_recv()
        for cp in first + passed:
            cp.wait_send()
        for cp in mine:
            cp.wait()

    return pl.pallas_call(
        body, in_specs=[_HBM] * n, out_specs=[_HBM] * n,
        out_shape=[jax.ShapeDtypeStruct((N_DEV,) + s.shape, s.dtype) for s in shards],
        scratch_shapes=[pltpu.SemaphoreType.DMA((7, n)), pltpu.SemaphoreType.DMA((7, n)), pltpu.SemaphoreType.DMA((n,))],
        name=name)(*shards)


_SEM = pl.BlockSpec(memory_space=pltpu.SEMAPHORE)
_ANY = pl.BlockSpec(memory_space=pl.ANY)
_FLIPS = [(dx, dy, dc) for dx in (0, 1) for dy in (0, 1) for dc in (0, 1)][1:]


def _place(shard, me_idx, dtype, *, name):
    r, cc = shard.shape
    tr = 512 if r % 512 == 0 else r

    def body(me_ref, s_ref, o_ref):
        del me_ref
        o_ref[...] = s_ref[...].astype(dtype)

    grid_spec = pltpu.PrefetchScalarGridSpec(
        num_scalar_prefetch=1, grid=(r // tr,),
        in_specs=[pl.BlockSpec((tr, cc), lambda i, me: (i, 0))],
        out_specs=pl.BlockSpec((None, tr, cc), lambda i, me: (me[0], i, 0)))
    return pl.pallas_call(body, grid_spec=grid_spec, out_shape=jax.ShapeDtypeStruct((N_DEV, r, cc), dtype),
                          compiler_params=_cp("arbitrary"), name=name)(me_idx, shard)


def _exchange_copies(srcs, lands, send_sems, recv_sems):
    x, y, c = lax.axis_index("x"), lax.axis_index("y"), lax.axis_index("c")
    me = 4 * x + 2 * y + c
    pairs = []
    for k, (dx, dy, dc) in enumerate(_FLIPS):
        peer = (1 - x if dx else x, 1 - y if dy else y, 1 - c if dc else c)
        pidx = 4 * peer[0] + 2 * peer[1] + peer[2]
        for a, land in enumerate(lands):
            src = land.at[me] if srcs is None else srcs[a].at[pidx]
            sem = k * len(lands) + a
            out = pltpu.make_async_remote_copy(src_ref=src, dst_ref=land.at[me], send_sem=send_sems.at[sem],
                                               recv_sem=recv_sems.at[sem], device_id=peer, device_id_type=_MESH)
            arrival = pltpu.make_async_remote_copy(src_ref=src, dst_ref=land.at[pidx], send_sem=send_sems.at[sem],
                                                   recv_sem=recv_sems.at[sem], device_id=peer, device_id_type=_MESH)
            pairs.append((out, arrival))
    return pairs


def _exchange_start(srcs, lands, *, name):
    n = len(lands)
    bufs = list(lands) if srcs is None else list(srcs) + list(lands)
    nb = len(bufs)

    def body(*refs):
        ins = refs[:nb]
        send_sems, recv_sems = refs[nb], refs[nb + 1]
        token = refs[-1]
        for out, _ in _exchange_copies(None if srcs is None else ins[:n], ins[nb - n:], send_sems, recv_sems):
            out.start()
        token[...] = jnp.zeros_like(token)

    sems = pltpu.SemaphoreType.DMA((len(_FLIPS) * n,))
    res = pl.pallas_call(
        body, name=name, in_specs=[_HBM] * nb,
        out_shape=(sems, sems, *[pltpu.HBM(b.shape, b.dtype) for b in bufs], jax.ShapeDtypeStruct((SUBLANES, 128), F32)),
        out_specs=(_SEM, _SEM, *[_HBM] * nb, pl.BlockSpec(memory_space=pltpu.VMEM)),
        input_output_aliases={i: 2 + i for i in range(nb)},
        compiler_params=pltpu.CompilerParams(has_side_effects=pltpu.SideEffectType.DATAFLOW_SIDE_EFFECTING),
    )(*[pltpu.with_memory_space_constraint(b, pltpu.HBM) for b in bufs])
    return res[0], res[1], list(res[2:2 + nb]), res[-1]


def _exchange_wait(send_sems, recv_sems, bufs, *, scatter, after, name):
    nb = len(bufs)
    n = nb // 2 if scatter else nb

    def body(*refs):
        ins = refs[:nb]
        for out, arrival in _exchange_copies(ins[:n] if scatter else None, ins[nb - n:], refs[nb], refs[nb + 1]):
            out.wait_send()
            arrival.wait_recv()

    extra = [] if after is None else [after]
    res = pl.pallas_call(
        body, name=name, in_specs=[_HBM] * nb + [_SEM, _SEM] + [_ANY] * len(extra),
        out_shape=tuple(pltpu.HBM(b.shape, b.dtype) for b in bufs), out_specs=tuple([_HBM] * nb),
        input_output_aliases={i: i for i in range(nb)},
        compiler_params=pltpu.CompilerParams(has_side_effects=pltpu.SideEffectType.DATAFLOW_SIDE_EFFECTING),
    )(*bufs, send_sems, recv_sems, *extra)
    return list(res)


def _adamw_math(w, g, m, v):
    m = ADAM_B1 * m + (1.0 - ADAM_B1) * g
    v = ADAM_B2 * v + (1.0 - ADAM_B2) * (g * g)
    m_hat = m / (1.0 - ADAM_B1 ** ADAM_STEP)
    v_hat = v / (1.0 - ADAM_B2 ** ADAM_STEP)
    delta = -ADAM_LR * (m_hat / (jnp.sqrt(v_hat) + ADAM_EPS) + ADAM_WD * w)
    return delta, m, v


def _adamw(w, parts, m, v, *, name):
    r, cc = w.shape
    P = parts.shape[0]
    tr = 512 if r % 512 == 0 else r

    def body(w_ref, p_ref, m_ref, v_ref, g_ref, d_ref, nm_ref, nv_ref):
        g = p_ref[0]
        for q in range(1, P):
            g = g + p_ref[q]
        d, nm, nv = _adamw_math(w_ref[...], g, m_ref[...], v_ref[...])
        g_ref[...] = g
        d_ref[...] = d
        nm_ref[...] = nm
        nv_ref[...] = nv

    blk = pl.BlockSpec((tr, cc), lambda i: (i, 0))
    o = jax.ShapeDtypeStruct((r, cc), F32)
    return pl.pallas_call(body, grid=(r // tr,),
                          in_specs=[blk, pl.BlockSpec((P, tr, cc), lambda i: (0, i, 0)), blk, blk],
                          out_specs=(blk, blk, blk, blk), out_shape=(o, o, o, o),
                          compiler_params=_cp("parallel"), name=name)(w, parts, m, v)


def _adamw_scattered(w, own, land, m, v, me_idx, *, name):
    r, cc = w.shape
    tr = 512 if r % 512 == 0 else r

    def body(me_ref, w_ref, own_ref, land_ref, m_ref, v_ref, g_ref, d_ref, nm_ref, nv_ref):
        me = me_ref[0]
        g = own_ref[...].astype(F32)
        for q in range(N_DEV):
            g = g + jnp.where(q == me, 0.0, land_ref[q].astype(F32))
        d, nm, nv = _adamw_math(w_ref[...], g, m_ref[...], v_ref[...])
        g_ref[...] = g
        d_ref[...] = d
        nm_ref[...] = nm
        nv_ref[...] = nv

    blk = pl.BlockSpec((tr, cc), lambda i, me: (i, 0))
    o = jax.ShapeDtypeStruct((r, cc), F32)
    grid_spec = pltpu.PrefetchScalarGridSpec(
        num_scalar_prefetch=1, grid=(r // tr,),
        in_specs=[blk, pl.BlockSpec((None, tr, cc), lambda i, me: (me[0], i, 0)),
                  pl.BlockSpec((N_DEV, tr, cc), lambda i, me: (0, i, 0)), blk, blk],
        out_specs=(blk, blk, blk, blk))
    return pl.pallas_call(body, grid_spec=grid_spec, out_shape=(o, o, o, o), compiler_params=_cp("parallel"),
                          name=name)(me_idx, w, own, land, m, v)


_SHARDED = ("w_in", "conv_a_w", "w_a_out", "conv_b_w", "w_b_out", "w_o", "w_1", "w_2")
_COL_SHARDED = ("w_in", "conv_a_w", "conv_b_w", "w_1")
_REPLICATED = ("g_mix", "b_in", "conv_a_b", "ln_g", "ln_b", "conv_b_b", "w_rg_a", "b_rg_a", "w_rg_x", "b_rg_x", "lam",
               "g_mlp")
_WEIGHTS = ("g_mix", "w_in", "b_in", "conv_a_w", "conv_a_b", "ln_g", "ln_b", "w_a_out", "conv_b_w", "conv_b_b", "w_rg_a",
            "b_rg_a", "w_rg_x", "b_rg_x", "lam", "w_b_out", "w_o", "g_mlp", "w_1", "w_2", "g_final")
_LANES = 128


def _cols_from_blocks(b):
    nb, K, n = b.shape
    return b.transpose(1, 0, 2).reshape(K, nb * n)


def _blocks_from_cols(w, K):
    n = w.shape[1] // N_DEV
    return w[:K].reshape(K, N_DEV, n).transpose(1, 0, 2)


def kernel(x, g_mix, w_in, b_in, conv_a_w, conv_a_b, ln_g, ln_b, w_a_out, conv_b_w, conv_b_b, w_rg_a, b_rg_a, w_rg_x, b_rg_x, lam, w_b_out, w_o, g_mlp, w_1, w_2, g_final, loss_target, m_g_mix, m_w_in, m_b_in, m_conv_a_w, m_conv_a_b, m_ln_g, m_ln_b, m_w_a_out, m_conv_b_w, m_conv_b_b, m_w_rg_a, m_b_rg_a, m_w_rg_x, m_b_rg_x, m_lam, m_w_b_out, m_w_o, m_g_mlp, m_w_1, m_w_2, m_g_final, v_g_mix, v_w_in, v_b_in, v_conv_a_w, v_conv_a_b, v_ln_g, v_ln_b, v_w_a_out, v_conv_b_w, v_conv_b_b, v_w_rg_a, v_b_rg_a, v_w_rg_x, v_b_rg_x, v_lam, v_w_b_out, v_w_o, v_g_mlp, v_w_1, v_w_2, v_g_final):
    W = dict(g_mix=g_mix, w_in=w_in, b_in=b_in, conv_a_w=conv_a_w, conv_a_b=conv_a_b, ln_g=ln_g, ln_b=ln_b,
             w_a_out=w_a_out, conv_b_w=conv_b_w, conv_b_b=conv_b_b, w_rg_a=w_rg_a, b_rg_a=b_rg_a, w_rg_x=w_rg_x,
             b_rg_x=b_rg_x, lam=lam, w_b_out=w_b_out, w_o=w_o, g_mlp=g_mlp, w_1=w_1, w_2=w_2, g_final=g_final)
    M = dict(g_mix=m_g_mix, w_in=m_w_in, b_in=m_b_in, conv_a_w=m_conv_a_w, conv_a_b=m_conv_a_b, ln_g=m_ln_g, ln_b=m_ln_b,
             w_a_out=m_w_a_out, conv_b_w=m_conv_b_w, conv_b_b=m_conv_b_b, w_rg_a=m_w_rg_a, b_rg_a=m_b_rg_a,
             w_rg_x=m_w_rg_x, b_rg_x=m_b_rg_x, lam=m_lam, w_b_out=m_w_b_out, w_o=m_w_o, g_mlp=m_g_mlp, w_1=m_w_1,
             w_2=m_w_2, g_final=m_g_final)
    V = dict(g_mix=v_g_mix, w_in=v_w_in, b_in=v_b_in, conv_a_w=v_conv_a_w, conv_a_b=v_conv_a_b, ln_g=v_ln_g, ln_b=v_ln_b,
             w_a_out=v_w_a_out, conv_b_w=v_conv_b_w, conv_b_b=v_conv_b_b, w_rg_a=v_w_rg_a, b_rg_a=v_b_rg_a,
             w_rg_x=v_w_rg_x, b_rg_x=v_b_rg_x, lam=v_lam, w_b_out=v_w_b_out, w_o=v_w_o, g_mlp=v_g_mlp, w_1=v_w_1,
             w_2=v_w_2, g_final=v_g_final)
    NB, S, D = x.shape
    L = g_mix.shape[0]
    hd = w_rg_a.shape[-1]
    me_idx = (4 * lax.axis_index("x") + 2 * lax.axis_index("y") + lax.axis_index("c")).astype(jnp.int32).reshape(1)

    gathers = []
    started = jnp.zeros((), F32)
    for l in range(L):
        lands = [_place(W[k][l], me_idx, F32 if k.startswith("conv") else BF16, name="place_" + k) for k in _SHARDED]
        send_sems, recv_sems, bufs, token = _exchange_start(None, lands, name=f"weights_start_{l}")
        gathers.append((send_sems, recv_sems, bufs))
        started = started + token[0, 0]

    xt = x.reshape(NB * S, D)
    layers, saved = [], []
    for l in range(L):
        send_sems, recv_sems, bufs = gathers[l]
        full = dict(zip(_SHARDED, _exchange_wait(send_sems, recv_sems, bufs, scatter=False, after=None if l == 0 else xt,
                                                 name=f"weights_wait_{l}")))
        p = dict(w_in=full["w_in"], w_1=full["w_1"],
                 conv_a_w=_cols_from_blocks(full["conv_a_w"]), conv_b_w=_cols_from_blocks(full["conv_b_w"]))
        for k in ("w_a_out", "w_b_out", "w_o", "w_2"):
            p[k] = full[k].reshape(-1, full[k].shape[-1])
        for k in ("g_mix", "b_in", "conv_a_b", "ln_g", "ln_b", "conv_b_b", "b_rg_a", "b_rg_x", "lam", "g_mlp"):
            p[k] = W[k][l][None]
        if l == 0:
            p["g_mix"] = p["g_mix"] + started
        p["wg_a"] = _group_weights(w_rg_a[l])
        p["wg_x"] = _group_weights(w_rg_x[l])
        layers.append(p)
        xt, sv = _layer_fwd(xt, p, S=S)
        saved.append(sv)
    loss, dx, dg_final = _final_loss(xt, g_final[None], loss_target.reshape(NB * S, D), name="final_loss")

    per_layer = {k: [None] * L for k in _SHARDED}

    def scatter_start(l, names, g, tag):
        srcs = []
        for k in names:
            shard_shape = W[k].shape[1:]
            if k in ("w_in", "w_1"):
                srcs.append(g[k])
            elif k in _COL_SHARDED:
                srcs.append(_blocks_from_cols(g[k], shard_shape[0]).astype(BF16))
            else:
                srcs.append(g[k].reshape((N_DEV,) + shard_shape))
        lands = [lax.empty(s.shape, BF16) for s in srcs]
        send_sems, recv_sems, bufs, _ = _exchange_start(srcs, lands, name=f"grads_start_{tag}_{l}")
        return names, send_sems, recv_sems, bufs, tag

    def scatter_finish(l, flight, after):
        names, send_sems, recv_sems, bufs, tag = flight
        bufs = _exchange_wait(send_sems, recv_sems, bufs, scatter=True, after=after, name=f"grads_wait_{tag}_{l}")
        n = len(names)
        for k, own, land in zip(names, bufs[:n], bufs[n:]):
            per_layer[k][l] = _adamw_scattered(W[k][l], own, land, M[k][l], V[k][l], me_idx, name="adamw_" + k)

    grads = [None] * L
    in_flight = []
    for l in reversed(range(L)):
        dx_mid, g = _layer_bwd_mlp(dx, layers[l], saved[l], wdt=BF16)
        f_mlp = scatter_start(l, ("w_2", "w_1"), g, "mlp")
        dx, g_mix_part = _layer_bwd_mix(dx_mid, layers[l], saved[l], S=S, wdt=BF16)
        f_mix = scatter_start(l, ("w_o", "w_b_out", "w_a_out", "conv_a_w", "conv_b_w", "w_in"), g_mix_part, "mix")
        g.update(g_mix_part)
        grads[l] = g
        for l_prev, flight in in_flight:
            scatter_finish(l_prev, flight, dx)
        in_flight = [(l, f_mlp), (l, f_mix)]
    for l_prev, flight in in_flight:
        scatter_finish(l_prev, flight, None)
    out_g, out_d, out_m, out_v = {}, {}, {}, {}
    for k in _SHARDED:
        out_g[k], out_d[k], out_m[k], out_v[k] = (jnp.stack([per_layer[k][l][i] for l in range(L)]) for i in range(4))

    def pack(get, with_loss):
        pieces = [get(k).reshape(-1) for k in _REPLICATED] + [get("g_final").reshape(-1), with_loss.reshape(-1)]
        flat = jnp.concatenate(pieces)
        pad = -flat.shape[0] % (512 * _LANES)
        return jnp.pad(flat, (0, pad)).reshape(-1, _LANES)

    def small_grad(k):
        if k == "g_final":
            return dg_final
        if k in ("w_rg_a", "w_rg_x"):
            return jnp.stack([_ungroup_grads(grads[l]["wg_a" if k == "w_rg_a" else "wg_x"], hd) for l in range(L)])
        return jnp.stack([grads[l][k][0] for l in range(L)])

    zero = jnp.zeros((1,), F32)
    g_packed = pack(small_grad, loss[0, :1])
    (g_all,) = _all_gather([g_packed], name="all_gather_small")
    sg, sd, sm, sv = _adamw(pack(W.get, zero), g_all, pack(M.get, zero), pack(V.get, zero), name="adamw_small")
    off = 0
    for k in _REPLICATED + ("g_final",):
        size = W[k].size
        for res, arr in ((out_g, sg), (out_d, sd), (out_m, sm), (out_v, sv)):
            res[k] = arr.reshape(-1)[off:off + size].reshape(W[k].shape)
        off += size
    loss_out = sg.reshape(-1)[off]

    return (loss_out, dx.reshape(NB, S, D), *[out_g[k] for k in _WEIGHTS], *[out_d[k] for k in _WEIGHTS],
            *[out_m[k] for k in _WEIGHTS], *[out_v[k] for k in _WEIGHTS])
```

```python
import functools

import jax
import jax.numpy as jnp
from jax import lax
from jax.experimental import pallas as pl
from jax.experimental.pallas import tpu as pltpu

F32 = jnp.float32
BF16 = jnp.bfloat16

EPS = 1e-6
LRU_C = 8.0
N_RNN_HEADS = 16
HEADS_PER_GROUP = 4
N_DEV = 8
ADAM_LR, ADAM_B1, ADAM_B2, ADAM_EPS, ADAM_WD, ADAM_STEP = 0.001, 0.9, 0.999, 1e-08, 0.01, 10

VMEM_LIMIT_BYTES = 48 * 1024 * 1024
CONV_PAD = 32
CONV_CHUNK = 128
SUBLANES = 8


def _cp(*sem):
    return pltpu.CompilerParams(dimension_semantics=sem, vmem_limit_bytes=VMEM_LIMIT_BYTES)


def _sig(x):
    return 1.0 / (1.0 + jnp.exp(-x))


def _gelu(x):
    c = 0.7978845608028654
    return 0.5 * x * (1.0 + jnp.tanh(c * (x + 0.044715 * x * x * x)))


def _gelu_grad(x):
    c = 0.7978845608028654
    th = jnp.tanh(c * (x + 0.044715 * x * x * x))
    return 0.5 * (1.0 + th) + 0.5 * x * (1.0 - th * th) * c * (1.0 + 3.0 * 0.044715 * x * x)


def _mm_nn(a, b, *, tm, tn, name, bias=None, resid=None, relu2=False, out_dtype=F32):
    M, K = a.shape
    blocked = b.ndim == 3
    N = b.shape[0] * b.shape[2] if blocked else b.shape[1]
    tm = min(tm, M)
    tn = min(tn, N)
    if blocked:
        assert tn == b.shape[2]
    n_extra = (bias is not None) + (resid is not None)

    def body(*refs):
        acc = jnp.dot(refs[0][...].astype(BF16), refs[1][...].astype(BF16), preferred_element_type=F32)
        k = 2
        if bias is not None:
            acc = acc + refs[k][...]
            k += 1
        if resid is not None:
            acc = acc + refs[k][...]
            k += 1
        if relu2:
            refs[k][...] = acc
            p = jnp.maximum(acc, 0.0)
            refs[k + 1][...] = (p * p).astype(BF16)
        else:
            refs[k][...] = acc.astype(out_dtype)

    in_specs = [pl.BlockSpec((tm, K), lambda j, i: (i, 0))]
    if blocked:
        in_specs.append(pl.BlockSpec((None, K, tn), lambda j, i: (j, 0, 0)))
    else:
        in_specs.append(pl.BlockSpec((K, tn), lambda j, i: (0, j)))
    args = [a, b]
    if bias is not None:
        in_specs.append(pl.BlockSpec((1, tn), lambda j, i: (0, j)))
        args.append(bias)
    if resid is not None:
        in_specs.append(pl.BlockSpec((tm, tn), lambda j, i: (i, j)))
        args.append(resid)
    o_spec = pl.BlockSpec((tm, tn), lambda j, i: (i, j))
    if relu2:
        out_shape = (jax.ShapeDtypeStruct((M, N), F32), jax.ShapeDtypeStruct((M, N), BF16))
        out_specs = (o_spec, o_spec)
    else:
        out_shape = jax.ShapeDtypeStruct((M, N), out_dtype)
        out_specs = o_spec
    del n_extra
    return pl.pallas_call(body, grid=(N // tn, M // tm), in_specs=in_specs, out_specs=out_specs,
                          out_shape=out_shape, compiler_params=_cp("parallel", "parallel"), name=name)(*args)


def _mm_nt(a, b, *, tm, tn, tk, name, mul_relu=None, resid=None, out_dtype=F32, dep=None):
    M, N = a.shape
    blocked = b.ndim == 3
    Kout = b.shape[1] if blocked else b.shape[0]
    tm = min(tm, M)
    tn = min(tn, Kout)
    tk = b.shape[2] if blocked else min(tk, N)
    nk = N // tk

    def body(*refs):
        acc_ref = refs[-1]
        kk = pl.program_id(2)
        part = lax.dot_general(refs[0][...].astype(BF16), refs[1][...].astype(BF16),
                               (((1,), (1,)), ((), ())), preferred_element_type=F32)

        @pl.when(kk == 0)
        def _():
            acc_ref[...] = part

        @pl.when(kk > 0)
        def _():
            acc_ref[...] += part

        @pl.when(kk == nk - 1)
        def _():
            acc = acc_ref[...]
            k = 2
            if mul_relu is not None:
                acc = acc * (2.0 * jnp.maximum(refs[k][...], 0.0))
                k += 1
            if resid is not None:
                acc = acc + refs[k][...]
                k += 1
            if dep is not None:
                k += 1
            refs[k][...] = acc.astype(out_dtype)

    in_specs = [pl.BlockSpec((tm, tk), lambda i, j, k: (i, k))]
    if blocked:
        in_specs.append(pl.BlockSpec((None, tn, tk), lambda i, j, k: (k, j, 0)))
    else:
        in_specs.append(pl.BlockSpec((tn, tk), lambda i, j, k: (j, k)))
    args = [a, b]
    for extra in (mul_relu, resid):
        if extra is not None:
            in_specs.append(pl.BlockSpec((tm, tn), lambda i, j, k: (i, j)))
            args.append(extra)
    if dep is not None:
        in_specs.append(pl.BlockSpec(dep.shape, lambda i, j, k: (0, 0)))
        args.append(dep)
    return pl.pallas_call(body, grid=(M // tm, Kout // tn, nk), in_specs=in_specs,
                          out_specs=pl.BlockSpec((tm, tn), lambda i, j, k: (i, j)),
                          out_shape=jax.ShapeDtypeStruct((M, Kout), out_dtype),
                          scratch_shapes=[pltpu.VMEM((tm, tn), F32)],
                          compiler_params=_cp("parallel", "parallel", "arbitrary"), name=name)(*args)


def _mm_tn(a, b, *, tm, tn, tk, name, out_blocks=None, colsum=False, out_dtype=F32):
    T, M = a.shape
    N = b.shape[1]
    tm = min(tm, M)
    tk = min(tk, T)
    if out_blocks is not None:
        tn = N // out_blocks
        tm = M
    tn = min(tn, N)
    nk = T // tk
    if colsum:
        assert tm == M

    def body(*refs):
        a_ref, b_ref, o_ref, acc_ref = refs[0], refs[1], refs[2], refs[-1]
        kk = pl.program_id(2)
        bv = b_ref[...]
        part = lax.dot_general(a_ref[...].astype(BF16), bv.astype(BF16),
                               (((0,), (0,)), ((), ())), preferred_element_type=F32)

        if colsum:
            csum = jnp.broadcast_to(jnp.sum(bv.astype(F32), axis=0, keepdims=True), (SUBLANES, tn))

        @pl.when(kk == 0)
        def _():
            acc_ref[...] = part
            if colsum:
                refs[3][...] = csum

        @pl.when(kk > 0)
        def _():
            acc_ref[...] += part
            if colsum:
                refs[3][...] += csum

        @pl.when(kk == nk - 1)
        def _():
            o_ref[...] = acc_ref[...].astype(out_dtype)

    in_specs = [pl.BlockSpec((tk, tm), lambda i, j, k: (k, i)), pl.BlockSpec((tk, tn), lambda i, j, k: (k, j))]
    if out_blocks is not None:
        o_shape = jax.ShapeDtypeStruct((out_blocks, M, tn), out_dtype)
        o_spec = pl.BlockSpec((None, M, tn), lambda i, j, k: (j, 0, 0))
    else:
        o_shape = jax.ShapeDtypeStruct((M, N), out_dtype)
        o_spec = pl.BlockSpec((tm, tn), lambda i, j, k: (i, j))
    if colsum:
        out_shape = (o_shape, jax.ShapeDtypeStruct((SUBLANES, N), F32))
        out_specs = (o_spec, pl.BlockSpec((SUBLANES, tn), lambda i, j, k: (0, j)))
    else:
        out_shape, out_specs = o_shape, o_spec
    return pl.pallas_call(body, grid=(M // tm, N // tn, nk), in_specs=in_specs, out_specs=out_specs,
                          out_shape=out_shape, scratch_shapes=[pltpu.VMEM((tm, tn), F32)],
                          compiler_params=_cp("parallel", "parallel", "arbitrary"), name=name)(a, b)


def _rms_fwd(x, g, *, name, tr=512):
    T, D = x.shape
    tr = min(tr, T)

    def body(x_ref, g_ref, h_ref):
        xv = x_ref[...]
        r = lax.rsqrt(jnp.mean(xv * xv, axis=-1, keepdims=True) + EPS)
        h_ref[...] = (xv * r * g_ref[...]).astype(BF16)

    return pl.pallas_call(body, grid=(T // tr,),
                          in_specs=[pl.BlockSpec((tr, D), lambda i: (i, 0)), pl.BlockSpec((1, D), lambda i: (0, 0))],
                          out_specs=pl.BlockSpec((tr, D), lambda i: (i, 0)),
                          out_shape=jax.ShapeDtypeStruct((T, D), BF16), compiler_params=_cp("parallel"), name=name)(x, g)


def _rms_bwd(x, g, dh, dres, *, name, tr=512):
    T, D = x.shape
    tr = min(tr, T)

    def body(x_ref, g_ref, dh_ref, dres_ref, dx_ref, dg_ref):
        xv = x_ref[...]
        r = lax.rsqrt(jnp.mean(xv * xv, axis=-1, keepdims=True) + EPS)
        n = xv * r
        dh = dh_ref[...]
        dn = dh * g_ref[...]
        dx_ref[...] = dres_ref[...] + r * (dn - n * jnp.mean(dn * n, axis=-1, keepdims=True))
        part = jnp.sum(dh * n, axis=0, keepdims=True)

        @pl.when(pl.program_id(0) == 0)
        def _():
            dg_ref[...] = part

        @pl.when(pl.program_id(0) > 0)
        def _():
            dg_ref[...] += part

    row = pl.BlockSpec((tr, D), lambda i: (i, 0))
    vec = pl.BlockSpec((1, D), lambda i: (0, 0))
    return pl.pallas_call(body, grid=(T // tr,), in_specs=[row, vec, row, row], out_specs=(row, vec),
                          out_shape=(jax.ShapeDtypeStruct((T, D), F32), jax.ShapeDtypeStruct((1, D), F32)),
                          compiler_params=_cp("arbitrary"), name=name)(x, g, dh, dres)


def _final_loss(x, g, tgt, *, name, tr=512):
    T, D = x.shape
    tr = min(tr, T)

    def body(x_ref, g_ref, t_ref, loss_ref, dx_ref, dg_ref):
        xv = x_ref[...]
        gv = g_ref[...]
        r = lax.rsqrt(jnp.mean(xv * xv, axis=-1, keepdims=True) + EPS)
        n = xv * r
        e = n * gv - t_ref[...]
        lpart = 0.5 * jnp.sum(jnp.mean(e * e, axis=-1, keepdims=True), axis=0, keepdims=True)
        dy = e * (1.0 / D)
        dn = dy * gv
        dx_ref[...] = r * (dn - n * jnp.mean(dn * n, axis=-1, keepdims=True))
        gpart = jnp.sum(dy * n, axis=0, keepdims=True)

        @pl.when(pl.program_id(0) == 0)
        def _():
            dg_ref[...] = gpart
            loss_ref[...] = jnp.broadcast_to(lpart, (1, 128))

        @pl.when(pl.program_id(0) > 0)
        def _():
            dg_ref[...] += gpart
            loss_ref[...] += jnp.broadcast_to(lpart, (1, 128))

    row = pl.BlockSpec((tr, D), lambda i: (i, 0))
    vec = pl.BlockSpec((1, D), lambda i: (0, 0))
    return pl.pallas_call(body, grid=(T // tr,), in_specs=[row, vec, row],
                          out_specs=(pl.BlockSpec((1, 128), lambda i: (0, 0)), row, vec),
                          out_shape=(jax.ShapeDtypeStruct((1, 128), F32), jax.ShapeDtypeStruct((T, D), F32),
                                     jax.ShapeDtypeStruct((1, D), F32)),
                          compiler_params=_cp("arbitrary"), name=name)(x, g, tgt)


def _ln_silu_fwd(u, g, b, *, name, tr=512):
    T, C = u.shape
    tr = min(tr, T)

    def body(u_ref, g_ref, b_ref, o_ref):
        uv = u_ref[...]
        mu = jnp.mean(uv, axis=-1, keepdims=True)
        xc = uv - mu
        r = lax.rsqrt(jnp.mean(xc * xc, axis=-1, keepdims=True) + EPS)
        y = xc * r * g_ref[...] + b_ref[...]
        o_ref[...] = (y * _sig(y)).astype(BF16)

    row = pl.BlockSpec((tr, C), lambda i: (i, 0))
    vec = pl.BlockSpec((1, C), lambda i: (0, 0))
    return pl.pallas_call(body, grid=(T // tr,), in_specs=[row, vec, vec], out_specs=row,
                          out_shape=jax.ShapeDtypeStruct((T, C), BF16), compiler_params=_cp("parallel"),
                          name=name)(u, g, b)


def _ln_silu_bwd(u, g, b, do, *, name, tr=512):
    T, C = u.shape
    tr = min(tr, T)

    def body(u_ref, g_ref, b_ref, do_ref, du_ref, dg_ref, db_ref):
        uv = u_ref[...]
        gv = g_ref[...]
        mu = jnp.mean(uv, axis=-1, keepdims=True)
        xc = uv - mu
        r = lax.rsqrt(jnp.mean(xc * xc, axis=-1, keepdims=True) + EPS)
        n = xc * r
        y = n * gv + b_ref[...]
        s = _sig(y)
        dy = do_ref[...] * (s * (1.0 + y * (1.0 - s)))
        dn = dy * gv
        du_ref[...] = r * (dn - jnp.mean(dn, axis=-1, keepdims=True) - n * jnp.mean(dn * n, axis=-1, keepdims=True))
        gpart = jnp.sum(dy * n, axis=0, keepdims=True)
        bpart = jnp.sum(dy, axis=0, keepdims=True)

        @pl.when(pl.program_id(0) == 0)
        def _():
            dg_ref[...] = gpart
            db_ref[...] = bpart

        @pl.when(pl.program_id(0) > 0)
        def _():
            dg_ref[...] += gpart
            db_ref[...] += bpart

    row = pl.BlockSpec((tr, C), lambda i: (i, 0))
    vec = pl.BlockSpec((1, C), lambda i: (0, 0))
    return pl.pallas_call(body, grid=(T // tr,), in_specs=[row, vec, vec, row], out_specs=(row, vec, vec),
                          out_shape=(jax.ShapeDtypeStruct((T, C), F32), jax.ShapeDtypeStruct((1, C), F32),
                                     jax.ShapeDtypeStruct((1, C), F32)),
                          compiler_params=_cp("arbitrary"), name=name)(u, g, b, do)


def _merge_fwd(z, ya, yb, *, off_sa, off_sb, name, tr=512):
    T, D = ya.shape
    tr = min(tr, T)
    assert off_sa % D == 0 and off_sb % D == 0

    def body(sa_ref, sb_ref, ya_ref, yb_ref, m_ref):
        m_ref[...] = (_sig(sa_ref[...]) * ya_ref[...] + _sig(sb_ref[...]) * yb_ref[...]).astype(BF16)

    row = pl.BlockSpec((tr, D), lambda i: (i, 0))
    return pl.pallas_call(body, grid=(T // tr,),
                          in_specs=[pl.BlockSpec((tr, D), lambda i: (i, off_sa // D)),
                                    pl.BlockSpec((tr, D), lambda i: (i, off_sb // D)), row, row],
                          out_specs=row, out_shape=jax.ShapeDtypeStruct((T, D), BF16),
                          compiler_params=_cp("parallel"), name=name)(z, z, ya, yb)


def _merge_bwd(z, ya, yb, dm, *, off_sa, off_sb, name, tr=512):
    T, D = ya.shape
    tr = min(tr, T)

    def body(sa_ref, sb_ref, ya_ref, yb_ref, dm_ref, dya_ref, dyb_ref, dsa_ref, dsb_ref):
        dm = dm_ref[...]
        ga = _sig(sa_ref[...])
        gb = _sig(sb_ref[...])
        dya_ref[...] = (dm * ga).astype(BF16)
        dyb_ref[...] = (dm * gb).astype(BF16)
        dsa_ref[...] = (dm * ya_ref[...] * ga * (1.0 - ga)).astype(BF16)
        dsb_ref[...] = (dm * yb_ref[...] * gb * (1.0 - gb)).astype(BF16)

    row = pl.BlockSpec((tr, D), lambda i: (i, 0))
    o = jax.ShapeDtypeStruct((T, D), BF16)
    return pl.pallas_call(body, grid=(T // tr,),
                          in_specs=[pl.BlockSpec((tr, D), lambda i: (i, off_sa // D)),
                                    pl.BlockSpec((tr, D), lambda i: (i, off_sb // D)), row, row, row],
                          out_specs=(row, row, row, row), out_shape=(o, o, o, o),
                          compiler_params=_cp("parallel"), name=name)(z, z, ya, yb, dm)


def _conv_taps(pad_ref, w_ref, K, base, rows, lanes, reverse):
    acc = None
    for s in range(K):
        st = base + s if reverse else base - s
        term = w_ref[K - 1 - s:K - s, lanes] * pad_ref[st:st + rows, lanes]
        acc = term if acc is None else acc + term
    return acc


def _conv_fwd(z, w, b, *, S, off_v, off_g, name, ct=256):
    T = z.shape[0]
    K, C = w.shape
    ct = min(ct, C)
    ch = min(CONV_CHUNK, S)
    glu = off_g is not None
    assert off_v % ct == 0 and (not glu or off_g % ct == 0) and K - 1 <= CONV_PAD

    def body(*refs):
        if glu:
            v_ref, g_ref, w_ref, b_ref, o_ref, pad_ref = refs
        else:
            v_ref, w_ref, b_ref, o_ref, pad_ref = refs
        pad_ref[0:CONV_PAD, :] = jnp.zeros((CONV_PAD, ct), F32)
        if glu:
            pad_ref[CONV_PAD:CONV_PAD + S, :] = v_ref[...] * _sig(g_ref[...])
        else:
            pad_ref[CONV_PAD:CONV_PAD + S, :] = v_ref[...]
        for l0 in range(0, ct, 128):
            lanes = slice(l0, l0 + 128)
            for c in range(S // ch):
                acc = _conv_taps(pad_ref, w_ref, K, CONV_PAD + c * ch, ch, lanes, False)
                o_ref[c * ch:(c + 1) * ch, lanes] = acc + b_ref[:, lanes]

    in_specs = [pl.BlockSpec((S, ct), lambda j, bb: (bb, off_v // ct + j))]
    args = [z]
    if glu:
        in_specs.append(pl.BlockSpec((S, ct), lambda j, bb: (bb, off_g // ct + j)))
        args.append(z)
    in_specs += [pl.BlockSpec((K, ct), lambda j, bb: (0, j)), pl.BlockSpec((1, ct), lambda j, bb: (0, j))]
    args += [w, b]
    return pl.pallas_call(body, grid=(C // ct, T // S), in_specs=in_specs,
                          out_specs=pl.BlockSpec((S, ct), lambda j, bb: (bb, j)),
                          out_shape=jax.ShapeDtypeStruct((T, C), F32),
                          scratch_shapes=[pltpu.VMEM((CONV_PAD + S, ct), F32)],
                          compiler_params=_cp("parallel", "parallel"), name=name)(*args)


def _conv_bwd(z, w, dy, *, S, off_v, off_g, name, ct=256):
    T = z.shape[0]
    K, C = w.shape
    KP = -(-K // SUBLANES) * SUBLANES
    ct = min(ct, C)
    ch = min(CONV_CHUNK, S)
    glu = off_g is not None
    n_seq = T // S

    def body(*refs):
        if glu:
            v_ref, g_ref, w_ref, dy_ref, dv_ref, dgo_ref, dw_ref, db_ref, pad_ref, padb_ref = refs
        else:
            v_ref, w_ref, dy_ref, dv_ref, dw_ref, db_ref, pad_ref, padb_ref = refs
        bb = pl.program_id(1)
        pad_ref[0:CONV_PAD, :] = jnp.zeros((CONV_PAD, ct), F32)
        if glu:
            pad_ref[CONV_PAD:CONV_PAD + S, :] = v_ref[...] * _sig(g_ref[...])
        else:
            pad_ref[CONV_PAD:CONV_PAD + S, :] = v_ref[...]
        padb_ref[0:S, :] = dy_ref[...]
        padb_ref[S:S + CONV_PAD, :] = jnp.zeros((CONV_PAD, ct), F32)

        @pl.when(bb == 0)
        def _():
            dw_ref[...] = jnp.zeros((KP, ct), F32)
            db_ref[...] = jnp.zeros((1, ct), F32)

        for l0 in range(0, ct, 128):
            lanes = slice(l0, l0 + 128)
            for c in range(S // ch):
                du = _conv_taps(padb_ref, w_ref, K, c * ch, ch, lanes, True)
                rows = slice(c * ch, (c + 1) * ch)
                if glu:
                    sg = _sig(g_ref[rows, lanes])
                    dv_ref[rows, lanes] = (du * sg).astype(BF16)
                    dgo_ref[rows, lanes] = (du * v_ref[rows, lanes] * sg * (1.0 - sg)).astype(BF16)
                else:
                    dv_ref[rows, lanes] = du.astype(BF16)
            for s in range(K):
                acc = None
                for c in range(S // ch):
                    prod = padb_ref[c * ch:(c + 1) * ch, lanes] * pad_ref[CONV_PAD + c * ch - s:CONV_PAD + c * ch - s + ch, lanes]
                    acc = prod if acc is None else acc + prod
                dw_ref[K - 1 - s:K - s, lanes] += jnp.sum(acc, axis=0, keepdims=True)
            db_ref[:, lanes] += jnp.sum(dy_ref[:, lanes], axis=0, keepdims=True)

    blk = lambda off: pl.BlockSpec((S, ct), lambda j, bb: (bb, off // ct + j))
    in_specs = [blk(off_v)]
    args = [z]
    if glu:
        in_specs.append(blk(off_g))
        args.append(z)
    in_specs += [pl.BlockSpec((K, ct), lambda j, bb: (0, j)), blk(0)]
    args += [w, dy]
    o_bf = jax.ShapeDtypeStruct((T, C), BF16)
    out_shape = [o_bf] + ([o_bf] if glu else []) + [jax.ShapeDtypeStruct((KP, C), F32), jax.ShapeDtypeStruct((1, C), F32)]
    out_specs = [blk(0)] + ([blk(0)] if glu else []) + [pl.BlockSpec((KP, ct), lambda j, bb: (0, j)),
                                                        pl.BlockSpec((1, ct), lambda j, bb: (0, j))]
    del n_seq
    return pl.pallas_call(body, grid=(C // ct, T // S), in_specs=in_specs, out_specs=tuple(out_specs),
                          out_shape=tuple(out_shape),
                          scratch_shapes=[pltpu.VMEM((CONV_PAD + S, ct), F32), pltpu.VMEM((S + CONV_PAD, ct), F32)],
                          compiler_params=_cp("parallel", "arbitrary"), name=name)(*args)


def _softplus_neg(lam):
    return jnp.maximum(-lam, 0.0) + jnp.log1p(jnp.exp(-jnp.abs(lam)))


def _neg_expm1(x):
    u = jnp.exp(x)
    um1 = u - 1.0
    lg = jnp.log(u)
    safe = jnp.where(lg == 0.0, 1.0, lg)
    em1 = jnp.where(um1 == 0.0, x, jnp.where(um1 == -1.0, -1.0, um1 * x / safe))
    return -em1


def _gates_fwd(v, wa, wx, ba, bx, lam, *, S, name, tm=512):
    T, C = v.shape
    G, gw, _ = wa.shape
    tm = min(tm, T)

    def body(v_ref, wa_ref, wx_ref, ba_ref, bx_ref, lam_ref, r_ref, i_ref, a_ref, bt_ref):
        vv = v_ref[...]
        vb = vv.astype(BF16)
        r = _sig(jnp.dot(vb, wa_ref[...], preferred_element_type=F32) + ba_ref[...])
        ig = _sig(jnp.dot(vb, wx_ref[...], preferred_element_type=F32) + bx_ref[...])
        log_a = -LRU_C * r * _softplus_neg(lam_ref[...])
        a = jnp.exp(log_a)
        mult = jnp.sqrt(_neg_expm1(2.0 * log_a))
        row = pl.program_id(0) * tm + lax.broadcasted_iota(jnp.int32, (tm, gw), 0)
        mult = jnp.where(row % S == 0, 1.0, mult)
        r_ref[...] = r
        i_ref[...] = ig
        a_ref[...] = a
        bt_ref[...] = mult * ig * vv

    blk = pl.BlockSpec((tm, gw), lambda i, g: (i, g))
    wsp = pl.BlockSpec((None, gw, gw), lambda i, g: (g, 0, 0))
    vec = pl.BlockSpec((1, gw), lambda i, g: (0, g))
    o = jax.ShapeDtypeStruct((T, C), F32)
    return pl.pallas_call(body, grid=(T // tm, G), in_specs=[blk, wsp, wsp, vec, vec, vec],
                          out_specs=(blk, blk, blk, blk), out_shape=(o, o, o, o),
                          compiler_params=_cp("parallel", "parallel"), name=name)(v, wa, wx, ba, bx, lam)


def _scan_fwd(a, bt, z, *, S, off_gate, name, ct=256):
    T, C = a.shape
    ct = min(ct, C)
    assert off_gate % ct == 0

    def body(a_ref, b_ref, g_ref, h_ref, hb_ref):
        row = lax.broadcasted_iota(jnp.int32, (SUBLANES, ct), 0)

        def step(i, carry):
            st = pl.multiple_of(i * SUBLANES, SUBLANES)
            A = a_ref[pl.ds(st, SUBLANES), :]
            B = b_ref[pl.ds(st, SUBLANES), :]
            for d in (1, 2, 4):
                m = row >= d
                Bn = jnp.where(m, A * pltpu.roll(B, d, 0) + B, B)
                A = jnp.where(m, A * pltpu.roll(A, d, 0), A)
                B = Bn
            h = B + A * carry
            h_ref[pl.ds(st, SUBLANES), :] = h
            hb_ref[pl.ds(st, SUBLANES), :] = (h * _gelu(g_ref[pl.ds(st, SUBLANES), :])).astype(BF16)
            return jnp.broadcast_to(h[SUBLANES - 1:SUBLANES, :], (SUBLANES, ct))

        lax.fori_loop(0, S // SUBLANES, step, jnp.zeros((SUBLANES, ct), F32))

    blk = pl.BlockSpec((S, ct), lambda j, bb: (bb, j))
    return pl.pallas_call(body, grid=(C // ct, T // S),
                          in_specs=[blk, blk, pl.BlockSpec((S, ct), lambda j, bb: (bb, off_gate // ct + j))],
                          out_specs=(blk, blk),
                          out_shape=(jax.ShapeDtypeStruct((T, C), F32), jax.ShapeDtypeStruct((T, C), BF16)),
                          compiler_params=_cp("parallel", "parallel"), name=name)(a, bt, z)


def _scan_bwd(a, h, z, dhb, *, S, off_gate, name, ct=256):
    T, C = a.shape
    ct = min(ct, C)
    n_tiles = S // SUBLANES

    def body(a_ref, h_ref, g_ref, dhb_ref, G_ref, da_ref, dgate_ref):
        row = lax.broadcasted_iota(jnp.int32, (SUBLANES, ct), 0)

        def step(k, qcarry):
            i = n_tiles - 1 - k
            st = pl.multiple_of(i * SUBLANES, SUBLANES)
            stp = pl.multiple_of(jnp.maximum(i - 1, 0) * SUBLANES, SUBLANES)
            A = a_ref[pl.ds(st, SUBLANES), :]
            hv = h_ref[pl.ds(st, SUBLANES), :]
            hprev_tile = h_ref[pl.ds(stp, SUBLANES), :]
            gate = g_ref[pl.ds(st, SUBLANES), :]
            dhb = dhb_ref[pl.ds(st, SUBLANES), :]
            dh = dhb * _gelu(gate)
            dgate_ref[pl.ds(st, SUBLANES), :] = (dhb * hv * _gelu_grad(gate)).astype(BF16)
            Aq = A
            Bq = A * dh
            for d in (1, 2, 4):
                m = row < SUBLANES - d
                Bn = jnp.where(m, Aq * pltpu.roll(Bq, SUBLANES - d, 0) + Bq, Bq)
                Aq = jnp.where(m, Aq * pltpu.roll(Aq, SUBLANES - d, 0), Aq)
                Bq = Bn
            q = Bq + Aq * qcarry
            qnext = jnp.where(row == SUBLANES - 1, qcarry, pltpu.roll(q, SUBLANES - 1, 0))
            g = dh + qnext
            hlast = jnp.where(i > 0, jnp.broadcast_to(hprev_tile[SUBLANES - 1:SUBLANES, :], (SUBLANES, ct)), 0.0)
            hprev = jnp.where(row == 0, hlast, pltpu.roll(hv, 1, 0))
            G_ref[pl.ds(st, SUBLANES), :] = g
            da_ref[pl.ds(st, SUBLANES), :] = g * hprev
            return jnp.broadcast_to(q[0:1, :], (SUBLANES, ct))

        lax.fori_loop(0, n_tiles, step, jnp.zeros((SUBLANES, ct), F32))

    blk = pl.BlockSpec((S, ct), lambda j, bb: (bb, j))
    o = jax.ShapeDtypeStruct((T, C), F32)
    return pl.pallas_call(body, grid=(C // ct, T // S),
                          in_specs=[blk, blk, pl.BlockSpec((S, ct), lambda j, bb: (bb, off_gate // ct + j)), blk],
                          out_specs=(blk, blk, blk), out_shape=(o, o, jax.ShapeDtypeStruct((T, C), BF16)),
                          compiler_params=_cp("parallel", "parallel"), name=name)(a, h, z, dhb)


def _gates_bwd(v, r, ig, Gb, da, lam, *, S, name, tr=256):
    T, C = v.shape
    tr = min(tr, T)

    def body(v_ref, r_ref, i_ref, G_ref, da_ref, lam_ref, dpr_ref, dpi_ref, dv_ref, dsp_ref, dba_ref, dbx_ref):
        vv, r, ig, Gv = v_ref[...], r_ref[...], i_ref[...], G_ref[...]
        sp = _softplus_neg(lam_ref[...])
        log_a = -LRU_C * r * sp
        a = jnp.exp(log_a)
        mult_raw = jnp.sqrt(_neg_expm1(2.0 * log_a))
        row = pl.program_id(0) * tr + lax.broadcasted_iota(jnp.int32, (tr, C), 0)
        start = row % S == 0
        mult = jnp.where(start, 1.0, mult_raw)
        dmult = jnp.where(start, 0.0, Gv * ig * vv)
        di = Gv * mult * vv
        dv_ref[...] = Gv * mult * ig
        dla = da_ref[...] * a - dmult * (a * a) / jnp.where(start, 1.0, mult_raw)
        dr = dla * (-LRU_C) * sp
        dpr = dr * r * (1.0 - r)
        dpi = di * ig * (1.0 - ig)
        dpr_ref[...] = dpr.astype(BF16)
        dpi_ref[...] = dpi.astype(BF16)
        p_sp = jnp.sum(dla * (-LRU_C) * r, axis=0, keepdims=True) * (-_sig(-lam_ref[...]))
        p_a = jnp.sum(dpr, axis=0, keepdims=True)
        p_x = jnp.sum(dpi, axis=0, keepdims=True)

        @pl.when(pl.program_id(0) == 0)
        def _():
            dsp_ref[...] = p_sp
            dba_ref[...] = p_a
            dbx_ref[...] = p_x

        @pl.when(pl.program_id(0) > 0)
        def _():
            dsp_ref[...] += p_sp
            dba_ref[...] += p_a
            dbx_ref[...] += p_x

    rowb = pl.BlockSpec((tr, C), lambda i: (i, 0))
    vec = pl.BlockSpec((1, C), lambda i: (0, 0))
    ob = jax.ShapeDtypeStruct((T, C), BF16)
    ov = jax.ShapeDtypeStruct((1, C), F32)
    return pl.pallas_call(body, grid=(T // tr,), in_specs=[rowb, rowb, rowb, rowb, rowb, vec],
                          out_specs=(rowb, rowb, rowb, vec, vec, vec),
                          out_shape=(ob, ob, jax.ShapeDtypeStruct((T, C), F32), ov, ov, ov),
                          compiler_params=_cp("arbitrary"), name=name)(v, r, ig, Gb, da, lam)


def _gates_dgrad(dpr, dpi, wa, wx, dv_direct, *, name, tm=512):
    T, C = dpr.shape
    G, gw, _ = wa.shape
    tm = min(tm, T)

    def body(dpr_ref, dpi_ref, wa_ref, wx_ref, dvd_ref, dv_ref):
        dn = (((1,), (1,)), ((), ()))
        dv_ref[...] = (dvd_ref[...]
                       + lax.dot_general(dpr_ref[...], wa_ref[...], dn, preferred_element_type=F32)
                       + lax.dot_general(dpi_ref[...], wx_ref[...], dn, preferred_element_type=F32))

    blk = pl.BlockSpec((tm, gw), lambda i, g: (i, g))
    wsp = pl.BlockSpec((None, gw, gw), lambda i, g: (g, 0, 0))
    return pl.pallas_call(body, grid=(T // tm, G), in_specs=[blk, blk, wsp, wsp, blk], out_specs=blk,
                          out_shape=jax.ShapeDtypeStruct((T, C), F32),
                          compiler_params=_cp("parallel", "parallel"), name=name)(dpr, dpi, wa, wx, dv_direct)


def _gates_wgrad(v, dpr, dpi, *, G, name, tk=512):
    T, C = v.shape
    gw = C // G
    tk = min(tk, T)

    def body(v_ref, dpr_ref, dpi_ref, dwa_ref, dwx_ref):
        dn = (((0,), (0,)), ((), ()))
        vb = v_ref[...].astype(BF16)
        pa = lax.dot_general(vb, dpr_ref[...], dn, preferred_element_type=F32)
        px = lax.dot_general(vb, dpi_ref[...], dn, preferred_element_type=F32)

        @pl.when(pl.program_id(1) == 0)
        def _():
            dwa_ref[...] = pa
            dwx_ref[...] = px

        @pl.when(pl.program_id(1) > 0)
        def _():
            dwa_ref[...] += pa
            dwx_ref[...] += px

    blk = pl.BlockSpec((tk, gw), lambda g, k: (k, g))
    wsp = pl.BlockSpec((None, gw, gw), lambda g, k: (g, 0, 0))
    o = jax.ShapeDtypeStruct((G, gw, gw), F32)
    return pl.pallas_call(body, grid=(G, T // tk), in_specs=[blk, blk, blk], out_specs=(wsp, wsp), out_shape=(o, o),
                          compiler_params=_cp("parallel", "arbitrary"), name=name)(v, dpr, dpi)


def _group_weights(w):
    H, hd, _ = w.shape
    G = H // HEADS_PER_GROUP
    eye = jnp.eye(HEADS_PER_GROUP, dtype=w.dtype)
    wg = jnp.einsum("ghij,hk->ghikj", w.reshape(G, HEADS_PER_GROUP, hd, hd), eye)
    return wg.reshape(G, HEADS_PER_GROUP * hd, HEADS_PER_GROUP * hd).astype(BF16)


def _ungroup_grads(dwg, hd):
    G = dwg.shape[0]
    d5 = dwg.reshape(G, HEADS_PER_GROUP, hd, HEADS_PER_GROUP, hd)
    parts = [d5[:, k, :, k, :] for k in range(HEADS_PER_GROUP)]
    return jnp.stack(parts, axis=1).reshape(G * HEADS_PER_GROUP, hd, hd)


def _layer_fwd(x, p, *, S, fetch=None):
    D = x.shape[1]
    Dc = p["conv_a_b"].shape[1]
    Dr = p["conv_b_b"].shape[1]
    offs = dict(va=0, ga=Dc, xb=2 * Dc, gb=2 * Dc + Dr, sa=2 * Dc + 2 * Dr, sb=2 * Dc + 2 * Dr + D)
    h = _rms_fwd(x, p["g_mix"], name="rms_mix_fwd")
    if fetch is not None:
        fetch("in", h)
    z = _mm_nn(h, p["w_in"], tm=512, tn=p["w_in"].shape[2], bias=p["b_in"], name="mm_in_fwd")
    if fetch is not None:
        fetch("mix", z)
    u1 = _conv_fwd(z, p["conv_a_w"], p["conv_a_b"], S=S, off_v=offs["va"], off_g=offs["ga"], name="conv_a_fwd")
    u2 = _ln_silu_fwd(u1, p["ln_g"], p["ln_b"], name="ln_silu_fwd")
    ya = _mm_nn(u2, p["w_a_out"], tm=512, tn=512, name="mm_a_out_fwd")
    v0 = _conv_fwd(z, p["conv_b_w"], p["conv_b_b"], S=S, off_v=offs["xb"], off_g=None, name="conv_b_fwd")
    r, ig, a, bt = _gates_fwd(v0, p["wg_a"], p["wg_x"], p["b_rg_a"], p["b_rg_x"], p["lam"], S=S, name="gates_fwd")
    hs, hb = _scan_fwd(a, bt, z, S=S, off_gate=offs["gb"], name="scan_fwd")
    yb = _mm_nn(hb, p["w_b_out"], tm=512, tn=512, name="mm_b_out_fwd")
    m = _merge_fwd(z, ya, yb, off_sa=offs["sa"], off_sb=offs["sb"], name="merge_fwd")
    x_mid = _mm_nn(m, p["w_o"], tm=512, tn=512, resid=x, name="mm_o_fwd")
    h2 = _rms_fwd(x_mid, p["g_mlp"], name="rms_mlp_fwd")
    if fetch is not None:
        fetch("mlp", h2)
    f_pre, f = _mm_nn(h2, p["w_1"], tm=512, tn=p["w_1"].shape[2], relu2=True, name="mm_1_fwd")
    x_next = _mm_nn(f, p["w_2"], tm=512, tn=512, resid=x_mid, name="mm_2_fwd")
    saved = dict(x=x, h=h, z=z, u1=u1, u2=u2, ya=ya, v0=v0, r=r, ig=ig, a=a, hs=hs, hb=hb, yb=yb, m=m,
                 x_mid=x_mid, h2=h2, f_pre=f_pre, f=f, offs=offs)
    return x_next, saved


def _layer_bwd_mlp(dx, p, sv, *, wdt, dep=None):
    g = {}
    g["w_2"] = _mm_tn(sv["f"], dx, tm=512, tn=1024, tk=512, out_dtype=wdt, name="mm_2_wgrad")
    dfp = _mm_nt(dx, p["w_2"], tm=512, tn=512, tk=1024, mul_relu=sv["f_pre"], out_dtype=BF16, dep=dep,
                 name="mm_2_dgrad")
    g["w_1"] = _mm_tn(sv["h2"], dfp, tm=1024, tn=512, tk=512, out_blocks=p["w_1"].shape[0], out_dtype=wdt,
                      name="mm_1_wgrad")
    dh2 = _mm_nt(dfp, p["w_1"], tm=512, tn=1024, tk=512, name="mm_1_dgrad")
    dx_mid, g["g_mlp"] = _rms_bwd(sv["x_mid"], p["g_mlp"], dh2, dx, name="rms_mlp_bwd")
    return dx_mid, g


def _layer_bwd_mix(dx_mid, p, sv, *, S, wdt, dep=None, on_weight_grads=None):
    offs = sv["offs"]
    g = {}
    g["w_o"] = _mm_tn(sv["m"], dx_mid, tm=512, tn=1024, tk=512, out_dtype=wdt, name="mm_o_wgrad")
    dm = _mm_nt(dx_mid, p["w_o"], tm=512, tn=1024, tk=1024, dep=dep, name="mm_o_dgrad")
    dya, dyb, dsa, dsb = _merge_bwd(sv["z"], sv["ya"], sv["yb"], dm, off_sa=offs["sa"], off_sb=offs["sb"], name="merge_bwd")
    g["w_b_out"] = _mm_tn(sv["hb"], dyb, tm=512, tn=1024, tk=512, out_dtype=wdt, name="mm_b_out_wgrad")
    dhb = _mm_nt(dyb, p["w_b_out"], tm=512, tn=512, tk=1024, name="mm_b_out_dgrad")
    Gb, da, dgb = _scan_bwd(sv["a"], sv["hs"], sv["z"], dhb, S=S, off_gate=offs["gb"], name="scan_bwd")
    dpr, dpi, dv_direct, g["lam"], g["b_rg_a"], g["b_rg_x"] = _gates_bwd(sv["v0"], sv["r"], sv["ig"], Gb, da, p["lam"], S=S,
                                                                    name="gates_bwd")
    g["wg_a"], g["wg_x"] = _gates_wgrad(sv["v0"], dpr, dpi, G=p["wg_a"].shape[0], name="gates_wgrad")
    dv0 = _gates_dgrad(dpr, dpi, p["wg_a"], p["wg_x"], dv_direct, name="gates_dgrad")
    dxb, g["conv_b_w"], g["conv_b_b"] = _conv_bwd(sv["z"], p["conv_b_w"], dv0, S=S, off_v=offs["xb"], off_g=None,
                                                  name="conv_b_bwd")
    g["w_a_out"] = _mm_tn(sv["u2"], dya, tm=512, tn=1024, tk=512, out_dtype=wdt, name="mm_a_out_wgrad")
    du2 = _mm_nt(dya, p["w_a_out"], tm=512, tn=1024, tk=1024, name="mm_a_out_dgrad")
    du1, g["ln_g"], g["ln_b"] = _ln_silu_bwd(sv["u1"], p["ln_g"], p["ln_b"], du2, name="ln_silu_bwd")
    dva, dga, g["conv_a_w"], g["conv_a_b"] = _conv_bwd(sv["z"], p["conv_a_w"], du1, S=S, off_v=offs["va"],
                                                       off_g=offs["ga"], name="conv_a_bwd")
    dz = jnp.concatenate([dva, dga, dxb, dgb, dsa, dsb], axis=1)
    g["w_in"], db_in = _mm_tn(sv["h"], dz, tm=1024, tn=512, tk=512, out_blocks=p["w_in"].shape[0], colsum=True,
                              out_dtype=wdt, name="mm_in_wgrad")
    g["b_in"] = db_in[:1]
    dep_in = on_weight_grads(g) if on_weight_grads is not None else None
    dh = _mm_nt(dz, p["w_in"], tm=512, tn=1024, tk=512, dep=dep_in, name="mm_in_dgrad")
    dx_in, g["g_mix"] = _rms_bwd(sv["x"], p["g_mix"], dh, dx_mid, name="rms_mix_bwd")
    return dx_in, g


def _layer_bwd(dx, p, sv, *, S, wdt=F32):
    dx_mid, g = _layer_bwd_mlp(dx, p, sv, wdt=wdt)
    dx_in, g2 = _layer_bwd_mix(dx_mid, p, sv, S=S, wdt=wdt)
    g.update(g2)
    return dx_in, g


def _local_step(x, tgt, layers, g_final, *, S):
    saved = []
    for p in layers:
        x, sv = _layer_fwd(x, p, S=S)
        saved.append(sv)
    loss, dx, dg_final = _final_loss(x, g_final, tgt, name="final_loss")
    grads = [None] * len(layers)
    for l in reversed(range(len(layers))):
        dx, grads[l] = _layer_bwd(dx, layers[l], saved[l], S=S)
    return loss, dx, grads, dg_final


_HBM = pl.BlockSpec(memory_space=pltpu.HBM)
_MESH = pl.DeviceIdType.MESH


_SEM = pl.BlockSpec(memory_space=pltpu.SEMAPHORE)
_ANY = pl.BlockSpec(memory_space=pl.ANY)
_FLIPS = [(dx, dy, dc) for dx in (0, 1) for dy in (0, 1) for dc in (0, 1)][1:]


def _place(shard, me_idx, dtype, *, name):
    r, cc = shard.shape
    tr = 512 if r % 512 == 0 else r

    def body(me_ref, s_ref, o_ref):
        del me_ref
        o_ref[...] = s_ref[...].astype(dtype)

    grid_spec = pltpu.PrefetchScalarGridSpec(
        num_scalar_prefetch=1, grid=(r // tr,),
        in_specs=[pl.BlockSpec((tr, cc), lambda i, me: (i, 0))],
        out_specs=pl.BlockSpec((None, tr, cc), lambda i, me: (me[0], i, 0)))
    return pl.pallas_call(body, grid_spec=grid_spec, out_shape=jax.ShapeDtypeStruct((N_DEV, r, cc), dtype),
                          compiler_params=_cp("arbitrary"), name=name)(me_idx, shard)


def _exchange_copies(srcs, lands, send_sems, recv_sems):
    x, y, c = lax.axis_index("x"), lax.axis_index("y"), lax.axis_index("c")
    me = 4 * x + 2 * y + c
    pairs = []
    for k, (dx, dy, dc) in enumerate(_FLIPS):
        peer = (1 - x if dx else x, 1 - y if dy else y, 1 - c if dc else c)
        pidx = 4 * peer[0] + 2 * peer[1] + peer[2]
        for a, land in enumerate(lands):
            src = land.at[me] if srcs is None else srcs[a].at[pidx]
            sem = k * len(lands) + a
            out = pltpu.make_async_remote_copy(src_ref=src, dst_ref=land.at[me], send_sem=send_sems.at[sem],
                                               recv_sem=recv_sems.at[sem], device_id=peer, device_id_type=_MESH)
            arrival = pltpu.make_async_remote_copy(src_ref=src, dst_ref=land.at[pidx], send_sem=send_sems.at[sem],
                                                   recv_sem=recv_sems.at[sem], device_id=peer, device_id_type=_MESH)
            pairs.append((out, arrival))
    return pairs


def _exchange_start(srcs, lands, *, name):
    n = len(lands)
    bufs = list(lands) if srcs is None else list(srcs) + list(lands)
    nb = len(bufs)

    def body(*refs):
        ins = refs[:nb]
        send_sems, recv_sems = refs[nb], refs[nb + 1]
        token = refs[-1]
        for out, _ in _exchange_copies(None if srcs is None else ins[:n], ins[nb - n:], send_sems, recv_sems):
            out.start()
        token[...] = jnp.zeros_like(token)

    sems = pltpu.SemaphoreType.DMA((len(_FLIPS) * n,))
    res = pl.pallas_call(
        body, name=name, in_specs=[_HBM] * nb,
        out_shape=(sems, sems, *[pltpu.HBM(b.shape, b.dtype) for b in bufs], jax.ShapeDtypeStruct((SUBLANES, 128), F32)),
        out_specs=(_SEM, _SEM, *[_HBM] * nb, pl.BlockSpec(memory_space=pltpu.VMEM)),
        input_output_aliases={i: 2 + i for i in range(nb)},
        compiler_params=pltpu.CompilerParams(has_side_effects=pltpu.SideEffectType.DATAFLOW_SIDE_EFFECTING),
    )(*[pltpu.with_memory_space_constraint(b, pltpu.HBM) for b in bufs])
    return res[0], res[1], list(res[2:2 + nb]), res[-1]


def _exchange_wait(send_sems, recv_sems, bufs, *, scatter, after, name):
    nb = len(bufs)
    n = nb // 2 if scatter else nb

    def body(*refs):
        ins = refs[:nb]
        for out, arrival in _exchange_copies(ins[:n] if scatter else None, ins[nb - n:], refs[nb], refs[nb + 1]):
            out.wait_send()
            arrival.wait_recv()

    extra = [] if after is None else [after]
    res = pl.pallas_call(
        body, name=name, in_specs=[_HBM] * nb + [_SEM, _SEM] + [_ANY] * len(extra),
        out_shape=tuple(pltpu.HBM(b.shape, b.dtype) for b in bufs), out_specs=tuple([_HBM] * nb),
        input_output_aliases={i: i for i in range(nb)},
        compiler_params=pltpu.CompilerParams(has_side_effects=pltpu.SideEffectType.DATAFLOW_SIDE_EFFECTING),
    )(*bufs, send_sems, recv_sems, *extra)
    return list(res)


def _adamw_math(w, g, m, v):
    m = ADAM_B1 * m + (1.0 - ADAM_B1) * g
    v = ADAM_B2 * v + (1.0 - ADAM_B2) * (g * g)
    m_hat = m / (1.0 - ADAM_B1 ** ADAM_STEP)
    v_hat = v / (1.0 - ADAM_B2 ** ADAM_STEP)
    delta = -ADAM_LR * (m_hat / (jnp.sqrt(v_hat) + ADAM_EPS) + ADAM_WD * w)
    return delta, m, v


def _adamw(w, parts, m, v, *, name):
    r, cc = w.shape
    P = parts.shape[0]
    tr = 512 if r % 512 == 0 else r

    def body(w_ref, p_ref, m_ref, v_ref, g_ref, d_ref, nm_ref, nv_ref):
        g = p_ref[0]
        for q in range(1, P):
            g = g + p_ref[q]
        d, nm, nv = _adamw_math(w_ref[...], g, m_ref[...], v_ref[...])
        g_ref[...] = g
        d_ref[...] = d
        nm_ref[...] = nm
        nv_ref[...] = nv

    blk = pl.BlockSpec((tr, cc), lambda i: (i, 0))
    o = jax.ShapeDtypeStruct((r, cc), F32)
    return pl.pallas_call(body, grid=(r // tr,),
                          in_specs=[blk, pl.BlockSpec((P, tr, cc), lambda i: (0, i, 0)), blk, blk],
                          out_specs=(blk, blk, blk, blk), out_shape=(o, o, o, o),
                          compiler_params=_cp("parallel"), name=name)(w, parts, m, v)


def _adamw_scattered(w, own, land, m, v, me_idx, *, name):
    r, cc = w.shape
    tr = 512 if r % 512 == 0 else r

    def body(me_ref, w_ref, own_ref, land_ref, m_ref, v_ref, g_ref, d_ref, nm_ref, nv_ref):
        me = me_ref[0]
        g = own_ref[...].astype(F32)
        for q in range(N_DEV):
            g = g + jnp.where(q == me, 0.0, land_ref[q].astype(F32))
        d, nm, nv = _adamw_math(w_ref[...], g, m_ref[...], v_ref[...])
        g_ref[...] = g
        d_ref[...] = d
        nm_ref[...] = nm
        nv_ref[...] = nv

    blk = pl.BlockSpec((tr, cc), lambda i, me: (i, 0))
    o = jax.ShapeDtypeStruct((r, cc), F32)
    grid_spec = pltpu.PrefetchScalarGridSpec(
        num_scalar_prefetch=1, grid=(r // tr,),
        in_specs=[blk, pl.BlockSpec((None, tr, cc), lambda i, me: (me[0], i, 0)),
                  pl.BlockSpec((N_DEV, tr, cc), lambda i, me: (0, i, 0)), blk, blk],
        out_specs=(blk, blk, blk, blk))
    return pl.pallas_call(body, grid_spec=grid_spec, out_shape=(o, o, o, o), compiler_params=_cp("parallel"),
                          name=name)(me_idx, w, own, land, m, v)


_SHARDED = ("w_in", "conv_a_w", "w_a_out", "conv_b_w", "w_b_out", "w_o", "w_1", "w_2")
_COL_SHARDED = ("w_in", "conv_a_w", "conv_b_w", "w_1")
_REPLICATED = ("g_mix", "b_in", "conv_a_b", "ln_g", "ln_b", "conv_b_b", "w_rg_a", "b_rg_a", "w_rg_x", "b_rg_x", "lam",
               "g_mlp")
_WEIGHTS = ("g_mix", "w_in", "b_in", "conv_a_w", "conv_a_b", "ln_g", "ln_b", "w_a_out", "conv_b_w", "conv_b_b", "w_rg_a",
            "b_rg_a", "w_rg_x", "b_rg_x", "lam", "w_b_out", "w_o", "g_mlp", "w_1", "w_2", "g_final")
_LANES = 128


def _cols_from_blocks(b):
    nb, K, n = b.shape
    return b.transpose(1, 0, 2).reshape(K, nb * n)


def _blocks_from_cols(w, K):
    n = w.shape[1] // N_DEV
    return w[:K].reshape(K, N_DEV, n).transpose(1, 0, 2)


def kernel(x, g_mix, w_in, b_in, conv_a_w, conv_a_b, ln_g, ln_b, w_a_out, conv_b_w, conv_b_b, w_rg_a, b_rg_a, w_rg_x, b_rg_x, lam, w_b_out, w_o, g_mlp, w_1, w_2, g_final, loss_target, m_g_mix, m_w_in, m_b_in, m_conv_a_w, m_conv_a_b, m_ln_g, m_ln_b, m_w_a_out, m_conv_b_w, m_conv_b_b, m_w_rg_a, m_b_rg_a, m_w_rg_x, m_b_rg_x, m_lam, m_w_b_out, m_w_o, m_g_mlp, m_w_1, m_w_2, m_g_final, v_g_mix, v_w_in, v_b_in, v_conv_a_w, v_conv_a_b, v_ln_g, v_ln_b, v_w_a_out, v_conv_b_w, v_conv_b_b, v_w_rg_a, v_b_rg_a, v_w_rg_x, v_b_rg_x, v_lam, v_w_b_out, v_w_o, v_g_mlp, v_w_1, v_w_2, v_g_final):
    W = dict(g_mix=g_mix, w_in=w_in, b_in=b_in, conv_a_w=conv_a_w, conv_a_b=conv_a_b, ln_g=ln_g, ln_b=ln_b,
             w_a_out=w_a_out, conv_b_w=conv_b_w, conv_b_b=conv_b_b, w_rg_a=w_rg_a, b_rg_a=b_rg_a, w_rg_x=w_rg_x,
             b_rg_x=b_rg_x, lam=lam, w_b_out=w_b_out, w_o=w_o, g_mlp=g_mlp, w_1=w_1, w_2=w_2, g_final=g_final)
    M = dict(g_mix=m_g_mix, w_in=m_w_in, b_in=m_b_in, conv_a_w=m_conv_a_w, conv_a_b=m_conv_a_b, ln_g=m_ln_g, ln_b=m_ln_b,
             w_a_out=m_w_a_out, conv_b_w=m_conv_b_w, conv_b_b=m_conv_b_b, w_rg_a=m_w_rg_a, b_rg_a=m_b_rg_a,
             w_rg_x=m_w_rg_x, b_rg_x=m_b_rg_x, lam=m_lam, w_b_out=m_w_b_out, w_o=m_w_o, g_mlp=m_g_mlp, w_1=m_w_1,
             w_2=m_w_2, g_final=m_g_final)
    V = dict(g_mix=v_g_mix, w_in=v_w_in, b_in=v_b_in, conv_a_w=v_conv_a_w, conv_a_b=v_conv_a_b, ln_g=v_ln_g, ln_b=v_ln_b,
             w_a_out=v_w_a_out, conv_b_w=v_conv_b_w, conv_b_b=v_conv_b_b, w_rg_a=v_w_rg_a, b_rg_a=v_b_rg_a,
             w_rg_x=v_w_rg_x, b_rg_x=v_b_rg_x, lam=v_lam, w_b_out=v_w_b_out, w_o=v_w_o, g_mlp=v_g_mlp, w_1=v_w_1,
             w_2=v_w_2, g_final=v_g_final)
    NB, S, D = x.shape
    L = g_mix.shape[0]
    hd = w_rg_a.shape[-1]
    me_idx = (4 * lax.axis_index("x") + 2 * lax.axis_index("y") + lax.axis_index("c")).astype(jnp.int32).reshape(1)

    stages = (("in", ("w_in",)), ("mix", ("conv_a_w", "w_a_out", "conv_b_w", "w_b_out", "w_o")), ("mlp", ("w_1", "w_2")))
    gathers = {}
    started = jnp.zeros((), F32)
    for l in range(L):
        for stage, names in stages:
            lands = [_place(W[k][l], me_idx, F32 if k.startswith("conv") else BF16, name="place_" + k) for k in names]
            send_sems, recv_sems, bufs, token = _exchange_start(None, lands, name=f"weights_start_{stage}_{l}")
            gathers[l, stage] = (names, send_sems, recv_sems, bufs)
            started = started + token[0, 0]

    xt = x.reshape(NB * S, D)
    layers, saved = [], []
    for l in range(L):
        p = {k: W[k][l][None] for k in ("g_mix", "b_in", "conv_a_b", "ln_g", "ln_b", "conv_b_b", "b_rg_a", "b_rg_x",
                                         "lam", "g_mlp")}
        if l == 0:
            p["g_mix"] = p["g_mix"] + started
        p["wg_a"] = _group_weights(w_rg_a[l])
        p["wg_x"] = _group_weights(w_rg_x[l])

        def fetch(stage, after, l=l, p=p):
            names, send_sems, recv_sems, bufs = gathers[l, stage]
            bufs = _exchange_wait(send_sems, recv_sems, bufs, scatter=False, after=after,
                                  name=f"weights_wait_{stage}_{l}")
            for k, full in zip(names, bufs):
                if k in ("w_in", "w_1"):
                    p[k] = full
                elif k in _COL_SHARDED:
                    p[k] = _cols_from_blocks(full)
                else:
                    p[k] = full.reshape(-1, full.shape[-1])

        layers.append(p)
        xt, sv = _layer_fwd(xt, p, S=S, fetch=fetch)
        saved.append(sv)
    loss, dx, dg_final = _final_loss(xt, g_final[None], loss_target.reshape(NB * S, D), name="final_loss")

    per_layer = {k: [None] * L for k in _SHARDED}

    def scatter_start(l, names, g, tag):
        srcs = []
        for k in names:
            shard_shape = W[k].shape[1:]
            if k in ("w_in", "w_1"):
                srcs.append(g[k])
            elif k in _COL_SHARDED:
                srcs.append(_blocks_from_cols(g[k], shard_shape[0]).astype(BF16))
            else:
                srcs.append(g[k].reshape((N_DEV,) + shard_shape))
        lands = [lax.empty(s.shape, BF16) for s in srcs]
        send_sems, recv_sems, bufs, token = _exchange_start(srcs, lands, name=f"grads_start_{tag}_{l}")
        return (names, send_sems, recv_sems, bufs, tag), token

    def scatter_finish(l, flight, after):
        names, send_sems, recv_sems, bufs, tag = flight
        bufs = _exchange_wait(send_sems, recv_sems, bufs, scatter=True, after=after, name=f"grads_wait_{tag}_{l}")
        n = len(names)
        for k, own, land in zip(names, bufs[:n], bufs[n:]):
            per_layer[k][l] = _adamw_scattered(W[k][l], own, land, M[k][l], V[k][l], me_idx, name="adamw_" + k)

    def pack(pieces):
        flat = jnp.concatenate([a.reshape(-1) for a in pieces])
        pad = -flat.shape[0] % (512 * _LANES)
        return jnp.pad(flat, (0, pad)).reshape(-1, _LANES)

    def pack_layer(get, l, final, last):
        return pack([get(k, l) for k in _REPLICATED] + ([final, last] if l == 0 else []))

    def small_grad(k, l):
        if k in ("w_rg_a", "w_rg_x"):
            return _ungroup_grads(grads[l]["wg_a" if k == "w_rg_a" else "wg_x"], hd)
        return grads[l][k]

    def small_start(l):
        land = _place(pack_layer(small_grad, l, dg_final, loss[0, :1]), me_idx, F32, name="place_small")
        send_sems, recv_sems, bufs, token = _exchange_start(None, [land], name=f"small_start_{l}")
        return (send_sems, recv_sems, bufs), token

    small_out = [None] * L
    zero = jnp.zeros((1,), F32)

    def small_finish(l, flight, after):
        send_sems, recv_sems, bufs = flight
        (g_all,) = _exchange_wait(send_sems, recv_sems, bufs, scatter=False, after=after, name=f"small_wait_{l}")
        small_out[l] = _adamw(pack_layer(lambda k, i: W[k][i], l, g_final, zero), g_all,
                              pack_layer(lambda k, i: M[k][i], l, m_g_final, zero),
                              pack_layer(lambda k, i: V[k][i], l, v_g_final, zero), name="adamw_small")

    grads = [None] * L
    in_flight = []
    dep = None
    for l in reversed(range(L)):
        dx_mid, g = _layer_bwd_mlp(dx, layers[l], saved[l], wdt=BF16, dep=dep)
        f_mlp, dep = scatter_start(l, ("w_2", "w_1"), g, "mlp")
        flights = [f_mlp]

        def on_weight_grads(g_part, l=l, flights=flights):
            f_mix, token = scatter_start(l, ("w_o", "w_b_out", "w_a_out", "conv_a_w", "conv_b_w", "w_in"), g_part, "mix")
            flights.append(f_mix)
            return token

        dx, g_mix_part = _layer_bwd_mix(dx_mid, layers[l], saved[l], S=S, wdt=BF16, dep=dep,
                                        on_weight_grads=on_weight_grads)
        g.update(g_mix_part)
        grads[l] = g
        f_small, dep = small_start(l)
        for l_prev, fs, f_sm in in_flight:
            for flight in fs:
                scatter_finish(l_prev, flight, dx)
            small_finish(l_prev, f_sm, dx)
        in_flight = [(l, flights, f_small)]
    for l_prev, fs, f_sm in in_flight:
        for flight in fs:
            scatter_finish(l_prev, flight, None)
        small_finish(l_prev, f_sm, None)
    out_g, out_d, out_m, out_v = {}, {}, {}, {}
    for k in _SHARDED:
        out_g[k], out_d[k], out_m[k], out_v[k] = (jnp.stack([per_layer[k][l][i] for l in range(L)]) for i in range(4))
    for i, res in enumerate((out_g, out_d, out_m, out_v)):
        off = 0
        for k in _REPLICATED:
            size = W[k][0].size
            res[k] = jnp.stack([small_out[l][i].reshape(-1)[off:off + size].reshape(W[k].shape[1:]) for l in range(L)])
            off += size
        res["g_final"] = small_out[0][i].reshape(-1)[off:off + g_final.size]
    loss_out = small_out[0][0].reshape(-1)[off + g_final.size]

    return (loss_out, dx.reshape(NB, S, D), *[out_g[k] for k in _WEIGHTS], *[out_d[k] for k in _WEIGHTS],
            *[out_m[k] for k in _WEIGHTS], *[out_v[k] for k in _WEIGHTS])
```

```python
import functools

import jax
import jax.numpy as jnp
from jax import lax
from jax.experimental import pallas as pl
from jax.experimental.pallas import tpu as pltpu

F32 = jnp.float32
BF16 = jnp.bfloat16

EPS = 1e-6
LRU_C = 8.0
N_RNN_HEADS = 16
HEADS_PER_GROUP = 4
N_DEV = 8
ADAM_LR, ADAM_B1, ADAM_B2, ADAM_EPS, ADAM_WD, ADAM_STEP = 0.001, 0.9, 0.999, 1e-08, 0.01, 10

VMEM_LIMIT_BYTES = 48 * 1024 * 1024
CONV_PAD = 32
CONV_CHUNK = 128
SUBLANES = 8


def _cp(*sem):
    return pltpu.CompilerParams(dimension_semantics=sem, vmem_limit_bytes=VMEM_LIMIT_BYTES)


def _sig(x):
    return 1.0 / (1.0 + jnp.exp(-x))


def _gelu(x):
    c = 0.7978845608028654
    return 0.5 * x * (1.0 + jnp.tanh(c * (x + 0.044715 * x * x * x)))


def _gelu_grad(x):
    c = 0.7978845608028654
    th = jnp.tanh(c * (x + 0.044715 * x * x * x))
    return 0.5 * (1.0 + th) + 0.5 * x * (1.0 - th * th) * c * (1.0 + 3.0 * 0.044715 * x * x)


def _mm_nn(a, b, *, tm, tn, name, bias=None, resid=None, relu2=False, out_dtype=F32, a_resident=False):
    M, K = a.shape
    blocked = b.ndim == 3
    N = b.shape[0] * b.shape[2] if blocked else b.shape[1]
    tm = min(tm, M)
    tn = min(tn, N)
    if blocked:
        assert tn == b.shape[2]
    n_extra = (bias is not None) + (resid is not None)

    def body(*refs):
        acc = jnp.dot(refs[0][...].astype(BF16), refs[1][...].astype(BF16), preferred_element_type=F32)
        k = 2
        if bias is not None:
            acc = acc + refs[k][...]
            k += 1
        if resid is not None:
            acc = acc + refs[k][...]
            k += 1
        if relu2:
            refs[k][...] = acc
            p = jnp.maximum(acc, 0.0)
            refs[k + 1][...] = (p * p).astype(BF16)
        else:
            refs[k][...] = acc.astype(out_dtype)

    def spec(shape, index):
        return pl.BlockSpec(shape, (lambda i, j: index(j, i)) if a_resident else index)

    in_specs = [spec((tm, K), lambda j, i: (i, 0))]
    if blocked:
        in_specs.append(spec((None, K, tn), lambda j, i: (j, 0, 0)))
    else:
        in_specs.append(spec((K, tn), lambda j, i: (0, j)))
    args = [a, b]
    if bias is not None:
        in_specs.append(spec((1, tn), lambda j, i: (0, j)))
        args.append(bias)
    if resid is not None:
        in_specs.append(spec((tm, tn), lambda j, i: (i, j)))
        args.append(resid)
    o_spec = spec((tm, tn), lambda j, i: (i, j))
    if relu2:
        out_shape = (jax.ShapeDtypeStruct((M, N), F32), jax.ShapeDtypeStruct((M, N), BF16))
        out_specs = (o_spec, o_spec)
    else:
        out_shape = jax.ShapeDtypeStruct((M, N), out_dtype)
        out_specs = o_spec
    del n_extra
    grid = (M // tm, N // tn) if a_resident else (N // tn, M // tm)
    return pl.pallas_call(body, grid=grid, in_specs=in_specs, out_specs=out_specs,
                          out_shape=out_shape, compiler_params=_cp("parallel", "parallel"), name=name)(*args)


def _mm_nt(a, b, *, tm, tn, tk, name, mul_relu=None, resid=None, out_dtype=F32, dep=None):
    M, N = a.shape
    blocked = b.ndim == 3
    Kout = b.shape[1] if blocked else b.shape[0]
    tm = min(tm, M)
    tn = min(tn, Kout)
    tk = b.shape[2] if blocked else min(tk, N)
    nk = N // tk

    def body(*refs):
        acc_ref = refs[-1]
        kk = pl.program_id(2)
        part = lax.dot_general(refs[0][...].astype(BF16), refs[1][...].astype(BF16),
                               (((1,), (1,)), ((), ())), preferred_element_type=F32)

        @pl.when(kk == 0)
        def _():
            acc_ref[...] = part

        @pl.when(kk > 0)
        def _():
            acc_ref[...] += part

        @pl.when(kk == nk - 1)
        def _():
            acc = acc_ref[...]
            k = 2
            if mul_relu is not None:
                acc = acc * (2.0 * jnp.maximum(refs[k][...], 0.0))
                k += 1
            if resid is not None:
                acc = acc + refs[k][...]
                k += 1
            if dep is not None:
                k += 1
            refs[k][...] = acc.astype(out_dtype)

    in_specs = [pl.BlockSpec((tm, tk), lambda i, j, k: (i, k))]
    if blocked:
        in_specs.append(pl.BlockSpec((None, tn, tk), lambda i, j, k: (k, j, 0)))
    else:
        in_specs.append(pl.BlockSpec((tn, tk), lambda i, j, k: (j, k)))
    args = [a, b]
    for extra in (mul_relu, resid):
        if extra is not None:
            in_specs.append(pl.BlockSpec((tm, tn), lambda i, j, k: (i, j)))
            args.append(extra)
    if dep is not None:
        in_specs.append(pl.BlockSpec(dep.shape, lambda i, j, k: (0, 0)))
        args.append(dep)
    return pl.pallas_call(body, grid=(M // tm, Kout // tn, nk), in_specs=in_specs,
                          out_specs=pl.BlockSpec((tm, tn), lambda i, j, k: (i, j)),
                          out_shape=jax.ShapeDtypeStruct((M, Kout), out_dtype),
                          scratch_shapes=[pltpu.VMEM((tm, tn), F32)],
                          compiler_params=_cp("parallel", "parallel", "arbitrary"), name=name)(*args)


def _mm_tn(a, b, *, tm, tn, tk, name, out_blocks=None, colsum=False, out_dtype=F32):
    T, M = a.shape
    N = b.shape[1]
    tm = min(tm, M)
    tk = min(tk, T)
    if out_blocks is not None:
        tn = N // out_blocks
        tm = M
    tn = min(tn, N)
    nk = T // tk
    if colsum:
        assert tm == M

    def body(*refs):
        a_ref, b_ref, o_ref, acc_ref = refs[0], refs[1], refs[2], refs[-1]
        kk = pl.program_id(2)
        bv = b_ref[...]
        part = lax.dot_general(a_ref[...].astype(BF16), bv.astype(BF16),
                               (((0,), (0,)), ((), ())), preferred_element_type=F32)

        if colsum:
            csum = jnp.broadcast_to(jnp.sum(bv.astype(F32), axis=0, keepdims=True), (SUBLANES, tn))

        @pl.when(kk == 0)
        def _():
            acc_ref[...] = part
            if colsum:
                refs[3][...] = csum

        @pl.when(kk > 0)
        def _():
            acc_ref[...] += part
            if colsum:
                refs[3][...] += csum

        @pl.when(kk == nk - 1)
        def _():
            o_ref[...] = acc_ref[...].astype(out_dtype)

    in_specs = [pl.BlockSpec((tk, tm), lambda i, j, k: (k, i)), pl.BlockSpec((tk, tn), lambda i, j, k: (k, j))]
    if out_blocks is not None:
        o_shape = jax.ShapeDtypeStruct((out_blocks, M, tn), out_dtype)
        o_spec = pl.BlockSpec((None, M, tn), lambda i, j, k: (j, 0, 0))
    else:
        o_shape = jax.ShapeDtypeStruct((M, N), out_dtype)
        o_spec = pl.BlockSpec((tm, tn), lambda i, j, k: (i, j))
    if colsum:
        out_shape = (o_shape, jax.ShapeDtypeStruct((SUBLANES, N), F32))
        out_specs = (o_spec, pl.BlockSpec((SUBLANES, tn), lambda i, j, k: (0, j)))
    else:
        out_shape, out_specs = o_shape, o_spec
    return pl.pallas_call(body, grid=(M // tm, N // tn, nk), in_specs=in_specs, out_specs=out_specs,
                          out_shape=out_shape, scratch_shapes=[pltpu.VMEM((tm, tn), F32)],
                          compiler_params=_cp("parallel", "parallel", "arbitrary"), name=name)(a, b)


def _rms_fwd(x, g, *, name, tr=512):
    T, D = x.shape
    tr = min(tr, T)

    def body(x_ref, g_ref, h_ref):
        xv = x_ref[...]
        r = lax.rsqrt(jnp.mean(xv * xv, axis=-1, keepdims=True) + EPS)
        h_ref[...] = (xv * r * g_ref[...]).astype(BF16)

    return pl.pallas_call(body, grid=(T // tr,),
                          in_specs=[pl.BlockSpec((tr, D), lambda i: (i, 0)), pl.BlockSpec((1, D), lambda i: (0, 0))],
                          out_specs=pl.BlockSpec((tr, D), lambda i: (i, 0)),
                          out_shape=jax.ShapeDtypeStruct((T, D), BF16), compiler_params=_cp("parallel"), name=name)(x, g)


def _rms_bwd(x, g, dh, dres, *, name, tr=512):
    T, D = x.shape
    tr = min(tr, T)

    def body(x_ref, g_ref, dh_ref, dres_ref, dx_ref, dg_ref):
        xv = x_ref[...]
        r = lax.rsqrt(jnp.mean(xv * xv, axis=-1, keepdims=True) + EPS)
        n = xv * r
        dh = dh_ref[...]
        dn = dh * g_ref[...]
        dx_ref[...] = dres_ref[...] + r * (dn - n * jnp.mean(dn * n, axis=-1, keepdims=True))
        part = jnp.sum(dh * n, axis=0, keepdims=True)

        @pl.when(pl.program_id(0) == 0)
        def _():
            dg_ref[...] = part

        @pl.when(pl.program_id(0) > 0)
        def _():
            dg_ref[...] += part

    row = pl.BlockSpec((tr, D), lambda i: (i, 0))
    vec = pl.BlockSpec((1, D), lambda i: (0, 0))
    return pl.pallas_call(body, grid=(T // tr,), in_specs=[row, vec, row, row], out_specs=(row, vec),
                          out_shape=(jax.ShapeDtypeStruct((T, D), F32), jax.ShapeDtypeStruct((1, D), F32)),
                          compiler_params=_cp("arbitrary"), name=name)(x, g, dh, dres)


def _final_loss(x, g, tgt, *, name, tr=512):
    T, D = x.shape
    tr = min(tr, T)

    def body(x_ref, g_ref, t_ref, loss_ref, dx_ref, dg_ref):
        xv = x_ref[...]
        gv = g_ref[...]
        r = lax.rsqrt(jnp.mean(xv * xv, axis=-1, keepdims=True) + EPS)
        n = xv * r
        e = n * gv - t_ref[...]
        lpart = 0.5 * jnp.sum(jnp.mean(e * e, axis=-1, keepdims=True), axis=0, keepdims=True)
        dy = e * (1.0 / D)
        dn = dy * gv
        dx_ref[...] = r * (dn - n * jnp.mean(dn * n, axis=-1, keepdims=True))
        gpart = jnp.sum(dy * n, axis=0, keepdims=True)

        @pl.when(pl.program_id(0) == 0)
        def _():
            dg_ref[...] = gpart
            loss_ref[...] = jnp.broadcast_to(lpart, (1, 128))

        @pl.when(pl.program_id(0) > 0)
        def _():
            dg_ref[...] += gpart
            loss_ref[...] += jnp.broadcast_to(lpart, (1, 128))

    row = pl.BlockSpec((tr, D), lambda i: (i, 0))
    vec = pl.BlockSpec((1, D), lambda i: (0, 0))
    return pl.pallas_call(body, grid=(T // tr,), in_specs=[row, vec, row],
                          out_specs=(pl.BlockSpec((1, 128), lambda i: (0, 0)), row, vec),
                          out_shape=(jax.ShapeDtypeStruct((1, 128), F32), jax.ShapeDtypeStruct((T, D), F32),
                                     jax.ShapeDtypeStruct((1, D), F32)),
                          compiler_params=_cp("arbitrary"), name=name)(x, g, tgt)


def _ln_silu_fwd(u, g, b, *, name, tr=512):
    T, C = u.shape
    tr = min(tr, T)

    def body(u_ref, g_ref, b_ref, o_ref):
        uv = u_ref[...]
        mu = jnp.mean(uv, axis=-1, keepdims=True)
        xc = uv - mu
        r = lax.rsqrt(jnp.mean(xc * xc, axis=-1, keepdims=True) + EPS)
        y = xc * r * g_ref[...] + b_ref[...]
        o_ref[...] = (y * _sig(y)).astype(BF16)

    row = pl.BlockSpec((tr, C), lambda i: (i, 0))
    vec = pl.BlockSpec((1, C), lambda i: (0, 0))
    return pl.pallas_call(body, grid=(T // tr,), in_specs=[row, vec, vec], out_specs=row,
                          out_shape=jax.ShapeDtypeStruct((T, C), BF16), compiler_params=_cp("parallel"),
                          name=name)(u, g, b)


def _ln_silu_bwd(u, g, b, do, *, name, tr=512):
    T, C = u.shape
    tr = min(tr, T)

    def body(u_ref, g_ref, b_ref, do_ref, du_ref, dg_ref, db_ref):
        uv = u_ref[...]
        gv = g_ref[...]
        mu = jnp.mean(uv, axis=-1, keepdims=True)
        xc = uv - mu
        r = lax.rsqrt(jnp.mean(xc * xc, axis=-1, keepdims=True) + EPS)
        n = xc * r
        y = n * gv + b_ref[...]
        s = _sig(y)
        dy = do_ref[...] * (s * (1.0 + y * (1.0 - s)))
        dn = dy * gv
        du_ref[...] = r * (dn - jnp.mean(dn, axis=-1, keepdims=True) - n * jnp.mean(dn * n, axis=-1, keepdims=True))
        gpart = jnp.sum(dy * n, axis=0, keepdims=True)
        bpart = jnp.sum(dy, axis=0, keepdims=True)

        @pl.when(pl.program_id(0) == 0)
        def _():
            dg_ref[...] = gpart
            db_ref[...] = bpart

        @pl.when(pl.program_id(0) > 0)
        def _():
            dg_ref[...] += gpart
            db_ref[...] += bpart

    row = pl.BlockSpec((tr, C), lambda i: (i, 0))
    vec = pl.BlockSpec((1, C), lambda i: (0, 0))
    return pl.pallas_call(body, grid=(T // tr,), in_specs=[row, vec, vec, row], out_specs=(row, vec, vec),
                          out_shape=(jax.ShapeDtypeStruct((T, C), F32), jax.ShapeDtypeStruct((1, C), F32),
                                     jax.ShapeDtypeStruct((1, C), F32)),
                          compiler_params=_cp("arbitrary"), name=name)(u, g, b, do)


def _merge_fwd(z, ya, yb, *, off_sa, off_sb, name, tr=512):
    T, D = ya.shape
    tr = min(tr, T)
    assert off_sa % D == 0 and off_sb % D == 0

    def body(sa_ref, sb_ref, ya_ref, yb_ref, m_ref):
        m_ref[...] = (_sig(sa_ref[...]) * ya_ref[...] + _sig(sb_ref[...]) * yb_ref[...]).astype(BF16)

    row = pl.BlockSpec((tr, D), lambda i: (i, 0))
    return pl.pallas_call(body, grid=(T // tr,),
                          in_specs=[pl.BlockSpec((tr, D), lambda i: (i, off_sa // D)),
                                    pl.BlockSpec((tr, D), lambda i: (i, off_sb // D)), row, row],
                          out_specs=row, out_shape=jax.ShapeDtypeStruct((T, D), BF16),
                          compiler_params=_cp("parallel"), name=name)(z, z, ya, yb)


def _merge_bwd(z, ya, yb, dm, *, off_sa, off_sb, name, tr=512):
    T, D = ya.shape
    tr = min(tr, T)

    def body(sa_ref, sb_ref, ya_ref, yb_ref, dm_ref, dya_ref, dyb_ref, dsa_ref, dsb_ref):
        dm = dm_ref[...]
        ga = _sig(sa_ref[...])
        gb = _sig(sb_ref[...])
        dya_ref[...] = (dm * ga).astype(BF16)
        dyb_ref[...] = (dm * gb).astype(BF16)
        dsa_ref[...] = (dm * ya_ref[...] * ga * (1.0 - ga)).astype(BF16)
        dsb_ref[...] = (dm * yb_ref[...] * gb * (1.0 - gb)).astype(BF16)

    row = pl.BlockSpec((tr, D), lambda i: (i, 0))
    o = jax.ShapeDtypeStruct((T, D), BF16)
    return pl.pallas_call(body, grid=(T // tr,),
                          in_specs=[pl.BlockSpec((tr, D), lambda i: (i, off_sa // D)),
                                    pl.BlockSpec((tr, D), lambda i: (i, off_sb // D)), row, row, row],
                          out_specs=(row, row, row, row), out_shape=(o, o, o, o),
                          compiler_params=_cp("parallel"), name=name)(z, z, ya, yb, dm)


def _conv_taps(pad_ref, w_ref, K, base, rows, lanes, reverse):
    acc = None
    for s in range(K):
        st = base + s if reverse else base - s
        term = w_ref[K - 1 - s:K - s, lanes] * pad_ref[st:st + rows, lanes]
        acc = term if acc is None else acc + term
    return acc


def _conv_fwd(z, w, b, *, S, off_v, off_g, name, ct=256):
    T = z.shape[0]
    K, C = w.shape
    ct = min(ct, C)
    ch = min(CONV_CHUNK, S)
    glu = off_g is not None
    assert off_v % ct == 0 and (not glu or off_g % ct == 0) and K - 1 <= CONV_PAD

    def body(*refs):
        if glu:
            v_ref, g_ref, w_ref, b_ref, o_ref, pad_ref = refs
        else:
            v_ref, w_ref, b_ref, o_ref, pad_ref = refs
        pad_ref[0:CONV_PAD, :] = jnp.zeros((CONV_PAD, ct), F32)
        if glu:
            pad_ref[CONV_PAD:CONV_PAD + S, :] = v_ref[...] * _sig(g_ref[...])
        else:
            pad_ref[CONV_PAD:CONV_PAD + S, :] = v_ref[...]
        for l0 in range(0, ct, 128):
            lanes = slice(l0, l0 + 128)
            for c in range(S // ch):
                acc = _conv_taps(pad_ref, w_ref, K, CONV_PAD + c * ch, ch, lanes, False)
                o_ref[c * ch:(c + 1) * ch, lanes] = acc + b_ref[:, lanes]

    in_specs = [pl.BlockSpec((S, ct), lambda j, bb: (bb, off_v // ct + j))]
    args = [z]
    if glu:
        in_specs.append(pl.BlockSpec((S, ct), lambda j, bb: (bb, off_g // ct + j)))
        args.append(z)
    in_specs += [pl.BlockSpec((K, ct), lambda j, bb: (0, j)), pl.BlockSpec((1, ct), lambda j, bb: (0, j))]
    args += [w, b]
    return pl.pallas_call(body, grid=(C // ct, T // S), in_specs=in_specs,
                          out_specs=pl.BlockSpec((S, ct), lambda j, bb: (bb, j)),
                          out_shape=jax.ShapeDtypeStruct((T, C), F32),
                          scratch_shapes=[pltpu.VMEM((CONV_PAD + S, ct), F32)],
                          compiler_params=_cp("parallel", "parallel"), name=name)(*args)


def _conv_bwd(z, w, dy, *, S, off_v, off_g, name, ct=256):
    T = z.shape[0]
    K, C = w.shape
    KP = -(-K // SUBLANES) * SUBLANES
    ct = min(ct, C)
    ch = min(CONV_CHUNK, S)
    glu = off_g is not None
    n_seq = T // S

    def body(*refs):
        if glu:
            v_ref, g_ref, w_ref, dy_ref, dv_ref, dgo_ref, dw_ref, db_ref, pad_ref, padb_ref = refs
        else:
            v_ref, w_ref, dy_ref, dv_ref, dw_ref, db_ref, pad_ref, padb_ref = refs
        bb = pl.program_id(1)
        pad_ref[0:CONV_PAD, :] = jnp.zeros((CONV_PAD, ct), F32)
        if glu:
            pad_ref[CONV_PAD:CONV_PAD + S, :] = v_ref[...] * _sig(g_ref[...])
        else:
            pad_ref[CONV_PAD:CONV_PAD + S, :] = v_ref[...]
        padb_ref[0:S, :] = dy_ref[...]
        padb_ref[S:S + CONV_PAD, :] = jnp.zeros((CONV_PAD, ct), F32)

        @pl.when(bb == 0)
        def _():
            dw_ref[...] = jnp.zeros((KP, ct), F32)
            db_ref[...] = jnp.zeros((1, ct), F32)

        for l0 in range(0, ct, 128):
            lanes = slice(l0, l0 + 128)
            for c in range(S // ch):
                du = _conv_taps(padb_ref, w_ref, K, c * ch, ch, lanes, True)
                rows = slice(c * ch, (c + 1) * ch)
                if glu:
                    sg = _sig(g_ref[rows, lanes])
                    dv_ref[rows, lanes] = (du * sg).astype(BF16)
                    dgo_ref[rows, lanes] = (du * v_ref[rows, lanes] * sg * (1.0 - sg)).astype(BF16)
                else:
                    dv_ref[rows, lanes] = du.astype(BF16)
            for s in range(K):
                acc = None
                for c in range(S // ch):
                    prod = padb_ref[c * ch:(c + 1) * ch, lanes] * pad_ref[CONV_PAD + c * ch - s:CONV_PAD + c * ch - s + ch, lanes]
                    acc = prod if acc is None else acc + prod
                dw_ref[K - 1 - s:K - s, lanes] += jnp.sum(acc, axis=0, keepdims=True)
            db_ref[:, lanes] += jnp.sum(dy_ref[:, lanes], axis=0, keepdims=True)

    blk = lambda off: pl.BlockSpec((S, ct), lambda j, bb: (bb, off // ct + j))
    in_specs = [blk(off_v)]
    args = [z]
    if glu:
        in_specs.append(blk(off_g))
        args.append(z)
    in_specs += [pl.BlockSpec((K, ct), lambda j, bb: (0, j)), blk(0)]
    args += [w, dy]
    o_bf = jax.ShapeDtypeStruct((T, C), BF16)
    out_shape = [o_bf] + ([o_bf] if glu else []) + [jax.ShapeDtypeStruct((KP, C), F32), jax.ShapeDtypeStruct((1, C), F32)]
    out_specs = [blk(0)] + ([blk(0)] if glu else []) + [pl.BlockSpec((KP, ct), lambda j, bb: (0, j)),
                                                        pl.BlockSpec((1, ct), lambda j, bb: (0, j))]
    del n_seq
    return pl.pallas_call(body, grid=(C // ct, T // S), in_specs=in_specs, out_specs=tuple(out_specs),
                          out_shape=tuple(out_shape),
                          scratch_shapes=[pltpu.VMEM((CONV_PAD + S, ct), F32), pltpu.VMEM((S + CONV_PAD, ct), F32)],
                          compiler_params=_cp("parallel", "arbitrary"), name=name)(*args)


def _softplus_neg(lam):
    return jnp.maximum(-lam, 0.0) + jnp.log1p(jnp.exp(-jnp.abs(lam)))


def _neg_expm1(x):
    u = jnp.exp(x)
    um1 = u - 1.0
    lg = jnp.log(u)
    safe = jnp.where(lg == 0.0, 1.0, lg)
    em1 = jnp.where(um1 == 0.0, x, jnp.where(um1 == -1.0, -1.0, um1 * x / safe))
    return -em1


def _gates_fwd(v, wa, wx, ba, bx, lam, *, S, name, tm=512):
    T, C = v.shape
    G, gw, _ = wa.shape
    tm = min(tm, T)

    def body(v_ref, wa_ref, wx_ref, ba_ref, bx_ref, lam_ref, r_ref, i_ref, a_ref, bt_ref):
        vv = v_ref[...]
        vb = vv.astype(BF16)
        r = _sig(jnp.dot(vb, wa_ref[...], preferred_element_type=F32) + ba_ref[...])
        ig = _sig(jnp.dot(vb, wx_ref[...], preferred_element_type=F32) + bx_ref[...])
        log_a = -LRU_C * r * _softplus_neg(lam_ref[...])
        a = jnp.exp(log_a)
        mult = jnp.sqrt(_neg_expm1(2.0 * log_a))
        row = pl.program_id(0) * tm + lax.broadcasted_iota(jnp.int32, (tm, gw), 0)
        mult = jnp.where(row % S == 0, 1.0, mult)
        r_ref[...] = r
        i_ref[...] = ig
        a_ref[...] = a
        bt_ref[...] = mult * ig * vv

    blk = pl.BlockSpec((tm, gw), lambda i, g: (i, g))
    wsp = pl.BlockSpec((None, gw, gw), lambda i, g: (g, 0, 0))
    vec = pl.BlockSpec((1, gw), lambda i, g: (0, g))
    o = jax.ShapeDtypeStruct((T, C), F32)
    return pl.pallas_call(body, grid=(T // tm, G), in_specs=[blk, wsp, wsp, vec, vec, vec],
                          out_specs=(blk, blk, blk, blk), out_shape=(o, o, o, o),
                          compiler_params=_cp("parallel", "parallel"), name=name)(v, wa, wx, ba, bx, lam)


def _scan_fwd(a, bt, z, *, S, off_gate, name, ct=512, tt=256):
    T, C = a.shape
    ct = min(ct, C)
    tt = min(tt, S)
    nt = S // tt
    assert off_gate % ct == 0

    def body(a_ref, b_ref, g_ref, h_ref, hb_ref, carry_ref):
        row = lax.broadcasted_iota(jnp.int32, (SUBLANES, ct), 0)

        @pl.when(pl.program_id(2) == 0)
        def _():
            carry_ref[...] = jnp.zeros((SUBLANES, ct), F32)

        def step(i, carry):
            st = pl.multiple_of(i * SUBLANES, SUBLANES)
            A = a_ref[pl.ds(st, SUBLANES), :]
            B = b_ref[pl.ds(st, SUBLANES), :]
            for d in (1, 2, 4):
                m = row >= d
                Bn = jnp.where(m, A * pltpu.roll(B, d, 0) + B, B)
                A = jnp.where(m, A * pltpu.roll(A, d, 0), A)
                B = Bn
            h = B + A * carry
            h_ref[pl.ds(st, SUBLANES), :] = h
            return jnp.broadcast_to(h[SUBLANES - 1:SUBLANES, :], (SUBLANES, ct))

        carry_ref[...] = lax.fori_loop(0, tt // SUBLANES, step, carry_ref[...], unroll=2)
        hb_ref[...] = (h_ref[...] * _gelu(g_ref[...])).astype(BF16)

    blk = pl.BlockSpec((tt, ct), lambda j, bb, t: (bb * nt + t, j))
    return pl.pallas_call(body, grid=(C // ct, T // S, nt),
                          in_specs=[blk, blk, pl.BlockSpec((tt, ct), lambda j, bb, t: (bb * nt + t, off_gate // ct + j))],
                          out_specs=(blk, blk),
                          out_shape=(jax.ShapeDtypeStruct((T, C), F32), jax.ShapeDtypeStruct((T, C), BF16)),
                          scratch_shapes=[pltpu.VMEM((SUBLANES, ct), F32)],
                          compiler_params=_cp("parallel", "parallel", "arbitrary"), name=name)(a, bt, z)


def _scan_bwd(a, h, z, dhb, *, S, off_gate, name, ct=512, tt=256):
    T, C = a.shape
    ct = min(ct, C)
    tt = min(tt, S)
    nt = S // tt
    n_tiles = tt // SUBLANES

    def body(a_ref, h_ref, hp_ref, g_ref, dhb_ref, G_ref, da_ref, dgate_ref, dh_ref, carry_ref):
        row = lax.broadcasted_iota(jnp.int32, (SUBLANES, ct), 0)
        tb = nt - 1 - pl.program_id(2)

        @pl.when(pl.program_id(2) == 0)
        def _():
            carry_ref[...] = jnp.zeros((SUBLANES, ct), F32)

        gate = g_ref[...]
        dhb = dhb_ref[...]
        dh_ref[...] = dhb * _gelu(gate)
        dgate_ref[...] = (dhb * h_ref[...] * _gelu_grad(gate)).astype(BF16)
        h_before = jnp.where(tb > 0, jnp.broadcast_to(hp_ref[SUBLANES - 1:SUBLANES, :], (SUBLANES, ct)), 0.0)

        def step(k, qcarry):
            i = n_tiles - 1 - k
            st = pl.multiple_of(i * SUBLANES, SUBLANES)
            stp = pl.multiple_of(jnp.maximum(i - 1, 0) * SUBLANES, SUBLANES)
            A = a_ref[pl.ds(st, SUBLANES), :]
            hv = h_ref[pl.ds(st, SUBLANES), :]
            hprev_tile = h_ref[pl.ds(stp, SUBLANES), :]
            dh = dh_ref[pl.ds(st, SUBLANES), :]
            Aq = A
            Bq = A * dh
            for d in (1, 2, 4):
                m = row < SUBLANES - d
                Bn = jnp.where(m, Aq * pltpu.roll(Bq, SUBLANES - d, 0) + Bq, Bq)
                Aq = jnp.where(m, Aq * pltpu.roll(Aq, SUBLANES - d, 0), Aq)
                Bq = Bn
            q = Bq + Aq * qcarry
            qnext = jnp.where(row == SUBLANES - 1, qcarry, pltpu.roll(q, SUBLANES - 1, 0))
            g = dh + qnext
            hlast = jnp.where(i > 0, jnp.broadcast_to(hprev_tile[SUBLANES - 1:SUBLANES, :], (SUBLANES, ct)), h_before)
            hprev = jnp.where(row == 0, hlast, pltpu.roll(hv, 1, 0))
            G_ref[pl.ds(st, SUBLANES), :] = g
            da_ref[pl.ds(st, SUBLANES), :] = g * hprev
            return jnp.broadcast_to(q[0:1, :], (SUBLANES, ct))

        carry_ref[...] = lax.fori_loop(0, n_tiles, step, carry_ref[...], unroll=2)

    tiles_per_block = tt // SUBLANES
    blk = pl.BlockSpec((tt, ct), lambda j, bb, t: (bb * nt + nt - 1 - t, j))
    before = pl.BlockSpec((SUBLANES, ct),
                          lambda j, bb, t: (jnp.maximum((bb * nt + nt - 1 - t) * tiles_per_block - 1, 0), j))
    o = jax.ShapeDtypeStruct((T, C), F32)
    return pl.pallas_call(body, grid=(C // ct, T // S, nt),
                          in_specs=[blk, blk, before,
                                    pl.BlockSpec((tt, ct), lambda j, bb, t: (bb * nt + nt - 1 - t, off_gate // ct + j)), blk],
                          out_specs=(blk, blk, blk), out_shape=(o, o, jax.ShapeDtypeStruct((T, C), BF16)),
                          scratch_shapes=[pltpu.VMEM((tt, ct), F32), pltpu.VMEM((SUBLANES, ct), F32)],
                          compiler_params=_cp("parallel", "parallel", "arbitrary"), name=name)(a, h, h, z, dhb)


def _gates_bwd(v, r, ig, Gb, da, lam, *, S, name, tr=256):
    T, C = v.shape
    tr = min(tr, T)

    def body(v_ref, r_ref, i_ref, G_ref, da_ref, lam_ref, dpr_ref, dpi_ref, dv_ref, dsp_ref, dba_ref, dbx_ref):
        vv, r, ig, Gv = v_ref[...], r_ref[...], i_ref[...], G_ref[...]
        sp = _softplus_neg(lam_ref[...])
        log_a = -LRU_C * r * sp
        a = jnp.exp(log_a)
        mult_raw = jnp.sqrt(_neg_expm1(2.0 * log_a))
        row = pl.program_id(0) * tr + lax.broadcasted_iota(jnp.int32, (tr, C), 0)
        start = row % S == 0
        mult = jnp.where(start, 1.0, mult_raw)
        dmult = jnp.where(start, 0.0, Gv * ig * vv)
        di = Gv * mult * vv
        dv_ref[...] = Gv * mult * ig
        dla = da_ref[...] * a - dmult * (a * a) / jnp.where(start, 1.0, mult_raw)
        dr = dla * (-LRU_C) * sp
        dpr = dr * r * (1.0 - r)
        dpi = di * ig * (1.0 - ig)
        dpr_ref[...] = dpr.astype(BF16)
        dpi_ref[...] = dpi.astype(BF16)
        p_sp = jnp.sum(dla * (-LRU_C) * r, axis=0, keepdims=True) * (-_sig(-lam_ref[...]))
        p_a = jnp.sum(dpr, axis=0, keepdims=True)
        p_x = jnp.sum(dpi, axis=0, keepdims=True)

        @pl.when(pl.program_id(0) == 0)
        def _():
            dsp_ref[...] = p_sp
            dba_ref[...] = p_a
            dbx_ref[...] = p_x

        @pl.when(pl.program_id(0) > 0)
        def _():
            dsp_ref[...] += p_sp
            dba_ref[...] += p_a
            dbx_ref[...] += p_x

    rowb = pl.BlockSpec((tr, C), lambda i: (i, 0))
    vec = pl.BlockSpec((1, C), lambda i: (0, 0))
    ob = jax.ShapeDtypeStruct((T, C), BF16)
    ov = jax.ShapeDtypeStruct((1, C), F32)
    return pl.pallas_call(body, grid=(T // tr,), in_specs=[rowb, rowb, rowb, rowb, rowb, vec],
                          out_specs=(rowb, rowb, rowb, vec, vec, vec),
                          out_shape=(ob, ob, jax.ShapeDtypeStruct((T, C), F32), ov, ov, ov),
                          compiler_params=_cp("arbitrary"), name=name)(v, r, ig, Gb, da, lam)


def _gates_dgrad(dpr, dpi, wa, wx, dv_direct, *, name, tm=512):
    T, C = dpr.shape
    G, gw, _ = wa.shape
    tm = min(tm, T)

    def body(dpr_ref, dpi_ref, wa_ref, wx_ref, dvd_ref, dv_ref):
        dn = (((1,), (1,)), ((), ()))
        dv_ref[...] = (dvd_ref[...]
                       + lax.dot_general(dpr_ref[...], wa_ref[...], dn, preferred_element_type=F32)
                       + lax.dot_general(dpi_ref[...], wx_ref[...], dn, preferred_element_type=F32))

    blk = pl.BlockSpec((tm, gw), lambda i, g: (i, g))
    wsp = pl.BlockSpec((None, gw, gw), lambda i, g: (g, 0, 0))
    return pl.pallas_call(body, grid=(T // tm, G), in_specs=[blk, blk, wsp, wsp, blk], out_specs=blk,
                          out_shape=jax.ShapeDtypeStruct((T, C), F32),
                          compiler_params=_cp("parallel", "parallel"), name=name)(dpr, dpi, wa, wx, dv_direct)


def _gates_wgrad(v, dpr, dpi, *, G, name, tk=512):
    T, C = v.shape
    gw = C // G
    hd = gw // HEADS_PER_GROUP
    tk = min(tk, T)
    nk = T // tk

    def body(v_ref, dpr_ref, dpi_ref, dwa_ref, dwx_ref, acc_a, acc_x):
        dn = (((0,), (0,)), ((), ()))
        vb = v_ref[...].astype(BF16)
        pa = lax.dot_general(vb, dpr_ref[...], dn, preferred_element_type=F32)
        px = lax.dot_general(vb, dpi_ref[...], dn, preferred_element_type=F32)

        @pl.when(pl.program_id(1) == 0)
        def _():
            acc_a[...] = pa
            acc_x[...] = px

        @pl.when(pl.program_id(1) > 0)
        def _():
            acc_a[...] += pa
            acc_x[...] += px

        @pl.when(pl.program_id(1) == nk - 1)
        def _():
            for h in range(HEADS_PER_GROUP):
                dwa_ref[h] = acc_a[h * hd:(h + 1) * hd, h * hd:(h + 1) * hd]
                dwx_ref[h] = acc_x[h * hd:(h + 1) * hd, h * hd:(h + 1) * hd]

    blk = pl.BlockSpec((tk, gw), lambda g, k: (k, g))
    wsp = pl.BlockSpec((HEADS_PER_GROUP, hd, hd), lambda g, k: (g, 0, 0))
    o = jax.ShapeDtypeStruct((G * HEADS_PER_GROUP, hd, hd), F32)
    return pl.pallas_call(body, grid=(G, nk), in_specs=[blk, blk, blk], out_specs=(wsp, wsp), out_shape=(o, o),
                          scratch_shapes=[pltpu.VMEM((gw, gw), F32), pltpu.VMEM((gw, gw), F32)],
                          compiler_params=_cp("parallel", "arbitrary"), name=name)(v, dpr, dpi)


def _group_weights(w):
    H, hd, _ = w.shape
    G = H // HEADS_PER_GROUP
    eye = jnp.eye(HEADS_PER_GROUP, dtype=w.dtype)
    wg = jnp.einsum("ghij,hk->ghikj", w.reshape(G, HEADS_PER_GROUP, hd, hd), eye)
    return wg.reshape(G, HEADS_PER_GROUP * hd, HEADS_PER_GROUP * hd).astype(BF16)


def _layer_fwd(x, p, *, S, fetch=None):
    D = x.shape[1]
    Dc = p["conv_a_b"].shape[1]
    Dr = p["conv_b_b"].shape[1]
    offs = dict(va=0, ga=Dc, xb=2 * Dc, gb=2 * Dc + Dr, sa=2 * Dc + 2 * Dr, sb=2 * Dc + 2 * Dr + D)
    h = _rms_fwd(x, p["g_mix"], name="rms_mix_fwd")
    if fetch is not None:
        fetch("in", h)
    z = _mm_nn(h, p["w_in"], tm=1024, tn=p["w_in"].shape[2], bias=p["b_in"], a_resident=True, name="mm_in_fwd")
    if fetch is not None:
        fetch("mix", z)
    u1 = _conv_fwd(z, p["conv_a_w"], p["conv_a_b"], S=S, off_v=offs["va"], off_g=offs["ga"], name="conv_a_fwd")
    u2 = _ln_silu_fwd(u1, p["ln_g"], p["ln_b"], name="ln_silu_fwd")
    ya = _mm_nn(u2, p["w_a_out"], tm=1024, tn=1024, name="mm_a_out_fwd")
    v0 = _conv_fwd(z, p["conv_b_w"], p["conv_b_b"], S=S, off_v=offs["xb"], off_g=None, name="conv_b_fwd")
    r, ig, a, bt = _gates_fwd(v0, p["wg_a"], p["wg_x"], p["b_rg_a"], p["b_rg_x"], p["lam"], S=S, name="gates_fwd")
    hs, hb = _scan_fwd(a, bt, z, S=S, off_gate=offs["gb"], name="scan_fwd")
    yb = _mm_nn(hb, p["w_b_out"], tm=1024, tn=1024, name="mm_b_out_fwd")
    m = _merge_fwd(z, ya, yb, off_sa=offs["sa"], off_sb=offs["sb"], name="merge_fwd")
    x_mid = _mm_nn(m, p["w_o"], tm=1024, tn=1024, resid=x, name="mm_o_fwd")
    h2 = _rms_fwd(x_mid, p["g_mlp"], name="rms_mlp_fwd")
    if fetch is not None:
        fetch("mlp", h2)
    f_pre, f = _mm_nn(h2, p["w_1"], tm=1024, tn=p["w_1"].shape[2], relu2=True, a_resident=True, name="mm_1_fwd")
    x_next = _mm_nn(f, p["w_2"], tm=512, tn=1024, resid=x_mid, name="mm_2_fwd")
    saved = dict(x=x, h=h, z=z, u1=u1, u2=u2, ya=ya, v0=v0, r=r, ig=ig, a=a, hs=hs, hb=hb, yb=yb, m=m,
                 x_mid=x_mid, h2=h2, f_pre=f_pre, f=f, offs=offs)
    return x_next, saved


def _layer_bwd_mlp(dx, p, sv, *, wdt, dep=None):
    g = {}
    g["w_2"] = _mm_tn(sv["f"], dx, tm=2048, tn=1024, tk=1024, out_dtype=wdt, name="mm_2_wgrad")
    dfp = _mm_nt(dx, p["w_2"], tm=1024, tn=1024, tk=1024, mul_relu=sv["f_pre"], out_dtype=BF16, dep=dep,
                 name="mm_2_dgrad")
    g["w_1"] = _mm_tn(sv["h2"], dfp, tm=1024, tn=512, tk=1024, out_blocks=p["w_1"].shape[0], out_dtype=wdt,
                      name="mm_1_wgrad")
    dh2 = _mm_nt(dfp, p["w_1"], tm=1024, tn=1024, tk=512, name="mm_1_dgrad")
    dx_mid, g["g_mlp"] = _rms_bwd(sv["x_mid"], p["g_mlp"], dh2, dx, name="rms_mlp_bwd")
    return dx_mid, g


def _layer_bwd_mix(dx_mid, p, sv, *, S, wdt, dep=None, on_weight_grads=None):
    offs = sv["offs"]
    g = {}
    g["w_o"] = _mm_tn(sv["m"], dx_mid, tm=1024, tn=1024, tk=1024, out_dtype=wdt, name="mm_o_wgrad")
    dm = _mm_nt(dx_mid, p["w_o"], tm=1024, tn=1024, tk=1024, dep=dep, name="mm_o_dgrad")
    dya, dyb, dsa, dsb = _merge_bwd(sv["z"], sv["ya"], sv["yb"], dm, off_sa=offs["sa"], off_sb=offs["sb"], name="merge_bwd")
    g["w_b_out"] = _mm_tn(sv["hb"], dyb, tm=1536, tn=1024, tk=1024, out_dtype=wdt, name="mm_b_out_wgrad")
    dhb = _mm_nt(dyb, p["w_b_out"], tm=1024, tn=1536, tk=1024, name="mm_b_out_dgrad")
    Gb, da, dgb = _scan_bwd(sv["a"], sv["hs"], sv["z"], dhb, S=S, off_gate=offs["gb"], name="scan_bwd")
    dpr, dpi, dv_direct, g["lam"], g["b_rg_a"], g["b_rg_x"] = _gates_bwd(sv["v0"], sv["r"], sv["ig"], Gb, da, p["lam"], S=S,
                                                                    name="gates_bwd")
    g["w_rg_a"], g["w_rg_x"] = _gates_wgrad(sv["v0"], dpr, dpi, G=p["wg_a"].shape[0], name="gates_wgrad")
    dv0 = _gates_dgrad(dpr, dpi, p["wg_a"], p["wg_x"], dv_direct, name="gates_dgrad")
    dxb, g["conv_b_w"], g["conv_b_b"] = _conv_bwd(sv["z"], p["conv_b_w"], dv0, S=S, off_v=offs["xb"], off_g=None,
                                                  name="conv_b_bwd")
    g["w_a_out"] = _mm_tn(sv["u2"], dya, tm=1024, tn=1024, tk=1024, out_dtype=wdt, name="mm_a_out_wgrad")
    du2 = _mm_nt(dya, p["w_a_out"], tm=1024, tn=1024, tk=1024, name="mm_a_out_dgrad")
    du1, g["ln_g"], g["ln_b"] = _ln_silu_bwd(sv["u1"], p["ln_g"], p["ln_b"], du2, name="ln_silu_bwd")
    dva, dga, g["conv_a_w"], g["conv_a_b"] = _conv_bwd(sv["z"], p["conv_a_w"], du1, S=S, off_v=offs["va"],
                                                       off_g=offs["ga"], name="conv_a_bwd")
    dz = jnp.concatenate([dva, dga, dxb, dgb, dsa, dsb], axis=1)
    g["w_in"], db_in = _mm_tn(sv["h"], dz, tm=1024, tn=512, tk=1024, out_blocks=p["w_in"].shape[0], colsum=True,
                              out_dtype=wdt, name="mm_in_wgrad")
    g["b_in"] = db_in[:1]
    dep_in = on_weight_grads(g) if on_weight_grads is not None else None
    dh = _mm_nt(dz, p["w_in"], tm=1024, tn=1024, tk=512, dep=dep_in, name="mm_in_dgrad")
    dx_in, g["g_mix"] = _rms_bwd(sv["x"], p["g_mix"], dh, dx_mid, name="rms_mix_bwd")
    return dx_in, g


def _layer_bwd(dx, p, sv, *, S, wdt=F32):
    dx_mid, g = _layer_bwd_mlp(dx, p, sv, wdt=wdt)
    dx_in, g2 = _layer_bwd_mix(dx_mid, p, sv, S=S, wdt=wdt)
    g.update(g2)
    return dx_in, g


def _local_step(x, tgt, layers, g_final, *, S, wdt=F32):
    saved = []
    for p in layers:
        x, sv = _layer_fwd(x, p, S=S)
        saved.append(sv)
    loss, dx, dg_final = _final_loss(x, g_final, tgt, name="final_loss")
    grads = [None] * len(layers)
    for l in reversed(range(len(layers))):
        dx, grads[l] = _layer_bwd(dx, layers[l], saved[l], S=S, wdt=wdt)
    return loss, dx, grads, dg_final


_HBM = pl.BlockSpec(memory_space=pltpu.HBM)
_MESH = pl.DeviceIdType.MESH


_SEM = pl.BlockSpec(memory_space=pltpu.SEMAPHORE)
_ANY = pl.BlockSpec(memory_space=pl.ANY)
_FLIPS = [(dx, dy, dc) for dx in (0, 1) for dy in (0, 1) for dc in (0, 1)][1:]


def _place(shard, me_idx, dtype, *, name):
    r, cc = shard.shape
    tr = 512 if r % 512 == 0 else r

    def body(me_ref, s_ref, o_ref):
        del me_ref
        o_ref[...] = s_ref[...].astype(dtype)

    grid_spec = pltpu.PrefetchScalarGridSpec(
        num_scalar_prefetch=1, grid=(r // tr,),
        in_specs=[pl.BlockSpec((tr, cc), lambda i, me: (i, 0))],
        out_specs=pl.BlockSpec((None, tr, cc), lambda i, me: (me[0], i, 0)))
    return pl.pallas_call(body, grid_spec=grid_spec, out_shape=jax.ShapeDtypeStruct((N_DEV, r, cc), dtype),
                          compiler_params=_cp("arbitrary"), name=name)(me_idx, shard)


def _exchange_copies(srcs, lands, send_sems, recv_sems):
    x, y, c = lax.axis_index("x"), lax.axis_index("y"), lax.axis_index("c")
    me = 4 * x + 2 * y + c
    pairs = []
    for k, (dx, dy, dc) in enumerate(_FLIPS):
        peer = (1 - x if dx else x, 1 - y if dy else y, 1 - c if dc else c)
        pidx = 4 * peer[0] + 2 * peer[1] + peer[2]
        for a, land in enumerate(lands):
            src = land.at[me] if srcs is None else srcs[a].at[pidx]
            sem = k * len(lands) + a
            out = pltpu.make_async_remote_copy(src_ref=src, dst_ref=land.at[me], send_sem=send_sems.at[sem],
                                               recv_sem=recv_sems.at[sem], device_id=peer, device_id_type=_MESH)
            arrival = pltpu.make_async_remote_copy(src_ref=src, dst_ref=land.at[pidx], send_sem=send_sems.at[sem],
                                                   recv_sem=recv_sems.at[sem], device_id=peer, device_id_type=_MESH)
            pairs.append((out, arrival))
    return pairs


def _exchange_start(srcs, lands, *, name):
    n = len(lands)
    bufs = list(lands) if srcs is None else list(srcs) + list(lands)
    nb = len(bufs)

    def body(*refs):
        ins = refs[:nb]
        send_sems, recv_sems = refs[nb], refs[nb + 1]
        token = refs[-1]
        for out, _ in _exchange_copies(None if srcs is None else ins[:n], ins[nb - n:], send_sems, recv_sems):
            out.start()
        token[...] = jnp.zeros_like(token)

    sems = pltpu.SemaphoreType.DMA((len(_FLIPS) * n,))
    res = pl.pallas_call(
        body, name=name, in_specs=[_HBM] * nb,
        out_shape=(sems, sems, *[pltpu.HBM(b.shape, b.dtype) for b in bufs], jax.ShapeDtypeStruct((SUBLANES, 128), F32)),
        out_specs=(_SEM, _SEM, *[_HBM] * nb, pl.BlockSpec(memory_space=pltpu.VMEM)),
        input_output_aliases={i: 2 + i for i in range(nb)},
        compiler_params=pltpu.CompilerParams(has_side_effects=pltpu.SideEffectType.DATAFLOW_SIDE_EFFECTING),
    )(*[pltpu.with_memory_space_constraint(b, pltpu.HBM) for b in bufs])
    return res[0], res[1], list(res[2:2 + nb]), res[-1]


def _exchange_wait(send_sems, recv_sems, bufs, *, scatter, after, name):
    nb = len(bufs)
    n = nb // 2 if scatter else nb

    def body(*refs):
        ins = refs[:nb]
        for out, arrival in _exchange_copies(ins[:n] if scatter else None, ins[nb - n:], refs[nb], refs[nb + 1]):
            out.wait_send()
            arrival.wait_recv()

    extra = [] if after is None else [after]
    res = pl.pallas_call(
        body, name=name, in_specs=[_HBM] * nb + [_SEM, _SEM] + [_ANY] * len(extra),
        out_shape=tuple(pltpu.HBM(b.shape, b.dtype) for b in bufs), out_specs=tuple([_HBM] * nb),
        input_output_aliases={i: i for i in range(nb)},
        compiler_params=pltpu.CompilerParams(has_side_effects=pltpu.SideEffectType.DATAFLOW_SIDE_EFFECTING),
    )(*bufs, send_sems, recv_sems, *extra)
    return list(res)


def _adamw_math(w, g, m, v):
    m = ADAM_B1 * m + (1.0 - ADAM_B1) * g
    v = ADAM_B2 * v + (1.0 - ADAM_B2) * (g * g)
    m_hat = m / (1.0 - ADAM_B1 ** ADAM_STEP)
    v_hat = v / (1.0 - ADAM_B2 ** ADAM_STEP)
    delta = -ADAM_LR * (m_hat / (jnp.sqrt(v_hat) + ADAM_EPS) + ADAM_WD * w)
    return delta, m, v


def _adamw(w, parts, m, v, *, name):
    r, cc = w.shape
    P = parts.shape[0]
    tr = 512 if r % 512 == 0 else r

    def body(w_ref, p_ref, m_ref, v_ref, g_ref, d_ref, nm_ref, nv_ref):
        g = p_ref[0]
        for q in range(1, P):
            g = g + p_ref[q]
        d, nm, nv = _adamw_math(w_ref[...], g, m_ref[...], v_ref[...])
        g_ref[...] = g
        d_ref[...] = d
        nm_ref[...] = nm
        nv_ref[...] = nv

    blk = pl.BlockSpec((tr, cc), lambda i: (i, 0))
    o = jax.ShapeDtypeStruct((r, cc), F32)
    return pl.pallas_call(body, grid=(r // tr,),
                          in_specs=[blk, pl.BlockSpec((P, tr, cc), lambda i: (0, i, 0)), blk, blk],
                          out_specs=(blk, blk, blk, blk), out_shape=(o, o, o, o),
                          compiler_params=_cp("parallel"), name=name)(w, parts, m, v)


def _adamw_scattered(w, own, land, m, v, me_idx, *, name):
    r, cc = w.shape
    tr = 512 if r % 512 == 0 else r

    def body(me_ref, w_ref, own_ref, land_ref, m_ref, v_ref, g_ref, d_ref, nm_ref, nv_ref):
        me = me_ref[0]
        g = own_ref[...].astype(F32)
        for q in range(N_DEV):
            g = g + jnp.where(q == me, 0.0, land_ref[q].astype(F32))
        d, nm, nv = _adamw_math(w_ref[...], g, m_ref[...], v_ref[...])
        g_ref[...] = g
        d_ref[...] = d
        nm_ref[...] = nm
        nv_ref[...] = nv

    blk = pl.BlockSpec((tr, cc), lambda i, me: (i, 0))
    o = jax.ShapeDtypeStruct((r, cc), F32)
    grid_spec = pltpu.PrefetchScalarGridSpec(
        num_scalar_prefetch=1, grid=(r // tr,),
        in_specs=[blk, pl.BlockSpec((None, tr, cc), lambda i, me: (me[0], i, 0)),
                  pl.BlockSpec((N_DEV, tr, cc), lambda i, me: (0, i, 0)), blk, blk],
        out_specs=(blk, blk, blk, blk))
    return pl.pallas_call(body, grid_spec=grid_spec, out_shape=(o, o, o, o), compiler_params=_cp("parallel"),
                          name=name)(me_idx, w, own, land, m, v)


_SHARDED = ("w_in", "conv_a_w", "w_a_out", "conv_b_w", "w_b_out", "w_o", "w_1", "w_2")
_COL_SHARDED = ("w_in", "conv_a_w", "conv_b_w", "w_1")
_REPLICATED = ("g_mix", "b_in", "conv_a_b", "ln_g", "ln_b", "conv_b_b", "w_rg_a", "b_rg_a", "w_rg_x", "b_rg_x", "lam",
               "g_mlp")
_WEIGHTS = ("g_mix", "w_in", "b_in", "conv_a_w", "conv_a_b", "ln_g", "ln_b", "w_a_out", "conv_b_w", "conv_b_b", "w_rg_a",
            "b_rg_a", "w_rg_x", "b_rg_x", "lam", "w_b_out", "w_o", "g_mlp", "w_1", "w_2", "g_final")
_LANES = 128


def _cols_from_blocks(b):
    nb, K, n = b.shape
    return b.transpose(1, 0, 2).reshape(K, nb * n)


def _blocks_from_cols(w, K):
    n = w.shape[1] // N_DEV
    return w[:K].reshape(K, N_DEV, n).transpose(1, 0, 2)


def kernel(x, g_mix, w_in, b_in, conv_a_w, conv_a_b, ln_g, ln_b, w_a_out, conv_b_w, conv_b_b, w_rg_a, b_rg_a, w_rg_x, b_rg_x, lam, w_b_out, w_o, g_mlp, w_1, w_2, g_final, loss_target, m_g_mix, m_w_in, m_b_in, m_conv_a_w, m_conv_a_b, m_ln_g, m_ln_b, m_w_a_out, m_conv_b_w, m_conv_b_b, m_w_rg_a, m_b_rg_a, m_w_rg_x, m_b_rg_x, m_lam, m_w_b_out, m_w_o, m_g_mlp, m_w_1, m_w_2, m_g_final, v_g_mix, v_w_in, v_b_in, v_conv_a_w, v_conv_a_b, v_ln_g, v_ln_b, v_w_a_out, v_conv_b_w, v_conv_b_b, v_w_rg_a, v_b_rg_a, v_w_rg_x, v_b_rg_x, v_lam, v_w_b_out, v_w_o, v_g_mlp, v_w_1, v_w_2, v_g_final):
    W = dict(g_mix=g_mix, w_in=w_in, b_in=b_in, conv_a_w=conv_a_w, conv_a_b=conv_a_b, ln_g=ln_g, ln_b=ln_b,
             w_a_out=w_a_out, conv_b_w=conv_b_w, conv_b_b=conv_b_b, w_rg_a=w_rg_a, b_rg_a=b_rg_a, w_rg_x=w_rg_x,
             b_rg_x=b_rg_x, lam=lam, w_b_out=w_b_out, w_o=w_o, g_mlp=g_mlp, w_1=w_1, w_2=w_2, g_final=g_final)
    M = dict(g_mix=m_g_mix, w_in=m_w_in, b_in=m_b_in, conv_a_w=m_conv_a_w, conv_a_b=m_conv_a_b, ln_g=m_ln_g, ln_b=m_ln_b,
             w_a_out=m_w_a_out, conv_b_w=m_conv_b_w, conv_b_b=m_conv_b_b, w_rg_a=m_w_rg_a, b_rg_a=m_b_rg_a,
             w_rg_x=m_w_rg_x, b_rg_x=m_b_rg_x, lam=m_lam, w_b_out=m_w_b_out, w_o=m_w_o, g_mlp=m_g_mlp, w_1=m_w_1,
             w_2=m_w_2, g_final=m_g_final)
    V = dict(g_mix=v_g_mix, w_in=v_w_in, b_in=v_b_in, conv_a_w=v_conv_a_w, conv_a_b=v_conv_a_b, ln_g=v_ln_g, ln_b=v_ln_b,
             w_a_out=v_w_a_out, conv_b_w=v_conv_b_w, conv_b_b=v_conv_b_b, w_rg_a=v_w_rg_a, b_rg_a=v_b_rg_a,
             w_rg_x=v_w_rg_x, b_rg_x=v_b_rg_x, lam=v_lam, w_b_out=v_w_b_out, w_o=v_w_o, g_mlp=v_g_mlp, w_1=v_w_1,
             w_2=v_w_2, g_final=v_g_final)
    NB, S, D = x.shape
    L = g_mix.shape[0]
    hd = w_rg_a.shape[-1]
    me_idx = (4 * lax.axis_index("x") + 2 * lax.axis_index("y") + lax.axis_index("c")).astype(jnp.int32).reshape(1)

    stages = (("in", ("w_in",)), ("mix", ("conv_a_w", "w_a_out", "conv_b_w", "w_b_out", "w_o")), ("mlp", ("w_1", "w_2")))
    gathers = {}
    started = jnp.zeros((), F32)
    for l in range(L):
        for stage, names in stages:
            lands = [_place(W[k][l], me_idx, F32 if k.startswith("conv") else BF16, name="place_" + k) for k in names]
            send_sems, recv_sems, bufs, token = _exchange_start(None, lands, name=f"weights_start_{stage}_{l}")
            gathers[l, stage] = (names, send_sems, recv_sems, bufs)
            started = started + token[0, 0]

    xt = x.reshape(NB * S, D)
    layers, saved = [], []
    for l in range(L):
        p = {k: W[k][l][None] for k in ("g_mix", "b_in", "conv_a_b", "ln_g", "ln_b", "conv_b_b", "b_rg_a", "b_rg_x",
                                         "lam", "g_mlp")}
        if l == 0:
            p["g_mix"] = p["g_mix"] + started
        p["wg_a"] = _group_weights(w_rg_a[l])
        p["wg_x"] = _group_weights(w_rg_x[l])

        def fetch(stage, after, l=l, p=p):
            names, send_sems, recv_sems, bufs = gathers[l, stage]
            bufs = _exchange_wait(send_sems, recv_sems, bufs, scatter=False, after=after,
                                  name=f"weights_wait_{stage}_{l}")
            for k, full in zip(names, bufs):
                if k in ("w_in", "w_1"):
                    p[k] = full
                elif k in _COL_SHARDED:
                    p[k] = _cols_from_blocks(full)
                else:
                    p[k] = full.reshape(-1, full.shape[-1])

        layers.append(p)
        xt, sv = _layer_fwd(xt, p, S=S, fetch=fetch)
        saved.append(sv)
    loss, dx, dg_final = _final_loss(xt, g_final[None], loss_target.reshape(NB * S, D), name="final_loss")

    per_layer = {k: [None] * L for k in _SHARDED}

    def scatter_start(l, names, g, tag):
        srcs = []
        for k in names:
            shard_shape = W[k].shape[1:]
            if k in ("w_in", "w_1"):
                srcs.append(g[k])
            elif k in _COL_SHARDED:
                srcs.append(_blocks_from_cols(g[k], shard_shape[0]).astype(BF16))
            else:
                srcs.append(g[k].reshape((N_DEV,) + shard_shape))
        lands = [lax.empty(s.shape, BF16) for s in srcs]
        send_sems, recv_sems, bufs, token = _exchange_start(srcs, lands, name=f"grads_start_{tag}_{l}")
        return (names, send_sems, recv_sems, bufs, tag), token

    def scatter_finish(l, flight, after):
        names, send_sems, recv_sems, bufs, tag = flight
        bufs = _exchange_wait(send_sems, recv_sems, bufs, scatter=True, after=after, name=f"grads_wait_{tag}_{l}")
        n = len(names)
        for k, own, land in zip(names, bufs[:n], bufs[n:]):
            per_layer[k][l] = _adamw_scattered(W[k][l], own, land, M[k][l], V[k][l], me_idx, name="adamw_" + k)

    gate_names = ("w_rg_a", "w_rg_x")
    gate_out = {k: [None] * L for k in gate_names}

    def small_start(l):
        lands = [_place(grads[l][k].reshape(-1, hd), me_idx, F32, name="place_gate_grad") for k in gate_names]
        send_sems, recv_sems, bufs, token = _exchange_start(None, lands, name=f"small_start_{l}")
        return (send_sems, recv_sems, bufs), token

    def small_finish(l, flight, after):
        send_sems, recv_sems, bufs = flight
        bufs = _exchange_wait(send_sems, recv_sems, bufs, scatter=False, after=after, name=f"small_wait_{l}")
        for k, g_all in zip(gate_names, bufs):
            gate_out[k][l] = _adamw(W[k][l].reshape(-1, hd), g_all, M[k][l].reshape(-1, hd), V[k][l].reshape(-1, hd),
                                    name="adamw_gate")

    grads = [None] * L
    in_flight = []
    dep = None
    for l in reversed(range(L)):
        dx_mid, g = _layer_bwd_mlp(dx, layers[l], saved[l], wdt=BF16, dep=dep)
        f_mlp, dep = scatter_start(l, ("w_2", "w_1"), g, "mlp")
        flights = [f_mlp]

        def on_weight_grads(g_part, l=l, flights=flights):
            f_mix, token = scatter_start(l, ("w_o", "w_b_out", "w_a_out", "conv_a_w", "conv_b_w", "w_in"), g_part, "mix")
            flights.append(f_mix)
            return token

        dx, g_mix_part = _layer_bwd_mix(dx_mid, layers[l], saved[l], S=S, wdt=BF16, dep=dep,
                                        on_weight_grads=on_weight_grads)
        g.update(g_mix_part)
        grads[l] = g
        f_small, dep = small_start(l)
        for l_prev, fs, f_sm in in_flight:
            for flight in fs:
                scatter_finish(l_prev, flight, dx)
            small_finish(l_prev, f_sm, dx)
        in_flight = [(l, flights, f_small)]
    for l_prev, fs, f_sm in in_flight:
        for flight in fs:
            scatter_finish(l_prev, flight, None)
        small_finish(l_prev, f_sm, None)
    out_g, out_d, out_m, out_v = {}, {}, {}, {}
    for k in _SHARDED:
        out_g[k], out_d[k], out_m[k], out_v[k] = (jnp.stack([per_layer[k][l][i] for l in range(L)]) for i in range(4))
    for i, res in enumerate((out_g, out_d, out_m, out_v)):
        for k in gate_names:
            res[k] = jnp.stack([gate_out[k][l][i] for l in range(L)]).reshape(W[k].shape)

    vec_names = tuple(k for k in _REPLICATED if k not in gate_names)
    n_vec = sum(W[k].shape[1] for k in vec_names)

    def vec_pack(rows, final, last):
        tail = jnp.concatenate([final.reshape(1, -1), jnp.broadcast_to(last.reshape(1, 1), (1, _LANES))], axis=1)
        tail = jnp.pad(tail, ((0, SUBLANES - L - 1), (0, n_vec - tail.shape[1])))
        return jnp.concatenate([rows, tail], axis=0)

    zero = jnp.zeros((1,), F32)
    g_rows = jnp.concatenate([jnp.concatenate([grads[l][k] for k in vec_names], axis=1) for l in range(L)], axis=0)
    land = _place(vec_pack(g_rows, dg_final, loss[0, :1]), me_idx, F32, name="place_vectors")
    send_sems, recv_sems, bufs, _ = _exchange_start(None, [land], name="vectors_start")
    (g_all,) = _exchange_wait(send_sems, recv_sems, bufs, scatter=False, after=None, name="vectors_wait")
    vec_out = _adamw(vec_pack(jnp.concatenate([W[k] for k in vec_names], axis=1), g_final, zero), g_all,
                     vec_pack(jnp.concatenate([M[k] for k in vec_names], axis=1), m_g_final, zero),
                     vec_pack(jnp.concatenate([V[k] for k in vec_names], axis=1), v_g_final, zero), name="adamw_vectors")
    for res, arr in zip((out_g, out_d, out_m, out_v), vec_out):
        off = 0
        for k in vec_names:
            res[k] = arr[:L, off:off + W[k].shape[1]]
            off += W[k].shape[1]
        res["g_final"] = arr[L, :g_final.size]
    loss_out = vec_out[0][L, g_final.size]

    return (loss_out, dx.reshape(NB, S, D), *[out_g[k] for k in _WEIGHTS], *[out_d[k] for k in _WEIGHTS],
            *[out_m[k] for k in _WEIGHTS], *[out_v[k] for k in _WEIGHTS])
```

```python
import functools

import jax
import jax.numpy as jnp
from jax import lax
from jax.experimental import pallas as pl
from jax.experimental.pallas import tpu as pltpu

F32 = jnp.float32
BF16 = jnp.bfloat16

EPS = 1e-6
LRU_C = 8.0
N_RNN_HEADS = 16
HEADS_PER_GROUP = 4
N_DEV = 8
ADAM_LR, ADAM_B1, ADAM_B2, ADAM_EPS, ADAM_WD, ADAM_STEP = 0.001, 0.9, 0.999, 1e-08, 0.01, 10

VMEM_LIMIT_BYTES = 48 * 1024 * 1024
CONV_PAD = 32
CONV_CHUNK = 128
SUBLANES = 8


def _cp(*sem):
    return pltpu.CompilerParams(dimension_semantics=sem, vmem_limit_bytes=VMEM_LIMIT_BYTES)


def _sig(x):
    return 1.0 / (1.0 + jnp.exp(-x))


def _gelu(x):
    c = 0.7978845608028654
    return 0.5 * x * (1.0 + jnp.tanh(c * (x + 0.044715 * x * x * x)))


def _gelu_grad(x):
    c = 0.7978845608028654
    th = jnp.tanh(c * (x + 0.044715 * x * x * x))
    return 0.5 * (1.0 + th) + 0.5 * x * (1.0 - th * th) * c * (1.0 + 3.0 * 0.044715 * x * x)


def _mm_nn(a, b, *, tm, tn, name, bias=None, resid=None, relu2=False, out_dtype=F32, a_resident=False):
    M, K = a.shape
    blocked = b.ndim == 3
    N = b.shape[0] * b.shape[2] if blocked else b.shape[1]
    tm = min(tm, M)
    tn = min(tn, N)
    if blocked:
        assert tn == b.shape[2]
    n_extra = (bias is not None) + (resid is not None)

    def body(*refs):
        acc = jnp.dot(refs[0][...].astype(BF16), refs[1][...].astype(BF16), preferred_element_type=F32)
        k = 2
        if bias is not None:
            acc = acc + refs[k][...]
            k += 1
        if resid is not None:
            acc = acc + refs[k][...]
            k += 1
        if relu2:
            refs[k][...] = acc
            p = jnp.maximum(acc, 0.0)
            refs[k + 1][...] = (p * p).astype(BF16)
        else:
            refs[k][...] = acc.astype(out_dtype)

    def spec(shape, index):
        return pl.BlockSpec(shape, (lambda i, j: index(j, i)) if a_resident else index)

    in_specs = [spec((tm, K), lambda j, i: (i, 0))]
    if blocked:
        in_specs.append(spec((None, K, tn), lambda j, i: (j, 0, 0)))
    else:
        in_specs.append(spec((K, tn), lambda j, i: (0, j)))
    args = [a, b]
    if bias is not None:
        in_specs.append(spec((1, tn), lambda j, i: (0, j)))
        args.append(bias)
    if resid is not None:
        in_specs.append(spec((tm, tn), lambda j, i: (i, j)))
        args.append(resid)
    o_spec = spec((tm, tn), lambda j, i: (i, j))
    if relu2:
        out_shape = (jax.ShapeDtypeStruct((M, N), F32), jax.ShapeDtypeStruct((M, N), BF16))
        out_specs = (o_spec, o_spec)
    else:
        out_shape = jax.ShapeDtypeStruct((M, N), out_dtype)
        out_specs = o_spec
    del n_extra
    grid = (M // tm, N // tn) if a_resident else (N // tn, M // tm)
    return pl.pallas_call(body, grid=grid, in_specs=in_specs, out_specs=out_specs,
                          out_shape=out_shape, compiler_params=_cp("parallel", "parallel"), name=name)(*args)


def _mm_nt(a, b, *, tm, tn, tk, name, mul_relu=None, resid=None, out_dtype=F32, dep=None):
    M, N = a.shape
    blocked = b.ndim == 3
    Kout = b.shape[1] if blocked else b.shape[0]
    tm = min(tm, M)
    tn = min(tn, Kout)
    tk = b.shape[2] if blocked else min(tk, N)
    nk = N // tk

    def body(*refs):
        acc_ref = refs[-1]
        kk = pl.program_id(2)
        part = lax.dot_general(refs[0][...].astype(BF16), refs[1][...].astype(BF16),
                               (((1,), (1,)), ((), ())), preferred_element_type=F32)

        @pl.when(kk == 0)
        def _():
            acc_ref[...] = part

        @pl.when(kk > 0)
        def _():
            acc_ref[...] += part

        @pl.when(kk == nk - 1)
        def _():
            acc = acc_ref[...]
            k = 2
            if mul_relu is not None:
                acc = acc * (2.0 * jnp.maximum(refs[k][...], 0.0))
                k += 1
            if resid is not None:
                acc = acc + refs[k][...]
                k += 1
            if dep is not None:
                k += 1
            refs[k][...] = acc.astype(out_dtype)

    in_specs = [pl.BlockSpec((tm, tk), lambda i, j, k: (i, k))]
    if blocked:
        in_specs.append(pl.BlockSpec((None, tn, tk), lambda i, j, k: (k, j, 0)))
    else:
        in_specs.append(pl.BlockSpec((tn, tk), lambda i, j, k: (j, k)))
    args = [a, b]
    for extra in (mul_relu, resid):
        if extra is not None:
            in_specs.append(pl.BlockSpec((tm, tn), lambda i, j, k: (i, j)))
            args.append(extra)
    if dep is not None:
        in_specs.append(pl.BlockSpec(dep.shape, lambda i, j, k: (0, 0)))
        args.append(dep)
    return pl.pallas_call(body, grid=(M // tm, Kout // tn, nk), in_specs=in_specs,
                          out_specs=pl.BlockSpec((tm, tn), lambda i, j, k: (i, j)),
                          out_shape=jax.ShapeDtypeStruct((M, Kout), out_dtype),
                          scratch_shapes=[pltpu.VMEM((tm, tn), F32)],
                          compiler_params=_cp("parallel", "parallel", "arbitrary"), name=name)(*args)


def _mm_tn(a, b, *, tm, tn, tk, name, out_blocks=None, colsum=False, out_dtype=F32):
    T, M = a.shape
    N = b.shape[1]
    tm = min(tm, M)
    tk = min(tk, T)
    if out_blocks is not None:
        tn = N // out_blocks
        tm = M
    tn = min(tn, N)
    nk = T // tk
    if colsum:
        assert tm == M

    def body(*refs):
        a_ref, b_ref, o_ref, acc_ref = refs[0], refs[1], refs[2], refs[-1]
        kk = pl.program_id(2)
        bv = b_ref[...]
        part = lax.dot_general(a_ref[...].astype(BF16), bv.astype(BF16),
                               (((0,), (0,)), ((), ())), preferred_element_type=F32)

        if colsum:
            csum = jnp.broadcast_to(jnp.sum(bv.astype(F32), axis=0, keepdims=True), (SUBLANES, tn))

        @pl.when(kk == 0)
        def _():
            acc_ref[...] = part
            if colsum:
                refs[3][...] = csum

        @pl.when(kk > 0)
        def _():
            acc_ref[...] += part
            if colsum:
                refs[3][...] += csum

        @pl.when(kk == nk - 1)
        def _():
            o_ref[...] = acc_ref[...].astype(out_dtype)

    in_specs = [pl.BlockSpec((tk, tm), lambda i, j, k: (k, i)), pl.BlockSpec((tk, tn), lambda i, j, k: (k, j))]
    if out_blocks is not None:
        o_shape = jax.ShapeDtypeStruct((out_blocks, M, tn), out_dtype)
        o_spec = pl.BlockSpec((None, M, tn), lambda i, j, k: (j, 0, 0))
    else:
        o_shape = jax.ShapeDtypeStruct((M, N), out_dtype)
        o_spec = pl.BlockSpec((tm, tn), lambda i, j, k: (i, j))
    if colsum:
        out_shape = (o_shape, jax.ShapeDtypeStruct((SUBLANES, N), F32))
        out_specs = (o_spec, pl.BlockSpec((SUBLANES, tn), lambda i, j, k: (0, j)))
    else:
        out_shape, out_specs = o_shape, o_spec
    return pl.pallas_call(body, grid=(M // tm, N // tn, nk), in_specs=in_specs, out_specs=out_specs,
                          out_shape=out_shape, scratch_shapes=[pltpu.VMEM((tm, tn), F32)],
                          compiler_params=_cp("parallel", "parallel", "arbitrary"), name=name)(a, b)


def _rms_fwd(x, g, *, name, tr=512):
    T, D = x.shape
    tr = min(tr, T)

    def body(x_ref, g_ref, h_ref):
        xv = x_ref[...]
        r = lax.rsqrt(jnp.mean(xv * xv, axis=-1, keepdims=True) + EPS)
        h_ref[...] = (xv * r * g_ref[...]).astype(BF16)

    return pl.pallas_call(body, grid=(T // tr,),
                          in_specs=[pl.BlockSpec((tr, D), lambda i: (i, 0)), pl.BlockSpec((1, D), lambda i: (0, 0))],
                          out_specs=pl.BlockSpec((tr, D), lambda i: (i, 0)),
                          out_shape=jax.ShapeDtypeStruct((T, D), BF16), compiler_params=_cp("parallel"), name=name)(x, g)


def _rms_bwd(x, g, dh, dres, *, name, tr=512):
    T, D = x.shape
    tr = min(tr, T)

    def body(x_ref, g_ref, dh_ref, dres_ref, dx_ref, dg_ref):
        xv = x_ref[...]
        r = lax.rsqrt(jnp.mean(xv * xv, axis=-1, keepdims=True) + EPS)
        n = xv * r
        dh = dh_ref[...]
        dn = dh * g_ref[...]
        dx_ref[...] = dres_ref[...] + r * (dn - n * jnp.mean(dn * n, axis=-1, keepdims=True))
        part = jnp.sum(dh * n, axis=0, keepdims=True)

        @pl.when(pl.program_id(0) == 0)
        def _():
            dg_ref[...] = part

        @pl.when(pl.program_id(0) > 0)
        def _():
            dg_ref[...] += part

    row = pl.BlockSpec((tr, D), lambda i: (i, 0))
    vec = pl.BlockSpec((1, D), lambda i: (0, 0))
    return pl.pallas_call(body, grid=(T // tr,), in_specs=[row, vec, row, row], out_specs=(row, vec),
                          out_shape=(jax.ShapeDtypeStruct((T, D), F32), jax.ShapeDtypeStruct((1, D), F32)),
                          compiler_params=_cp("arbitrary"), name=name)(x, g, dh, dres)


def _final_loss(x, g, tgt, *, name, tr=512):
    T, D = x.shape
    tr = min(tr, T)

    def body(x_ref, g_ref, t_ref, loss_ref, dx_ref, dg_ref):
        xv = x_ref[...]
        gv = g_ref[...]
        r = lax.rsqrt(jnp.mean(xv * xv, axis=-1, keepdims=True) + EPS)
        n = xv * r
        e = n * gv - t_ref[...]
        lpart = 0.5 * jnp.sum(jnp.mean(e * e, axis=-1, keepdims=True), axis=0, keepdims=True)
        dy = e * (1.0 / D)
        dn = dy * gv
        dx_ref[...] = r * (dn - n * jnp.mean(dn * n, axis=-1, keepdims=True))
        gpart = jnp.sum(dy * n, axis=0, keepdims=True)

        @pl.when(pl.program_id(0) == 0)
        def _():
            dg_ref[...] = gpart
            loss_ref[...] = jnp.broadcast_to(lpart, (1, 128))

        @pl.when(pl.program_id(0) > 0)
        def _():
            dg_ref[...] += gpart
            loss_ref[...] += jnp.broadcast_to(lpart, (1, 128))

    row = pl.BlockSpec((tr, D), lambda i: (i, 0))
    vec = pl.BlockSpec((1, D), lambda i: (0, 0))
    return pl.pallas_call(body, grid=(T // tr,), in_specs=[row, vec, row],
                          out_specs=(pl.BlockSpec((1, 128), lambda i: (0, 0)), row, vec),
                          out_shape=(jax.ShapeDtypeStruct((1, 128), F32), jax.ShapeDtypeStruct((T, D), F32),
                                     jax.ShapeDtypeStruct((1, D), F32)),
                          compiler_params=_cp("arbitrary"), name=name)(x, g, tgt)


def _ln_silu_fwd(u, g, b, *, name, tr=512):
    T, C = u.shape
    tr = min(tr, T)

    def body(u_ref, g_ref, b_ref, o_ref):
        uv = u_ref[...]
        mu = jnp.mean(uv, axis=-1, keepdims=True)
        xc = uv - mu
        r = lax.rsqrt(jnp.mean(xc * xc, axis=-1, keepdims=True) + EPS)
        y = xc * r * g_ref[...] + b_ref[...]
        o_ref[...] = (y * _sig(y)).astype(BF16)

    row = pl.BlockSpec((tr, C), lambda i: (i, 0))
    vec = pl.BlockSpec((1, C), lambda i: (0, 0))
    return pl.pallas_call(body, grid=(T // tr,), in_specs=[row, vec, vec], out_specs=row,
                          out_shape=jax.ShapeDtypeStruct((T, C), BF16), compiler_params=_cp("parallel"),
                          name=name)(u, g, b)


def _ln_silu_bwd(u, g, b, do, *, name, tr=512):
    T, C = u.shape
    tr = min(tr, T)

    def body(u_ref, g_ref, b_ref, do_ref, du_ref, dg_ref, db_ref):
        uv = u_ref[...]
        gv = g_ref[...]
        mu = jnp.mean(uv, axis=-1, keepdims=True)
        xc = uv - mu
        r = lax.rsqrt(jnp.mean(xc * xc, axis=-1, keepdims=True) + EPS)
        n = xc * r
        y = n * gv + b_ref[...]
        s = _sig(y)
        dy = do_ref[...] * (s * (1.0 + y * (1.0 - s)))
        dn = dy * gv
        du_ref[...] = r * (dn - jnp.mean(dn, axis=-1, keepdims=True) - n * jnp.mean(dn * n, axis=-1, keepdims=True))
        gpart = jnp.sum(dy * n, axis=0, keepdims=True)
        bpart = jnp.sum(dy, axis=0, keepdims=True)

        @pl.when(pl.program_id(0) == 0)
        def _():
            dg_ref[...] = gpart
            db_ref[...] = bpart

        @pl.when(pl.program_id(0) > 0)
        def _():
            dg_ref[...] += gpart
            db_ref[...] += bpart

    row = pl.BlockSpec((tr, C), lambda i: (i, 0))
    vec = pl.BlockSpec((1, C), lambda i: (0, 0))
    return pl.pallas_call(body, grid=(T // tr,), in_specs=[row, vec, vec, row], out_specs=(row, vec, vec),
                          out_shape=(jax.ShapeDtypeStruct((T, C), F32), jax.ShapeDtypeStruct((1, C), F32),
                                     jax.ShapeDtypeStruct((1, C), F32)),
                          compiler_params=_cp("arbitrary"), name=name)(u, g, b, do)


def _merge_fwd(z, ya, yb, *, off_sa, off_sb, name, tr=512):
    T, D = ya.shape
    tr = min(tr, T)
    assert off_sa % D == 0 and off_sb % D == 0

    def body(sa_ref, sb_ref, ya_ref, yb_ref, m_ref):
        m_ref[...] = (_sig(sa_ref[...]) * ya_ref[...] + _sig(sb_ref[...]) * yb_ref[...]).astype(BF16)

    row = pl.BlockSpec((tr, D), lambda i: (i, 0))
    return pl.pallas_call(body, grid=(T // tr,),
                          in_specs=[pl.BlockSpec((tr, D), lambda i: (i, off_sa // D)),
                                    pl.BlockSpec((tr, D), lambda i: (i, off_sb // D)), row, row],
                          out_specs=row, out_shape=jax.ShapeDtypeStruct((T, D), BF16),
                          compiler_params=_cp("parallel"), name=name)(z, z, ya, yb)


def _put_columns(stage_ref, dz_ref, row0, rows, col0, sem):
    dst = dz_ref.at[pl.ds(pl.multiple_of(row0, SUBLANES), rows),
                    pl.ds(pl.multiple_of(col0, 128), stage_ref.shape[1])]
    cp = pltpu.make_async_copy(stage_ref, dst, sem)
    cp.start()
    cp.wait()


def _merge_bwd(z, ya, yb, dm, dz, *, off_sa, off_sb, name, tr=512):
    T, D = ya.shape
    tr = min(tr, T)
    assert off_sb == off_sa + D

    def body(sa_ref, sb_ref, ya_ref, yb_ref, dm_ref, dz_in, dya_ref, dyb_ref, dz_ref, stage, sem):
        del dz_in
        dm = dm_ref[...]
        ga = _sig(sa_ref[...])
        gb = _sig(sb_ref[...])
        dya_ref[...] = (dm * ga).astype(BF16)
        dyb_ref[...] = (dm * gb).astype(BF16)
        stage[:, 0:D] = (dm * ya_ref[...] * ga * (1.0 - ga)).astype(BF16)
        stage[:, D:2 * D] = (dm * yb_ref[...] * gb * (1.0 - gb)).astype(BF16)
        _put_columns(stage, dz_ref, pl.program_id(0) * tr, tr, off_sa, sem)

    row = pl.BlockSpec((tr, D), lambda i: (i, 0))
    o = jax.ShapeDtypeStruct((T, D), BF16)
    return pl.pallas_call(body, grid=(T // tr,),
                          in_specs=[pl.BlockSpec((tr, D), lambda i: (i, off_sa // D)),
                                    pl.BlockSpec((tr, D), lambda i: (i, off_sb // D)), row, row, row, _ANY],
                          out_specs=(row, row, _ANY), out_shape=(o, o, jax.ShapeDtypeStruct(dz.shape, dz.dtype)),
                          scratch_shapes=[pltpu.VMEM((tr, 2 * D), BF16), pltpu.SemaphoreType.DMA],
                          input_output_aliases={5: 2},
                          compiler_params=_cp("parallel"), name=name)(z, z, ya, yb, dm, dz)


def _shift_rows(dst_ref, src_ref, r, total, back):
    for c0 in range(0, total - SUBLANES, CONV_CHUNK):
        n = min(CONV_CHUNK, total - SUBLANES - c0)
        if back:
            dst_ref[SUBLANES + c0:SUBLANES + c0 + n, :] = src_ref[SUBLANES + c0 - r:SUBLANES + c0 - r + n, :]
        else:
            dst_ref[c0:c0 + n, :] = src_ref[c0 + r:c0 + r + n, :]


def _taps_of(r, K):
    return [(q, K - 1 - (SUBLANES * q + r)) for q in range(-(-K // SUBLANES)) if SUBLANES * q + r < K]


def _conv_fwd(z, w, b, *, S, off_v, off_g, name, ct=256):
    T = z.shape[0]
    K, C = w.shape
    ct = min(ct, C)
    ch = min(CONV_CHUNK, S)
    glu = off_g is not None
    assert off_v % ct == 0 and (not glu or off_g % ct == 0)
    assert SUBLANES * ((K - 1) // SUBLANES) <= CONV_PAD - SUBLANES

    def body(*refs):
        if glu:
            v_ref, g_ref, w_ref, b_ref, o_ref, pad_ref, sh_ref = refs
        else:
            v_ref, w_ref, b_ref, o_ref, pad_ref, sh_ref = refs
        pad_ref[0:CONV_PAD, :] = jnp.zeros((CONV_PAD, ct), F32)
        if glu:
            pad_ref[CONV_PAD:CONV_PAD + S, :] = v_ref[...] * _sig(g_ref[...])
        else:
            pad_ref[CONV_PAD:CONV_PAD + S, :] = v_ref[...]
        for r in range(min(SUBLANES, K)):
            src = pad_ref
            if r > 0:
                _shift_rows(sh_ref, pad_ref, r, CONV_PAD + S, True)
                src = sh_ref
            for l0 in range(0, ct, 128):
                lanes = slice(l0, l0 + 128)
                for c in range(S // ch):
                    acc = None
                    for q, wrow in _taps_of(r, K):
                        st = CONV_PAD + c * ch - SUBLANES * q
                        term = w_ref[wrow:wrow + 1, lanes] * src[st:st + ch, lanes]
                        acc = term if acc is None else acc + term
                    rows = slice(c * ch, (c + 1) * ch)
                    if r == 0:
                        o_ref[rows, lanes] = acc + b_ref[:, lanes]
                    else:
                        o_ref[rows, lanes] += acc

    in_specs = [pl.BlockSpec((S, ct), lambda j, bb: (bb, off_v // ct + j))]
    args = [z]
    if glu:
        in_specs.append(pl.BlockSpec((S, ct), lambda j, bb: (bb, off_g // ct + j)))
        args.append(z)
    in_specs += [pl.BlockSpec((K, ct), lambda j, bb: (0, j)), pl.BlockSpec((1, ct), lambda j, bb: (0, j))]
    args += [w, b]
    return pl.pallas_call(body, grid=(C // ct, T // S), in_specs=in_specs,
                          out_specs=pl.BlockSpec((S, ct), lambda j, bb: (bb, j)),
                          out_shape=jax.ShapeDtypeStruct((T, C), F32),
                          scratch_shapes=[pltpu.VMEM((CONV_PAD + S, ct), F32), pltpu.VMEM((CONV_PAD + S, ct), F32)],
                          compiler_params=_cp("parallel", "parallel"), name=name)(*args)


def _conv_bwd(z, w, dy, dz, *, S, off_v, off_g, name, ct=256):
    T = z.shape[0]
    K, C = w.shape
    KP = -(-K // SUBLANES) * SUBLANES
    ct = min(ct, C)
    ch = min(CONV_CHUNK, S)
    glu = off_g is not None
    total = S + CONV_PAD

    def body(*refs):
        if glu:
            (v_ref, g_ref, w_ref, dy_ref, dz_in, dz_ref, dw_ref, db_ref,
             pad_ref, sh_ref, padb_ref, shb_ref, du_ref, stage_v, stage_g, sem) = refs
        else:
            (v_ref, w_ref, dy_ref, dz_in, dz_ref, dw_ref, db_ref,
             pad_ref, sh_ref, padb_ref, shb_ref, du_ref, stage_v, sem) = refs
        del dz_in
        j, bb = pl.program_id(0), pl.program_id(1)
        pad_ref[0:CONV_PAD, :] = jnp.zeros((CONV_PAD, ct), F32)
        if glu:
            pad_ref[CONV_PAD:total, :] = v_ref[...] * _sig(g_ref[...])
        else:
            pad_ref[CONV_PAD:total, :] = v_ref[...]
        padb_ref[0:S, :] = dy_ref[...]
        padb_ref[S:total, :] = jnp.zeros((CONV_PAD, ct), F32)

        @pl.when(bb == 0)
        def _():
            dw_ref[...] = jnp.zeros((KP, ct), F32)
            db_ref[...] = jnp.zeros((1, ct), F32)

        for r in range(min(SUBLANES, K)):
            u_src, d_src = pad_ref, padb_ref
            if r > 0:
                _shift_rows(sh_ref, pad_ref, r, total, True)
                _shift_rows(shb_ref, padb_ref, r, total, False)
                u_src, d_src = sh_ref, shb_ref
            for l0 in range(0, ct, 128):
                lanes = slice(l0, l0 + 128)
                for c in range(S // ch):
                    acc = None
                    for q, wrow in _taps_of(r, K):
                        st = c * ch + SUBLANES * q
                        term = w_ref[wrow:wrow + 1, lanes] * d_src[st:st + ch, lanes]
                        acc = term if acc is None else acc + term
                    rows = slice(c * ch, (c + 1) * ch)
                    if r == 0:
                        du_ref[rows, lanes] = acc
                    else:
                        du_ref[rows, lanes] += acc
                for q, wrow in _taps_of(r, K):
                    acc = None
                    for c in range(S // ch):
                        st = CONV_PAD + c * ch - SUBLANES * q
                        prod = padb_ref[c * ch:(c + 1) * ch, lanes] * u_src[st:st + ch, lanes]
                        acc = prod if acc is None else acc + prod
                    dw_ref[wrow:wrow + 1, lanes] += jnp.sum(acc, axis=0, keepdims=True)
        db_ref[...] += jnp.sum(dy_ref[...], axis=0, keepdims=True)
        for l0 in range(0, ct, 128):
            lanes = slice(l0, l0 + 128)
            for c in range(S // ch):
                rows = slice(c * ch, (c + 1) * ch)
                du = du_ref[rows, lanes]
                if glu:
                    sg = _sig(g_ref[rows, lanes])
                    stage_v[rows, lanes] = (du * sg).astype(BF16)
                    stage_g[rows, lanes] = (du * v_ref[rows, lanes] * sg * (1.0 - sg)).astype(BF16)
                else:
                    stage_v[rows, lanes] = du.astype(BF16)
        _put_columns(stage_v, dz_ref, bb * S, S, off_v + j * ct, sem)
        if glu:
            _put_columns(stage_g, dz_ref, bb * S, S, off_g + j * ct, sem)

    blk = lambda off: pl.BlockSpec((S, ct), lambda j, bb: (bb, off // ct + j))
    in_specs = [blk(off_v)]
    args = [z]
    if glu:
        in_specs.append(blk(off_g))
        args.append(z)
    in_specs += [pl.BlockSpec((K, ct), lambda j, bb: (0, j)), blk(0), _ANY]
    args += [w, dy, dz]
    out_shape = (jax.ShapeDtypeStruct(dz.shape, dz.dtype), jax.ShapeDtypeStruct((KP, C), F32),
                 jax.ShapeDtypeStruct((1, C), F32))
    out_specs = (_ANY, pl.BlockSpec((KP, ct), lambda j, bb: (0, j)), pl.BlockSpec((1, ct), lambda j, bb: (0, j)))
    padded = pltpu.VMEM((total, ct), F32)
    stage = pltpu.VMEM((S, ct), BF16)
    return pl.pallas_call(body, grid=(C // ct, T // S), in_specs=in_specs, out_specs=out_specs, out_shape=out_shape,
                          scratch_shapes=[padded, padded, padded, padded, pltpu.VMEM((S, ct), F32), stage]
                          + ([stage] if glu else []) + [pltpu.SemaphoreType.DMA],
                          input_output_aliases={len(args) - 1: 0},
                          compiler_params=_cp("parallel", "arbitrary"), name=name)(*args)


def _softplus_neg(lam):
    return jnp.maximum(-lam, 0.0) + jnp.log1p(jnp.exp(-jnp.abs(lam)))


def _neg_expm1(x):
    u = jnp.exp(x)
    um1 = u - 1.0
    lg = jnp.log(u)
    safe = jnp.where(lg == 0.0, 1.0, lg)
    em1 = jnp.where(um1 == 0.0, x, jnp.where(um1 == -1.0, -1.0, um1 * x / safe))
    return -em1


def _gates_fwd(v, wa, wx, ba, bx, lam, *, S, name, tm=512):
    T, C = v.shape
    G, gw, _ = wa.shape
    tm = min(tm, T)

    def body(v_ref, wa_ref, wx_ref, ba_ref, bx_ref, lam_ref, r_ref, i_ref, a_ref, bt_ref):
        vv = v_ref[...]
        vb = vv.astype(BF16)
        r = _sig(jnp.dot(vb, wa_ref[...], preferred_element_type=F32) + ba_ref[...])
        ig = _sig(jnp.dot(vb, wx_ref[...], preferred_element_type=F32) + bx_ref[...])
        log_a = -LRU_C * r * _softplus_neg(lam_ref[...])
        a = jnp.exp(log_a)
        mult = jnp.sqrt(_neg_expm1(2.0 * log_a))
        row = pl.program_id(0) * tm + lax.broadcasted_iota(jnp.int32, (tm, gw), 0)
        mult = jnp.where(row % S == 0, 1.0, mult)
        r_ref[...] = r
        i_ref[...] = ig
        a_ref[...] = a
        bt_ref[...] = mult * ig * vv

    blk = pl.BlockSpec((tm, gw), lambda i, g: (i, g))
    wsp = pl.BlockSpec((None, gw, gw), lambda i, g: (g, 0, 0))
    vec = pl.BlockSpec((1, gw), lambda i, g: (0, g))
    o = jax.ShapeDtypeStruct((T, C), F32)
    return pl.pallas_call(body, grid=(T // tm, G), in_specs=[blk, wsp, wsp, vec, vec, vec],
                          out_specs=(blk, blk, blk, blk), out_shape=(o, o, o, o),
                          compiler_params=_cp("parallel", "parallel"), name=name)(v, wa, wx, ba, bx, lam)


def _scan_fwd(a, bt, z, *, S, off_gate, name, ct=512, tt=256):
    T, C = a.shape
    ct = min(ct, C)
    tt = min(tt, S)
    nt = S // tt
    assert off_gate % ct == 0

    def body(a_ref, b_ref, g_ref, h_ref, hb_ref, carry_ref):
        row = lax.broadcasted_iota(jnp.int32, (SUBLANES, ct), 0)

        @pl.when(pl.program_id(2) == 0)
        def _():
            carry_ref[...] = jnp.zeros((SUBLANES, ct), F32)

        def step(i, carry):
            st = pl.multiple_of(i * SUBLANES, SUBLANES)
            A = a_ref[pl.ds(st, SUBLANES), :]
            B = b_ref[pl.ds(st, SUBLANES), :]
            for d in (1, 2, 4):
                m = row >= d
                Bn = jnp.where(m, A * pltpu.roll(B, d, 0) + B, B)
                A = jnp.where(m, A * pltpu.roll(A, d, 0), A)
                B = Bn
            h = B + A * carry
            h_ref[pl.ds(st, SUBLANES), :] = h
            return jnp.broadcast_to(h[SUBLANES - 1:SUBLANES, :], (SUBLANES, ct))

        carry_ref[...] = lax.fori_loop(0, tt // SUBLANES, step, carry_ref[...], unroll=2)
        hb_ref[...] = (h_ref[...] * _gelu(g_ref[...])).astype(BF16)

    blk = pl.BlockSpec((tt, ct), lambda j, bb, t: (bb * nt + t, j))
    return pl.pallas_call(body, grid=(C // ct, T // S, nt),
                          in_specs=[blk, blk, pl.BlockSpec((tt, ct), lambda j, bb, t: (bb * nt + t, off_gate // ct + j))],
                          out_specs=(blk, blk),
                          out_shape=(jax.ShapeDtypeStruct((T, C), F32), jax.ShapeDtypeStruct((T, C), BF16)),
                          scratch_shapes=[pltpu.VMEM((SUBLANES, ct), F32)],
                          compiler_params=_cp("parallel", "parallel", "arbitrary"), name=name)(a, bt, z)


def _scan_bwd(a, h, z, dhb, dz, *, S, off_gate, name, ct=512, tt=256):
    T, C = a.shape
    ct = min(ct, C)
    tt = min(tt, S)
    nt = S // tt
    n_tiles = tt // SUBLANES

    def body(a_ref, h_ref, hp_ref, g_ref, dhb_ref, dz_in, G_ref, da_ref, dz_ref, dh_ref, carry_ref, stage, sem):
        del dz_in
        row = lax.broadcasted_iota(jnp.int32, (SUBLANES, ct), 0)
        tb = nt - 1 - pl.program_id(2)

        @pl.when(pl.program_id(2) == 0)
        def _():
            carry_ref[...] = jnp.zeros((SUBLANES, ct), F32)

        gate = g_ref[...]
        dhb = dhb_ref[...]
        dh_ref[...] = dhb * _gelu(gate)
        stage[...] = (dhb * h_ref[...] * _gelu_grad(gate)).astype(BF16)
        _put_columns(stage, dz_ref, (pl.program_id(1) * nt + tb) * tt, tt, off_gate + pl.program_id(0) * ct, sem)
        h_before = jnp.where(tb > 0, jnp.broadcast_to(hp_ref[SUBLANES - 1:SUBLANES, :], (SUBLANES, ct)), 0.0)

        def step(k, qcarry):
            i = n_tiles - 1 - k
            st = pl.multiple_of(i * SUBLANES, SUBLANES)
            stp = pl.multiple_of(jnp.maximum(i - 1, 0) * SUBLANES, SUBLANES)
            A = a_ref[pl.ds(st, SUBLANES), :]
            hv = h_ref[pl.ds(st, SUBLANES), :]
            hprev_tile = h_ref[pl.ds(stp, SUBLANES), :]
            dh = dh_ref[pl.ds(st, SUBLANES), :]
            Aq = A
            Bq = A * dh
            for d in (1, 2, 4):
                m = row < SUBLANES - d
                Bn = jnp.where(m, Aq * pltpu.roll(Bq, SUBLANES - d, 0) + Bq, Bq)
                Aq = jnp.where(m, Aq * pltpu.roll(Aq, SUBLANES - d, 0), Aq)
                Bq = Bn
            q = Bq + Aq * qcarry
            qnext = jnp.where(row == SUBLANES - 1, qcarry, pltpu.roll(q, SUBLANES - 1, 0))
            g = dh + qnext
            hlast = jnp.where(i > 0, jnp.broadcast_to(hprev_tile[SUBLANES - 1:SUBLANES, :], (SUBLANES, ct)), h_before)
            hprev = jnp.where(row == 0, hlast, pltpu.roll(hv, 1, 0))
            G_ref[pl.ds(st, SUBLANES), :] = g
            da_ref[pl.ds(st, SUBLANES), :] = g * hprev
            return jnp.broadcast_to(q[0:1, :], (SUBLANES, ct))

        carry_ref[...] = lax.fori_loop(0, n_tiles, step, carry_ref[...], unroll=2)

    tiles_per_block = tt // SUBLANES
    blk = pl.BlockSpec((tt, ct), lambda j, bb, t: (bb * nt + nt - 1 - t, j))
    before = pl.BlockSpec((SUBLANES, ct),
                          lambda j, bb, t: (jnp.maximum((bb * nt + nt - 1 - t) * tiles_per_block - 1, 0), j))
    o = jax.ShapeDtypeStruct((T, C), F32)
    return pl.pallas_call(body, grid=(C // ct, T // S, nt),
                          in_specs=[blk, blk, before,
                                    pl.BlockSpec((tt, ct), lambda j, bb, t: (bb * nt + nt - 1 - t, off_gate // ct + j)), blk,
                                    _ANY],
                          out_specs=(blk, blk, _ANY), out_shape=(o, o, jax.ShapeDtypeStruct(dz.shape, dz.dtype)),
                          scratch_shapes=[pltpu.VMEM((tt, ct), F32), pltpu.VMEM((SUBLANES, ct), F32),
                                          pltpu.VMEM((tt, ct), BF16), pltpu.SemaphoreType.DMA],
                          input_output_aliases={5: 2},
                          compiler_params=_cp("parallel", "parallel", "arbitrary"), name=name)(a, h, h, z, dhb, dz)


def _gates_bwd(v, r, ig, Gb, da, lam, *, S, name, tr=256):
    T, C = v.shape
    tr = min(tr, T)

    def body(v_ref, r_ref, i_ref, G_ref, da_ref, lam_ref, dpr_ref, dpi_ref, dv_ref, dsp_ref, dba_ref, dbx_ref):
        vv, r, ig, Gv = v_ref[...], r_ref[...], i_ref[...], G_ref[...]
        sp = _softplus_neg(lam_ref[...])
        log_a = -LRU_C * r * sp
        a = jnp.exp(log_a)
        mult_raw = jnp.sqrt(_neg_expm1(2.0 * log_a))
        row = pl.program_id(0) * tr + lax.broadcasted_iota(jnp.int32, (tr, C), 0)
        start = row % S == 0
        mult = jnp.where(start, 1.0, mult_raw)
        dmult = jnp.where(start, 0.0, Gv * ig * vv)
        di = Gv * mult * vv
        dv_ref[...] = Gv * mult * ig
        dla = da_ref[...] * a - dmult * (a * a) / jnp.where(start, 1.0, mult_raw)
        dr = dla * (-LRU_C) * sp
        dpr = dr * r * (1.0 - r)
        dpi = di * ig * (1.0 - ig)
        dpr_ref[...] = dpr.astype(BF16)
        dpi_ref[...] = dpi.astype(BF16)
        p_sp = jnp.sum(dla * (-LRU_C) * r, axis=0, keepdims=True) * (-_sig(-lam_ref[...]))
        p_a = jnp.sum(dpr, axis=0, keepdims=True)
        p_x = jnp.sum(dpi, axis=0, keepdims=True)

        @pl.when(pl.program_id(0) == 0)
        def _():
            dsp_ref[...] = p_sp
            dba_ref[...] = p_a
            dbx_ref[...] = p_x

        @pl.when(pl.program_id(0) > 0)
        def _():
            dsp_ref[...] += p_sp
            dba_ref[...] += p_a
            dbx_ref[...] += p_x

    rowb = pl.BlockSpec((tr, C), lambda i: (i, 0))
    vec = pl.BlockSpec((1, C), lambda i: (0, 0))
    ob = jax.ShapeDtypeStruct((T, C), BF16)
    ov = jax.ShapeDtypeStruct((1, C), F32)
    return pl.pallas_call(body, grid=(T // tr,), in_specs=[rowb, rowb, rowb, rowb, rowb, vec],
                          out_specs=(rowb, rowb, rowb, vec, vec, vec),
                          out_shape=(ob, ob, jax.ShapeDtypeStruct((T, C), F32), ov, ov, ov),
                          compiler_params=_cp("arbitrary"), name=name)(v, r, ig, Gb, da, lam)


def _gates_dgrad(dpr, dpi, wa, wx, dv_direct, *, name, tm=512, dep=None):
    T, C = dpr.shape
    G, gw, _ = wa.shape
    tm = min(tm, T)

    def body(dpr_ref, dpi_ref, wa_ref, wx_ref, dvd_ref, *rest):
        dv_ref = rest[-1]
        dn = (((1,), (1,)), ((), ()))
        dv_ref[...] = (dvd_ref[...]
                       + lax.dot_general(dpr_ref[...], wa_ref[...], dn, preferred_element_type=F32)
                       + lax.dot_general(dpi_ref[...], wx_ref[...], dn, preferred_element_type=F32))

    blk = pl.BlockSpec((tm, gw), lambda i, g: (i, g))
    wsp = pl.BlockSpec((None, gw, gw), lambda i, g: (g, 0, 0))
    in_specs, args = [blk, blk, wsp, wsp, blk], [dpr, dpi, wa, wx, dv_direct]
    if dep is not None:
        in_specs.append(pl.BlockSpec(dep.shape, lambda i, g: (0, 0)))
        args.append(dep)
    return pl.pallas_call(body, grid=(T // tm, G), in_specs=in_specs, out_specs=blk,
                          out_shape=jax.ShapeDtypeStruct((T, C), F32),
                          compiler_params=_cp("parallel", "parallel"), name=name)(*args)


def _gates_wgrad(v, dpr, dpi, *, G, name, tk=512):
    T, C = v.shape
    gw = C // G
    hd = gw // HEADS_PER_GROUP
    tk = min(tk, T)
    nk = T // tk

    def body(v_ref, dpr_ref, dpi_ref, dwa_ref, dwx_ref, acc_a, acc_x):
        dn = (((0,), (0,)), ((), ()))
        vb = v_ref[...].astype(BF16)
        pa = lax.dot_general(vb, dpr_ref[...], dn, preferred_element_type=F32)
        px = lax.dot_general(vb, dpi_ref[...], dn, preferred_element_type=F32)

        @pl.when(pl.program_id(1) == 0)
        def _():
            acc_a[...] = pa
            acc_x[...] = px

        @pl.when(pl.program_id(1) > 0)
        def _():
            acc_a[...] += pa
            acc_x[...] += px

        @pl.when(pl.program_id(1) == nk - 1)
        def _():
            for h in range(HEADS_PER_GROUP):
                dwa_ref[h] = acc_a[h * hd:(h + 1) * hd, h * hd:(h + 1) * hd]
                dwx_ref[h] = acc_x[h * hd:(h + 1) * hd, h * hd:(h + 1) * hd]

    blk = pl.BlockSpec((tk, gw), lambda g, k: (k, g))
    wsp = pl.BlockSpec((HEADS_PER_GROUP, hd, hd), lambda g, k: (g, 0, 0))
    o = jax.ShapeDtypeStruct((G * HEADS_PER_GROUP, hd, hd), F32)
    return pl.pallas_call(body, grid=(G, nk), in_specs=[blk, blk, blk], out_specs=(wsp, wsp), out_shape=(o, o),
                          scratch_shapes=[pltpu.VMEM((gw, gw), F32), pltpu.VMEM((gw, gw), F32)],
                          compiler_params=_cp("parallel", "arbitrary"), name=name)(v, dpr, dpi)


def _group_weights(w):
    H, hd, _ = w.shape
    G = H // HEADS_PER_GROUP
    eye = jnp.eye(HEADS_PER_GROUP, dtype=w.dtype)
    wg = jnp.einsum("ghij,hk->ghikj", w.reshape(G, HEADS_PER_GROUP, hd, hd), eye)
    return wg.reshape(G, HEADS_PER_GROUP * hd, HEADS_PER_GROUP * hd).astype(BF16)


def _layer_fwd(x, p, *, S, fetch=None):
    D = x.shape[1]
    Dc = p["conv_a_b"].shape[1]
    Dr = p["conv_b_b"].shape[1]
    offs = dict(va=0, ga=Dc, xb=2 * Dc, gb=2 * Dc + Dr, sa=2 * Dc + 2 * Dr, sb=2 * Dc + 2 * Dr + D)
    h = _rms_fwd(x, p["g_mix"], name="rms_mix_fwd")
    if fetch is not None:
        fetch("in", h)
    z = _mm_nn(h, p["w_in"], tm=1024, tn=p["w_in"].shape[2], bias=p["b_in"], a_resident=True, name="mm_in_fwd")
    if fetch is not None:
        fetch("mix", z)
    u1 = _conv_fwd(z, p["conv_a_w"], p["conv_a_b"], S=S, off_v=offs["va"], off_g=offs["ga"], name="conv_a_fwd")
    u2 = _ln_silu_fwd(u1, p["ln_g"], p["ln_b"], name="ln_silu_fwd")
    ya = _mm_nn(u2, p["w_a_out"], tm=1024, tn=1024, name="mm_a_out_fwd")
    v0 = _conv_fwd(z, p["conv_b_w"], p["conv_b_b"], S=S, off_v=offs["xb"], off_g=None, name="conv_b_fwd")
    r, ig, a, bt = _gates_fwd(v0, p["wg_a"], p["wg_x"], p["b_rg_a"], p["b_rg_x"], p["lam"], S=S, name="gates_fwd")
    hs, hb = _scan_fwd(a, bt, z, S=S, off_gate=offs["gb"], name="scan_fwd")
    yb = _mm_nn(hb, p["w_b_out"], tm=1024, tn=1024, name="mm_b_out_fwd")
    m = _merge_fwd(z, ya, yb, off_sa=offs["sa"], off_sb=offs["sb"], name="merge_fwd")
    x_mid = _mm_nn(m, p["w_o"], tm=1024, tn=1024, resid=x, name="mm_o_fwd")
    h2 = _rms_fwd(x_mid, p["g_mlp"], name="rms_mlp_fwd")
    if fetch is not None:
        fetch("mlp", h2)
    f_pre, f = _mm_nn(h2, p["w_1"], tm=1024, tn=p["w_1"].shape[2], relu2=True, a_resident=True, name="mm_1_fwd")
    x_next = _mm_nn(f, p["w_2"], tm=512, tn=1024, resid=x_mid, name="mm_2_fwd")
    saved = dict(x=x, h=h, z=z, u1=u1, u2=u2, ya=ya, v0=v0, r=r, ig=ig, a=a, hs=hs, hb=hb, yb=yb, m=m,
                 x_mid=x_mid, h2=h2, f_pre=f_pre, f=f, offs=offs)
    return x_next, saved


def _layer_bwd_mlp(dx, p, sv, *, wdt, dep=None):
    g = {}
    g["w_2"] = _mm_tn(sv["f"], dx, tm=2048, tn=1024, tk=1024, out_dtype=wdt, name="mm_2_wgrad")
    dfp = _mm_nt(dx, p["w_2"], tm=1024, tn=1024, tk=1024, mul_relu=sv["f_pre"], out_dtype=BF16, dep=dep,
                 name="mm_2_dgrad")
    g["w_1"] = _mm_tn(sv["h2"], dfp, tm=1024, tn=512, tk=1024, out_blocks=p["w_1"].shape[0], out_dtype=wdt,
                      name="mm_1_wgrad")
    dh2 = _mm_nt(dfp, p["w_1"], tm=1024, tn=1024, tk=512, name="mm_1_dgrad")
    dx_mid, g["g_mlp"] = _rms_bwd(sv["x_mid"], p["g_mlp"], dh2, dx, name="rms_mlp_bwd")
    return dx_mid, g


def _layer_bwd_mix(dx_mid, p, sv, *, S, wdt, dep=None, on_gate_grads=None, on_weight_grads=None):
    offs = sv["offs"]
    g = {}
    g["w_o"] = _mm_tn(sv["m"], dx_mid, tm=1024, tn=1024, tk=1024, out_dtype=wdt, name="mm_o_wgrad")
    dm = _mm_nt(dx_mid, p["w_o"], tm=1024, tn=1024, tk=1024, dep=dep, name="mm_o_dgrad")
    dz = lax.empty(sv["z"].shape, BF16)
    dya, dyb, dz = _merge_bwd(sv["z"], sv["ya"], sv["yb"], dm, dz, off_sa=offs["sa"], off_sb=offs["sb"], name="merge_bwd")
    g["w_b_out"] = _mm_tn(sv["hb"], dyb, tm=1536, tn=1024, tk=1024, out_dtype=wdt, name="mm_b_out_wgrad")
    dhb = _mm_nt(dyb, p["w_b_out"], tm=1024, tn=1536, tk=1024, name="mm_b_out_dgrad")
    Gb, da, dz = _scan_bwd(sv["a"], sv["hs"], sv["z"], dhb, dz, S=S, off_gate=offs["gb"], name="scan_bwd")
    dpr, dpi, dv_direct, g["lam"], g["b_rg_a"], g["b_rg_x"] = _gates_bwd(sv["v0"], sv["r"], sv["ig"], Gb, da, p["lam"], S=S,
                                                                    name="gates_bwd")
    g["w_rg_a"], g["w_rg_x"] = _gates_wgrad(sv["v0"], dpr, dpi, G=p["wg_a"].shape[0], name="gates_wgrad")
    dep_gates = on_gate_grads(g) if on_gate_grads is not None else None
    dv0 = _gates_dgrad(dpr, dpi, p["wg_a"], p["wg_x"], dv_direct, dep=dep_gates, name="gates_dgrad")
    dz, g["conv_b_w"], g["conv_b_b"] = _conv_bwd(sv["z"], p["conv_b_w"], dv0, dz, S=S, off_v=offs["xb"], off_g=None,
                                                 name="conv_b_bwd")
    g["w_a_out"] = _mm_tn(sv["u2"], dya, tm=1024, tn=1024, tk=1024, out_dtype=wdt, name="mm_a_out_wgrad")
    du2 = _mm_nt(dya, p["w_a_out"], tm=1024, tn=1024, tk=1024, name="mm_a_out_dgrad")
    du1, g["ln_g"], g["ln_b"] = _ln_silu_bwd(sv["u1"], p["ln_g"], p["ln_b"], du2, name="ln_silu_bwd")
    dz, g["conv_a_w"], g["conv_a_b"] = _conv_bwd(sv["z"], p["conv_a_w"], du1, dz, S=S, off_v=offs["va"],
                                                 off_g=offs["ga"], name="conv_a_bwd")
    g["w_in"], db_in = _mm_tn(sv["h"], dz, tm=1024, tn=512, tk=1024, out_blocks=p["w_in"].shape[0], colsum=True,
                              out_dtype=wdt, name="mm_in_wgrad")
    g["b_in"] = db_in[:1]
    dep_in = on_weight_grads(g) if on_weight_grads is not None else None
    dh = _mm_nt(dz, p["w_in"], tm=1024, tn=1024, tk=512, dep=dep_in, name="mm_in_dgrad")
    dx_in, g["g_mix"] = _rms_bwd(sv["x"], p["g_mix"], dh, dx_mid, name="rms_mix_bwd")
    return dx_in, g


def _layer_bwd(dx, p, sv, *, S, wdt=F32):
    dx_mid, g = _layer_bwd_mlp(dx, p, sv, wdt=wdt)
    dx_in, g2 = _layer_bwd_mix(dx_mid, p, sv, S=S, wdt=wdt)
    g.update(g2)
    return dx_in, g


def _local_step(x, tgt, layers, g_final, *, S, wdt=F32):
    saved = []
    for p in layers:
        x, sv = _layer_fwd(x, p, S=S)
        saved.append(sv)
    loss, dx, dg_final = _final_loss(x, g_final, tgt, name="final_loss")
    grads = [None] * len(layers)
    for l in reversed(range(len(layers))):
        dx, grads[l] = _layer_bwd(dx, layers[l], saved[l], S=S, wdt=wdt)
    return loss, dx, grads, dg_final


_HBM = pl.BlockSpec(memory_space=pltpu.HBM)
_MESH = pl.DeviceIdType.MESH


_SEM = pl.BlockSpec(memory_space=pltpu.SEMAPHORE)
_ANY = pl.BlockSpec(memory_space=pl.ANY)
_FLIPS = [(dx, dy, dc) for dx in (0, 1) for dy in (0, 1) for dc in (0, 1)][1:]


def _place(shard, me_idx, dtype, *, name):
    r, cc = shard.shape
    tr = 512 if r % 512 == 0 else r

    def body(me_ref, s_ref, o_ref):
        del me_ref
        o_ref[...] = s_ref[...].astype(dtype)

    grid_spec = pltpu.PrefetchScalarGridSpec(
        num_scalar_prefetch=1, grid=(r // tr,),
        in_specs=[pl.BlockSpec((tr, cc), lambda i, me: (i, 0))],
        out_specs=pl.BlockSpec((None, tr, cc), lambda i, me: (me[0], i, 0)))
    return pl.pallas_call(body, grid_spec=grid_spec, out_shape=jax.ShapeDtypeStruct((N_DEV, r, cc), dtype),
                          compiler_params=_cp("arbitrary"), name=name)(me_idx, shard)


def _exchange_copies(srcs, lands, send_sems, recv_sems):
    x, y, c = lax.axis_index("x"), lax.axis_index("y"), lax.axis_index("c")
    me = 4 * x + 2 * y + c
    pairs = []
    for k, (dx, dy, dc) in enumerate(_FLIPS):
        peer = (1 - x if dx else x, 1 - y if dy else y, 1 - c if dc else c)
        pidx = 4 * peer[0] + 2 * peer[1] + peer[2]
        for a, land in enumerate(lands):
            src = land.at[me] if srcs is None else srcs[a].at[pidx]
            sem = k * len(lands) + a
            out = pltpu.make_async_remote_copy(src_ref=src, dst_ref=land.at[me], send_sem=send_sems.at[sem],
                                               recv_sem=recv_sems.at[sem], device_id=peer, device_id_type=_MESH)
            arrival = pltpu.make_async_remote_copy(src_ref=src, dst_ref=land.at[pidx], send_sem=send_sems.at[sem],
                                                   recv_sem=recv_sems.at[sem], device_id=peer, device_id_type=_MESH)
            pairs.append((out, arrival))
    return pairs


def _exchange_start(srcs, lands, *, name):
    n = len(lands)
    bufs = list(lands) if srcs is None else list(srcs) + list(lands)
    nb = len(bufs)

    def body(*refs):
        ins = refs[:nb]
        send_sems, recv_sems = refs[nb], refs[nb + 1]
        token = refs[-1]
        for out, _ in _exchange_copies(None if srcs is None else ins[:n], ins[nb - n:], send_sems, recv_sems):
            out.start()
        token[...] = jnp.zeros_like(token)

    sems = pltpu.SemaphoreType.DMA((len(_FLIPS) * n,))
    res = pl.pallas_call(
        body, name=name, in_specs=[_HBM] * nb,
        out_shape=(sems, sems, *[pltpu.HBM(b.shape, b.dtype) for b in bufs], jax.ShapeDtypeStruct((SUBLANES, 128), F32)),
        out_specs=(_SEM, _SEM, *[_HBM] * nb, pl.BlockSpec(memory_space=pltpu.VMEM)),
        input_output_aliases={i: 2 + i for i in range(nb)},
        compiler_params=pltpu.CompilerParams(has_side_effects=pltpu.SideEffectType.DATAFLOW_SIDE_EFFECTING),
    )(*[pltpu.with_memory_space_constraint(b, pltpu.HBM) for b in bufs])
    return res[0], res[1], list(res[2:2 + nb]), res[-1]


def _exchange_wait(send_sems, recv_sems, bufs, *, scatter, after, name):
    nb = len(bufs)
    n = nb // 2 if scatter else nb

    def body(*refs):
        ins = refs[:nb]
        for out, arrival in _exchange_copies(ins[:n] if scatter else None, ins[nb - n:], refs[nb], refs[nb + 1]):
            out.wait_send()
            arrival.wait_recv()

    extra = [] if after is None else [after]
    res = pl.pallas_call(
        body, name=name, in_specs=[_HBM] * nb + [_SEM, _SEM] + [_ANY] * len(extra),
        out_shape=tuple(pltpu.HBM(b.shape, b.dtype) for b in bufs), out_specs=tuple([_HBM] * nb),
        input_output_aliases={i: i for i in range(nb)},
        compiler_params=pltpu.CompilerParams(has_side_effects=pltpu.SideEffectType.DATAFLOW_SIDE_EFFECTING),
    )(*bufs, send_sems, recv_sems, *extra)
    return list(res)


def _adamw_math(w, g, m, v):
    m = ADAM_B1 * m + (1.0 - ADAM_B1) * g
    v = ADAM_B2 * v + (1.0 - ADAM_B2) * (g * g)
    m_hat = m / (1.0 - ADAM_B1 ** ADAM_STEP)
    v_hat = v / (1.0 - ADAM_B2 ** ADAM_STEP)
    delta = -ADAM_LR * (m_hat / (jnp.sqrt(v_hat) + ADAM_EPS) + ADAM_WD * w)
    return delta, m, v


def _adamw(w, m, v, parts, prev, layer, me_idx, *, name, own=None):
    L, r, cc = w.shape
    P = parts.shape[0]
    tr = 512 if r % 512 == 0 else r
    if prev is None:
        prev = tuple(lax.empty(w.shape, F32) for _ in range(4))

    def body(me_ref, w_ref, m_ref, v_ref, p_ref, *rest):
        g_ref, d_ref, nm_ref, nv_ref = rest[-4:]
        if own is None:
            g = p_ref[0].astype(F32)
            for q in range(1, P):
                g = g + p_ref[q].astype(F32)
        else:
            me = me_ref[0]
            g = rest[0][...].astype(F32)
            for q in range(P):
                g = g + jnp.where(q == me, 0.0, p_ref[q].astype(F32))
        d, nm, nv = _adamw_math(w_ref[...], g, m_ref[...], v_ref[...])
        g_ref[...] = g
        d_ref[...] = d
        nm_ref[...] = nm
        nv_ref[...] = nv

    blk = pl.BlockSpec((None, tr, cc), lambda i, me: (layer, i, 0))
    in_specs = [blk, blk, blk, pl.BlockSpec((P, tr, cc), lambda i, me: (0, i, 0))]
    args = [me_idx, w, m, v, parts]
    if own is not None:
        in_specs.append(pl.BlockSpec((None, tr, cc), lambda i, me: (me[0], i, 0)))
        args.append(own)
    first_prev = len(args)
    in_specs += [_ANY] * 4
    args += list(prev)
    grid_spec = pltpu.PrefetchScalarGridSpec(num_scalar_prefetch=1, grid=(r // tr,), in_specs=in_specs,
                                             out_specs=(blk, blk, blk, blk))
    o = jax.ShapeDtypeStruct(w.shape, F32)
    return pl.pallas_call(body, grid_spec=grid_spec, out_shape=(o, o, o, o),
                          input_output_aliases={first_prev + i: i for i in range(4)},
                          compiler_params=_cp("parallel"), name=name)(*args)


_SHARDED = ("w_in", "conv_a_w", "w_a_out", "conv_b_w", "w_b_out", "w_o", "w_1", "w_2")
_COL_SHARDED = ("w_in", "conv_a_w", "conv_b_w", "w_1")
_REPLICATED = ("g_mix", "b_in", "conv_a_b", "ln_g", "ln_b", "conv_b_b", "w_rg_a", "b_rg_a", "w_rg_x", "b_rg_x", "lam",
               "g_mlp")
_WEIGHTS = ("g_mix", "w_in", "b_in", "conv_a_w", "conv_a_b", "ln_g", "ln_b", "w_a_out", "conv_b_w", "conv_b_b", "w_rg_a",
            "b_rg_a", "w_rg_x", "b_rg_x", "lam", "w_b_out", "w_o", "g_mlp", "w_1", "w_2", "g_final")
_LANES = 128


def _cols_from_blocks(b):
    nb, K, n = b.shape
    return b.transpose(1, 0, 2).reshape(K, nb * n)


def _blocks_from_cols(w, K):
    n = w.shape[1] // N_DEV
    return w[:K].reshape(K, N_DEV, n).transpose(1, 0, 2)


def kernel(x, g_mix, w_in, b_in, conv_a_w, conv_a_b, ln_g, ln_b, w_a_out, conv_b_w, conv_b_b, w_rg_a, b_rg_a, w_rg_x, b_rg_x, lam, w_b_out, w_o, g_mlp, w_1, w_2, g_final, loss_target, m_g_mix, m_w_in, m_b_in, m_conv_a_w, m_conv_a_b, m_ln_g, m_ln_b, m_w_a_out, m_conv_b_w, m_conv_b_b, m_w_rg_a, m_b_rg_a, m_w_rg_x, m_b_rg_x, m_lam, m_w_b_out, m_w_o, m_g_mlp, m_w_1, m_w_2, m_g_final, v_g_mix, v_w_in, v_b_in, v_conv_a_w, v_conv_a_b, v_ln_g, v_ln_b, v_w_a_out, v_conv_b_w, v_conv_b_b, v_w_rg_a, v_b_rg_a, v_w_rg_x, v_b_rg_x, v_lam, v_w_b_out, v_w_o, v_g_mlp, v_w_1, v_w_2, v_g_final):
    W = dict(g_mix=g_mix, w_in=w_in, b_in=b_in, conv_a_w=conv_a_w, conv_a_b=conv_a_b, ln_g=ln_g, ln_b=ln_b,
             w_a_out=w_a_out, conv_b_w=conv_b_w, conv_b_b=conv_b_b, w_rg_a=w_rg_a, b_rg_a=b_rg_a, w_rg_x=w_rg_x,
             b_rg_x=b_rg_x, lam=lam, w_b_out=w_b_out, w_o=w_o, g_mlp=g_mlp, w_1=w_1, w_2=w_2, g_final=g_final)
    M = dict(g_mix=m_g_mix, w_in=m_w_in, b_in=m_b_in, conv_a_w=m_conv_a_w, conv_a_b=m_conv_a_b, ln_g=m_ln_g, ln_b=m_ln_b,
             w_a_out=m_w_a_out, conv_b_w=m_conv_b_w, conv_b_b=m_conv_b_b, w_rg_a=m_w_rg_a, b_rg_a=m_b_rg_a,
             w_rg_x=m_w_rg_x, b_rg_x=m_b_rg_x, lam=m_lam, w_b_out=m_w_b_out, w_o=m_w_o, g_mlp=m_g_mlp, w_1=m_w_1,
             w_2=m_w_2, g_final=m_g_final)
    V = dict(g_mix=v_g_mix, w_in=v_w_in, b_in=v_b_in, conv_a_w=v_conv_a_w, conv_a_b=v_conv_a_b, ln_g=v_ln_g, ln_b=v_ln_b,
             w_a_out=v_w_a_out, conv_b_w=v_conv_b_w, conv_b_b=v_conv_b_b, w_rg_a=v_w_rg_a, b_rg_a=v_b_rg_a,
             w_rg_x=v_w_rg_x, b_rg_x=v_b_rg_x, lam=v_lam, w_b_out=v_w_b_out, w_o=v_w_o, g_mlp=v_g_mlp, w_1=v_w_1,
             w_2=v_w_2, g_final=v_g_final)
    NB, S, D = x.shape
    L = g_mix.shape[0]
    hd = w_rg_a.shape[-1]
    me_idx = (4 * lax.axis_index("x") + 2 * lax.axis_index("y") + lax.axis_index("c")).astype(jnp.int32).reshape(1)

    stages = (("in", ("w_in",)), ("mix", ("conv_a_w", "w_a_out", "conv_b_w", "w_b_out", "w_o")), ("mlp", ("w_1", "w_2")))
    gathers = {}
    started = jnp.zeros((), F32)
    for l in range(L):
        for stage, names in stages:
            lands = [_place(W[k][l], me_idx, F32 if k.startswith("conv") else BF16, name="place_" + k) for k in names]
            send_sems, recv_sems, bufs, token = _exchange_start(None, lands, name=f"weights_start_{stage}_{l}")
            gathers[l, stage] = (names, send_sems, recv_sems, bufs)
            started = started + token[0, 0]

    xt = x.reshape(NB * S, D)
    layers, saved = [], []
    for l in range(L):
        p = {k: W[k][l][None] for k in ("g_mix", "b_in", "conv_a_b", "ln_g", "ln_b", "conv_b_b", "b_rg_a", "b_rg_x",
                                         "lam", "g_mlp")}
        if l == 0:
            p["g_mix"] = p["g_mix"] + started
        p["wg_a"] = _group_weights(w_rg_a[l])
        p["wg_x"] = _group_weights(w_rg_x[l])

        def fetch(stage, after, l=l, p=p):
            names, send_sems, recv_sems, bufs = gathers[l, stage]
            bufs = _exchange_wait(send_sems, recv_sems, bufs, scatter=False, after=after,
                                  name=f"weights_wait_{stage}_{l}")
            for k, full in zip(names, bufs):
                if k in ("w_in", "w_1"):
                    p[k] = full
                elif k in _COL_SHARDED:
                    p[k] = _cols_from_blocks(full)
                else:
                    p[k] = full.reshape(-1, full.shape[-1])

        layers.append(p)
        xt, sv = _layer_fwd(xt, p, S=S, fetch=fetch)
        saved.append(sv)
    loss, dx, dg_final = _final_loss(xt, g_final[None], loss_target.reshape(NB * S, D), name="final_loss")

    results = {k: None for k in _SHARDED}

    def scatter_start(l, names, g, tag):
        srcs = []
        for k in names:
            shard_shape = W[k].shape[1:]
            if k in ("w_in", "w_1"):
                srcs.append(g[k])
            elif k in _COL_SHARDED:
                srcs.append(_blocks_from_cols(g[k], shard_shape[0]).astype(BF16))
            else:
                srcs.append(g[k].reshape((N_DEV,) + shard_shape))
        lands = [lax.empty(s.shape, BF16) for s in srcs]
        send_sems, recv_sems, bufs, token = _exchange_start(srcs, lands, name=f"grads_start_{tag}_{l}")
        return (names, send_sems, recv_sems, bufs, tag), token

    def scatter_finish(l, flight, after):
        names, send_sems, recv_sems, bufs, tag = flight
        bufs = _exchange_wait(send_sems, recv_sems, bufs, scatter=True, after=after, name=f"grads_wait_{tag}_{l}")
        n = len(names)
        for k, own, land in zip(names, bufs[:n], bufs[n:]):
            results[k] = _adamw(W[k], M[k], V[k], land, results[k], l, me_idx, own=own, name="adamw_" + k)

    gate_names = ("w_rg_a", "w_rg_x")
    gate_results = {k: None for k in gate_names}

    def small_start(l, g):
        lands = [_place(g[k].reshape(-1, hd), me_idx, F32, name="place_gate_grad") for k in gate_names]
        send_sems, recv_sems, bufs, token = _exchange_start(None, lands, name=f"small_start_{l}")
        return (send_sems, recv_sems, bufs), token

    def small_finish(l, flight, after):
        send_sems, recv_sems, bufs = flight
        bufs = _exchange_wait(send_sems, recv_sems, bufs, scatter=False, after=after, name=f"small_wait_{l}")
        for k, g_all in zip(gate_names, bufs):
            gate_results[k] = _adamw(W[k].reshape(L, -1, hd), M[k].reshape(L, -1, hd), V[k].reshape(L, -1, hd), g_all,
                                     gate_results[k], l, me_idx, name="adamw_gate")

    grads = [None] * L
    in_flight = []
    dep = None
    for l in reversed(range(L)):
        dx_mid, g = _layer_bwd_mlp(dx, layers[l], saved[l], wdt=BF16, dep=dep)
        f_mlp, dep = scatter_start(l, ("w_2", "w_1"), g, "mlp")
        flights = [f_mlp]

        small = []

        def on_gate_grads(g_part, l=l, small=small):
            f_small, token = small_start(l, g_part)
            small.append(f_small)
            return token

        def on_weight_grads(g_part, l=l, flights=flights):
            f_mix, token = scatter_start(l, ("w_o", "w_b_out", "w_a_out", "conv_a_w", "conv_b_w", "w_in"), g_part, "mix")
            flights.append(f_mix)
            return token

        dx, g_mix_part = _layer_bwd_mix(dx_mid, layers[l], saved[l], S=S, wdt=BF16, dep=dep,
                                        on_gate_grads=on_gate_grads, on_weight_grads=on_weight_grads)
        dep = None
        g.update(g_mix_part)
        grads[l] = g
        for l_prev, fs, f_sm in in_flight:
            for flight in fs:
                scatter_finish(l_prev, flight, dx)
            small_finish(l_prev, f_sm, dx)
        in_flight = [(l, flights, small[0])]
    for l_prev, fs, f_sm in in_flight:
        for flight in fs:
            scatter_finish(l_prev, flight, dx)
        small_finish(l_prev, f_sm, dx)
    out_g, out_d, out_m, out_v = {}, {}, {}, {}
    for k in _SHARDED:
        out_g[k], out_d[k], out_m[k], out_v[k] = results[k]
    for k in gate_names:
        out_g[k], out_d[k], out_m[k], out_v[k] = (a.reshape(W[k].shape) for a in gate_results[k])

    vec_names = tuple(k for k in _REPLICATED if k not in gate_names)
    n_vec = sum(W[k].shape[1] for k in vec_names)

    def vec_pack(rows, final, last):
        tail = jnp.concatenate([final.reshape(1, -1), jnp.broadcast_to(last.reshape(1, 1), (1, _LANES))], axis=1)
        tail = jnp.pad(tail, ((0, SUBLANES - L - 1), (0, n_vec - tail.shape[1])))
        return jnp.concatenate([rows, tail], axis=0)

    zero = jnp.zeros((1,), F32)
    g_rows = jnp.concatenate([jnp.concatenate([grads[l][k] for k in vec_names], axis=1) for l in range(L)], axis=0)
    land = _place(vec_pack(g_rows, dg_final, loss[0, :1]), me_idx, F32, name="place_vectors")
    send_sems, recv_sems, bufs, _ = _exchange_start(None, [land], name="vectors_start")
    (g_all,) = _exchange_wait(send_sems, recv_sems, bufs, scatter=False, after=None, name="vectors_wait")
    vec_out = _adamw(vec_pack(jnp.concatenate([W[k] for k in vec_names], axis=1), g_final, zero)[None],
                     vec_pack(jnp.concatenate([M[k] for k in vec_names], axis=1), m_g_final, zero)[None],
                     vec_pack(jnp.concatenate([V[k] for k in vec_names], axis=1), v_g_final, zero)[None],
                     g_all, None, 0, me_idx, name="adamw_vectors")
    vec_out = [a[0] for a in vec_out]
    for res, arr in zip((out_g, out_d, out_m, out_v), vec_out):
        off = 0
        for k in vec_names:
            res[k] = arr[:L, off:off + W[k].shape[1]]
            off += W[k].shape[1]
        res["g_final"] = arr[L, :g_final.size]
    loss_out = vec_out[0][L, g_final.size]

    return (loss_out, dx.reshape(NB, S, D), *[out_g[k] for k in _WEIGHTS], *[out_d[k] for k in _WEIGHTS],
            *[out_m[k] for k in _WEIGHTS], *[out_v[k] for k in _WEIGHTS])
```

```python
import functools

import jax
import jax.numpy as jnp
from jax import lax
from jax.experimental import pallas as pl
from jax.experimental.pallas import tpu as pltpu

F32 = jnp.float32
BF16 = jnp.bfloat16

EPS = 1e-6
LRU_C = 8.0
N_RNN_HEADS = 16
HEADS_PER_GROUP = 4
N_DEV = 8
ADAM_LR, ADAM_B1, ADAM_B2, ADAM_EPS, ADAM_WD, ADAM_STEP = 0.001, 0.9, 0.999, 1e-08, 0.01, 10

VMEM_LIMIT_BYTES = 48 * 1024 * 1024
CONV_PAD = 32
CONV_CHUNK = 128
SUBLANES = 8


def _cp(*sem):
    return pltpu.CompilerParams(dimension_semantics=sem, vmem_limit_bytes=VMEM_LIMIT_BYTES)


def _sig(x):
    return 1.0 / (1.0 + jnp.exp(-x))


def _gelu(x):
    c = 0.7978845608028654
    return 0.5 * x * (1.0 + jnp.tanh(c * (x + 0.044715 * x * x * x)))


def _gelu_grad(x):
    c = 0.7978845608028654
    th = jnp.tanh(c * (x + 0.044715 * x * x * x))
    return 0.5 * (1.0 + th) + 0.5 * x * (1.0 - th * th) * c * (1.0 + 3.0 * 0.044715 * x * x)


def _mm_nn(a, b, *, tm, tn, name, bias=None, resid=None, relu2=False, out_dtype=F32, a_resident=False):
    M, K = a.shape
    blocked = b.ndim == 3
    N = b.shape[0] * b.shape[2] if blocked else b.shape[1]
    tm = min(tm, M)
    tn = min(tn, N)
    if blocked:
        assert tn == b.shape[2]
    n_extra = (bias is not None) + (resid is not None)

    def body(*refs):
        acc = jnp.dot(refs[0][...].astype(BF16), refs[1][...].astype(BF16), preferred_element_type=F32)
        k = 2
        if bias is not None:
            acc = acc + refs[k][...]
            k += 1
        if resid is not None:
            acc = acc + refs[k][...]
            k += 1
        if relu2:
            p = jnp.maximum(acc, 0.0)
            acc = p * p
        refs[k][...] = acc.astype(out_dtype)

    def spec(shape, index):
        return pl.BlockSpec(shape, (lambda i, j: index(j, i)) if a_resident else index)

    in_specs = [spec((tm, K), lambda j, i: (i, 0))]
    if blocked:
        in_specs.append(spec((None, K, tn), lambda j, i: (j, 0, 0)))
    else:
        in_specs.append(spec((K, tn), lambda j, i: (0, j)))
    args = [a, b]
    if bias is not None:
        in_specs.append(spec((1, tn), lambda j, i: (0, j)))
        args.append(bias)
    if resid is not None:
        in_specs.append(spec((tm, tn), lambda j, i: (i, j)))
        args.append(resid)
    out_specs = spec((tm, tn), lambda j, i: (i, j))
    out_shape = jax.ShapeDtypeStruct((M, N), out_dtype)
    del n_extra
    grid = (M // tm, N // tn) if a_resident else (N // tn, M // tm)
    return pl.pallas_call(body, grid=grid, in_specs=in_specs, out_specs=out_specs,
                          out_shape=out_shape, compiler_params=_cp("parallel", "parallel"), name=name)(*args)


def _mm_nt(a, b, *, tm, tn, tk, name, mul_sqrt=None, resid=None, out_dtype=F32, dep=None):
    M, N = a.shape
    blocked = b.ndim == 3
    Kout = b.shape[1] if blocked else b.shape[0]
    tm = min(tm, M)
    tn = min(tn, Kout)
    tk = b.shape[2] if blocked else min(tk, N)
    nk = N // tk

    def body(*refs):
        acc_ref = refs[-1]
        kk = pl.program_id(2)
        part = lax.dot_general(refs[0][...].astype(BF16), refs[1][...].astype(BF16),
                               (((1,), (1,)), ((), ())), preferred_element_type=F32)

        @pl.when(kk == 0)
        def _():
            acc_ref[...] = part

        @pl.when(kk > 0)
        def _():
            acc_ref[...] += part

        @pl.when(kk == nk - 1)
        def _():
            acc = acc_ref[...]
            k = 2
            if mul_sqrt is not None:
                acc = acc * (2.0 * jnp.sqrt(refs[k][...].astype(F32)))
                k += 1
            if resid is not None:
                acc = acc + refs[k][...]
                k += 1
            if dep is not None:
                k += 1
            refs[k][...] = acc.astype(out_dtype)

    in_specs = [pl.BlockSpec((tm, tk), lambda i, j, k: (i, k))]
    if blocked:
        in_specs.append(pl.BlockSpec((None, tn, tk), lambda i, j, k: (k, j, 0)))
    else:
        in_specs.append(pl.BlockSpec((tn, tk), lambda i, j, k: (j, k)))
    args = [a, b]
    for extra in (mul_sqrt, resid):
        if extra is not None:
            in_specs.append(pl.BlockSpec((tm, tn), lambda i, j, k: (i, j)))
            args.append(extra)
    if dep is not None:
        in_specs.append(pl.BlockSpec(dep.shape, lambda i, j, k: (0, 0)))
        args.append(dep)
    return pl.pallas_call(body, grid=(M // tm, Kout // tn, nk), in_specs=in_specs,
                          out_specs=pl.BlockSpec((tm, tn), lambda i, j, k: (i, j)),
                          out_shape=jax.ShapeDtypeStruct((M, Kout), out_dtype),
                          scratch_shapes=[pltpu.VMEM((tm, tn), F32)],
                          compiler_params=_cp("parallel", "parallel", "arbitrary"), name=name)(*args)


def _mm_tn(a, b, *, tm, tn, tk, name, out_blocks=None, colsum=False, out_dtype=F32):
    T, M = a.shape
    N = b.shape[1]
    tm = min(tm, M)
    tk = min(tk, T)
    if out_blocks is not None:
        tn = N // out_blocks
        tm = M
    tn = min(tn, N)
    nk = T // tk
    if colsum:
        assert tm == M

    def body(*refs):
        a_ref, b_ref, o_ref, acc_ref = refs[0], refs[1], refs[2], refs[-1]
        kk = pl.program_id(2)
        bv = b_ref[...]
        part = lax.dot_general(a_ref[...].astype(BF16), bv.astype(BF16),
                               (((0,), (0,)), ((), ())), preferred_element_type=F32)

        if colsum:
            csum = jnp.broadcast_to(jnp.sum(bv.astype(F32), axis=0, keepdims=True), (SUBLANES, tn))

        @pl.when(kk == 0)
        def _():
            acc_ref[...] = part
            if colsum:
                refs[3][...] = csum

        @pl.when(kk > 0)
        def _():
            acc_ref[...] += part
            if colsum:
                refs[3][...] += csum

        @pl.when(kk == nk - 1)
        def _():
            o_ref[...] = acc_ref[...].astype(out_dtype)

    in_specs = [pl.BlockSpec((tk, tm), lambda i, j, k: (k, i)), pl.BlockSpec((tk, tn), lambda i, j, k: (k, j))]
    if out_blocks is not None:
        o_shape = jax.ShapeDtypeStruct((out_blocks, M, tn), out_dtype)
        o_spec = pl.BlockSpec((None, M, tn), lambda i, j, k: (j, 0, 0))
    else:
        o_shape = jax.ShapeDtypeStruct((M, N), out_dtype)
        o_spec = pl.BlockSpec((tm, tn), lambda i, j, k: (i, j))
    if colsum:
        out_shape = (o_shape, jax.ShapeDtypeStruct((SUBLANES, N), F32))
        out_specs = (o_spec, pl.BlockSpec((SUBLANES, tn), lambda i, j, k: (0, j)))
    else:
        out_shape, out_specs = o_shape, o_spec
    return pl.pallas_call(body, grid=(M // tm, N // tn, nk), in_specs=in_specs, out_specs=out_specs,
                          out_shape=out_shape, scratch_shapes=[pltpu.VMEM((tm, tn), F32)],
                          compiler_params=_cp("parallel", "parallel", "arbitrary"), name=name)(a, b)


def _rms_fwd(x, g, *, name, tr=512):
    T, D = x.shape
    tr = min(tr, T)

    def body(x_ref, g_ref, h_ref):
        xv = x_ref[...]
        r = lax.rsqrt(jnp.mean(xv * xv, axis=-1, keepdims=True) + EPS)
        h_ref[...] = (xv * r * g_ref[...]).astype(BF16)

    return pl.pallas_call(body, grid=(T // tr,),
                          in_specs=[pl.BlockSpec((tr, D), lambda i: (i, 0)), pl.BlockSpec((1, D), lambda i: (0, 0))],
                          out_specs=pl.BlockSpec((tr, D), lambda i: (i, 0)),
                          out_shape=jax.ShapeDtypeStruct((T, D), BF16), compiler_params=_cp("parallel"), name=name)(x, g)


def _rms_bwd(x, g, dh, dres, *, name, tr=512):
    T, D = x.shape
    tr = min(tr, T)

    def body(x_ref, g_ref, dh_ref, dres_ref, dx_ref, dg_ref):
        xv = x_ref[...]
        r = lax.rsqrt(jnp.mean(xv * xv, axis=-1, keepdims=True) + EPS)
        n = xv * r
        dh = dh_ref[...]
        dn = dh * g_ref[...]
        dx_ref[...] = dres_ref[...] + r * (dn - n * jnp.mean(dn * n, axis=-1, keepdims=True))
        part = jnp.sum(dh * n, axis=0, keepdims=True)

        @pl.when(pl.program_id(0) == 0)
        def _():
            dg_ref[...] = part

        @pl.when(pl.program_id(0) > 0)
        def _():
            dg_ref[...] += part

    row = pl.BlockSpec((tr, D), lambda i: (i, 0))
    vec = pl.BlockSpec((1, D), lambda i: (0, 0))
    return pl.pallas_call(body, grid=(T // tr,), in_specs=[row, vec, row, row], out_specs=(row, vec),
                          out_shape=(jax.ShapeDtypeStruct((T, D), F32), jax.ShapeDtypeStruct((1, D), F32)),
                          compiler_params=_cp("arbitrary"), name=name)(x, g, dh, dres)


def _final_loss(x, g, tgt, *, name, tr=512):
    T, D = x.shape
    tr = min(tr, T)

    def body(x_ref, g_ref, t_ref, loss_ref, dx_ref, dg_ref):
        xv = x_ref[...]
        gv = g_ref[...]
        r = lax.rsqrt(jnp.mean(xv * xv, axis=-1, keepdims=True) + EPS)
        n = xv * r
        e = n * gv - t_ref[...]
        lpart = 0.5 * jnp.sum(jnp.mean(e * e, axis=-1, keepdims=True), axis=0, keepdims=True)
        dy = e * (1.0 / D)
        dn = dy * gv
        dx_ref[...] = r * (dn - n * jnp.mean(dn * n, axis=-1, keepdims=True))
        gpart = jnp.sum(dy * n, axis=0, keepdims=True)

        @pl.when(pl.program_id(0) == 0)
        def _():
            dg_ref[...] = gpart
            loss_ref[...] = jnp.broadcast_to(lpart, (1, 128))

        @pl.when(pl.program_id(0) > 0)
        def _():
            dg_ref[...] += gpart
            loss_ref[...] += jnp.broadcast_to(lpart, (1, 128))

    row = pl.BlockSpec((tr, D), lambda i: (i, 0))
    vec = pl.BlockSpec((1, D), lambda i: (0, 0))
    return pl.pallas_call(body, grid=(T // tr,), in_specs=[row, vec, row],
                          out_specs=(pl.BlockSpec((1, 128), lambda i: (0, 0)), row, vec),
                          out_shape=(jax.ShapeDtypeStruct((1, 128), F32), jax.ShapeDtypeStruct((T, D), F32),
                                     jax.ShapeDtypeStruct((1, D), F32)),
                          compiler_params=_cp("arbitrary"), name=name)(x, g, tgt)


def _ln_silu_fwd(u, g, b, *, name, tr=512):
    T, C = u.shape
    tr = min(tr, T)

    def body(u_ref, g_ref, b_ref, o_ref):
        uv = u_ref[...]
        mu = jnp.mean(uv, axis=-1, keepdims=True)
        xc = uv - mu
        r = lax.rsqrt(jnp.mean(xc * xc, axis=-1, keepdims=True) + EPS)
        y = xc * r * g_ref[...] + b_ref[...]
        o_ref[...] = (y * _sig(y)).astype(BF16)

    row = pl.BlockSpec((tr, C), lambda i: (i, 0))
    vec = pl.BlockSpec((1, C), lambda i: (0, 0))
    return pl.pallas_call(body, grid=(T // tr,), in_specs=[row, vec, vec], out_specs=row,
                          out_shape=jax.ShapeDtypeStruct((T, C), BF16), compiler_params=_cp("parallel"),
                          name=name)(u, g, b)


def _ln_silu_bwd(u, g, b, do, *, name, tr=512):
    T, C = u.shape
    tr = min(tr, T)

    def body(u_ref, g_ref, b_ref, do_ref, du_ref, dg_ref, db_ref):
        uv = u_ref[...]
        gv = g_ref[...]
        mu = jnp.mean(uv, axis=-1, keepdims=True)
        xc = uv - mu
        r = lax.rsqrt(jnp.mean(xc * xc, axis=-1, keepdims=True) + EPS)
        n = xc * r
        y = n * gv + b_ref[...]
        s = _sig(y)
        dy = do_ref[...] * (s * (1.0 + y * (1.0 - s)))
        dn = dy * gv
        du_ref[...] = r * (dn - jnp.mean(dn, axis=-1, keepdims=True) - n * jnp.mean(dn * n, axis=-1, keepdims=True))
        gpart = jnp.sum(dy * n, axis=0, keepdims=True)
        bpart = jnp.sum(dy, axis=0, keepdims=True)

        @pl.when(pl.program_id(0) == 0)
        def _():
            dg_ref[...] = gpart
            db_ref[...] = bpart

        @pl.when(pl.program_id(0) > 0)
        def _():
            dg_ref[...] += gpart
            db_ref[...] += bpart

    row = pl.BlockSpec((tr, C), lambda i: (i, 0))
    vec = pl.BlockSpec((1, C), lambda i: (0, 0))
    return pl.pallas_call(body, grid=(T // tr,), in_specs=[row, vec, vec, row], out_specs=(row, vec, vec),
                          out_shape=(jax.ShapeDtypeStruct((T, C), F32), jax.ShapeDtypeStruct((1, C), F32),
                                     jax.ShapeDtypeStruct((1, C), F32)),
                          compiler_params=_cp("arbitrary"), name=name)(u, g, b, do)


def _merge_fwd(z, ya, yb, *, off_sa, off_sb, name, tr=512):
    T, D = ya.shape
    tr = min(tr, T)
    assert off_sa % D == 0 and off_sb % D == 0

    def body(sa_ref, sb_ref, ya_ref, yb_ref, m_ref):
        m_ref[...] = (_sig(sa_ref[...]) * ya_ref[...] + _sig(sb_ref[...]) * yb_ref[...]).astype(BF16)

    row = pl.BlockSpec((tr, D), lambda i: (i, 0))
    return pl.pallas_call(body, grid=(T // tr,),
                          in_specs=[pl.BlockSpec((tr, D), lambda i: (i, off_sa // D)),
                                    pl.BlockSpec((tr, D), lambda i: (i, off_sb // D)), row, row],
                          out_specs=row, out_shape=jax.ShapeDtypeStruct((T, D), BF16),
                          compiler_params=_cp("parallel"), name=name)(z, z, ya, yb)


def _columns_copy(stage_ref, dz_ref, row0, rows, col0, sem):
    dst = dz_ref.at[pl.ds(pl.multiple_of(row0, SUBLANES), rows),
                    pl.ds(pl.multiple_of(col0, 128), stage_ref.shape[1])]
    return pltpu.make_async_copy(stage_ref, dst, sem)


def _put_columns(stage_ref, dz_ref, row0, rows, col0, sem):
    cp = _columns_copy(stage_ref, dz_ref, row0, rows, col0, sem)
    cp.start()
    cp.wait()


def _merge_bwd(z, ya, yb, dm, dz, *, off_sa, off_sb, name, tr=512):
    T, D = ya.shape
    tr = min(tr, T)
    assert off_sb == off_sa + D

    def body(sa_ref, sb_ref, ya_ref, yb_ref, dm_ref, dz_in, dya_ref, dyb_ref, dz_ref, stage, sem):
        del dz_in
        dm = dm_ref[...]
        ga = _sig(sa_ref[...])
        gb = _sig(sb_ref[...])
        dya_ref[...] = (dm * ga).astype(BF16)
        dyb_ref[...] = (dm * gb).astype(BF16)
        stage[:, 0:D] = (dm * ya_ref[...] * ga * (1.0 - ga)).astype(BF16)
        stage[:, D:2 * D] = (dm * yb_ref[...] * gb * (1.0 - gb)).astype(BF16)
        _put_columns(stage, dz_ref, pl.program_id(0) * tr, tr, off_sa, sem)

    row = pl.BlockSpec((tr, D), lambda i: (i, 0))
    o = jax.ShapeDtypeStruct((T, D), BF16)
    return pl.pallas_call(body, grid=(T // tr,),
                          in_specs=[pl.BlockSpec((tr, D), lambda i: (i, off_sa // D)),
                                    pl.BlockSpec((tr, D), lambda i: (i, off_sb // D)), row, row, row, _ANY],
                          out_specs=(row, row, _ANY), out_shape=(o, o, jax.ShapeDtypeStruct(dz.shape, dz.dtype)),
                          scratch_shapes=[pltpu.VMEM((tr, 2 * D), BF16), pltpu.SemaphoreType.DMA],
                          input_output_aliases={5: 2},
                          compiler_params=_cp("parallel"), name=name)(z, z, ya, yb, dm, dz)


def _shift_rows(dst_ref, src_ref, r, total, back):
    for c0 in range(0, total - SUBLANES, CONV_CHUNK):
        n = min(CONV_CHUNK, total - SUBLANES - c0)
        if back:
            dst_ref[SUBLANES + c0:SUBLANES + c0 + n, :] = src_ref[SUBLANES + c0 - r:SUBLANES + c0 - r + n, :]
        else:
            dst_ref[c0:c0 + n, :] = src_ref[c0 + r:c0 + r + n, :]


def _tap_plan(K):
    if K <= SUBLANES:
        return [(0, [(s, K - 1 - s) for s in range(K)])]
    return [(r, [(SUBLANES * q, K - 1 - (SUBLANES * q + r)) for q in range(-(-K // SUBLANES)) if SUBLANES * q + r < K])
            for r in range(SUBLANES)]


def _conv_fwd(z, w, b, *, S, off_v, off_g, name, ct=256):
    T = z.shape[0]
    K, C = w.shape
    ct = min(ct, C)
    ch = min(CONV_CHUNK, S)
    glu = off_g is not None
    assert off_v % ct == 0 and (not glu or off_g % ct == 0)
    assert SUBLANES * ((K - 1) // SUBLANES) <= CONV_PAD - SUBLANES

    def body(*refs):
        if glu:
            v_ref, g_ref, w_ref, b_ref, o_ref, pad_ref, sh_ref = refs
        else:
            v_ref, w_ref, b_ref, o_ref, pad_ref, sh_ref = refs
        pad_ref[0:CONV_PAD, :] = jnp.zeros((CONV_PAD, ct), F32)
        if glu:
            pad_ref[CONV_PAD:CONV_PAD + S, :] = v_ref[...] * _sig(g_ref[...])
        else:
            pad_ref[CONV_PAD:CONV_PAD + S, :] = v_ref[...]
        for r, taps in _tap_plan(K):
            src = pad_ref
            if r > 0:
                _shift_rows(sh_ref, pad_ref, r, CONV_PAD + S, True)
                src = sh_ref
            for l0 in range(0, ct, 128):
                lanes = slice(l0, l0 + 128)
                for c in range(S // ch):
                    acc = None
                    for off, wrow in taps:
                        st = CONV_PAD + c * ch - off
                        term = w_ref[wrow:wrow + 1, lanes] * src[st:st + ch, lanes]
                        acc = term if acc is None else acc + term
                    rows = slice(c * ch, (c + 1) * ch)
                    if r == 0:
                        o_ref[rows, lanes] = acc + b_ref[:, lanes]
                    else:
                        o_ref[rows, lanes] += acc

    in_specs = [pl.BlockSpec((S, ct), lambda j, bb: (bb, off_v // ct + j))]
    args = [z]
    if glu:
        in_specs.append(pl.BlockSpec((S, ct), lambda j, bb: (bb, off_g // ct + j)))
        args.append(z)
    in_specs += [pl.BlockSpec((K, ct), lambda j, bb: (0, j)), pl.BlockSpec((1, ct), lambda j, bb: (0, j))]
    args += [w, b]
    return pl.pallas_call(body, grid=(C // ct, T // S), in_specs=in_specs,
                          out_specs=pl.BlockSpec((S, ct), lambda j, bb: (bb, j)),
                          out_shape=jax.ShapeDtypeStruct((T, C), F32),
                          scratch_shapes=[pltpu.VMEM((CONV_PAD + S, ct), F32), pltpu.VMEM((CONV_PAD + S, ct), F32)],
                          compiler_params=_cp("parallel", "parallel"), name=name)(*args)


def _conv_bwd(z, w, dy, dz, *, S, off_v, off_g, name, ct=256):
    T = z.shape[0]
    K, C = w.shape
    KP = -(-K // SUBLANES) * SUBLANES
    ct = min(ct, C)
    ch = min(CONV_CHUNK, S)
    glu = off_g is not None
    total = S + CONV_PAD

    def body(*refs):
        if glu:
            (v_ref, g_ref, w_ref, dy_ref, dz_in, dz_ref, dw_ref, db_ref,
             pad_ref, sh_ref, padb_ref, shb_ref, du_ref, stage_v, stage_g, sem) = refs
        else:
            (v_ref, w_ref, dy_ref, dz_in, dz_ref, dw_ref, db_ref,
             pad_ref, sh_ref, padb_ref, shb_ref, du_ref, stage_v, sem) = refs
        del dz_in
        j, bb = pl.program_id(0), pl.program_id(1)
        pad_ref[0:CONV_PAD, :] = jnp.zeros((CONV_PAD, ct), F32)
        if glu:
            pad_ref[CONV_PAD:total, :] = v_ref[...] * _sig(g_ref[...])
        else:
            pad_ref[CONV_PAD:total, :] = v_ref[...]
        padb_ref[0:S, :] = dy_ref[...]
        padb_ref[S:total, :] = jnp.zeros((CONV_PAD, ct), F32)

        @pl.when(bb == 0)
        def _():
            dw_ref[...] = jnp.zeros((KP, ct), F32)
            db_ref[...] = jnp.zeros((1, ct), F32)

        for r, taps in _tap_plan(K):
            u_src, d_src = pad_ref, padb_ref
            if r > 0:
                _shift_rows(sh_ref, pad_ref, r, total, True)
                _shift_rows(shb_ref, padb_ref, r, total, False)
                u_src, d_src = sh_ref, shb_ref
            for l0 in range(0, ct, 128):
                lanes = slice(l0, l0 + 128)
                for c in range(S // ch):
                    acc = None
                    for off, wrow in taps:
                        st = c * ch + off
                        term = w_ref[wrow:wrow + 1, lanes] * d_src[st:st + ch, lanes]
                        acc = term if acc is None else acc + term
                    rows = slice(c * ch, (c + 1) * ch)
                    if r == 0:
                        du_ref[rows, lanes] = acc
                    else:
                        du_ref[rows, lanes] += acc
                for off, wrow in taps:
                    acc = None
                    for c in range(S // ch):
                        st = CONV_PAD + c * ch - off
                        prod = padb_ref[c * ch:(c + 1) * ch, lanes] * u_src[st:st + ch, lanes]
                        acc = prod if acc is None else acc + prod
                    dw_ref[wrow:wrow + 1, lanes] += jnp.sum(acc, axis=0, keepdims=True)
        db_ref[...] += jnp.sum(dy_ref[...], axis=0, keepdims=True)
        for l0 in range(0, ct, 128):
            lanes = slice(l0, l0 + 128)
            for c in range(S // ch):
                rows = slice(c * ch, (c + 1) * ch)
                du = du_ref[rows, lanes]
                if glu:
                    sg = _sig(g_ref[rows, lanes])
                    stage_v[rows, lanes] = (du * sg).astype(BF16)
                    stage_g[rows, lanes] = (du * v_ref[rows, lanes] * sg * (1.0 - sg)).astype(BF16)
                else:
                    stage_v[rows, lanes] = du.astype(BF16)
        _put_columns(stage_v, dz_ref, bb * S, S, off_v + j * ct, sem)
        if glu:
            _put_columns(stage_g, dz_ref, bb * S, S, off_g + j * ct, sem)

    blk = lambda off: pl.BlockSpec((S, ct), lambda j, bb: (bb, off // ct + j))
    in_specs = [blk(off_v)]
    args = [z]
    if glu:
        in_specs.append(blk(off_g))
        args.append(z)
    in_specs += [pl.BlockSpec((K, ct), lambda j, bb: (0, j)), blk(0), _ANY]
    args += [w, dy, dz]
    out_shape = (jax.ShapeDtypeStruct(dz.shape, dz.dtype), jax.ShapeDtypeStruct((KP, C), F32),
                 jax.ShapeDtypeStruct((1, C), F32))
    out_specs = (_ANY, pl.BlockSpec((KP, ct), lambda j, bb: (0, j)), pl.BlockSpec((1, ct), lambda j, bb: (0, j)))
    padded = pltpu.VMEM((total, ct), F32)
    stage = pltpu.VMEM((S, ct), BF16)
    return pl.pallas_call(body, grid=(C // ct, T // S), in_specs=in_specs, out_specs=out_specs, out_shape=out_shape,
                          scratch_shapes=[padded, padded, padded, padded, pltpu.VMEM((S, ct), F32), stage]
                          + ([stage] if glu else []) + [pltpu.SemaphoreType.DMA],
                          input_output_aliases={len(args) - 1: 0},
                          compiler_params=_cp("parallel", "arbitrary"), name=name)(*args)


def _softplus_neg(lam):
    return jnp.maximum(-lam, 0.0) + jnp.log1p(jnp.exp(-jnp.abs(lam)))


def _neg_expm1(x):
    u = jnp.exp(x)
    um1 = u - 1.0
    lg = jnp.log(u)
    safe = jnp.where(lg == 0.0, 1.0, lg)
    em1 = jnp.where(um1 == 0.0, x, jnp.where(um1 == -1.0, -1.0, um1 * x / safe))
    return -em1


def _gates_fwd(v, wa, wx, ba, bx, lam, *, S, name, tm=512):
    T, C = v.shape
    G, gw, _ = wa.shape
    tm = min(tm, T)

    def body(v_ref, wa_ref, wx_ref, ba_ref, bx_ref, lam_ref, r_ref, i_ref, a_ref, bt_ref):
        vv = v_ref[...]
        vb = vv.astype(BF16)
        r = _sig(jnp.dot(vb, wa_ref[...], preferred_element_type=F32) + ba_ref[...])
        ig = _sig(jnp.dot(vb, wx_ref[...], preferred_element_type=F32) + bx_ref[...])
        log_a = -LRU_C * r * _softplus_neg(lam_ref[...])
        a = jnp.exp(log_a)
        mult = jnp.sqrt(_neg_expm1(2.0 * log_a))
        row = pl.program_id(0) * tm + lax.broadcasted_iota(jnp.int32, (tm, gw), 0)
        mult = jnp.where(row % S == 0, 1.0, mult)
        r_ref[...] = r
        i_ref[...] = ig
        a_ref[...] = a
        bt_ref[...] = mult * ig * vv

    blk = pl.BlockSpec((tm, gw), lambda i, g: (i, g))
    wsp = pl.BlockSpec((None, gw, gw), lambda i, g: (g, 0, 0))
    vec = pl.BlockSpec((1, gw), lambda i, g: (0, g))
    o = jax.ShapeDtypeStruct((T, C), F32)
    return pl.pallas_call(body, grid=(T // tm, G), in_specs=[blk, wsp, wsp, vec, vec, vec],
                          out_specs=(blk, blk, blk, blk), out_shape=(o, o, o, o),
                          compiler_params=_cp("parallel", "parallel"), name=name)(v, wa, wx, ba, bx, lam)


def _scan_fwd(a, bt, z, *, S, off_gate, name, ct=512, tt=256):
    T, C = a.shape
    ct = min(ct, C)
    tt = min(tt, S)
    nt = S // tt
    assert off_gate % ct == 0

    def body(a_ref, b_ref, g_ref, h_ref, hb_ref, carry_ref):
        row = lax.broadcasted_iota(jnp.int32, (SUBLANES, ct), 0)

        @pl.when(pl.program_id(2) == 0)
        def _():
            carry_ref[...] = jnp.zeros((SUBLANES, ct), F32)

        def step(i, carry):
            st = pl.multiple_of(i * SUBLANES, SUBLANES)
            A = a_ref[pl.ds(st, SUBLANES), :]
            B = b_ref[pl.ds(st, SUBLANES), :]
            for d in (1, 2, 4):
                m = row >= d
                Bn = jnp.where(m, A * pltpu.roll(B, d, 0) + B, B)
                A = jnp.where(m, A * pltpu.roll(A, d, 0), A)
                B = Bn
            h = B + A * carry
            h_ref[pl.ds(st, SUBLANES), :] = h
            return jnp.broadcast_to(h[SUBLANES - 1:SUBLANES, :], (SUBLANES, ct))

        carry_ref[...] = lax.fori_loop(0, tt // SUBLANES, step, carry_ref[...], unroll=2)
        hb_ref[...] = (h_ref[...] * _gelu(g_ref[...])).astype(BF16)

    blk = pl.BlockSpec((tt, ct), lambda j, bb, t: (bb * nt + t, j))
    return pl.pallas_call(body, grid=(C // ct, T // S, nt),
                          in_specs=[blk, blk, pl.BlockSpec((tt, ct), lambda j, bb, t: (bb * nt + t, off_gate // ct + j))],
                          out_specs=(blk, blk),
                          out_shape=(jax.ShapeDtypeStruct((T, C), F32), jax.ShapeDtypeStruct((T, C), BF16)),
                          scratch_shapes=[pltpu.VMEM((SUBLANES, ct), F32)],
                          compiler_params=_cp("parallel", "parallel", "arbitrary"), name=name)(a, bt, z)


def _scan_bwd(a, h, z, dhb, dz, *, S, off_gate, name, ct=512, tt=256):
    T, C = a.shape
    ct = min(ct, C)
    tt = min(tt, S)
    nt = S // tt
    n_tiles = tt // SUBLANES

    def body(a_ref, h_ref, hp_ref, g_ref, dhb_ref, dz_in, G_ref, da_ref, dz_ref, dh_ref, carry_ref, stage, sem):
        del dz_in
        row = lax.broadcasted_iota(jnp.int32, (SUBLANES, ct), 0)
        tb = nt - 1 - pl.program_id(2)

        @pl.when(pl.program_id(2) == 0)
        def _():
            carry_ref[...] = jnp.zeros((SUBLANES, ct), F32)

        gate = g_ref[...]
        dhb = dhb_ref[...]
        dh_ref[...] = dhb * _gelu(gate)
        stage[...] = (dhb * h_ref[...] * _gelu_grad(gate)).astype(BF16)
        put = _columns_copy(stage, dz_ref, (pl.program_id(1) * nt + tb) * tt, tt, off_gate + pl.program_id(0) * ct, sem)
        put.start()
        h_before = jnp.where(tb > 0, jnp.broadcast_to(hp_ref[SUBLANES - 1:SUBLANES, :], (SUBLANES, ct)), 0.0)

        def step(k, qcarry):
            i = n_tiles - 1 - k
            st = pl.multiple_of(i * SUBLANES, SUBLANES)
            stp = pl.multiple_of(jnp.maximum(i - 1, 0) * SUBLANES, SUBLANES)
            A = a_ref[pl.ds(st, SUBLANES), :]
            hv = h_ref[pl.ds(st, SUBLANES), :]
            hprev_tile = h_ref[pl.ds(stp, SUBLANES), :]
            dh = dh_ref[pl.ds(st, SUBLANES), :]
            Aq = A
            Bq = A * dh
            for d in (1, 2, 4):
                m = row < SUBLANES - d
                Bn = jnp.where(m, Aq * pltpu.roll(Bq, SUBLANES - d, 0) + Bq, Bq)
                Aq = jnp.where(m, Aq * pltpu.roll(Aq, SUBLANES - d, 0), Aq)
                Bq = Bn
            q = Bq + Aq * qcarry
            qnext = jnp.where(row == SUBLANES - 1, qcarry, pltpu.roll(q, SUBLANES - 1, 0))
            g = dh + qnext
            hlast = jnp.where(i > 0, jnp.broadcast_to(hprev_tile[SUBLANES - 1:SUBLANES, :], (SUBLANES, ct)), h_before)
            hprev = jnp.where(row == 0, hlast, pltpu.roll(hv, 1, 0))
            G_ref[pl.ds(st, SUBLANES), :] = g
            da_ref[pl.ds(st, SUBLANES), :] = g * hprev
            return jnp.broadcast_to(q[0:1, :], (SUBLANES, ct))

        carry_ref[...] = lax.fori_loop(0, n_tiles, step, carry_ref[...], unroll=2)
        put.wait()

    tiles_per_block = tt // SUBLANES
    blk = pl.BlockSpec((tt, ct), lambda j, bb, t: (bb * nt + nt - 1 - t, j))
    before = pl.BlockSpec((SUBLANES, ct),
                          lambda j, bb, t: (jnp.maximum((bb * nt + nt - 1 - t) * tiles_per_block - 1, 0), j))
    o = jax.ShapeDtypeStruct((T, C), F32)
    return pl.pallas_call(body, grid=(C // ct, T // S, nt),
                          in_specs=[blk, blk, before,
                                    pl.BlockSpec((tt, ct), lambda j, bb, t: (bb * nt + nt - 1 - t, off_gate // ct + j)), blk,
                                    _ANY],
                          out_specs=(blk, blk, _ANY), out_shape=(o, o, jax.ShapeDtypeStruct(dz.shape, dz.dtype)),
                          scratch_shapes=[pltpu.VMEM((tt, ct), F32), pltpu.VMEM((SUBLANES, ct), F32),
                                          pltpu.VMEM((tt, ct), BF16), pltpu.SemaphoreType.DMA],
                          input_output_aliases={5: 2},
                          compiler_params=_cp("parallel", "parallel", "arbitrary"), name=name)(a, h, h, z, dhb, dz)


def _gates_bwd(v, r, ig, Gb, da, lam, *, S, name, tr=256):
    T, C = v.shape
    tr = min(tr, T)

    def body(v_ref, r_ref, i_ref, G_ref, da_ref, lam_ref, dpr_ref, dpi_ref, dv_ref, dsp_ref, dba_ref, dbx_ref):
        vv, r, ig, Gv = v_ref[...], r_ref[...], i_ref[...], G_ref[...]
        sp = _softplus_neg(lam_ref[...])
        log_a = -LRU_C * r * sp
        a = jnp.exp(log_a)
        mult_raw = jnp.sqrt(_neg_expm1(2.0 * log_a))
        row = pl.program_id(0) * tr + lax.broadcasted_iota(jnp.int32, (tr, C), 0)
        start = row % S == 0
        mult = jnp.where(start, 1.0, mult_raw)
        dmult = jnp.where(start, 0.0, Gv * ig * vv)
        di = Gv * mult * vv
        dv_ref[...] = Gv * mult * ig
        dla = da_ref[...] * a - dmult * (a * a) / jnp.where(start, 1.0, mult_raw)
        dr = dla * (-LRU_C) * sp
        dpr = dr * r * (1.0 - r)
        dpi = di * ig * (1.0 - ig)
        dpr_ref[...] = dpr.astype(BF16)
        dpi_ref[...] = dpi.astype(BF16)
        p_sp = jnp.sum(dla * (-LRU_C) * r, axis=0, keepdims=True) * (-_sig(-lam_ref[...]))
        p_a = jnp.sum(dpr, axis=0, keepdims=True)
        p_x = jnp.sum(dpi, axis=0, keepdims=True)

        @pl.when(pl.program_id(0) == 0)
        def _():
            dsp_ref[...] = p_sp
            dba_ref[...] = p_a
            dbx_ref[...] = p_x

        @pl.when(pl.program_id(0) > 0)
        def _():
            dsp_ref[...] += p_sp
            dba_ref[...] += p_a
            dbx_ref[...] += p_x

    rowb = pl.BlockSpec((tr, C), lambda i: (i, 0))
    vec = pl.BlockSpec((1, C), lambda i: (0, 0))
    ob = jax.ShapeDtypeStruct((T, C), BF16)
    ov = jax.ShapeDtypeStruct((1, C), F32)
    return pl.pallas_call(body, grid=(T // tr,), in_specs=[rowb, rowb, rowb, rowb, rowb, vec],
                          out_specs=(rowb, rowb, rowb, vec, vec, vec),
                          out_shape=(ob, ob, jax.ShapeDtypeStruct((T, C), F32), ov, ov, ov),
                          compiler_params=_cp("arbitrary"), name=name)(v, r, ig, Gb, da, lam)


def _gates_dgrad(dpr, dpi, wa, wx, dv_direct, *, name, tm=512, dep=None):
    T, C = dpr.shape
    G, gw, _ = wa.shape
    tm = min(tm, T)

    def body(dpr_ref, dpi_ref, wa_ref, wx_ref, dvd_ref, *rest):
        dv_ref = rest[-1]
        dn = (((1,), (1,)), ((), ()))
        dv_ref[...] = (dvd_ref[...]
                       + lax.dot_general(dpr_ref[...], wa_ref[...], dn, preferred_element_type=F32)
                       + lax.dot_general(dpi_ref[...], wx_ref[...], dn, preferred_element_type=F32))

    blk = pl.BlockSpec((tm, gw), lambda i, g: (i, g))
    wsp = pl.BlockSpec((None, gw, gw), lambda i, g: (g, 0, 0))
    in_specs, args = [blk, blk, wsp, wsp, blk], [dpr, dpi, wa, wx, dv_direct]
    if dep is not None:
        in_specs.append(pl.BlockSpec(dep.shape, lambda i, g: (0, 0)))
        args.append(dep)
    return pl.pallas_call(body, grid=(T // tm, G), in_specs=in_specs, out_specs=blk,
                          out_shape=jax.ShapeDtypeStruct((T, C), F32),
                          compiler_params=_cp("parallel", "parallel"), name=name)(*args)


def _gates_wgrad(v, dpr, dpi, *, G, name, tk=512):
    T, C = v.shape
    gw = C // G
    hd = gw // HEADS_PER_GROUP
    tk = min(tk, T)
    nk = T // tk

    def body(v_ref, dpr_ref, dpi_ref, dwa_ref, dwx_ref, acc_a, acc_x):
        dn = (((0,), (0,)), ((), ()))
        vb = v_ref[...].astype(BF16)
        pa = lax.dot_general(vb, dpr_ref[...], dn, preferred_element_type=F32)
        px = lax.dot_general(vb, dpi_ref[...], dn, preferred_element_type=F32)

        @pl.when(pl.program_id(1) == 0)
        def _():
            acc_a[...] = pa
            acc_x[...] = px

        @pl.when(pl.program_id(1) > 0)
        def _():
            acc_a[...] += pa
            acc_x[...] += px

        @pl.when(pl.program_id(1) == nk - 1)
        def _():
            for h in range(HEADS_PER_GROUP):
                dwa_ref[h] = acc_a[h * hd:(h + 1) * hd, h * hd:(h + 1) * hd]
                dwx_ref[h] = acc_x[h * hd:(h + 1) * hd, h * hd:(h + 1) * hd]

    blk = pl.BlockSpec((tk, gw), lambda g, k: (k, g))
    wsp = pl.BlockSpec((HEADS_PER_GROUP, hd, hd), lambda g, k: (g, 0, 0))
    o = jax.ShapeDtypeStruct((G * HEADS_PER_GROUP, hd, hd), F32)
    return pl.pallas_call(body, grid=(G, nk), in_specs=[blk, blk, blk], out_specs=(wsp, wsp), out_shape=(o, o),
                          scratch_shapes=[pltpu.VMEM((gw, gw), F32), pltpu.VMEM((gw, gw), F32)],
                          compiler_params=_cp("parallel", "arbitrary"), name=name)(v, dpr, dpi)


def _group_weights(w):
    H, hd, _ = w.shape
    G = H // HEADS_PER_GROUP
    eye = jnp.eye(HEADS_PER_GROUP, dtype=w.dtype)
    wg = jnp.einsum("ghij,hk->ghikj", w.reshape(G, HEADS_PER_GROUP, hd, hd), eye)
    return wg.reshape(G, HEADS_PER_GROUP * hd, HEADS_PER_GROUP * hd).astype(BF16)


def _layer_fwd(x, p, *, S, fetch=None):
    D = x.shape[1]
    Dc = p["conv_a_b"].shape[1]
    Dr = p["conv_b_b"].shape[1]
    offs = dict(va=0, ga=Dc, xb=2 * Dc, gb=2 * Dc + Dr, sa=2 * Dc + 2 * Dr, sb=2 * Dc + 2 * Dr + D)
    h = _rms_fwd(x, p["g_mix"], name="rms_mix_fwd")
    if fetch is not None:
        fetch("in", h)
    z = _mm_nn(h, p["w_in"], tm=1024, tn=p["w_in"].shape[2], bias=p["b_in"], a_resident=True, name="mm_in_fwd")
    if fetch is not None:
        fetch("mix", z)
    u1 = _conv_fwd(z, p["conv_a_w"], p["conv_a_b"], S=S, off_v=offs["va"], off_g=offs["ga"], name="conv_a_fwd")
    u2 = _ln_silu_fwd(u1, p["ln_g"], p["ln_b"], name="ln_silu_fwd")
    ya = _mm_nn(u2, p["w_a_out"], tm=1024, tn=1024, name="mm_a_out_fwd")
    v0 = _conv_fwd(z, p["conv_b_w"], p["conv_b_b"], S=S, off_v=offs["xb"], off_g=None, name="conv_b_fwd")
    r, ig, a, bt = _gates_fwd(v0, p["wg_a"], p["wg_x"], p["b_rg_a"], p["b_rg_x"], p["lam"], S=S, name="gates_fwd")
    hs, hb = _scan_fwd(a, bt, z, S=S, off_gate=offs["gb"], name="scan_fwd")
    yb = _mm_nn(hb, p["w_b_out"], tm=1024, tn=1024, name="mm_b_out_fwd")
    m = _merge_fwd(z, ya, yb, off_sa=offs["sa"], off_sb=offs["sb"], name="merge_fwd")
    x_mid = _mm_nn(m, p["w_o"], tm=1024, tn=1024, resid=x, name="mm_o_fwd")
    h2 = _rms_fwd(x_mid, p["g_mlp"], name="rms_mlp_fwd")
    if fetch is not None:
        fetch("mlp", h2)
    f = _mm_nn(h2, p["w_1"], tm=1024, tn=p["w_1"].shape[2], relu2=True, out_dtype=BF16, a_resident=True,
               name="mm_1_fwd")
    x_next = _mm_nn(f, p["w_2"], tm=512, tn=1024, resid=x_mid, name="mm_2_fwd")
    saved = dict(x=x, h=h, z=z, u1=u1, u2=u2, ya=ya, v0=v0, r=r, ig=ig, a=a, hs=hs, hb=hb, yb=yb, m=m,
                 x_mid=x_mid, h2=h2, f=f, offs=offs)
    return x_next, saved


def _layer_bwd_mlp(dx, p, sv, *, wdt, dep=None):
    g = {}
    g["w_2"] = _mm_tn(sv["f"], dx, tm=2048, tn=1024, tk=1024, out_dtype=wdt, name="mm_2_wgrad")
    dfp = _mm_nt(dx, p["w_2"], tm=1024, tn=1024, tk=1024, mul_sqrt=sv["f"], out_dtype=BF16, dep=dep,
                 name="mm_2_dgrad")
    g["w_1"] = _mm_tn(sv["h2"], dfp, tm=1024, tn=512, tk=1024, out_blocks=p["w_1"].shape[0], out_dtype=wdt,
                      name="mm_1_wgrad")
    dh2 = _mm_nt(dfp, p["w_1"], tm=1024, tn=1024, tk=512, name="mm_1_dgrad")
    dx_mid, g["g_mlp"] = _rms_bwd(sv["x_mid"], p["g_mlp"], dh2, dx, name="rms_mlp_bwd")
    return dx_mid, g


def _layer_bwd_mix(dx_mid, p, sv, *, S, wdt, dep=None, on_gate_grads=None, on_weight_grads=None):
    offs = sv["offs"]
    g = {}
    g["w_o"] = _mm_tn(sv["m"], dx_mid, tm=1024, tn=1024, tk=1024, out_dtype=wdt, name="mm_o_wgrad")
    dm = _mm_nt(dx_mid, p["w_o"], tm=1024, tn=1024, tk=1024, dep=dep, name="mm_o_dgrad")
    dz = lax.empty(sv["z"].shape, BF16)
    dya, dyb, dz = _merge_bwd(sv["z"], sv["ya"], sv["yb"], dm, dz, off_sa=offs["sa"], off_sb=offs["sb"], name="merge_bwd")
    g["w_b_out"] = _mm_tn(sv["hb"], dyb, tm=1536, tn=1024, tk=1024, out_dtype=wdt, name="mm_b_out_wgrad")
    dhb = _mm_nt(dyb, p["w_b_out"], tm=1024, tn=1536, tk=1024, name="mm_b_out_dgrad")
    Gb, da, dz = _scan_bwd(sv["a"], sv["hs"], sv["z"], dhb, dz, S=S, off_gate=offs["gb"], name="scan_bwd")
    dpr, dpi, dv_direct, g["lam"], g["b_rg_a"], g["b_rg_x"] = _gates_bwd(sv["v0"], sv["r"], sv["ig"], Gb, da, p["lam"], S=S,
                                                                    name="gates_bwd")
    g["w_rg_a"], g["w_rg_x"] = _gates_wgrad(sv["v0"], dpr, dpi, G=p["wg_a"].shape[0], name="gates_wgrad")
    dep_gates = on_gate_grads(g) if on_gate_grads is not None else None
    dv0 = _gates_dgrad(dpr, dpi, p["wg_a"], p["wg_x"], dv_direct, dep=dep_gates, name="gates_dgrad")
    dz, g["conv_b_w"], g["conv_b_b"] = _conv_bwd(sv["z"], p["conv_b_w"], dv0, dz, S=S, off_v=offs["xb"], off_g=None,
                                                 name="conv_b_bwd")
    g["w_a_out"] = _mm_tn(sv["u2"], dya, tm=1024, tn=1024, tk=1024, out_dtype=wdt, name="mm_a_out_wgrad")
    du2 = _mm_nt(dya, p["w_a_out"], tm=1024, tn=1024, tk=1024, name="mm_a_out_dgrad")
    du1, g["ln_g"], g["ln_b"] = _ln_silu_bwd(sv["u1"], p["ln_g"], p["ln_b"], du2, name="ln_silu_bwd")
    dz, g["conv_a_w"], g["conv_a_b"] = _conv_bwd(sv["z"], p["conv_a_w"], du1, dz, S=S, off_v=offs["va"],
                                                 off_g=offs["ga"], name="conv_a_bwd")
    g["w_in"], db_in = _mm_tn(sv["h"], dz, tm=1024, tn=512, tk=1024, out_blocks=p["w_in"].shape[0], colsum=True,
                              out_dtype=wdt, name="mm_in_wgrad")
    g["b_in"] = db_in[:1]
    dep_in = on_weight_grads(g) if on_weight_grads is not None else None
    dh = _mm_nt(dz, p["w_in"], tm=1024, tn=1024, tk=512, dep=dep_in, name="mm_in_dgrad")
    dx_in, g["g_mix"] = _rms_bwd(sv["x"], p["g_mix"], dh, dx_mid, name="rms_mix_bwd")
    return dx_in, g


def _layer_bwd(dx, p, sv, *, S, wdt=F32):
    dx_mid, g = _layer_bwd_mlp(dx, p, sv, wdt=wdt)
    dx_in, g2 = _layer_bwd_mix(dx_mid, p, sv, S=S, wdt=wdt)
    g.update(g2)
    return dx_in, g


def _local_step(x, tgt, layers, g_final, *, S, wdt=F32):
    saved = []
    for p in layers:
        x, sv = _layer_fwd(x, p, S=S)
        saved.append(sv)
    loss, dx, dg_final = _final_loss(x, g_final, tgt, name="final_loss")
    grads = [None] * len(layers)
    for l in reversed(range(len(layers))):
        dx, grads[l] = _layer_bwd(dx, layers[l], saved[l], S=S, wdt=wdt)
    return loss, dx, grads, dg_final


_HBM = pl.BlockSpec(memory_space=pltpu.HBM)
_MESH = pl.DeviceIdType.MESH


_SEM = pl.BlockSpec(memory_space=pltpu.SEMAPHORE)
_ANY = pl.BlockSpec(memory_space=pl.ANY)
_FLIPS = [(dx, dy, dc) for dx in (0, 1) for dy in (0, 1) for dc in (0, 1)][1:]


def _place(shard, me_idx, dtype, *, name):
    r, cc = shard.shape
    tr = 512 if r % 512 == 0 else r

    def body(me_ref, s_ref, o_ref):
        del me_ref
        o_ref[...] = s_ref[...].astype(dtype)

    grid_spec = pltpu.PrefetchScalarGridSpec(
        num_scalar_prefetch=1, grid=(r // tr,),
        in_specs=[pl.BlockSpec((tr, cc), lambda i, me: (i, 0))],
        out_specs=pl.BlockSpec((None, tr, cc), lambda i, me: (me[0], i, 0)))
    return pl.pallas_call(body, grid_spec=grid_spec, out_shape=jax.ShapeDtypeStruct((N_DEV, r, cc), dtype),
                          compiler_params=_cp("arbitrary"), name=name)(me_idx, shard)


def _exchange_copies(srcs, lands, send_sems, recv_sems):
    x, y, c = lax.axis_index("x"), lax.axis_index("y"), lax.axis_index("c")
    me = 4 * x + 2 * y + c
    pairs = []
    for k, (dx, dy, dc) in enumerate(_FLIPS):
        peer = (1 - x if dx else x, 1 - y if dy else y, 1 - c if dc else c)
        pidx = 4 * peer[0] + 2 * peer[1] + peer[2]
        for a, land in enumerate(lands):
            src = land.at[me] if srcs is None else srcs[a].at[pidx]
            sem = k * len(lands) + a
            out = pltpu.make_async_remote_copy(src_ref=src, dst_ref=land.at[me], send_sem=send_sems.at[sem],
                                               recv_sem=recv_sems.at[sem], device_id=peer, device_id_type=_MESH)
            arrival = pltpu.make_async_remote_copy(src_ref=src, dst_ref=land.at[pidx], send_sem=send_sems.at[sem],
                                                   recv_sem=recv_sems.at[sem], device_id=peer, device_id_type=_MESH)
            pairs.append((out, arrival))
    return pairs


def _exchange_start(srcs, lands, *, name):
    n = len(lands)
    bufs = list(lands) if srcs is None else list(srcs) + list(lands)
    nb = len(bufs)

    def body(*refs):
        ins = refs[:nb]
        send_sems, recv_sems = refs[nb], refs[nb + 1]
        token = refs[-1]
        for out, _ in _exchange_copies(None if srcs is None else ins[:n], ins[nb - n:], send_sems, recv_sems):
            out.start()
        token[...] = jnp.zeros_like(token)

    sems = pltpu.SemaphoreType.DMA((len(_FLIPS) * n,))
    res = pl.pallas_call(
        body, name=name, in_specs=[_HBM] * nb,
        out_shape=(sems, sems, *[pltpu.HBM(b.shape, b.dtype) for b in bufs], jax.ShapeDtypeStruct((SUBLANES, 128), F32)),
        out_specs=(_SEM, _SEM, *[_HBM] * nb, pl.BlockSpec(memory_space=pltpu.VMEM)),
        input_output_aliases={i: 2 + i for i in range(nb)},
        compiler_params=pltpu.CompilerParams(has_side_effects=pltpu.SideEffectType.DATAFLOW_SIDE_EFFECTING),
    )(*[pltpu.with_memory_space_constraint(b, pltpu.HBM) for b in bufs])
    return res[0], res[1], list(res[2:2 + nb]), res[-1]


def _exchange_wait(send_sems, recv_sems, bufs, *, scatter, after, name):
    nb = len(bufs)
    n = nb // 2 if scatter else nb

    def body(*refs):
        ins = refs[:nb]
        for out, arrival in _exchange_copies(ins[:n] if scatter else None, ins[nb - n:], refs[nb], refs[nb + 1]):
            out.wait_send()
            arrival.wait_recv()

    extra = [] if after is None else [after]
    res = pl.pallas_call(
        body, name=name, in_specs=[_HBM] * nb + [_SEM, _SEM] + [_ANY] * len(extra),
        out_shape=tuple(pltpu.HBM(b.shape, b.dtype) for b in bufs), out_specs=tuple([_HBM] * nb),
        input_output_aliases={i: i for i in range(nb)},
        compiler_params=pltpu.CompilerParams(has_side_effects=pltpu.SideEffectType.DATAFLOW_SIDE_EFFECTING),
    )(*bufs, send_sems, recv_sems, *extra)
    return list(res)


def _adamw_math(w, g, m, v):
    m = ADAM_B1 * m + (1.0 - ADAM_B1) * g
    v = ADAM_B2 * v + (1.0 - ADAM_B2) * (g * g)
    m_hat = m / (1.0 - ADAM_B1 ** ADAM_STEP)
    v_hat = v / (1.0 - ADAM_B2 ** ADAM_STEP)
    delta = -ADAM_LR * (m_hat / (jnp.sqrt(v_hat) + ADAM_EPS) + ADAM_WD * w)
    return delta, m, v


def _adamw(w, m, v, parts, prev, layer, me_idx, *, name, own=None):
    L, r, cc = w.shape
    P = parts.shape[0]
    tr = 512 if r % 512 == 0 else r
    if prev is None:
        prev = tuple(lax.empty(w.shape, F32) for _ in range(4))

    def body(me_ref, w_ref, m_ref, v_ref, p_ref, *rest):
        g_ref, d_ref, nm_ref, nv_ref = rest[-4:]
        if own is None:
            g = p_ref[0].astype(F32)
            for q in range(1, P):
                g = g + p_ref[q].astype(F32)
        else:
            me = me_ref[0]
            g = rest[0][...].astype(F32)
            for q in range(P):
                g = g + jnp.where(q == me, 0.0, p_ref[q].astype(F32))
        d, nm, nv = _adamw_math(w_ref[...], g, m_ref[...], v_ref[...])
        g_ref[...] = g
        d_ref[...] = d
        nm_ref[...] = nm
        nv_ref[...] = nv

    blk = pl.BlockSpec((None, tr, cc), lambda i, me: (layer, i, 0))
    in_specs = [blk, blk, blk, pl.BlockSpec((P, tr, cc), lambda i, me: (0, i, 0))]
    args = [me_idx, w, m, v, parts]
    if own is not None:
        in_specs.append(pl.BlockSpec((None, tr, cc), lambda i, me: (me[0], i, 0)))
        args.append(own)
    first_prev = len(args)
    in_specs += [_ANY] * 4
    args += list(prev)
    grid_spec = pltpu.PrefetchScalarGridSpec(num_scalar_prefetch=1, grid=(r // tr,), in_specs=in_specs,
                                             out_specs=(blk, blk, blk, blk))
    o = jax.ShapeDtypeStruct(w.shape, F32)
    return pl.pallas_call(body, grid_spec=grid_spec, out_shape=(o, o, o, o),
                          input_output_aliases={first_prev + i: i for i in range(4)},
                          compiler_params=_cp("parallel"), name=name)(*args)


_SHARDED = ("w_in", "conv_a_w", "w_a_out", "conv_b_w", "w_b_out", "w_o", "w_1", "w_2")
_COL_SHARDED = ("w_in", "conv_a_w", "conv_b_w", "w_1")
_REPLICATED = ("g_mix", "b_in", "conv_a_b", "ln_g", "ln_b", "conv_b_b", "w_rg_a", "b_rg_a", "w_rg_x", "b_rg_x", "lam",
               "g_mlp")
_WEIGHTS = ("g_mix", "w_in", "b_in", "conv_a_w", "conv_a_b", "ln_g", "ln_b", "w_a_out", "conv_b_w", "conv_b_b", "w_rg_a",
            "b_rg_a", "w_rg_x", "b_rg_x", "lam", "w_b_out", "w_o", "g_mlp", "w_1", "w_2", "g_final")
_LANES = 128


def _cols_from_blocks(b):
    nb, K, n = b.shape
    return b.transpose(1, 0, 2).reshape(K, nb * n)


def _blocks_from_cols(w, K):
    n = w.shape[1] // N_DEV
    return w[:K].reshape(K, N_DEV, n).transpose(1, 0, 2)


def kernel(x, g_mix, w_in, b_in, conv_a_w, conv_a_b, ln_g, ln_b, w_a_out, conv_b_w, conv_b_b, w_rg_a, b_rg_a, w_rg_x, b_rg_x, lam, w_b_out, w_o, g_mlp, w_1, w_2, g_final, loss_target, m_g_mix, m_w_in, m_b_in, m_conv_a_w, m_conv_a_b, m_ln_g, m_ln_b, m_w_a_out, m_conv_b_w, m_conv_b_b, m_w_rg_a, m_b_rg_a, m_w_rg_x, m_b_rg_x, m_lam, m_w_b_out, m_w_o, m_g_mlp, m_w_1, m_w_2, m_g_final, v_g_mix, v_w_in, v_b_in, v_conv_a_w, v_conv_a_b, v_ln_g, v_ln_b, v_w_a_out, v_conv_b_w, v_conv_b_b, v_w_rg_a, v_b_rg_a, v_w_rg_x, v_b_rg_x, v_lam, v_w_b_out, v_w_o, v_g_mlp, v_w_1, v_w_2, v_g_final):
    W = dict(g_mix=g_mix, w_in=w_in, b_in=b_in, conv_a_w=conv_a_w, conv_a_b=conv_a_b, ln_g=ln_g, ln_b=ln_b,
             w_a_out=w_a_out, conv_b_w=conv_b_w, conv_b_b=conv_b_b, w_rg_a=w_rg_a, b_rg_a=b_rg_a, w_rg_x=w_rg_x,
             b_rg_x=b_rg_x, lam=lam, w_b_out=w_b_out, w_o=w_o, g_mlp=g_mlp, w_1=w_1, w_2=w_2, g_final=g_final)
    M = dict(g_mix=m_g_mix, w_in=m_w_in, b_in=m_b_in, conv_a_w=m_conv_a_w, conv_a_b=m_conv_a_b, ln_g=m_ln_g, ln_b=m_ln_b,
             w_a_out=m_w_a_out, conv_b_w=m_conv_b_w, conv_b_b=m_conv_b_b, w_rg_a=m_w_rg_a, b_rg_a=m_b_rg_a,
             w_rg_x=m_w_rg_x, b_rg_x=m_b_rg_x, lam=m_lam, w_b_out=m_w_b_out, w_o=m_w_o, g_mlp=m_g_mlp, w_1=m_w_1,
             w_2=m_w_2, g_final=m_g_final)
    V = dict(g_mix=v_g_mix, w_in=v_w_in, b_in=v_b_in, conv_a_w=v_conv_a_w, conv_a_b=v_conv_a_b, ln_g=v_ln_g, ln_b=v_ln_b,
             w_a_out=v_w_a_out, conv_b_w=v_conv_b_w, conv_b_b=v_conv_b_b, w_rg_a=v_w_rg_a, b_rg_a=v_b_rg_a,
             w_rg_x=v_w_rg_x, b_rg_x=v_b_rg_x, lam=v_lam, w_b_out=v_w_b_out, w_o=v_w_o, g_mlp=v_g_mlp, w_1=v_w_1,
             w_2=v_w_2, g_final=v_g_final)
    NB, S, D = x.shape
    L = g_mix.shape[0]
    hd = w_rg_a.shape[-1]
    me_idx = (4 * lax.axis_index("x") + 2 * lax.axis_index("y") + lax.axis_index("c")).astype(jnp.int32).reshape(1)

    stages = (("in", ("w_in",)), ("mix", ("conv_a_w", "w_a_out", "conv_b_w", "w_b_out", "w_o")), ("mlp", ("w_1", "w_2")))
    gathers = {}
    started = jnp.zeros((), F32)
    for l in range(L):
        for stage, names in stages:
            lands = [_place(W[k][l], me_idx, F32 if k.startswith("conv") else BF16, name="place_" + k) for k in names]
            send_sems, recv_sems, bufs, token = _exchange_start(None, lands, name=f"weights_start_{stage}_{l}")
            gathers[l, stage] = (names, send_sems, recv_sems, bufs)
            started = started + token[0, 0]

    xt = x.reshape(NB * S, D)
    layers, saved = [], []
    for l in range(L):
        p = {k: W[k][l][None] for k in ("g_mix", "b_in", "conv_a_b", "ln_g", "ln_b", "conv_b_b", "b_rg_a", "b_rg_x",
                                         "lam", "g_mlp")}
        if l == 0:
            p["g_mix"] = p["g_mix"] + started
        p["wg_a"] = _group_weights(w_rg_a[l])
        p["wg_x"] = _group_weights(w_rg_x[l])

        def fetch(stage, after, l=l, p=p):
            names, send_sems, recv_sems, bufs = gathers[l, stage]
            bufs = _exchange_wait(send_sems, recv_sems, bufs, scatter=False, after=after,
                                  name=f"weights_wait_{stage}_{l}")
            for k, full in zip(names, bufs):
                if k in ("w_in", "w_1"):
                    p[k] = full
                elif k in _COL_SHARDED:
                    p[k] = _cols_from_blocks(full)
                else:
                    p[k] = full.reshape(-1, full.shape[-1])

        layers.append(p)
        xt, sv = _layer_fwd(xt, p, S=S, fetch=fetch)
        saved.append(sv)
    loss, dx, dg_final = _final_loss(xt, g_final[None], loss_target.reshape(NB * S, D), name="final_loss")

    results = {k: None for k in _SHARDED}

    def scatter_start(l, names, g, tag):
        srcs = []
        for k in names:
            shard_shape = W[k].shape[1:]
            if k in ("w_in", "w_1"):
                srcs.append(g[k])
            elif k in _COL_SHARDED:
                srcs.append(_blocks_from_cols(g[k], shard_shape[0]).astype(BF16))
            else:
                srcs.append(g[k].reshape((N_DEV,) + shard_shape))
        lands = [lax.empty(s.shape, BF16) for s in srcs]
        send_sems, recv_sems, bufs, token = _exchange_start(srcs, lands, name=f"grads_start_{tag}_{l}")
        return (names, send_sems, recv_sems, bufs, tag), token

    def scatter_finish(l, flight, after):
        names, send_sems, recv_sems, bufs, tag = flight
        bufs = _exchange_wait(send_sems, recv_sems, bufs, scatter=True, after=after, name=f"grads_wait_{tag}_{l}")
        n = len(names)
        for k, own, land in zip(names, bufs[:n], bufs[n:]):
            results[k] = _adamw(W[k], M[k], V[k], land, results[k], l, me_idx, own=own, name="adamw_" + k)

    gate_names = ("w_rg_a", "w_rg_x")
    gate_results = {k: None for k in gate_names}

    def small_start(l, g):
        lands = [_place(g[k].reshape(-1, hd), me_idx, F32, name="place_gate_grad") for k in gate_names]
        send_sems, recv_sems, bufs, token = _exchange_start(None, lands, name=f"small_start_{l}")
        return (send_sems, recv_sems, bufs), token

    def small_finish(l, flight, after):
        send_sems, recv_sems, bufs = flight
        bufs = _exchange_wait(send_sems, recv_sems, bufs, scatter=False, after=after, name=f"small_wait_{l}")
        for k, g_all in zip(gate_names, bufs):
            gate_results[k] = _adamw(W[k].reshape(L, -1, hd), M[k].reshape(L, -1, hd), V[k].reshape(L, -1, hd), g_all,
                                     gate_results[k], l, me_idx, name="adamw_gate")

    grads = [None] * L
    in_flight = []
    dep = None
    for l in reversed(range(L)):
        dx_mid, g = _layer_bwd_mlp(dx, layers[l], saved[l], wdt=BF16, dep=dep)
        f_mlp, dep = scatter_start(l, ("w_2", "w_1"), g, "mlp")
        flights = [f_mlp]

        small = []

        def on_gate_grads(g_part, l=l, small=small):
            f_small, token = small_start(l, g_part)
            small.append(f_small)
            return token

        def on_weight_grads(g_part, l=l, flights=flights):
            f_mix, token = scatter_start(l, ("w_o", "w_b_out", "w_a_out", "conv_a_w", "conv_b_w", "w_in"), g_part, "mix")
            flights.append(f_mix)
            return token

        dx, g_mix_part = _layer_bwd_mix(dx_mid, layers[l], saved[l], S=S, wdt=BF16, dep=dep,
                                        on_gate_grads=on_gate_grads, on_weight_grads=on_weight_grads)
        dep = None
        g.update(g_mix_part)
        grads[l] = g
        for l_prev, fs, f_sm in in_flight:
            for flight in fs:
                scatter_finish(l_prev, flight, dx)
            small_finish(l_prev, f_sm, dx)
        in_flight = [(l, flights, small[0])]

    vec_names = tuple(k for k in _REPLICATED if k not in gate_names)
    n_vec = sum(W[k].shape[1] for k in vec_names)

    def vec_pack(rows, final, last):
        tail = jnp.concatenate([final.reshape(1, -1), jnp.broadcast_to(last.reshape(1, 1), (1, _LANES))], axis=1)
        tail = jnp.pad(tail, ((0, SUBLANES - L - 1), (0, n_vec - tail.shape[1])))
        return jnp.concatenate([rows, tail], axis=0)

    g_rows = jnp.concatenate([jnp.concatenate([grads[l][k] for k in vec_names], axis=1) for l in range(L)], axis=0)
    land = _place(vec_pack(g_rows, dg_final, loss[0, :1]), me_idx, F32, name="place_vectors")
    vec_send_sems, vec_recv_sems, vec_bufs, _ = _exchange_start(None, [land], name="vectors_start")
    for l_prev, fs, f_sm in in_flight:
        after = dx if L == 1 else results["w_in"][0]
        for flight in fs:
            scatter_finish(l_prev, flight, after)
            after = results[flight[0][-1]][0]
        small_finish(l_prev, f_sm, after)
    out_g, out_d, out_m, out_v = {}, {}, {}, {}
    for k in _SHARDED:
        out_g[k], out_d[k], out_m[k], out_v[k] = results[k]
    for k in gate_names:
        out_g[k], out_d[k], out_m[k], out_v[k] = (a.reshape(W[k].shape) for a in gate_results[k])

    zero = jnp.zeros((1,), F32)
    (g_all,) = _exchange_wait(vec_send_sems, vec_recv_sems, vec_bufs, scatter=False, after=results["w_in"][0],
                              name="vectors_wait")
    vec_out = _adamw(vec_pack(jnp.concatenate([W[k] for k in vec_names], axis=1), g_final, zero)[None],
                     vec_pack(jnp.concatenate([M[k] for k in vec_names], axis=1), m_g_final, zero)[None],
                     vec_pack(jnp.concatenate([V[k] for k in vec_names], axis=1), v_g_final, zero)[None],
                     g_all, None, 0, me_idx, name="adamw_vectors")
    vec_out = [a[0] for a in vec_out]
    for res, arr in zip((out_g, out_d, out_m, out_v), vec_out):
        off = 0
        for k in vec_names:
            res[k] = arr[:L, off:off + W[k].shape[1]]
            off += W[k].shape[1]
        res["g_final"] = arr[L, :g_final.size]
    loss_out = vec_out[0][L, g_final.size]

    return (loss_out, dx.reshape(NB, S, D), *[out_g[k] for k in _WEIGHTS], *[out_d[k] for k in _WEIGHTS],
            *[out_m[k] for k in _WEIGHTS], *[out_v[k] for k in _WEIGHTS])
```

```python
import functools

import jax
import jax.numpy as jnp
from jax import lax
from jax.experimental import pallas as pl
from jax.experimental.pallas import tpu as pltpu

F32 = jnp.float32
BF16 = jnp.bfloat16

EPS = 1e-6
LRU_C = 8.0
N_RNN_HEADS = 16
HEADS_PER_GROUP = 4
N_DEV = 8
ADAM_LR, ADAM_B1, ADAM_B2, ADAM_EPS, ADAM_WD, ADAM_STEP = 0.001, 0.9, 0.999, 1e-08, 0.01, 10

VMEM_LIMIT_BYTES = 48 * 1024 * 1024
CONV_PAD = 32
CONV_CHUNK = 128
SUBLANES = 8


def _cp(*sem):
    return pltpu.CompilerParams(dimension_semantics=sem, vmem_limit_bytes=VMEM_LIMIT_BYTES)


def _sig(x):
    return 1.0 / (1.0 + jnp.exp(-x))


def _gelu(x):
    c = 0.7978845608028654
    return 0.5 * x * (1.0 + jnp.tanh(c * (x + 0.044715 * x * x * x)))


def _gelu_grad(x):
    c = 0.7978845608028654
    th = jnp.tanh(c * (x + 0.044715 * x * x * x))
    return 0.5 * (1.0 + th) + 0.5 * x * (1.0 - th * th) * c * (1.0 + 3.0 * 0.044715 * x * x)


def _mm_nn(a, b, *, tm, tn, name, bias=None, resid=None, relu2=False, out_dtype=F32, a_resident=False):
    M, K = a.shape
    blocked = b.ndim == 3
    N = b.shape[0] * b.shape[2] if blocked else b.shape[1]
    tm = min(tm, M)
    tn = min(tn, N)
    if blocked:
        assert tn == b.shape[2]
    n_extra = (bias is not None) + (resid is not None)

    def body(*refs):
        acc = jnp.dot(refs[0][...].astype(BF16), refs[1][...].astype(BF16), preferred_element_type=F32)
        k = 2
        if bias is not None:
            acc = acc + refs[k][...]
            k += 1
        if resid is not None:
            acc = acc + refs[k][...]
            k += 1
        if relu2:
            p = jnp.maximum(acc, 0.0)
            acc = p * p
        refs[k][...] = acc.astype(out_dtype)

    def spec(shape, index):
        return pl.BlockSpec(shape, (lambda i, j: index(j, i)) if a_resident else index)

    in_specs = [spec((tm, K), lambda j, i: (i, 0))]
    if blocked:
        in_specs.append(spec((None, K, tn), lambda j, i: (j, 0, 0)))
    else:
        in_specs.append(spec((K, tn), lambda j, i: (0, j)))
    args = [a, b]
    if bias is not None:
        in_specs.append(spec((1, tn), lambda j, i: (0, j)))
        args.append(bias)
    if resid is not None:
        in_specs.append(spec((tm, tn), lambda j, i: (i, j)))
        args.append(resid)
    out_specs = spec((tm, tn), lambda j, i: (i, j))
    out_shape = jax.ShapeDtypeStruct((M, N), out_dtype)
    del n_extra
    grid = (M // tm, N // tn) if a_resident else (N // tn, M // tm)
    return pl.pallas_call(body, grid=grid, in_specs=in_specs, out_specs=out_specs,
                          out_shape=out_shape, compiler_params=_cp("parallel", "parallel"), name=name)(*args)


def _mm_nt(a, b, *, tm, tn, tk, name, mul_sqrt=None, resid=None, out_dtype=F32, dep=None):
    M, N = a.shape
    blocked = b.ndim == 3
    Kout = b.shape[1] if blocked else b.shape[0]
    tm = min(tm, M)
    tn = min(tn, Kout)
    tk = b.shape[2] if blocked else min(tk, N)
    nk = N // tk

    def body(*refs):
        acc_ref = refs[-1]
        kk = pl.program_id(2)
        part = lax.dot_general(refs[0][...].astype(BF16), refs[1][...].astype(BF16),
                               (((1,), (1,)), ((), ())), preferred_element_type=F32)

        @pl.when(kk == 0)
        def _():
            acc_ref[...] = part

        @pl.when(kk > 0)
        def _():
            acc_ref[...] += part

        @pl.when(kk == nk - 1)
        def _():
            acc = acc_ref[...]
            k = 2
            if mul_sqrt is not None:
                acc = acc * (2.0 * jnp.sqrt(refs[k][...].astype(F32)))
                k += 1
            if resid is not None:
                acc = acc + refs[k][...]
                k += 1
            if dep is not None:
                k += 1
            refs[k][...] = acc.astype(out_dtype)

    in_specs = [pl.BlockSpec((tm, tk), lambda i, j, k: (i, k))]
    if blocked:
        in_specs.append(pl.BlockSpec((None, tn, tk), lambda i, j, k: (k, j, 0)))
    else:
        in_specs.append(pl.BlockSpec((tn, tk), lambda i, j, k: (j, k)))
    args = [a, b]
    for extra in (mul_sqrt, resid):
        if extra is not None:
            in_specs.append(pl.BlockSpec((tm, tn), lambda i, j, k: (i, j)))
            args.append(extra)
    if dep is not None:
        in_specs.append(pl.BlockSpec(dep.shape, lambda i, j, k: (0, 0)))
        args.append(dep)
    return pl.pallas_call(body, grid=(M // tm, Kout // tn, nk), in_specs=in_specs,
                          out_specs=pl.BlockSpec((tm, tn), lambda i, j, k: (i, j)),
                          out_shape=jax.ShapeDtypeStruct((M, Kout), out_dtype),
                          scratch_shapes=[pltpu.VMEM((tm, tn), F32)],
                          compiler_params=_cp("parallel", "parallel", "arbitrary"), name=name)(*args)


def _mm_tn(a, b, *, tm, tn, tk, name, out_blocks=None, colsum=False, out_dtype=F32):
    T, M = a.shape
    N = b.shape[1]
    tm = min(tm, M)
    tk = min(tk, T)
    if out_blocks is not None:
        tn = N // out_blocks
        tm = M
    tn = min(tn, N)
    nk = T // tk
    if colsum:
        assert tm == M

    def body(*refs):
        a_ref, b_ref, o_ref, acc_ref = refs[0], refs[1], refs[2], refs[-1]
        kk = pl.program_id(2)
        bv = b_ref[...]
        part = lax.dot_general(a_ref[...].astype(BF16), bv.astype(BF16),
                               (((0,), (0,)), ((), ())), preferred_element_type=F32)

        if colsum:
            csum = jnp.broadcast_to(jnp.sum(bv.astype(F32), axis=0, keepdims=True), (SUBLANES, tn))

        @pl.when(kk == 0)
        def _():
            acc_ref[...] = part
            if colsum:
                refs[3][...] = csum

        @pl.when(kk > 0)
        def _():
            acc_ref[...] += part
            if colsum:
                refs[3][...] += csum

        @pl.when(kk == nk - 1)
        def _():
            o_ref[...] = acc_ref[...].astype(out_dtype)

    in_specs = [pl.BlockSpec((tk, tm), lambda i, j, k: (k, i)), pl.BlockSpec((tk, tn), lambda i, j, k: (k, j))]
    if out_blocks is not None:
        o_shape = jax.ShapeDtypeStruct((out_blocks, M, tn), out_dtype)
        o_spec = pl.BlockSpec((None, M, tn), lambda i, j, k: (j, 0, 0))
    else:
        o_shape = jax.ShapeDtypeStruct((M, N), out_dtype)
        o_spec = pl.BlockSpec((tm, tn), lambda i, j, k: (i, j))
    if colsum:
        out_shape = (o_shape, jax.ShapeDtypeStruct((SUBLANES, N), F32))
        out_specs = (o_spec, pl.BlockSpec((SUBLANES, tn), lambda i, j, k: (0, j)))
    else:
        out_shape, out_specs = o_shape, o_spec
    return pl.pallas_call(body, grid=(M // tm, N // tn, nk), in_specs=in_specs, out_specs=out_specs,
                          out_shape=out_shape, scratch_shapes=[pltpu.VMEM((tm, tn), F32)],
                          compiler_params=_cp("parallel", "parallel", "arbitrary"), name=name)(a, b)


def _rms_fwd(x, g, *, name, tr=512):
    T, D = x.shape
    tr = min(tr, T)

    def body(x_ref, g_ref, h_ref):
        xv = x_ref[...]
        r = lax.rsqrt(jnp.mean(xv * xv, axis=-1, keepdims=True) + EPS)
        h_ref[...] = (xv * r * g_ref[...]).astype(BF16)

    return pl.pallas_call(body, grid=(T // tr,),
                          in_specs=[pl.BlockSpec((tr, D), lambda i: (i, 0)), pl.BlockSpec((1, D), lambda i: (0, 0))],
                          out_specs=pl.BlockSpec((tr, D), lambda i: (i, 0)),
                          out_shape=jax.ShapeDtypeStruct((T, D), BF16), compiler_params=_cp("parallel"), name=name)(x, g)


def _rms_bwd(x, g, dh, dres, *, name, tr=512):
    T, D = x.shape
    tr = min(tr, T)

    def body(x_ref, g_ref, dh_ref, dres_ref, dx_ref, dg_ref):
        xv = x_ref[...]
        r = lax.rsqrt(jnp.mean(xv * xv, axis=-1, keepdims=True) + EPS)
        n = xv * r
        dh = dh_ref[...]
        dn = dh * g_ref[...]
        dx_ref[...] = dres_ref[...] + r * (dn - n * jnp.mean(dn * n, axis=-1, keepdims=True))
        part = jnp.sum(dh * n, axis=0, keepdims=True)

        @pl.when(pl.program_id(0) == 0)
        def _():
            dg_ref[...] = part

        @pl.when(pl.program_id(0) > 0)
        def _():
            dg_ref[...] += part

    row = pl.BlockSpec((tr, D), lambda i: (i, 0))
    vec = pl.BlockSpec((1, D), lambda i: (0, 0))
    return pl.pallas_call(body, grid=(T // tr,), in_specs=[row, vec, row, row], out_specs=(row, vec),
                          out_shape=(jax.ShapeDtypeStruct((T, D), F32), jax.ShapeDtypeStruct((1, D), F32)),
                          compiler_params=_cp("arbitrary"), name=name)(x, g, dh, dres)


def _final_loss(x, g, tgt, *, name, tr=512):
    T, D = x.shape
    tr = min(tr, T)

    def body(x_ref, g_ref, t_ref, loss_ref, dx_ref, dg_ref):
        xv = x_ref[...]
        gv = g_ref[...]
        r = lax.rsqrt(jnp.mean(xv * xv, axis=-1, keepdims=True) + EPS)
        n = xv * r
        e = n * gv - t_ref[...]
        lpart = 0.5 * jnp.sum(jnp.mean(e * e, axis=-1, keepdims=True), axis=0, keepdims=True)
        dy = e * (1.0 / D)
        dn = dy * gv
        dx_ref[...] = r * (dn - n * jnp.mean(dn * n, axis=-1, keepdims=True))
        gpart = jnp.sum(dy * n, axis=0, keepdims=True)

        @pl.when(pl.program_id(0) == 0)
        def _():
            dg_ref[...] = gpart
            loss_ref[...] = jnp.broadcast_to(lpart, (1, 128))

        @pl.when(pl.program_id(0) > 0)
        def _():
            dg_ref[...] += gpart
            loss_ref[...] += jnp.broadcast_to(lpart, (1, 128))

    row = pl.BlockSpec((tr, D), lambda i: (i, 0))
    vec = pl.BlockSpec((1, D), lambda i: (0, 0))
    return pl.pallas_call(body, grid=(T // tr,), in_specs=[row, vec, row],
                          out_specs=(pl.BlockSpec((1, 128), lambda i: (0, 0)), row, vec),
                          out_shape=(jax.ShapeDtypeStruct((1, 128), F32), jax.ShapeDtypeStruct((T, D), F32),
                                     jax.ShapeDtypeStruct((1, D), F32)),
                          compiler_params=_cp("arbitrary"), name=name)(x, g, tgt)


def _ln_silu_fwd(u, g, b, *, name, tr=512):
    T, C = u.shape
    tr = min(tr, T)

    def body(u_ref, g_ref, b_ref, o_ref):
        uv = u_ref[...]
        mu = jnp.mean(uv, axis=-1, keepdims=True)
        xc = uv - mu
        r = lax.rsqrt(jnp.mean(xc * xc, axis=-1, keepdims=True) + EPS)
        y = xc * r * g_ref[...] + b_ref[...]
        o_ref[...] = (y * _sig(y)).astype(BF16)

    row = pl.BlockSpec((tr, C), lambda i: (i, 0))
    vec = pl.BlockSpec((1, C), lambda i: (0, 0))
    return pl.pallas_call(body, grid=(T // tr,), in_specs=[row, vec, vec], out_specs=row,
                          out_shape=jax.ShapeDtypeStruct((T, C), BF16), compiler_params=_cp("parallel"),
                          name=name)(u, g, b)


def _ln_silu_bwd(u, g, b, do, *, name, tr=512):
    T, C = u.shape
    tr = min(tr, T)

    def body(u_ref, g_ref, b_ref, do_ref, du_ref, dg_ref, db_ref):
        uv = u_ref[...]
        gv = g_ref[...]
        mu = jnp.mean(uv, axis=-1, keepdims=True)
        xc = uv - mu
        r = lax.rsqrt(jnp.mean(xc * xc, axis=-1, keepdims=True) + EPS)
        n = xc * r
        y = n * gv + b_ref[...]
        s = _sig(y)
        dy = do_ref[...] * (s * (1.0 + y * (1.0 - s)))
        dn = dy * gv
        du_ref[...] = r * (dn - jnp.mean(dn, axis=-1, keepdims=True) - n * jnp.mean(dn * n, axis=-1, keepdims=True))
        gpart = jnp.sum(dy * n, axis=0, keepdims=True)
        bpart = jnp.sum(dy, axis=0, keepdims=True)

        @pl.when(pl.program_id(0) == 0)
        def _():
            dg_ref[...] = gpart
            db_ref[...] = bpart

        @pl.when(pl.program_id(0) > 0)
        def _():
            dg_ref[...] += gpart
            db_ref[...] += bpart

    row = pl.BlockSpec((tr, C), lambda i: (i, 0))
    vec = pl.BlockSpec((1, C), lambda i: (0, 0))
    return pl.pallas_call(body, grid=(T // tr,), in_specs=[row, vec, vec, row], out_specs=(row, vec, vec),
                          out_shape=(jax.ShapeDtypeStruct((T, C), F32), jax.ShapeDtypeStruct((1, C), F32),
                                     jax.ShapeDtypeStruct((1, C), F32)),
                          compiler_params=_cp("arbitrary"), name=name)(u, g, b, do)


def _merge_fwd(z, ya, yb, *, off_sa, off_sb, name, tr=512):
    T, D = ya.shape
    tr = min(tr, T)
    assert off_sa % D == 0 and off_sb % D == 0

    def body(sa_ref, sb_ref, ya_ref, yb_ref, m_ref):
        m_ref[...] = (_sig(sa_ref[...]) * ya_ref[...] + _sig(sb_ref[...]) * yb_ref[...]).astype(BF16)

    row = pl.BlockSpec((tr, D), lambda i: (i, 0))
    return pl.pallas_call(body, grid=(T // tr,),
                          in_specs=[pl.BlockSpec((tr, D), lambda i: (i, off_sa // D)),
                                    pl.BlockSpec((tr, D), lambda i: (i, off_sb // D)), row, row],
                          out_specs=row, out_shape=jax.ShapeDtypeStruct((T, D), BF16),
                          compiler_params=_cp("parallel"), name=name)(z, z, ya, yb)


def _columns_copy(stage_ref, dz_ref, row0, rows, col0, sem):
    dst = dz_ref.at[pl.ds(pl.multiple_of(row0, SUBLANES), rows),
                    pl.ds(pl.multiple_of(col0, 128), stage_ref.shape[1])]
    return pltpu.make_async_copy(stage_ref, dst, sem)


def _put_columns(stage_ref, dz_ref, row0, rows, col0, sem):
    cp = _columns_copy(stage_ref, dz_ref, row0, rows, col0, sem)
    cp.start()
    cp.wait()


def _merge_bwd(z, ya, yb, dm, dz, *, off_sa, off_sb, name, tr=512):
    T, D = ya.shape
    tr = min(tr, T)
    assert off_sb == off_sa + D

    def body(sa_ref, sb_ref, ya_ref, yb_ref, dm_ref, dz_in, dya_ref, dyb_ref, dz_ref, stage, sem):
        del dz_in
        dm = dm_ref[...]
        ga = _sig(sa_ref[...])
        gb = _sig(sb_ref[...])
        dya_ref[...] = (dm * ga).astype(BF16)
        dyb_ref[...] = (dm * gb).astype(BF16)
        stage[:, 0:D] = (dm * ya_ref[...] * ga * (1.0 - ga)).astype(BF16)
        stage[:, D:2 * D] = (dm * yb_ref[...] * gb * (1.0 - gb)).astype(BF16)
        _put_columns(stage, dz_ref, pl.program_id(0) * tr, tr, off_sa, sem)

    row = pl.BlockSpec((tr, D), lambda i: (i, 0))
    o = jax.ShapeDtypeStruct((T, D), BF16)
    return pl.pallas_call(body, grid=(T // tr,),
                          in_specs=[pl.BlockSpec((tr, D), lambda i: (i, off_sa // D)),
                                    pl.BlockSpec((tr, D), lambda i: (i, off_sb // D)), row, row, row, _ANY],
                          out_specs=(row, row, _ANY), out_shape=(o, o, jax.ShapeDtypeStruct(dz.shape, dz.dtype)),
                          scratch_shapes=[pltpu.VMEM((tr, 2 * D), BF16), pltpu.SemaphoreType.DMA],
                          input_output_aliases={5: 2},
                          compiler_params=_cp("parallel"), name=name)(z, z, ya, yb, dm, dz)


def _shift_rows(dst_ref, src_ref, r, total, back):
    for c0 in range(0, total - SUBLANES, CONV_CHUNK):
        n = min(CONV_CHUNK, total - SUBLANES - c0)
        if back:
            dst_ref[SUBLANES + c0:SUBLANES + c0 + n, :] = src_ref[SUBLANES + c0 - r:SUBLANES + c0 - r + n, :]
        else:
            dst_ref[c0:c0 + n, :] = src_ref[c0 + r:c0 + r + n, :]


def _tap_plan(K):
    if K <= SUBLANES:
        return [(0, [(s, K - 1 - s) for s in range(K)])]
    return [(r, [(SUBLANES * q, K - 1 - (SUBLANES * q + r)) for q in range(-(-K // SUBLANES)) if SUBLANES * q + r < K])
            for r in range(SUBLANES)]


def _conv_fwd(z, w, b, *, S, off_v, off_g, name, ct=256):
    T = z.shape[0]
    K, C = w.shape
    ct = min(ct, C)
    ch = min(CONV_CHUNK, S)
    glu = off_g is not None
    assert off_v % ct == 0 and (not glu or off_g % ct == 0)
    assert SUBLANES * ((K - 1) // SUBLANES) <= CONV_PAD - SUBLANES

    def body(*refs):
        if glu:
            v_ref, g_ref, w_ref, b_ref, o_ref, pad_ref, sh_ref = refs
        else:
            v_ref, w_ref, b_ref, o_ref, pad_ref, sh_ref = refs
        pad_ref[0:CONV_PAD, :] = jnp.zeros((CONV_PAD, ct), F32)
        if glu:
            pad_ref[CONV_PAD:CONV_PAD + S, :] = v_ref[...] * _sig(g_ref[...])
        else:
            pad_ref[CONV_PAD:CONV_PAD + S, :] = v_ref[...]
        for r, taps in _tap_plan(K):
            src = pad_ref
            if r > 0:
                _shift_rows(sh_ref, pad_ref, r, CONV_PAD + S, True)
                src = sh_ref
            for l0 in range(0, ct, 128):
                lanes = slice(l0, l0 + 128)
                for c in range(S // ch):
                    acc = None
                    for off, wrow in taps:
                        st = CONV_PAD + c * ch - off
                        term = w_ref[wrow:wrow + 1, lanes] * src[st:st + ch, lanes]
                        acc = term if acc is None else acc + term
                    rows = slice(c * ch, (c + 1) * ch)
                    if r == 0:
                        o_ref[rows, lanes] = acc + b_ref[:, lanes]
                    else:
                        o_ref[rows, lanes] += acc

    in_specs = [pl.BlockSpec((S, ct), lambda j, bb: (bb, off_v // ct + j))]
    args = [z]
    if glu:
        in_specs.append(pl.BlockSpec((S, ct), lambda j, bb: (bb, off_g // ct + j)))
        args.append(z)
    in_specs += [pl.BlockSpec((K, ct), lambda j, bb: (0, j)), pl.BlockSpec((1, ct), lambda j, bb: (0, j))]
    args += [w, b]
    return pl.pallas_call(body, grid=(C // ct, T // S), in_specs=in_specs,
                          out_specs=pl.BlockSpec((S, ct), lambda j, bb: (bb, j)),
                          out_shape=jax.ShapeDtypeStruct((T, C), F32),
                          scratch_shapes=[pltpu.VMEM((CONV_PAD + S, ct), F32), pltpu.VMEM((CONV_PAD + S, ct), F32)],
                          compiler_params=_cp("parallel", "parallel"), name=name)(*args)


def _conv_bwd(z, w, dy, dz, *, S, off_v, off_g, name, ct=256):
    T = z.shape[0]
    K, C = w.shape
    KP = -(-K // SUBLANES) * SUBLANES
    ct = min(ct, C)
    ch = min(CONV_CHUNK, S)
    glu = off_g is not None
    total = S + CONV_PAD

    def body(*refs):
        if glu:
            (v_ref, g_ref, w_ref, dy_ref, dz_in, dz_ref, dw_ref, db_ref,
             pad_ref, sh_ref, padb_ref, shb_ref, du_ref, stage_v, stage_g, sem) = refs
        else:
            (v_ref, w_ref, dy_ref, dz_in, dz_ref, dw_ref, db_ref,
             pad_ref, sh_ref, padb_ref, shb_ref, du_ref, stage_v, sem) = refs
        del dz_in
        j, bb = pl.program_id(0), pl.program_id(1)
        pad_ref[0:CONV_PAD, :] = jnp.zeros((CONV_PAD, ct), F32)
        if glu:
            pad_ref[CONV_PAD:total, :] = v_ref[...] * _sig(g_ref[...])
        else:
            pad_ref[CONV_PAD:total, :] = v_ref[...]
        padb_ref[0:S, :] = dy_ref[...]
        padb_ref[S:total, :] = jnp.zeros((CONV_PAD, ct), F32)

        @pl.when(bb == 0)
        def _():
            dw_ref[...] = jnp.zeros((KP, ct), F32)
            db_ref[...] = jnp.zeros((1, ct), F32)

        for r, taps in _tap_plan(K):
            u_src, d_src = pad_ref, padb_ref
            if r > 0:
                _shift_rows(sh_ref, pad_ref, r, total, True)
                _shift_rows(shb_ref, padb_ref, r, total, False)
                u_src, d_src = sh_ref, shb_ref
            for l0 in range(0, ct, 128):
                lanes = slice(l0, l0 + 128)
                for c in range(S // ch):
                    acc = None
                    for off, wrow in taps:
                        st = c * ch + off
                        term = w_ref[wrow:wrow + 1, lanes] * d_src[st:st + ch, lanes]
                        acc = term if acc is None else acc + term
                    rows = slice(c * ch, (c + 1) * ch)
                    if r == 0:
                        du_ref[rows, lanes] = acc
                    else:
                        du_ref[rows, lanes] += acc
                for off, wrow in taps:
                    acc = None
                    for c in range(S // ch):
                        st = CONV_PAD + c * ch - off
                        prod = padb_ref[c * ch:(c + 1) * ch, lanes] * u_src[st:st + ch, lanes]
                        acc = prod if acc is None else acc + prod
                    dw_ref[wrow:wrow + 1, lanes] += jnp.sum(acc, axis=0, keepdims=True)
        db_ref[...] += jnp.sum(dy_ref[...], axis=0, keepdims=True)
        for l0 in range(0, ct, 128):
            lanes = slice(l0, l0 + 128)
            for c in range(S // ch):
                rows = slice(c * ch, (c + 1) * ch)
                du = du_ref[rows, lanes]
                if glu:
                    sg = _sig(g_ref[rows, lanes])
                    stage_v[rows, lanes] = (du * sg).astype(BF16)
                    stage_g[rows, lanes] = (du * v_ref[rows, lanes] * sg * (1.0 - sg)).astype(BF16)
                else:
                    stage_v[rows, lanes] = du.astype(BF16)
        _put_columns(stage_v, dz_ref, bb * S, S, off_v + j * ct, sem)
        if glu:
            _put_columns(stage_g, dz_ref, bb * S, S, off_g + j * ct, sem)

    blk = lambda off: pl.BlockSpec((S, ct), lambda j, bb: (bb, off // ct + j))
    in_specs = [blk(off_v)]
    args = [z]
    if glu:
        in_specs.append(blk(off_g))
        args.append(z)
    in_specs += [pl.BlockSpec((K, ct), lambda j, bb: (0, j)), blk(0), _ANY]
    args += [w, dy, dz]
    out_shape = (jax.ShapeDtypeStruct(dz.shape, dz.dtype), jax.ShapeDtypeStruct((KP, C), F32),
                 jax.ShapeDtypeStruct((1, C), F32))
    out_specs = (_ANY, pl.BlockSpec((KP, ct), lambda j, bb: (0, j)), pl.BlockSpec((1, ct), lambda j, bb: (0, j)))
    padded = pltpu.VMEM((total, ct), F32)
    stage = pltpu.VMEM((S, ct), BF16)
    return pl.pallas_call(body, grid=(C // ct, T // S), in_specs=in_specs, out_specs=out_specs, out_shape=out_shape,
                          scratch_shapes=[padded, padded, padded, padded, pltpu.VMEM((S, ct), F32), stage]
                          + ([stage] if glu else []) + [pltpu.SemaphoreType.DMA],
                          input_output_aliases={len(args) - 1: 0},
                          compiler_params=_cp("parallel", "arbitrary"), name=name)(*args)


def _softplus_neg(lam):
    return jnp.maximum(-lam, 0.0) + jnp.log1p(jnp.exp(-jnp.abs(lam)))


def _neg_expm1(x):
    u = jnp.exp(x)
    um1 = u - 1.0
    lg = jnp.log(u)
    safe = jnp.where(lg == 0.0, 1.0, lg)
    em1 = jnp.where(um1 == 0.0, x, jnp.where(um1 == -1.0, -1.0, um1 * x / safe))
    return -em1


def _gates_fwd(v, wa, wx, ba, bx, lam, *, S, name, tm=512):
    T, C = v.shape
    G, gw, _ = wa.shape
    tm = min(tm, T)

    def body(v_ref, wa_ref, wx_ref, ba_ref, bx_ref, lam_ref, r_ref, i_ref, a_ref, bt_ref):
        vv = v_ref[...]
        vb = vv.astype(BF16)
        r = _sig(jnp.dot(vb, wa_ref[...], preferred_element_type=F32) + ba_ref[...])
        ig = _sig(jnp.dot(vb, wx_ref[...], preferred_element_type=F32) + bx_ref[...])
        log_a = -LRU_C * r * _softplus_neg(lam_ref[...])
        a = jnp.exp(log_a)
        mult = jnp.sqrt(_neg_expm1(2.0 * log_a))
        row = pl.program_id(0) * tm + lax.broadcasted_iota(jnp.int32, (tm, gw), 0)
        mult = jnp.where(row % S == 0, 1.0, mult)
        r_ref[...] = r
        i_ref[...] = ig
        a_ref[...] = a
        bt_ref[...] = mult * ig * vv

    blk = pl.BlockSpec((tm, gw), lambda i, g: (i, g))
    wsp = pl.BlockSpec((None, gw, gw), lambda i, g: (g, 0, 0))
    vec = pl.BlockSpec((1, gw), lambda i, g: (0, g))
    o = jax.ShapeDtypeStruct((T, C), F32)
    return pl.pallas_call(body, grid=(T // tm, G), in_specs=[blk, wsp, wsp, vec, vec, vec],
                          out_specs=(blk, blk, blk, blk), out_shape=(o, o, o, o),
                          compiler_params=_cp("parallel", "parallel"), name=name)(v, wa, wx, ba, bx, lam)


def _scan_fwd(a, bt, z, *, S, off_gate, name, ct=512, tt=256):
    T, C = a.shape
    ct = min(ct, C)
    tt = min(tt, S)
    nt = S // tt
    assert off_gate % ct == 0

    def body(a_ref, b_ref, g_ref, h_ref, hb_ref, carry_ref):
        row = lax.broadcasted_iota(jnp.int32, (SUBLANES, ct), 0)

        @pl.when(pl.program_id(2) == 0)
        def _():
            carry_ref[...] = jnp.zeros((SUBLANES, ct), F32)

        def step(i, carry):
            st = pl.multiple_of(i * SUBLANES, SUBLANES)
            A = a_ref[pl.ds(st, SUBLANES), :]
            B = b_ref[pl.ds(st, SUBLANES), :]
            for d in (1, 2, 4):
                m = row >= d
                Bn = jnp.where(m, A * pltpu.roll(B, d, 0) + B, B)
                A = jnp.where(m, A * pltpu.roll(A, d, 0), A)
                B = Bn
            h = B + A * carry
            h_ref[pl.ds(st, SUBLANES), :] = h
            return jnp.broadcast_to(h[SUBLANES - 1:SUBLANES, :], (SUBLANES, ct))

        carry_ref[...] = lax.fori_loop(0, tt // SUBLANES, step, carry_ref[...], unroll=2)
        hb_ref[...] = (h_ref[...] * _gelu(g_ref[...])).astype(BF16)

    blk = pl.BlockSpec((tt, ct), lambda j, bb, t: (bb * nt + t, j))
    return pl.pallas_call(body, grid=(C // ct, T // S, nt),
                          in_specs=[blk, blk, pl.BlockSpec((tt, ct), lambda j, bb, t: (bb * nt + t, off_gate // ct + j))],
                          out_specs=(blk, blk),
                          out_shape=(jax.ShapeDtypeStruct((T, C), F32), jax.ShapeDtypeStruct((T, C), BF16)),
                          scratch_shapes=[pltpu.VMEM((SUBLANES, ct), F32)],
                          compiler_params=_cp("parallel", "parallel", "arbitrary"), name=name)(a, bt, z)


def _scan_bwd(a, h, z, dhb, dz, *, S, off_gate, name, ct=512, tt=256):
    T, C = a.shape
    ct = min(ct, C)
    tt = min(tt, S)
    nt = S // tt
    n_tiles = tt // SUBLANES

    def body(a_ref, h_ref, hp_ref, g_ref, dhb_ref, dz_in, G_ref, da_ref, dz_ref, dh_ref, carry_ref, stage, sem):
        del dz_in
        row = lax.broadcasted_iota(jnp.int32, (SUBLANES, ct), 0)
        tb = nt - 1 - pl.program_id(2)

        @pl.when(pl.program_id(2) == 0)
        def _():
            carry_ref[...] = jnp.zeros((SUBLANES, ct), F32)

        gate = g_ref[...]
        dhb = dhb_ref[...]
        dh_ref[...] = dhb * _gelu(gate)
        stage[...] = (dhb * h_ref[...] * _gelu_grad(gate)).astype(BF16)
        put = _columns_copy(stage, dz_ref, (pl.program_id(1) * nt + tb) * tt, tt, off_gate + pl.program_id(0) * ct, sem)
        put.start()
        h_before = jnp.where(tb > 0, jnp.broadcast_to(hp_ref[SUBLANES - 1:SUBLANES, :], (SUBLANES, ct)), 0.0)

        def step(k, qcarry):
            i = n_tiles - 1 - k
            st = pl.multiple_of(i * SUBLANES, SUBLANES)
            stp = pl.multiple_of(jnp.maximum(i - 1, 0) * SUBLANES, SUBLANES)
            A = a_ref[pl.ds(st, SUBLANES), :]
            hv = h_ref[pl.ds(st, SUBLANES), :]
            hprev_tile = h_ref[pl.ds(stp, SUBLANES), :]
            dh = dh_ref[pl.ds(st, SUBLANES), :]
            Aq = A
            Bq = A * dh
            for d in (1, 2, 4):
                m = row < SUBLANES - d
                Bn = jnp.where(m, Aq * pltpu.roll(Bq, SUBLANES - d, 0) + Bq, Bq)
                Aq = jnp.where(m, Aq * pltpu.roll(Aq, SUBLANES - d, 0), Aq)
                Bq = Bn
            q = Bq + Aq * qcarry
            qnext = jnp.where(row == SUBLANES - 1, qcarry, pltpu.roll(q, SUBLANES - 1, 0))
            g = dh + qnext
            hlast = jnp.where(i > 0, jnp.broadcast_to(hprev_tile[SUBLANES - 1:SUBLANES, :], (SUBLANES, ct)), h_before)
            hprev = jnp.where(row == 0, hlast, pltpu.roll(hv, 1, 0))
            G_ref[pl.ds(st, SUBLANES), :] = g
            da_ref[pl.ds(st, SUBLANES), :] = g * hprev
            return jnp.broadcast_to(q[0:1, :], (SUBLANES, ct))

        carry_ref[...] = lax.fori_loop(0, n_tiles, step, carry_ref[...], unroll=2)
        put.wait()

    tiles_per_block = tt // SUBLANES
    blk = pl.BlockSpec((tt, ct), lambda j, bb, t: (bb * nt + nt - 1 - t, j))
    before = pl.BlockSpec((SUBLANES, ct),
                          lambda j, bb, t: (jnp.maximum((bb * nt + nt - 1 - t) * tiles_per_block - 1, 0), j))
    o = jax.ShapeDtypeStruct((T, C), F32)
    return pl.pallas_call(body, grid=(C // ct, T // S, nt),
                          in_specs=[blk, blk, before,
                                    pl.BlockSpec((tt, ct), lambda j, bb, t: (bb * nt + nt - 1 - t, off_gate // ct + j)), blk,
                                    _ANY],
                          out_specs=(blk, blk, _ANY), out_shape=(o, o, jax.ShapeDtypeStruct(dz.shape, dz.dtype)),
                          scratch_shapes=[pltpu.VMEM((tt, ct), F32), pltpu.VMEM((SUBLANES, ct), F32),
                                          pltpu.VMEM((tt, ct), BF16), pltpu.SemaphoreType.DMA],
                          input_output_aliases={5: 2},
                          compiler_params=_cp("parallel", "parallel", "arbitrary"), name=name)(a, h, h, z, dhb, dz)


def _gates_bwd(v, r, ig, Gb, da, lam, *, S, name, tr=256):
    T, C = v.shape
    tr = min(tr, T)

    def body(v_ref, r_ref, i_ref, G_ref, da_ref, lam_ref, dpr_ref, dpi_ref, dv_ref, dsp_ref, dba_ref, dbx_ref):
        vv, r, ig, Gv = v_ref[...], r_ref[...], i_ref[...], G_ref[...]
        sp = _softplus_neg(lam_ref[...])
        log_a = -LRU_C * r * sp
        a = jnp.exp(log_a)
        mult_raw = jnp.sqrt(_neg_expm1(2.0 * log_a))
        row = pl.program_id(0) * tr + lax.broadcasted_iota(jnp.int32, (tr, C), 0)
        start = row % S == 0
        mult = jnp.where(start, 1.0, mult_raw)
        dmult = jnp.where(start, 0.0, Gv * ig * vv)
        di = Gv * mult * vv
        dv_ref[...] = Gv * mult * ig
        dla = da_ref[...] * a - dmult * (a * a) / jnp.where(start, 1.0, mult_raw)
        dr = dla * (-LRU_C) * sp
        dpr = dr * r * (1.0 - r)
        dpi = di * ig * (1.0 - ig)
        dpr_ref[...] = dpr.astype(BF16)
        dpi_ref[...] = dpi.astype(BF16)
        p_sp = jnp.sum(dla * (-LRU_C) * r, axis=0, keepdims=True) * (-_sig(-lam_ref[...]))
        p_a = jnp.sum(dpr, axis=0, keepdims=True)
        p_x = jnp.sum(dpi, axis=0, keepdims=True)

        @pl.when(pl.program_id(0) == 0)
        def _():
            dsp_ref[...] = p_sp
            dba_ref[...] = p_a
            dbx_ref[...] = p_x

        @pl.when(pl.program_id(0) > 0)
        def _():
            dsp_ref[...] += p_sp
            dba_ref[...] += p_a
            dbx_ref[...] += p_x

    rowb = pl.BlockSpec((tr, C), lambda i: (i, 0))
    vec = pl.BlockSpec((1, C), lambda i: (0, 0))
    ob = jax.ShapeDtypeStruct((T, C), BF16)
    ov = jax.ShapeDtypeStruct((1, C), F32)
    return pl.pallas_call(body, grid=(T // tr,), in_specs=[rowb, rowb, rowb, rowb, rowb, vec],
                          out_specs=(rowb, rowb, rowb, vec, vec, vec),
                          out_shape=(ob, ob, jax.ShapeDtypeStruct((T, C), F32), ov, ov, ov),
                          compiler_params=_cp("arbitrary"), name=name)(v, r, ig, Gb, da, lam)


def _gates_dgrad(dpr, dpi, wa, wx, dv_direct, *, name, tm=512, dep=None):
    T, C = dpr.shape
    G, gw, _ = wa.shape
    tm = min(tm, T)

    def body(dpr_ref, dpi_ref, wa_ref, wx_ref, dvd_ref, *rest):
        dv_ref = rest[-1]
        dn = (((1,), (1,)), ((), ()))
        dv_ref[...] = (dvd_ref[...]
                       + lax.dot_general(dpr_ref[...], wa_ref[...], dn, preferred_element_type=F32)
                       + lax.dot_general(dpi_ref[...], wx_ref[...], dn, preferred_element_type=F32))

    blk = pl.BlockSpec((tm, gw), lambda i, g: (i, g))
    wsp = pl.BlockSpec((None, gw, gw), lambda i, g: (g, 0, 0))
    in_specs, args = [blk, blk, wsp, wsp, blk], [dpr, dpi, wa, wx, dv_direct]
    if dep is not None:
        in_specs.append(pl.BlockSpec(dep.shape, lambda i, g: (0, 0)))
        args.append(dep)
    return pl.pallas_call(body, grid=(T // tm, G), in_specs=in_specs, out_specs=blk,
                          out_shape=jax.ShapeDtypeStruct((T, C), F32),
                          compiler_params=_cp("parallel", "parallel"), name=name)(*args)


def _gates_wgrad(v, dpr, dpi, *, G, name, tk=512):
    T, C = v.shape
    gw = C // G
    hd = gw // HEADS_PER_GROUP
    tk = min(tk, T)
    nk = T // tk

    def body(v_ref, dpr_ref, dpi_ref, dwa_ref, dwx_ref, acc_a, acc_x):
        dn = (((0,), (0,)), ((), ()))
        vb = v_ref[...].astype(BF16)
        pa = lax.dot_general(vb, dpr_ref[...], dn, preferred_element_type=F32)
        px = lax.dot_general(vb, dpi_ref[...], dn, preferred_element_type=F32)

        @pl.when(pl.program_id(1) == 0)
        def _():
            acc_a[...] = pa
            acc_x[...] = px

        @pl.when(pl.program_id(1) > 0)
        def _():
            acc_a[...] += pa
            acc_x[...] += px

        @pl.when(pl.program_id(1) == nk - 1)
        def _():
            for h in range(HEADS_PER_GROUP):
                dwa_ref[h] = acc_a[h * hd:(h + 1) * hd, h * hd:(h + 1) * hd]
                dwx_ref[h] = acc_x[h * hd:(h + 1) * hd, h * hd:(h + 1) * hd]

    blk = pl.BlockSpec((tk, gw), lambda g, k: (k, g))
    wsp = pl.BlockSpec((HEADS_PER_GROUP, hd, hd), lambda g, k: (g, 0, 0))
    o = jax.ShapeDtypeStruct((G * HEADS_PER_GROUP, hd, hd), F32)
    return pl.pallas_call(body, grid=(G, nk), in_specs=[blk, blk, blk], out_specs=(wsp, wsp), out_shape=(o, o),
                          scratch_shapes=[pltpu.VMEM((gw, gw), F32), pltpu.VMEM((gw, gw), F32)],
                          compiler_params=_cp("parallel", "arbitrary"), name=name)(v, dpr, dpi)


def _rglru_fwd(v, z, wa, wx, ba, bx, lam, *, S, off_gate, name, tt=512):
    T, C = v.shape
    G, gw, _ = wa.shape
    tt = min(tt, S)
    nt = S // tt
    nlb = gw // 128
    assert gw % 128 == 0 and off_gate % 128 == 0

    def body(*refs):
        v_ref = refs[0]
        gate_refs = refs[1:1 + nlb]
        wa_ref, wx_ref, ba_ref, bx_ref, lam_ref, r_ref, i_ref, h_ref, hb_ref, a_s, b_s, carry_ref = refs[1 + nlb:]
        t = pl.program_id(2)

        @pl.when(t == 0)
        def _():
            carry_ref[...] = jnp.zeros((SUBLANES, gw), F32)

        vv = v_ref[...]
        vb = vv.astype(BF16)
        r = _sig(jnp.dot(vb, wa_ref[...], preferred_element_type=F32) + ba_ref[...])
        ig = _sig(jnp.dot(vb, wx_ref[...], preferred_element_type=F32) + bx_ref[...])
        log_a = -LRU_C * r * _softplus_neg(lam_ref[...])
        mult = jnp.sqrt(_neg_expm1(2.0 * log_a))
        start = jnp.logical_and(t == 0, lax.broadcasted_iota(jnp.int32, (tt, gw), 0) == 0)
        mult = jnp.where(start, 1.0, mult)
        r_ref[...] = r
        i_ref[...] = ig
        a_s[...] = jnp.exp(log_a)
        b_s[...] = mult * ig * vv
        row = lax.broadcasted_iota(jnp.int32, (SUBLANES, gw), 0)

        def step(i, carry):
            st = pl.multiple_of(i * SUBLANES, SUBLANES)
            A = a_s[pl.ds(st, SUBLANES), :]
            B = b_s[pl.ds(st, SUBLANES), :]
            for d in (1, 2, 4):
                m = row >= d
                Bn = jnp.where(m, A * pltpu.roll(B, d, 0) + B, B)
                A = jnp.where(m, A * pltpu.roll(A, d, 0), A)
                B = Bn
            h = B + A * carry
            h_ref[pl.ds(st, SUBLANES), :] = h
            return jnp.broadcast_to(h[SUBLANES - 1:SUBLANES, :], (SUBLANES, gw))

        carry_ref[...] = lax.fori_loop(0, tt // SUBLANES, step, carry_ref[...], unroll=2)
        for k in range(nlb):
            lanes = slice(k * 128, (k + 1) * 128)
            hb_ref[:, lanes] = (h_ref[:, lanes] * _gelu(gate_refs[k][...])).astype(BF16)

    blk = pl.BlockSpec((tt, gw), lambda g, bb, t: (bb * nt + t, g))
    gates = [pl.BlockSpec((tt, 128), lambda g, bb, t, k=k: (bb * nt + t, off_gate // 128 + g * nlb + k)) for k in range(nlb)]
    wsp = pl.BlockSpec((None, gw, gw), lambda g, bb, t: (g, 0, 0))
    vec = pl.BlockSpec((1, gw), lambda g, bb, t: (0, g))
    o = jax.ShapeDtypeStruct((T, C), F32)
    return pl.pallas_call(body, grid=(G, T // S, nt), in_specs=[blk] + gates + [wsp, wsp, vec, vec, vec],
                          out_specs=(blk, blk, blk, blk), out_shape=(o, o, o, jax.ShapeDtypeStruct((T, C), BF16)),
                          scratch_shapes=[pltpu.VMEM((tt, gw), F32), pltpu.VMEM((tt, gw), F32),
                                          pltpu.VMEM((SUBLANES, gw), F32)],
                          compiler_params=_cp("parallel", "parallel", "arbitrary"),
                          name=name)(v, *([z] * nlb), wa, wx, ba, bx, lam)


def _rglru_bwd(dhb, h, z, r, ig, v, wa, wx, lam, dz, *, S, off_gate, name, tt=512):
    T, C = v.shape
    G, gw, _ = wa.shape
    hd = gw // HEADS_PER_GROUP
    tt = min(tt, S)
    nt = S // tt
    n_seq = T // S
    n_tiles = tt // SUBLANES
    nlb = gw // 128

    def body(*refs):
        dhb_ref, h_ref, hp_ref = refs[0:3]
        gate_refs = refs[3:3 + nlb]
        (r_ref, i_ref, v_ref, wa_ref, wx_ref, lam_ref, dz_in, dv_ref, dz_ref, dlam_ref, dba_ref, dbx_ref, dwa_ref, dwx_ref,
         dh_s, a_s, G_s, da_s, carry_ref, stage, acc_a, acc_x, sem) = refs[3 + nlb:]
        del dz_in
        g, bb, t = pl.program_id(0), pl.program_id(1), pl.program_id(2)
        tb = nt - 1 - t
        first = jnp.logical_and(bb == 0, t == 0)

        @pl.when(t == 0)
        def _():
            carry_ref[...] = jnp.zeros((SUBLANES, gw), F32)

        @pl.when(first)
        def _():
            dlam_ref[...] = jnp.zeros((SUBLANES, gw), F32)
            dba_ref[...] = jnp.zeros((SUBLANES, gw), F32)
            dbx_ref[...] = jnp.zeros((SUBLANES, gw), F32)
            acc_a[...] = jnp.zeros((gw, gw), F32)
            acc_x[...] = jnp.zeros((gw, gw), F32)

        rr, ig, vv = r_ref[...], i_ref[...], v_ref[...]
        lam_v = lam_ref[...]
        sp = _softplus_neg(lam_v)
        log_a = -LRU_C * rr * sp
        a = jnp.exp(log_a)
        a_s[...] = a
        for k in range(nlb):
            lanes = slice(k * 128, (k + 1) * 128)
            gate = gate_refs[k][...]
            dhb = dhb_ref[:, lanes]
            dh_s[:, lanes] = dhb * _gelu(gate)
            stage[:, lanes] = (dhb * h_ref[:, lanes] * _gelu_grad(gate)).astype(BF16)
        put = _columns_copy(stage, dz_ref, (bb * nt + tb) * tt, tt, off_gate + g * gw, sem)
        put.start()
        h_before = jnp.where(tb > 0, jnp.broadcast_to(hp_ref[SUBLANES - 1:SUBLANES, :], (SUBLANES, gw)), 0.0)
        row = lax.broadcasted_iota(jnp.int32, (SUBLANES, gw), 0)

        def step(k, qcarry):
            i = n_tiles - 1 - k
            st = pl.multiple_of(i * SUBLANES, SUBLANES)
            stp = pl.multiple_of(jnp.maximum(i - 1, 0) * SUBLANES, SUBLANES)
            A = a_s[pl.ds(st, SUBLANES), :]
            hv = h_ref[pl.ds(st, SUBLANES), :]
            hprev_tile = h_ref[pl.ds(stp, SUBLANES), :]
            dh = dh_s[pl.ds(st, SUBLANES), :]
            Aq = A
            Bq = A * dh
            for d in (1, 2, 4):
                m = row < SUBLANES - d
                Bn = jnp.where(m, Aq * pltpu.roll(Bq, SUBLANES - d, 0) + Bq, Bq)
                Aq = jnp.where(m, Aq * pltpu.roll(Aq, SUBLANES - d, 0), Aq)
                Bq = Bn
            q = Bq + Aq * qcarry
            qnext = jnp.where(row == SUBLANES - 1, qcarry, pltpu.roll(q, SUBLANES - 1, 0))
            gq = dh + qnext
            hlast = jnp.where(i > 0, jnp.broadcast_to(hprev_tile[SUBLANES - 1:SUBLANES, :], (SUBLANES, gw)), h_before)
            hprev = jnp.where(row == 0, hlast, pltpu.roll(hv, 1, 0))
            G_s[pl.ds(st, SUBLANES), :] = gq
            da_s[pl.ds(st, SUBLANES), :] = gq * hprev
            return jnp.broadcast_to(q[0:1, :], (SUBLANES, gw))

        carry_ref[...] = lax.fori_loop(0, n_tiles, step, carry_ref[...], unroll=2)

        Gv = G_s[...]
        mult_raw = jnp.sqrt(_neg_expm1(2.0 * log_a))
        start = jnp.logical_and(tb == 0, lax.broadcasted_iota(jnp.int32, (tt, gw), 0) == 0)
        mult = jnp.where(start, 1.0, mult_raw)
        dmult = jnp.where(start, 0.0, Gv * ig * vv)
        di = Gv * mult * vv
        dla = da_s[...] * a - dmult * (a * a) / jnp.where(start, 1.0, mult_raw)
        dpr = dla * (-LRU_C) * sp * rr * (1.0 - rr)
        dpi = di * ig * (1.0 - ig)
        dlam_ref[...] += jnp.broadcast_to(jnp.sum(dla * (-LRU_C) * rr, axis=0, keepdims=True) * (-_sig(-lam_v)), (SUBLANES, gw))
        dba_ref[...] += jnp.broadcast_to(jnp.sum(dpr, axis=0, keepdims=True), (SUBLANES, gw))
        dbx_ref[...] += jnp.broadcast_to(jnp.sum(dpi, axis=0, keepdims=True), (SUBLANES, gw))
        dprb, dpib = dpr.astype(BF16), dpi.astype(BF16)
        nt_dims = (((1,), (1,)), ((), ()))
        dv_ref[...] = (Gv * mult * ig
                       + lax.dot_general(dprb, wa_ref[...], nt_dims, preferred_element_type=F32)
                       + lax.dot_general(dpib, wx_ref[...], nt_dims, preferred_element_type=F32))
        tn_dims = (((0,), (0,)), ((), ()))
        vb = vv.astype(BF16)
        acc_a[...] += lax.dot_general(vb, dprb, tn_dims, preferred_element_type=F32)
        acc_x[...] += lax.dot_general(vb, dpib, tn_dims, preferred_element_type=F32)

        @pl.when(jnp.logical_and(bb == n_seq - 1, t == nt - 1))
        def _():
            for hh in range(HEADS_PER_GROUP):
                dwa_ref[hh] = acc_a[hh * hd:(hh + 1) * hd, hh * hd:(hh + 1) * hd]
                dwx_ref[hh] = acc_x[hh * hd:(hh + 1) * hd, hh * hd:(hh + 1) * hd]

        put.wait()

    rowblk = lambda g, bb, t: (bb * nt + nt - 1 - t, g)
    blk = pl.BlockSpec((tt, gw), rowblk)
    before = pl.BlockSpec((SUBLANES, gw), lambda g, bb, t: (jnp.maximum((bb * nt + nt - 1 - t) * n_tiles - 1, 0), g))
    gates = [pl.BlockSpec((tt, 128), lambda g, bb, t, k=k: (bb * nt + nt - 1 - t, off_gate // 128 + g * nlb + k))
             for k in range(nlb)]
    wsp = pl.BlockSpec((None, gw, gw), lambda g, bb, t: (g, 0, 0))
    vec = pl.BlockSpec((1, gw), lambda g, bb, t: (0, g))
    acc8 = pl.BlockSpec((SUBLANES, gw), lambda g, bb, t: (0, g))
    heads = pl.BlockSpec((HEADS_PER_GROUP, hd, hd), lambda g, bb, t: (g, 0, 0))
    o8 = jax.ShapeDtypeStruct((SUBLANES, C), F32)
    ow = jax.ShapeDtypeStruct((G * HEADS_PER_GROUP, hd, hd), F32)
    scr = pltpu.VMEM((tt, gw), F32)
    return pl.pallas_call(
        body, grid=(G, n_seq, nt),
        in_specs=[blk, blk, before] + gates + [blk, blk, blk, wsp, wsp, vec, _ANY],
        out_specs=(blk, _ANY, acc8, acc8, acc8, heads, heads),
        out_shape=(jax.ShapeDtypeStruct((T, C), F32), jax.ShapeDtypeStruct(dz.shape, dz.dtype), o8, o8, o8, ow, ow),
        scratch_shapes=[scr, scr, scr, scr, pltpu.VMEM((SUBLANES, gw), F32), pltpu.VMEM((tt, gw), BF16),
                        pltpu.VMEM((gw, gw), F32), pltpu.VMEM((gw, gw), F32), pltpu.SemaphoreType.DMA],
        input_output_aliases={9 + nlb: 1},
        compiler_params=_cp("parallel", "arbitrary", "arbitrary"),
        name=name)(dhb, h, h, *([z] * nlb), r, ig, v, wa, wx, lam, dz)


def _group_weights(w):
    H, hd, _ = w.shape
    G = H // HEADS_PER_GROUP
    eye = jnp.eye(HEADS_PER_GROUP, dtype=w.dtype)
    wg = jnp.einsum("ghij,hk->ghikj", w.reshape(G, HEADS_PER_GROUP, hd, hd), eye)
    return wg.reshape(G, HEADS_PER_GROUP * hd, HEADS_PER_GROUP * hd).astype(BF16)


def _layer_fwd(x, p, *, S, fetch=None):
    D = x.shape[1]
    Dc = p["conv_a_b"].shape[1]
    Dr = p["conv_b_b"].shape[1]
    offs = dict(va=0, ga=Dc, xb=2 * Dc, gb=2 * Dc + Dr, sa=2 * Dc + 2 * Dr, sb=2 * Dc + 2 * Dr + D)
    h = _rms_fwd(x, p["g_mix"], name="rms_mix_fwd")
    if fetch is not None:
        fetch("in", h)
    z = _mm_nn(h, p["w_in"], tm=1024, tn=p["w_in"].shape[2], bias=p["b_in"], a_resident=True, name="mm_in_fwd")
    if fetch is not None:
        fetch("mix", z)
    u1 = _conv_fwd(z, p["conv_a_w"], p["conv_a_b"], S=S, off_v=offs["va"], off_g=offs["ga"], name="conv_a_fwd")
    u2 = _ln_silu_fwd(u1, p["ln_g"], p["ln_b"], name="ln_silu_fwd")
    ya = _mm_nn(u2, p["w_a_out"], tm=1024, tn=1024, name="mm_a_out_fwd")
    v0 = _conv_fwd(z, p["conv_b_w"], p["conv_b_b"], S=S, off_v=offs["xb"], off_g=None, name="conv_b_fwd")
    r, ig, hs, hb = _rglru_fwd(v0, z, p["wg_a"], p["wg_x"], p["b_rg_a"], p["b_rg_x"], p["lam"], S=S, off_gate=offs["gb"],
                               name="rglru_fwd")
    yb = _mm_nn(hb, p["w_b_out"], tm=1024, tn=1024, name="mm_b_out_fwd")
    m = _merge_fwd(z, ya, yb, off_sa=offs["sa"], off_sb=offs["sb"], name="merge_fwd")
    x_mid = _mm_nn(m, p["w_o"], tm=1024, tn=1024, resid=x, name="mm_o_fwd")
    h2 = _rms_fwd(x_mid, p["g_mlp"], name="rms_mlp_fwd")
    if fetch is not None:
        fetch("mlp", h2)
    f = _mm_nn(h2, p["w_1"], tm=1024, tn=p["w_1"].shape[2], relu2=True, out_dtype=BF16, a_resident=True,
               name="mm_1_fwd")
    x_next = _mm_nn(f, p["w_2"], tm=512, tn=1024, resid=x_mid, name="mm_2_fwd")
    saved = dict(x=x, h=h, z=z, u1=u1, u2=u2, ya=ya, v0=v0, r=r, ig=ig, hs=hs, hb=hb, yb=yb, m=m,
                 x_mid=x_mid, h2=h2, f=f, offs=offs)
    return x_next, saved


def _layer_bwd_mlp(dx, p, sv, *, wdt, dep=None):
    g = {}
    g["w_2"] = _mm_tn(sv["f"], dx, tm=2048, tn=1024, tk=1024, out_dtype=wdt, name="mm_2_wgrad")
    dfp = _mm_nt(dx, p["w_2"], tm=1024, tn=1024, tk=1024, mul_sqrt=sv["f"], out_dtype=BF16, dep=dep,
                 name="mm_2_dgrad")
    g["w_1"] = _mm_tn(sv["h2"], dfp, tm=1024, tn=512, tk=1024, out_blocks=p["w_1"].shape[0], out_dtype=wdt,
                      name="mm_1_wgrad")
    dh2 = _mm_nt(dfp, p["w_1"], tm=1024, tn=1024, tk=512, name="mm_1_dgrad")
    dx_mid, g["g_mlp"] = _rms_bwd(sv["x_mid"], p["g_mlp"], dh2, dx, name="rms_mlp_bwd")
    return dx_mid, g


def _layer_bwd_mix(dx_mid, p, sv, *, S, wdt, dep=None, on_gate_grads=None, on_weight_grads=None):
    offs = sv["offs"]
    g = {}
    g["w_o"] = _mm_tn(sv["m"], dx_mid, tm=1024, tn=1024, tk=1024, out_dtype=wdt, name="mm_o_wgrad")
    dm = _mm_nt(dx_mid, p["w_o"], tm=1024, tn=1024, tk=1024, dep=dep, name="mm_o_dgrad")
    dz = lax.empty(sv["z"].shape, BF16)
    dya, dyb, dz = _merge_bwd(sv["z"], sv["ya"], sv["yb"], dm, dz, off_sa=offs["sa"], off_sb=offs["sb"], name="merge_bwd")
    g["w_b_out"] = _mm_tn(sv["hb"], dyb, tm=1536, tn=1024, tk=1024, out_dtype=wdt, name="mm_b_out_wgrad")
    dhb = _mm_nt(dyb, p["w_b_out"], tm=1024, tn=1536, tk=1024, name="mm_b_out_dgrad")
    dv0, dz, dlam, dba, dbx, g["w_rg_a"], g["w_rg_x"] = _rglru_bwd(
        dhb, sv["hs"], sv["z"], sv["r"], sv["ig"], sv["v0"], p["wg_a"], p["wg_x"], p["lam"], dz, S=S, off_gate=offs["gb"],
        name="rglru_bwd")
    g["lam"], g["b_rg_a"], g["b_rg_x"] = dlam[:1], dba[:1], dbx[:1]
    dep_gates = on_gate_grads(g) if on_gate_grads is not None else None
    dz, g["conv_b_w"], g["conv_b_b"] = _conv_bwd(sv["z"], p["conv_b_w"], dv0, dz, S=S, off_v=offs["xb"], off_g=None,
                                                 name="conv_b_bwd")
    g["w_a_out"] = _mm_tn(sv["u2"], dya, tm=1024, tn=1024, tk=1024, out_dtype=wdt, name="mm_a_out_wgrad")
    du2 = _mm_nt(dya, p["w_a_out"], tm=1024, tn=1024, tk=1024, dep=dep_gates, name="mm_a_out_dgrad")
    du1, g["ln_g"], g["ln_b"] = _ln_silu_bwd(sv["u1"], p["ln_g"], p["ln_b"], du2, name="ln_silu_bwd")
    dz, g["conv_a_w"], g["conv_a_b"] = _conv_bwd(sv["z"], p["conv_a_w"], du1, dz, S=S, off_v=offs["va"],
                                                 off_g=offs["ga"], name="conv_a_bwd")
    g["w_in"], db_in = _mm_tn(sv["h"], dz, tm=1024, tn=512, tk=1024, out_blocks=p["w_in"].shape[0], colsum=True,
                              out_dtype=wdt, name="mm_in_wgrad")
    g["b_in"] = db_in[:1]
    dep_in = on_weight_grads(g) if on_weight_grads is not None else None
    dh = _mm_nt(dz, p["w_in"], tm=1024, tn=1024, tk=512, dep=dep_in, name="mm_in_dgrad")
    dx_in, g["g_mix"] = _rms_bwd(sv["x"], p["g_mix"], dh, dx_mid, name="rms_mix_bwd")
    return dx_in, g


def _layer_bwd(dx, p, sv, *, S, wdt=F32):
    dx_mid, g = _layer_bwd_mlp(dx, p, sv, wdt=wdt)
    dx_in, g2 = _layer_bwd_mix(dx_mid, p, sv, S=S, wdt=wdt)
    g.update(g2)
    return dx_in, g


def _local_step(x, tgt, layers, g_final, *, S, wdt=F32):
    saved = []
    for p in layers:
        x, sv = _layer_fwd(x, p, S=S)
        saved.append(sv)
    loss, dx, dg_final = _final_loss(x, g_final, tgt, name="final_loss")
    grads = [None] * len(layers)
    for l in reversed(range(len(layers))):
        dx, grads[l] = _layer_bwd(dx, layers[l], saved[l], S=S, wdt=wdt)
    return loss, dx, grads, dg_final


_HBM = pl.BlockSpec(memory_space=pltpu.HBM)
_MESH = pl.DeviceIdType.MESH


_SEM = pl.BlockSpec(memory_space=pltpu.SEMAPHORE)
_ANY = pl.BlockSpec(memory_space=pl.ANY)
_FLIPS = [(dx, dy, dc) for dx in (0, 1) for dy in (0, 1) for dc in (0, 1)][1:]


def _place(shard, me_idx, dtype, *, name):
    r, cc = shard.shape
    tr = 512 if r % 512 == 0 else r

    def body(me_ref, s_ref, o_ref):
        del me_ref
        o_ref[...] = s_ref[...].astype(dtype)

    grid_spec = pltpu.PrefetchScalarGridSpec(
        num_scalar_prefetch=1, grid=(r // tr,),
        in_specs=[pl.BlockSpec((tr, cc), lambda i, me: (i, 0))],
        out_specs=pl.BlockSpec((None, tr, cc), lambda i, me: (me[0], i, 0)))
    return pl.pallas_call(body, grid_spec=grid_spec, out_shape=jax.ShapeDtypeStruct((N_DEV, r, cc), dtype),
                          compiler_params=_cp("arbitrary"), name=name)(me_idx, shard)


def _exchange_copies(srcs, lands, send_sems, recv_sems):
    x, y, c = lax.axis_index("x"), lax.axis_index("y"), lax.axis_index("c")
    me = 4 * x + 2 * y + c
    pairs = []
    for k, (dx, dy, dc) in enumerate(_FLIPS):
        peer = (1 - x if dx else x, 1 - y if dy else y, 1 - c if dc else c)
        pidx = 4 * peer[0] + 2 * peer[1] + peer[2]
        for a, land in enumerate(lands):
            src = land.at[me] if srcs is None else srcs[a].at[pidx]
            sem = k * len(lands) + a
            out = pltpu.make_async_remote_copy(src_ref=src, dst_ref=land.at[me], send_sem=send_sems.at[sem],
                                               recv_sem=recv_sems.at[sem], device_id=peer, device_id_type=_MESH)
            arrival = pltpu.make_async_remote_copy(src_ref=src, dst_ref=land.at[pidx], send_sem=send_sems.at[sem],
                                                   recv_sem=recv_sems.at[sem], device_id=peer, device_id_type=_MESH)
            pairs.append((out, arrival))
    return pairs


def _exchange_start(srcs, lands, *, name):
    n = len(lands)
    bufs = list(lands) if srcs is None else list(srcs) + list(lands)
    nb = len(bufs)

    def body(*refs):
        ins = refs[:nb]
        send_sems, recv_sems = refs[nb], refs[nb + 1]
        token = refs[-1]
        for out, _ in _exchange_copies(None if srcs is None else ins[:n], ins[nb - n:], send_sems, recv_sems):
            out.start()
        token[...] = jnp.zeros_like(token)

    sems = pltpu.SemaphoreType.DMA((len(_FLIPS) * n,))
    res = pl.pallas_call(
        body, name=name, in_specs=[_HBM] * nb,
        out_shape=(sems, sems, *[pltpu.HBM(b.shape, b.dtype) for b in bufs], jax.ShapeDtypeStruct((SUBLANES, 128), F32)),
        out_specs=(_SEM, _SEM, *[_HBM] * nb, pl.BlockSpec(memory_space=pltpu.VMEM)),
        input_output_aliases={i: 2 + i for i in range(nb)},
        compiler_params=pltpu.CompilerParams(has_side_effects=pltpu.SideEffectType.DATAFLOW_SIDE_EFFECTING),
    )(*[pltpu.with_memory_space_constraint(b, pltpu.HBM) for b in bufs])
    return res[0], res[1], list(res[2:2 + nb]), res[-1]


def _exchange_wait(send_sems, recv_sems, bufs, *, scatter, after, name):
    nb = len(bufs)
    n = nb // 2 if scatter else nb

    def body(*refs):
        ins = refs[:nb]
        for out, arrival in _exchange_copies(ins[:n] if scatter else None, ins[nb - n:], refs[nb], refs[nb + 1]):
            out.wait_send()
            arrival.wait_recv()

    extra = [] if after is None else [after]
    res = pl.pallas_call(
        body, name=name, in_specs=[_HBM] * nb + [_SEM, _SEM] + [_ANY] * len(extra),
        out_shape=tuple(pltpu.HBM(b.shape, b.dtype) for b in bufs), out_specs=tuple([_HBM] * nb),
        input_output_aliases={i: i for i in range(nb)},
        compiler_params=pltpu.CompilerParams(has_side_effects=pltpu.SideEffectType.DATAFLOW_SIDE_EFFECTING),
    )(*bufs, send_sems, recv_sems, *extra)
    return list(res)


def _adamw_math(w, g, m, v):
    m = ADAM_B1 * m + (1.0 - ADAM_B1) * g
    v = ADAM_B2 * v + (1.0 - ADAM_B2) * (g * g)
    m_hat = m / (1.0 - ADAM_B1 ** ADAM_STEP)
    v_hat = v / (1.0 - ADAM_B2 ** ADAM_STEP)
    delta = -ADAM_LR * (m_hat / (jnp.sqrt(v_hat) + ADAM_EPS) + ADAM_WD * w)
    return delta, m, v


def _adamw(w, m, v, parts, prev, layer, me_idx, *, name, own=None):
    L, r, cc = w.shape
    P = parts.shape[0]
    tr = 512 if r % 512 == 0 else r
    if prev is None:
        prev = tuple(lax.empty(w.shape, F32) for _ in range(4))

    def body(me_ref, w_ref, m_ref, v_ref, p_ref, *rest):
        g_ref, d_ref, nm_ref, nv_ref = rest[-4:]
        if own is None:
            g = p_ref[0].astype(F32)
            for q in range(1, P):
                g = g + p_ref[q].astype(F32)
        else:
            me = me_ref[0]
            g = rest[0][...].astype(F32)
            for q in range(P):
                g = g + jnp.where(q == me, 0.0, p_ref[q].astype(F32))
        d, nm, nv = _adamw_math(w_ref[...], g, m_ref[...], v_ref[...])
        g_ref[...] = g
        d_ref[...] = d
        nm_ref[...] = nm
        nv_ref[...] = nv

    blk = pl.BlockSpec((None, tr, cc), lambda i, me: (layer, i, 0))
    in_specs = [blk, blk, blk, pl.BlockSpec((P, tr, cc), lambda i, me: (0, i, 0))]
    args = [me_idx, w, m, v, parts]
    if own is not None:
        in_specs.append(pl.BlockSpec((None, tr, cc), lambda i, me: (me[0], i, 0)))
        args.append(own)
    first_prev = len(args)
    in_specs += [_ANY] * 4
    args += list(prev)
    grid_spec = pltpu.PrefetchScalarGridSpec(num_scalar_prefetch=1, grid=(r // tr,), in_specs=in_specs,
                                             out_specs=(blk, blk, blk, blk))
    o = jax.ShapeDtypeStruct(w.shape, F32)
    return pl.pallas_call(body, grid_spec=grid_spec, out_shape=(o, o, o, o),
                          input_output_aliases={first_prev + i: i for i in range(4)},
                          compiler_params=_cp("parallel"), name=name)(*args)


_SHARDED = ("w_in", "conv_a_w", "w_a_out", "conv_b_w", "w_b_out", "w_o", "w_1", "w_2")
_COL_SHARDED = ("w_in", "conv_a_w", "conv_b_w", "w_1")
_REPLICATED = ("g_mix", "b_in", "conv_a_b", "ln_g", "ln_b", "conv_b_b", "w_rg_a", "b_rg_a", "w_rg_x", "b_rg_x", "lam",
               "g_mlp")
_WEIGHTS = ("g_mix", "w_in", "b_in", "conv_a_w", "conv_a_b", "ln_g", "ln_b", "w_a_out", "conv_b_w", "conv_b_b", "w_rg_a",
            "b_rg_a", "w_rg_x", "b_rg_x", "lam", "w_b_out", "w_o", "g_mlp", "w_1", "w_2", "g_final")
_LANES = 128


def _cols_from_blocks(b):
    nb, K, n = b.shape
    return b.transpose(1, 0, 2).reshape(K, nb * n)


def _blocks_from_cols(w, K):
    n = w.shape[1] // N_DEV
    return w[:K].reshape(K, N_DEV, n).transpose(1, 0, 2)


def kernel(x, g_mix, w_in, b_in, conv_a_w, conv_a_b, ln_g, ln_b, w_a_out, conv_b_w, conv_b_b, w_rg_a, b_rg_a, w_rg_x, b_rg_x, lam, w_b_out, w_o, g_mlp, w_1, w_2, g_final, loss_target, m_g_mix, m_w_in, m_b_in, m_conv_a_w, m_conv_a_b, m_ln_g, m_ln_b, m_w_a_out, m_conv_b_w, m_conv_b_b, m_w_rg_a, m_b_rg_a, m_w_rg_x, m_b_rg_x, m_lam, m_w_b_out, m_w_o, m_g_mlp, m_w_1, m_w_2, m_g_final, v_g_mix, v_w_in, v_b_in, v_conv_a_w, v_conv_a_b, v_ln_g, v_ln_b, v_w_a_out, v_conv_b_w, v_conv_b_b, v_w_rg_a, v_b_rg_a, v_w_rg_x, v_b_rg_x, v_lam, v_w_b_out, v_w_o, v_g_mlp, v_w_1, v_w_2, v_g_final):
    W = dict(g_mix=g_mix, w_in=w_in, b_in=b_in, conv_a_w=conv_a_w, conv_a_b=conv_a_b, ln_g=ln_g, ln_b=ln_b,
             w_a_out=w_a_out, conv_b_w=conv_b_w, conv_b_b=conv_b_b, w_rg_a=w_rg_a, b_rg_a=b_rg_a, w_rg_x=w_rg_x,
             b_rg_x=b_rg_x, lam=lam, w_b_out=w_b_out, w_o=w_o, g_mlp=g_mlp, w_1=w_1, w_2=w_2, g_final=g_final)
    M = dict(g_mix=m_g_mix, w_in=m_w_in, b_in=m_b_in, conv_a_w=m_conv_a_w, conv_a_b=m_conv_a_b, ln_g=m_ln_g, ln_b=m_ln_b,
             w_a_out=m_w_a_out, conv_b_w=m_conv_b_w, conv_b_b=m_conv_b_b, w_rg_a=m_w_rg_a, b_rg_a=m_b_rg_a,
             w_rg_x=m_w_rg_x, b_rg_x=m_b_rg_x, lam=m_lam, w_b_out=m_w_b_out, w_o=m_w_o, g_mlp=m_g_mlp, w_1=m_w_1,
             w_2=m_w_2, g_final=m_g_final)
    V = dict(g_mix=v_g_mix, w_in=v_w_in, b_in=v_b_in, conv_a_w=v_conv_a_w, conv_a_b=v_conv_a_b, ln_g=v_ln_g, ln_b=v_ln_b,
             w_a_out=v_w_a_out, conv_b_w=v_conv_b_w, conv_b_b=v_conv_b_b, w_rg_a=v_w_rg_a, b_rg_a=v_b_rg_a,
             w_rg_x=v_w_rg_x, b_rg_x=v_b_rg_x, lam=v_lam, w_b_out=v_w_b_out, w_o=v_w_o, g_mlp=v_g_mlp, w_1=v_w_1,
             w_2=v_w_2, g_final=v_g_final)
    NB, S, D = x.shape
    L = g_mix.shape[0]
    hd = w_rg_a.shape[-1]
    me_idx = (4 * lax.axis_index("x") + 2 * lax.axis_index("y") + lax.axis_index("c")).astype(jnp.int32).reshape(1)

    stages = (("in", ("w_in",)), ("mix", ("conv_a_w", "w_a_out", "conv_b_w", "w_b_out", "w_o")), ("mlp", ("w_1", "w_2")))
    gathers = {}
    started = jnp.zeros((), F32)
    for l in range(L):
        for stage, names in stages:
            lands = [_place(W[k][l], me_idx, F32 if k.startswith("conv") else BF16, name="place_" + k) for k in names]
            send_sems, recv_sems, bufs, token = _exchange_start(None, lands, name=f"weights_start_{stage}_{l}")
            gathers[l, stage] = (names, send_sems, recv_sems, bufs)
            started = started + token[0, 0]

    xt = x.reshape(NB * S, D)
    layers, saved = [], []
    for l in range(L):
        p = {k: W[k][l][None] for k in ("g_mix", "b_in", "conv_a_b", "ln_g", "ln_b", "conv_b_b", "b_rg_a", "b_rg_x",
                                         "lam", "g_mlp")}
        if l == 0:
            p["g_mix"] = p["g_mix"] + started
        p["wg_a"] = _group_weights(w_rg_a[l])
        p["wg_x"] = _group_weights(w_rg_x[l])

        def fetch(stage, after, l=l, p=p):
            names, send_sems, recv_sems, bufs = gathers[l, stage]
            bufs = _exchange_wait(send_sems, recv_sems, bufs, scatter=False, after=after,
                                  name=f"weights_wait_{stage}_{l}")
            for k, full in zip(names, bufs):
                if k in ("w_in", "w_1"):
                    p[k] = full
                elif k in _COL_SHARDED:
                    p[k] = _cols_from_blocks(full)
                else:
                    p[k] = full.reshape(-1, full.shape[-1])

        layers.append(p)
        xt, sv = _layer_fwd(xt, p, S=S, fetch=fetch)
        saved.append(sv)
    loss, dx, dg_final = _final_loss(xt, g_final[None], loss_target.reshape(NB * S, D), name="final_loss")

    results = {k: None for k in _SHARDED}

    def scatter_start(l, names, g, tag):
        srcs = []
        for k in names:
            shard_shape = W[k].shape[1:]
            if k in ("w_in", "w_1"):
                srcs.append(g[k])
            elif k in _COL_SHARDED:
                srcs.append(_blocks_from_cols(g[k], shard_shape[0]).astype(BF16))
            else:
                srcs.append(g[k].reshape((N_DEV,) + shard_shape))
        lands = [lax.empty(s.shape, BF16) for s in srcs]
        send_sems, recv_sems, bufs, token = _exchange_start(srcs, lands, name=f"grads_start_{tag}_{l}")
        return (names, send_sems, recv_sems, bufs, tag), token

    def scatter_finish(l, flight, after):
        names, send_sems, recv_sems, bufs, tag = flight
        bufs = _exchange_wait(send_sems, recv_sems, bufs, scatter=True, after=after, name=f"grads_wait_{tag}_{l}")
        n = len(names)
        for k, own, land in zip(names, bufs[:n], bufs[n:]):
            results[k] = _adamw(W[k], M[k], V[k], land, results[k], l, me_idx, own=own, name="adamw_" + k)

    gate_names = ("w_rg_a", "w_rg_x")
    gate_results = {k: None for k in gate_names}

    def small_start(l, g):
        lands = [_place(g[k].reshape(-1, hd), me_idx, F32, name="place_gate_grad") for k in gate_names]
        send_sems, recv_sems, bufs, token = _exchange_start(None, lands, name=f"small_start_{l}")
        return (send_sems, recv_sems, bufs), token

    def small_finish(l, flight, after):
        send_sems, recv_sems, bufs = flight
        bufs = _exchange_wait(send_sems, recv_sems, bufs, scatter=False, after=after, name=f"small_wait_{l}")
        for k, g_all in zip(gate_names, bufs):
            gate_results[k] = _adamw(W[k].reshape(L, -1, hd), M[k].reshape(L, -1, hd), V[k].reshape(L, -1, hd), g_all,
                                     gate_results[k], l, me_idx, name="adamw_gate")

    grads = [None] * L
    in_flight = []
    dep = None
    for l in reversed(range(L)):
        dx_mid, g = _layer_bwd_mlp(dx, layers[l], saved[l], wdt=BF16, dep=dep)
        f_mlp, dep = scatter_start(l, ("w_2", "w_1"), g, "mlp")
        flights = [f_mlp]

        small = []

        def on_gate_grads(g_part, l=l, small=small):
            f_small, token = small_start(l, g_part)
            small.append(f_small)
            return token

        def on_weight_grads(g_part, l=l, flights=flights):
            f_mix, token = scatter_start(l, ("w_o", "w_b_out", "w_a_out", "conv_a_w", "conv_b_w", "w_in"), g_part, "mix")
            flights.append(f_mix)
            return token

        dx, g_mix_part = _layer_bwd_mix(dx_mid, layers[l], saved[l], S=S, wdt=BF16, dep=dep,
                                        on_gate_grads=on_gate_grads, on_weight_grads=on_weight_grads)
        dep = None
        g.update(g_mix_part)
        grads[l] = g
        for l_prev, fs, f_sm in in_flight:
            for flight in fs:
                scatter_finish(l_prev, flight, dx)
            small_finish(l_prev, f_sm, dx)
        in_flight = [(l, flights, small[0])]

    vec_names = tuple(k for k in _REPLICATED if k not in gate_names)
    n_vec = sum(W[k].shape[1] for k in vec_names)

    def vec_pack(rows, final, last):
        tail = jnp.concatenate([final.reshape(1, -1), jnp.broadcast_to(last.reshape(1, 1), (1, _LANES))], axis=1)
        tail = jnp.pad(tail, ((0, SUBLANES - L - 1), (0, n_vec - tail.shape[1])))
        return jnp.concatenate([rows, tail], axis=0)

    g_rows = jnp.concatenate([jnp.concatenate([grads[l][k] for k in vec_names], axis=1) for l in range(L)], axis=0)
    land = _place(vec_pack(g_rows, dg_final, loss[0, :1]), me_idx, F32, name="place_vectors")
    vec_send_sems, vec_recv_sems, vec_bufs, _ = _exchange_start(None, [land], name="vectors_start")
    for l_prev, fs, f_sm in in_flight:
        after = dx if L == 1 else results["w_in"][0]
        for flight in fs:
            scatter_finish(l_prev, flight, after)
            after = results[flight[0][-1]][0]
        small_finish(l_prev, f_sm, after)
    out_g, out_d, out_m, out_v = {}, {}, {}, {}
    for k in _SHARDED:
        out_g[k], out_d[k], out_m[k], out_v[k] = results[k]
    for k in gate_names:
        out_g[k], out_d[k], out_m[k], out_v[k] = (a.reshape(W[k].shape) for a in gate_results[k])

    zero = jnp.zeros((1,), F32)
    (g_all,) = _exchange_wait(vec_send_sems, vec_recv_sems, vec_bufs, scatter=False, after=results["w_in"][0],
                              name="vectors_wait")
    vec_out = _adamw(vec_pack(jnp.concatenate([W[k] for k in vec_names], axis=1), g_final, zero)[None],
                     vec_pack(jnp.concatenate([M[k] for k in vec_names], axis=1), m_g_final, zero)[None],
                     vec_pack(jnp.concatenate([V[k] for k in vec_names], axis=1), v_g_final, zero)[None],
                     g_all, None, 0, me_idx, name="adamw_vectors")
    vec_out = [a[0] for a in vec_out]
    for res, arr in zip((out_g, out_d, out_m, out_v), vec_out):
        off = 0
        for k in vec_names:
            res[k] = arr[:L, off:off + W[k].shape[1]]
            off += W[k].shape[1]
        res["g_final"] = arr[L, :g_final.size]
    loss_out = vec_out[0][L, g_final.size]

    return (loss_out, dx.reshape(NB, S, D), *[out_g[k] for k in _WEIGHTS], *[out_d[k] for k in _WEIGHTS],
            *[out_m[k] for k in _WEIGHTS], *[out_v[k] for k in _WEIGHTS])
```

```python
import functools

import jax
import jax.numpy as jnp
from jax import lax
from jax.experimental import pallas as pl
from jax.experimental.pallas import tpu as pltpu

F32 = jnp.float32
BF16 = jnp.bfloat16

EPS = 1e-6
LRU_C = 8.0
N_RNN_HEADS = 16
HEADS_PER_GROUP = 4
N_DEV = 8
ADAM_LR, ADAM_B1, ADAM_B2, ADAM_EPS, ADAM_WD, ADAM_STEP = 0.001, 0.9, 0.999, 1e-08, 0.01, 10

VMEM_LIMIT_BYTES = 48 * 1024 * 1024
CONV_PAD = 32
CONV_CHUNK = 128
SUBLANES = 8


def _cp(*sem):
    return pltpu.CompilerParams(dimension_semantics=sem, vmem_limit_bytes=VMEM_LIMIT_BYTES)


def _sig(x):
    return 1.0 / (1.0 + jnp.exp(-x))


def _gelu(x):
    c = 0.7978845608028654
    return 0.5 * x * (1.0 + jnp.tanh(c * (x + 0.044715 * x * x * x)))


def _gelu_grad(x):
    c = 0.7978845608028654
    th = jnp.tanh(c * (x + 0.044715 * x * x * x))
    return 0.5 * (1.0 + th) + 0.5 * x * (1.0 - th * th) * c * (1.0 + 3.0 * 0.044715 * x * x)


def _mm_nn(a, b, *, tm, tn, name, bias=None, resid=None, relu2=False, out_dtype=F32, a_resident=False):
    M, K = a.shape
    blocked = b.ndim == 3
    N = b.shape[0] * b.shape[2] if blocked else b.shape[1]
    tm = min(tm, M)
    tn = min(tn, N)
    if blocked:
        assert tn == b.shape[2]
    n_extra = (bias is not None) + (resid is not None)

    def body(*refs):
        acc = jnp.dot(refs[0][...].astype(BF16), refs[1][...].astype(BF16), preferred_element_type=F32)
        k = 2
        if bias is not None:
            acc = acc + refs[k][...]
            k += 1
        if resid is not None:
            acc = acc + refs[k][...]
            k += 1
        if relu2:
            p = jnp.maximum(acc, 0.0)
            acc = p * p
        refs[k][...] = acc.astype(out_dtype)

    def spec(shape, index):
        return pl.BlockSpec(shape, (lambda i, j: index(j, i)) if a_resident else index)

    in_specs = [spec((tm, K), lambda j, i: (i, 0))]
    if blocked:
        in_specs.append(spec((None, K, tn), lambda j, i: (j, 0, 0)))
    else:
        in_specs.append(spec((K, tn), lambda j, i: (0, j)))
    args = [a, b]
    if bias is not None:
        in_specs.append(spec((1, tn), lambda j, i: (0, j)))
        args.append(bias)
    if resid is not None:
        in_specs.append(spec((tm, tn), lambda j, i: (i, j)))
        args.append(resid)
    out_specs = spec((tm, tn), lambda j, i: (i, j))
    out_shape = jax.ShapeDtypeStruct((M, N), out_dtype)
    del n_extra
    grid = (M // tm, N // tn) if a_resident else (N // tn, M // tm)
    return pl.pallas_call(body, grid=grid, in_specs=in_specs, out_specs=out_specs,
                          out_shape=out_shape, compiler_params=_cp("parallel", "parallel"), name=name)(*args)


def _mm_nt(a, b, *, tm, tn, tk, name, mul_sqrt=None, resid=None, out_dtype=F32, dep=None, whole_b=False):
    M, N = a.shape
    blocked = b.ndim == 3
    Kout = b.shape[1] if blocked else b.shape[0]
    tm = min(tm, M)
    tn = min(tn, Kout)
    tk = N if whole_b else (b.shape[2] if blocked else min(tk, N))
    nk = N // tk
    nt_dims = (((1,), (1,)), ((), ()))

    def body(*refs):
        acc_ref = refs[-1]
        kk = pl.program_id(2)
        if blocked and whole_b:
            n = b.shape[2]
            part = None
            for jb in range(b.shape[0]):
                pj = lax.dot_general(refs[0][:, jb * n:(jb + 1) * n].astype(BF16), refs[1][jb].astype(BF16), nt_dims,
                                     preferred_element_type=F32)
                part = pj if part is None else part + pj
        else:
            part = lax.dot_general(refs[0][...].astype(BF16), refs[1][...].astype(BF16), nt_dims,
                                   preferred_element_type=F32)

        def finish(acc):
            k = 2
            if mul_sqrt is not None:
                acc = acc * (2.0 * jnp.sqrt(refs[k][...].astype(F32)))
                k += 1
            if resid is not None:
                acc = acc + refs[k][...]
                k += 1
            if dep is not None:
                k += 1
            refs[k][...] = acc.astype(out_dtype)

        if nk == 1:
            finish(part)
            return

        @pl.when(kk == 0)
        def _():
            acc_ref[...] = part

        @pl.when(kk > 0)
        def _():
            acc_ref[...] += part

        @pl.when(kk == nk - 1)
        def _():
            finish(acc_ref[...])

    in_specs = [pl.BlockSpec((tm, tk), lambda i, j, k: (i, k))]
    if blocked and whole_b:
        in_specs.append(pl.BlockSpec((b.shape[0], tn, b.shape[2]), lambda i, j, k: (0, j, 0)))
    elif blocked:
        in_specs.append(pl.BlockSpec((None, tn, tk), lambda i, j, k: (k, j, 0)))
    else:
        in_specs.append(pl.BlockSpec((tn, tk), lambda i, j, k: (j, k)))
    args = [a, b]
    for extra in (mul_sqrt, resid):
        if extra is not None:
            in_specs.append(pl.BlockSpec((tm, tn), lambda i, j, k: (i, j)))
            args.append(extra)
    if dep is not None:
        in_specs.append(pl.BlockSpec(dep.shape, lambda i, j, k: (0, 0)))
        args.append(dep)
    return pl.pallas_call(body, grid=(M // tm, Kout // tn, nk), in_specs=in_specs,
                          out_specs=pl.BlockSpec((tm, tn), lambda i, j, k: (i, j)),
                          out_shape=jax.ShapeDtypeStruct((M, Kout), out_dtype),
                          scratch_shapes=[pltpu.VMEM((tm, tn), F32)] if nk > 1 else [],
                          compiler_params=_cp("parallel", "parallel", "arbitrary"), name=name)(*args)


def _mm_tn(a, b, *, tm, tn, tk, name, out_blocks=None, colsum=False, out_dtype=F32):
    T, M = a.shape
    N = b.shape[1]
    tm = min(tm, M)
    tk = min(tk, T)
    if out_blocks is not None:
        tn = N // out_blocks
        tm = M
    tn = min(tn, N)
    nk = T // tk
    if colsum:
        assert tm == M

    def body(*refs):
        a_ref, b_ref, o_ref, acc_ref = refs[0], refs[1], refs[2], refs[-1]
        kk = pl.program_id(2)
        bv = b_ref[...]
        part = lax.dot_general(a_ref[...].astype(BF16), bv.astype(BF16),
                               (((0,), (0,)), ((), ())), preferred_element_type=F32)

        if colsum:
            csum = jnp.broadcast_to(jnp.sum(bv.astype(F32), axis=0, keepdims=True), (SUBLANES, tn))

        if nk == 1:
            o_ref[...] = part.astype(out_dtype)
            if colsum:
                refs[3][...] = csum
            return

        @pl.when(kk == 0)
        def _():
            acc_ref[...] = part
            if colsum:
                refs[3][...] = csum

        @pl.when(kk > 0)
        def _():
            acc_ref[...] += part
            if colsum:
                refs[3][...] += csum

        @pl.when(kk == nk - 1)
        def _():
            o_ref[...] = acc_ref[...].astype(out_dtype)

    in_specs = [pl.BlockSpec((tk, tm), lambda i, j, k: (k, i)), pl.BlockSpec((tk, tn), lambda i, j, k: (k, j))]
    if out_blocks is not None:
        o_shape = jax.ShapeDtypeStruct((out_blocks, M, tn), out_dtype)
        o_spec = pl.BlockSpec((None, M, tn), lambda i, j, k: (j, 0, 0))
    else:
        o_shape = jax.ShapeDtypeStruct((M, N), out_dtype)
        o_spec = pl.BlockSpec((tm, tn), lambda i, j, k: (i, j))
    if colsum:
        out_shape = (o_shape, jax.ShapeDtypeStruct((SUBLANES, N), F32))
        out_specs = (o_spec, pl.BlockSpec((SUBLANES, tn), lambda i, j, k: (0, j)))
    else:
        out_shape, out_specs = o_shape, o_spec
    return pl.pallas_call(body, grid=(M // tm, N // tn, nk), in_specs=in_specs, out_specs=out_specs,
                          out_shape=out_shape, scratch_shapes=[pltpu.VMEM((tm, tn), F32)] if nk > 1 else [],
                          compiler_params=_cp("parallel", "parallel", "arbitrary"), name=name)(a, b)


def _rms_fwd(x, g, *, name, tr=512):
    T, D = x.shape
    tr = min(tr, T)

    def body(x_ref, g_ref, h_ref):
        xv = x_ref[...]
        r = lax.rsqrt(jnp.mean(xv * xv, axis=-1, keepdims=True) + EPS)
        h_ref[...] = (xv * r * g_ref[...]).astype(BF16)

    return pl.pallas_call(body, grid=(T // tr,),
                          in_specs=[pl.BlockSpec((tr, D), lambda i: (i, 0)), pl.BlockSpec((1, D), lambda i: (0, 0))],
                          out_specs=pl.BlockSpec((tr, D), lambda i: (i, 0)),
                          out_shape=jax.ShapeDtypeStruct((T, D), BF16), compiler_params=_cp("parallel"), name=name)(x, g)


def _rms_bwd(x, g, dh, dres, *, name, tr=512):
    T, D = x.shape
    tr = min(tr, T)

    def body(x_ref, g_ref, dh_ref, dres_ref, dx_ref, dg_ref):
        xv = x_ref[...]
        r = lax.rsqrt(jnp.mean(xv * xv, axis=-1, keepdims=True) + EPS)
        n = xv * r
        dh = dh_ref[...]
        dn = dh * g_ref[...]
        dx_ref[...] = dres_ref[...] + r * (dn - n * jnp.mean(dn * n, axis=-1, keepdims=True))
        part = jnp.sum(dh * n, axis=0, keepdims=True)

        @pl.when(pl.program_id(0) == 0)
        def _():
            dg_ref[...] = part

        @pl.when(pl.program_id(0) > 0)
        def _():
            dg_ref[...] += part

    row = pl.BlockSpec((tr, D), lambda i: (i, 0))
    vec = pl.BlockSpec((1, D), lambda i: (0, 0))
    return pl.pallas_call(body, grid=(T // tr,), in_specs=[row, vec, row, row], out_specs=(row, vec),
                          out_shape=(jax.ShapeDtypeStruct((T, D), F32), jax.ShapeDtypeStruct((1, D), F32)),
                          compiler_params=_cp("arbitrary"), name=name)(x, g, dh, dres)


def _final_loss(x, g, tgt, *, name, tr=512):
    T, D = x.shape
    tr = min(tr, T)

    def body(x_ref, g_ref, t_ref, loss_ref, dx_ref, dg_ref):
        xv = x_ref[...]
        gv = g_ref[...]
        r = lax.rsqrt(jnp.mean(xv * xv, axis=-1, keepdims=True) + EPS)
        n = xv * r
        e = n * gv - t_ref[...]
        lpart = 0.5 * jnp.sum(jnp.mean(e * e, axis=-1, keepdims=True), axis=0, keepdims=True)
        dy = e * (1.0 / D)
        dn = dy * gv
        dx_ref[...] = r * (dn - n * jnp.mean(dn * n, axis=-1, keepdims=True))
        gpart = jnp.sum(dy * n, axis=0, keepdims=True)

        @pl.when(pl.program_id(0) == 0)
        def _():
            dg_ref[...] = gpart
            loss_ref[...] = jnp.broadcast_to(lpart, (1, 128))

        @pl.when(pl.program_id(0) > 0)
        def _():
            dg_ref[...] += gpart
            loss_ref[...] += jnp.broadcast_to(lpart, (1, 128))

    row = pl.BlockSpec((tr, D), lambda i: (i, 0))
    vec = pl.BlockSpec((1, D), lambda i: (0, 0))
    return pl.pallas_call(body, grid=(T // tr,), in_specs=[row, vec, row],
                          out_specs=(pl.BlockSpec((1, 128), lambda i: (0, 0)), row, vec),
                          out_shape=(jax.ShapeDtypeStruct((1, 128), F32), jax.ShapeDtypeStruct((T, D), F32),
                                     jax.ShapeDtypeStruct((1, D), F32)),
                          compiler_params=_cp("arbitrary"), name=name)(x, g, tgt)


def _ln_silu_fwd(u, g, b, *, name, tr=512):
    T, C = u.shape
    tr = min(tr, T)

    def body(u_ref, g_ref, b_ref, o_ref):
        uv = u_ref[...]
        mu = jnp.mean(uv, axis=-1, keepdims=True)
        xc = uv - mu
        r = lax.rsqrt(jnp.mean(xc * xc, axis=-1, keepdims=True) + EPS)
        y = xc * r * g_ref[...] + b_ref[...]
        o_ref[...] = (y * _sig(y)).astype(BF16)

    row = pl.BlockSpec((tr, C), lambda i: (i, 0))
    vec = pl.BlockSpec((1, C), lambda i: (0, 0))
    return pl.pallas_call(body, grid=(T // tr,), in_specs=[row, vec, vec], out_specs=row,
                          out_shape=jax.ShapeDtypeStruct((T, C), BF16), compiler_params=_cp("parallel"),
                          name=name)(u, g, b)


def _ln_silu_bwd(u, g, b, do, *, name, tr=512):
    T, C = u.shape
    tr = min(tr, T)

    def body(u_ref, g_ref, b_ref, do_ref, du_ref, dg_ref, db_ref):
        uv = u_ref[...]
        gv = g_ref[...]
        mu = jnp.mean(uv, axis=-1, keepdims=True)
        xc = uv - mu
        r = lax.rsqrt(jnp.mean(xc * xc, axis=-1, keepdims=True) + EPS)
        n = xc * r
        y = n * gv + b_ref[...]
        s = _sig(y)
        dy = do_ref[...] * (s * (1.0 + y * (1.0 - s)))
        dn = dy * gv
        du_ref[...] = r * (dn - jnp.mean(dn, axis=-1, keepdims=True) - n * jnp.mean(dn * n, axis=-1, keepdims=True))
        gpart = jnp.sum(dy * n, axis=0, keepdims=True)
        bpart = jnp.sum(dy, axis=0, keepdims=True)

        @pl.when(pl.program_id(0) == 0)
        def _():
            dg_ref[...] = gpart
            db_ref[...] = bpart

        @pl.when(pl.program_id(0) > 0)
        def _():
            dg_ref[...] += gpart
            db_ref[...] += bpart

    row = pl.BlockSpec((tr, C), lambda i: (i, 0))
    vec = pl.BlockSpec((1, C), lambda i: (0, 0))
    return pl.pallas_call(body, grid=(T // tr,), in_specs=[row, vec, vec, row], out_specs=(row, vec, vec),
                          out_shape=(jax.ShapeDtypeStruct((T, C), F32), jax.ShapeDtypeStruct((1, C), F32),
                                     jax.ShapeDtypeStruct((1, C), F32)),
                          compiler_params=_cp("arbitrary"), name=name)(u, g, b, do)


def _merge_fwd(z, ya, yb, *, off_sa, off_sb, name, tr=512):
    T, D = ya.shape
    tr = min(tr, T)
    assert off_sa % D == 0 and off_sb % D == 0

    def body(sa_ref, sb_ref, ya_ref, yb_ref, m_ref):
        m_ref[...] = (_sig(sa_ref[...]) * ya_ref[...] + _sig(sb_ref[...]) * yb_ref[...]).astype(BF16)

    row = pl.BlockSpec((tr, D), lambda i: (i, 0))
    return pl.pallas_call(body, grid=(T // tr,),
                          in_specs=[pl.BlockSpec((tr, D), lambda i: (i, off_sa // D)),
                                    pl.BlockSpec((tr, D), lambda i: (i, off_sb // D)), row, row],
                          out_specs=row, out_shape=jax.ShapeDtypeStruct((T, D), BF16),
                          compiler_params=_cp("parallel"), name=name)(z, z, ya, yb)


def _columns_copy(stage_ref, dz_ref, row0, rows, col0, sem):
    dst = dz_ref.at[pl.ds(pl.multiple_of(row0, SUBLANES), rows),
                    pl.ds(pl.multiple_of(col0, 128), stage_ref.shape[1])]
    return pltpu.make_async_copy(stage_ref, dst, sem)


def _put_columns(stage_ref, dz_ref, row0, rows, col0, sem):
    cp = _columns_copy(stage_ref, dz_ref, row0, rows, col0, sem)
    cp.start()
    cp.wait()


def _merge_bwd(z, ya, yb, dm, dz, *, off_sa, off_sb, name, tr=512):
    T, D = ya.shape
    tr = min(tr, T)
    assert off_sb == off_sa + D

    def body(sa_ref, sb_ref, ya_ref, yb_ref, dm_ref, dz_in, dya_ref, dyb_ref, dz_ref, stage, sem):
        del dz_in
        dm = dm_ref[...]
        ga = _sig(sa_ref[...])
        gb = _sig(sb_ref[...])
        dya_ref[...] = (dm * ga).astype(BF16)
        dyb_ref[...] = (dm * gb).astype(BF16)
        stage[:, 0:D] = (dm * ya_ref[...] * ga * (1.0 - ga)).astype(BF16)
        stage[:, D:2 * D] = (dm * yb_ref[...] * gb * (1.0 - gb)).astype(BF16)
        _put_columns(stage, dz_ref, pl.program_id(0) * tr, tr, off_sa, sem)

    row = pl.BlockSpec((tr, D), lambda i: (i, 0))
    o = jax.ShapeDtypeStruct((T, D), BF16)
    return pl.pallas_call(body, grid=(T // tr,),
                          in_specs=[pl.BlockSpec((tr, D), lambda i: (i, off_sa // D)),
                                    pl.BlockSpec((tr, D), lambda i: (i, off_sb // D)), row, row, row, _ANY],
                          out_specs=(row, row, _ANY), out_shape=(o, o, jax.ShapeDtypeStruct(dz.shape, dz.dtype)),
                          scratch_shapes=[pltpu.VMEM((tr, 2 * D), BF16), pltpu.SemaphoreType.DMA],
                          input_output_aliases={5: 2},
                          compiler_params=_cp("parallel"), name=name)(z, z, ya, yb, dm, dz)


def _shift_rows(dst_ref, src_ref, r, total, back):
    for c0 in range(0, total - SUBLANES, CONV_CHUNK):
        n = min(CONV_CHUNK, total - SUBLANES - c0)
        if back:
            dst_ref[SUBLANES + c0:SUBLANES + c0 + n, :] = src_ref[SUBLANES + c0 - r:SUBLANES + c0 - r + n, :]
        else:
            dst_ref[c0:c0 + n, :] = src_ref[c0 + r:c0 + r + n, :]


def _tap_plan(K):
    if K <= SUBLANES:
        return [(0, [(s, K - 1 - s) for s in range(K)])]
    return [(r, [(SUBLANES * q, K - 1 - (SUBLANES * q + r)) for q in range(-(-K // SUBLANES)) if SUBLANES * q + r < K])
            for r in range(SUBLANES)]


def _conv_fwd(z, w, b, *, S, off_v, off_g, name, ct=256):
    T = z.shape[0]
    K, C = w.shape
    ct = min(ct, C)
    ch = min(CONV_CHUNK, S)
    glu = off_g is not None
    assert off_v % ct == 0 and (not glu or off_g % ct == 0)
    assert SUBLANES * ((K - 1) // SUBLANES) <= CONV_PAD - SUBLANES

    def body(*refs):
        if glu:
            v_ref, g_ref, w_ref, b_ref, o_ref, pad_ref, sh_ref = refs
        else:
            v_ref, w_ref, b_ref, o_ref, pad_ref, sh_ref = refs
        pad_ref[0:CONV_PAD, :] = jnp.zeros((CONV_PAD, ct), F32)
        if glu:
            pad_ref[CONV_PAD:CONV_PAD + S, :] = v_ref[...] * _sig(g_ref[...])
        else:
            pad_ref[CONV_PAD:CONV_PAD + S, :] = v_ref[...]
        for r, taps in _tap_plan(K):
            src = pad_ref
            if r > 0:
                _shift_rows(sh_ref, pad_ref, r, CONV_PAD + S, True)
                src = sh_ref
            for l0 in range(0, ct, 128):
                lanes = slice(l0, l0 + 128)
                for c in range(S // ch):
                    acc = None
                    for off, wrow in taps:
                        st = CONV_PAD + c * ch - off
                        term = w_ref[wrow:wrow + 1, lanes] * src[st:st + ch, lanes]
                        acc = term if acc is None else acc + term
                    rows = slice(c * ch, (c + 1) * ch)
                    if r == 0:
                        o_ref[rows, lanes] = acc + b_ref[:, lanes]
                    else:
                        o_ref[rows, lanes] += acc

    in_specs = [pl.BlockSpec((S, ct), lambda j, bb: (bb, off_v // ct + j))]
    args = [z]
    if glu:
        in_specs.append(pl.BlockSpec((S, ct), lambda j, bb: (bb, off_g // ct + j)))
        args.append(z)
    in_specs += [pl.BlockSpec((K, ct), lambda j, bb: (0, j)), pl.BlockSpec((1, ct), lambda j, bb: (0, j))]
    args += [w, b]
    return pl.pallas_call(body, grid=(C // ct, T // S), in_specs=in_specs,
                          out_specs=pl.BlockSpec((S, ct), lambda j, bb: (bb, j)),
                          out_shape=jax.ShapeDtypeStruct((T, C), F32),
                          scratch_shapes=[pltpu.VMEM((CONV_PAD + S, ct), F32), pltpu.VMEM((CONV_PAD + S, ct), F32)],
                          compiler_params=_cp("parallel", "parallel"), name=name)(*args)


def _conv_bwd(z, w, dy, dz, *, S, off_v, off_g, name, ct=256):
    T = z.shape[0]
    K, C = w.shape
    KP = -(-K // SUBLANES) * SUBLANES
    ct = min(ct, C)
    ch = min(CONV_CHUNK, S)
    glu = off_g is not None
    total = S + CONV_PAD

    def body(*refs):
        if glu:
            (v_ref, g_ref, w_ref, dy_ref, dz_in, dz_ref, dw_ref, db_ref,
             pad_ref, sh_ref, padb_ref, shb_ref, du_ref, stage_v, stage_g, sem) = refs
        else:
            (v_ref, w_ref, dy_ref, dz_in, dz_ref, dw_ref, db_ref,
             pad_ref, sh_ref, padb_ref, shb_ref, du_ref, stage_v, sem) = refs
        del dz_in
        j, bb = pl.program_id(0), pl.program_id(1)
        pad_ref[0:CONV_PAD, :] = jnp.zeros((CONV_PAD, ct), F32)
        if glu:
            pad_ref[CONV_PAD:total, :] = v_ref[...] * _sig(g_ref[...])
        else:
            pad_ref[CONV_PAD:total, :] = v_ref[...]
        padb_ref[0:S, :] = dy_ref[...]
        padb_ref[S:total, :] = jnp.zeros((CONV_PAD, ct), F32)

        @pl.when(bb == 0)
        def _():
            dw_ref[...] = jnp.zeros((KP, ct), F32)
            db_ref[...] = jnp.zeros((1, ct), F32)

        for r, taps in _tap_plan(K):
            u_src, d_src = pad_ref, padb_ref
            if r > 0:
                _shift_rows(sh_ref, pad_ref, r, total, True)
                _shift_rows(shb_ref, padb_ref, r, total, False)
                u_src, d_src = sh_ref, shb_ref
            for l0 in range(0, ct, 128):
                lanes = slice(l0, l0 + 128)
                for c in range(S // ch):
                    acc = None
                    for off, wrow in taps:
                        st = c * ch + off
                        term = w_ref[wrow:wrow + 1, lanes] * d_src[st:st + ch, lanes]
                        acc = term if acc is None else acc + term
                    rows = slice(c * ch, (c + 1) * ch)
                    if r == 0:
                        du_ref[rows, lanes] = acc
                    else:
                        du_ref[rows, lanes] += acc
                for off, wrow in taps:
                    acc = None
                    for c in range(S // ch):
                        st = CONV_PAD + c * ch - off
                        prod = padb_ref[c * ch:(c + 1) * ch, lanes] * u_src[st:st + ch, lanes]
                        acc = prod if acc is None else acc + prod
                    dw_ref[wrow:wrow + 1, lanes] += jnp.sum(acc, axis=0, keepdims=True)
        db_ref[...] += jnp.sum(dy_ref[...], axis=0, keepdims=True)
        for l0 in range(0, ct, 128):
            lanes = slice(l0, l0 + 128)
            for c in range(S // ch):
                rows = slice(c * ch, (c + 1) * ch)
                du = du_ref[rows, lanes]
                if glu:
                    sg = _sig(g_ref[rows, lanes])
                    stage_v[rows, lanes] = (du * sg).astype(BF16)
                    stage_g[rows, lanes] = (du * v_ref[rows, lanes] * sg * (1.0 - sg)).astype(BF16)
                else:
                    stage_v[rows, lanes] = du.astype(BF16)
        _put_columns(stage_v, dz_ref, bb * S, S, off_v + j * ct, sem)
        if glu:
            _put_columns(stage_g, dz_ref, bb * S, S, off_g + j * ct, sem)

    blk = lambda off: pl.BlockSpec((S, ct), lambda j, bb: (bb, off // ct + j))
    in_specs = [blk(off_v)]
    args = [z]
    if glu:
        in_specs.append(blk(off_g))
        args.append(z)
    in_specs += [pl.BlockSpec((K, ct), lambda j, bb: (0, j)), blk(0), _ANY]
    args += [w, dy, dz]
    out_shape = (jax.ShapeDtypeStruct(dz.shape, dz.dtype), jax.ShapeDtypeStruct((KP, C), F32),
                 jax.ShapeDtypeStruct((1, C), F32))
    out_specs = (_ANY, pl.BlockSpec((KP, ct), lambda j, bb: (0, j)), pl.BlockSpec((1, ct), lambda j, bb: (0, j)))
    padded = pltpu.VMEM((total, ct), F32)
    stage = pltpu.VMEM((S, ct), BF16)
    return pl.pallas_call(body, grid=(C // ct, T // S), in_specs=in_specs, out_specs=out_specs, out_shape=out_shape,
                          scratch_shapes=[padded, padded, padded, padded, pltpu.VMEM((S, ct), F32), stage]
                          + ([stage] if glu else []) + [pltpu.SemaphoreType.DMA],
                          input_output_aliases={len(args) - 1: 0},
                          compiler_params=_cp("parallel", "arbitrary"), name=name)(*args)


def _softplus_neg(lam):
    return jnp.maximum(-lam, 0.0) + jnp.log1p(jnp.exp(-jnp.abs(lam)))


def _neg_expm1(x):
    u = jnp.exp(x)
    um1 = u - 1.0
    lg = jnp.log(u)
    safe = jnp.where(lg == 0.0, 1.0, lg)
    em1 = jnp.where(um1 == 0.0, x, jnp.where(um1 == -1.0, -1.0, um1 * x / safe))
    return -em1


def _rglru_fwd(v, z, wa, wx, ba, bx, lam, *, S, off_gate, name, tt=512):
    T, C = v.shape
    G, gw, _ = wa.shape
    tt = min(tt, S)
    nt = S // tt
    nlb = gw // 128
    assert gw % 128 == 0 and off_gate % 128 == 0

    def body(*refs):
        v_ref = refs[0]
        gate_refs = refs[1:1 + nlb]
        wa_ref, wx_ref, ba_ref, bx_ref, lam_ref, r_ref, i_ref, h_ref, hb_ref, a_s, b_s, carry_ref = refs[1 + nlb:]
        t = pl.program_id(2)

        @pl.when(t == 0)
        def _():
            carry_ref[...] = jnp.zeros((SUBLANES, gw), F32)

        vv = v_ref[...]
        vb = vv.astype(BF16)
        r = _sig(jnp.dot(vb, wa_ref[...], preferred_element_type=F32) + ba_ref[...])
        ig = _sig(jnp.dot(vb, wx_ref[...], preferred_element_type=F32) + bx_ref[...])
        log_a = -LRU_C * r * _softplus_neg(lam_ref[...])
        mult = jnp.sqrt(_neg_expm1(2.0 * log_a))
        start = jnp.logical_and(t == 0, lax.broadcasted_iota(jnp.int32, (tt, gw), 0) == 0)
        mult = jnp.where(start, 1.0, mult)
        r_ref[...] = r
        i_ref[...] = ig
        a_s[...] = jnp.exp(log_a)
        b_s[...] = mult * ig * vv
        row = lax.broadcasted_iota(jnp.int32, (SUBLANES, gw), 0)

        def step(i, carry):
            st = pl.multiple_of(i * SUBLANES, SUBLANES)
            A = a_s[pl.ds(st, SUBLANES), :]
            B = b_s[pl.ds(st, SUBLANES), :]
            for d in (1, 2, 4):
                m = row >= d
                Bn = jnp.where(m, A * pltpu.roll(B, d, 0) + B, B)
                A = jnp.where(m, A * pltpu.roll(A, d, 0), A)
                B = Bn
            h = B + A * carry
            h_ref[pl.ds(st, SUBLANES), :] = h
            return jnp.broadcast_to(h[SUBLANES - 1:SUBLANES, :], (SUBLANES, gw))

        carry_ref[...] = lax.fori_loop(0, tt // SUBLANES, step, carry_ref[...], unroll=2)
        for k in range(nlb):
            lanes = slice(k * 128, (k + 1) * 128)
            hb_ref[:, lanes] = (h_ref[:, lanes] * _gelu(gate_refs[k][...])).astype(BF16)

    blk = pl.BlockSpec((tt, gw), lambda g, bb, t: (bb * nt + t, g))
    gates = [pl.BlockSpec((tt, 128), lambda g, bb, t, k=k: (bb * nt + t, off_gate // 128 + g * nlb + k)) for k in range(nlb)]
    wsp = pl.BlockSpec((None, gw, gw), lambda g, bb, t: (g, 0, 0))
    vec = pl.BlockSpec((1, gw), lambda g, bb, t: (0, g))
    o = jax.ShapeDtypeStruct((T, C), F32)
    return pl.pallas_call(body, grid=(G, T // S, nt), in_specs=[blk] + gates + [wsp, wsp, vec, vec, vec],
                          out_specs=(blk, blk, blk, blk), out_shape=(o, o, o, jax.ShapeDtypeStruct((T, C), BF16)),
                          scratch_shapes=[pltpu.VMEM((tt, gw), F32), pltpu.VMEM((tt, gw), F32),
                                          pltpu.VMEM((SUBLANES, gw), F32)],
                          compiler_params=_cp("parallel", "parallel", "arbitrary"),
                          name=name)(v, *([z] * nlb), wa, wx, ba, bx, lam)


def _rglru_bwd(dhb, h, z, r, ig, v, wa, wx, lam, dz, *, S, off_gate, name, tt=512):
    T, C = v.shape
    G, gw, _ = wa.shape
    hd = gw // HEADS_PER_GROUP
    tt = min(tt, S)
    nt = S // tt
    n_seq = T // S
    n_tiles = tt // SUBLANES
    nlb = gw // 128

    def body(*refs):
        dhb_ref, h_ref, hp_ref = refs[0:3]
        gate_refs = refs[3:3 + nlb]
        (r_ref, i_ref, v_ref, wa_ref, wx_ref, lam_ref, dz_in, dv_ref, dz_ref, dlam_ref, dba_ref, dbx_ref, dwa_ref, dwx_ref,
         dh_s, a_s, G_s, da_s, carry_ref, stage, acc_a, acc_x, sem) = refs[3 + nlb:]
        del dz_in
        g, bb, t = pl.program_id(0), pl.program_id(1), pl.program_id(2)
        tb = nt - 1 - t
        first = jnp.logical_and(bb == 0, t == 0)

        @pl.when(t == 0)
        def _():
            carry_ref[...] = jnp.zeros((SUBLANES, gw), F32)

        @pl.when(first)
        def _():
            dlam_ref[...] = jnp.zeros((SUBLANES, gw), F32)
            dba_ref[...] = jnp.zeros((SUBLANES, gw), F32)
            dbx_ref[...] = jnp.zeros((SUBLANES, gw), F32)
            acc_a[...] = jnp.zeros((gw, gw), F32)
            acc_x[...] = jnp.zeros((gw, gw), F32)

        rr, ig, vv = r_ref[...], i_ref[...], v_ref[...]
        lam_v = lam_ref[...]
        sp = _softplus_neg(lam_v)
        log_a = -LRU_C * rr * sp
        a = jnp.exp(log_a)
        a_s[...] = a
        for k in range(nlb):
            lanes = slice(k * 128, (k + 1) * 128)
            gate = gate_refs[k][...]
            dhb = dhb_ref[:, lanes]
            dh_s[:, lanes] = dhb * _gelu(gate)
            stage[:, lanes] = (dhb * h_ref[:, lanes] * _gelu_grad(gate)).astype(BF16)
        put = _columns_copy(stage, dz_ref, (bb * nt + tb) * tt, tt, off_gate + g * gw, sem)
        put.start()
        h_before = jnp.where(tb > 0, jnp.broadcast_to(hp_ref[SUBLANES - 1:SUBLANES, :], (SUBLANES, gw)), 0.0)
        row = lax.broadcasted_iota(jnp.int32, (SUBLANES, gw), 0)

        def step(k, qcarry):
            i = n_tiles - 1 - k
            st = pl.multiple_of(i * SUBLANES, SUBLANES)
            stp = pl.multiple_of(jnp.maximum(i - 1, 0) * SUBLANES, SUBLANES)
            A = a_s[pl.ds(st, SUBLANES), :]
            hv = h_ref[pl.ds(st, SUBLANES), :]
            hprev_tile = h_ref[pl.ds(stp, SUBLANES), :]
            dh = dh_s[pl.ds(st, SUBLANES), :]
            Aq = A
            Bq = A * dh
            for d in (1, 2, 4):
                m = row < SUBLANES - d
                Bn = jnp.where(m, Aq * pltpu.roll(Bq, SUBLANES - d, 0) + Bq, Bq)
                Aq = jnp.where(m, Aq * pltpu.roll(Aq, SUBLANES - d, 0), Aq)
                Bq = Bn
            q = Bq + Aq * qcarry
            qnext = jnp.where(row == SUBLANES - 1, qcarry, pltpu.roll(q, SUBLANES - 1, 0))
            gq = dh + qnext
            hlast = jnp.where(i > 0, jnp.broadcast_to(hprev_tile[SUBLANES - 1:SUBLANES, :], (SUBLANES, gw)), h_before)
            hprev = jnp.where(row == 0, hlast, pltpu.roll(hv, 1, 0))
            G_s[pl.ds(st, SUBLANES), :] = gq
            da_s[pl.ds(st, SUBLANES), :] = gq * hprev
            return jnp.broadcast_to(q[0:1, :], (SUBLANES, gw))

        carry_ref[...] = lax.fori_loop(0, n_tiles, step, carry_ref[...], unroll=2)

        Gv = G_s[...]
        mult_raw = jnp.sqrt(_neg_expm1(2.0 * log_a))
        start = jnp.logical_and(tb == 0, lax.broadcasted_iota(jnp.int32, (tt, gw), 0) == 0)
        mult = jnp.where(start, 1.0, mult_raw)
        dmult = jnp.where(start, 0.0, Gv * ig * vv)
        di = Gv * mult * vv
        dla = da_s[...] * a - dmult * (a * a) / jnp.where(start, 1.0, mult_raw)
        dpr = dla * (-LRU_C) * sp * rr * (1.0 - rr)
        dpi = di * ig * (1.0 - ig)
        dlam_ref[...] += jnp.broadcast_to(jnp.sum(dla * (-LRU_C) * rr, axis=0, keepdims=True) * (-_sig(-lam_v)), (SUBLANES, gw))
        dba_ref[...] += jnp.broadcast_to(jnp.sum(dpr, axis=0, keepdims=True), (SUBLANES, gw))
        dbx_ref[...] += jnp.broadcast_to(jnp.sum(dpi, axis=0, keepdims=True), (SUBLANES, gw))
        dprb, dpib = dpr.astype(BF16), dpi.astype(BF16)
        nt_dims = (((1,), (1,)), ((), ()))
        dv_ref[...] = (Gv * mult * ig
                       + lax.dot_general(dprb, wa_ref[...], nt_dims, preferred_element_type=F32)
                       + lax.dot_general(dpib, wx_ref[...], nt_dims, preferred_element_type=F32))
        tn_dims = (((0,), (0,)), ((), ()))
        vb = vv.astype(BF16)
        acc_a[...] += lax.dot_general(vb, dprb, tn_dims, preferred_element_type=F32)
        acc_x[...] += lax.dot_general(vb, dpib, tn_dims, preferred_element_type=F32)

        @pl.when(jnp.logical_and(bb == n_seq - 1, t == nt - 1))
        def _():
            for hh in range(HEADS_PER_GROUP):
                dwa_ref[hh] = acc_a[hh * hd:(hh + 1) * hd, hh * hd:(hh + 1) * hd]
                dwx_ref[hh] = acc_x[hh * hd:(hh + 1) * hd, hh * hd:(hh + 1) * hd]

        put.wait()

    rowblk = lambda g, bb, t: (bb * nt + nt - 1 - t, g)
    blk = pl.BlockSpec((tt, gw), rowblk)
    before = pl.BlockSpec((SUBLANES, gw), lambda g, bb, t: (jnp.maximum((bb * nt + nt - 1 - t) * n_tiles - 1, 0), g))
    gates = [pl.BlockSpec((tt, 128), lambda g, bb, t, k=k: (bb * nt + nt - 1 - t, off_gate // 128 + g * nlb + k))
             for k in range(nlb)]
    wsp = pl.BlockSpec((None, gw, gw), lambda g, bb, t: (g, 0, 0))
    vec = pl.BlockSpec((1, gw), lambda g, bb, t: (0, g))
    acc8 = pl.BlockSpec((SUBLANES, gw), lambda g, bb, t: (0, g))
    heads = pl.BlockSpec((HEADS_PER_GROUP, hd, hd), lambda g, bb, t: (g, 0, 0))
    o8 = jax.ShapeDtypeStruct((SUBLANES, C), F32)
    ow = jax.ShapeDtypeStruct((G * HEADS_PER_GROUP, hd, hd), F32)
    scr = pltpu.VMEM((tt, gw), F32)
    return pl.pallas_call(
        body, grid=(G, n_seq, nt),
        in_specs=[blk, blk, before] + gates + [blk, blk, blk, wsp, wsp, vec, _ANY],
        out_specs=(blk, _ANY, acc8, acc8, acc8, heads, heads),
        out_shape=(jax.ShapeDtypeStruct((T, C), F32), jax.ShapeDtypeStruct(dz.shape, dz.dtype), o8, o8, o8, ow, ow),
        scratch_shapes=[scr, scr, scr, scr, pltpu.VMEM((SUBLANES, gw), F32), pltpu.VMEM((tt, gw), BF16),
                        pltpu.VMEM((gw, gw), F32), pltpu.VMEM((gw, gw), F32), pltpu.SemaphoreType.DMA],
        input_output_aliases={9 + nlb: 1},
        compiler_params=_cp("parallel", "arbitrary", "arbitrary"),
        name=name)(dhb, h, h, *([z] * nlb), r, ig, v, wa, wx, lam, dz)


def _group_weights(w):
    H, hd, _ = w.shape
    G = H // HEADS_PER_GROUP
    eye = jnp.eye(HEADS_PER_GROUP, dtype=w.dtype)
    wg = jnp.einsum("ghij,hk->ghikj", w.reshape(G, HEADS_PER_GROUP, hd, hd), eye)
    return wg.reshape(G, HEADS_PER_GROUP * hd, HEADS_PER_GROUP * hd).astype(BF16)


def _layer_fwd(x, p, *, S, fetch=None):
    D = x.shape[1]
    Dc = p["conv_a_b"].shape[1]
    Dr = p["conv_b_b"].shape[1]
    offs = dict(va=0, ga=Dc, xb=2 * Dc, gb=2 * Dc + Dr, sa=2 * Dc + 2 * Dr, sb=2 * Dc + 2 * Dr + D)
    h = _rms_fwd(x, p["g_mix"], name="rms_mix_fwd")
    if fetch is not None:
        fetch("in", h)
    z = _mm_nn(h, p["w_in"], tm=1024, tn=p["w_in"].shape[2], bias=p["b_in"], a_resident=True, name="mm_in_fwd")
    if fetch is not None:
        fetch("mix", z)
    u1 = _conv_fwd(z, p["conv_a_w"], p["conv_a_b"], S=S, off_v=offs["va"], off_g=offs["ga"], name="conv_a_fwd")
    u2 = _ln_silu_fwd(u1, p["ln_g"], p["ln_b"], name="ln_silu_fwd")
    ya = _mm_nn(u2, p["w_a_out"], tm=1024, tn=1024, name="mm_a_out_fwd")
    v0 = _conv_fwd(z, p["conv_b_w"], p["conv_b_b"], S=S, off_v=offs["xb"], off_g=None, name="conv_b_fwd")
    r, ig, hs, hb = _rglru_fwd(v0, z, p["wg_a"], p["wg_x"], p["b_rg_a"], p["b_rg_x"], p["lam"], S=S, off_gate=offs["gb"],
                               name="rglru_fwd")
    yb = _mm_nn(hb, p["w_b_out"], tm=1024, tn=1024, name="mm_b_out_fwd")
    m = _merge_fwd(z, ya, yb, off_sa=offs["sa"], off_sb=offs["sb"], name="merge_fwd")
    x_mid = _mm_nn(m, p["w_o"], tm=1024, tn=1024, resid=x, name="mm_o_fwd")
    h2 = _rms_fwd(x_mid, p["g_mlp"], name="rms_mlp_fwd")
    if fetch is not None:
        fetch("mlp", h2)
    f = _mm_nn(h2, p["w_1"], tm=1024, tn=p["w_1"].shape[2], relu2=True, out_dtype=BF16, a_resident=True,
               name="mm_1_fwd")
    x_next = _mm_nn(f, p["w_2"], tm=512, tn=1024, resid=x_mid, name="mm_2_fwd")
    saved = dict(x=x, h=h, z=z, u1=u1, u2=u2, ya=ya, v0=v0, r=r, ig=ig, hs=hs, hb=hb, yb=yb, m=m,
                 x_mid=x_mid, h2=h2, f=f, offs=offs)
    return x_next, saved


def _layer_bwd_mlp(dx, p, sv, *, wdt, dep=None):
    g = {}
    g["w_2"] = _mm_tn(sv["f"], dx, tm=2048, tn=1024, tk=1024, out_dtype=wdt, name="mm_2_wgrad")
    dfp = _mm_nt(dx, p["w_2"], tm=1024, tn=1024, tk=1024, mul_sqrt=sv["f"], out_dtype=BF16, dep=dep,
                 name="mm_2_dgrad")
    g["w_1"] = _mm_tn(sv["h2"], dfp, tm=1024, tn=512, tk=4096, out_blocks=p["w_1"].shape[0], out_dtype=wdt,
                      name="mm_1_wgrad")
    dh2 = _mm_nt(dfp, p["w_1"], tm=512, tn=1024, tk=512, whole_b=True, name="mm_1_dgrad")
    dx_mid, g["g_mlp"] = _rms_bwd(sv["x_mid"], p["g_mlp"], dh2, dx, name="rms_mlp_bwd")
    return dx_mid, g


def _layer_bwd_mix(dx_mid, p, sv, *, S, wdt, dep=None, on_gate_grads=None, on_weight_grads=None):
    offs = sv["offs"]
    g = {}
    g["w_o"] = _mm_tn(sv["m"], dx_mid, tm=1024, tn=1024, tk=1024, out_dtype=wdt, name="mm_o_wgrad")
    dm = _mm_nt(dx_mid, p["w_o"], tm=1024, tn=1024, tk=1024, dep=dep, name="mm_o_dgrad")
    dz = lax.empty(sv["z"].shape, BF16)
    dya, dyb, dz = _merge_bwd(sv["z"], sv["ya"], sv["yb"], dm, dz, off_sa=offs["sa"], off_sb=offs["sb"], name="merge_bwd")
    g["w_b_out"] = _mm_tn(sv["hb"], dyb, tm=1536, tn=1024, tk=1024, out_dtype=wdt, name="mm_b_out_wgrad")
    dhb = _mm_nt(dyb, p["w_b_out"], tm=1024, tn=1536, tk=1024, name="mm_b_out_dgrad")
    dv0, dz, dlam, dba, dbx, g["w_rg_a"], g["w_rg_x"] = _rglru_bwd(
        dhb, sv["hs"], sv["z"], sv["r"], sv["ig"], sv["v0"], p["wg_a"], p["wg_x"], p["lam"], dz, S=S, off_gate=offs["gb"],
        name="rglru_bwd")
    g["lam"], g["b_rg_a"], g["b_rg_x"] = dlam[:1], dba[:1], dbx[:1]
    dep_gates = on_gate_grads(g) if on_gate_grads is not None else None
    dz, g["conv_b_w"], g["conv_b_b"] = _conv_bwd(sv["z"], p["conv_b_w"], dv0, dz, S=S, off_v=offs["xb"], off_g=None,
                                                 name="conv_b_bwd")
    g["w_a_out"] = _mm_tn(sv["u2"], dya, tm=1024, tn=1024, tk=4096, out_dtype=wdt, name="mm_a_out_wgrad")
    du2 = _mm_nt(dya, p["w_a_out"], tm=1024, tn=1024, tk=1024, dep=dep_gates, name="mm_a_out_dgrad")
    du1, g["ln_g"], g["ln_b"] = _ln_silu_bwd(sv["u1"], p["ln_g"], p["ln_b"], du2, name="ln_silu_bwd")
    dz, g["conv_a_w"], g["conv_a_b"] = _conv_bwd(sv["z"], p["conv_a_w"], du1, dz, S=S, off_v=offs["va"],
                                                 off_g=offs["ga"], name="conv_a_bwd")
    g["w_in"], db_in = _mm_tn(sv["h"], dz, tm=1024, tn=512, tk=4096, out_blocks=p["w_in"].shape[0], colsum=True,
                              out_dtype=wdt, name="mm_in_wgrad")
    g["b_in"] = db_in[:1]
    dep_in = on_weight_grads(g) if on_weight_grads is not None else None
    dh = _mm_nt(dz, p["w_in"], tm=256, tn=1024, tk=512, whole_b=True, dep=dep_in, name="mm_in_dgrad")
    dx_in, g["g_mix"] = _rms_bwd(sv["x"], p["g_mix"], dh, dx_mid, name="rms_mix_bwd")
    return dx_in, g


def _layer_bwd(dx, p, sv, *, S, wdt=F32):
    dx_mid, g = _layer_bwd_mlp(dx, p, sv, wdt=wdt)
    dx_in, g2 = _layer_bwd_mix(dx_mid, p, sv, S=S, wdt=wdt)
    g.update(g2)
    return dx_in, g


def _local_step(x, tgt, layers, g_final, *, S, wdt=F32):
    saved = []
    for p in layers:
        x, sv = _layer_fwd(x, p, S=S)
        saved.append(sv)
    loss, dx, dg_final = _final_loss(x, g_final, tgt, name="final_loss")
    grads = [None] * len(layers)
    for l in reversed(range(len(layers))):
        dx, grads[l] = _layer_bwd(dx, layers[l], saved[l], S=S, wdt=wdt)
    return loss, dx, grads, dg_final


_HBM = pl.BlockSpec(memory_space=pltpu.HBM)
_MESH = pl.DeviceIdType.MESH


_SEM = pl.BlockSpec(memory_space=pltpu.SEMAPHORE)
_ANY = pl.BlockSpec(memory_space=pl.ANY)
_FLIPS = [(dx, dy, dc) for dx in (0, 1) for dy in (0, 1) for dc in (0, 1)][1:]


def _place(shard, me_idx, dtype, *, name):
    r, cc = shard.shape
    tr = 512 if r % 512 == 0 else r

    def body(me_ref, s_ref, o_ref):
        del me_ref
        o_ref[...] = s_ref[...].astype(dtype)

    grid_spec = pltpu.PrefetchScalarGridSpec(
        num_scalar_prefetch=1, grid=(r // tr,),
        in_specs=[pl.BlockSpec((tr, cc), lambda i, me: (i, 0))],
        out_specs=pl.BlockSpec((None, tr, cc), lambda i, me: (me[0], i, 0)))
    return pl.pallas_call(body, grid_spec=grid_spec, out_shape=jax.ShapeDtypeStruct((N_DEV, r, cc), dtype),
                          compiler_params=_cp("arbitrary"), name=name)(me_idx, shard)


def _exchange_copies(srcs, lands, send_sems, recv_sems):
    x, y, c = lax.axis_index("x"), lax.axis_index("y"), lax.axis_index("c")
    me = 4 * x + 2 * y + c
    pairs = []
    for k, (dx, dy, dc) in enumerate(_FLIPS):
        peer = (1 - x if dx else x, 1 - y if dy else y, 1 - c if dc else c)
        pidx = 4 * peer[0] + 2 * peer[1] + peer[2]
        for a, land in enumerate(lands):
            src = land.at[me] if srcs is None else srcs[a].at[pidx]
            sem = k * len(lands) + a
            out = pltpu.make_async_remote_copy(src_ref=src, dst_ref=land.at[me], send_sem=send_sems.at[sem],
                                               recv_sem=recv_sems.at[sem], device_id=peer, device_id_type=_MESH)
            arrival = pltpu.make_async_remote_copy(src_ref=src, dst_ref=land.at[pidx], send_sem=send_sems.at[sem],
                                                   recv_sem=recv_sems.at[sem], device_id=peer, device_id_type=_MESH)
            pairs.append((out, arrival))
    return pairs


def _exchange_start(srcs, lands, *, name):
    n = len(lands)
    bufs = list(lands) if srcs is None else list(srcs) + list(lands)
    nb = len(bufs)

    def body(*refs):
        ins = refs[:nb]
        send_sems, recv_sems = refs[nb], refs[nb + 1]
        token = refs[-1]
        for out, _ in _exchange_copies(None if srcs is None else ins[:n], ins[nb - n:], send_sems, recv_sems):
            out.start()
        token[...] = jnp.zeros_like(token)

    sems = pltpu.SemaphoreType.DMA((len(_FLIPS) * n,))
    res = pl.pallas_call(
        body, name=name, in_specs=[_HBM] * nb,
        out_shape=(sems, sems, *[pltpu.HBM(b.shape, b.dtype) for b in bufs], jax.ShapeDtypeStruct((SUBLANES, 128), F32)),
        out_specs=(_SEM, _SEM, *[_HBM] * nb, pl.BlockSpec(memory_space=pltpu.VMEM)),
        input_output_aliases={i: 2 + i for i in range(nb)},
        compiler_params=pltpu.CompilerParams(has_side_effects=pltpu.SideEffectType.DATAFLOW_SIDE_EFFECTING),
    )(*[pltpu.with_memory_space_constraint(b, pltpu.HBM) for b in bufs])
    return res[0], res[1], list(res[2:2 + nb]), res[-1]


def _exchange_wait(send_sems, recv_sems, bufs, *, scatter, after, name):
    nb = len(bufs)
    n = nb // 2 if scatter else nb

    def body(*refs):
        ins = refs[:nb]
        for out, arrival in _exchange_copies(ins[:n] if scatter else None, ins[nb - n:], refs[nb], refs[nb + 1]):
            out.wait_send()
            arrival.wait_recv()

    extra = [] if after is None else [after]
    res = pl.pallas_call(
        body, name=name, in_specs=[_HBM] * nb + [_SEM, _SEM] + [_ANY] * len(extra),
        out_shape=tuple(pltpu.HBM(b.shape, b.dtype) for b in bufs), out_specs=tuple([_HBM] * nb),
        input_output_aliases={i: i for i in range(nb)},
        compiler_params=pltpu.CompilerParams(has_side_effects=pltpu.SideEffectType.DATAFLOW_SIDE_EFFECTING),
    )(*bufs, send_sems, recv_sems, *extra)
    return list(res)


def _adamw_math(w, g, m, v):
    m = ADAM_B1 * m + (1.0 - ADAM_B1) * g
    v = ADAM_B2 * v + (1.0 - ADAM_B2) * (g * g)
    m_hat = m / (1.0 - ADAM_B1 ** ADAM_STEP)
    v_hat = v / (1.0 - ADAM_B2 ** ADAM_STEP)
    delta = -ADAM_LR * (m_hat / (jnp.sqrt(v_hat) + ADAM_EPS) + ADAM_WD * w)
    return delta, m, v


def _adamw(w, m, v, parts, prev, layer, me_idx, *, name, own=None):
    L, r, cc = w.shape
    P = parts.shape[0]
    tr = 512 if r % 512 == 0 else r
    if prev is None:
        prev = tuple(lax.empty(w.shape, F32) for _ in range(4))

    def body(me_ref, w_ref, m_ref, v_ref, p_ref, *rest):
        g_ref, d_ref, nm_ref, nv_ref = rest[-4:]
        if own is None:
            g = p_ref[0].astype(F32)
            for q in range(1, P):
                g = g + p_ref[q].astype(F32)
        else:
            me = me_ref[0]
            g = rest[0][...].astype(F32)
            for q in range(P):
                g = g + jnp.where(q == me, 0.0, p_ref[q].astype(F32))
        d, nm, nv = _adamw_math(w_ref[...], g, m_ref[...], v_ref[...])
        g_ref[...] = g
        d_ref[...] = d
        nm_ref[...] = nm
        nv_ref[...] = nv

    blk = pl.BlockSpec((None, tr, cc), lambda i, me: (layer, i, 0))
    in_specs = [blk, blk, blk, pl.BlockSpec((P, tr, cc), lambda i, me: (0, i, 0))]
    args = [me_idx, w, m, v, parts]
    if own is not None:
        in_specs.append(pl.BlockSpec((None, tr, cc), lambda i, me: (me[0], i, 0)))
        args.append(own)
    first_prev = len(args)
    in_specs += [_ANY] * 4
    args += list(prev)
    grid_spec = pltpu.PrefetchScalarGridSpec(num_scalar_prefetch=1, grid=(r // tr,), in_specs=in_specs,
                                             out_specs=(blk, blk, blk, blk))
    o = jax.ShapeDtypeStruct(w.shape, F32)
    return pl.pallas_call(body, grid_spec=grid_spec, out_shape=(o, o, o, o),
                          input_output_aliases={first_prev + i: i for i in range(4)},
                          compiler_params=_cp("parallel"), name=name)(*args)


_SHARDED = ("w_in", "conv_a_w", "w_a_out", "conv_b_w", "w_b_out", "w_o", "w_1", "w_2")
_COL_SHARDED = ("w_in", "conv_a_w", "conv_b_w", "w_1")
_REPLICATED = ("g_mix", "b_in", "conv_a_b", "ln_g", "ln_b", "conv_b_b", "w_rg_a", "b_rg_a", "w_rg_x", "b_rg_x", "lam",
               "g_mlp")
_WEIGHTS = ("g_mix", "w_in", "b_in", "conv_a_w", "conv_a_b", "ln_g", "ln_b", "w_a_out", "conv_b_w", "conv_b_b", "w_rg_a",
            "b_rg_a", "w_rg_x", "b_rg_x", "lam", "w_b_out", "w_o", "g_mlp", "w_1", "w_2", "g_final")
_LANES = 128


def _cols_from_blocks(b):
    nb, K, n = b.shape
    return b.transpose(1, 0, 2).reshape(K, nb * n)


def _blocks_from_cols(w, K):
    n = w.shape[1] // N_DEV
    return w[:K].reshape(K, N_DEV, n).transpose(1, 0, 2)


def kernel(x, g_mix, w_in, b_in, conv_a_w, conv_a_b, ln_g, ln_b, w_a_out, conv_b_w, conv_b_b, w_rg_a, b_rg_a, w_rg_x, b_rg_x, lam, w_b_out, w_o, g_mlp, w_1, w_2, g_final, loss_target, m_g_mix, m_w_in, m_b_in, m_conv_a_w, m_conv_a_b, m_ln_g, m_ln_b, m_w_a_out, m_conv_b_w, m_conv_b_b, m_w_rg_a, m_b_rg_a, m_w_rg_x, m_b_rg_x, m_lam, m_w_b_out, m_w_o, m_g_mlp, m_w_1, m_w_2, m_g_final, v_g_mix, v_w_in, v_b_in, v_conv_a_w, v_conv_a_b, v_ln_g, v_ln_b, v_w_a_out, v_conv_b_w, v_conv_b_b, v_w_rg_a, v_b_rg_a, v_w_rg_x, v_b_rg_x, v_lam, v_w_b_out, v_w_o, v_g_mlp, v_w_1, v_w_2, v_g_final):
    W = dict(g_mix=g_mix, w_in=w_in, b_in=b_in, conv_a_w=conv_a_w, conv_a_b=conv_a_b, ln_g=ln_g, ln_b=ln_b,
             w_a_out=w_a_out, conv_b_w=conv_b_w, conv_b_b=conv_b_b, w_rg_a=w_rg_a, b_rg_a=b_rg_a, w_rg_x=w_rg_x,
             b_rg_x=b_rg_x, lam=lam, w_b_out=w_b_out, w_o=w_o, g_mlp=g_mlp, w_1=w_1, w_2=w_2, g_final=g_final)
    M = dict(g_mix=m_g_mix, w_in=m_w_in, b_in=m_b_in, conv_a_w=m_conv_a_w, conv_a_b=m_conv_a_b, ln_g=m_ln_g, ln_b=m_ln_b,
             w_a_out=m_w_a_out, conv_b_w=m_conv_b_w, conv_b_b=m_conv_b_b, w_rg_a=m_w_rg_a, b_rg_a=m_b_rg_a,
             w_rg_x=m_w_rg_x, b_rg_x=m_b_rg_x, lam=m_lam, w_b_out=m_w_b_out, w_o=m_w_o, g_mlp=m_g_mlp, w_1=m_w_1,
             w_2=m_w_2, g_final=m_g_final)
    V = dict(g_mix=v_g_mix, w_in=v_w_in, b_in=v_b_in, conv_a_w=v_conv_a_w, conv_a_b=v_conv_a_b, ln_g=v_ln_g, ln_b=v_ln_b,
             w_a_out=v_w_a_out, conv_b_w=v_conv_b_w, conv_b_b=v_conv_b_b, w_rg_a=v_w_rg_a, b_rg_a=v_b_rg_a,
             w_rg_x=v_w_rg_x, b_rg_x=v_b_rg_x, lam=v_lam, w_b_out=v_w_b_out, w_o=v_w_o, g_mlp=v_g_mlp, w_1=v_w_1,
             w_2=v_w_2, g_final=v_g_final)
    NB, S, D = x.shape
    L = g_mix.shape[0]
    hd = w_rg_a.shape[-1]
    me_idx = (4 * lax.axis_index("x") + 2 * lax.axis_index("y") + lax.axis_index("c")).astype(jnp.int32).reshape(1)

    stages = (("in", ("w_in",)), ("mix", ("conv_a_w", "w_a_out", "conv_b_w", "w_b_out", "w_o")), ("mlp", ("w_1", "w_2")))
    gathers = {}
    started = jnp.zeros((), F32)
    for l in range(L):
        for stage, names in stages:
            lands = [_place(W[k][l], me_idx, F32 if k.startswith("conv") else BF16, name="place_" + k) for k in names]
            send_sems, recv_sems, bufs, token = _exchange_start(None, lands, name=f"weights_start_{stage}_{l}")
            gathers[l, stage] = (names, send_sems, recv_sems, bufs)
            started = started + token[0, 0]

    xt = x.reshape(NB * S, D)
    layers, saved = [], []
    for l in range(L):
        p = {k: W[k][l][None] for k in ("g_mix", "b_in", "conv_a_b", "ln_g", "ln_b", "conv_b_b", "b_rg_a", "b_rg_x",
                                         "lam", "g_mlp")}
        if l == 0:
            p["g_mix"] = p["g_mix"] + started
        p["wg_a"] = _group_weights(w_rg_a[l])
        p["wg_x"] = _group_weights(w_rg_x[l])

        def fetch(stage, after, l=l, p=p):
            names, send_sems, recv_sems, bufs = gathers[l, stage]
            bufs = _exchange_wait(send_sems, recv_sems, bufs, scatter=False, after=after,
                                  name=f"weights_wait_{stage}_{l}")
            for k, full in zip(names, bufs):
                if k in ("w_in", "w_1"):
                    p[k] = full
                elif k in _COL_SHARDED:
                    p[k] = _cols_from_blocks(full)
                else:
                    p[k] = full.reshape(-1, full.shape[-1])

        layers.append(p)
        xt, sv = _layer_fwd(xt, p, S=S, fetch=fetch)
        saved.append(sv)
    loss, dx, dg_final = _final_loss(xt, g_final[None], loss_target.reshape(NB * S, D), name="final_loss")

    results = {k: None for k in _SHARDED}

    def scatter_start(l, names, g, tag):
        srcs = []
        for k in names:
            shard_shape = W[k].shape[1:]
            if k in ("w_in", "w_1"):
                srcs.append(g[k])
            elif k in _COL_SHARDED:
                srcs.append(_blocks_from_cols(g[k], shard_shape[0]).astype(BF16))
            else:
                srcs.append(g[k].reshape((N_DEV,) + shard_shape))
        lands = [lax.empty(s.shape, BF16) for s in srcs]
        send_sems, recv_sems, bufs, token = _exchange_start(srcs, lands, name=f"grads_start_{tag}_{l}")
        return (names, send_sems, recv_sems, bufs, tag), token

    def scatter_finish(l, flight, after):
        names, send_sems, recv_sems, bufs, tag = flight
        bufs = _exchange_wait(send_sems, recv_sems, bufs, scatter=True, after=after, name=f"grads_wait_{tag}_{l}")
        n = len(names)
        for k, own, land in zip(names, bufs[:n], bufs[n:]):
            results[k] = _adamw(W[k], M[k], V[k], land, results[k], l, me_idx, own=own, name="adamw_" + k)

    gate_names = ("w_rg_a", "w_rg_x")
    gate_results = {k: None for k in gate_names}

    def small_start(l, g):
        lands = [_place(g[k].reshape(-1, hd), me_idx, F32, name="place_gate_grad") for k in gate_names]
        send_sems, recv_sems, bufs, token = _exchange_start(None, lands, name=f"small_start_{l}")
        return (send_sems, recv_sems, bufs), token

    def small_finish(l, flight, after):
        send_sems, recv_sems, bufs = flight
        bufs = _exchange_wait(send_sems, recv_sems, bufs, scatter=False, after=after, name=f"small_wait_{l}")
        for k, g_all in zip(gate_names, bufs):
            gate_results[k] = _adamw(W[k].reshape(L, -1, hd), M[k].reshape(L, -1, hd), V[k].reshape(L, -1, hd), g_all,
                                     gate_results[k], l, me_idx, name="adamw_gate")

    grads = [None] * L
    in_flight = []
    dep = None
    for l in reversed(range(L)):
        dx_mid, g = _layer_bwd_mlp(dx, layers[l], saved[l], wdt=BF16, dep=dep)
        f_mlp, dep = scatter_start(l, ("w_2", "w_1"), g, "mlp")
        flights = [f_mlp]

        small = []

        def on_gate_grads(g_part, l=l, small=small):
            f_small, token = small_start(l, g_part)
            small.append(f_small)
            return token

        def on_weight_grads(g_part, l=l, flights=flights):
            f_mix, token = scatter_start(l, ("w_o", "w_b_out", "w_a_out", "conv_a_w", "conv_b_w", "w_in"), g_part, "mix")
            flights.append(f_mix)
            return token

        dx, g_mix_part = _layer_bwd_mix(dx_mid, layers[l], saved[l], S=S, wdt=BF16, dep=dep,
                                        on_gate_grads=on_gate_grads, on_weight_grads=on_weight_grads)
        dep = None
        g.update(g_mix_part)
        grads[l] = g
        for l_prev, fs, f_sm in in_flight:
            for flight in fs:
                scatter_finish(l_prev, flight, dx)
            small_finish(l_prev, f_sm, dx)
        in_flight = [(l, flights, small[0])]

    vec_names = tuple(k for k in _REPLICATED if k not in gate_names)
    n_vec = sum(W[k].shape[1] for k in vec_names)

    def vec_pack(rows, final, last):
        tail = jnp.concatenate([final.reshape(1, -1), jnp.broadcast_to(last.reshape(1, 1), (1, _LANES))], axis=1)
        tail = jnp.pad(tail, ((0, SUBLANES - L - 1), (0, n_vec - tail.shape[1])))
        return jnp.concatenate([rows, tail], axis=0)

    g_rows = jnp.concatenate([jnp.concatenate([grads[l][k] for k in vec_names], axis=1) for l in range(L)], axis=0)
    land = _place(vec_pack(g_rows, dg_final, loss[0, :1]), me_idx, F32, name="place_vectors")
    vec_send_sems, vec_recv_sems, vec_bufs, _ = _exchange_start(None, [land], name="vectors_start")
    for l_prev, fs, f_sm in in_flight:
        after = dx if L == 1 else results["w_in"][0]
        for flight in fs:
            scatter_finish(l_prev, flight, after)
            after = results[flight[0][-1]][0]
        small_finish(l_prev, f_sm, after)
    out_g, out_d, out_m, out_v = {}, {}, {}, {}
    for k in _SHARDED:
        out_g[k], out_d[k], out_m[k], out_v[k] = results[k]
    for k in gate_names:
        out_g[k], out_d[k], out_m[k], out_v[k] = (a.reshape(W[k].shape) for a in gate_results[k])

    zero = jnp.zeros((1,), F32)
    (g_all,) = _exchange_wait(vec_send_sems, vec_recv_sems, vec_bufs, scatter=False, after=results["w_in"][0],
                              name="vectors_wait")
    vec_out = _adamw(vec_pack(jnp.concatenate([W[k] for k in vec_names], axis=1), g_final, zero)[None],
                     vec_pack(jnp.concatenate([M[k] for k in vec_names], axis=1), m_g_final, zero)[None],
                     vec_pack(jnp.concatenate([V[k] for k in vec_names], axis=1), v_g_final, zero)[None],
                     g_all, None, 0, me_idx, name="adamw_vectors")
    vec_out = [a[0] for a in vec_out]
    for res, arr in zip((out_g, out_d, out_m, out_v), vec_out):
        off = 0
        for k in vec_names:
            res[k] = arr[:L, off:off + W[k].shape[1]]
            off += W[k].shape[1]
        res["g_final"] = arr[L, :g_final.size]
    loss_out = vec_out[0][L, g_final.size]

    return (loss_out, dx.reshape(NB, S, D), *[out_g[k] for k in _WEIGHTS], *[out_d[k] for k in _WEIGHTS],
            *[out_m[k] for k in _WEIGHTS], *[out_v[k] for k in _WEIGHTS])
```

```python
import functools

import jax
import jax.numpy as jnp
from jax import lax
from jax.experimental import pallas as pl
from jax.experimental.pallas import tpu as pltpu

F32 = jnp.float32
BF16 = jnp.bfloat16

EPS = 1e-6
LRU_C = 8.0
N_RNN_HEADS = 16
HEADS_PER_GROUP = 4
N_DEV = 8
ADAM_LR, ADAM_B1, ADAM_B2, ADAM_EPS, ADAM_WD, ADAM_STEP = 0.001, 0.9, 0.999, 1e-08, 0.01, 10

VMEM_LIMIT_BYTES = 48 * 1024 * 1024
CONV_PAD = 32
CONV_CHUNK = 128
SUBLANES = 8


def _cp(*sem):
    return pltpu.CompilerParams(dimension_semantics=sem, vmem_limit_bytes=VMEM_LIMIT_BYTES)


def _sig(x):
    return 1.0 / (1.0 + jnp.exp(-x))


def _gelu(x):
    c = 0.7978845608028654
    return 0.5 * x * (1.0 + jnp.tanh(c * (x + 0.044715 * x * x * x)))


def _gelu_grad(x):
    c = 0.7978845608028654
    th = jnp.tanh(c * (x + 0.044715 * x * x * x))
    return 0.5 * (1.0 + th) + 0.5 * x * (1.0 - th * th) * c * (1.0 + 3.0 * 0.044715 * x * x)


def _mm_nn(a, b, *, tm, tn, name, bias=None, resid=None, relu2=False, out_dtype=F32, a_resident=False):
    M, K = a.shape
    blocked = b.ndim == 3
    N = b.shape[0] * b.shape[2] if blocked else b.shape[1]
    tm = min(tm, M)
    tn = min(tn, N)
    if blocked:
        assert tn == b.shape[2]
    n_extra = (bias is not None) + (resid is not None)

    def body(*refs):
        acc = jnp.dot(refs[0][...].astype(BF16), refs[1][...].astype(BF16), preferred_element_type=F32)
        k = 2
        if bias is not None:
            acc = acc + refs[k][...]
            k += 1
        if resid is not None:
            acc = acc + refs[k][...]
            k += 1
        if relu2:
            p = jnp.maximum(acc, 0.0)
            acc = p * p
        refs[k][...] = acc.astype(out_dtype)

    def spec(shape, index):
        return pl.BlockSpec(shape, (lambda i, j: index(j, i)) if a_resident else index)

    in_specs = [spec((tm, K), lambda j, i: (i, 0))]
    if blocked:
        in_specs.append(spec((None, K, tn), lambda j, i: (j, 0, 0)))
    else:
        in_specs.append(spec((K, tn), lambda j, i: (0, j)))
    args = [a, b]
    if bias is not None:
        in_specs.append(spec((1, tn), lambda j, i: (0, j)))
        args.append(bias)
    if resid is not None:
        in_specs.append(spec((tm, tn), lambda j, i: (i, j)))
        args.append(resid)
    out_specs = spec((tm, tn), lambda j, i: (i, j))
    out_shape = jax.ShapeDtypeStruct((M, N), out_dtype)
    del n_extra
    grid = (M // tm, N // tn) if a_resident else (N // tn, M // tm)
    return pl.pallas_call(body, grid=grid, in_specs=in_specs, out_specs=out_specs,
                          out_shape=out_shape, compiler_params=_cp("parallel", "parallel"), name=name)(*args)


def _mm_nt(a, b, *, tm, tn, tk, name, mul_sqrt=None, resid=None, out_dtype=F32, dep=None, whole_b=False):
    M, N = a.shape
    blocked = b.ndim == 3
    Kout = b.shape[1] if blocked else b.shape[0]
    tm = min(tm, M)
    tn = min(tn, Kout)
    tk = N if whole_b else (b.shape[2] if blocked else min(tk, N))
    nk = N // tk
    nt_dims = (((1,), (1,)), ((), ()))

    def body(*refs):
        acc_ref = refs[-1]
        kk = pl.program_id(2)
        if blocked and whole_b:
            n = b.shape[2]
            part = None
            for jb in range(b.shape[0]):
                pj = lax.dot_general(refs[0][:, jb * n:(jb + 1) * n].astype(BF16), refs[1][jb].astype(BF16), nt_dims,
                                     preferred_element_type=F32)
                part = pj if part is None else part + pj
        else:
            part = lax.dot_general(refs[0][...].astype(BF16), refs[1][...].astype(BF16), nt_dims,
                                   preferred_element_type=F32)

        def finish(acc):
            k = 2
            if mul_sqrt is not None:
                acc = acc * (2.0 * jnp.sqrt(refs[k][...].astype(F32)))
                k += 1
            if resid is not None:
                acc = acc + refs[k][...]
                k += 1
            if dep is not None:
                k += 1
            refs[k][...] = acc.astype(out_dtype)

        if nk == 1:
            finish(part)
            return

        @pl.when(kk == 0)
        def _():
            acc_ref[...] = part

        @pl.when(kk > 0)
        def _():
            acc_ref[...] += part

        @pl.when(kk == nk - 1)
        def _():
            finish(acc_ref[...])

    in_specs = [pl.BlockSpec((tm, tk), lambda i, j, k: (i, k))]
    if blocked and whole_b:
        in_specs.append(pl.BlockSpec((b.shape[0], tn, b.shape[2]), lambda i, j, k: (0, j, 0)))
    elif blocked:
        in_specs.append(pl.BlockSpec((None, tn, tk), lambda i, j, k: (k, j, 0)))
    else:
        in_specs.append(pl.BlockSpec((tn, tk), lambda i, j, k: (j, k)))
    args = [a, b]
    for extra in (mul_sqrt, resid):
        if extra is not None:
            in_specs.append(pl.BlockSpec((tm, tn), lambda i, j, k: (i, j)))
            args.append(extra)
    if dep is not None:
        in_specs.append(pl.BlockSpec(dep.shape, lambda i, j, k: (0, 0)))
        args.append(dep)
    return pl.pallas_call(body, grid=(M // tm, Kout // tn, nk), in_specs=in_specs,
                          out_specs=pl.BlockSpec((tm, tn), lambda i, j, k: (i, j)),
                          out_shape=jax.ShapeDtypeStruct((M, Kout), out_dtype),
                          scratch_shapes=[pltpu.VMEM((tm, tn), F32)] if nk > 1 else [],
                          compiler_params=_cp("parallel", "parallel", "arbitrary"), name=name)(*args)


def _mm_tn(a, b, *, tm, tn, tk, name, out_blocks=None, colsum=False, out_dtype=F32):
    T, M = a.shape
    N = b.shape[1]
    tm = min(tm, M)
    tk = min(tk, T)
    if out_blocks is not None:
        tn = N // out_blocks
        tm = M
    tn = min(tn, N)
    nk = T // tk
    if colsum:
        assert tm == M

    def body(*refs):
        a_ref, b_ref, o_ref, acc_ref = refs[0], refs[1], refs[2], refs[-1]
        kk = pl.program_id(2)
        bv = b_ref[...]
        part = lax.dot_general(a_ref[...].astype(BF16), bv.astype(BF16),
                               (((0,), (0,)), ((), ())), preferred_element_type=F32)

        if colsum:
            csum = jnp.broadcast_to(jnp.sum(bv.astype(F32), axis=0, keepdims=True), (SUBLANES, tn))

        if nk == 1:
            o_ref[...] = part.astype(out_dtype)
            if colsum:
                refs[3][...] = csum
            return

        @pl.when(kk == 0)
        def _():
            acc_ref[...] = part
            if colsum:
                refs[3][...] = csum

        @pl.when(kk > 0)
        def _():
            acc_ref[...] += part
            if colsum:
                refs[3][...] += csum

        @pl.when(kk == nk - 1)
        def _():
            o_ref[...] = acc_ref[...].astype(out_dtype)

    in_specs = [pl.BlockSpec((tk, tm), lambda i, j, k: (k, i)), pl.BlockSpec((tk, tn), lambda i, j, k: (k, j))]
    if out_blocks is not None:
        o_shape = jax.ShapeDtypeStruct((out_blocks, M, tn), out_dtype)
        o_spec = pl.BlockSpec((None, M, tn), lambda i, j, k: (j, 0, 0))
    else:
        o_shape = jax.ShapeDtypeStruct((M, N), out_dtype)
        o_spec = pl.BlockSpec((tm, tn), lambda i, j, k: (i, j))
    if colsum:
        out_shape = (o_shape, jax.ShapeDtypeStruct((SUBLANES, N), F32))
        out_specs = (o_spec, pl.BlockSpec((SUBLANES, tn), lambda i, j, k: (0, j)))
    else:
        out_shape, out_specs = o_shape, o_spec
    return pl.pallas_call(body, grid=(M // tm, N // tn, nk), in_specs=in_specs, out_specs=out_specs,
                          out_shape=out_shape, scratch_shapes=[pltpu.VMEM((tm, tn), F32)] if nk > 1 else [],
                          compiler_params=_cp("parallel", "parallel", "arbitrary"), name=name)(a, b)


def _rms_fwd(x, g, *, name, tr=512):
    T, D = x.shape
    tr = min(tr, T)

    def body(x_ref, g_ref, h_ref):
        xv = x_ref[...]
        r = lax.rsqrt(jnp.mean(xv * xv, axis=-1, keepdims=True) + EPS)
        h_ref[...] = (xv * r * g_ref[...]).astype(BF16)

    return pl.pallas_call(body, grid=(T // tr,),
                          in_specs=[pl.BlockSpec((tr, D), lambda i: (i, 0)), pl.BlockSpec((1, D), lambda i: (0, 0))],
                          out_specs=pl.BlockSpec((tr, D), lambda i: (i, 0)),
                          out_shape=jax.ShapeDtypeStruct((T, D), BF16), compiler_params=_cp("parallel"), name=name)(x, g)


def _rms_bwd(x, g, dh, dres, *, name, tr=512):
    T, D = x.shape
    tr = min(tr, T)

    def body(x_ref, g_ref, dh_ref, dres_ref, dx_ref, dg_ref):
        xv = x_ref[...]
        r = lax.rsqrt(jnp.mean(xv * xv, axis=-1, keepdims=True) + EPS)
        n = xv * r
        dh = dh_ref[...]
        dn = dh * g_ref[...]
        dx_ref[...] = dres_ref[...] + r * (dn - n * jnp.mean(dn * n, axis=-1, keepdims=True))
        part = jnp.sum(dh * n, axis=0, keepdims=True)

        @pl.when(pl.program_id(0) == 0)
        def _():
            dg_ref[...] = part

        @pl.when(pl.program_id(0) > 0)
        def _():
            dg_ref[...] += part

    row = pl.BlockSpec((tr, D), lambda i: (i, 0))
    vec = pl.BlockSpec((1, D), lambda i: (0, 0))
    return pl.pallas_call(body, grid=(T // tr,), in_specs=[row, vec, row, row], out_specs=(row, vec),
                          out_shape=(jax.ShapeDtypeStruct((T, D), F32), jax.ShapeDtypeStruct((1, D), F32)),
                          compiler_params=_cp("arbitrary"), name=name)(x, g, dh, dres)


def _final_loss(x, g, tgt, *, name, tr=512):
    T, D = x.shape
    tr = min(tr, T)

    def body(x_ref, g_ref, t_ref, loss_ref, dx_ref, dg_ref):
        xv = x_ref[...]
        gv = g_ref[...]
        r = lax.rsqrt(jnp.mean(xv * xv, axis=-1, keepdims=True) + EPS)
        n = xv * r
        e = n * gv - t_ref[...]
        lpart = 0.5 * jnp.sum(jnp.mean(e * e, axis=-1, keepdims=True), axis=0, keepdims=True)
        dy = e * (1.0 / D)
        dn = dy * gv
        dx_ref[...] = r * (dn - n * jnp.mean(dn * n, axis=-1, keepdims=True))
        gpart = jnp.sum(dy * n, axis=0, keepdims=True)

        @pl.when(pl.program_id(0) == 0)
        def _():
            dg_ref[...] = gpart
            loss_ref[...] = jnp.broadcast_to(lpart, (1, 128))

        @pl.when(pl.program_id(0) > 0)
        def _():
            dg_ref[...] += gpart
            loss_ref[...] += jnp.broadcast_to(lpart, (1, 128))

    row = pl.BlockSpec((tr, D), lambda i: (i, 0))
    vec = pl.BlockSpec((1, D), lambda i: (0, 0))
    return pl.pallas_call(body, grid=(T // tr,), in_specs=[row, vec, row],
                          out_specs=(pl.BlockSpec((1, 128), lambda i: (0, 0)), row, vec),
                          out_shape=(jax.ShapeDtypeStruct((1, 128), F32), jax.ShapeDtypeStruct((T, D), F32),
                                     jax.ShapeDtypeStruct((1, D), F32)),
                          compiler_params=_cp("arbitrary"), name=name)(x, g, tgt)


def _ln_silu_fwd(u, g, b, *, name, tr=512):
    T, C = u.shape
    tr = min(tr, T)

    def body(u_ref, g_ref, b_ref, o_ref):
        uv = u_ref[...]
        mu = jnp.mean(uv, axis=-1, keepdims=True)
        xc = uv - mu
        r = lax.rsqrt(jnp.mean(xc * xc, axis=-1, keepdims=True) + EPS)
        y = xc * r * g_ref[...] + b_ref[...]
        o_ref[...] = (y * _sig(y)).astype(BF16)

    row = pl.BlockSpec((tr, C), lambda i: (i, 0))
    vec = pl.BlockSpec((1, C), lambda i: (0, 0))
    return pl.pallas_call(body, grid=(T // tr,), in_specs=[row, vec, vec], out_specs=row,
                          out_shape=jax.ShapeDtypeStruct((T, C), BF16), compiler_params=_cp("parallel"),
                          name=name)(u, g, b)


def _ln_silu_bwd(u, g, b, do, *, name, tr=512):
    T, C = u.shape
    tr = min(tr, T)

    def body(u_ref, g_ref, b_ref, do_ref, du_ref, dg_ref, db_ref):
        uv = u_ref[...]
        gv = g_ref[...]
        mu = jnp.mean(uv, axis=-1, keepdims=True)
        xc = uv - mu
        r = lax.rsqrt(jnp.mean(xc * xc, axis=-1, keepdims=True) + EPS)
        n = xc * r
        y = n * gv + b_ref[...]
        s = _sig(y)
        dy = do_ref[...] * (s * (1.0 + y * (1.0 - s)))
        dn = dy * gv
        du_ref[...] = r * (dn - jnp.mean(dn, axis=-1, keepdims=True) - n * jnp.mean(dn * n, axis=-1, keepdims=True))
        gpart = jnp.sum(dy * n, axis=0, keepdims=True)
        bpart = jnp.sum(dy, axis=0, keepdims=True)

        @pl.when(pl.program_id(0) == 0)
        def _():
            dg_ref[...] = gpart
            db_ref[...] = bpart

        @pl.when(pl.program_id(0) > 0)
        def _():
            dg_ref[...] += gpart
            db_ref[...] += bpart

    row = pl.BlockSpec((tr, C), lambda i: (i, 0))
    vec = pl.BlockSpec((1, C), lambda i: (0, 0))
    return pl.pallas_call(body, grid=(T // tr,), in_specs=[row, vec, vec, row], out_specs=(row, vec, vec),
                          out_shape=(jax.ShapeDtypeStruct((T, C), F32), jax.ShapeDtypeStruct((1, C), F32),
                                     jax.ShapeDtypeStruct((1, C), F32)),
                          compiler_params=_cp("arbitrary"), name=name)(u, g, b, do)


def _merge_fwd(z, ya, yb, *, off_sa, off_sb, name, tr=512):
    T, D = ya.shape
    tr = min(tr, T)
    assert off_sa % D == 0 and off_sb % D == 0

    def body(sa_ref, sb_ref, ya_ref, yb_ref, m_ref):
        m_ref[...] = (_sig(sa_ref[...].astype(F32)) * ya_ref[...]
                      + _sig(sb_ref[...].astype(F32)) * yb_ref[...]).astype(BF16)

    row = pl.BlockSpec((tr, D), lambda i: (i, 0))
    return pl.pallas_call(body, grid=(T // tr,),
                          in_specs=[pl.BlockSpec((tr, D), lambda i: (i, off_sa // D)),
                                    pl.BlockSpec((tr, D), lambda i: (i, off_sb // D)), row, row],
                          out_specs=row, out_shape=jax.ShapeDtypeStruct((T, D), BF16),
                          compiler_params=_cp("parallel"), name=name)(z, z, ya, yb)


def _columns_copy(stage_ref, dz_ref, row0, rows, col0, sem):
    dst = dz_ref.at[pl.ds(pl.multiple_of(row0, SUBLANES), rows),
                    pl.ds(pl.multiple_of(col0, 128), stage_ref.shape[1])]
    return pltpu.make_async_copy(stage_ref, dst, sem)


def _put_columns(stage_ref, dz_ref, row0, rows, col0, sem):
    cp = _columns_copy(stage_ref, dz_ref, row0, rows, col0, sem)
    cp.start()
    cp.wait()


def _merge_bwd(z, ya, yb, dm, dz, *, off_sa, off_sb, name, tr=512):
    T, D = ya.shape
    tr = min(tr, T)
    assert off_sb == off_sa + D

    def body(sa_ref, sb_ref, ya_ref, yb_ref, dm_ref, dz_in, dya_ref, dyb_ref, dz_ref, stage, sem):
        del dz_in
        dm = dm_ref[...]
        ga = _sig(sa_ref[...].astype(F32))
        gb = _sig(sb_ref[...].astype(F32))
        dya_ref[...] = (dm * ga).astype(BF16)
        dyb_ref[...] = (dm * gb).astype(BF16)
        stage[:, 0:D] = (dm * ya_ref[...] * ga * (1.0 - ga)).astype(BF16)
        stage[:, D:2 * D] = (dm * yb_ref[...] * gb * (1.0 - gb)).astype(BF16)
        _put_columns(stage, dz_ref, pl.program_id(0) * tr, tr, off_sa, sem)

    row = pl.BlockSpec((tr, D), lambda i: (i, 0))
    o = jax.ShapeDtypeStruct((T, D), BF16)
    return pl.pallas_call(body, grid=(T // tr,),
                          in_specs=[pl.BlockSpec((tr, D), lambda i: (i, off_sa // D)),
                                    pl.BlockSpec((tr, D), lambda i: (i, off_sb // D)), row, row, row, _ANY],
                          out_specs=(row, row, _ANY), out_shape=(o, o, jax.ShapeDtypeStruct(dz.shape, dz.dtype)),
                          scratch_shapes=[pltpu.VMEM((tr, 2 * D), BF16), pltpu.SemaphoreType.DMA],
                          input_output_aliases={5: 2},
                          compiler_params=_cp("parallel"), name=name)(z, z, ya, yb, dm, dz)


def _shift_rows(dst_ref, src_ref, r, total, back):
    for c0 in range(0, total - SUBLANES, CONV_CHUNK):
        n = min(CONV_CHUNK, total - SUBLANES - c0)
        if back:
            dst_ref[SUBLANES + c0:SUBLANES + c0 + n, :] = src_ref[SUBLANES + c0 - r:SUBLANES + c0 - r + n, :]
        else:
            dst_ref[c0:c0 + n, :] = src_ref[c0 + r:c0 + r + n, :]


def _tap_plan(K):
    if K <= SUBLANES:
        return [(0, [(s, K - 1 - s) for s in range(K)])]
    return [(r, [(SUBLANES * q, K - 1 - (SUBLANES * q + r)) for q in range(-(-K // SUBLANES)) if SUBLANES * q + r < K])
            for r in range(SUBLANES)]


def _conv_fwd(z, w, b, *, S, off_v, off_g, name, ct=256):
    T = z.shape[0]
    K, C = w.shape
    ct = min(ct, C)
    ch = min(CONV_CHUNK, S)
    glu = off_g is not None
    assert off_v % ct == 0 and (not glu or off_g % ct == 0)
    assert SUBLANES * ((K - 1) // SUBLANES) <= CONV_PAD - SUBLANES

    def body(*refs):
        if glu:
            v_ref, g_ref, w_ref, b_ref, o_ref, pad_ref, sh_ref = refs
        else:
            v_ref, w_ref, b_ref, o_ref, pad_ref, sh_ref = refs
        pad_ref[0:CONV_PAD, :] = jnp.zeros((CONV_PAD, ct), F32)
        if glu:
            pad_ref[CONV_PAD:CONV_PAD + S, :] = v_ref[...].astype(F32) * _sig(g_ref[...].astype(F32))
        else:
            pad_ref[CONV_PAD:CONV_PAD + S, :] = v_ref[...].astype(F32)
        for r, taps in _tap_plan(K):
            src = pad_ref
            if r > 0:
                _shift_rows(sh_ref, pad_ref, r, CONV_PAD + S, True)
                src = sh_ref
            for l0 in range(0, ct, 128):
                lanes = slice(l0, l0 + 128)
                for c in range(S // ch):
                    acc = None
                    for off, wrow in taps:
                        st = CONV_PAD + c * ch - off
                        term = w_ref[wrow:wrow + 1, lanes] * src[st:st + ch, lanes]
                        acc = term if acc is None else acc + term
                    rows = slice(c * ch, (c + 1) * ch)
                    if r == 0:
                        o_ref[rows, lanes] = acc + b_ref[:, lanes]
                    else:
                        o_ref[rows, lanes] += acc

    in_specs = [pl.BlockSpec((S, ct), lambda j, bb: (bb, off_v // ct + j))]
    args = [z]
    if glu:
        in_specs.append(pl.BlockSpec((S, ct), lambda j, bb: (bb, off_g // ct + j)))
        args.append(z)
    in_specs += [pl.BlockSpec((K, ct), lambda j, bb: (0, j)), pl.BlockSpec((1, ct), lambda j, bb: (0, j))]
    args += [w, b]
    return pl.pallas_call(body, grid=(C // ct, T // S), in_specs=in_specs,
                          out_specs=pl.BlockSpec((S, ct), lambda j, bb: (bb, j)),
                          out_shape=jax.ShapeDtypeStruct((T, C), F32),
                          scratch_shapes=[pltpu.VMEM((CONV_PAD + S, ct), F32), pltpu.VMEM((CONV_PAD + S, ct), F32)],
                          compiler_params=_cp("parallel", "parallel"), name=name)(*args)


def _conv_bwd(z, w, dy, dz, *, S, off_v, off_g, name, ct=256):
    T = z.shape[0]
    K, C = w.shape
    KP = -(-K // SUBLANES) * SUBLANES
    ct = min(ct, C)
    ch = min(CONV_CHUNK, S)
    glu = off_g is not None
    total = S + CONV_PAD

    def body(*refs):
        if glu:
            (v_ref, g_ref, w_ref, dy_ref, dz_in, dz_ref, dw_ref, db_ref,
             pad_ref, sh_ref, padb_ref, shb_ref, du_ref, stage_v, stage_g, sem) = refs
        else:
            (v_ref, w_ref, dy_ref, dz_in, dz_ref, dw_ref, db_ref,
             pad_ref, sh_ref, padb_ref, shb_ref, du_ref, stage_v, sem) = refs
        del dz_in
        j, bb = pl.program_id(0), pl.program_id(1)
        pad_ref[0:CONV_PAD, :] = jnp.zeros((CONV_PAD, ct), F32)
        if glu:
            pad_ref[CONV_PAD:total, :] = v_ref[...].astype(F32) * _sig(g_ref[...].astype(F32))
        else:
            pad_ref[CONV_PAD:total, :] = v_ref[...].astype(F32)
        padb_ref[0:S, :] = dy_ref[...]
        padb_ref[S:total, :] = jnp.zeros((CONV_PAD, ct), F32)

        @pl.when(bb == 0)
        def _():
            dw_ref[...] = jnp.zeros((KP, ct), F32)
            db_ref[...] = jnp.zeros((1, ct), F32)

        for r, taps in _tap_plan(K):
            u_src, d_src = pad_ref, padb_ref
            if r > 0:
                _shift_rows(sh_ref, pad_ref, r, total, True)
                _shift_rows(shb_ref, padb_ref, r, total, False)
                u_src, d_src = sh_ref, shb_ref
            for l0 in range(0, ct, 128):
                lanes = slice(l0, l0 + 128)
                for c in range(S // ch):
                    acc = None
                    for off, wrow in taps:
                        st = c * ch + off
                        term = w_ref[wrow:wrow + 1, lanes] * d_src[st:st + ch, lanes]
                        acc = term if acc is None else acc + term
                    rows = slice(c * ch, (c + 1) * ch)
                    if r == 0:
                        du_ref[rows, lanes] = acc
                    else:
                        du_ref[rows, lanes] += acc
                for off, wrow in taps:
                    acc = None
                    for c in range(S // ch):
                        st = CONV_PAD + c * ch - off
                        prod = padb_ref[c * ch:(c + 1) * ch, lanes] * u_src[st:st + ch, lanes]
                        acc = prod if acc is None else acc + prod
                    dw_ref[wrow:wrow + 1, lanes] += jnp.sum(acc, axis=0, keepdims=True)
        db_ref[...] += jnp.sum(dy_ref[...], axis=0, keepdims=True)
        for l0 in range(0, ct, 128):
            lanes = slice(l0, l0 + 128)
            for c in range(S // ch):
                rows = slice(c * ch, (c + 1) * ch)
                du = du_ref[rows, lanes]
                if glu:
                    sg = _sig(g_ref[rows, lanes].astype(F32))
                    stage_v[rows, lanes] = (du * sg).astype(BF16)
                    stage_g[rows, lanes] = (du * v_ref[rows, lanes].astype(F32) * sg * (1.0 - sg)).astype(BF16)
                else:
                    stage_v[rows, lanes] = du.astype(BF16)
        _put_columns(stage_v, dz_ref, bb * S, S, off_v + j * ct, sem)
        if glu:
            _put_columns(stage_g, dz_ref, bb * S, S, off_g + j * ct, sem)

    blk = lambda off: pl.BlockSpec((S, ct), lambda j, bb: (bb, off // ct + j))
    in_specs = [blk(off_v)]
    args = [z]
    if glu:
        in_specs.append(blk(off_g))
        args.append(z)
    in_specs += [pl.BlockSpec((K, ct), lambda j, bb: (0, j)), blk(0), _ANY]
    args += [w, dy, dz]
    out_shape = (jax.ShapeDtypeStruct(dz.shape, dz.dtype), jax.ShapeDtypeStruct((KP, C), F32),
                 jax.ShapeDtypeStruct((1, C), F32))
    out_specs = (_ANY, pl.BlockSpec((KP, ct), lambda j, bb: (0, j)), pl.BlockSpec((1, ct), lambda j, bb: (0, j)))
    padded = pltpu.VMEM((total, ct), F32)
    stage = pltpu.VMEM((S, ct), BF16)
    return pl.pallas_call(body, grid=(C // ct, T // S), in_specs=in_specs, out_specs=out_specs, out_shape=out_shape,
                          scratch_shapes=[padded, padded, padded, padded, pltpu.VMEM((S, ct), F32), stage]
                          + ([stage] if glu else []) + [pltpu.SemaphoreType.DMA],
                          input_output_aliases={len(args) - 1: 0},
                          compiler_params=_cp("parallel", "arbitrary"), name=name)(*args)


def _softplus_neg(lam):
    return jnp.maximum(-lam, 0.0) + jnp.log1p(jnp.exp(-jnp.abs(lam)))


def _neg_expm1(x):
    u = jnp.exp(x)
    um1 = u - 1.0
    lg = jnp.log(u)
    safe = jnp.where(lg == 0.0, 1.0, lg)
    em1 = jnp.where(um1 == 0.0, x, jnp.where(um1 == -1.0, -1.0, um1 * x / safe))
    return -em1


def _rglru_fwd(v, z, wa, wx, ba, bx, lam, *, S, off_gate, name, tt=512):
    T, C = v.shape
    G, gw, _ = wa.shape
    tt = min(tt, S)
    nt = S // tt
    nlb = gw // 128
    assert gw % 128 == 0 and off_gate % 128 == 0

    def body(*refs):
        v_ref = refs[0]
        gate_refs = refs[1:1 + nlb]
        wa_ref, wx_ref, ba_ref, bx_ref, lam_ref, r_ref, i_ref, h_ref, hb_ref, a_s, b_s, carry_ref = refs[1 + nlb:]
        t = pl.program_id(2)

        @pl.when(t == 0)
        def _():
            carry_ref[...] = jnp.zeros((SUBLANES, gw), F32)

        vv = v_ref[...]
        vb = vv.astype(BF16)
        r = _sig(jnp.dot(vb, wa_ref[...], preferred_element_type=F32) + ba_ref[...])
        ig = _sig(jnp.dot(vb, wx_ref[...], preferred_element_type=F32) + bx_ref[...])
        log_a = -LRU_C * r * _softplus_neg(lam_ref[...])
        mult = jnp.sqrt(_neg_expm1(2.0 * log_a))
        start = jnp.logical_and(t == 0, lax.broadcasted_iota(jnp.int32, (tt, gw), 0) == 0)
        mult = jnp.where(start, 1.0, mult)
        r_ref[...] = r
        i_ref[...] = ig
        a_s[...] = jnp.exp(log_a)
        b_s[...] = mult * ig * vv
        row = lax.broadcasted_iota(jnp.int32, (SUBLANES, gw), 0)

        def step(i, carry):
            st = pl.multiple_of(i * SUBLANES, SUBLANES)
            A = a_s[pl.ds(st, SUBLANES), :]
            B = b_s[pl.ds(st, SUBLANES), :]
            for d in (1, 2, 4):
                m = row >= d
                Bn = jnp.where(m, A * pltpu.roll(B, d, 0) + B, B)
                A = jnp.where(m, A * pltpu.roll(A, d, 0), A)
                B = Bn
            h = B + A * carry
            h_ref[pl.ds(st, SUBLANES), :] = h
            return jnp.broadcast_to(h[SUBLANES - 1:SUBLANES, :], (SUBLANES, gw))

        carry_ref[...] = lax.fori_loop(0, tt // SUBLANES, step, carry_ref[...], unroll=2)
        for k in range(nlb):
            lanes = slice(k * 128, (k + 1) * 128)
            hb_ref[:, lanes] = (h_ref[:, lanes] * _gelu(gate_refs[k][...].astype(F32))).astype(BF16)

    blk = pl.BlockSpec((tt, gw), lambda g, bb, t: (bb * nt + t, g))
    gates = [pl.BlockSpec((tt, 128), lambda g, bb, t, k=k: (bb * nt + t, off_gate // 128 + g * nlb + k)) for k in range(nlb)]
    wsp = pl.BlockSpec((None, gw, gw), lambda g, bb, t: (g, 0, 0))
    vec = pl.BlockSpec((1, gw), lambda g, bb, t: (0, g))
    o = jax.ShapeDtypeStruct((T, C), F32)
    return pl.pallas_call(body, grid=(G, T // S, nt), in_specs=[blk] + gates + [wsp, wsp, vec, vec, vec],
                          out_specs=(blk, blk, blk, blk), out_shape=(o, o, o, jax.ShapeDtypeStruct((T, C), BF16)),
                          scratch_shapes=[pltpu.VMEM((tt, gw), F32), pltpu.VMEM((tt, gw), F32),
                                          pltpu.VMEM((SUBLANES, gw), F32)],
                          compiler_params=_cp("parallel", "parallel", "arbitrary"),
                          name=name)(v, *([z] * nlb), wa, wx, ba, bx, lam)


def _rglru_bwd(dhb, h, z, r, ig, v, wa, wx, lam, dz, *, S, off_gate, name, tt=512):
    T, C = v.shape
    G, gw, _ = wa.shape
    hd = gw // HEADS_PER_GROUP
    tt = min(tt, S)
    nt = S // tt
    n_seq = T // S
    n_tiles = tt // SUBLANES
    nlb = gw // 128

    def body(*refs):
        dhb_ref, h_ref, hp_ref = refs[0:3]
        gate_refs = refs[3:3 + nlb]
        (r_ref, i_ref, v_ref, wa_ref, wx_ref, lam_ref, dz_in, dv_ref, dz_ref, dlam_ref, dba_ref, dbx_ref, dwa_ref, dwx_ref,
         dh_s, a_s, G_s, da_s, carry_ref, stage, acc_a, acc_x, sem) = refs[3 + nlb:]
        del dz_in
        g, bb, t = pl.program_id(0), pl.program_id(1), pl.program_id(2)
        tb = nt - 1 - t
        first = jnp.logical_and(bb == 0, t == 0)

        @pl.when(t == 0)
        def _():
            carry_ref[...] = jnp.zeros((SUBLANES, gw), F32)

        @pl.when(first)
        def _():
            dlam_ref[...] = jnp.zeros((SUBLANES, gw), F32)
            dba_ref[...] = jnp.zeros((SUBLANES, gw), F32)
            dbx_ref[...] = jnp.zeros((SUBLANES, gw), F32)
            acc_a[...] = jnp.zeros((gw, gw), F32)
            acc_x[...] = jnp.zeros((gw, gw), F32)

        rr, ig, vv = r_ref[...], i_ref[...], v_ref[...]
        lam_v = lam_ref[...]
        sp = _softplus_neg(lam_v)
        log_a = -LRU_C * rr * sp
        a = jnp.exp(log_a)
        a_s[...] = a
        for k in range(nlb):
            lanes = slice(k * 128, (k + 1) * 128)
            gate = gate_refs[k][...].astype(F32)
            dhb = dhb_ref[:, lanes]
            dh_s[:, lanes] = dhb * _gelu(gate)
            stage[:, lanes] = (dhb * h_ref[:, lanes] * _gelu_grad(gate)).astype(BF16)
        put = _columns_copy(stage, dz_ref, (bb * nt + tb) * tt, tt, off_gate + g * gw, sem)
        put.start()
        h_before = jnp.where(tb > 0, jnp.broadcast_to(hp_ref[SUBLANES - 1:SUBLANES, :], (SUBLANES, gw)), 0.0)
        row = lax.broadcasted_iota(jnp.int32, (SUBLANES, gw), 0)

        def step(k, qcarry):
            i = n_tiles - 1 - k
            st = pl.multiple_of(i * SUBLANES, SUBLANES)
            stp = pl.multiple_of(jnp.maximum(i - 1, 0) * SUBLANES, SUBLANES)
            A = a_s[pl.ds(st, SUBLANES), :]
            hv = h_ref[pl.ds(st, SUBLANES), :]
            hprev_tile = h_ref[pl.ds(stp, SUBLANES), :]
            dh = dh_s[pl.ds(st, SUBLANES), :]
            Aq = A
            Bq = A * dh
            for d in (1, 2, 4):
                m = row < SUBLANES - d
                Bn = jnp.where(m, Aq * pltpu.roll(Bq, SUBLANES - d, 0) + Bq, Bq)
                Aq = jnp.where(m, Aq * pltpu.roll(Aq, SUBLANES - d, 0), Aq)
                Bq = Bn
            q = Bq + Aq * qcarry
            qnext = jnp.where(row == SUBLANES - 1, qcarry, pltpu.roll(q, SUBLANES - 1, 0))
            gq = dh + qnext
            hlast = jnp.where(i > 0, jnp.broadcast_to(hprev_tile[SUBLANES - 1:SUBLANES, :], (SUBLANES, gw)), h_before)
            hprev = jnp.where(row == 0, hlast, pltpu.roll(hv, 1, 0))
            G_s[pl.ds(st, SUBLANES), :] = gq
            da_s[pl.ds(st, SUBLANES), :] = gq * hprev
            return jnp.broadcast_to(q[0:1, :], (SUBLANES, gw))

        carry_ref[...] = lax.fori_loop(0, n_tiles, step, carry_ref[...], unroll=2)

        Gv = G_s[...]
        mult_raw = jnp.sqrt(_neg_expm1(2.0 * log_a))
        start = jnp.logical_and(tb == 0, lax.broadcasted_iota(jnp.int32, (tt, gw), 0) == 0)
        mult = jnp.where(start, 1.0, mult_raw)
        dmult = jnp.where(start, 0.0, Gv * ig * vv)
        di = Gv * mult * vv
        dla = da_s[...] * a - dmult * (a * a) / jnp.where(start, 1.0, mult_raw)
        dpr = dla * (-LRU_C) * sp * rr * (1.0 - rr)
        dpi = di * ig * (1.0 - ig)
        dlam_ref[...] += jnp.broadcast_to(jnp.sum(dla * (-LRU_C) * rr, axis=0, keepdims=True) * (-_sig(-lam_v)), (SUBLANES, gw))
        dba_ref[...] += jnp.broadcast_to(jnp.sum(dpr, axis=0, keepdims=True), (SUBLANES, gw))
        dbx_ref[...] += jnp.broadcast_to(jnp.sum(dpi, axis=0, keepdims=True), (SUBLANES, gw))
        dprb, dpib = dpr.astype(BF16), dpi.astype(BF16)
        nt_dims = (((1,), (1,)), ((), ()))
        dv_ref[...] = (Gv * mult * ig
                       + lax.dot_general(dprb, wa_ref[...], nt_dims, preferred_element_type=F32)
                       + lax.dot_general(dpib, wx_ref[...], nt_dims, preferred_element_type=F32))
        tn_dims = (((0,), (0,)), ((), ()))
        vb = vv.astype(BF16)
        acc_a[...] += lax.dot_general(vb, dprb, tn_dims, preferred_element_type=F32)
        acc_x[...] += lax.dot_general(vb, dpib, tn_dims, preferred_element_type=F32)

        @pl.when(jnp.logical_and(bb == n_seq - 1, t == nt - 1))
        def _():
            for hh in range(HEADS_PER_GROUP):
                dwa_ref[hh] = acc_a[hh * hd:(hh + 1) * hd, hh * hd:(hh + 1) * hd]
                dwx_ref[hh] = acc_x[hh * hd:(hh + 1) * hd, hh * hd:(hh + 1) * hd]

        put.wait()

    rowblk = lambda g, bb, t: (bb * nt + nt - 1 - t, g)
    blk = pl.BlockSpec((tt, gw), rowblk)
    before = pl.BlockSpec((SUBLANES, gw), lambda g, bb, t: (jnp.maximum((bb * nt + nt - 1 - t) * n_tiles - 1, 0), g))
    gates = [pl.BlockSpec((tt, 128), lambda g, bb, t, k=k: (bb * nt + nt - 1 - t, off_gate // 128 + g * nlb + k))
             for k in range(nlb)]
    wsp = pl.BlockSpec((None, gw, gw), lambda g, bb, t: (g, 0, 0))
    vec = pl.BlockSpec((1, gw), lambda g, bb, t: (0, g))
    acc8 = pl.BlockSpec((SUBLANES, gw), lambda g, bb, t: (0, g))
    heads = pl.BlockSpec((HEADS_PER_GROUP, hd, hd), lambda g, bb, t: (g, 0, 0))
    o8 = jax.ShapeDtypeStruct((SUBLANES, C), F32)
    ow = jax.ShapeDtypeStruct((G * HEADS_PER_GROUP, hd, hd), F32)
    scr = pltpu.VMEM((tt, gw), F32)
    return pl.pallas_call(
        body, grid=(G, n_seq, nt),
        in_specs=[blk, blk, before] + gates + [blk, blk, blk, wsp, wsp, vec, _ANY],
        out_specs=(blk, _ANY, acc8, acc8, acc8, heads, heads),
        out_shape=(jax.ShapeDtypeStruct((T, C), F32), jax.ShapeDtypeStruct(dz.shape, dz.dtype), o8, o8, o8, ow, ow),
        scratch_shapes=[scr, scr, scr, scr, pltpu.VMEM((SUBLANES, gw), F32), pltpu.VMEM((tt, gw), BF16),
                        pltpu.VMEM((gw, gw), F32), pltpu.VMEM((gw, gw), F32), pltpu.SemaphoreType.DMA],
        input_output_aliases={9 + nlb: 1},
        compiler_params=_cp("parallel", "arbitrary", "arbitrary"),
        name=name)(dhb, h, h, *([z] * nlb), r, ig, v, wa, wx, lam, dz)


def _group_weights(w):
    H, hd, _ = w.shape
    G = H // HEADS_PER_GROUP
    eye = jnp.eye(HEADS_PER_GROUP, dtype=w.dtype)
    wg = jnp.einsum("ghij,hk->ghikj", w.reshape(G, HEADS_PER_GROUP, hd, hd), eye)
    return wg.reshape(G, HEADS_PER_GROUP * hd, HEADS_PER_GROUP * hd).astype(BF16)


def _layer_fwd(x, p, *, S, fetch=None):
    D = x.shape[1]
    Dc = p["conv_a_b"].shape[1]
    Dr = p["conv_b_b"].shape[1]
    offs = dict(va=0, ga=Dc, xb=2 * Dc, gb=2 * Dc + Dr, sa=2 * Dc + 2 * Dr, sb=2 * Dc + 2 * Dr + D)
    h = _rms_fwd(x, p["g_mix"], name="rms_mix_fwd")
    if fetch is not None:
        fetch("in", h)
    z = _mm_nn(h, p["w_in"], tm=1024, tn=p["w_in"].shape[2], bias=p["b_in"], out_dtype=BF16, a_resident=True,
               name="mm_in_fwd")
    if fetch is not None:
        fetch("mix", z)
    u1 = _conv_fwd(z, p["conv_a_w"], p["conv_a_b"], S=S, off_v=offs["va"], off_g=offs["ga"], name="conv_a_fwd")
    u2 = _ln_silu_fwd(u1, p["ln_g"], p["ln_b"], name="ln_silu_fwd")
    ya = _mm_nn(u2, p["w_a_out"], tm=1024, tn=1024, name="mm_a_out_fwd")
    v0 = _conv_fwd(z, p["conv_b_w"], p["conv_b_b"], S=S, off_v=offs["xb"], off_g=None, name="conv_b_fwd")
    r, ig, hs, hb = _rglru_fwd(v0, z, p["wg_a"], p["wg_x"], p["b_rg_a"], p["b_rg_x"], p["lam"], S=S, off_gate=offs["gb"],
                               name="rglru_fwd")
    yb = _mm_nn(hb, p["w_b_out"], tm=1024, tn=1024, name="mm_b_out_fwd")
    m = _merge_fwd(z, ya, yb, off_sa=offs["sa"], off_sb=offs["sb"], name="merge_fwd")
    x_mid = _mm_nn(m, p["w_o"], tm=1024, tn=1024, resid=x, name="mm_o_fwd")
    h2 = _rms_fwd(x_mid, p["g_mlp"], name="rms_mlp_fwd")
    if fetch is not None:
        fetch("mlp", h2)
    f = _mm_nn(h2, p["w_1"], tm=1024, tn=p["w_1"].shape[2], relu2=True, out_dtype=BF16, a_resident=True,
               name="mm_1_fwd")
    x_next = _mm_nn(f, p["w_2"], tm=512, tn=1024, resid=x_mid, name="mm_2_fwd")
    saved = dict(x=x, h=h, z=z, u1=u1, u2=u2, ya=ya, v0=v0, r=r, ig=ig, hs=hs, hb=hb, yb=yb, m=m,
                 x_mid=x_mid, h2=h2, f=f, offs=offs)
    return x_next, saved


def _layer_bwd_mlp(dx, p, sv, *, wdt, dep=None):
    g = {}
    g["w_2"] = _mm_tn(sv["f"], dx, tm=2048, tn=1024, tk=1024, out_dtype=wdt, name="mm_2_wgrad")
    dfp = _mm_nt(dx, p["w_2"], tm=1024, tn=1024, tk=1024, mul_sqrt=sv["f"], out_dtype=BF16, dep=dep,
                 name="mm_2_dgrad")
    g["w_1"] = _mm_tn(sv["h2"], dfp, tm=1024, tn=512, tk=4096, out_blocks=p["w_1"].shape[0], out_dtype=wdt,
                      name="mm_1_wgrad")
    dh2 = _mm_nt(dfp, p["w_1"], tm=512, tn=1024, tk=512, whole_b=True, name="mm_1_dgrad")
    dx_mid, g["g_mlp"] = _rms_bwd(sv["x_mid"], p["g_mlp"], dh2, dx, name="rms_mlp_bwd")
    return dx_mid, g


def _layer_bwd_mix(dx_mid, p, sv, *, S, wdt, dep=None, on_gate_grads=None, on_weight_grads=None):
    offs = sv["offs"]
    g = {}
    g["w_o"] = _mm_tn(sv["m"], dx_mid, tm=1024, tn=1024, tk=1024, out_dtype=wdt, name="mm_o_wgrad")
    dm = _mm_nt(dx_mid, p["w_o"], tm=1024, tn=1024, tk=1024, dep=dep, name="mm_o_dgrad")
    dz = lax.empty(sv["z"].shape, BF16)
    dya, dyb, dz = _merge_bwd(sv["z"], sv["ya"], sv["yb"], dm, dz, off_sa=offs["sa"], off_sb=offs["sb"], name="merge_bwd")
    g["w_b_out"] = _mm_tn(sv["hb"], dyb, tm=1536, tn=1024, tk=1024, out_dtype=wdt, name="mm_b_out_wgrad")
    dhb = _mm_nt(dyb, p["w_b_out"], tm=1024, tn=1536, tk=1024, name="mm_b_out_dgrad")
    dv0, dz, dlam, dba, dbx, g["w_rg_a"], g["w_rg_x"] = _rglru_bwd(
        dhb, sv["hs"], sv["z"], sv["r"], sv["ig"], sv["v0"], p["wg_a"], p["wg_x"], p["lam"], dz, S=S, off_gate=offs["gb"],
        name="rglru_bwd")
    g["lam"], g["b_rg_a"], g["b_rg_x"] = dlam[:1], dba[:1], dbx[:1]
    dep_gates = on_gate_grads(g) if on_gate_grads is not None else None
    dz, g["conv_b_w"], g["conv_b_b"] = _conv_bwd(sv["z"], p["conv_b_w"], dv0, dz, S=S, off_v=offs["xb"], off_g=None,
                                                 name="conv_b_bwd")
    g["w_a_out"] = _mm_tn(sv["u2"], dya, tm=1024, tn=1024, tk=4096, out_dtype=wdt, name="mm_a_out_wgrad")
    du2 = _mm_nt(dya, p["w_a_out"], tm=1024, tn=1024, tk=1024, dep=dep_gates, name="mm_a_out_dgrad")
    du1, g["ln_g"], g["ln_b"] = _ln_silu_bwd(sv["u1"], p["ln_g"], p["ln_b"], du2, name="ln_silu_bwd")
    dz, g["conv_a_w"], g["conv_a_b"] = _conv_bwd(sv["z"], p["conv_a_w"], du1, dz, S=S, off_v=offs["va"],
                                                 off_g=offs["ga"], name="conv_a_bwd")
    g["w_in"], db_in = _mm_tn(sv["h"], dz, tm=1024, tn=512, tk=4096, out_blocks=p["w_in"].shape[0], colsum=True,
                              out_dtype=wdt, name="mm_in_wgrad")
    g["b_in"] = db_in[:1]
    dep_in = on_weight_grads(g) if on_weight_grads is not None else None
    dh = _mm_nt(dz, p["w_in"], tm=256, tn=1024, tk=512, whole_b=True, dep=dep_in, name="mm_in_dgrad")
    dx_in, g["g_mix"] = _rms_bwd(sv["x"], p["g_mix"], dh, dx_mid, name="rms_mix_bwd")
    return dx_in, g


def _layer_bwd(dx, p, sv, *, S, wdt=F32):
    dx_mid, g = _layer_bwd_mlp(dx, p, sv, wdt=wdt)
    dx_in, g2 = _layer_bwd_mix(dx_mid, p, sv, S=S, wdt=wdt)
    g.update(g2)
    return dx_in, g


def _local_step(x, tgt, layers, g_final, *, S, wdt=F32):
    saved = []
    for p in layers:
        x, sv = _layer_fwd(x, p, S=S)
        saved.append(sv)
    loss, dx, dg_final = _final_loss(x, g_final, tgt, name="final_loss")
    grads = [None] * len(layers)
    for l in reversed(range(len(layers))):
        dx, grads[l] = _layer_bwd(dx, layers[l], saved[l], S=S, wdt=wdt)
    return loss, dx, grads, dg_final


_HBM = pl.BlockSpec(memory_space=pltpu.HBM)
_MESH = pl.DeviceIdType.MESH


_SEM = pl.BlockSpec(memory_space=pltpu.SEMAPHORE)
_ANY = pl.BlockSpec(memory_space=pl.ANY)
_FLIPS = [(dx, dy, dc) for dx in (0, 1) for dy in (0, 1) for dc in (0, 1)][1:]


def _place(shard, me_idx, dtype, *, name):
    r, cc = shard.shape
    tr = 512 if r % 512 == 0 else r

    def body(me_ref, s_ref, o_ref):
        del me_ref
        o_ref[...] = s_ref[...].astype(dtype)

    grid_spec = pltpu.PrefetchScalarGridSpec(
        num_scalar_prefetch=1, grid=(r // tr,),
        in_specs=[pl.BlockSpec((tr, cc), lambda i, me: (i, 0))],
        out_specs=pl.BlockSpec((None, tr, cc), lambda i, me: (me[0], i, 0)))
    return pl.pallas_call(body, grid_spec=grid_spec, out_shape=jax.ShapeDtypeStruct((N_DEV, r, cc), dtype),
                          compiler_params=_cp("arbitrary"), name=name)(me_idx, shard)


def _exchange_copies(srcs, lands, send_sems, recv_sems):
    x, y, c = lax.axis_index("x"), lax.axis_index("y"), lax.axis_index("c")
    me = 4 * x + 2 * y + c
    pairs = []
    for k, (dx, dy, dc) in enumerate(_FLIPS):
        peer = (1 - x if dx else x, 1 - y if dy else y, 1 - c if dc else c)
        pidx = 4 * peer[0] + 2 * peer[1] + peer[2]
        for a, land in enumerate(lands):
            src = land.at[me] if srcs is None else srcs[a].at[pidx]
            sem = k * len(lands) + a
            out = pltpu.make_async_remote_copy(src_ref=src, dst_ref=land.at[me], send_sem=send_sems.at[sem],
                                               recv_sem=recv_sems.at[sem], device_id=peer, device_id_type=_MESH)
            arrival = pltpu.make_async_remote_copy(src_ref=src, dst_ref=land.at[pidx], send_sem=send_sems.at[sem],
                                                   recv_sem=recv_sems.at[sem], device_id=peer, device_id_type=_MESH)
            pairs.append((out, arrival))
    return pairs


def _exchange_start(srcs, lands, *, name):
    n = len(lands)
    bufs = list(lands) if srcs is None else list(srcs) + list(lands)
    nb = len(bufs)

    def body(*refs):
        ins = refs[:nb]
        send_sems, recv_sems = refs[nb], refs[nb + 1]
        token = refs[-1]
        for out, _ in _exchange_copies(None if srcs is None else ins[:n], ins[nb - n:], send_sems, recv_sems):
            out.start()
        token[...] = jnp.zeros_like(token)

    sems = pltpu.SemaphoreType.DMA((len(_FLIPS) * n,))
    res = pl.pallas_call(
        body, name=name, in_specs=[_HBM] * nb,
        out_shape=(sems, sems, *[pltpu.HBM(b.shape, b.dtype) for b in bufs], jax.ShapeDtypeStruct((SUBLANES, 128), F32)),
        out_specs=(_SEM, _SEM, *[_HBM] * nb, pl.BlockSpec(memory_space=pltpu.VMEM)),
        input_output_aliases={i: 2 + i for i in range(nb)},
        compiler_params=pltpu.CompilerParams(has_side_effects=pltpu.SideEffectType.DATAFLOW_SIDE_EFFECTING),
    )(*[pltpu.with_memory_space_constraint(b, pltpu.HBM) for b in bufs])
    return res[0], res[1], list(res[2:2 + nb]), res[-1]


def _exchange_wait(send_sems, recv_sems, bufs, *, scatter, after, name):
    nb = len(bufs)
    n = nb // 2 if scatter else nb

    def body(*refs):
        ins = refs[:nb]
        for out, arrival in _exchange_copies(ins[:n] if scatter else None, ins[nb - n:], refs[nb], refs[nb + 1]):
            out.wait_send()
            arrival.wait_recv()

    extra = [] if after is None else [after]
    res = pl.pallas_call(
        body, name=name, in_specs=[_HBM] * nb + [_SEM, _SEM] + [_ANY] * len(extra),
        out_shape=tuple(pltpu.HBM(b.shape, b.dtype) for b in bufs), out_specs=tuple([_HBM] * nb),
        input_output_aliases={i: i for i in range(nb)},
        compiler_params=pltpu.CompilerParams(has_side_effects=pltpu.SideEffectType.DATAFLOW_SIDE_EFFECTING),
    )(*bufs, send_sems, recv_sems, *extra)
    return list(res)


def _adamw_math(w, g, m, v):
    m = ADAM_B1 * m + (1.0 - ADAM_B1) * g
    v = ADAM_B2 * v + (1.0 - ADAM_B2) * (g * g)
    m_hat = m / (1.0 - ADAM_B1 ** ADAM_STEP)
    v_hat = v / (1.0 - ADAM_B2 ** ADAM_STEP)
    delta = -ADAM_LR * (m_hat / (jnp.sqrt(v_hat) + ADAM_EPS) + ADAM_WD * w)
    return delta, m, v


def _adamw(w, m, v, parts, prev, layer, me_idx, *, name, own=None):
    L, r, cc = w.shape
    P = parts.shape[0]
    tr = 512 if r % 512 == 0 else r
    if prev is None:
        prev = tuple(lax.empty(w.shape, F32) for _ in range(4))

    def body(me_ref, w_ref, m_ref, v_ref, p_ref, *rest):
        g_ref, d_ref, nm_ref, nv_ref = rest[-4:]
        if own is None:
            g = p_ref[0].astype(F32)
            for q in range(1, P):
                g = g + p_ref[q].astype(F32)
        else:
            me = me_ref[0]
            g = rest[0][...].astype(F32)
            for q in range(P):
                g = g + jnp.where(q == me, 0.0, p_ref[q].astype(F32))
        d, nm, nv = _adamw_math(w_ref[...], g, m_ref[...], v_ref[...])
        g_ref[...] = g
        d_ref[...] = d
        nm_ref[...] = nm
        nv_ref[...] = nv

    blk = pl.BlockSpec((None, tr, cc), lambda i, me: (layer, i, 0))
    in_specs = [blk, blk, blk, pl.BlockSpec((P, tr, cc), lambda i, me: (0, i, 0))]
    args = [me_idx, w, m, v, parts]
    if own is not None:
        in_specs.append(pl.BlockSpec((None, tr, cc), lambda i, me: (me[0], i, 0)))
        args.append(own)
    first_prev = len(args)
    in_specs += [_ANY] * 4
    args += list(prev)
    grid_spec = pltpu.PrefetchScalarGridSpec(num_scalar_prefetch=1, grid=(r // tr,), in_specs=in_specs,
                                             out_specs=(blk, blk, blk, blk))
    o = jax.ShapeDtypeStruct(w.shape, F32)
    return pl.pallas_call(body, grid_spec=grid_spec, out_shape=(o, o, o, o),
                          input_output_aliases={first_prev + i: i for i in range(4)},
                          compiler_params=_cp("parallel"), name=name)(*args)


_SHARDED = ("w_in", "conv_a_w", "w_a_out", "conv_b_w", "w_b_out", "w_o", "w_1", "w_2")
_COL_SHARDED = ("w_in", "conv_a_w", "conv_b_w", "w_1")
_REPLICATED = ("g_mix", "b_in", "conv_a_b", "ln_g", "ln_b", "conv_b_b", "w_rg_a", "b_rg_a", "w_rg_x", "b_rg_x", "lam",
               "g_mlp")
_WEIGHTS = ("g_mix", "w_in", "b_in", "conv_a_w", "conv_a_b", "ln_g", "ln_b", "w_a_out", "conv_b_w", "conv_b_b", "w_rg_a",
            "b_rg_a", "w_rg_x", "b_rg_x", "lam", "w_b_out", "w_o", "g_mlp", "w_1", "w_2", "g_final")
_LANES = 128


def _cols_from_blocks(b):
    nb, K, n = b.shape
    return b.transpose(1, 0, 2).reshape(K, nb * n)


def _blocks_from_cols(w, K):
    n = w.shape[1] // N_DEV
    return w[:K].reshape(K, N_DEV, n).transpose(1, 0, 2)


def kernel(x, g_mix, w_in, b_in, conv_a_w, conv_a_b, ln_g, ln_b, w_a_out, conv_b_w, conv_b_b, w_rg_a, b_rg_a, w_rg_x, b_rg_x, lam, w_b_out, w_o, g_mlp, w_1, w_2, g_final, loss_target, m_g_mix, m_w_in, m_b_in, m_conv_a_w, m_conv_a_b, m_ln_g, m_ln_b, m_w_a_out, m_conv_b_w, m_conv_b_b, m_w_rg_a, m_b_rg_a, m_w_rg_x, m_b_rg_x, m_lam, m_w_b_out, m_w_o, m_g_mlp, m_w_1, m_w_2, m_g_final, v_g_mix, v_w_in, v_b_in, v_conv_a_w, v_conv_a_b, v_ln_g, v_ln_b, v_w_a_out, v_conv_b_w, v_conv_b_b, v_w_rg_a, v_b_rg_a, v_w_rg_x, v_b_rg_x, v_lam, v_w_b_out, v_w_o, v_g_mlp, v_w_1, v_w_2, v_g_final):
    W = dict(g_mix=g_mix, w_in=w_in, b_in=b_in, conv_a_w=conv_a_w, conv_a_b=conv_a_b, ln_g=ln_g, ln_b=ln_b,
             w_a_out=w_a_out, conv_b_w=conv_b_w, conv_b_b=conv_b_b, w_rg_a=w_rg_a, b_rg_a=b_rg_a, w_rg_x=w_rg_x,
             b_rg_x=b_rg_x, lam=lam, w_b_out=w_b_out, w_o=w_o, g_mlp=g_mlp, w_1=w_1, w_2=w_2, g_final=g_final)
    M = dict(g_mix=m_g_mix, w_in=m_w_in, b_in=m_b_in, conv_a_w=m_conv_a_w, conv_a_b=m_conv_a_b, ln_g=m_ln_g, ln_b=m_ln_b,
             w_a_out=m_w_a_out, conv_b_w=m_conv_b_w, conv_b_b=m_conv_b_b, w_rg_a=m_w_rg_a, b_rg_a=m_b_rg_a,
             w_rg_x=m_w_rg_x, b_rg_x=m_b_rg_x, lam=m_lam, w_b_out=m_w_b_out, w_o=m_w_o, g_mlp=m_g_mlp, w_1=m_w_1,
             w_2=m_w_2, g_final=m_g_final)
    V = dict(g_mix=v_g_mix, w_in=v_w_in, b_in=v_b_in, conv_a_w=v_conv_a_w, conv_a_b=v_conv_a_b, ln_g=v_ln_g, ln_b=v_ln_b,
             w_a_out=v_w_a_out, conv_b_w=v_conv_b_w, conv_b_b=v_conv_b_b, w_rg_a=v_w_rg_a, b_rg_a=v_b_rg_a,
             w_rg_x=v_w_rg_x, b_rg_x=v_b_rg_x, lam=v_lam, w_b_out=v_w_b_out, w_o=v_w_o, g_mlp=v_g_mlp, w_1=v_w_1,
             w_2=v_w_2, g_final=v_g_final)
    NB, S, D = x.shape
    L = g_mix.shape[0]
    hd = w_rg_a.shape[-1]
    me_idx = (4 * lax.axis_index("x") + 2 * lax.axis_index("y") + lax.axis_index("c")).astype(jnp.int32).reshape(1)

    stages = (("in", ("w_in",)), ("mix", ("conv_a_w", "w_a_out", "conv_b_w", "w_b_out", "w_o")), ("mlp", ("w_1", "w_2")))
    gathers = {}
    started = jnp.zeros((), F32)
    for l in range(L):
        for stage, names in stages:
            lands = [_place(W[k][l], me_idx, F32 if k.startswith("conv") else BF16, name="place_" + k) for k in names]
            send_sems, recv_sems, bufs, token = _exchange_start(None, lands, name=f"weights_start_{stage}_{l}")
            gathers[l, stage] = (names, send_sems, recv_sems, bufs)
            started = started + token[0, 0]

    xt = x.reshape(NB * S, D)
    layers, saved = [], []
    for l in range(L):
        p = {k: W[k][l][None] for k in ("g_mix", "b_in", "conv_a_b", "ln_g", "ln_b", "conv_b_b", "b_rg_a", "b_rg_x",
                                         "lam", "g_mlp")}
        if l == 0:
            p["g_mix"] = p["g_mix"] + started
        p["wg_a"] = _group_weights(w_rg_a[l])
        p["wg_x"] = _group_weights(w_rg_x[l])

        def fetch(stage, after, l=l, p=p):
            names, send_sems, recv_sems, bufs = gathers[l, stage]
            bufs = _exchange_wait(send_sems, recv_sems, bufs, scatter=False, after=after,
                                  name=f"weights_wait_{stage}_{l}")
            for k, full in zip(names, bufs):
                if k in ("w_in", "w_1"):
                    p[k] = full
                elif k in _COL_SHARDED:
                    p[k] = _cols_from_blocks(full)
                else:
                    p[k] = full.reshape(-1, full.shape[-1])

        layers.append(p)
        xt, sv = _layer_fwd(xt, p, S=S, fetch=fetch)
        saved.append(sv)
    loss, dx, dg_final = _final_loss(xt, g_final[None], loss_target.reshape(NB * S, D), name="final_loss")

    results = {k: None for k in _SHARDED}

    def scatter_start(l, names, g, tag):
        srcs = []
        for k in names:
            shard_shape = W[k].shape[1:]
            if k in ("w_in", "w_1"):
                srcs.append(g[k])
            elif k in _COL_SHARDED:
                srcs.append(_blocks_from_cols(g[k], shard_shape[0]).astype(BF16))
            else:
                srcs.append(g[k].reshape((N_DEV,) + shard_shape))
        lands = [lax.empty(s.shape, BF16) for s in srcs]
        send_sems, recv_sems, bufs, token = _exchange_start(srcs, lands, name=f"grads_start_{tag}_{l}")
        return (names, send_sems, recv_sems, bufs, tag), token

    def scatter_finish(l, flight, after):
        names, send_sems, recv_sems, bufs, tag = flight
        bufs = _exchange_wait(send_sems, recv_sems, bufs, scatter=True, after=after, name=f"grads_wait_{tag}_{l}")
        n = len(names)
        for k, own, land in zip(names, bufs[:n], bufs[n:]):
            results[k] = _adamw(W[k], M[k], V[k], land, results[k], l, me_idx, own=own, name="adamw_" + k)

    gate_names = ("w_rg_a", "w_rg_x")
    gate_results = {k: None for k in gate_names}

    def small_start(l, g):
        lands = [_place(g[k].reshape(-1, hd), me_idx, F32, name="place_gate_grad") for k in gate_names]
        send_sems, recv_sems, bufs, token = _exchange_start(None, lands, name=f"small_start_{l}")
        return (send_sems, recv_sems, bufs), token

    def small_finish(l, flight, after):
        send_sems, recv_sems, bufs = flight
        bufs = _exchange_wait(send_sems, recv_sems, bufs, scatter=False, after=after, name=f"small_wait_{l}")
        for k, g_all in zip(gate_names, bufs):
            gate_results[k] = _adamw(W[k].reshape(L, -1, hd), M[k].reshape(L, -1, hd), V[k].reshape(L, -1, hd), g_all,
                                     gate_results[k], l, me_idx, name="adamw_gate")

    grads = [None] * L
    in_flight = []
    dep = None
    for l in reversed(range(L)):
        dx_mid, g = _layer_bwd_mlp(dx, layers[l], saved[l], wdt=BF16, dep=dep)
        f_mlp, dep = scatter_start(l, ("w_2", "w_1"), g, "mlp")
        flights = [f_mlp]

        small = []

        def on_gate_grads(g_part, l=l, small=small):
            f_small, token = small_start(l, g_part)
            small.append(f_small)
            return token

        def on_weight_grads(g_part, l=l, flights=flights):
            f_mix, token = scatter_start(l, ("w_o", "w_b_out", "w_a_out", "conv_a_w", "conv_b_w", "w_in"), g_part, "mix")
            flights.append(f_mix)
            return token

        dx, g_mix_part = _layer_bwd_mix(dx_mid, layers[l], saved[l], S=S, wdt=BF16, dep=dep,
                                        on_gate_grads=on_gate_grads, on_weight_grads=on_weight_grads)
        dep = None
        g.update(g_mix_part)
        grads[l] = g
        for l_prev, fs, f_sm in in_flight:
            for flight in fs:
                scatter_finish(l_prev, flight, dx)
            small_finish(l_prev, f_sm, dx)
        in_flight = [(l, flights, small[0])]

    vec_names = tuple(k for k in _REPLICATED if k not in gate_names)
    n_vec = sum(W[k].shape[1] for k in vec_names)

    def vec_pack(rows, final, last):
        tail = jnp.concatenate([final.reshape(1, -1), jnp.broadcast_to(last.reshape(1, 1), (1, _LANES))], axis=1)
        tail = jnp.pad(tail, ((0, SUBLANES - L - 1), (0, n_vec - tail.shape[1])))
        return jnp.concatenate([rows, tail], axis=0)

    g_rows = jnp.concatenate([jnp.concatenate([grads[l][k] for k in vec_names], axis=1) for l in range(L)], axis=0)
    land = _place(vec_pack(g_rows, dg_final, loss[0, :1]), me_idx, F32, name="place_vectors")
    vec_send_sems, vec_recv_sems, vec_bufs, _ = _exchange_start(None, [land], name="vectors_start")
    for l_prev, fs, f_sm in in_flight:
        after = dx if L == 1 else results["w_in"][0]
        for flight in fs:
            scatter_finish(l_prev, flight, after)
            after = results[flight[0][-1]][0]
        small_finish(l_prev, f_sm, after)
    out_g, out_d, out_m, out_v = {}, {}, {}, {}
    for k in _SHARDED:
        out_g[k], out_d[k], out_m[k], out_v[k] = results[k]
    for k in gate_names:
        out_g[k], out_d[k], out_m[k], out_v[k] = (a.reshape(W[k].shape) for a in gate_results[k])

    zero = jnp.zeros((1,), F32)
    (g_all,) = _exchange_wait(vec_send_sems, vec_recv_sems, vec_bufs, scatter=False, after=results["w_in"][0],
                              name="vectors_wait")
    vec_out = _adamw(vec_pack(jnp.concatenate([W[k] for k in vec_names], axis=1), g_final, zero)[None],
                     vec_pack(jnp.concatenate([M[k] for k in vec_names], axis=1), m_g_final, zero)[None],
                     vec_pack(jnp.concatenate([V[k] for k in vec_names], axis=1), v_g_final, zero)[None],
                     g_all, None, 0, me_idx, name="adamw_vectors")
    vec_out = [a[0] for a in vec_out]
    for res, arr in zip((out_g, out_d, out_m, out_v), vec_out):
        off = 0
        for k in vec_names:
            res[k] = arr[:L, off:off + W[k].shape[1]]
            off += W[k].shape[1]
        res["g_final"] = arr[L, :g_final.size]
    loss_out = vec_out[0][L, g_final.size]

    return (loss_out, dx.reshape(NB, S, D), *[out_g[k] for k in _WEIGHTS], *[out_d[k] for k in _WEIGHTS],
            *[out_m[k] for k in _WEIGHTS], *[out_v[k] for k in _WEIGHTS])
```

```python
import functools

import jax
import jax.numpy as jnp
from jax import lax
from jax.experimental import pallas as pl
from jax.experimental.pallas import tpu as pltpu

F32 = jnp.float32
BF16 = jnp.bfloat16

EPS = 1e-6
LRU_C = 8.0
N_RNN_HEADS = 16
HEADS_PER_GROUP = 4
N_DEV = 8
ADAM_LR, ADAM_B1, ADAM_B2, ADAM_EPS, ADAM_WD, ADAM_STEP = 0.001, 0.9, 0.999, 1e-08, 0.01, 10

VMEM_LIMIT_BYTES = 48 * 1024 * 1024
CONV_PAD = 32
CONV_CHUNK = 128
SUBLANES = 8


def _cp(*sem):
    return pltpu.CompilerParams(dimension_semantics=sem, vmem_limit_bytes=VMEM_LIMIT_BYTES)


def _sig(x):
    return 1.0 / (1.0 + jnp.exp(-x))


def _gelu(x):
    c = 0.7978845608028654
    return 0.5 * x * (1.0 + jnp.tanh(c * (x + 0.044715 * x * x * x)))


def _gelu_grad(x):
    c = 0.7978845608028654
    th = jnp.tanh(c * (x + 0.044715 * x * x * x))
    return 0.5 * (1.0 + th) + 0.5 * x * (1.0 - th * th) * c * (1.0 + 3.0 * 0.044715 * x * x)


def _mm_nn(a, b, *, tm, tn, name, bias=None, resid=None, relu2=False, out_dtype=F32, a_resident=False):
    M, K = a.shape
    blocked = b.ndim == 3
    N = b.shape[0] * b.shape[2] if blocked else b.shape[1]
    tm = min(tm, M)
    tn = min(tn, N)
    if blocked:
        assert tn == b.shape[2]
    n_extra = (bias is not None) + (resid is not None)

    def body(*refs):
        acc = jnp.dot(refs[0][...].astype(BF16), refs[1][...].astype(BF16), preferred_element_type=F32)
        k = 2
        if bias is not None:
            acc = acc + refs[k][...]
            k += 1
        if resid is not None:
            acc = acc + refs[k][...]
            k += 1
        if relu2:
            p = jnp.maximum(acc, 0.0)
            acc = p * p
        refs[k][...] = acc.astype(out_dtype)

    def spec(shape, index):
        return pl.BlockSpec(shape, (lambda i, j: index(j, i)) if a_resident else index)

    in_specs = [spec((tm, K), lambda j, i: (i, 0))]
    if blocked:
        in_specs.append(spec((None, K, tn), lambda j, i: (j, 0, 0)))
    else:
        in_specs.append(spec((K, tn), lambda j, i: (0, j)))
    args = [a, b]
    if bias is not None:
        in_specs.append(spec((1, tn), lambda j, i: (0, j)))
        args.append(bias)
    if resid is not None:
        in_specs.append(spec((tm, tn), lambda j, i: (i, j)))
        args.append(resid)
    out_specs = spec((tm, tn), lambda j, i: (i, j))
    out_shape = jax.ShapeDtypeStruct((M, N), out_dtype)
    del n_extra
    grid = (M // tm, N // tn) if a_resident else (N // tn, M // tm)
    return pl.pallas_call(body, grid=grid, in_specs=in_specs, out_specs=out_specs,
                          out_shape=out_shape, compiler_params=_cp("parallel", "parallel"), name=name)(*args)


def _mm_nt(a, b, *, tm, tn, tk, name, mul_sqrt=None, resid=None, out_dtype=F32, dep=None, whole_b=False):
    M, N = a.shape
    blocked = b.ndim == 3
    Kout = b.shape[1] if blocked else b.shape[0]
    tm = min(tm, M)
    tn = min(tn, Kout)
    tk = N if whole_b else (b.shape[2] if blocked else min(tk, N))
    nk = N // tk
    nt_dims = (((1,), (1,)), ((), ()))

    def body(*refs):
        acc_ref = refs[-1]
        kk = pl.program_id(2)
        if blocked and whole_b:
            n = b.shape[2]
            part = None
            for jb in range(b.shape[0]):
                pj = lax.dot_general(refs[0][:, jb * n:(jb + 1) * n].astype(BF16), refs[1][jb].astype(BF16), nt_dims,
                                     preferred_element_type=F32)
                part = pj if part is None else part + pj
        else:
            part = lax.dot_general(refs[0][...].astype(BF16), refs[1][...].astype(BF16), nt_dims,
                                   preferred_element_type=F32)

        def finish(acc):
            k = 2
            if mul_sqrt is not None:
                acc = acc * (2.0 * jnp.sqrt(refs[k][...].astype(F32)))
                k += 1
            if resid is not None:
                acc = acc + refs[k][...]
                k += 1
            if dep is not None:
                k += 1
            refs[k][...] = acc.astype(out_dtype)

        if nk == 1:
            finish(part)
            return

        @pl.when(kk == 0)
        def _():
            acc_ref[...] = part

        @pl.when(kk > 0)
        def _():
            acc_ref[...] += part

        @pl.when(kk == nk - 1)
        def _():
            finish(acc_ref[...])

    in_specs = [pl.BlockSpec((tm, tk), lambda i, j, k: (i, k))]
    if blocked and whole_b:
        in_specs.append(pl.BlockSpec((b.shape[0], tn, b.shape[2]), lambda i, j, k: (0, j, 0)))
    elif blocked:
        in_specs.append(pl.BlockSpec((None, tn, tk), lambda i, j, k: (k, j, 0)))
    else:
        in_specs.append(pl.BlockSpec((tn, tk), lambda i, j, k: (j, k)))
    args = [a, b]
    for extra in (mul_sqrt, resid):
        if extra is not None:
            in_specs.append(pl.BlockSpec((tm, tn), lambda i, j, k: (i, j)))
            args.append(extra)
    if dep is not None:
        in_specs.append(pl.BlockSpec(dep.shape, lambda i, j, k: (0, 0)))
        args.append(dep)
    return pl.pallas_call(body, grid=(M // tm, Kout // tn, nk), in_specs=in_specs,
                          out_specs=pl.BlockSpec((tm, tn), lambda i, j, k: (i, j)),
                          out_shape=jax.ShapeDtypeStruct((M, Kout), out_dtype),
                          scratch_shapes=[pltpu.VMEM((tm, tn), F32)] if nk > 1 else [],
                          compiler_params=_cp("parallel", "parallel", "arbitrary"), name=name)(*args)


def _mm_tn(a, b, *, tm, tn, tk, name, out_blocks=None, colsum=False, out_dtype=F32):
    T, M = a.shape
    N = b.shape[1]
    tm = min(tm, M)
    tk = min(tk, T)
    if out_blocks is not None:
        tn = N // out_blocks
        tm = M
    tn = min(tn, N)
    nk = T // tk
    if colsum:
        assert tm == M

    def body(*refs):
        a_ref, b_ref, o_ref, acc_ref = refs[0], refs[1], refs[2], refs[-1]
        kk = pl.program_id(2)
        bv = b_ref[...]
        part = lax.dot_general(a_ref[...].astype(BF16), bv.astype(BF16),
                               (((0,), (0,)), ((), ())), preferred_element_type=F32)

        if colsum:
            csum = jnp.broadcast_to(jnp.sum(bv.astype(F32), axis=0, keepdims=True), (SUBLANES, tn))

        if nk == 1:
            o_ref[...] = part.astype(out_dtype)
            if colsum:
                refs[3][...] = csum
            return

        @pl.when(kk == 0)
        def _():
            acc_ref[...] = part
            if colsum:
                refs[3][...] = csum

        @pl.when(kk > 0)
        def _():
            acc_ref[...] += part
            if colsum:
                refs[3][...] += csum

        @pl.when(kk == nk - 1)
        def _():
            o_ref[...] = acc_ref[...].astype(out_dtype)

    in_specs = [pl.BlockSpec((tk, tm), lambda i, j, k: (k, i)), pl.BlockSpec((tk, tn), lambda i, j, k: (k, j))]
    if out_blocks is not None:
        o_shape = jax.ShapeDtypeStruct((out_blocks, M, tn), out_dtype)
        o_spec = pl.BlockSpec((None, M, tn), lambda i, j, k: (j, 0, 0))
    else:
        o_shape = jax.ShapeDtypeStruct((M, N), out_dtype)
        o_spec = pl.BlockSpec((tm, tn), lambda i, j, k: (i, j))
    if colsum:
        out_shape = (o_shape, jax.ShapeDtypeStruct((SUBLANES, N), F32))
        out_specs = (o_spec, pl.BlockSpec((SUBLANES, tn), lambda i, j, k: (0, j)))
    else:
        out_shape, out_specs = o_shape, o_spec
    return pl.pallas_call(body, grid=(M // tm, N // tn, nk), in_specs=in_specs, out_specs=out_specs,
                          out_shape=out_shape, scratch_shapes=[pltpu.VMEM((tm, tn), F32)] if nk > 1 else [],
                          compiler_params=_cp("parallel", "parallel", "arbitrary"), name=name)(a, b)


def _rms_fwd(x, g, *, name, tr=512):
    T, D = x.shape
    tr = min(tr, T)

    def body(x_ref, g_ref, h_ref):
        xv = x_ref[...]
        r = lax.rsqrt(jnp.mean(xv * xv, axis=-1, keepdims=True) + EPS)
        h_ref[...] = (xv * r * g_ref[...]).astype(BF16)

    return pl.pallas_call(body, grid=(T // tr,),
                          in_specs=[pl.BlockSpec((tr, D), lambda i: (i, 0)), pl.BlockSpec((1, D), lambda i: (0, 0))],
                          out_specs=pl.BlockSpec((tr, D), lambda i: (i, 0)),
                          out_shape=jax.ShapeDtypeStruct((T, D), BF16), compiler_params=_cp("parallel"), name=name)(x, g)


def _rms_bwd(x, g, dh, dres, *, name, tr=512):
    T, D = x.shape
    tr = min(tr, T)

    def body(x_ref, g_ref, dh_ref, dres_ref, dx_ref, dxb_ref, dg_ref):
        xv = x_ref[...]
        r = lax.rsqrt(jnp.mean(xv * xv, axis=-1, keepdims=True) + EPS)
        n = xv * r
        dh = dh_ref[...]
        dn = dh * g_ref[...]
        dx = dres_ref[...] + r * (dn - n * jnp.mean(dn * n, axis=-1, keepdims=True))
        dx_ref[...] = dx
        dxb_ref[...] = dx.astype(BF16)
        part = jnp.sum(dh * n, axis=0, keepdims=True)

        @pl.when(pl.program_id(0) == 0)
        def _():
            dg_ref[...] = part

        @pl.when(pl.program_id(0) > 0)
        def _():
            dg_ref[...] += part

    row = pl.BlockSpec((tr, D), lambda i: (i, 0))
    vec = pl.BlockSpec((1, D), lambda i: (0, 0))
    return pl.pallas_call(body, grid=(T // tr,), in_specs=[row, vec, row, row], out_specs=(row, row, vec),
                          out_shape=(jax.ShapeDtypeStruct((T, D), F32), jax.ShapeDtypeStruct((T, D), BF16),
                                     jax.ShapeDtypeStruct((1, D), F32)),
                          compiler_params=_cp("arbitrary"), name=name)(x, g, dh, dres)


def _final_loss(x, g, tgt, *, name, tr=512):
    T, D = x.shape
    tr = min(tr, T)

    def body(x_ref, g_ref, t_ref, loss_ref, dx_ref, dxb_ref, dg_ref):
        xv = x_ref[...]
        gv = g_ref[...]
        r = lax.rsqrt(jnp.mean(xv * xv, axis=-1, keepdims=True) + EPS)
        n = xv * r
        e = n * gv - t_ref[...]
        lpart = 0.5 * jnp.sum(jnp.mean(e * e, axis=-1, keepdims=True), axis=0, keepdims=True)
        dy = e * (1.0 / D)
        dn = dy * gv
        dx = r * (dn - n * jnp.mean(dn * n, axis=-1, keepdims=True))
        dx_ref[...] = dx
        dxb_ref[...] = dx.astype(BF16)
        gpart = jnp.sum(dy * n, axis=0, keepdims=True)

        @pl.when(pl.program_id(0) == 0)
        def _():
            dg_ref[...] = gpart
            loss_ref[...] = jnp.broadcast_to(lpart, (1, 128))

        @pl.when(pl.program_id(0) > 0)
        def _():
            dg_ref[...] += gpart
            loss_ref[...] += jnp.broadcast_to(lpart, (1, 128))

    row = pl.BlockSpec((tr, D), lambda i: (i, 0))
    vec = pl.BlockSpec((1, D), lambda i: (0, 0))
    return pl.pallas_call(body, grid=(T // tr,), in_specs=[row, vec, row],
                          out_specs=(pl.BlockSpec((1, 128), lambda i: (0, 0)), row, row, vec),
                          out_shape=(jax.ShapeDtypeStruct((1, 128), F32), jax.ShapeDtypeStruct((T, D), F32),
                                     jax.ShapeDtypeStruct((T, D), BF16), jax.ShapeDtypeStruct((1, D), F32)),
                          compiler_params=_cp("arbitrary"), name=name)(x, g, tgt)


def _ln_silu_fwd(u, g, b, *, name, tr=512):
    T, C = u.shape
    tr = min(tr, T)

    def body(u_ref, g_ref, b_ref, o_ref):
        uv = u_ref[...]
        mu = jnp.mean(uv, axis=-1, keepdims=True)
        xc = uv - mu
        r = lax.rsqrt(jnp.mean(xc * xc, axis=-1, keepdims=True) + EPS)
        y = xc * r * g_ref[...] + b_ref[...]
        o_ref[...] = (y * _sig(y)).astype(BF16)

    row = pl.BlockSpec((tr, C), lambda i: (i, 0))
    vec = pl.BlockSpec((1, C), lambda i: (0, 0))
    return pl.pallas_call(body, grid=(T // tr,), in_specs=[row, vec, vec], out_specs=row,
                          out_shape=jax.ShapeDtypeStruct((T, C), BF16), compiler_params=_cp("parallel"),
                          name=name)(u, g, b)


def _ln_silu_bwd(u, g, b, do, *, name, tr=512):
    T, C = u.shape
    tr = min(tr, T)

    def body(u_ref, g_ref, b_ref, do_ref, du_ref, dg_ref, db_ref):
        uv = u_ref[...]
        gv = g_ref[...]
        mu = jnp.mean(uv, axis=-1, keepdims=True)
        xc = uv - mu
        r = lax.rsqrt(jnp.mean(xc * xc, axis=-1, keepdims=True) + EPS)
        n = xc * r
        y = n * gv + b_ref[...]
        s = _sig(y)
        dy = do_ref[...] * (s * (1.0 + y * (1.0 - s)))
        dn = dy * gv
        du_ref[...] = r * (dn - jnp.mean(dn, axis=-1, keepdims=True) - n * jnp.mean(dn * n, axis=-1, keepdims=True))
        gpart = jnp.sum(dy * n, axis=0, keepdims=True)
        bpart = jnp.sum(dy, axis=0, keepdims=True)

        @pl.when(pl.program_id(0) == 0)
        def _():
            dg_ref[...] = gpart
            db_ref[...] = bpart

        @pl.when(pl.program_id(0) > 0)
        def _():
            dg_ref[...] += gpart
            db_ref[...] += bpart

    row = pl.BlockSpec((tr, C), lambda i: (i, 0))
    vec = pl.BlockSpec((1, C), lambda i: (0, 0))
    return pl.pallas_call(body, grid=(T // tr,), in_specs=[row, vec, vec, row], out_specs=(row, vec, vec),
                          out_shape=(jax.ShapeDtypeStruct((T, C), F32), jax.ShapeDtypeStruct((1, C), F32),
                                     jax.ShapeDtypeStruct((1, C), F32)),
                          compiler_params=_cp("arbitrary"), name=name)(u, g, b, do)


def _merge_fwd(z, ya, yb, *, off_sa, off_sb, name, tr=512):
    T, D = ya.shape
    tr = min(tr, T)
    assert off_sa % D == 0 and off_sb % D == 0

    def body(sa_ref, sb_ref, ya_ref, yb_ref, m_ref):
        m_ref[...] = (_sig(sa_ref[...].astype(F32)) * ya_ref[...]
                      + _sig(sb_ref[...].astype(F32)) * yb_ref[...]).astype(BF16)

    row = pl.BlockSpec((tr, D), lambda i: (i, 0))
    return pl.pallas_call(body, grid=(T // tr,),
                          in_specs=[pl.BlockSpec((tr, D), lambda i: (i, off_sa // D)),
                                    pl.BlockSpec((tr, D), lambda i: (i, off_sb // D)), row, row],
                          out_specs=row, out_shape=jax.ShapeDtypeStruct((T, D), BF16),
                          compiler_params=_cp("parallel"), name=name)(z, z, ya, yb)


def _columns_copy(stage_ref, dz_ref, row0, rows, col0, sem):
    dst = dz_ref.at[pl.ds(pl.multiple_of(row0, SUBLANES), rows),
                    pl.ds(pl.multiple_of(col0, 128), stage_ref.shape[1])]
    return pltpu.make_async_copy(stage_ref, dst, sem)


def _put_columns(stage_ref, dz_ref, row0, rows, col0, sem):
    cp = _columns_copy(stage_ref, dz_ref, row0, rows, col0, sem)
    cp.start()
    cp.wait()


def _merge_bwd(z, ya, yb, dm, dz, *, off_sa, off_sb, name, tr=512):
    T, D = ya.shape
    tr = min(tr, T)
    assert off_sb == off_sa + D

    def body(sa_ref, sb_ref, ya_ref, yb_ref, dm_ref, dz_in, dya_ref, dyb_ref, dz_ref, stage, sem):
        del dz_in
        dm = dm_ref[...]
        ga = _sig(sa_ref[...].astype(F32))
        gb = _sig(sb_ref[...].astype(F32))
        dya_ref[...] = (dm * ga).astype(BF16)
        dyb_ref[...] = (dm * gb).astype(BF16)
        stage[:, 0:D] = (dm * ya_ref[...] * ga * (1.0 - ga)).astype(BF16)
        stage[:, D:2 * D] = (dm * yb_ref[...] * gb * (1.0 - gb)).astype(BF16)
        _put_columns(stage, dz_ref, pl.program_id(0) * tr, tr, off_sa, sem)

    row = pl.BlockSpec((tr, D), lambda i: (i, 0))
    o = jax.ShapeDtypeStruct((T, D), BF16)
    return pl.pallas_call(body, grid=(T // tr,),
                          in_specs=[pl.BlockSpec((tr, D), lambda i: (i, off_sa // D)),
                                    pl.BlockSpec((tr, D), lambda i: (i, off_sb // D)), row, row, row, _ANY],
                          out_specs=(row, row, _ANY), out_shape=(o, o, jax.ShapeDtypeStruct(dz.shape, dz.dtype)),
                          scratch_shapes=[pltpu.VMEM((tr, 2 * D), BF16), pltpu.SemaphoreType.DMA],
                          input_output_aliases={5: 2},
                          compiler_params=_cp("parallel"), name=name)(z, z, ya, yb, dm, dz)


def _shift_rows(dst_ref, src_ref, r, total, back):
    for c0 in range(0, total - SUBLANES, CONV_CHUNK):
        n = min(CONV_CHUNK, total - SUBLANES - c0)
        if back:
            dst_ref[SUBLANES + c0:SUBLANES + c0 + n, :] = src_ref[SUBLANES + c0 - r:SUBLANES + c0 - r + n, :]
        else:
            dst_ref[c0:c0 + n, :] = src_ref[c0 + r:c0 + r + n, :]


def _tap_plan(K):
    if K <= SUBLANES:
        return [(0, [(s, K - 1 - s) for s in range(K)])]
    return [(r, [(SUBLANES * q, K - 1 - (SUBLANES * q + r)) for q in range(-(-K // SUBLANES)) if SUBLANES * q + r < K])
            for r in range(SUBLANES)]


def _conv_fwd(z, w, b, *, S, off_v, off_g, name, ct=256):
    T = z.shape[0]
    K, C = w.shape
    ct = min(ct, C)
    ch = min(CONV_CHUNK, S)
    glu = off_g is not None
    assert off_v % ct == 0 and (not glu or off_g % ct == 0)
    assert SUBLANES * ((K - 1) // SUBLANES) <= CONV_PAD - SUBLANES

    def body(*refs):
        if glu:
            v_ref, g_ref, w_ref, b_ref, o_ref, pad_ref, sh_ref = refs
        else:
            v_ref, w_ref, b_ref, o_ref, pad_ref, sh_ref = refs
        pad_ref[0:CONV_PAD, :] = jnp.zeros((CONV_PAD, ct), F32)
        if glu:
            pad_ref[CONV_PAD:CONV_PAD + S, :] = v_ref[...].astype(F32) * _sig(g_ref[...].astype(F32))
        else:
            pad_ref[CONV_PAD:CONV_PAD + S, :] = v_ref[...].astype(F32)
        for r, taps in _tap_plan(K):
            src = pad_ref
            if r > 0:
                _shift_rows(sh_ref, pad_ref, r, CONV_PAD + S, True)
                src = sh_ref
            for l0 in range(0, ct, 128):
                lanes = slice(l0, l0 + 128)
                for c in range(S // ch):
                    acc = None
                    for off, wrow in taps:
                        st = CONV_PAD + c * ch - off
                        term = w_ref[wrow:wrow + 1, lanes] * src[st:st + ch, lanes]
                        acc = term if acc is None else acc + term
                    rows = slice(c * ch, (c + 1) * ch)
                    if r == 0:
                        o_ref[rows, lanes] = acc + b_ref[:, lanes]
                    else:
                        o_ref[rows, lanes] += acc

    in_specs = [pl.BlockSpec((S, ct), lambda j, bb: (bb, off_v // ct + j))]
    args = [z]
    if glu:
        in_specs.append(pl.BlockSpec((S, ct), lambda j, bb: (bb, off_g // ct + j)))
        args.append(z)
    in_specs += [pl.BlockSpec((K, ct), lambda j, bb: (0, j)), pl.BlockSpec((1, ct), lambda j, bb: (0, j))]
    args += [w, b]
    return pl.pallas_call(body, grid=(C // ct, T // S), in_specs=in_specs,
                          out_specs=pl.BlockSpec((S, ct), lambda j, bb: (bb, j)),
                          out_shape=jax.ShapeDtypeStruct((T, C), F32),
                          scratch_shapes=[pltpu.VMEM((CONV_PAD + S, ct), F32), pltpu.VMEM((CONV_PAD + S, ct), F32)],
                          compiler_params=_cp("parallel", "parallel"), name=name)(*args)


def _conv_bwd(z, w, dy, dz, *, S, off_v, off_g, name, ct=256):
    T = z.shape[0]
    K, C = w.shape
    KP = -(-K // SUBLANES) * SUBLANES
    ct = min(ct, C)
    ch = min(CONV_CHUNK, S)
    glu = off_g is not None
    total = S + CONV_PAD

    def body(*refs):
        if glu:
            (v_ref, g_ref, w_ref, dy_ref, dz_in, dz_ref, dw_ref, db_ref,
             pad_ref, sh_ref, padb_ref, shb_ref, du_ref, stage_v, stage_g, sem) = refs
        else:
            (v_ref, w_ref, dy_ref, dz_in, dz_ref, dw_ref, db_ref,
             pad_ref, sh_ref, padb_ref, shb_ref, du_ref, stage_v, sem) = refs
        del dz_in
        j, bb = pl.program_id(0), pl.program_id(1)
        pad_ref[0:CONV_PAD, :] = jnp.zeros((CONV_PAD, ct), F32)
        if glu:
            pad_ref[CONV_PAD:total, :] = v_ref[...].astype(F32) * _sig(g_ref[...].astype(F32))
        else:
            pad_ref[CONV_PAD:total, :] = v_ref[...].astype(F32)
        padb_ref[0:S, :] = dy_ref[...]
        padb_ref[S:total, :] = jnp.zeros((CONV_PAD, ct), F32)

        @pl.when(bb == 0)
        def _():
            dw_ref[...] = jnp.zeros((KP, ct), F32)
            db_ref[...] = jnp.zeros((1, ct), F32)

        for r, taps in _tap_plan(K):
            u_src, d_src = pad_ref, padb_ref
            if r > 0:
                _shift_rows(sh_ref, pad_ref, r, total, True)
                _shift_rows(shb_ref, padb_ref, r, total, False)
                u_src, d_src = sh_ref, shb_ref
            for l0 in range(0, ct, 128):
                lanes = slice(l0, l0 + 128)
                for c in range(S // ch):
                    acc = None
                    for off, wrow in taps:
                        st = c * ch + off
                        term = w_ref[wrow:wrow + 1, lanes] * d_src[st:st + ch, lanes]
                        acc = term if acc is None else acc + term
                    rows = slice(c * ch, (c + 1) * ch)
                    if r == 0:
                        du_ref[rows, lanes] = acc
                    else:
                        du_ref[rows, lanes] += acc
                for off, wrow in taps:
                    acc = None
                    for c in range(S // ch):
                        st = CONV_PAD + c * ch - off
                        prod = padb_ref[c * ch:(c + 1) * ch, lanes] * u_src[st:st + ch, lanes]
                        acc = prod if acc is None else acc + prod
                    dw_ref[wrow:wrow + 1, lanes] += jnp.sum(acc, axis=0, keepdims=True)
        db_ref[...] += jnp.sum(dy_ref[...], axis=0, keepdims=True)
        for l0 in range(0, ct, 128):
            lanes = slice(l0, l0 + 128)
            for c in range(S // ch):
                rows = slice(c * ch, (c + 1) * ch)
                du = du_ref[rows, lanes]
                if glu:
                    sg = _sig(g_ref[rows, lanes].astype(F32))
                    stage_v[rows, lanes] = (du * sg).astype(BF16)
                    stage_g[rows, lanes] = (du * v_ref[rows, lanes].astype(F32) * sg * (1.0 - sg)).astype(BF16)
                else:
                    stage_v[rows, lanes] = du.astype(BF16)
        _put_columns(stage_v, dz_ref, bb * S, S, off_v + j * ct, sem)
        if glu:
            _put_columns(stage_g, dz_ref, bb * S, S, off_g + j * ct, sem)

    blk = lambda off: pl.BlockSpec((S, ct), lambda j, bb: (bb, off // ct + j))
    in_specs = [blk(off_v)]
    args = [z]
    if glu:
        in_specs.append(blk(off_g))
        args.append(z)
    in_specs += [pl.BlockSpec((K, ct), lambda j, bb: (0, j)), blk(0), _ANY]
    args += [w, dy, dz]
    out_shape = (jax.ShapeDtypeStruct(dz.shape, dz.dtype), jax.ShapeDtypeStruct((KP, C), F32),
                 jax.ShapeDtypeStruct((1, C), F32))
    out_specs = (_ANY, pl.BlockSpec((KP, ct), lambda j, bb: (0, j)), pl.BlockSpec((1, ct), lambda j, bb: (0, j)))
    padded = pltpu.VMEM((total, ct), F32)
    stage = pltpu.VMEM((S, ct), BF16)
    return pl.pallas_call(body, grid=(C // ct, T // S), in_specs=in_specs, out_specs=out_specs, out_shape=out_shape,
                          scratch_shapes=[padded, padded, padded, padded, pltpu.VMEM((S, ct), F32), stage]
                          + ([stage] if glu else []) + [pltpu.SemaphoreType.DMA],
                          input_output_aliases={len(args) - 1: 0},
                          compiler_params=_cp("parallel", "arbitrary"), name=name)(*args)


def _softplus_neg(lam):
    return jnp.maximum(-lam, 0.0) + jnp.log1p(jnp.exp(-jnp.abs(lam)))


def _neg_expm1(x):
    u = jnp.exp(x)
    um1 = u - 1.0
    lg = jnp.log(u)
    safe = jnp.where(lg == 0.0, 1.0, lg)
    em1 = jnp.where(um1 == 0.0, x, jnp.where(um1 == -1.0, -1.0, um1 * x / safe))
    return -em1


def _rglru_fwd(v, z, wa, wx, ba, bx, lam, *, S, off_gate, name, tt=512):
    T, C = v.shape
    G, gw, _ = wa.shape
    tt = min(tt, S)
    nt = S // tt
    nlb = gw // 128
    assert gw % 128 == 0 and off_gate % 128 == 0

    def body(*refs):
        v_ref = refs[0]
        gate_refs = refs[1:1 + nlb]
        wa_ref, wx_ref, ba_ref, bx_ref, lam_ref, r_ref, i_ref, h_ref, hb_ref, a_s, b_s, carry_ref = refs[1 + nlb:]
        t = pl.program_id(2)

        @pl.when(t == 0)
        def _():
            carry_ref[...] = jnp.zeros((SUBLANES, gw), F32)

        vv = v_ref[...]
        vb = vv.astype(BF16)
        r = _sig(jnp.dot(vb, wa_ref[...], preferred_element_type=F32) + ba_ref[...])
        ig = _sig(jnp.dot(vb, wx_ref[...], preferred_element_type=F32) + bx_ref[...])
        log_a = -LRU_C * r * _softplus_neg(lam_ref[...])
        mult = jnp.sqrt(_neg_expm1(2.0 * log_a))
        start = jnp.logical_and(t == 0, lax.broadcasted_iota(jnp.int32, (tt, gw), 0) == 0)
        mult = jnp.where(start, 1.0, mult)
        r_ref[...] = r
        i_ref[...] = ig
        a_s[...] = jnp.exp(log_a)
        b_s[...] = mult * ig * vv
        row = lax.broadcasted_iota(jnp.int32, (SUBLANES, gw), 0)

        def step(i, carry):
            st = pl.multiple_of(i * SUBLANES, SUBLANES)
            A = a_s[pl.ds(st, SUBLANES), :]
            B = b_s[pl.ds(st, SUBLANES), :]
            for d in (1, 2, 4):
                m = row >= d
                Bn = jnp.where(m, A * pltpu.roll(B, d, 0) + B, B)
                A = jnp.where(m, A * pltpu.roll(A, d, 0), A)
                B = Bn
            h = B + A * carry
            h_ref[pl.ds(st, SUBLANES), :] = h
            return jnp.broadcast_to(h[SUBLANES - 1:SUBLANES, :], (SUBLANES, gw))

        carry_ref[...] = lax.fori_loop(0, tt // SUBLANES, step, carry_ref[...], unroll=2)
        for k in range(nlb):
            lanes = slice(k * 128, (k + 1) * 128)
            hb_ref[:, lanes] = (h_ref[:, lanes] * _gelu(gate_refs[k][...].astype(F32))).astype(BF16)

    blk = pl.BlockSpec((tt, gw), lambda g, bb, t: (bb * nt + t, g))
    gates = [pl.BlockSpec((tt, 128), lambda g, bb, t, k=k: (bb * nt + t, off_gate // 128 + g * nlb + k)) for k in range(nlb)]
    wsp = pl.BlockSpec((None, gw, gw), lambda g, bb, t: (g, 0, 0))
    vec = pl.BlockSpec((1, gw), lambda g, bb, t: (0, g))
    o = jax.ShapeDtypeStruct((T, C), F32)
    return pl.pallas_call(body, grid=(G, T // S, nt), in_specs=[blk] + gates + [wsp, wsp, vec, vec, vec],
                          out_specs=(blk, blk, blk, blk), out_shape=(o, o, o, jax.ShapeDtypeStruct((T, C), BF16)),
                          scratch_shapes=[pltpu.VMEM((tt, gw), F32), pltpu.VMEM((tt, gw), F32),
                                          pltpu.VMEM((SUBLANES, gw), F32)],
                          compiler_params=_cp("parallel", "parallel", "arbitrary"),
                          name=name)(v, *([z] * nlb), wa, wx, ba, bx, lam)


def _rglru_bwd(dhb, h, z, r, ig, v, wa, wx, lam, dz, *, S, off_gate, name, tt=512):
    T, C = v.shape
    G, gw, _ = wa.shape
    hd = gw // HEADS_PER_GROUP
    tt = min(tt, S)
    nt = S // tt
    n_seq = T // S
    n_tiles = tt // SUBLANES
    nlb = gw // 128

    def body(*refs):
        dhb_ref, h_ref, hp_ref = refs[0:3]
        gate_refs = refs[3:3 + nlb]
        (r_ref, i_ref, v_ref, wa_ref, wx_ref, lam_ref, dz_in, dv_ref, dz_ref, dlam_ref, dba_ref, dbx_ref, dwa_ref, dwx_ref,
         dh_s, a_s, G_s, da_s, carry_ref, stage, acc_a, acc_x, sem) = refs[3 + nlb:]
        del dz_in
        g, bb, t = pl.program_id(0), pl.program_id(1), pl.program_id(2)
        tb = nt - 1 - t
        first = jnp.logical_and(bb == 0, t == 0)

        @pl.when(t == 0)
        def _():
            carry_ref[...] = jnp.zeros((SUBLANES, gw), F32)

        @pl.when(first)
        def _():
            dlam_ref[...] = jnp.zeros((SUBLANES, gw), F32)
            dba_ref[...] = jnp.zeros((SUBLANES, gw), F32)
            dbx_ref[...] = jnp.zeros((SUBLANES, gw), F32)
            acc_a[...] = jnp.zeros((gw, gw), F32)
            acc_x[...] = jnp.zeros((gw, gw), F32)

        rr, ig, vv = r_ref[...], i_ref[...], v_ref[...]
        lam_v = lam_ref[...]
        sp = _softplus_neg(lam_v)
        log_a = -LRU_C * rr * sp
        a = jnp.exp(log_a)
        a_s[...] = a
        for k in range(nlb):
            lanes = slice(k * 128, (k + 1) * 128)
            gate = gate_refs[k][...].astype(F32)
            dhb = dhb_ref[:, lanes]
            dh_s[:, lanes] = dhb * _gelu(gate)
            stage[:, lanes] = (dhb * h_ref[:, lanes] * _gelu_grad(gate)).astype(BF16)
        put = _columns_copy(stage, dz_ref, (bb * nt + tb) * tt, tt, off_gate + g * gw, sem)
        put.start()
        h_before = jnp.where(tb > 0, jnp.broadcast_to(hp_ref[SUBLANES - 1:SUBLANES, :], (SUBLANES, gw)), 0.0)
        row = lax.broadcasted_iota(jnp.int32, (SUBLANES, gw), 0)

        def step(k, qcarry):
            i = n_tiles - 1 - k
            st = pl.multiple_of(i * SUBLANES, SUBLANES)
            stp = pl.multiple_of(jnp.maximum(i - 1, 0) * SUBLANES, SUBLANES)
            A = a_s[pl.ds(st, SUBLANES), :]
            hv = h_ref[pl.ds(st, SUBLANES), :]
            hprev_tile = h_ref[pl.ds(stp, SUBLANES), :]
            dh = dh_s[pl.ds(st, SUBLANES), :]
            Aq = A
            Bq = A * dh
            for d in (1, 2, 4):
                m = row < SUBLANES - d
                Bn = jnp.where(m, Aq * pltpu.roll(Bq, SUBLANES - d, 0) + Bq, Bq)
                Aq = jnp.where(m, Aq * pltpu.roll(Aq, SUBLANES - d, 0), Aq)
                Bq = Bn
            q = Bq + Aq * qcarry
            qnext = jnp.where(row == SUBLANES - 1, qcarry, pltpu.roll(q, SUBLANES - 1, 0))
            gq = dh + qnext
            hlast = jnp.where(i > 0, jnp.broadcast_to(hprev_tile[SUBLANES - 1:SUBLANES, :], (SUBLANES, gw)), h_before)
            hprev = jnp.where(row == 0, hlast, pltpu.roll(hv, 1, 0))
            G_s[pl.ds(st, SUBLANES), :] = gq
            da_s[pl.ds(st, SUBLANES), :] = gq * hprev
            return jnp.broadcast_to(q[0:1, :], (SUBLANES, gw))

        carry_ref[...] = lax.fori_loop(0, n_tiles, step, carry_ref[...], unroll=2)

        Gv = G_s[...]
        mult_raw = jnp.sqrt(_neg_expm1(2.0 * log_a))
        start = jnp.logical_and(tb == 0, lax.broadcasted_iota(jnp.int32, (tt, gw), 0) == 0)
        mult = jnp.where(start, 1.0, mult_raw)
        dmult = jnp.where(start, 0.0, Gv * ig * vv)
        di = Gv * mult * vv
        dla = da_s[...] * a - dmult * (a * a) / jnp.where(start, 1.0, mult_raw)
        dpr = dla * (-LRU_C) * sp * rr * (1.0 - rr)
        dpi = di * ig * (1.0 - ig)
        dlam_ref[...] += jnp.broadcast_to(jnp.sum(dla * (-LRU_C) * rr, axis=0, keepdims=True) * (-_sig(-lam_v)), (SUBLANES, gw))
        dba_ref[...] += jnp.broadcast_to(jnp.sum(dpr, axis=0, keepdims=True), (SUBLANES, gw))
        dbx_ref[...] += jnp.broadcast_to(jnp.sum(dpi, axis=0, keepdims=True), (SUBLANES, gw))
        dprb, dpib = dpr.astype(BF16), dpi.astype(BF16)
        nt_dims = (((1,), (1,)), ((), ()))
        dv_ref[...] = (Gv * mult * ig
                       + lax.dot_general(dprb, wa_ref[...], nt_dims, preferred_element_type=F32)
                       + lax.dot_general(dpib, wx_ref[...], nt_dims, preferred_element_type=F32))
        tn_dims = (((0,), (0,)), ((), ()))
        vb = vv.astype(BF16)
        acc_a[...] += lax.dot_general(vb, dprb, tn_dims, preferred_element_type=F32)
        acc_x[...] += lax.dot_general(vb, dpib, tn_dims, preferred_element_type=F32)

        @pl.when(jnp.logical_and(bb == n_seq - 1, t == nt - 1))
        def _():
            for hh in range(HEADS_PER_GROUP):
                dwa_ref[hh] = acc_a[hh * hd:(hh + 1) * hd, hh * hd:(hh + 1) * hd]
                dwx_ref[hh] = acc_x[hh * hd:(hh + 1) * hd, hh * hd:(hh + 1) * hd]

        put.wait()

    rowblk = lambda g, bb, t: (bb * nt + nt - 1 - t, g)
    blk = pl.BlockSpec((tt, gw), rowblk)
    before = pl.BlockSpec((SUBLANES, gw), lambda g, bb, t: (jnp.maximum((bb * nt + nt - 1 - t) * n_tiles - 1, 0), g))
    gates = [pl.BlockSpec((tt, 128), lambda g, bb, t, k=k: (bb * nt + nt - 1 - t, off_gate // 128 + g * nlb + k))
             for k in range(nlb)]
    wsp = pl.BlockSpec((None, gw, gw), lambda g, bb, t: (g, 0, 0))
    vec = pl.BlockSpec((1, gw), lambda g, bb, t: (0, g))
    acc8 = pl.BlockSpec((SUBLANES, gw), lambda g, bb, t: (0, g))
    heads = pl.BlockSpec((HEADS_PER_GROUP, hd, hd), lambda g, bb, t: (g, 0, 0))
    o8 = jax.ShapeDtypeStruct((SUBLANES, C), F32)
    ow = jax.ShapeDtypeStruct((G * HEADS_PER_GROUP, hd, hd), F32)
    scr = pltpu.VMEM((tt, gw), F32)
    return pl.pallas_call(
        body, grid=(G, n_seq, nt),
        in_specs=[blk, blk, before] + gates + [blk, blk, blk, wsp, wsp, vec, _ANY],
        out_specs=(blk, _ANY, acc8, acc8, acc8, heads, heads),
        out_shape=(jax.ShapeDtypeStruct((T, C), F32), jax.ShapeDtypeStruct(dz.shape, dz.dtype), o8, o8, o8, ow, ow),
        scratch_shapes=[scr, scr, scr, scr, pltpu.VMEM((SUBLANES, gw), F32), pltpu.VMEM((tt, gw), BF16),
                        pltpu.VMEM((gw, gw), F32), pltpu.VMEM((gw, gw), F32), pltpu.SemaphoreType.DMA],
        input_output_aliases={9 + nlb: 1},
        compiler_params=_cp("parallel", "arbitrary", "arbitrary"),
        name=name)(dhb, h, h, *([z] * nlb), r, ig, v, wa, wx, lam, dz)


def _group_weights(w):
    H, hd, _ = w.shape
    G = H // HEADS_PER_GROUP
    eye = jnp.eye(HEADS_PER_GROUP, dtype=w.dtype)
    wg = jnp.einsum("ghij,hk->ghikj", w.reshape(G, HEADS_PER_GROUP, hd, hd), eye)
    return wg.reshape(G, HEADS_PER_GROUP * hd, HEADS_PER_GROUP * hd).astype(BF16)


def _layer_fwd(x, p, *, S, fetch=None):
    D = x.shape[1]
    Dc = p["conv_a_b"].shape[1]
    Dr = p["conv_b_b"].shape[1]
    offs = dict(va=0, ga=Dc, xb=2 * Dc, gb=2 * Dc + Dr, sa=2 * Dc + 2 * Dr, sb=2 * Dc + 2 * Dr + D)
    h = _rms_fwd(x, p["g_mix"], name="rms_mix_fwd")
    if fetch is not None:
        fetch("in", h)
    z = _mm_nn(h, p["w_in"], tm=2048, tn=p["w_in"].shape[2], bias=p["b_in"], out_dtype=BF16, a_resident=True,
               name="mm_in_fwd")
    if fetch is not None:
        fetch("mix", z)
    u1 = _conv_fwd(z, p["conv_a_w"], p["conv_a_b"], S=S, off_v=offs["va"], off_g=offs["ga"], name="conv_a_fwd")
    u2 = _ln_silu_fwd(u1, p["ln_g"], p["ln_b"], name="ln_silu_fwd")
    ya = _mm_nn(u2, p["w_a_out"], tm=1024, tn=1024, name="mm_a_out_fwd")
    v0 = _conv_fwd(z, p["conv_b_w"], p["conv_b_b"], S=S, off_v=offs["xb"], off_g=None, name="conv_b_fwd")
    r, ig, hs, hb = _rglru_fwd(v0, z, p["wg_a"], p["wg_x"], p["b_rg_a"], p["b_rg_x"], p["lam"], S=S, off_gate=offs["gb"],
                               name="rglru_fwd")
    yb = _mm_nn(hb, p["w_b_out"], tm=1024, tn=1024, name="mm_b_out_fwd")
    m = _merge_fwd(z, ya, yb, off_sa=offs["sa"], off_sb=offs["sb"], name="merge_fwd")
    x_mid = _mm_nn(m, p["w_o"], tm=1024, tn=1024, resid=x, name="mm_o_fwd")
    h2 = _rms_fwd(x_mid, p["g_mlp"], name="rms_mlp_fwd")
    if fetch is not None:
        fetch("mlp", h2)
    f = _mm_nn(h2, p["w_1"], tm=2048, tn=p["w_1"].shape[2], relu2=True, out_dtype=BF16, a_resident=True,
               name="mm_1_fwd")
    x_next = _mm_nn(f, p["w_2"], tm=512, tn=1024, resid=x_mid, name="mm_2_fwd")
    saved = dict(x=x, h=h, z=z, u1=u1, u2=u2, ya=ya, v0=v0, r=r, ig=ig, hs=hs, hb=hb, yb=yb, m=m,
                 x_mid=x_mid, h2=h2, f=f, offs=offs)
    return x_next, saved


def _layer_bwd_mlp(dx, p, sv, *, wdt, dep=None):
    dx, dxb = dx
    g = {}
    g["w_2"] = _mm_tn(sv["f"], dxb, tm=1024, tn=1024, tk=4096, out_dtype=wdt, name="mm_2_wgrad")
    dfp = _mm_nt(dxb, p["w_2"], tm=1024, tn=1024, tk=1024, mul_sqrt=sv["f"], out_dtype=BF16, dep=dep,
                 name="mm_2_dgrad")
    g["w_1"] = _mm_tn(sv["h2"], dfp, tm=1024, tn=512, tk=4096, out_blocks=p["w_1"].shape[0], out_dtype=wdt,
                      name="mm_1_wgrad")
    dh2 = _mm_nt(dfp, p["w_1"], tm=512, tn=1024, tk=512, whole_b=True, name="mm_1_dgrad")
    dx_mid, dx_mid_b, g["g_mlp"] = _rms_bwd(sv["x_mid"], p["g_mlp"], dh2, dx, name="rms_mlp_bwd")
    return (dx_mid, dx_mid_b), g


def _layer_bwd_mix(dx_mid, p, sv, *, S, wdt, dep=None, on_gate_grads=None, on_weight_grads=None):
    offs = sv["offs"]
    dx_mid, dx_mid_b = dx_mid
    g = {}
    g["w_o"] = _mm_tn(sv["m"], dx_mid_b, tm=1024, tn=1024, tk=4096, out_dtype=wdt, name="mm_o_wgrad")
    dm = _mm_nt(dx_mid_b, p["w_o"], tm=1024, tn=1024, tk=1024, dep=dep, name="mm_o_dgrad")
    dz = lax.empty(sv["z"].shape, BF16)
    dya, dyb, dz = _merge_bwd(sv["z"], sv["ya"], sv["yb"], dm, dz, off_sa=offs["sa"], off_sb=offs["sb"], name="merge_bwd")
    g["w_b_out"] = _mm_tn(sv["hb"], dyb, tm=768, tn=1024, tk=4096, out_dtype=wdt, name="mm_b_out_wgrad")
    dhb = _mm_nt(dyb, p["w_b_out"], tm=1024, tn=1536, tk=1024, name="mm_b_out_dgrad")
    dv0, dz, dlam, dba, dbx, g["w_rg_a"], g["w_rg_x"] = _rglru_bwd(
        dhb, sv["hs"], sv["z"], sv["r"], sv["ig"], sv["v0"], p["wg_a"], p["wg_x"], p["lam"], dz, S=S, off_gate=offs["gb"],
        name="rglru_bwd")
    g["lam"], g["b_rg_a"], g["b_rg_x"] = dlam[:1], dba[:1], dbx[:1]
    dep_gates = on_gate_grads(g) if on_gate_grads is not None else None
    dz, g["conv_b_w"], g["conv_b_b"] = _conv_bwd(sv["z"], p["conv_b_w"], dv0, dz, S=S, off_v=offs["xb"], off_g=None,
                                                 name="conv_b_bwd")
    g["w_a_out"] = _mm_tn(sv["u2"], dya, tm=1024, tn=1024, tk=4096, out_dtype=wdt, name="mm_a_out_wgrad")
    du2 = _mm_nt(dya, p["w_a_out"], tm=1024, tn=1024, tk=1024, dep=dep_gates, name="mm_a_out_dgrad")
    du1, g["ln_g"], g["ln_b"] = _ln_silu_bwd(sv["u1"], p["ln_g"], p["ln_b"], du2, name="ln_silu_bwd")
    dz, g["conv_a_w"], g["conv_a_b"] = _conv_bwd(sv["z"], p["conv_a_w"], du1, dz, S=S, off_v=offs["va"],
                                                 off_g=offs["ga"], name="conv_a_bwd")
    g["w_in"], db_in = _mm_tn(sv["h"], dz, tm=1024, tn=512, tk=4096, out_blocks=p["w_in"].shape[0], colsum=True,
                              out_dtype=wdt, name="mm_in_wgrad")
    g["b_in"] = db_in[:1]
    dep_in = on_weight_grads(g) if on_weight_grads is not None else None
    dh = _mm_nt(dz, p["w_in"], tm=256, tn=1024, tk=512, whole_b=True, dep=dep_in, name="mm_in_dgrad")
    dx_in, dx_in_b, g["g_mix"] = _rms_bwd(sv["x"], p["g_mix"], dh, dx_mid, name="rms_mix_bwd")
    return (dx_in, dx_in_b), g


def _layer_bwd(dx, p, sv, *, S, wdt=F32):
    dx_mid, g = _layer_bwd_mlp(dx, p, sv, wdt=wdt)
    dx_in, g2 = _layer_bwd_mix(dx_mid, p, sv, S=S, wdt=wdt)
    g.update(g2)
    return dx_in, g


def _local_step(x, tgt, layers, g_final, *, S, wdt=F32):
    saved = []
    for p in layers:
        x, sv = _layer_fwd(x, p, S=S)
        saved.append(sv)
    loss, dx, dxb, dg_final = _final_loss(x, g_final, tgt, name="final_loss")
    dx = (dx, dxb)
    grads = [None] * len(layers)
    for l in reversed(range(len(layers))):
        dx, grads[l] = _layer_bwd(dx, layers[l], saved[l], S=S, wdt=wdt)
    return loss, dx[0], grads, dg_final


_HBM = pl.BlockSpec(memory_space=pltpu.HBM)
_MESH = pl.DeviceIdType.MESH


_SEM = pl.BlockSpec(memory_space=pltpu.SEMAPHORE)
_ANY = pl.BlockSpec(memory_space=pl.ANY)
_FLIPS = [(dx, dy, dc) for dx in (0, 1) for dy in (0, 1) for dc in (0, 1)][1:]


def _place(shard, me_idx, dtype, *, name, dep=None):
    r, cc = shard.shape
    tr = 512 if r % 512 == 0 else r

    def body(me_ref, s_ref, *rest):
        del me_ref
        rest[-1][...] = s_ref[...].astype(dtype)

    in_specs, args = [pl.BlockSpec((tr, cc), lambda i, me: (i, 0))], [me_idx, shard]
    if dep is not None:
        in_specs.append(pl.BlockSpec(dep.shape, lambda i, me: (0, 0)))
        args.append(dep)
    grid_spec = pltpu.PrefetchScalarGridSpec(
        num_scalar_prefetch=1, grid=(r // tr,), in_specs=in_specs,
        out_specs=pl.BlockSpec((None, tr, cc), lambda i, me: (me[0], i, 0)))
    return pl.pallas_call(body, grid_spec=grid_spec, out_shape=jax.ShapeDtypeStruct((N_DEV, r, cc), dtype),
                          compiler_params=_cp("arbitrary"), name=name)(*args)


def _exchange_copies(srcs, lands, send_sems, recv_sems):
    x, y, c = lax.axis_index("x"), lax.axis_index("y"), lax.axis_index("c")
    me = 4 * x + 2 * y + c
    pairs = []
    for k, (dx, dy, dc) in enumerate(_FLIPS):
        peer = (1 - x if dx else x, 1 - y if dy else y, 1 - c if dc else c)
        pidx = 4 * peer[0] + 2 * peer[1] + peer[2]
        for a, land in enumerate(lands):
            src = land.at[me] if srcs is None else srcs[a].at[pidx]
            sem = k * len(lands) + a
            out = pltpu.make_async_remote_copy(src_ref=src, dst_ref=land.at[me], send_sem=send_sems.at[sem],
                                               recv_sem=recv_sems.at[sem], device_id=peer, device_id_type=_MESH)
            arrival = pltpu.make_async_remote_copy(src_ref=src, dst_ref=land.at[pidx], send_sem=send_sems.at[sem],
                                                   recv_sem=recv_sems.at[sem], device_id=peer, device_id_type=_MESH)
            pairs.append((out, arrival))
    return pairs


def _exchange_start(srcs, lands, *, name):
    n = len(lands)
    bufs = list(lands) if srcs is None else list(srcs) + list(lands)
    nb = len(bufs)

    def body(*refs):
        ins = refs[:nb]
        send_sems, recv_sems = refs[nb], refs[nb + 1]
        token = refs[-1]
        for out, _ in _exchange_copies(None if srcs is None else ins[:n], ins[nb - n:], send_sems, recv_sems):
            out.start()
        token[...] = jnp.zeros_like(token)

    sems = pltpu.SemaphoreType.DMA((len(_FLIPS) * n,))
    res = pl.pallas_call(
        body, name=name, in_specs=[_HBM] * nb,
        out_shape=(sems, sems, *[pltpu.HBM(b.shape, b.dtype) for b in bufs], jax.ShapeDtypeStruct((SUBLANES, 128), F32)),
        out_specs=(_SEM, _SEM, *[_HBM] * nb, pl.BlockSpec(memory_space=pltpu.VMEM)),
        input_output_aliases={i: 2 + i for i in range(nb)},
        compiler_params=pltpu.CompilerParams(has_side_effects=pltpu.SideEffectType.DATAFLOW_SIDE_EFFECTING),
    )(*[pltpu.with_memory_space_constraint(b, pltpu.HBM) for b in bufs])
    return res[0], res[1], list(res[2:2 + nb]), res[-1]


def _exchange_wait(send_sems, recv_sems, bufs, *, scatter, after, name):
    nb = len(bufs)
    n = nb // 2 if scatter else nb

    def body(*refs):
        ins = refs[:nb]
        for out, arrival in _exchange_copies(ins[:n] if scatter else None, ins[nb - n:], refs[nb], refs[nb + 1]):
            out.wait_send()
            arrival.wait_recv()

    extra = [] if after is None else [after]
    res = pl.pallas_call(
        body, name=name, in_specs=[_HBM] * nb + [_SEM, _SEM] + [_ANY] * len(extra),
        out_shape=tuple(pltpu.HBM(b.shape, b.dtype) for b in bufs), out_specs=tuple([_HBM] * nb),
        input_output_aliases={i: i for i in range(nb)},
        compiler_params=pltpu.CompilerParams(has_side_effects=pltpu.SideEffectType.DATAFLOW_SIDE_EFFECTING),
    )(*bufs, send_sems, recv_sems, *extra)
    return list(res)


def _adamw_math(w, g, m, v):
    m = ADAM_B1 * m + (1.0 - ADAM_B1) * g
    v = ADAM_B2 * v + (1.0 - ADAM_B2) * (g * g)
    m_hat = m / (1.0 - ADAM_B1 ** ADAM_STEP)
    v_hat = v / (1.0 - ADAM_B2 ** ADAM_STEP)
    delta = -ADAM_LR * (m_hat / (jnp.sqrt(v_hat) + ADAM_EPS) + ADAM_WD * w)
    return delta, m, v


def _adamw(w, m, v, parts, prev, layer, me_idx, *, name, own=None):
    L, r, cc = w.shape
    P = parts.shape[0]
    tr = 512 if r % 512 == 0 else r
    if prev is None:
        prev = tuple(lax.empty(w.shape, F32) for _ in range(4))

    def body(me_ref, w_ref, m_ref, v_ref, p_ref, *rest):
        g_ref, d_ref, nm_ref, nv_ref = rest[-4:]
        if own is None:
            g = p_ref[0].astype(F32)
            for q in range(1, P):
                g = g + p_ref[q].astype(F32)
        else:
            me = me_ref[0]
            g = rest[0][...].astype(F32)
            for q in range(P):
                g = g + jnp.where(q == me, 0.0, p_ref[q].astype(F32))
        d, nm, nv = _adamw_math(w_ref[...], g, m_ref[...], v_ref[...])
        g_ref[...] = g
        d_ref[...] = d
        nm_ref[...] = nm
        nv_ref[...] = nv

    blk = pl.BlockSpec((None, tr, cc), lambda i, me: (layer, i, 0))
    in_specs = [blk, blk, blk, pl.BlockSpec((P, tr, cc), lambda i, me: (0, i, 0))]
    args = [me_idx, w, m, v, parts]
    if own is not None:
        in_specs.append(pl.BlockSpec((None, tr, cc), lambda i, me: (me[0], i, 0)))
        args.append(own)
    first_prev = len(args)
    in_specs += [_ANY] * 4
    args += list(prev)
    grid_spec = pltpu.PrefetchScalarGridSpec(num_scalar_prefetch=1, grid=(r // tr,), in_specs=in_specs,
                                             out_specs=(blk, blk, blk, blk))
    o = jax.ShapeDtypeStruct(w.shape, F32)
    return pl.pallas_call(body, grid_spec=grid_spec, out_shape=(o, o, o, o),
                          input_output_aliases={first_prev + i: i for i in range(4)},
                          compiler_params=_cp("parallel"), name=name)(*args)


_SHARDED = ("w_in", "conv_a_w", "w_a_out", "conv_b_w", "w_b_out", "w_o", "w_1", "w_2")
_COL_SHARDED = ("w_in", "conv_a_w", "conv_b_w", "w_1")
_REPLICATED = ("g_mix", "b_in", "conv_a_b", "ln_g", "ln_b", "conv_b_b", "w_rg_a", "b_rg_a", "w_rg_x", "b_rg_x", "lam",
               "g_mlp")
_WEIGHTS = ("g_mix", "w_in", "b_in", "conv_a_w", "conv_a_b", "ln_g", "ln_b", "w_a_out", "conv_b_w", "conv_b_b", "w_rg_a",
            "b_rg_a", "w_rg_x", "b_rg_x", "lam", "w_b_out", "w_o", "g_mlp", "w_1", "w_2", "g_final")
_LANES = 128


def _cols_from_blocks(b):
    nb, K, n = b.shape
    return b.transpose(1, 0, 2).reshape(K, nb * n)


def _blocks_from_cols(w, K):
    n = w.shape[1] // N_DEV
    return w[:K].reshape(K, N_DEV, n).transpose(1, 0, 2)


def kernel(x, g_mix, w_in, b_in, conv_a_w, conv_a_b, ln_g, ln_b, w_a_out, conv_b_w, conv_b_b, w_rg_a, b_rg_a, w_rg_x, b_rg_x, lam, w_b_out, w_o, g_mlp, w_1, w_2, g_final, loss_target, m_g_mix, m_w_in, m_b_in, m_conv_a_w, m_conv_a_b, m_ln_g, m_ln_b, m_w_a_out, m_conv_b_w, m_conv_b_b, m_w_rg_a, m_b_rg_a, m_w_rg_x, m_b_rg_x, m_lam, m_w_b_out, m_w_o, m_g_mlp, m_w_1, m_w_2, m_g_final, v_g_mix, v_w_in, v_b_in, v_conv_a_w, v_conv_a_b, v_ln_g, v_ln_b, v_w_a_out, v_conv_b_w, v_conv_b_b, v_w_rg_a, v_b_rg_a, v_w_rg_x, v_b_rg_x, v_lam, v_w_b_out, v_w_o, v_g_mlp, v_w_1, v_w_2, v_g_final):
    W = dict(g_mix=g_mix, w_in=w_in, b_in=b_in, conv_a_w=conv_a_w, conv_a_b=conv_a_b, ln_g=ln_g, ln_b=ln_b,
             w_a_out=w_a_out, conv_b_w=conv_b_w, conv_b_b=conv_b_b, w_rg_a=w_rg_a, b_rg_a=b_rg_a, w_rg_x=w_rg_x,
             b_rg_x=b_rg_x, lam=lam, w_b_out=w_b_out, w_o=w_o, g_mlp=g_mlp, w_1=w_1, w_2=w_2, g_final=g_final)
    M = dict(g_mix=m_g_mix, w_in=m_w_in, b_in=m_b_in, conv_a_w=m_conv_a_w, conv_a_b=m_conv_a_b, ln_g=m_ln_g, ln_b=m_ln_b,
             w_a_out=m_w_a_out, conv_b_w=m_conv_b_w, conv_b_b=m_conv_b_b, w_rg_a=m_w_rg_a, b_rg_a=m_b_rg_a,
             w_rg_x=m_w_rg_x, b_rg_x=m_b_rg_x, lam=m_lam, w_b_out=m_w_b_out, w_o=m_w_o, g_mlp=m_g_mlp, w_1=m_w_1,
             w_2=m_w_2, g_final=m_g_final)
    V = dict(g_mix=v_g_mix, w_in=v_w_in, b_in=v_b_in, conv_a_w=v_conv_a_w, conv_a_b=v_conv_a_b, ln_g=v_ln_g, ln_b=v_ln_b,
             w_a_out=v_w_a_out, conv_b_w=v_conv_b_w, conv_b_b=v_conv_b_b, w_rg_a=v_w_rg_a, b_rg_a=v_b_rg_a,
             w_rg_x=v_w_rg_x, b_rg_x=v_b_rg_x, lam=v_lam, w_b_out=v_w_b_out, w_o=v_w_o, g_mlp=v_g_mlp, w_1=v_w_1,
             w_2=v_w_2, g_final=v_g_final)
    NB, S, D = x.shape
    L = g_mix.shape[0]
    hd = w_rg_a.shape[-1]
    me_idx = (4 * lax.axis_index("x") + 2 * lax.axis_index("y") + lax.axis_index("c")).astype(jnp.int32).reshape(1)

    stages = (("in", ("w_in",)), ("mix", ("conv_a_w", "w_a_out", "conv_b_w", "w_b_out", "w_o")), ("mlp", ("w_1", "w_2")))
    gathers = {}
    started = jnp.zeros((), F32)
    first = None
    for l in range(L):
        for stage, names in stages:
            lands = [_place(W[k][l], me_idx, F32 if k.startswith("conv") else BF16, dep=first, name="place_" + k)
                     for k in names]
            send_sems, recv_sems, bufs, token = _exchange_start(None, lands, name=f"weights_start_{stage}_{l}")
            gathers[l, stage] = (names, send_sems, recv_sems, bufs)
            started = started + token[0, 0]
            if first is None:
                first = token

    xt = x.reshape(NB * S, D)
    layers, saved = [], []
    for l in range(L):
        p = {k: W[k][l][None] for k in ("g_mix", "b_in", "conv_a_b", "ln_g", "ln_b", "conv_b_b", "b_rg_a", "b_rg_x",
                                         "lam", "g_mlp")}
        if l == 0:
            p["g_mix"] = p["g_mix"] + started
        p["wg_a"] = _group_weights(w_rg_a[l])
        p["wg_x"] = _group_weights(w_rg_x[l])

        def fetch(stage, after, l=l, p=p):
            names, send_sems, recv_sems, bufs = gathers[l, stage]
            bufs = _exchange_wait(send_sems, recv_sems, bufs, scatter=False, after=after,
                                  name=f"weights_wait_{stage}_{l}")
            for k, full in zip(names, bufs):
                if k in ("w_in", "w_1"):
                    p[k] = full
                elif k in _COL_SHARDED:
                    p[k] = _cols_from_blocks(full)
                else:
                    p[k] = full.reshape(-1, full.shape[-1])

        layers.append(p)
        xt, sv = _layer_fwd(xt, p, S=S, fetch=fetch)
        saved.append(sv)
    loss, dx, dxb, dg_final = _final_loss(xt, g_final[None], loss_target.reshape(NB * S, D), name="final_loss")
    dx = (dx, dxb)

    results = {k: None for k in _SHARDED}

    def scatter_start(l, names, g, tag):
        srcs = []
        for k in names:
            shard_shape = W[k].shape[1:]
            if k in ("w_in", "w_1"):
                srcs.append(g[k])
            elif k in _COL_SHARDED:
                srcs.append(_blocks_from_cols(g[k], shard_shape[0]).astype(BF16))
            else:
                srcs.append(g[k].reshape((N_DEV,) + shard_shape))
        lands = [lax.empty(s.shape, BF16) for s in srcs]
        send_sems, recv_sems, bufs, token = _exchange_start(srcs, lands, name=f"grads_start_{tag}_{l}")
        return (names, send_sems, recv_sems, bufs, tag), token

    def scatter_finish(l, flight, after):
        names, send_sems, recv_sems, bufs, tag = flight
        bufs = _exchange_wait(send_sems, recv_sems, bufs, scatter=True, after=after, name=f"grads_wait_{tag}_{l}")
        n = len(names)
        for k, own, land in zip(names, bufs[:n], bufs[n:]):
            results[k] = _adamw(W[k], M[k], V[k], land, results[k], l, me_idx, own=own, name="adamw_" + k)

    gate_names = ("w_rg_a", "w_rg_x")
    gate_results = {k: None for k in gate_names}

    def small_start(l, g):
        lands = [_place(g[k].reshape(-1, hd), me_idx, F32, name="place_gate_grad") for k in gate_names]
        send_sems, recv_sems, bufs, token = _exchange_start(None, lands, name=f"small_start_{l}")
        return (send_sems, recv_sems, bufs), token

    def small_finish(l, flight, after):
        send_sems, recv_sems, bufs = flight
        bufs = _exchange_wait(send_sems, recv_sems, bufs, scatter=False, after=after, name=f"small_wait_{l}")
        for k, g_all in zip(gate_names, bufs):
            gate_results[k] = _adamw(W[k].reshape(L, -1, hd), M[k].reshape(L, -1, hd), V[k].reshape(L, -1, hd), g_all,
                                     gate_results[k], l, me_idx, name="adamw_gate")

    grads = [None] * L
    in_flight = []
    dep = None
    for l in reversed(range(L)):
        dx_mid, g = _layer_bwd_mlp(dx, layers[l], saved[l], wdt=BF16, dep=dep)
        f_mlp, dep = scatter_start(l, ("w_2", "w_1"), g, "mlp")
        flights = [f_mlp]

        small = []

        def on_gate_grads(g_part, l=l, small=small):
            f_small, token = small_start(l, g_part)
            small.append(f_small)
            return token

        def on_weight_grads(g_part, l=l, flights=flights):
            f_mix, token = scatter_start(l, ("w_o", "w_b_out", "w_a_out", "conv_a_w", "conv_b_w", "w_in"), g_part, "mix")
            flights.append(f_mix)
            return token

        dx, g_mix_part = _layer_bwd_mix(dx_mid, layers[l], saved[l], S=S, wdt=BF16, dep=dep,
                                        on_gate_grads=on_gate_grads, on_weight_grads=on_weight_grads)
        dep = None
        g.update(g_mix_part)
        grads[l] = g
        for l_prev, fs, f_sm in in_flight:
            for flight in fs:
                scatter_finish(l_prev, flight, dx[0])
            small_finish(l_prev, f_sm, dx[0])
        in_flight = [(l, flights, small[0])]

    vec_names = tuple(k for k in _REPLICATED if k not in gate_names)
    n_vec = sum(W[k].shape[1] for k in vec_names)

    def vec_pack(rows, final, last):
        tail = jnp.concatenate([final.reshape(1, -1), jnp.broadcast_to(last.reshape(1, 1), (1, _LANES))], axis=1)
        tail = jnp.pad(tail, ((0, SUBLANES - L - 1), (0, n_vec - tail.shape[1])))
        return jnp.concatenate([rows, tail], axis=0)

    g_rows = jnp.concatenate([jnp.concatenate([grads[l][k] for k in vec_names], axis=1) for l in range(L)], axis=0)
    land = _place(vec_pack(g_rows, dg_final, loss[0, :1]), me_idx, F32, name="place_vectors")
    vec_send_sems, vec_recv_sems, vec_bufs, _ = _exchange_start(None, [land], name="vectors_start")
    for l_prev, fs, f_sm in in_flight:
        after = dx[0] if L == 1 else results["w_in"][0]
        for flight in fs:
            scatter_finish(l_prev, flight, after)
            after = results[flight[0][-1]][0]
        small_finish(l_prev, f_sm, after)
    out_g, out_d, out_m, out_v = {}, {}, {}, {}
    for k in _SHARDED:
        out_g[k], out_d[k], out_m[k], out_v[k] = results[k]
    for k in gate_names:
        out_g[k], out_d[k], out_m[k], out_v[k] = (a.reshape(W[k].shape) for a in gate_results[k])

    zero = jnp.zeros((1,), F32)
    (g_all,) = _exchange_wait(vec_send_sems, vec_recv_sems, vec_bufs, scatter=False, after=results["w_in"][0],
                              name="vectors_wait")
    vec_out = _adamw(vec_pack(jnp.concatenate([W[k] for k in vec_names], axis=1), g_final, zero)[None],
                     vec_pack(jnp.concatenate([M[k] for k in vec_names], axis=1), m_g_final, zero)[None],
                     vec_pack(jnp.concatenate([V[k] for k in vec_names], axis=1), v_g_final, zero)[None],
                     g_all, None, 0, me_idx, name="adamw_vectors")
    vec_out = [a[0] for a in vec_out]
    for res, arr in zip((out_g, out_d, out_m, out_v), vec_out):
        off = 0
        for k in vec_names:
            res[k] = arr[:L, off:off + W[k].shape[1]]
            off += W[k].shape[1]
        res["g_final"] = arr[L, :g_final.size]
    loss_out = vec_out[0][L, g_final.size]

    return (loss_out, dx[0].reshape(NB, S, D), *[out_g[k] for k in _WEIGHTS], *[out_d[k] for k in _WEIGHTS],
            *[out_m[k] for k in _WEIGHTS], *[out_v[k] for k in _WEIGHTS])
```

```python
import functools

import jax
import jax.numpy as jnp
from jax import lax
from jax.experimental import pallas as pl
from jax.experimental.pallas import tpu as pltpu

F32 = jnp.float32
BF16 = jnp.bfloat16

EPS = 1e-6
LRU_C = 8.0
N_RNN_HEADS = 16
HEADS_PER_GROUP = 4
N_DEV = 8
ADAM_LR, ADAM_B1, ADAM_B2, ADAM_EPS, ADAM_WD, ADAM_STEP = 0.001, 0.9, 0.999, 1e-08, 0.01, 10

VMEM_LIMIT_BYTES = 48 * 1024 * 1024
CONV_PAD = 32
CONV_CHUNK = 128
SUBLANES = 8


def _cp(*sem):
    return pltpu.CompilerParams(dimension_semantics=sem, vmem_limit_bytes=VMEM_LIMIT_BYTES)


def _sig(x):
    return 1.0 / (1.0 + jnp.exp(-x))


def _gelu(x):
    c = 0.7978845608028654
    return 0.5 * x * (1.0 + jnp.tanh(c * (x + 0.044715 * x * x * x)))


def _gelu_grad(x):
    c = 0.7978845608028654
    th = jnp.tanh(c * (x + 0.044715 * x * x * x))
    return 0.5 * (1.0 + th) + 0.5 * x * (1.0 - th * th) * c * (1.0 + 3.0 * 0.044715 * x * x)


def _mm_nn(a, b, *, tm, tn, name, bias=None, resid=None, relu2=False, out_dtype=F32, a_resident=False):
    M, K = a.shape
    blocked = b.ndim == 3
    N = b.shape[0] * b.shape[2] if blocked else b.shape[1]
    tm = min(tm, M)
    tn = min(tn, N)
    if blocked:
        assert tn == b.shape[2]
    n_extra = (bias is not None) + (resid is not None)

    def body(*refs):
        acc = jnp.dot(refs[0][...].astype(BF16), refs[1][...].astype(BF16), preferred_element_type=F32)
        k = 2
        if bias is not None:
            acc = acc + refs[k][...]
            k += 1
        if resid is not None:
            acc = acc + refs[k][...]
            k += 1
        if relu2:
            p = jnp.maximum(acc, 0.0)
            acc = p * p
        refs[k][...] = acc.astype(out_dtype)

    def spec(shape, index):
        return pl.BlockSpec(shape, (lambda i, j: index(j, i)) if a_resident else index)

    in_specs = [spec((tm, K), lambda j, i: (i, 0))]
    if blocked:
        in_specs.append(spec((None, K, tn), lambda j, i: (j, 0, 0)))
    else:
        in_specs.append(spec((K, tn), lambda j, i: (0, j)))
    args = [a, b]
    if bias is not None:
        in_specs.append(spec((1, tn), lambda j, i: (0, j)))
        args.append(bias)
    if resid is not None:
        in_specs.append(spec((tm, tn), lambda j, i: (i, j)))
        args.append(resid)
    out_specs = spec((tm, tn), lambda j, i: (i, j))
    out_shape = jax.ShapeDtypeStruct((M, N), out_dtype)
    del n_extra
    grid = (M // tm, N // tn) if a_resident else (N // tn, M // tm)
    return pl.pallas_call(body, grid=grid, in_specs=in_specs, out_specs=out_specs,
                          out_shape=out_shape, compiler_params=_cp("parallel", "parallel"), name=name)(*args)


def _mm_nt(a, b, *, tm, tn, tk, name, mul_sqrt=None, resid=None, out_dtype=F32, dep=None, whole_b=False):
    M, N = a.shape
    blocked = b.ndim == 3
    Kout = b.shape[1] if blocked else b.shape[0]
    tm = min(tm, M)
    tn = min(tn, Kout)
    tk = N if whole_b else (b.shape[2] if blocked else min(tk, N))
    nk = N // tk
    nt_dims = (((1,), (1,)), ((), ()))

    def body(*refs):
        acc_ref = refs[-1]
        kk = pl.program_id(2)
        if blocked and whole_b:
            n = b.shape[2]
            part = None
            for jb in range(b.shape[0]):
                pj = lax.dot_general(refs[0][:, jb * n:(jb + 1) * n].astype(BF16), refs[1][jb].astype(BF16), nt_dims,
                                     preferred_element_type=F32)
                part = pj if part is None else part + pj
        else:
            part = lax.dot_general(refs[0][...].astype(BF16), refs[1][...].astype(BF16), nt_dims,
                                   preferred_element_type=F32)

        def finish(acc):
            k = 2
            if mul_sqrt is not None:
                acc = acc * (2.0 * jnp.sqrt(refs[k][...].astype(F32)))
                k += 1
            if resid is not None:
                acc = acc + refs[k][...]
                k += 1
            if dep is not None:
                k += 1
            refs[k][...] = acc.astype(out_dtype)

        if nk == 1:
            finish(part)
            return

        @pl.when(kk == 0)
        def _():
            acc_ref[...] = part

        @pl.when(kk > 0)
        def _():
            acc_ref[...] += part

        @pl.when(kk == nk - 1)
        def _():
            finish(acc_ref[...])

    in_specs = [pl.BlockSpec((tm, tk), lambda i, j, k: (i, k))]
    if blocked and whole_b:
        in_specs.append(pl.BlockSpec((b.shape[0], tn, b.shape[2]), lambda i, j, k: (0, j, 0)))
    elif blocked:
        in_specs.append(pl.BlockSpec((None, tn, tk), lambda i, j, k: (k, j, 0)))
    else:
        in_specs.append(pl.BlockSpec((tn, tk), lambda i, j, k: (j, k)))
    args = [a, b]
    for extra in (mul_sqrt, resid):
        if extra is not None:
            in_specs.append(pl.BlockSpec((tm, tn), lambda i, j, k: (i, j)))
            args.append(extra)
    if dep is not None:
        in_specs.append(pl.BlockSpec(dep.shape, lambda i, j, k: (0, 0)))
        args.append(dep)
    return pl.pallas_call(body, grid=(M // tm, Kout // tn, nk), in_specs=in_specs,
                          out_specs=pl.BlockSpec((tm, tn), lambda i, j, k: (i, j)),
                          out_shape=jax.ShapeDtypeStruct((M, Kout), out_dtype),
                          scratch_shapes=[pltpu.VMEM((tm, tn), F32)] if nk > 1 else [],
                          compiler_params=_cp("parallel", "parallel", "arbitrary"), name=name)(*args)


def _mm_tn(a, b, *, tm, tn, tk, name, out_blocks=None, colsum=False, out_dtype=F32):
    T, M = a.shape
    N = b.shape[1]
    tm = min(tm, M)
    tk = min(tk, T)
    if out_blocks is not None:
        tn = N // out_blocks
        tm = M
    tn = min(tn, N)
    nk = T // tk
    if colsum:
        assert tm == M

    def body(*refs):
        a_ref, b_ref, o_ref, acc_ref = refs[0], refs[1], refs[2], refs[-1]
        kk = pl.program_id(2)
        bv = b_ref[...]
        part = lax.dot_general(a_ref[...].astype(BF16), bv.astype(BF16),
                               (((0,), (0,)), ((), ())), preferred_element_type=F32)

        if colsum:
            csum = jnp.broadcast_to(jnp.sum(bv.astype(F32), axis=0, keepdims=True), (SUBLANES, tn))

        if nk == 1:
            o_ref[...] = part.astype(out_dtype)
            if colsum:
                refs[3][...] = csum
            return

        @pl.when(kk == 0)
        def _():
            acc_ref[...] = part
            if colsum:
                refs[3][...] = csum

        @pl.when(kk > 0)
        def _():
            acc_ref[...] += part
            if colsum:
                refs[3][...] += csum

        @pl.when(kk == nk - 1)
        def _():
            o_ref[...] = acc_ref[...].astype(out_dtype)

    in_specs = [pl.BlockSpec((tk, tm), lambda i, j, k: (k, i)), pl.BlockSpec((tk, tn), lambda i, j, k: (k, j))]
    if out_blocks is not None:
        o_shape = jax.ShapeDtypeStruct((out_blocks, M, tn), out_dtype)
        o_spec = pl.BlockSpec((None, M, tn), lambda i, j, k: (j, 0, 0))
    else:
        o_shape = jax.ShapeDtypeStruct((M, N), out_dtype)
        o_spec = pl.BlockSpec((tm, tn), lambda i, j, k: (i, j))
    if colsum:
        out_shape = (o_shape, jax.ShapeDtypeStruct((SUBLANES, N), F32))
        out_specs = (o_spec, pl.BlockSpec((SUBLANES, tn), lambda i, j, k: (0, j)))
    else:
        out_shape, out_specs = o_shape, o_spec
    return pl.pallas_call(body, grid=(M // tm, N // tn, nk), in_specs=in_specs, out_specs=out_specs,
                          out_shape=out_shape, scratch_shapes=[pltpu.VMEM((tm, tn), F32)] if nk > 1 else [],
                          compiler_params=_cp("parallel", "parallel", "arbitrary"), name=name)(a, b)


def _rms_fwd(x, g, *, name, tr=512):
    T, D = x.shape
    tr = min(tr, T)

    def body(x_ref, g_ref, h_ref):
        xv = x_ref[...]
        r = lax.rsqrt(jnp.mean(xv * xv, axis=-1, keepdims=True) + EPS)
        h_ref[...] = (xv * r * g_ref[...]).astype(BF16)

    return pl.pallas_call(body, grid=(T // tr,),
                          in_specs=[pl.BlockSpec((tr, D), lambda i: (i, 0)), pl.BlockSpec((1, D), lambda i: (0, 0))],
                          out_specs=pl.BlockSpec((tr, D), lambda i: (i, 0)),
                          out_shape=jax.ShapeDtypeStruct((T, D), BF16), compiler_params=_cp("parallel"), name=name)(x, g)


def _rms_bwd(x, g, dh, dres, *, name, tr=512):
    T, D = x.shape
    tr = min(tr, T)

    def body(x_ref, g_ref, dh_ref, dres_ref, dx_ref, dxb_ref, dg_ref):
        xv = x_ref[...]
        r = lax.rsqrt(jnp.mean(xv * xv, axis=-1, keepdims=True) + EPS)
        n = xv * r
        dh = dh_ref[...]
        dn = dh * g_ref[...]
        dx = dres_ref[...] + r * (dn - n * jnp.mean(dn * n, axis=-1, keepdims=True))
        dx_ref[...] = dx
        dxb_ref[...] = dx.astype(BF16)
        part = jnp.sum(dh * n, axis=0, keepdims=True)

        @pl.when(pl.program_id(0) == 0)
        def _():
            dg_ref[...] = part

        @pl.when(pl.program_id(0) > 0)
        def _():
            dg_ref[...] += part

    row = pl.BlockSpec((tr, D), lambda i: (i, 0))
    vec = pl.BlockSpec((1, D), lambda i: (0, 0))
    return pl.pallas_call(body, grid=(T // tr,), in_specs=[row, vec, row, row], out_specs=(row, row, vec),
                          out_shape=(jax.ShapeDtypeStruct((T, D), F32), jax.ShapeDtypeStruct((T, D), BF16),
                                     jax.ShapeDtypeStruct((1, D), F32)),
                          compiler_params=_cp("arbitrary"), name=name)(x, g, dh, dres)


def _final_loss(x, g, tgt, *, name, tr=512):
    T, D = x.shape
    tr = min(tr, T)

    def body(x_ref, g_ref, t_ref, loss_ref, dx_ref, dxb_ref, dg_ref):
        xv = x_ref[...]
        gv = g_ref[...]
        r = lax.rsqrt(jnp.mean(xv * xv, axis=-1, keepdims=True) + EPS)
        n = xv * r
        e = n * gv - t_ref[...]
        lpart = 0.5 * jnp.sum(jnp.mean(e * e, axis=-1, keepdims=True), axis=0, keepdims=True)
        dy = e * (1.0 / D)
        dn = dy * gv
        dx = r * (dn - n * jnp.mean(dn * n, axis=-1, keepdims=True))
        dx_ref[...] = dx
        dxb_ref[...] = dx.astype(BF16)
        gpart = jnp.sum(dy * n, axis=0, keepdims=True)

        @pl.when(pl.program_id(0) == 0)
        def _():
            dg_ref[...] = gpart
            loss_ref[...] = jnp.broadcast_to(lpart, (1, 128))

        @pl.when(pl.program_id(0) > 0)
        def _():
            dg_ref[...] += gpart
            loss_ref[...] += jnp.broadcast_to(lpart, (1, 128))

    row = pl.BlockSpec((tr, D), lambda i: (i, 0))
    vec = pl.BlockSpec((1, D), lambda i: (0, 0))
    return pl.pallas_call(body, grid=(T // tr,), in_specs=[row, vec, row],
                          out_specs=(pl.BlockSpec((1, 128), lambda i: (0, 0)), row, row, vec),
                          out_shape=(jax.ShapeDtypeStruct((1, 128), F32), jax.ShapeDtypeStruct((T, D), F32),
                                     jax.ShapeDtypeStruct((T, D), BF16), jax.ShapeDtypeStruct((1, D), F32)),
                          compiler_params=_cp("arbitrary"), name=name)(x, g, tgt)


def _ln_silu_fwd(u, g, b, *, name, tr=512):
    T, C = u.shape
    tr = min(tr, T)

    def body(u_ref, g_ref, b_ref, o_ref):
        uv = u_ref[...]
        mu = jnp.mean(uv, axis=-1, keepdims=True)
        xc = uv - mu
        r = lax.rsqrt(jnp.mean(xc * xc, axis=-1, keepdims=True) + EPS)
        y = xc * r * g_ref[...] + b_ref[...]
        o_ref[...] = (y * _sig(y)).astype(BF16)

    row = pl.BlockSpec((tr, C), lambda i: (i, 0))
    vec = pl.BlockSpec((1, C), lambda i: (0, 0))
    return pl.pallas_call(body, grid=(T // tr,), in_specs=[row, vec, vec], out_specs=row,
                          out_shape=jax.ShapeDtypeStruct((T, C), BF16), compiler_params=_cp("parallel"),
                          name=name)(u, g, b)


def _ln_silu_bwd(u, g, b, do, *, name, tr=512):
    T, C = u.shape
    tr = min(tr, T)

    def body(u_ref, g_ref, b_ref, do_ref, du_ref, dg_ref, db_ref):
        uv = u_ref[...]
        gv = g_ref[...]
        mu = jnp.mean(uv, axis=-1, keepdims=True)
        xc = uv - mu
        r = lax.rsqrt(jnp.mean(xc * xc, axis=-1, keepdims=True) + EPS)
        n = xc * r
        y = n * gv + b_ref[...]
        s = _sig(y)
        dy = do_ref[...] * (s * (1.0 + y * (1.0 - s)))
        dn = dy * gv
        du_ref[...] = r * (dn - jnp.mean(dn, axis=-1, keepdims=True) - n * jnp.mean(dn * n, axis=-1, keepdims=True))
        gpart = jnp.sum(dy * n, axis=0, keepdims=True)
        bpart = jnp.sum(dy, axis=0, keepdims=True)

        @pl.when(pl.program_id(0) == 0)
        def _():
            dg_ref[...] = gpart
            db_ref[...] = bpart

        @pl.when(pl.program_id(0) > 0)
        def _():
            dg_ref[...] += gpart
            db_ref[...] += bpart

    row = pl.BlockSpec((tr, C), lambda i: (i, 0))
    vec = pl.BlockSpec((1, C), lambda i: (0, 0))
    return pl.pallas_call(body, grid=(T // tr,), in_specs=[row, vec, vec, row], out_specs=(row, vec, vec),
                          out_shape=(jax.ShapeDtypeStruct((T, C), F32), jax.ShapeDtypeStruct((1, C), F32),
                                     jax.ShapeDtypeStruct((1, C), F32)),
                          compiler_params=_cp("arbitrary"), name=name)(u, g, b, do)


def _merge_fwd(z, ya, yb, *, off_sa, off_sb, name, tr=512):
    T, D = ya.shape
    tr = min(tr, T)
    assert off_sa % D == 0 and off_sb % D == 0

    def body(sa_ref, sb_ref, ya_ref, yb_ref, m_ref):
        m_ref[...] = (_sig(sa_ref[...].astype(F32)) * ya_ref[...]
                      + _sig(sb_ref[...].astype(F32)) * yb_ref[...]).astype(BF16)

    row = pl.BlockSpec((tr, D), lambda i: (i, 0))
    return pl.pallas_call(body, grid=(T // tr,),
                          in_specs=[pl.BlockSpec((tr, D), lambda i: (i, off_sa // D)),
                                    pl.BlockSpec((tr, D), lambda i: (i, off_sb // D)), row, row],
                          out_specs=row, out_shape=jax.ShapeDtypeStruct((T, D), BF16),
                          compiler_params=_cp("parallel"), name=name)(z, z, ya, yb)


def _columns_copy(stage_ref, dz_ref, row0, rows, col0, sem):
    dst = dz_ref.at[pl.ds(pl.multiple_of(row0, SUBLANES), rows),
                    pl.ds(pl.multiple_of(col0, 128), stage_ref.shape[1])]
    return pltpu.make_async_copy(stage_ref, dst, sem)


def _merge_bwd(z, ya, yb, dm, dz, *, off_sa, off_sb, name, tr=512):
    T, D = ya.shape
    tr = min(tr, T)
    assert off_sb == off_sa + D

    def body(sa_ref, sb_ref, ya_ref, yb_ref, dm_ref, dz_in, dya_ref, dyb_ref, dz_ref, stage, sem):
        del dz_in
        i = pl.program_id(0)
        dm = dm_ref[...]
        ga = _sig(sa_ref[...].astype(F32))
        gb = _sig(sb_ref[...].astype(F32))
        dya_ref[...] = (dm * ga).astype(BF16)
        dyb_ref[...] = (dm * gb).astype(BF16)
        put = _columns_copy(stage, dz_ref, i * tr, tr, off_sa, sem)

        @pl.when(i > 0)
        def _():
            put.wait()

        stage[:, 0:D] = (dm * ya_ref[...] * ga * (1.0 - ga)).astype(BF16)
        stage[:, D:2 * D] = (dm * yb_ref[...] * gb * (1.0 - gb)).astype(BF16)
        put.start()

        @pl.when(i == T // tr - 1)
        def _():
            put.wait()

    row = pl.BlockSpec((tr, D), lambda i: (i, 0))
    o = jax.ShapeDtypeStruct((T, D), BF16)
    return pl.pallas_call(body, grid=(T // tr,),
                          in_specs=[pl.BlockSpec((tr, D), lambda i: (i, off_sa // D)),
                                    pl.BlockSpec((tr, D), lambda i: (i, off_sb // D)), row, row, row, _ANY],
                          out_specs=(row, row, _ANY), out_shape=(o, o, jax.ShapeDtypeStruct(dz.shape, dz.dtype)),
                          scratch_shapes=[pltpu.VMEM((tr, 2 * D), BF16), pltpu.SemaphoreType.DMA],
                          input_output_aliases={5: 2},
                          compiler_params=_cp("arbitrary"), name=name)(z, z, ya, yb, dm, dz)


def _shift_rows(dst_ref, src_ref, r, total, back):
    for c0 in range(0, total - SUBLANES, CONV_CHUNK):
        n = min(CONV_CHUNK, total - SUBLANES - c0)
        if back:
            dst_ref[SUBLANES + c0:SUBLANES + c0 + n, :] = src_ref[SUBLANES + c0 - r:SUBLANES + c0 - r + n, :]
        else:
            dst_ref[c0:c0 + n, :] = src_ref[c0 + r:c0 + r + n, :]


def _tap_plan(K):
    if K <= SUBLANES:
        return [(0, [(s, K - 1 - s) for s in range(K)])]
    return [(r, [(SUBLANES * q, K - 1 - (SUBLANES * q + r)) for q in range(-(-K // SUBLANES)) if SUBLANES * q + r < K])
            for r in range(SUBLANES)]


def _conv_fwd(z, w, b, *, S, off_v, off_g, name, ct=256):
    T = z.shape[0]
    K, C = w.shape
    ct = min(ct, C)
    ch = min(CONV_CHUNK, S)
    glu = off_g is not None
    assert off_v % ct == 0 and (not glu or off_g % ct == 0)
    assert SUBLANES * ((K - 1) // SUBLANES) <= CONV_PAD - SUBLANES

    def body(*refs):
        if glu:
            v_ref, g_ref, w_ref, b_ref, o_ref, pad_ref, sh_ref = refs
        else:
            v_ref, w_ref, b_ref, o_ref, pad_ref, sh_ref = refs
        pad_ref[0:CONV_PAD, :] = jnp.zeros((CONV_PAD, ct), F32)
        if glu:
            pad_ref[CONV_PAD:CONV_PAD + S, :] = v_ref[...].astype(F32) * _sig(g_ref[...].astype(F32))
        else:
            pad_ref[CONV_PAD:CONV_PAD + S, :] = v_ref[...].astype(F32)
        for r, taps in _tap_plan(K):
            src = pad_ref
            if r > 0:
                _shift_rows(sh_ref, pad_ref, r, CONV_PAD + S, True)
                src = sh_ref
            for l0 in range(0, ct, 128):
                lanes = slice(l0, l0 + 128)
                for c in range(S // ch):
                    acc = None
                    for off, wrow in taps:
                        st = CONV_PAD + c * ch - off
                        term = w_ref[wrow:wrow + 1, lanes] * src[st:st + ch, lanes]
                        acc = term if acc is None else acc + term
                    rows = slice(c * ch, (c + 1) * ch)
                    if r == 0:
                        o_ref[rows, lanes] = acc + b_ref[:, lanes]
                    else:
                        o_ref[rows, lanes] += acc

    in_specs = [pl.BlockSpec((S, ct), lambda j, bb: (bb, off_v // ct + j))]
    args = [z]
    if glu:
        in_specs.append(pl.BlockSpec((S, ct), lambda j, bb: (bb, off_g // ct + j)))
        args.append(z)
    in_specs += [pl.BlockSpec((K, ct), lambda j, bb: (0, j)), pl.BlockSpec((1, ct), lambda j, bb: (0, j))]
    args += [w, b]
    return pl.pallas_call(body, grid=(C // ct, T // S), in_specs=in_specs,
                          out_specs=pl.BlockSpec((S, ct), lambda j, bb: (bb, j)),
                          out_shape=jax.ShapeDtypeStruct((T, C), F32),
                          scratch_shapes=[pltpu.VMEM((CONV_PAD + S, ct), F32), pltpu.VMEM((CONV_PAD + S, ct), F32)],
                          compiler_params=_cp("parallel", "parallel"), name=name)(*args)


def _conv_bwd(z, w, dy, dz, *, S, off_v, off_g, name, ct=256):
    T = z.shape[0]
    K, C = w.shape
    KP = -(-K // SUBLANES) * SUBLANES
    ct = min(ct, C)
    ch = min(CONV_CHUNK, S)
    glu = off_g is not None
    total = S + CONV_PAD

    def body(*refs):
        if glu:
            (v_ref, g_ref, w_ref, dy_ref, dz_in, dz_ref, dw_ref, db_ref,
             pad_ref, sh_ref, padb_ref, shb_ref, du_ref, stage_v, stage_g, sem) = refs
        else:
            (v_ref, w_ref, dy_ref, dz_in, dz_ref, dw_ref, db_ref,
             pad_ref, sh_ref, padb_ref, shb_ref, du_ref, stage_v, sem) = refs
        del dz_in
        j, bb = pl.program_id(0), pl.program_id(1)
        pad_ref[0:CONV_PAD, :] = jnp.zeros((CONV_PAD, ct), F32)
        if glu:
            pad_ref[CONV_PAD:total, :] = v_ref[...].astype(F32) * _sig(g_ref[...].astype(F32))
        else:
            pad_ref[CONV_PAD:total, :] = v_ref[...].astype(F32)
        padb_ref[0:S, :] = dy_ref[...]
        padb_ref[S:total, :] = jnp.zeros((CONV_PAD, ct), F32)

        @pl.when(bb == 0)
        def _():
            dw_ref[...] = jnp.zeros((KP, ct), F32)
            db_ref[...] = jnp.zeros((1, ct), F32)

        for r, taps in _tap_plan(K):
            u_src, d_src = pad_ref, padb_ref
            if r > 0:
                _shift_rows(sh_ref, pad_ref, r, total, True)
                _shift_rows(shb_ref, padb_ref, r, total, False)
                u_src, d_src = sh_ref, shb_ref
            for l0 in range(0, ct, 128):
                lanes = slice(l0, l0 + 128)
                for c in range(S // ch):
                    acc = None
                    for off, wrow in taps:
                        st = c * ch + off
                        term = w_ref[wrow:wrow + 1, lanes] * d_src[st:st + ch, lanes]
                        acc = term if acc is None else acc + term
                    rows = slice(c * ch, (c + 1) * ch)
                    if r == 0:
                        du_ref[rows, lanes] = acc
                    else:
                        du_ref[rows, lanes] += acc
                for off, wrow in taps:
                    acc = None
                    for c in range(S // ch):
                        st = CONV_PAD + c * ch - off
                        prod = padb_ref[c * ch:(c + 1) * ch, lanes] * u_src[st:st + ch, lanes]
                        acc = prod if acc is None else acc + prod
                    dw_ref[wrow:wrow + 1, lanes] += jnp.sum(acc, axis=0, keepdims=True)
        db_ref[...] += jnp.sum(dy_ref[...], axis=0, keepdims=True)
        puts = [_columns_copy(stage_v, dz_ref, bb * S, S, off_v + j * ct, sem.at[0])]
        if glu:
            puts.append(_columns_copy(stage_g, dz_ref, bb * S, S, off_g + j * ct, sem.at[1]))

        @pl.when(jnp.logical_or(j > 0, bb > 0))
        def _():
            for cp in puts:
                cp.wait()

        for l0 in range(0, ct, 128):
            lanes = slice(l0, l0 + 128)
            for c in range(S // ch):
                rows = slice(c * ch, (c + 1) * ch)
                du = du_ref[rows, lanes]
                if glu:
                    sg = _sig(g_ref[rows, lanes].astype(F32))
                    stage_v[rows, lanes] = (du * sg).astype(BF16)
                    stage_g[rows, lanes] = (du * v_ref[rows, lanes].astype(F32) * sg * (1.0 - sg)).astype(BF16)
                else:
                    stage_v[rows, lanes] = du.astype(BF16)
        for cp in puts:
            cp.start()

        @pl.when(jnp.logical_and(j == C // ct - 1, bb == T // S - 1))
        def _():
            for cp in puts:
                cp.wait()

    blk = lambda off: pl.BlockSpec((S, ct), lambda j, bb: (bb, off // ct + j))
    in_specs = [blk(off_v)]
    args = [z]
    if glu:
        in_specs.append(blk(off_g))
        args.append(z)
    in_specs += [pl.BlockSpec((K, ct), lambda j, bb: (0, j)), blk(0), _ANY]
    args += [w, dy, dz]
    out_shape = (jax.ShapeDtypeStruct(dz.shape, dz.dtype), jax.ShapeDtypeStruct((KP, C), F32),
                 jax.ShapeDtypeStruct((1, C), F32))
    out_specs = (_ANY, pl.BlockSpec((KP, ct), lambda j, bb: (0, j)), pl.BlockSpec((1, ct), lambda j, bb: (0, j)))
    padded = pltpu.VMEM((total, ct), F32)
    stage = pltpu.VMEM((S, ct), BF16)
    return pl.pallas_call(body, grid=(C // ct, T // S), in_specs=in_specs, out_specs=out_specs, out_shape=out_shape,
                          scratch_shapes=[padded, padded, padded, padded, pltpu.VMEM((S, ct), F32), stage]
                          + ([stage] if glu else []) + [pltpu.SemaphoreType.DMA((2,))],
                          input_output_aliases={len(args) - 1: 0},
                          compiler_params=_cp("arbitrary", "arbitrary"), name=name)(*args)


def _softplus_neg(lam):
    return jnp.maximum(-lam, 0.0) + jnp.log1p(jnp.exp(-jnp.abs(lam)))


def _neg_expm1(x):
    u = jnp.exp(x)
    um1 = u - 1.0
    lg = jnp.log(u)
    safe = jnp.where(lg == 0.0, 1.0, lg)
    em1 = jnp.where(um1 == 0.0, x, jnp.where(um1 == -1.0, -1.0, um1 * x / safe))
    return -em1


def _rglru_fwd(v, z, wa, wx, ba, bx, lam, *, S, off_gate, name, tt=512):
    T, C = v.shape
    G, gw, _ = wa.shape
    tt = min(tt, S)
    nt = S // tt
    nlb = gw // 128
    assert gw % 128 == 0 and off_gate % 128 == 0

    def body(*refs):
        v_ref = refs[0]
        gate_refs = refs[1:1 + nlb]
        wa_ref, wx_ref, ba_ref, bx_ref, lam_ref, r_ref, i_ref, h_ref, hb_ref, a_s, b_s, carry_ref = refs[1 + nlb:]
        t = pl.program_id(2)

        @pl.when(t == 0)
        def _():
            carry_ref[...] = jnp.zeros((SUBLANES, gw), F32)

        vv = v_ref[...]
        vb = vv.astype(BF16)
        r = _sig(jnp.dot(vb, wa_ref[...], preferred_element_type=F32) + ba_ref[...])
        ig = _sig(jnp.dot(vb, wx_ref[...], preferred_element_type=F32) + bx_ref[...])
        log_a = -LRU_C * r * _softplus_neg(lam_ref[...])
        mult = jnp.sqrt(_neg_expm1(2.0 * log_a))
        start = jnp.logical_and(t == 0, lax.broadcasted_iota(jnp.int32, (tt, gw), 0) == 0)
        mult = jnp.where(start, 1.0, mult)
        r_ref[...] = r
        i_ref[...] = ig
        a_s[...] = jnp.exp(log_a)
        b_s[...] = mult * ig * vv
        row = lax.broadcasted_iota(jnp.int32, (SUBLANES, gw), 0)

        def step(i, carry):
            st = pl.multiple_of(i * SUBLANES, SUBLANES)
            A = a_s[pl.ds(st, SUBLANES), :]
            B = b_s[pl.ds(st, SUBLANES), :]
            for d in (1, 2, 4):
                m = row >= d
                Bn = jnp.where(m, A * pltpu.roll(B, d, 0) + B, B)
                A = jnp.where(m, A * pltpu.roll(A, d, 0), A)
                B = Bn
            h = B + A * carry
            h_ref[pl.ds(st, SUBLANES), :] = h
            return jnp.broadcast_to(h[SUBLANES - 1:SUBLANES, :], (SUBLANES, gw))

        carry_ref[...] = lax.fori_loop(0, tt // SUBLANES, step, carry_ref[...], unroll=2)
        for k in range(nlb):
            lanes = slice(k * 128, (k + 1) * 128)
            hb_ref[:, lanes] = (h_ref[:, lanes] * _gelu(gate_refs[k][...].astype(F32))).astype(BF16)

    blk = pl.BlockSpec((tt, gw), lambda g, bb, t: (bb * nt + t, g))
    gates = [pl.BlockSpec((tt, 128), lambda g, bb, t, k=k: (bb * nt + t, off_gate // 128 + g * nlb + k)) for k in range(nlb)]
    wsp = pl.BlockSpec((None, gw, gw), lambda g, bb, t: (g, 0, 0))
    vec = pl.BlockSpec((1, gw), lambda g, bb, t: (0, g))
    o = jax.ShapeDtypeStruct((T, C), F32)
    return pl.pallas_call(body, grid=(G, T // S, nt), in_specs=[blk] + gates + [wsp, wsp, vec, vec, vec],
                          out_specs=(blk, blk, blk, blk), out_shape=(o, o, o, jax.ShapeDtypeStruct((T, C), BF16)),
                          scratch_shapes=[pltpu.VMEM((tt, gw), F32), pltpu.VMEM((tt, gw), F32),
                                          pltpu.VMEM((SUBLANES, gw), F32)],
                          compiler_params=_cp("parallel", "parallel", "arbitrary"),
                          name=name)(v, *([z] * nlb), wa, wx, ba, bx, lam)


def _rglru_bwd(dhb, h, z, r, ig, v, wa, wx, lam, dz, *, S, off_gate, name, tt=512):
    T, C = v.shape
    G, gw, _ = wa.shape
    hd = gw // HEADS_PER_GROUP
    tt = min(tt, S)
    nt = S // tt
    n_seq = T // S
    n_tiles = tt // SUBLANES
    nlb = gw // 128

    def body(*refs):
        dhb_ref, h_ref, hp_ref = refs[0:3]
        gate_refs = refs[3:3 + nlb]
        (r_ref, i_ref, v_ref, wa_ref, wx_ref, lam_ref, dz_in, dv_ref, dz_ref, dlam_ref, dba_ref, dbx_ref, dwa_ref, dwx_ref,
         dh_s, a_s, G_s, da_s, carry_ref, stage, acc_a, acc_x, sem) = refs[3 + nlb:]
        del dz_in
        g, bb, t = pl.program_id(0), pl.program_id(1), pl.program_id(2)
        tb = nt - 1 - t
        first = jnp.logical_and(bb == 0, t == 0)

        @pl.when(t == 0)
        def _():
            carry_ref[...] = jnp.zeros((SUBLANES, gw), F32)

        @pl.when(first)
        def _():
            dlam_ref[...] = jnp.zeros((SUBLANES, gw), F32)
            dba_ref[...] = jnp.zeros((SUBLANES, gw), F32)
            dbx_ref[...] = jnp.zeros((SUBLANES, gw), F32)
            acc_a[...] = jnp.zeros((gw, gw), F32)
            acc_x[...] = jnp.zeros((gw, gw), F32)

        rr, ig, vv = r_ref[...], i_ref[...], v_ref[...]
        lam_v = lam_ref[...]
        sp = _softplus_neg(lam_v)
        log_a = -LRU_C * rr * sp
        a = jnp.exp(log_a)
        a_s[...] = a
        for k in range(nlb):
            lanes = slice(k * 128, (k + 1) * 128)
            gate = gate_refs[k][...].astype(F32)
            dhb = dhb_ref[:, lanes]
            dh_s[:, lanes] = dhb * _gelu(gate)
            stage[:, lanes] = (dhb * h_ref[:, lanes] * _gelu_grad(gate)).astype(BF16)
        put = _columns_copy(stage, dz_ref, (bb * nt + tb) * tt, tt, off_gate + g * gw, sem)
        put.start()
        h_before = jnp.where(tb > 0, jnp.broadcast_to(hp_ref[SUBLANES - 1:SUBLANES, :], (SUBLANES, gw)), 0.0)
        row = lax.broadcasted_iota(jnp.int32, (SUBLANES, gw), 0)

        def step(k, qcarry):
            i = n_tiles - 1 - k
            st = pl.multiple_of(i * SUBLANES, SUBLANES)
            stp = pl.multiple_of(jnp.maximum(i - 1, 0) * SUBLANES, SUBLANES)
            A = a_s[pl.ds(st, SUBLANES), :]
            hv = h_ref[pl.ds(st, SUBLANES), :]
            hprev_tile = h_ref[pl.ds(stp, SUBLANES), :]
            dh = dh_s[pl.ds(st, SUBLANES), :]
            Aq = A
            Bq = A * dh
            for d in (1, 2, 4):
                m = row < SUBLANES - d
                Bn = jnp.where(m, Aq * pltpu.roll(Bq, SUBLANES - d, 0) + Bq, Bq)
                Aq = jnp.where(m, Aq * pltpu.roll(Aq, SUBLANES - d, 0), Aq)
                Bq = Bn
            q = Bq + Aq * qcarry
            qnext = jnp.where(row == SUBLANES - 1, qcarry, pltpu.roll(q, SUBLANES - 1, 0))
            gq = dh + qnext
            hlast = jnp.where(i > 0, jnp.broadcast_to(hprev_tile[SUBLANES - 1:SUBLANES, :], (SUBLANES, gw)), h_before)
            hprev = jnp.where(row == 0, hlast, pltpu.roll(hv, 1, 0))
            G_s[pl.ds(st, SUBLANES), :] = gq
            da_s[pl.ds(st, SUBLANES), :] = gq * hprev
            return jnp.broadcast_to(q[0:1, :], (SUBLANES, gw))

        carry_ref[...] = lax.fori_loop(0, n_tiles, step, carry_ref[...], unroll=2)

        Gv = G_s[...]
        mult_raw = jnp.sqrt(_neg_expm1(2.0 * log_a))
        start = jnp.logical_and(tb == 0, lax.broadcasted_iota(jnp.int32, (tt, gw), 0) == 0)
        mult = jnp.where(start, 1.0, mult_raw)
        dmult = jnp.where(start, 0.0, Gv * ig * vv)
        di = Gv * mult * vv
        dla = da_s[...] * a - dmult * (a * a) / jnp.where(start, 1.0, mult_raw)
        dpr = dla * (-LRU_C) * sp * rr * (1.0 - rr)
        dpi = di * ig * (1.0 - ig)
        dlam_ref[...] += jnp.broadcast_to(jnp.sum(dla * (-LRU_C) * rr, axis=0, keepdims=True) * (-_sig(-lam_v)), (SUBLANES, gw))
        dba_ref[...] += jnp.broadcast_to(jnp.sum(dpr, axis=0, keepdims=True), (SUBLANES, gw))
        dbx_ref[...] += jnp.broadcast_to(jnp.sum(dpi, axis=0, keepdims=True), (SUBLANES, gw))
        dprb, dpib = dpr.astype(BF16), dpi.astype(BF16)
        nt_dims = (((1,), (1,)), ((), ()))
        dv_ref[...] = (Gv * mult * ig
                       + lax.dot_general(dprb, wa_ref[...], nt_dims, preferred_element_type=F32)
                       + lax.dot_general(dpib, wx_ref[...], nt_dims, preferred_element_type=F32))
        tn_dims = (((0,), (0,)), ((), ()))
        vb = vv.astype(BF16)
        acc_a[...] += lax.dot_general(vb, dprb, tn_dims, preferred_element_type=F32)
        acc_x[...] += lax.dot_general(vb, dpib, tn_dims, preferred_element_type=F32)

        @pl.when(jnp.logical_and(bb == n_seq - 1, t == nt - 1))
        def _():
            for hh in range(HEADS_PER_GROUP):
                dwa_ref[hh] = acc_a[hh * hd:(hh + 1) * hd, hh * hd:(hh + 1) * hd]
                dwx_ref[hh] = acc_x[hh * hd:(hh + 1) * hd, hh * hd:(hh + 1) * hd]

        put.wait()

    rowblk = lambda g, bb, t: (bb * nt + nt - 1 - t, g)
    blk = pl.BlockSpec((tt, gw), rowblk)
    before = pl.BlockSpec((SUBLANES, gw), lambda g, bb, t: (jnp.maximum((bb * nt + nt - 1 - t) * n_tiles - 1, 0), g))
    gates = [pl.BlockSpec((tt, 128), lambda g, bb, t, k=k: (bb * nt + nt - 1 - t, off_gate // 128 + g * nlb + k))
             for k in range(nlb)]
    wsp = pl.BlockSpec((None, gw, gw), lambda g, bb, t: (g, 0, 0))
    vec = pl.BlockSpec((1, gw), lambda g, bb, t: (0, g))
    acc8 = pl.BlockSpec((SUBLANES, gw), lambda g, bb, t: (0, g))
    heads = pl.BlockSpec((HEADS_PER_GROUP, hd, hd), lambda g, bb, t: (g, 0, 0))
    o8 = jax.ShapeDtypeStruct((SUBLANES, C), F32)
    ow = jax.ShapeDtypeStruct((G * HEADS_PER_GROUP, hd, hd), F32)
    scr = pltpu.VMEM((tt, gw), F32)
    return pl.pallas_call(
        body, grid=(G, n_seq, nt),
        in_specs=[blk, blk, before] + gates + [blk, blk, blk, wsp, wsp, vec, _ANY],
        out_specs=(blk, _ANY, acc8, acc8, acc8, heads, heads),
        out_shape=(jax.ShapeDtypeStruct((T, C), F32), jax.ShapeDtypeStruct(dz.shape, dz.dtype), o8, o8, o8, ow, ow),
        scratch_shapes=[scr, scr, scr, scr, pltpu.VMEM((SUBLANES, gw), F32), pltpu.VMEM((tt, gw), BF16),
                        pltpu.VMEM((gw, gw), F32), pltpu.VMEM((gw, gw), F32), pltpu.SemaphoreType.DMA],
        input_output_aliases={9 + nlb: 1},
        compiler_params=_cp("parallel", "arbitrary", "arbitrary"),
        name=name)(dhb, h, h, *([z] * nlb), r, ig, v, wa, wx, lam, dz)


def _group_weights(w):
    H, hd, _ = w.shape
    G = H // HEADS_PER_GROUP
    eye = jnp.eye(HEADS_PER_GROUP, dtype=w.dtype)
    wg = jnp.einsum("ghij,hk->ghikj", w.reshape(G, HEADS_PER_GROUP, hd, hd), eye)
    return wg.reshape(G, HEADS_PER_GROUP * hd, HEADS_PER_GROUP * hd).astype(BF16)


def _layer_fwd(x, p, *, S, fetch=None):
    D = x.shape[1]
    Dc = p["conv_a_b"].shape[1]
    Dr = p["conv_b_b"].shape[1]
    offs = dict(va=0, ga=Dc, xb=2 * Dc, gb=2 * Dc + Dr, sa=2 * Dc + 2 * Dr, sb=2 * Dc + 2 * Dr + D)
    h = _rms_fwd(x, p["g_mix"], name="rms_mix_fwd")
    if fetch is not None:
        fetch("in", h)
    z = _mm_nn(h, p["w_in"], tm=2048, tn=p["w_in"].shape[2], bias=p["b_in"], out_dtype=BF16, a_resident=True,
               name="mm_in_fwd")
    if fetch is not None:
        fetch("mix", z)
    u1 = _conv_fwd(z, p["conv_a_w"], p["conv_a_b"], S=S, off_v=offs["va"], off_g=offs["ga"], name="conv_a_fwd")
    u2 = _ln_silu_fwd(u1, p["ln_g"], p["ln_b"], name="ln_silu_fwd")
    ya = _mm_nn(u2, p["w_a_out"], tm=1024, tn=1024, name="mm_a_out_fwd")
    v0 = _conv_fwd(z, p["conv_b_w"], p["conv_b_b"], S=S, off_v=offs["xb"], off_g=None, name="conv_b_fwd")
    r, ig, hs, hb = _rglru_fwd(v0, z, p["wg_a"], p["wg_x"], p["b_rg_a"], p["b_rg_x"], p["lam"], S=S, off_gate=offs["gb"],
                               name="rglru_fwd")
    yb = _mm_nn(hb, p["w_b_out"], tm=1024, tn=1024, name="mm_b_out_fwd")
    m = _merge_fwd(z, ya, yb, off_sa=offs["sa"], off_sb=offs["sb"], name="merge_fwd")
    x_mid = _mm_nn(m, p["w_o"], tm=1024, tn=1024, resid=x, name="mm_o_fwd")
    h2 = _rms_fwd(x_mid, p["g_mlp"], name="rms_mlp_fwd")
    if fetch is not None:
        fetch("mlp", h2)
    f = _mm_nn(h2, p["w_1"], tm=2048, tn=p["w_1"].shape[2], relu2=True, out_dtype=BF16, a_resident=True,
               name="mm_1_fwd")
    x_next = _mm_nn(f, p["w_2"], tm=512, tn=1024, resid=x_mid, name="mm_2_fwd")
    saved = dict(x=x, h=h, z=z, u1=u1, u2=u2, ya=ya, v0=v0, r=r, ig=ig, hs=hs, hb=hb, yb=yb, m=m,
                 x_mid=x_mid, h2=h2, f=f, offs=offs)
    return x_next, saved


def _layer_bwd_mlp(dx, p, sv, *, wdt, dep=None):
    dx, dxb = dx
    g = {}
    g["w_2"] = _mm_tn(sv["f"], dxb, tm=1024, tn=1024, tk=4096, out_dtype=wdt, name="mm_2_wgrad")
    dfp = _mm_nt(dxb, p["w_2"], tm=1024, tn=1024, tk=1024, mul_sqrt=sv["f"], out_dtype=BF16, dep=dep,
                 name="mm_2_dgrad")
    g["w_1"] = _mm_tn(sv["h2"], dfp, tm=1024, tn=512, tk=4096, out_blocks=p["w_1"].shape[0], out_dtype=wdt,
                      name="mm_1_wgrad")
    dh2 = _mm_nt(dfp, p["w_1"], tm=512, tn=1024, tk=512, whole_b=True, name="mm_1_dgrad")
    dx_mid, dx_mid_b, g["g_mlp"] = _rms_bwd(sv["x_mid"], p["g_mlp"], dh2, dx, name="rms_mlp_bwd")
    return (dx_mid, dx_mid_b), g


def _layer_bwd_mix(dx_mid, p, sv, *, S, wdt, dep=None, on_gate_grads=None, on_weight_grads=None):
    offs = sv["offs"]
    dx_mid, dx_mid_b = dx_mid
    g = {}
    g["w_o"] = _mm_tn(sv["m"], dx_mid_b, tm=1024, tn=1024, tk=4096, out_dtype=wdt, name="mm_o_wgrad")
    dm = _mm_nt(dx_mid_b, p["w_o"], tm=1024, tn=1024, tk=1024, dep=dep, name="mm_o_dgrad")
    dz = lax.empty(sv["z"].shape, BF16)
    dya, dyb, dz = _merge_bwd(sv["z"], sv["ya"], sv["yb"], dm, dz, off_sa=offs["sa"], off_sb=offs["sb"], name="merge_bwd")
    g["w_b_out"] = _mm_tn(sv["hb"], dyb, tm=768, tn=1024, tk=4096, out_dtype=wdt, name="mm_b_out_wgrad")
    dhb = _mm_nt(dyb, p["w_b_out"], tm=1024, tn=1536, tk=1024, name="mm_b_out_dgrad")
    dv0, dz, dlam, dba, dbx, g["w_rg_a"], g["w_rg_x"] = _rglru_bwd(
        dhb, sv["hs"], sv["z"], sv["r"], sv["ig"], sv["v0"], p["wg_a"], p["wg_x"], p["lam"], dz, S=S, off_gate=offs["gb"],
        name="rglru_bwd")
    g["lam"], g["b_rg_a"], g["b_rg_x"] = dlam[:1], dba[:1], dbx[:1]
    dep_gates = on_gate_grads(g) if on_gate_grads is not None else None
    dz, g["conv_b_w"], g["conv_b_b"] = _conv_bwd(sv["z"], p["conv_b_w"], dv0, dz, S=S, off_v=offs["xb"], off_g=None,
                                                 name="conv_b_bwd")
    g["w_a_out"] = _mm_tn(sv["u2"], dya, tm=1024, tn=1024, tk=4096, out_dtype=wdt, name="mm_a_out_wgrad")
    du2 = _mm_nt(dya, p["w_a_out"], tm=1024, tn=1024, tk=1024, dep=dep_gates, name="mm_a_out_dgrad")
    du1, g["ln_g"], g["ln_b"] = _ln_silu_bwd(sv["u1"], p["ln_g"], p["ln_b"], du2, name="ln_silu_bwd")
    dz, g["conv_a_w"], g["conv_a_b"] = _conv_bwd(sv["z"], p["conv_a_w"], du1, dz, S=S, off_v=offs["va"],
                                                 off_g=offs["ga"], name="conv_a_bwd")
    g["w_in"], db_in = _mm_tn(sv["h"], dz, tm=1024, tn=512, tk=4096, out_blocks=p["w_in"].shape[0], colsum=True,
                              out_dtype=wdt, name="mm_in_wgrad")
    g["b_in"] = db_in[:1]
    dep_in = on_weight_grads(g) if on_weight_grads is not None else None
    dh = _mm_nt(dz, p["w_in"], tm=256, tn=1024, tk=512, whole_b=True, dep=dep_in, name="mm_in_dgrad")
    dx_in, dx_in_b, g["g_mix"] = _rms_bwd(sv["x"], p["g_mix"], dh, dx_mid, name="rms_mix_bwd")
    return (dx_in, dx_in_b), g


def _layer_bwd(dx, p, sv, *, S, wdt=F32):
    dx_mid, g = _layer_bwd_mlp(dx, p, sv, wdt=wdt)
    dx_in, g2 = _layer_bwd_mix(dx_mid, p, sv, S=S, wdt=wdt)
    g.update(g2)
    return dx_in, g


def _local_step(x, tgt, layers, g_final, *, S, wdt=F32):
    saved = []
    for p in layers:
        x, sv = _layer_fwd(x, p, S=S)
        saved.append(sv)
    loss, dx, dxb, dg_final = _final_loss(x, g_final, tgt, name="final_loss")
    dx = (dx, dxb)
    grads = [None] * len(layers)
    for l in reversed(range(len(layers))):
        dx, grads[l] = _layer_bwd(dx, layers[l], saved[l], S=S, wdt=wdt)
    return loss, dx[0], grads, dg_final


_HBM = pl.BlockSpec(memory_space=pltpu.HBM)
_MESH = pl.DeviceIdType.MESH


_SEM = pl.BlockSpec(memory_space=pltpu.SEMAPHORE)
_ANY = pl.BlockSpec(memory_space=pl.ANY)
_FLIPS = [(dx, dy, dc) for dx in (0, 1) for dy in (0, 1) for dc in (0, 1)][1:]


def _place(shard, me_idx, dtype, *, name, dep=None):
    r, cc = shard.shape
    tr = 512 if r % 512 == 0 else r

    def body(me_ref, s_ref, *rest):
        del me_ref
        rest[-1][...] = s_ref[...].astype(dtype)

    in_specs, args = [pl.BlockSpec((tr, cc), lambda i, me: (i, 0))], [me_idx, shard]
    if dep is not None:
        in_specs.append(pl.BlockSpec(dep.shape, lambda i, me: (0, 0)))
        args.append(dep)
    grid_spec = pltpu.PrefetchScalarGridSpec(
        num_scalar_prefetch=1, grid=(r // tr,), in_specs=in_specs,
        out_specs=pl.BlockSpec((None, tr, cc), lambda i, me: (me[0], i, 0)))
    return pl.pallas_call(body, grid_spec=grid_spec, out_shape=jax.ShapeDtypeStruct((N_DEV, r, cc), dtype),
                          compiler_params=_cp("arbitrary"), name=name)(*args)


def _exchange_copies(srcs, lands, send_sems, recv_sems):
    x, y, c = lax.axis_index("x"), lax.axis_index("y"), lax.axis_index("c")
    me = 4 * x + 2 * y + c
    pairs = []
    for k, (dx, dy, dc) in enumerate(_FLIPS):
        peer = (1 - x if dx else x, 1 - y if dy else y, 1 - c if dc else c)
        pidx = 4 * peer[0] + 2 * peer[1] + peer[2]
        for a, land in enumerate(lands):
            src = land.at[me] if srcs is None else srcs[a].at[pidx]
            sem = k * len(lands) + a
            out = pltpu.make_async_remote_copy(src_ref=src, dst_ref=land.at[me], send_sem=send_sems.at[sem],
                                               recv_sem=recv_sems.at[sem], device_id=peer, device_id_type=_MESH)
            arrival = pltpu.make_async_remote_copy(src_ref=src, dst_ref=land.at[pidx], send_sem=send_sems.at[sem],
                                                   recv_sem=recv_sems.at[sem], device_id=peer, device_id_type=_MESH)
            pairs.append((out, arrival))
    return pairs


def _exchange_start(srcs, lands, *, name):
    n = len(lands)
    bufs = list(lands) if srcs is None else list(srcs) + list(lands)
    nb = len(bufs)

    def body(*refs):
        ins = refs[:nb]
        send_sems, recv_sems = refs[nb], refs[nb + 1]
        token = refs[-1]
        for out, _ in _exchange_copies(None if srcs is None else ins[:n], ins[nb - n:], send_sems, recv_sems):
            out.start()
        token[...] = jnp.zeros_like(token)

    sems = pltpu.SemaphoreType.DMA((len(_FLIPS) * n,))
    res = pl.pallas_call(
        body, name=name, in_specs=[_HBM] * nb,
        out_shape=(sems, sems, *[pltpu.HBM(b.shape, b.dtype) for b in bufs], jax.ShapeDtypeStruct((SUBLANES, 128), F32)),
        out_specs=(_SEM, _SEM, *[_HBM] * nb, pl.BlockSpec(memory_space=pltpu.VMEM)),
        input_output_aliases={i: 2 + i for i in range(nb)},
        compiler_params=pltpu.CompilerParams(has_side_effects=pltpu.SideEffectType.DATAFLOW_SIDE_EFFECTING),
    )(*[pltpu.with_memory_space_constraint(b, pltpu.HBM) for b in bufs])
    return res[0], res[1], list(res[2:2 + nb]), res[-1]


def _exchange_wait(send_sems, recv_sems, bufs, *, scatter, after, name):
    nb = len(bufs)
    n = nb // 2 if scatter else nb

    def body(*refs):
        ins = refs[:nb]
        for out, arrival in _exchange_copies(ins[:n] if scatter else None, ins[nb - n:], refs[nb], refs[nb + 1]):
            out.wait_send()
            arrival.wait_recv()

    extra = [] if after is None else [after]
    res = pl.pallas_call(
        body, name=name, in_specs=[_HBM] * nb + [_SEM, _SEM] + [_ANY] * len(extra),
        out_shape=tuple(pltpu.HBM(b.shape, b.dtype) for b in bufs), out_specs=tuple([_HBM] * nb),
        input_output_aliases={i: i for i in range(nb)},
        compiler_params=pltpu.CompilerParams(has_side_effects=pltpu.SideEffectType.DATAFLOW_SIDE_EFFECTING),
    )(*bufs, send_sems, recv_sems, *extra)
    return list(res)


def _adamw_math(w, g, m, v):
    m = ADAM_B1 * m + (1.0 - ADAM_B1) * g
    v = ADAM_B2 * v + (1.0 - ADAM_B2) * (g * g)
    m_hat = m / (1.0 - ADAM_B1 ** ADAM_STEP)
    v_hat = v / (1.0 - ADAM_B2 ** ADAM_STEP)
    delta = -ADAM_LR * (m_hat / (jnp.sqrt(v_hat) + ADAM_EPS) + ADAM_WD * w)
    return delta, m, v


def _adamw(w, m, v, parts, prev, layer, me_idx, *, name, own=None):
    L, r, cc = w.shape
    P = parts.shape[0]
    tr = 512 if r % 512 == 0 else r
    if prev is None:
        prev = tuple(lax.empty(w.shape, F32) for _ in range(4))

    def body(me_ref, w_ref, m_ref, v_ref, p_ref, *rest):
        g_ref, d_ref, nm_ref, nv_ref = rest[-4:]
        if own is None:
            g = p_ref[0].astype(F32)
            for q in range(1, P):
                g = g + p_ref[q].astype(F32)
        else:
            me = me_ref[0]
            g = rest[0][...].astype(F32)
            for q in range(P):
                g = g + jnp.where(q == me, 0.0, p_ref[q].astype(F32))
        d, nm, nv = _adamw_math(w_ref[...], g, m_ref[...], v_ref[...])
        g_ref[...] = g
        d_ref[...] = d
        nm_ref[...] = nm
        nv_ref[...] = nv

    blk = pl.BlockSpec((None, tr, cc), lambda i, me: (layer, i, 0))
    in_specs = [blk, blk, blk, pl.BlockSpec((P, tr, cc), lambda i, me: (0, i, 0))]
    args = [me_idx, w, m, v, parts]
    if own is not None:
        in_specs.append(pl.BlockSpec((None, tr, cc), lambda i, me: (me[0], i, 0)))
        args.append(own)
    first_prev = len(args)
    in_specs += [_ANY] * 4
    args += list(prev)
    grid_spec = pltpu.PrefetchScalarGridSpec(num_scalar_prefetch=1, grid=(r // tr,), in_specs=in_specs,
                                             out_specs=(blk, blk, blk, blk))
    o = jax.ShapeDtypeStruct(w.shape, F32)
    return pl.pallas_call(body, grid_spec=grid_spec, out_shape=(o, o, o, o),
                          input_output_aliases={first_prev + i: i for i in range(4)},
                          compiler_params=_cp("parallel"), name=name)(*args)


_SHARDED = ("w_in", "conv_a_w", "w_a_out", "conv_b_w", "w_b_out", "w_o", "w_1", "w_2")
_COL_SHARDED = ("w_in", "conv_a_w", "conv_b_w", "w_1")
_REPLICATED = ("g_mix", "b_in", "conv_a_b", "ln_g", "ln_b", "conv_b_b", "w_rg_a", "b_rg_a", "w_rg_x", "b_rg_x", "lam",
               "g_mlp")
_WEIGHTS = ("g_mix", "w_in", "b_in", "conv_a_w", "conv_a_b", "ln_g", "ln_b", "w_a_out", "conv_b_w", "conv_b_b", "w_rg_a",
            "b_rg_a", "w_rg_x", "b_rg_x", "lam", "w_b_out", "w_o", "g_mlp", "w_1", "w_2", "g_final")
_LANES = 128


def _cols_from_blocks(b):
    nb, K, n = b.shape
    return b.transpose(1, 0, 2).reshape(K, nb * n)


def _blocks_from_cols(w, K):
    n = w.shape[1] // N_DEV
    return w[:K].reshape(K, N_DEV, n).transpose(1, 0, 2)


def kernel(x, g_mix, w_in, b_in, conv_a_w, conv_a_b, ln_g, ln_b, w_a_out, conv_b_w, conv_b_b, w_rg_a, b_rg_a, w_rg_x, b_rg_x, lam, w_b_out, w_o, g_mlp, w_1, w_2, g_final, loss_target, m_g_mix, m_w_in, m_b_in, m_conv_a_w, m_conv_a_b, m_ln_g, m_ln_b, m_w_a_out, m_conv_b_w, m_conv_b_b, m_w_rg_a, m_b_rg_a, m_w_rg_x, m_b_rg_x, m_lam, m_w_b_out, m_w_o, m_g_mlp, m_w_1, m_w_2, m_g_final, v_g_mix, v_w_in, v_b_in, v_conv_a_w, v_conv_a_b, v_ln_g, v_ln_b, v_w_a_out, v_conv_b_w, v_conv_b_b, v_w_rg_a, v_b_rg_a, v_w_rg_x, v_b_rg_x, v_lam, v_w_b_out, v_w_o, v_g_mlp, v_w_1, v_w_2, v_g_final):
    W = dict(g_mix=g_mix, w_in=w_in, b_in=b_in, conv_a_w=conv_a_w, conv_a_b=conv_a_b, ln_g=ln_g, ln_b=ln_b,
             w_a_out=w_a_out, conv_b_w=conv_b_w, conv_b_b=conv_b_b, w_rg_a=w_rg_a, b_rg_a=b_rg_a, w_rg_x=w_rg_x,
             b_rg_x=b_rg_x, lam=lam, w_b_out=w_b_out, w_o=w_o, g_mlp=g_mlp, w_1=w_1, w_2=w_2, g_final=g_final)
    M = dict(g_mix=m_g_mix, w_in=m_w_in, b_in=m_b_in, conv_a_w=m_conv_a_w, conv_a_b=m_conv_a_b, ln_g=m_ln_g, ln_b=m_ln_b,
             w_a_out=m_w_a_out, conv_b_w=m_conv_b_w, conv_b_b=m_conv_b_b, w_rg_a=m_w_rg_a, b_rg_a=m_b_rg_a,
             w_rg_x=m_w_rg_x, b_rg_x=m_b_rg_x, lam=m_lam, w_b_out=m_w_b_out, w_o=m_w_o, g_mlp=m_g_mlp, w_1=m_w_1,
             w_2=m_w_2, g_final=m_g_final)
    V = dict(g_mix=v_g_mix, w_in=v_w_in, b_in=v_b_in, conv_a_w=v_conv_a_w, conv_a_b=v_conv_a_b, ln_g=v_ln_g, ln_b=v_ln_b,
             w_a_out=v_w_a_out, conv_b_w=v_conv_b_w, conv_b_b=v_conv_b_b, w_rg_a=v_w_rg_a, b_rg_a=v_b_rg_a,
             w_rg_x=v_w_rg_x, b_rg_x=v_b_rg_x, lam=v_lam, w_b_out=v_w_b_out, w_o=v_w_o, g_mlp=v_g_mlp, w_1=v_w_1,
             w_2=v_w_2, g_final=v_g_final)
    NB, S, D = x.shape
    L = g_mix.shape[0]
    hd = w_rg_a.shape[-1]
    me_idx = (4 * lax.axis_index("x") + 2 * lax.axis_index("y") + lax.axis_index("c")).astype(jnp.int32).reshape(1)

    stages = (("in", ("w_in",)), ("mix", ("conv_a_w", "w_a_out", "conv_b_w", "w_b_out", "w_o")), ("mlp", ("w_1", "w_2")))
    gathers = {}
    started = jnp.zeros((), F32)
    first = None
    for l in range(L):
        for stage, names in stages:
            lands = [_place(W[k][l], me_idx, F32 if k.startswith("conv") else BF16, dep=first, name="place_" + k)
                     for k in names]
            send_sems, recv_sems, bufs, token = _exchange_start(None, lands, name=f"weights_start_{stage}_{l}")
            gathers[l, stage] = (names, send_sems, recv_sems, bufs)
            started = started + token[0, 0]
            if first is None:
                first = token

    xt = x.reshape(NB * S, D)
    layers, saved = [], []
    for l in range(L):
        p = {k: W[k][l][None] for k in ("g_mix", "b_in", "conv_a_b", "ln_g", "ln_b", "conv_b_b", "b_rg_a", "b_rg_x",
                                         "lam", "g_mlp")}
        if l == 0:
            p["g_mix"] = p["g_mix"] + started
        p["wg_a"] = _group_weights(w_rg_a[l])
        p["wg_x"] = _group_weights(w_rg_x[l])

        def fetch(stage, after, l=l, p=p):
            names, send_sems, recv_sems, bufs = gathers[l, stage]
            bufs = _exchange_wait(send_sems, recv_sems, bufs, scatter=False, after=after,
                                  name=f"weights_wait_{stage}_{l}")
            for k, full in zip(names, bufs):
                if k in ("w_in", "w_1"):
                    p[k] = full
                elif k in _COL_SHARDED:
                    p[k] = _cols_from_blocks(full)
                else:
                    p[k] = full.reshape(-1, full.shape[-1])

        layers.append(p)
        xt, sv = _layer_fwd(xt, p, S=S, fetch=fetch)
        saved.append(sv)
    loss, dx, dxb, dg_final = _final_loss(xt, g_final[None], loss_target.reshape(NB * S, D), name="final_loss")
    dx = (dx, dxb)

    results = {k: None for k in _SHARDED}

    def scatter_start(l, names, g, tag):
        srcs = []
        for k in names:
            shard_shape = W[k].shape[1:]
            if k in ("w_in", "w_1"):
                srcs.append(g[k])
            elif k in _COL_SHARDED:
                srcs.append(_blocks_from_cols(g[k], shard_shape[0]).astype(BF16))
            else:
                srcs.append(g[k].reshape((N_DEV,) + shard_shape))
        lands = [lax.empty(s.shape, BF16) for s in srcs]
        send_sems, recv_sems, bufs, token = _exchange_start(srcs, lands, name=f"grads_start_{tag}_{l}")
        return (names, send_sems, recv_sems, bufs, tag), token

    def scatter_finish(l, flight, after):
        names, send_sems, recv_sems, bufs, tag = flight
        bufs = _exchange_wait(send_sems, recv_sems, bufs, scatter=True, after=after, name=f"grads_wait_{tag}_{l}")
        n = len(names)
        for k, own, land in zip(names, bufs[:n], bufs[n:]):
            results[k] = _adamw(W[k], M[k], V[k], land, results[k], l, me_idx, own=own, name="adamw_" + k)

    gate_names = ("w_rg_a", "w_rg_x")
    gate_results = {k: None for k in gate_names}

    def small_start(l, g):
        lands = [_place(g[k].reshape(-1, hd), me_idx, F32, name="place_gate_grad") for k in gate_names]
        send_sems, recv_sems, bufs, token = _exchange_start(None, lands, name=f"small_start_{l}")
        return (send_sems, recv_sems, bufs), token

    def small_finish(l, flight, after):
        send_sems, recv_sems, bufs = flight
        bufs = _exchange_wait(send_sems, recv_sems, bufs, scatter=False, after=after, name=f"small_wait_{l}")
        for k, g_all in zip(gate_names, bufs):
            gate_results[k] = _adamw(W[k].reshape(L, -1, hd), M[k].reshape(L, -1, hd), V[k].reshape(L, -1, hd), g_all,
                                     gate_results[k], l, me_idx, name="adamw_gate")

    grads = [None] * L
    in_flight = []
    dep = None
    for l in reversed(range(L)):
        dx_mid, g = _layer_bwd_mlp(dx, layers[l], saved[l], wdt=BF16, dep=dep)
        f_mlp, dep = scatter_start(l, ("w_2", "w_1"), g, "mlp")
        flights = [f_mlp]

        small = []

        def on_gate_grads(g_part, l=l, small=small):
            f_small, token = small_start(l, g_part)
            small.append(f_small)
            return token

        def on_weight_grads(g_part, l=l, flights=flights):
            f_mix, token = scatter_start(l, ("w_o", "w_b_out", "w_a_out", "conv_a_w", "conv_b_w", "w_in"), g_part, "mix")
            flights.append(f_mix)
            return token

        dx, g_mix_part = _layer_bwd_mix(dx_mid, layers[l], saved[l], S=S, wdt=BF16, dep=dep,
                                        on_gate_grads=on_gate_grads, on_weight_grads=on_weight_grads)
        dep = None
        g.update(g_mix_part)
        grads[l] = g
        for l_prev, fs, f_sm in in_flight:
            for flight in fs:
                scatter_finish(l_prev, flight, dx[0])
            small_finish(l_prev, f_sm, dx[0])
        in_flight = [(l, flights, small[0])]

    vec_names = tuple(k for k in _REPLICATED if k not in gate_names)
    n_vec = sum(W[k].shape[1] for k in vec_names)

    def vec_pack(rows, final, last):
        tail = jnp.concatenate([final.reshape(1, -1), jnp.broadcast_to(last.reshape(1, 1), (1, _LANES))], axis=1)
        tail = jnp.pad(tail, ((0, SUBLANES - L - 1), (0, n_vec - tail.shape[1])))
        return jnp.concatenate([rows, tail], axis=0)

    g_rows = jnp.concatenate([jnp.concatenate([grads[l][k] for k in vec_names], axis=1) for l in range(L)], axis=0)
    land = _place(vec_pack(g_rows, dg_final, loss[0, :1]), me_idx, F32, name="place_vectors")
    vec_send_sems, vec_recv_sems, vec_bufs, _ = _exchange_start(None, [land], name="vectors_start")
    for l_prev, fs, f_sm in in_flight:
        after = dx[0] if L == 1 else results["w_in"][0]
        for flight in fs:
            scatter_finish(l_prev, flight, after)
            after = results[flight[0][-1]][0]
        small_finish(l_prev, f_sm, after)
    out_g, out_d, out_m, out_v = {}, {}, {}, {}
    for k in _SHARDED:
        out_g[k], out_d[k], out_m[k], out_v[k] = results[k]
    for k in gate_names:
        out_g[k], out_d[k], out_m[k], out_v[k] = (a.reshape(W[k].shape) for a in gate_results[k])

    zero = jnp.zeros((1,), F32)
    (g_all,) = _exchange_wait(vec_send_sems, vec_recv_sems, vec_bufs, scatter=False, after=results["w_in"][0],
                              name="vectors_wait")
    vec_out = _adamw(vec_pack(jnp.concatenate([W[k] for k in vec_names], axis=1), g_final, zero)[None],
                     vec_pack(jnp.concatenate([M[k] for k in vec_names], axis=1), m_g_final, zero)[None],
                     vec_pack(jnp.concatenate([V[k] for k in vec_names], axis=1), v_g_final, zero)[None],
                     g_all, None, 0, me_idx, name="adamw_vectors")
    vec_out = [a[0] for a in vec_out]
    for res, arr in zip((out_g, out_d, out_m, out_v), vec_out):
        off = 0
        for k in vec_names:
            res[k] = arr[:L, off:off + W[k].shape[1]]
            off += W[k].shape[1]
        res["g_final"] = arr[L, :g_final.size]
    loss_out = vec_out[0][L, g_final.size]

    return (loss_out, dx[0].reshape(NB, S, D), *[out_g[k] for k in _WEIGHTS], *[out_d[k] for k in _WEIGHTS],
            *[out_m[k] for k in _WEIGHTS], *[out_v[k] for k in _WEIGHTS])
```

```python
import functools

import jax
import jax.numpy as jnp
from jax import lax
from jax.experimental import pallas as pl
from jax.experimental.pallas import tpu as pltpu

F32 = jnp.float32
BF16 = jnp.bfloat16

EPS = 1e-6
LRU_C = 8.0
N_RNN_HEADS = 16
HEADS_PER_GROUP = 4
N_DEV = 8
ADAM_LR, ADAM_B1, ADAM_B2, ADAM_EPS, ADAM_WD, ADAM_STEP = 0.001, 0.9, 0.999, 1e-08, 0.01, 10

VMEM_LIMIT_BYTES = 48 * 1024 * 1024
CONV_PAD = 32
CONV_CHUNK = 128
SUBLANES = 8


def _cp(*sem):
    return pltpu.CompilerParams(dimension_semantics=sem, vmem_limit_bytes=VMEM_LIMIT_BYTES)


def _sig(x):
    return 1.0 / (1.0 + jnp.exp(-x))


def _gelu(x):
    c = 0.7978845608028654
    return 0.5 * x * (1.0 + jnp.tanh(c * (x + 0.044715 * x * x * x)))


def _gelu_grad(x):
    c = 0.7978845608028654
    th = jnp.tanh(c * (x + 0.044715 * x * x * x))
    return 0.5 * (1.0 + th) + 0.5 * x * (1.0 - th * th) * c * (1.0 + 3.0 * 0.044715 * x * x)


def _mm_nn(a, b, *, tm, tn, name, bias=None, resid=None, relu2=False, out_dtype=F32, a_resident=False):
    M, K = a.shape
    blocked = b.ndim == 3
    N = b.shape[0] * b.shape[2] if blocked else b.shape[1]
    tm = min(tm, M)
    tn = min(tn, N)
    if blocked:
        assert tn == b.shape[2]
    n_extra = (bias is not None) + (resid is not None)

    def body(*refs):
        acc = jnp.dot(refs[0][...].astype(BF16), refs[1][...].astype(BF16), preferred_element_type=F32)
        k = 2
        if bias is not None:
            acc = acc + refs[k][...]
            k += 1
        if resid is not None:
            acc = acc + refs[k][...]
            k += 1
        if relu2:
            p = jnp.maximum(acc, 0.0)
            acc = p * p
        refs[k][...] = acc.astype(out_dtype)

    def spec(shape, index):
        return pl.BlockSpec(shape, (lambda i, j: index(j, i)) if a_resident else index)

    in_specs = [spec((tm, K), lambda j, i: (i, 0))]
    if blocked:
        in_specs.append(spec((None, K, tn), lambda j, i: (j, 0, 0)))
    else:
        in_specs.append(spec((K, tn), lambda j, i: (0, j)))
    args = [a, b]
    if bias is not None:
        in_specs.append(spec((1, tn), lambda j, i: (0, j)))
        args.append(bias)
    if resid is not None:
        in_specs.append(spec((tm, tn), lambda j, i: (i, j)))
        args.append(resid)
    out_specs = spec((tm, tn), lambda j, i: (i, j))
    out_shape = jax.ShapeDtypeStruct((M, N), out_dtype)
    del n_extra
    grid = (M // tm, N // tn) if a_resident else (N // tn, M // tm)
    return pl.pallas_call(body, grid=grid, in_specs=in_specs, out_specs=out_specs,
                          out_shape=out_shape, compiler_params=_cp("parallel", "parallel"), name=name)(*args)


def _mm_nt(a, b, *, tm, tn, tk, name, mul_sqrt=None, resid=None, out_dtype=F32, dep=None, whole_b=False):
    M, N = a.shape
    blocked = b.ndim == 3
    Kout = b.shape[1] if blocked else b.shape[0]
    tm = min(tm, M)
    tn = min(tn, Kout)
    tk = N if whole_b else (b.shape[2] if blocked else min(tk, N))
    nk = N // tk
    nt_dims = (((1,), (1,)), ((), ()))

    def body(*refs):
        acc_ref = refs[-1]
        kk = pl.program_id(2)
        if blocked and whole_b:
            n = b.shape[2]
            part = None
            for jb in range(b.shape[0]):
                pj = lax.dot_general(refs[0][:, jb * n:(jb + 1) * n].astype(BF16), refs[1][jb].astype(BF16), nt_dims,
                                     preferred_element_type=F32)
                part = pj if part is None else part + pj
        else:
            part = lax.dot_general(refs[0][...].astype(BF16), refs[1][...].astype(BF16), nt_dims,
                                   preferred_element_type=F32)

        def finish(acc):
            k = 2
            if mul_sqrt is not None:
                acc = acc * (2.0 * jnp.sqrt(refs[k][...].astype(F32)))
                k += 1
            if resid is not None:
                acc = acc + refs[k][...]
                k += 1
            if dep is not None:
                k += 1
            refs[k][...] = acc.astype(out_dtype)

        if nk == 1:
            finish(part)
            return

        @pl.when(kk == 0)
        def _():
            acc_ref[...] = part

        @pl.when(kk > 0)
        def _():
            acc_ref[...] += part

        @pl.when(kk == nk - 1)
        def _():
            finish(acc_ref[...])

    in_specs = [pl.BlockSpec((tm, tk), lambda i, j, k: (i, k))]
    if blocked and whole_b:
        in_specs.append(pl.BlockSpec((b.shape[0], tn, b.shape[2]), lambda i, j, k: (0, j, 0)))
    elif blocked:
        in_specs.append(pl.BlockSpec((None, tn, tk), lambda i, j, k: (k, j, 0)))
    else:
        in_specs.append(pl.BlockSpec((tn, tk), lambda i, j, k: (j, k)))
    args = [a, b]
    for extra in (mul_sqrt, resid):
        if extra is not None:
            in_specs.append(pl.BlockSpec((tm, tn), lambda i, j, k: (i, j)))
            args.append(extra)
    if dep is not None:
        in_specs.append(pl.BlockSpec(dep.shape, lambda i, j, k: (0, 0)))
        args.append(dep)
    return pl.pallas_call(body, grid=(M // tm, Kout // tn, nk), in_specs=in_specs,
                          out_specs=pl.BlockSpec((tm, tn), lambda i, j, k: (i, j)),
                          out_shape=jax.ShapeDtypeStruct((M, Kout), out_dtype),
                          scratch_shapes=[pltpu.VMEM((tm, tn), F32)] if nk > 1 else [],
                          compiler_params=_cp("parallel", "parallel", "arbitrary"), name=name)(*args)


def _mm_tn(a, b, *, tm, tn, tk, name, out_blocks=None, colsum=False, out_dtype=F32):
    T, M = a.shape
    N = b.shape[1]
    tm = min(tm, M)
    tk = min(tk, T)
    if out_blocks is not None:
        tn = N // out_blocks
        tm = M
    tn = min(tn, N)
    nk = T // tk
    if colsum:
        assert tm == M

    def body(*refs):
        a_ref, b_ref, o_ref, acc_ref = refs[0], refs[1], refs[2], refs[-1]
        kk = pl.program_id(2)
        bv = b_ref[...]
        part = lax.dot_general(a_ref[...].astype(BF16), bv.astype(BF16),
                               (((0,), (0,)), ((), ())), preferred_element_type=F32)

        if colsum:
            csum = jnp.broadcast_to(jnp.sum(bv.astype(F32), axis=0, keepdims=True), (SUBLANES, tn))

        if nk == 1:
            o_ref[...] = part.astype(out_dtype)
            if colsum:
                refs[3][...] = csum
            return

        @pl.when(kk == 0)
        def _():
            acc_ref[...] = part
            if colsum:
                refs[3][...] = csum

        @pl.when(kk > 0)
        def _():
            acc_ref[...] += part
            if colsum:
                refs[3][...] += csum

        @pl.when(kk == nk - 1)
        def _():
            o_ref[...] = acc_ref[...].astype(out_dtype)

    in_specs = [pl.BlockSpec((tk, tm), lambda i, j, k: (k, i)), pl.BlockSpec((tk, tn), lambda i, j, k: (k, j))]
    if out_blocks is not None:
        o_shape = jax.ShapeDtypeStruct((out_blocks, M, tn), out_dtype)
        o_spec = pl.BlockSpec((None, M, tn), lambda i, j, k: (j, 0, 0))
    else:
        o_shape = jax.ShapeDtypeStruct((M, N), out_dtype)
        o_spec = pl.BlockSpec((tm, tn), lambda i, j, k: (i, j))
    if colsum:
        out_shape = (o_shape, jax.ShapeDtypeStruct((SUBLANES, N), F32))
        out_specs = (o_spec, pl.BlockSpec((SUBLANES, tn), lambda i, j, k: (0, j)))
    else:
        out_shape, out_specs = o_shape, o_spec
    return pl.pallas_call(body, grid=(M // tm, N // tn, nk), in_specs=in_specs, out_specs=out_specs,
                          out_shape=out_shape, scratch_shapes=[pltpu.VMEM((tm, tn), F32)] if nk > 1 else [],
                          compiler_params=_cp("parallel", "parallel", "arbitrary"), name=name)(a, b)


def _rms_fwd(x, g, *, name, tr=512):
    T, D = x.shape
    tr = min(tr, T)

    def body(x_ref, g_ref, h_ref):
        xv = x_ref[...]
        r = lax.rsqrt(jnp.mean(xv * xv, axis=-1, keepdims=True) + EPS)
        h_ref[...] = (xv * r * g_ref[...]).astype(BF16)

    return pl.pallas_call(body, grid=(T // tr,),
                          in_specs=[pl.BlockSpec((tr, D), lambda i: (i, 0)), pl.BlockSpec((1, D), lambda i: (0, 0))],
                          out_specs=pl.BlockSpec((tr, D), lambda i: (i, 0)),
                          out_shape=jax.ShapeDtypeStruct((T, D), BF16), compiler_params=_cp("parallel"), name=name)(x, g)


def _rms_bwd(x, g, dh, dres, *, name, tr=512):
    T, D = x.shape
    tr = min(tr, T)

    def body(x_ref, g_ref, dh_ref, dres_ref, dx_ref, dxb_ref, dg_ref):
        xv = x_ref[...]
        r = lax.rsqrt(jnp.mean(xv * xv, axis=-1, keepdims=True) + EPS)
        n = xv * r
        dh = dh_ref[...]
        dn = dh * g_ref[...]
        dx = dres_ref[...] + r * (dn - n * jnp.mean(dn * n, axis=-1, keepdims=True))
        dx_ref[...] = dx
        dxb_ref[...] = dx.astype(BF16)
        part = jnp.sum(dh * n, axis=0, keepdims=True)

        @pl.when(pl.program_id(0) == 0)
        def _():
            dg_ref[...] = part

        @pl.when(pl.program_id(0) > 0)
        def _():
            dg_ref[...] += part

    row = pl.BlockSpec((tr, D), lambda i: (i, 0))
    vec = pl.BlockSpec((1, D), lambda i: (0, 0))
    return pl.pallas_call(body, grid=(T // tr,), in_specs=[row, vec, row, row], out_specs=(row, row, vec),
                          out_shape=(jax.ShapeDtypeStruct((T, D), F32), jax.ShapeDtypeStruct((T, D), BF16),
                                     jax.ShapeDtypeStruct((1, D), F32)),
                          compiler_params=_cp("arbitrary"), name=name)(x, g, dh, dres)


def _final_loss(x, g, tgt, *, name, tr=512):
    T, D = x.shape
    tr = min(tr, T)

    def body(x_ref, g_ref, t_ref, loss_ref, dx_ref, dxb_ref, dg_ref):
        xv = x_ref[...]
        gv = g_ref[...]
        r = lax.rsqrt(jnp.mean(xv * xv, axis=-1, keepdims=True) + EPS)
        n = xv * r
        e = n * gv - t_ref[...]
        lpart = 0.5 * jnp.sum(jnp.mean(e * e, axis=-1, keepdims=True), axis=0, keepdims=True)
        dy = e * (1.0 / D)
        dn = dy * gv
        dx = r * (dn - n * jnp.mean(dn * n, axis=-1, keepdims=True))
        dx_ref[...] = dx
        dxb_ref[...] = dx.astype(BF16)
        gpart = jnp.sum(dy * n, axis=0, keepdims=True)

        @pl.when(pl.program_id(0) == 0)
        def _():
            dg_ref[...] = gpart
            loss_ref[...] = jnp.broadcast_to(lpart, (1, 128))

        @pl.when(pl.program_id(0) > 0)
        def _():
            dg_ref[...] += gpart
            loss_ref[...] += jnp.broadcast_to(lpart, (1, 128))

    row = pl.BlockSpec((tr, D), lambda i: (i, 0))
    vec = pl.BlockSpec((1, D), lambda i: (0, 0))
    return pl.pallas_call(body, grid=(T // tr,), in_specs=[row, vec, row],
                          out_specs=(pl.BlockSpec((1, 128), lambda i: (0, 0)), row, row, vec),
                          out_shape=(jax.ShapeDtypeStruct((1, 128), F32), jax.ShapeDtypeStruct((T, D), F32),
                                     jax.ShapeDtypeStruct((T, D), BF16), jax.ShapeDtypeStruct((1, D), F32)),
                          compiler_params=_cp("arbitrary"), name=name)(x, g, tgt)


def _ln_silu_fwd(u, g, b, *, name, tr=512):
    T, C = u.shape
    tr = min(tr, T)

    def body(u_ref, g_ref, b_ref, o_ref):
        uv = u_ref[...]
        mu = jnp.mean(uv, axis=-1, keepdims=True)
        xc = uv - mu
        r = lax.rsqrt(jnp.mean(xc * xc, axis=-1, keepdims=True) + EPS)
        y = xc * r * g_ref[...] + b_ref[...]
        o_ref[...] = (y * _sig(y)).astype(BF16)

    row = pl.BlockSpec((tr, C), lambda i: (i, 0))
    vec = pl.BlockSpec((1, C), lambda i: (0, 0))
    return pl.pallas_call(body, grid=(T // tr,), in_specs=[row, vec, vec], out_specs=row,
                          out_shape=jax.ShapeDtypeStruct((T, C), BF16), compiler_params=_cp("parallel"),
                          name=name)(u, g, b)


def _ln_silu_bwd(u, g, b, do, *, name, tr=512):
    T, C = u.shape
    tr = min(tr, T)

    def body(u_ref, g_ref, b_ref, do_ref, du_ref, dg_ref, db_ref):
        uv = u_ref[...]
        gv = g_ref[...]
        mu = jnp.mean(uv, axis=-1, keepdims=True)
        xc = uv - mu
        r = lax.rsqrt(jnp.mean(xc * xc, axis=-1, keepdims=True) + EPS)
        n = xc * r
        y = n * gv + b_ref[...]
        s = _sig(y)
        dy = do_ref[...] * (s * (1.0 + y * (1.0 - s)))
        dn = dy * gv
        du_ref[...] = r * (dn - jnp.mean(dn, axis=-1, keepdims=True) - n * jnp.mean(dn * n, axis=-1, keepdims=True))
        gpart = jnp.sum(dy * n, axis=0, keepdims=True)
        bpart = jnp.sum(dy, axis=0, keepdims=True)

        @pl.when(pl.program_id(0) == 0)
        def _():
            dg_ref[...] = gpart
            db_ref[...] = bpart

        @pl.when(pl.program_id(0) > 0)
        def _():
            dg_ref[...] += gpart
            db_ref[...] += bpart

    row = pl.BlockSpec((tr, C), lambda i: (i, 0))
    vec = pl.BlockSpec((1, C), lambda i: (0, 0))
    return pl.pallas_call(body, grid=(T // tr,), in_specs=[row, vec, vec, row], out_specs=(row, vec, vec),
                          out_shape=(jax.ShapeDtypeStruct((T, C), F32), jax.ShapeDtypeStruct((1, C), F32),
                                     jax.ShapeDtypeStruct((1, C), F32)),
                          compiler_params=_cp("arbitrary"), name=name)(u, g, b, do)


def _merge_fwd(z, ya, yb, *, off_sa, off_sb, name, tr=512):
    T, D = ya.shape
    tr = min(tr, T)
    assert off_sa % D == 0 and off_sb % D == 0

    def body(sa_ref, sb_ref, ya_ref, yb_ref, m_ref):
        m_ref[...] = (_sig(sa_ref[...].astype(F32)) * ya_ref[...]
                      + _sig(sb_ref[...].astype(F32)) * yb_ref[...]).astype(BF16)

    row = pl.BlockSpec((tr, D), lambda i: (i, 0))
    return pl.pallas_call(body, grid=(T // tr,),
                          in_specs=[pl.BlockSpec((tr, D), lambda i: (i, off_sa // D)),
                                    pl.BlockSpec((tr, D), lambda i: (i, off_sb // D)), row, row],
                          out_specs=row, out_shape=jax.ShapeDtypeStruct((T, D), BF16),
                          compiler_params=_cp("parallel"), name=name)(z, z, ya, yb)


def _columns_copy(stage_ref, dz_ref, row0, rows, col0, sem):
    dst = dz_ref.at[pl.ds(pl.multiple_of(row0, SUBLANES), rows),
                    pl.ds(pl.multiple_of(col0, 128), stage_ref.shape[1])]
    return pltpu.make_async_copy(stage_ref, dst, sem)


def _put_columns(stage_ref, dz_ref, row0, rows, col0, sem):
    cp = _columns_copy(stage_ref, dz_ref, row0, rows, col0, sem)
    cp.start()
    cp.wait()


def _merge_bwd(z, ya, yb, dm, dz, *, off_sa, off_sb, name, tr=512):
    T, D = ya.shape
    tr = min(tr, T)
    assert off_sb == off_sa + D

    def body(sa_ref, sb_ref, ya_ref, yb_ref, dm_ref, dz_in, dya_ref, dyb_ref, dz_ref, stage, sem):
        del dz_in
        dm = dm_ref[...]
        ga = _sig(sa_ref[...].astype(F32))
        gb = _sig(sb_ref[...].astype(F32))
        dya_ref[...] = (dm * ga).astype(BF16)
        dyb_ref[...] = (dm * gb).astype(BF16)
        stage[:, 0:D] = (dm * ya_ref[...] * ga * (1.0 - ga)).astype(BF16)
        stage[:, D:2 * D] = (dm * yb_ref[...] * gb * (1.0 - gb)).astype(BF16)
        _put_columns(stage, dz_ref, pl.program_id(0) * tr, tr, off_sa, sem)

    row = pl.BlockSpec((tr, D), lambda i: (i, 0))
    o = jax.ShapeDtypeStruct((T, D), BF16)
    return pl.pallas_call(body, grid=(T // tr,),
                          in_specs=[pl.BlockSpec((tr, D), lambda i: (i, off_sa // D)),
                                    pl.BlockSpec((tr, D), lambda i: (i, off_sb // D)), row, row, row, _ANY],
                          out_specs=(row, row, _ANY), out_shape=(o, o, jax.ShapeDtypeStruct(dz.shape, dz.dtype)),
                          scratch_shapes=[pltpu.VMEM((tr, 2 * D), BF16), pltpu.SemaphoreType.DMA],
                          input_output_aliases={5: 2},
                          compiler_params=_cp("parallel"), name=name)(z, z, ya, yb, dm, dz)


def _shift_rows(dst_ref, src_ref, r, total, back):
    for c0 in range(0, total - SUBLANES, CONV_CHUNK):
        n = min(CONV_CHUNK, total - SUBLANES - c0)
        if back:
            dst_ref[SUBLANES + c0:SUBLANES + c0 + n, :] = src_ref[SUBLANES + c0 - r:SUBLANES + c0 - r + n, :]
        else:
            dst_ref[c0:c0 + n, :] = src_ref[c0 + r:c0 + r + n, :]


def _tap_plan(K):
    if K <= SUBLANES:
        return [(0, [(s, K - 1 - s) for s in range(K)])]
    return [(r, [(SUBLANES * q, K - 1 - (SUBLANES * q + r)) for q in range(-(-K // SUBLANES)) if SUBLANES * q + r < K])
            for r in range(SUBLANES)]


def _conv_fwd(z, w, b, *, S, off_v, off_g, name, ct=256):
    T = z.shape[0]
    K, C = w.shape
    ct = min(ct, C)
    ch = min(CONV_CHUNK, S)
    glu = off_g is not None
    assert off_v % ct == 0 and (not glu or off_g % ct == 0)
    assert SUBLANES * ((K - 1) // SUBLANES) <= CONV_PAD - SUBLANES

    def body(*refs):
        if glu:
            v_ref, g_ref, w_ref, b_ref, o_ref, pad_ref, sh_ref = refs
        else:
            v_ref, w_ref, b_ref, o_ref, pad_ref, sh_ref = refs
        pad_ref[0:CONV_PAD, :] = jnp.zeros((CONV_PAD, ct), F32)
        if glu:
            pad_ref[CONV_PAD:CONV_PAD + S, :] = v_ref[...].astype(F32) * _sig(g_ref[...].astype(F32))
        else:
            pad_ref[CONV_PAD:CONV_PAD + S, :] = v_ref[...].astype(F32)
        for r, taps in _tap_plan(K):
            src = pad_ref
            if r > 0:
                _shift_rows(sh_ref, pad_ref, r, CONV_PAD + S, True)
                src = sh_ref
            for l0 in range(0, ct, 128):
                lanes = slice(l0, l0 + 128)
                for c in range(S // ch):
                    acc = None
                    for off, wrow in taps:
                        st = CONV_PAD + c * ch - off
                        term = w_ref[wrow:wrow + 1, lanes] * src[st:st + ch, lanes]
                        acc = term if acc is None else acc + term
                    rows = slice(c * ch, (c + 1) * ch)
                    if r == 0:
                        o_ref[rows, lanes] = acc + b_ref[:, lanes]
                    else:
                        o_ref[rows, lanes] += acc

    in_specs = [pl.BlockSpec((S, ct), lambda j, bb: (bb, off_v // ct + j))]
    args = [z]
    if glu:
        in_specs.append(pl.BlockSpec((S, ct), lambda j, bb: (bb, off_g // ct + j)))
        args.append(z)
    in_specs += [pl.BlockSpec((K, ct), lambda j, bb: (0, j)), pl.BlockSpec((1, ct), lambda j, bb: (0, j))]
    args += [w, b]
    return pl.pallas_call(body, grid=(C // ct, T // S), in_specs=in_specs,
                          out_specs=pl.BlockSpec((S, ct), lambda j, bb: (bb, j)),
                          out_shape=jax.ShapeDtypeStruct((T, C), F32),
                          scratch_shapes=[pltpu.VMEM((CONV_PAD + S, ct), F32), pltpu.VMEM((CONV_PAD + S, ct), F32)],
                          compiler_params=_cp("parallel", "parallel"), name=name)(*args)


def _conv_bwd(z, w, dy, dz, *, S, off_v, off_g, name, ct=256):
    T = z.shape[0]
    K, C = w.shape
    KP = -(-K // SUBLANES) * SUBLANES
    ct = min(ct, C)
    ch = min(CONV_CHUNK, S)
    glu = off_g is not None
    total = S + CONV_PAD

    def body(*refs):
        if glu:
            (v_ref, g_ref, w_ref, dy_ref, dz_in, dz_ref, dw_ref, db_ref,
             pad_ref, sh_ref, padb_ref, shb_ref, du_ref, stage_v, stage_g, sem) = refs
        else:
            (v_ref, w_ref, dy_ref, dz_in, dz_ref, dw_ref, db_ref,
             pad_ref, sh_ref, padb_ref, shb_ref, du_ref, stage_v, sem) = refs
        del dz_in
        j, bb = pl.program_id(0), pl.program_id(1)
        pad_ref[0:CONV_PAD, :] = jnp.zeros((CONV_PAD, ct), F32)
        if glu:
            pad_ref[CONV_PAD:total, :] = v_ref[...].astype(F32) * _sig(g_ref[...].astype(F32))
        else:
            pad_ref[CONV_PAD:total, :] = v_ref[...].astype(F32)
        padb_ref[0:S, :] = dy_ref[...]
        padb_ref[S:total, :] = jnp.zeros((CONV_PAD, ct), F32)

        @pl.when(bb == 0)
        def _():
            dw_ref[...] = jnp.zeros((KP, ct), F32)
            db_ref[...] = jnp.zeros((1, ct), F32)

        for r, taps in _tap_plan(K):
            u_src, d_src = pad_ref, padb_ref
            if r > 0:
                _shift_rows(sh_ref, pad_ref, r, total, True)
                _shift_rows(shb_ref, padb_ref, r, total, False)
                u_src, d_src = sh_ref, shb_ref
            for l0 in range(0, ct, 128):
                lanes = slice(l0, l0 + 128)
                for c in range(S // ch):
                    acc = None
                    for off, wrow in taps:
                        st = c * ch + off
                        term = w_ref[wrow:wrow + 1, lanes] * d_src[st:st + ch, lanes]
                        acc = term if acc is None else acc + term
                    rows = slice(c * ch, (c + 1) * ch)
                    if r == 0:
                        du_ref[rows, lanes] = acc
                    else:
                        du_ref[rows, lanes] += acc
                for off, wrow in taps:
                    acc = None
                    for c in range(S // ch):
                        st = CONV_PAD + c * ch - off
                        prod = padb_ref[c * ch:(c + 1) * ch, lanes] * u_src[st:st + ch, lanes]
                        acc = prod if acc is None else acc + prod
                    dw_ref[wrow:wrow + 1, lanes] += jnp.sum(acc, axis=0, keepdims=True)
        db_ref[...] += jnp.sum(dy_ref[...], axis=0, keepdims=True)
        for l0 in range(0, ct, 128):
            lanes = slice(l0, l0 + 128)
            for c in range(S // ch):
                rows = slice(c * ch, (c + 1) * ch)
                du = du_ref[rows, lanes]
                if glu:
                    sg = _sig(g_ref[rows, lanes].astype(F32))
                    stage_v[rows, lanes] = (du * sg).astype(BF16)
                    stage_g[rows, lanes] = (du * v_ref[rows, lanes].astype(F32) * sg * (1.0 - sg)).astype(BF16)
                else:
                    stage_v[rows, lanes] = du.astype(BF16)
        _put_columns(stage_v, dz_ref, bb * S, S, off_v + j * ct, sem)
        if glu:
            _put_columns(stage_g, dz_ref, bb * S, S, off_g + j * ct, sem)

    blk = lambda off: pl.BlockSpec((S, ct), lambda j, bb: (bb, off // ct + j))
    in_specs = [blk(off_v)]
    args = [z]
    if glu:
        in_specs.append(blk(off_g))
        args.append(z)
    in_specs += [pl.BlockSpec((K, ct), lambda j, bb: (0, j)), blk(0), _ANY]
    args += [w, dy, dz]
    out_shape = (jax.ShapeDtypeStruct(dz.shape, dz.dtype), jax.ShapeDtypeStruct((KP, C), F32),
                 jax.ShapeDtypeStruct((1, C), F32))
    out_specs = (_ANY, pl.BlockSpec((KP, ct), lambda j, bb: (0, j)), pl.BlockSpec((1, ct), lambda j, bb: (0, j)))
    padded = pltpu.VMEM((total, ct), F32)
    stage = pltpu.VMEM((S, ct), BF16)
    return pl.pallas_call(body, grid=(C // ct, T // S), in_specs=in_specs, out_specs=out_specs, out_shape=out_shape,
                          scratch_shapes=[padded, padded, padded, padded, pltpu.VMEM((S, ct), F32), stage]
                          + ([stage] if glu else []) + [pltpu.SemaphoreType.DMA],
                          input_output_aliases={len(args) - 1: 0},
                          compiler_params=_cp("parallel", "arbitrary"), name=name)(*args)


def _softplus_neg(lam):
    return jnp.maximum(-lam, 0.0) + jnp.log1p(jnp.exp(-jnp.abs(lam)))


def _neg_expm1(x):
    u = jnp.exp(x)
    um1 = u - 1.0
    lg = jnp.log(u)
    safe = jnp.where(lg == 0.0, 1.0, lg)
    em1 = jnp.where(um1 == 0.0, x, jnp.where(um1 == -1.0, -1.0, um1 * x / safe))
    return -em1


def _rglru_fwd(v, z, wa, wx, ba, bx, lam, *, S, off_gate, name, tt=512):
    T, C = v.shape
    G, gw, _ = wa.shape
    tt = min(tt, S)
    nt = S // tt
    nlb = gw // 128
    assert gw % 128 == 0 and off_gate % 128 == 0

    def body(*refs):
        v_ref = refs[0]
        gate_refs = refs[1:1 + nlb]
        wa_ref, wx_ref, ba_ref, bx_ref, lam_ref, r_ref, i_ref, h_ref, hb_ref, a_s, b_s, carry_ref = refs[1 + nlb:]
        t = pl.program_id(2)

        @pl.when(t == 0)
        def _():
            carry_ref[...] = jnp.zeros((SUBLANES, gw), F32)

        vv = v_ref[...]
        vb = vv.astype(BF16)
        r = _sig(jnp.dot(vb, wa_ref[...], preferred_element_type=F32) + ba_ref[...])
        ig = _sig(jnp.dot(vb, wx_ref[...], preferred_element_type=F32) + bx_ref[...])
        log_a = -LRU_C * r * _softplus_neg(lam_ref[...])
        mult = jnp.sqrt(_neg_expm1(2.0 * log_a))
        start = jnp.logical_and(t == 0, lax.broadcasted_iota(jnp.int32, (tt, gw), 0) == 0)
        mult = jnp.where(start, 1.0, mult)
        r_ref[...] = r
        i_ref[...] = ig
        a_s[...] = jnp.exp(log_a)
        b_s[...] = mult * ig * vv
        row = lax.broadcasted_iota(jnp.int32, (SUBLANES, gw), 0)

        def step(i, carry):
            st = pl.multiple_of(i * SUBLANES, SUBLANES)
            A = a_s[pl.ds(st, SUBLANES), :]
            B = b_s[pl.ds(st, SUBLANES), :]
            for d in (1, 2, 4):
                m = row >= d
                Bn = jnp.where(m, A * pltpu.roll(B, d, 0) + B, B)
                A = jnp.where(m, A * pltpu.roll(A, d, 0), A)
                B = Bn
            h = B + A * carry
            h_ref[pl.ds(st, SUBLANES), :] = h
            return jnp.broadcast_to(h[SUBLANES - 1:SUBLANES, :], (SUBLANES, gw))

        carry_ref[...] = lax.fori_loop(0, tt // SUBLANES, step, carry_ref[...], unroll=8)
        for k in range(nlb):
            lanes = slice(k * 128, (k + 1) * 128)
            hb_ref[:, lanes] = (h_ref[:, lanes] * _gelu(gate_refs[k][...].astype(F32))).astype(BF16)

    blk = pl.BlockSpec((tt, gw), lambda g, bb, t: (bb * nt + t, g))
    gates = [pl.BlockSpec((tt, 128), lambda g, bb, t, k=k: (bb * nt + t, off_gate // 128 + g * nlb + k)) for k in range(nlb)]
    wsp = pl.BlockSpec((None, gw, gw), lambda g, bb, t: (g, 0, 0))
    vec = pl.BlockSpec((1, gw), lambda g, bb, t: (0, g))
    o = jax.ShapeDtypeStruct((T, C), F32)
    return pl.pallas_call(body, grid=(G, T // S, nt), in_specs=[blk] + gates + [wsp, wsp, vec, vec, vec],
                          out_specs=(blk, blk, blk, blk), out_shape=(o, o, o, jax.ShapeDtypeStruct((T, C), BF16)),
                          scratch_shapes=[pltpu.VMEM((tt, gw), F32), pltpu.VMEM((tt, gw), F32),
                                          pltpu.VMEM((SUBLANES, gw), F32)],
                          compiler_params=_cp("parallel", "parallel", "arbitrary"),
                          name=name)(v, *([z] * nlb), wa, wx, ba, bx, lam)


def _rglru_bwd(dhb, h, z, r, ig, v, wa, wx, lam, dz, *, S, off_gate, name, tt=512):
    T, C = v.shape
    G, gw, _ = wa.shape
    hd = gw // HEADS_PER_GROUP
    tt = min(tt, S)
    nt = S // tt
    n_seq = T // S
    n_tiles = tt // SUBLANES
    nlb = gw // 128

    def body(*refs):
        dhb_ref, h_ref, hp_ref = refs[0:3]
        gate_refs = refs[3:3 + nlb]
        (r_ref, i_ref, v_ref, wa_ref, wx_ref, lam_ref, dz_in, dv_ref, dz_ref, dlam_ref, dba_ref, dbx_ref, dwa_ref, dwx_ref,
         dh_s, a_s, G_s, da_s, carry_ref, stage, acc_a, acc_x, sem) = refs[3 + nlb:]
        del dz_in
        g, bb, t = pl.program_id(0), pl.program_id(1), pl.program_id(2)
        tb = nt - 1 - t
        first = jnp.logical_and(bb == 0, t == 0)

        @pl.when(t == 0)
        def _():
            carry_ref[...] = jnp.zeros((SUBLANES, gw), F32)

        @pl.when(first)
        def _():
            dlam_ref[...] = jnp.zeros((SUBLANES, gw), F32)
            dba_ref[...] = jnp.zeros((SUBLANES, gw), F32)
            dbx_ref[...] = jnp.zeros((SUBLANES, gw), F32)
            acc_a[...] = jnp.zeros((gw, gw), F32)
            acc_x[...] = jnp.zeros((gw, gw), F32)

        rr, ig, vv = r_ref[...], i_ref[...], v_ref[...]
        lam_v = lam_ref[...]
        sp = _softplus_neg(lam_v)
        log_a = -LRU_C * rr * sp
        a = jnp.exp(log_a)
        a_s[...] = a
        for k in range(nlb):
            lanes = slice(k * 128, (k + 1) * 128)
            gate = gate_refs[k][...].astype(F32)
            dhb = dhb_ref[:, lanes]
            dh_s[:, lanes] = dhb * _gelu(gate)
            stage[:, lanes] = (dhb * h_ref[:, lanes] * _gelu_grad(gate)).astype(BF16)
        put = _columns_copy(stage, dz_ref, (bb * nt + tb) * tt, tt, off_gate + g * gw, sem)
        put.start()
        h_before = jnp.where(tb > 0, jnp.broadcast_to(hp_ref[SUBLANES - 1:SUBLANES, :], (SUBLANES, gw)), 0.0)
        row = lax.broadcasted_iota(jnp.int32, (SUBLANES, gw), 0)

        def step(k, qcarry):
            i = n_tiles - 1 - k
            st = pl.multiple_of(i * SUBLANES, SUBLANES)
            stp = pl.multiple_of(jnp.maximum(i - 1, 0) * SUBLANES, SUBLANES)
            A = a_s[pl.ds(st, SUBLANES), :]
            hv = h_ref[pl.ds(st, SUBLANES), :]
            hprev_tile = h_ref[pl.ds(stp, SUBLANES), :]
            dh = dh_s[pl.ds(st, SUBLANES), :]
            Aq = A
            Bq = A * dh
            for d in (1, 2, 4):
                m = row < SUBLANES - d
                Bn = jnp.where(m, Aq * pltpu.roll(Bq, SUBLANES - d, 0) + Bq, Bq)
                Aq = jnp.where(m, Aq * pltpu.roll(Aq, SUBLANES - d, 0), Aq)
                Bq = Bn
            q = Bq + Aq * qcarry
            qnext = jnp.where(row == SUBLANES - 1, qcarry, pltpu.roll(q, SUBLANES - 1, 0))
            gq = dh + qnext
            hlast = jnp.where(i > 0, jnp.broadcast_to(hprev_tile[SUBLANES - 1:SUBLANES, :], (SUBLANES, gw)), h_before)
            hprev = jnp.where(row == 0, hlast, pltpu.roll(hv, 1, 0))
            G_s[pl.ds(st, SUBLANES), :] = gq
            da_s[pl.ds(st, SUBLANES), :] = gq * hprev
            return jnp.broadcast_to(q[0:1, :], (SUBLANES, gw))

        carry_ref[...] = lax.fori_loop(0, n_tiles, step, carry_ref[...], unroll=8)

        Gv = G_s[...]
        mult_raw = jnp.sqrt(_neg_expm1(2.0 * log_a))
        start = jnp.logical_and(tb == 0, lax.broadcasted_iota(jnp.int32, (tt, gw), 0) == 0)
        mult = jnp.where(start, 1.0, mult_raw)
        dmult = jnp.where(start, 0.0, Gv * ig * vv)
        di = Gv * mult * vv
        dla = da_s[...] * a - dmult * (a * a) / jnp.where(start, 1.0, mult_raw)
        dpr = dla * (-LRU_C) * sp * rr * (1.0 - rr)
        dpi = di * ig * (1.0 - ig)
        dlam_ref[...] += jnp.broadcast_to(jnp.sum(dla * (-LRU_C) * rr, axis=0, keepdims=True) * (-_sig(-lam_v)), (SUBLANES, gw))
        dba_ref[...] += jnp.broadcast_to(jnp.sum(dpr, axis=0, keepdims=True), (SUBLANES, gw))
        dbx_ref[...] += jnp.broadcast_to(jnp.sum(dpi, axis=0, keepdims=True), (SUBLANES, gw))
        dprb, dpib = dpr.astype(BF16), dpi.astype(BF16)
        nt_dims = (((1,), (1,)), ((), ()))
        dv_ref[...] = (Gv * mult * ig
                       + lax.dot_general(dprb, wa_ref[...], nt_dims, preferred_element_type=F32)
                       + lax.dot_general(dpib, wx_ref[...], nt_dims, preferred_element_type=F32))
        tn_dims = (((0,), (0,)), ((), ()))
        vb = vv.astype(BF16)
        acc_a[...] += lax.dot_general(vb, dprb, tn_dims, preferred_element_type=F32)
        acc_x[...] += lax.dot_general(vb, dpib, tn_dims, preferred_element_type=F32)

        @pl.when(jnp.logical_and(bb == n_seq - 1, t == nt - 1))
        def _():
            for hh in range(HEADS_PER_GROUP):
                dwa_ref[hh] = acc_a[hh * hd:(hh + 1) * hd, hh * hd:(hh + 1) * hd]
                dwx_ref[hh] = acc_x[hh * hd:(hh + 1) * hd, hh * hd:(hh + 1) * hd]

        put.wait()

    rowblk = lambda g, bb, t: (bb * nt + nt - 1 - t, g)
    blk = pl.BlockSpec((tt, gw), rowblk)
    before = pl.BlockSpec((SUBLANES, gw), lambda g, bb, t: (jnp.maximum((bb * nt + nt - 1 - t) * n_tiles - 1, 0), g))
    gates = [pl.BlockSpec((tt, 128), lambda g, bb, t, k=k: (bb * nt + nt - 1 - t, off_gate // 128 + g * nlb + k))
             for k in range(nlb)]
    wsp = pl.BlockSpec((None, gw, gw), lambda g, bb, t: (g, 0, 0))
    vec = pl.BlockSpec((1, gw), lambda g, bb, t: (0, g))
    acc8 = pl.BlockSpec((SUBLANES, gw), lambda g, bb, t: (0, g))
    heads = pl.BlockSpec((HEADS_PER_GROUP, hd, hd), lambda g, bb, t: (g, 0, 0))
    o8 = jax.ShapeDtypeStruct((SUBLANES, C), F32)
    ow = jax.ShapeDtypeStruct((G * HEADS_PER_GROUP, hd, hd), F32)
    scr = pltpu.VMEM((tt, gw), F32)
    return pl.pallas_call(
        body, grid=(G, n_seq, nt),
        in_specs=[blk, blk, before] + gates + [blk, blk, blk, wsp, wsp, vec, _ANY],
        out_specs=(blk, _ANY, acc8, acc8, acc8, heads, heads),
        out_shape=(jax.ShapeDtypeStruct((T, C), F32), jax.ShapeDtypeStruct(dz.shape, dz.dtype), o8, o8, o8, ow, ow),
        scratch_shapes=[scr, scr, scr, scr, pltpu.VMEM((SUBLANES, gw), F32), pltpu.VMEM((tt, gw), BF16),
                        pltpu.VMEM((gw, gw), F32), pltpu.VMEM((gw, gw), F32), pltpu.SemaphoreType.DMA],
        input_output_aliases={9 + nlb: 1},
        compiler_params=_cp("parallel", "arbitrary", "arbitrary"),
        name=name)(dhb, h, h, *([z] * nlb), r, ig, v, wa, wx, lam, dz)


def _group_weights(w):
    H, hd, _ = w.shape
    G = H // HEADS_PER_GROUP
    eye = jnp.eye(HEADS_PER_GROUP, dtype=w.dtype)
    wg = jnp.einsum("ghij,hk->ghikj", w.reshape(G, HEADS_PER_GROUP, hd, hd), eye)
    return wg.reshape(G, HEADS_PER_GROUP * hd, HEADS_PER_GROUP * hd).astype(BF16)


def _layer_fwd(x, p, *, S, fetch=None):
    D = x.shape[1]
    Dc = p["conv_a_b"].shape[1]
    Dr = p["conv_b_b"].shape[1]
    offs = dict(va=0, ga=Dc, xb=2 * Dc, gb=2 * Dc + Dr, sa=2 * Dc + 2 * Dr, sb=2 * Dc + 2 * Dr + D)
    h = _rms_fwd(x, p["g_mix"], name="rms_mix_fwd")
    if fetch is not None:
        fetch("in", h)
    z = _mm_nn(h, p["w_in"], tm=2048, tn=p["w_in"].shape[2], bias=p["b_in"], out_dtype=BF16, a_resident=True,
               name="mm_in_fwd")
    if fetch is not None:
        fetch("mix", z)
    u1 = _conv_fwd(z, p["conv_a_w"], p["conv_a_b"], S=S, off_v=offs["va"], off_g=offs["ga"], name="conv_a_fwd")
    u2 = _ln_silu_fwd(u1, p["ln_g"], p["ln_b"], name="ln_silu_fwd")
    ya = _mm_nn(u2, p["w_a_out"], tm=1024, tn=1024, name="mm_a_out_fwd")
    v0 = _conv_fwd(z, p["conv_b_w"], p["conv_b_b"], S=S, off_v=offs["xb"], off_g=None, name="conv_b_fwd")
    r, ig, hs, hb = _rglru_fwd(v0, z, p["wg_a"], p["wg_x"], p["b_rg_a"], p["b_rg_x"], p["lam"], S=S, off_gate=offs["gb"],
                               name="rglru_fwd")
    yb = _mm_nn(hb, p["w_b_out"], tm=1024, tn=1024, name="mm_b_out_fwd")
    m = _merge_fwd(z, ya, yb, off_sa=offs["sa"], off_sb=offs["sb"], name="merge_fwd")
    x_mid = _mm_nn(m, p["w_o"], tm=1024, tn=1024, resid=x, name="mm_o_fwd")
    h2 = _rms_fwd(x_mid, p["g_mlp"], name="rms_mlp_fwd")
    if fetch is not None:
        fetch("mlp", h2)
    f = _mm_nn(h2, p["w_1"], tm=2048, tn=p["w_1"].shape[2], relu2=True, out_dtype=BF16, a_resident=True,
               name="mm_1_fwd")
    x_next = _mm_nn(f, p["w_2"], tm=512, tn=1024, resid=x_mid, name="mm_2_fwd")
    saved = dict(x=x, h=h, z=z, u1=u1, u2=u2, ya=ya, v0=v0, r=r, ig=ig, hs=hs, hb=hb, yb=yb, m=m,
                 x_mid=x_mid, h2=h2, f=f, offs=offs)
    return x_next, saved


def _layer_bwd_mlp(dx, p, sv, *, wdt, dep=None):
    dx, dxb = dx
    g = {}
    g["w_2"] = _mm_tn(sv["f"], dxb, tm=1024, tn=1024, tk=4096, out_dtype=wdt, name="mm_2_wgrad")
    dfp = _mm_nt(dxb, p["w_2"], tm=1024, tn=1024, tk=1024, mul_sqrt=sv["f"], out_dtype=BF16, dep=dep,
                 name="mm_2_dgrad")
    g["w_1"] = _mm_tn(sv["h2"], dfp, tm=1024, tn=512, tk=4096, out_blocks=p["w_1"].shape[0], out_dtype=wdt,
                      name="mm_1_wgrad")
    dh2 = _mm_nt(dfp, p["w_1"], tm=512, tn=1024, tk=512, whole_b=True, name="mm_1_dgrad")
    dx_mid, dx_mid_b, g["g_mlp"] = _rms_bwd(sv["x_mid"], p["g_mlp"], dh2, dx, name="rms_mlp_bwd")
    return (dx_mid, dx_mid_b), g


def _layer_bwd_mix(dx_mid, p, sv, *, S, wdt, dep=None, on_gate_grads=None, on_weight_grads=None):
    offs = sv["offs"]
    dx_mid, dx_mid_b = dx_mid
    g = {}
    g["w_o"] = _mm_tn(sv["m"], dx_mid_b, tm=1024, tn=1024, tk=4096, out_dtype=wdt, name="mm_o_wgrad")
    dm = _mm_nt(dx_mid_b, p["w_o"], tm=1024, tn=1024, tk=1024, dep=dep, name="mm_o_dgrad")
    dz = lax.empty(sv["z"].shape, BF16)
    dya, dyb, dz = _merge_bwd(sv["z"], sv["ya"], sv["yb"], dm, dz, off_sa=offs["sa"], off_sb=offs["sb"], name="merge_bwd")
    g["w_b_out"] = _mm_tn(sv["hb"], dyb, tm=768, tn=1024, tk=4096, out_dtype=wdt, name="mm_b_out_wgrad")
    dhb = _mm_nt(dyb, p["w_b_out"], tm=1024, tn=1536, tk=1024, name="mm_b_out_dgrad")
    dv0, dz, dlam, dba, dbx, g["w_rg_a"], g["w_rg_x"] = _rglru_bwd(
        dhb, sv["hs"], sv["z"], sv["r"], sv["ig"], sv["v0"], p["wg_a"], p["wg_x"], p["lam"], dz, S=S, off_gate=offs["gb"],
        name="rglru_bwd")
    g["lam"], g["b_rg_a"], g["b_rg_x"] = dlam[:1], dba[:1], dbx[:1]
    dep_gates = on_gate_grads(g) if on_gate_grads is not None else None
    dz, g["conv_b_w"], g["conv_b_b"] = _conv_bwd(sv["z"], p["conv_b_w"], dv0, dz, S=S, off_v=offs["xb"], off_g=None,
                                                 name="conv_b_bwd")
    g["w_a_out"] = _mm_tn(sv["u2"], dya, tm=1024, tn=1024, tk=4096, out_dtype=wdt, name="mm_a_out_wgrad")
    du2 = _mm_nt(dya, p["w_a_out"], tm=1024, tn=1024, tk=1024, dep=dep_gates, name="mm_a_out_dgrad")
    du1, g["ln_g"], g["ln_b"] = _ln_silu_bwd(sv["u1"], p["ln_g"], p["ln_b"], du2, name="ln_silu_bwd")
    dz, g["conv_a_w"], g["conv_a_b"] = _conv_bwd(sv["z"], p["conv_a_w"], du1, dz, S=S, off_v=offs["va"],
                                                 off_g=offs["ga"], name="conv_a_bwd")
    g["w_in"], db_in = _mm_tn(sv["h"], dz, tm=1024, tn=512, tk=4096, out_blocks=p["w_in"].shape[0], colsum=True,
                              out_dtype=wdt, name="mm_in_wgrad")
    g["b_in"] = db_in[:1]
    dep_in = on_weight_grads(g) if on_weight_grads is not None else None
    dh = _mm_nt(dz, p["w_in"], tm=256, tn=1024, tk=512, whole_b=True, dep=dep_in, name="mm_in_dgrad")
    dx_in, dx_in_b, g["g_mix"] = _rms_bwd(sv["x"], p["g_mix"], dh, dx_mid, name="rms_mix_bwd")
    return (dx_in, dx_in_b), g


def _layer_bwd(dx, p, sv, *, S, wdt=F32):
    dx_mid, g = _layer_bwd_mlp(dx, p, sv, wdt=wdt)
    dx_in, g2 = _layer_bwd_mix(dx_mid, p, sv, S=S, wdt=wdt)
    g.update(g2)
    return dx_in, g


def _local_step(x, tgt, layers, g_final, *, S, wdt=F32):
    saved = []
    for p in layers:
        x, sv = _layer_fwd(x, p, S=S)
        saved.append(sv)
    loss, dx, dxb, dg_final = _final_loss(x, g_final, tgt, name="final_loss")
    dx = (dx, dxb)
    grads = [None] * len(layers)
    for l in reversed(range(len(layers))):
        dx, grads[l] = _layer_bwd(dx, layers[l], saved[l], S=S, wdt=wdt)
    return loss, dx[0], grads, dg_final


_HBM = pl.BlockSpec(memory_space=pltpu.HBM)
_MESH = pl.DeviceIdType.MESH


_SEM = pl.BlockSpec(memory_space=pltpu.SEMAPHORE)
_ANY = pl.BlockSpec(memory_space=pl.ANY)
_FLIPS = [(dx, dy, dc) for dx in (0, 1) for dy in (0, 1) for dc in (0, 1)][1:]


def _place(shard, me_idx, dtype, *, name, dep=None):
    r, cc = shard.shape
    tr = 512 if r % 512 == 0 else r

    def body(me_ref, s_ref, *rest):
        del me_ref
        rest[-1][...] = s_ref[...].astype(dtype)

    in_specs, args = [pl.BlockSpec((tr, cc), lambda i, me: (i, 0))], [me_idx, shard]
    if dep is not None:
        in_specs.append(pl.BlockSpec(dep.shape, lambda i, me: (0, 0)))
        args.append(dep)
    grid_spec = pltpu.PrefetchScalarGridSpec(
        num_scalar_prefetch=1, grid=(r // tr,), in_specs=in_specs,
        out_specs=pl.BlockSpec((None, tr, cc), lambda i, me: (me[0], i, 0)))
    return pl.pallas_call(body, grid_spec=grid_spec, out_shape=jax.ShapeDtypeStruct((N_DEV, r, cc), dtype),
                          compiler_params=_cp("arbitrary"), name=name)(*args)


def _exchange_copies(srcs, lands, send_sems, recv_sems):
    x, y, c = lax.axis_index("x"), lax.axis_index("y"), lax.axis_index("c")
    me = 4 * x + 2 * y + c
    pairs = []
    for k, (dx, dy, dc) in enumerate(_FLIPS):
        peer = (1 - x if dx else x, 1 - y if dy else y, 1 - c if dc else c)
        pidx = 4 * peer[0] + 2 * peer[1] + peer[2]
        for a, land in enumerate(lands):
            src = land.at[me] if srcs is None else srcs[a].at[pidx]
            sem = k * len(lands) + a
            out = pltpu.make_async_remote_copy(src_ref=src, dst_ref=land.at[me], send_sem=send_sems.at[sem],
                                               recv_sem=recv_sems.at[sem], device_id=peer, device_id_type=_MESH)
            arrival = pltpu.make_async_remote_copy(src_ref=src, dst_ref=land.at[pidx], send_sem=send_sems.at[sem],
                                                   recv_sem=recv_sems.at[sem], device_id=peer, device_id_type=_MESH)
            pairs.append((out, arrival))
    return pairs


def _exchange_start(srcs, lands, *, name):
    n = len(lands)
    bufs = list(lands) if srcs is None else list(srcs) + list(lands)
    nb = len(bufs)

    def body(*refs):
        ins = refs[:nb]
        send_sems, recv_sems = refs[nb], refs[nb + 1]
        token = refs[-1]
        for out, _ in _exchange_copies(None if srcs is None else ins[:n], ins[nb - n:], send_sems, recv_sems):
            out.start()
        token[...] = jnp.zeros_like(token)

    sems = pltpu.SemaphoreType.DMA((len(_FLIPS) * n,))
    res = pl.pallas_call(
        body, name=name, in_specs=[_HBM] * nb,
        out_shape=(sems, sems, *[pltpu.HBM(b.shape, b.dtype) for b in bufs], jax.ShapeDtypeStruct((SUBLANES, 128), F32)),
        out_specs=(_SEM, _SEM, *[_HBM] * nb, pl.BlockSpec(memory_space=pltpu.VMEM)),
        input_output_aliases={i: 2 + i for i in range(nb)},
        compiler_params=pltpu.CompilerParams(has_side_effects=pltpu.SideEffectType.DATAFLOW_SIDE_EFFECTING),
    )(*[pltpu.with_memory_space_constraint(b, pltpu.HBM) for b in bufs])
    return res[0], res[1], list(res[2:2 + nb]), res[-1]


def _exchange_wait(send_sems, recv_sems, bufs, *, scatter, after, name):
    nb = len(bufs)
    n = nb // 2 if scatter else nb

    def body(*refs):
        ins = refs[:nb]
        for out, arrival in _exchange_copies(ins[:n] if scatter else None, ins[nb - n:], refs[nb], refs[nb + 1]):
            out.wait_send()
            arrival.wait_recv()

    extra = [] if after is None else [after]
    res = pl.pallas_call(
        body, name=name, in_specs=[_HBM] * nb + [_SEM, _SEM] + [_ANY] * len(extra),
        out_shape=tuple(pltpu.HBM(b.shape, b.dtype) for b in bufs), out_specs=tuple([_HBM] * nb),
        input_output_aliases={i: i for i in range(nb)},
        compiler_params=pltpu.CompilerParams(has_side_effects=pltpu.SideEffectType.DATAFLOW_SIDE_EFFECTING),
    )(*bufs, send_sems, recv_sems, *extra)
    return list(res)


def _adamw_math(w, g, m, v):
    m = ADAM_B1 * m + (1.0 - ADAM_B1) * g
    v = ADAM_B2 * v + (1.0 - ADAM_B2) * (g * g)
    m_hat = m / (1.0 - ADAM_B1 ** ADAM_STEP)
    v_hat = v / (1.0 - ADAM_B2 ** ADAM_STEP)
    delta = -ADAM_LR * (m_hat / (jnp.sqrt(v_hat) + ADAM_EPS) + ADAM_WD * w)
    return delta, m, v


def _adamw(w, m, v, parts, prev, layer, me_idx, *, name, own=None):
    L, r, cc = w.shape
    P = parts.shape[0]
    tr = 512 if r % 512 == 0 else r
    if prev is None:
        prev = tuple(lax.empty(w.shape, F32) for _ in range(4))

    def body(me_ref, w_ref, m_ref, v_ref, p_ref, *rest):
        g_ref, d_ref, nm_ref, nv_ref = rest[-4:]
        if own is None:
            g = p_ref[0].astype(F32)
            for q in range(1, P):
                g = g + p_ref[q].astype(F32)
        else:
            me = me_ref[0]
            g = rest[0][...].astype(F32)
            for q in range(P):
                g = g + jnp.where(q == me, 0.0, p_ref[q].astype(F32))
        d, nm, nv = _adamw_math(w_ref[...], g, m_ref[...], v_ref[...])
        g_ref[...] = g
        d_ref[...] = d
        nm_ref[...] = nm
        nv_ref[...] = nv

    blk = pl.BlockSpec((None, tr, cc), lambda i, me: (layer, i, 0))
    in_specs = [blk, blk, blk, pl.BlockSpec((P, tr, cc), lambda i, me: (0, i, 0))]
    args = [me_idx, w, m, v, parts]
    if own is not None:
        in_specs.append(pl.BlockSpec((None, tr, cc), lambda i, me: (me[0], i, 0)))
        args.append(own)
    first_prev = len(args)
    in_specs += [_ANY] * 4
    args += list(prev)
    grid_spec = pltpu.PrefetchScalarGridSpec(num_scalar_prefetch=1, grid=(r // tr,), in_specs=in_specs,
                                             out_specs=(blk, blk, blk, blk))
    o = jax.ShapeDtypeStruct(w.shape, F32)
    return pl.pallas_call(body, grid_spec=grid_spec, out_shape=(o, o, o, o),
                          input_output_aliases={first_prev + i: i for i in range(4)},
                          compiler_params=_cp("parallel"), name=name)(*args)


_SHARDED = ("w_in", "conv_a_w", "w_a_out", "conv_b_w", "w_b_out", "w_o", "w_1", "w_2")
_COL_SHARDED = ("w_in", "conv_a_w", "conv_b_w", "w_1")
_REPLICATED = ("g_mix", "b_in", "conv_a_b", "ln_g", "ln_b", "conv_b_b", "w_rg_a", "b_rg_a", "w_rg_x", "b_rg_x", "lam",
               "g_mlp")
_WEIGHTS = ("g_mix", "w_in", "b_in", "conv_a_w", "conv_a_b", "ln_g", "ln_b", "w_a_out", "conv_b_w", "conv_b_b", "w_rg_a",
            "b_rg_a", "w_rg_x", "b_rg_x", "lam", "w_b_out", "w_o", "g_mlp", "w_1", "w_2", "g_final")
_LANES = 128


def _cols_from_blocks(b):
    nb, K, n = b.shape
    return b.transpose(1, 0, 2).reshape(K, nb * n)


def _blocks_from_cols(w, K):
    n = w.shape[1] // N_DEV
    return w[:K].reshape(K, N_DEV, n).transpose(1, 0, 2)


def kernel(x, g_mix, w_in, b_in, conv_a_w, conv_a_b, ln_g, ln_b, w_a_out, conv_b_w, conv_b_b, w_rg_a, b_rg_a, w_rg_x, b_rg_x, lam, w_b_out, w_o, g_mlp, w_1, w_2, g_final, loss_target, m_g_mix, m_w_in, m_b_in, m_conv_a_w, m_conv_a_b, m_ln_g, m_ln_b, m_w_a_out, m_conv_b_w, m_conv_b_b, m_w_rg_a, m_b_rg_a, m_w_rg_x, m_b_rg_x, m_lam, m_w_b_out, m_w_o, m_g_mlp, m_w_1, m_w_2, m_g_final, v_g_mix, v_w_in, v_b_in, v_conv_a_w, v_conv_a_b, v_ln_g, v_ln_b, v_w_a_out, v_conv_b_w, v_conv_b_b, v_w_rg_a, v_b_rg_a, v_w_rg_x, v_b_rg_x, v_lam, v_w_b_out, v_w_o, v_g_mlp, v_w_1, v_w_2, v_g_final):
    W = dict(g_mix=g_mix, w_in=w_in, b_in=b_in, conv_a_w=conv_a_w, conv_a_b=conv_a_b, ln_g=ln_g, ln_b=ln_b,
             w_a_out=w_a_out, conv_b_w=conv_b_w, conv_b_b=conv_b_b, w_rg_a=w_rg_a, b_rg_a=b_rg_a, w_rg_x=w_rg_x,
             b_rg_x=b_rg_x, lam=lam, w_b_out=w_b_out, w_o=w_o, g_mlp=g_mlp, w_1=w_1, w_2=w_2, g_final=g_final)
    M = dict(g_mix=m_g_mix, w_in=m_w_in, b_in=m_b_in, conv_a_w=m_conv_a_w, conv_a_b=m_conv_a_b, ln_g=m_ln_g, ln_b=m_ln_b,
             w_a_out=m_w_a_out, conv_b_w=m_conv_b_w, conv_b_b=m_conv_b_b, w_rg_a=m_w_rg_a, b_rg_a=m_b_rg_a,
             w_rg_x=m_w_rg_x, b_rg_x=m_b_rg_x, lam=m_lam, w_b_out=m_w_b_out, w_o=m_w_o, g_mlp=m_g_mlp, w_1=m_w_1,
             w_2=m_w_2, g_final=m_g_final)
    V = dict(g_mix=v_g_mix, w_in=v_w_in, b_in=v_b_in, conv_a_w=v_conv_a_w, conv_a_b=v_conv_a_b, ln_g=v_ln_g, ln_b=v_ln_b,
             w_a_out=v_w_a_out, conv_b_w=v_conv_b_w, conv_b_b=v_conv_b_b, w_rg_a=v_w_rg_a, b_rg_a=v_b_rg_a,
             w_rg_x=v_w_rg_x, b_rg_x=v_b_rg_x, lam=v_lam, w_b_out=v_w_b_out, w_o=v_w_o, g_mlp=v_g_mlp, w_1=v_w_1,
             w_2=v_w_2, g_final=v_g_final)
    NB, S, D = x.shape
    L = g_mix.shape[0]
    hd = w_rg_a.shape[-1]
    me_idx = (4 * lax.axis_index("x") + 2 * lax.axis_index("y") + lax.axis_index("c")).astype(jnp.int32).reshape(1)

    stages = (("in", ("w_in",)), ("mix", ("conv_a_w", "w_a_out", "conv_b_w", "w_b_out", "w_o")), ("mlp", ("w_1", "w_2")))
    gathers = {}
    started = jnp.zeros((), F32)
    first = None
    for l in range(L):
        for stage, names in stages:
            lands = [_place(W[k][l], me_idx, F32 if k.startswith("conv") else BF16, dep=first, name="place_" + k)
                     for k in names]
            send_sems, recv_sems, bufs, token = _exchange_start(None, lands, name=f"weights_start_{stage}_{l}")
            gathers[l, stage] = (names, send_sems, recv_sems, bufs)
            started = started + token[0, 0]
            if first is None:
                first = token

    xt = x.reshape(NB * S, D)
    layers, saved = [], []
    for l in range(L):
        p = {k: W[k][l][None] for k in ("g_mix", "b_in", "conv_a_b", "ln_g", "ln_b", "conv_b_b", "b_rg_a", "b_rg_x",
                                         "lam", "g_mlp")}
        if l == 0:
            p["g_mix"] = p["g_mix"] + started
        p["wg_a"] = _group_weights(w_rg_a[l])
        p["wg_x"] = _group_weights(w_rg_x[l])

        def fetch(stage, after, l=l, p=p):
            names, send_sems, recv_sems, bufs = gathers[l, stage]
            bufs = _exchange_wait(send_sems, recv_sems, bufs, scatter=False, after=after,
                                  name=f"weights_wait_{stage}_{l}")
            for k, full in zip(names, bufs):
                if k in ("w_in", "w_1"):
                    p[k] = full
                elif k in _COL_SHARDED:
                    p[k] = _cols_from_blocks(full)
                else:
                    p[k] = full.reshape(-1, full.shape[-1])

        layers.append(p)
        xt, sv = _layer_fwd(xt, p, S=S, fetch=fetch)
        saved.append(sv)
    loss, dx, dxb, dg_final = _final_loss(xt, g_final[None], loss_target.reshape(NB * S, D), name="final_loss")
    dx = (dx, dxb)

    results = {k: None for k in _SHARDED}

    def scatter_start(l, names, g, tag):
        srcs = []
        for k in names:
            shard_shape = W[k].shape[1:]
            if k in ("w_in", "w_1"):
                srcs.append(g[k])
            elif k in _COL_SHARDED:
                srcs.append(_blocks_from_cols(g[k], shard_shape[0]).astype(BF16))
            else:
                srcs.append(g[k].reshape((N_DEV,) + shard_shape))
        lands = [lax.empty(s.shape, BF16) for s in srcs]
        send_sems, recv_sems, bufs, token = _exchange_start(srcs, lands, name=f"grads_start_{tag}_{l}")
        return (names, send_sems, recv_sems, bufs, tag), token

    def scatter_finish(l, flight, after):
        names, send_sems, recv_sems, bufs, tag = flight
        bufs = _exchange_wait(send_sems, recv_sems, bufs, scatter=True, after=after, name=f"grads_wait_{tag}_{l}")
        n = len(names)
        for k, own, land in zip(names, bufs[:n], bufs[n:]):
            results[k] = _adamw(W[k], M[k], V[k], land, results[k], l, me_idx, own=own, name="adamw_" + k)

    gate_names = ("w_rg_a", "w_rg_x")
    gate_results = {k: None for k in gate_names}

    def small_start(l, g):
        lands = [_place(g[k].reshape(-1, hd), me_idx, F32, name="place_gate_grad") for k in gate_names]
        send_sems, recv_sems, bufs, token = _exchange_start(None, lands, name=f"small_start_{l}")
        return (send_sems, recv_sems, bufs), token

    def small_finish(l, flight, after):
        send_sems, recv_sems, bufs = flight
        bufs = _exchange_wait(send_sems, recv_sems, bufs, scatter=False, after=after, name=f"small_wait_{l}")
        for k, g_all in zip(gate_names, bufs):
            gate_results[k] = _adamw(W[k].reshape(L, -1, hd), M[k].reshape(L, -1, hd), V[k].reshape(L, -1, hd), g_all,
                                     gate_results[k], l, me_idx, name="adamw_gate")

    grads = [None] * L
    in_flight = []
    dep = None
    for l in reversed(range(L)):
        dx_mid, g = _layer_bwd_mlp(dx, layers[l], saved[l], wdt=BF16, dep=dep)
        f_mlp, dep = scatter_start(l, ("w_2", "w_1"), g, "mlp")
        flights = [f_mlp]

        small = []

        def on_gate_grads(g_part, l=l, small=small):
            f_small, token = small_start(l, g_part)
            small.append(f_small)
            return token

        def on_weight_grads(g_part, l=l, flights=flights):
            f_mix, token = scatter_start(l, ("w_o", "w_b_out", "w_a_out", "conv_a_w", "conv_b_w", "w_in"), g_part, "mix")
            flights.append(f_mix)
            return token

        dx, g_mix_part = _layer_bwd_mix(dx_mid, layers[l], saved[l], S=S, wdt=BF16, dep=dep,
                                        on_gate_grads=on_gate_grads, on_weight_grads=on_weight_grads)
        dep = None
        g.update(g_mix_part)
        grads[l] = g
        for l_prev, fs, f_sm in in_flight:
            for flight in fs:
                scatter_finish(l_prev, flight, dx[0])
            small_finish(l_prev, f_sm, dx[0])
        in_flight = [(l, flights, small[0])]

    vec_names = tuple(k for k in _REPLICATED if k not in gate_names)
    n_vec = sum(W[k].shape[1] for k in vec_names)

    def vec_pack(rows, final, last):
        tail = jnp.concatenate([final.reshape(1, -1), jnp.broadcast_to(last.reshape(1, 1), (1, _LANES))], axis=1)
        tail = jnp.pad(tail, ((0, SUBLANES - L - 1), (0, n_vec - tail.shape[1])))
        return jnp.concatenate([rows, tail], axis=0)

    g_rows = jnp.concatenate([jnp.concatenate([grads[l][k] for k in vec_names], axis=1) for l in range(L)], axis=0)
    land = _place(vec_pack(g_rows, dg_final, loss[0, :1]), me_idx, F32, name="place_vectors")
    vec_send_sems, vec_recv_sems, vec_bufs, _ = _exchange_start(None, [land], name="vectors_start")
    for l_prev, fs, f_sm in in_flight:
        after = dx[0] if L == 1 else results["w_in"][0]
        for flight in fs:
            scatter_finish(l_prev, flight, after)
            after = results[flight[0][-1]][0]
        small_finish(l_prev, f_sm, after)
    out_g, out_d, out_m, out_v = {}, {}, {}, {}
    for k in _SHARDED:
        out_g[k], out_d[k], out_m[k], out_v[k] = results[k]
    for k in gate_names:
        out_g[k], out_d[k], out_m[k], out_v[k] = (a.reshape(W[k].shape) for a in gate_results[k])

    zero = jnp.zeros((1,), F32)
    (g_all,) = _exchange_wait(vec_send_sems, vec_recv_sems, vec_bufs, scatter=False, after=results["w_in"][0],
                              name="vectors_wait")
    vec_out = _adamw(vec_pack(jnp.concatenate([W[k] for k in vec_names], axis=1), g_final, zero)[None],
                     vec_pack(jnp.concatenate([M[k] for k in vec_names], axis=1), m_g_final, zero)[None],
                     vec_pack(jnp.concatenate([V[k] for k in vec_names], axis=1), v_g_final, zero)[None],
                     g_all, None, 0, me_idx, name="adamw_vectors")
    vec_out = [a[0] for a in vec_out]
    for res, arr in zip((out_g, out_d, out_m, out_v), vec_out):
        off = 0
        for k in vec_names:
            res[k] = arr[:L, off:off + W[k].shape[1]]
            off += W[k].shape[1]
        res["g_final"] = arr[L, :g_final.size]
    loss_out = vec_out[0][L, g_final.size]

    return (loss_out, dx[0].reshape(NB, S, D), *[out_g[k] for k in _WEIGHTS], *[out_d[k] for k in _WEIGHTS],
            *[out_m[k] for k in _WEIGHTS], *[out_v[k] for k in _WEIGHTS])
```

```python
import functools

import jax
import jax.numpy as jnp
from jax import lax
from jax.experimental import pallas as pl
from jax.experimental.pallas import tpu as pltpu

F32 = jnp.float32
BF16 = jnp.bfloat16

EPS = 1e-6
LRU_C = 8.0
N_RNN_HEADS = 16
HEADS_PER_GROUP = 4
N_DEV = 8
ADAM_LR, ADAM_B1, ADAM_B2, ADAM_EPS, ADAM_WD, ADAM_STEP = 0.001, 0.9, 0.999, 1e-08, 0.01, 10

VMEM_LIMIT_BYTES = 48 * 1024 * 1024
CONV_PAD = 32
CONV_CHUNK = 64
SUBLANES = 8


def _cp(*sem):
    return pltpu.CompilerParams(dimension_semantics=sem, vmem_limit_bytes=VMEM_LIMIT_BYTES)


def _sig(x):
    return 1.0 / (1.0 + jnp.exp(-x))


def _gelu(x):
    c = 0.7978845608028654
    return 0.5 * x * (1.0 + jnp.tanh(c * (x + 0.044715 * x * x * x)))


def _gelu_grad(x):
    c = 0.7978845608028654
    th = jnp.tanh(c * (x + 0.044715 * x * x * x))
    return 0.5 * (1.0 + th) + 0.5 * x * (1.0 - th * th) * c * (1.0 + 3.0 * 0.044715 * x * x)


def _mm_nn(a, b, *, tm, tn, name, bias=None, resid=None, relu2=False, out_dtype=F32, a_resident=False):
    M, K = a.shape
    blocked = b.ndim == 3
    N = b.shape[0] * b.shape[2] if blocked else b.shape[1]
    tm = min(tm, M)
    tn = min(tn, N)
    if blocked:
        assert tn == b.shape[2]
    n_extra = (bias is not None) + (resid is not None)

    def body(*refs):
        acc = jnp.dot(refs[0][...].astype(BF16), refs[1][...].astype(BF16), preferred_element_type=F32)
        k = 2
        if bias is not None:
            acc = acc + refs[k][...]
            k += 1
        if resid is not None:
            acc = acc + refs[k][...]
            k += 1
        if relu2:
            p = jnp.maximum(acc, 0.0)
            acc = p * p
        refs[k][...] = acc.astype(out_dtype)

    def spec(shape, index):
        return pl.BlockSpec(shape, (lambda i, j: index(j, i)) if a_resident else index)

    in_specs = [spec((tm, K), lambda j, i: (i, 0))]
    if blocked:
        in_specs.append(spec((None, K, tn), lambda j, i: (j, 0, 0)))
    else:
        in_specs.append(spec((K, tn), lambda j, i: (0, j)))
    args = [a, b]
    if bias is not None:
        in_specs.append(spec((1, tn), lambda j, i: (0, j)))
        args.append(bias)
    if resid is not None:
        in_specs.append(spec((tm, tn), lambda j, i: (i, j)))
        args.append(resid)
    out_specs = spec((tm, tn), lambda j, i: (i, j))
    out_shape = jax.ShapeDtypeStruct((M, N), out_dtype)
    del n_extra
    grid = (M // tm, N // tn) if a_resident else (N // tn, M // tm)
    return pl.pallas_call(body, grid=grid, in_specs=in_specs, out_specs=out_specs,
                          out_shape=out_shape, compiler_params=_cp("parallel", "parallel"), name=name)(*args)


def _mm_nt(a, b, *, tm, tn, tk, name, mul_sqrt=None, resid=None, out_dtype=F32, dep=None, whole_b=False):
    M, N = a.shape
    blocked = b.ndim == 3
    Kout = b.shape[1] if blocked else b.shape[0]
    tm = min(tm, M)
    tn = min(tn, Kout)
    tk = N if whole_b else (b.shape[2] if blocked else min(tk, N))
    nk = N // tk
    nt_dims = (((1,), (1,)), ((), ()))

    def body(*refs):
        acc_ref = refs[-1]
        kk = pl.program_id(2)
        if blocked and whole_b:
            n = b.shape[2]
            part = None
            for jb in range(b.shape[0]):
                pj = lax.dot_general(refs[0][:, jb * n:(jb + 1) * n].astype(BF16), refs[1][jb].astype(BF16), nt_dims,
                                     preferred_element_type=F32)
                part = pj if part is None else part + pj
        else:
            part = lax.dot_general(refs[0][...].astype(BF16), refs[1][...].astype(BF16), nt_dims,
                                   preferred_element_type=F32)

        def finish(acc):
            k = 2
            if mul_sqrt is not None:
                acc = acc * (2.0 * jnp.sqrt(refs[k][...].astype(F32)))
                k += 1
            if resid is not None:
                acc = acc + refs[k][...]
                k += 1
            if dep is not None:
                k += 1
            refs[k][...] = acc.astype(out_dtype)

        if nk == 1:
            finish(part)
            return

        @pl.when(kk == 0)
        def _():
            acc_ref[...] = part

        @pl.when(kk > 0)
        def _():
            acc_ref[...] += part

        @pl.when(kk == nk - 1)
        def _():
            finish(acc_ref[...])

    in_specs = [pl.BlockSpec((tm, tk), lambda i, j, k: (i, k))]
    if blocked and whole_b:
        in_specs.append(pl.BlockSpec((b.shape[0], tn, b.shape[2]), lambda i, j, k: (0, j, 0)))
    elif blocked:
        in_specs.append(pl.BlockSpec((None, tn, tk), lambda i, j, k: (k, j, 0)))
    else:
        in_specs.append(pl.BlockSpec((tn, tk), lambda i, j, k: (j, k)))
    args = [a, b]
    for extra in (mul_sqrt, resid):
        if extra is not None:
            in_specs.append(pl.BlockSpec((tm, tn), lambda i, j, k: (i, j)))
            args.append(extra)
    if dep is not None:
        in_specs.append(pl.BlockSpec(dep.shape, lambda i, j, k: (0, 0)))
        args.append(dep)
    return pl.pallas_call(body, grid=(M // tm, Kout // tn, nk), in_specs=in_specs,
                          out_specs=pl.BlockSpec((tm, tn), lambda i, j, k: (i, j)),
                          out_shape=jax.ShapeDtypeStruct((M, Kout), out_dtype),
                          scratch_shapes=[pltpu.VMEM((tm, tn), F32)] if nk > 1 else [],
                          compiler_params=_cp("parallel", "parallel", "arbitrary"), name=name)(*args)


def _mm_tn(a, b, *, tm, tn, tk, name, out_blocks=None, colsum=False, out_dtype=F32):
    T, M = a.shape
    N = b.shape[1]
    tm = min(tm, M)
    tk = min(tk, T)
    if out_blocks is not None:
        tn = N // out_blocks
        tm = M
    tn = min(tn, N)
    nk = T // tk
    if colsum:
        assert tm == M

    def body(*refs):
        a_ref, b_ref, o_ref, acc_ref = refs[0], refs[1], refs[2], refs[-1]
        kk = pl.program_id(2)
        bv = b_ref[...]
        part = lax.dot_general(a_ref[...].astype(BF16), bv.astype(BF16),
                               (((0,), (0,)), ((), ())), preferred_element_type=F32)

        if colsum:
            csum = jnp.broadcast_to(jnp.sum(bv.astype(F32), axis=0, keepdims=True), (SUBLANES, tn))

        if nk == 1:
            o_ref[...] = part.astype(out_dtype)
            if colsum:
                refs[3][...] = csum
            return

        @pl.when(kk == 0)
        def _():
            acc_ref[...] = part
            if colsum:
                refs[3][...] = csum

        @pl.when(kk > 0)
        def _():
            acc_ref[...] += part
            if colsum:
                refs[3][...] += csum

        @pl.when(kk == nk - 1)
        def _():
            o_ref[...] = acc_ref[...].astype(out_dtype)

    in_specs = [pl.BlockSpec((tk, tm), lambda i, j, k: (k, i)), pl.BlockSpec((tk, tn), lambda i, j, k: (k, j))]
    if out_blocks is not None:
        o_shape = jax.ShapeDtypeStruct((out_blocks, M, tn), out_dtype)
        o_spec = pl.BlockSpec((None, M, tn), lambda i, j, k: (j, 0, 0))
    else:
        o_shape = jax.ShapeDtypeStruct((M, N), out_dtype)
        o_spec = pl.BlockSpec((tm, tn), lambda i, j, k: (i, j))
    if colsum:
        out_shape = (o_shape, jax.ShapeDtypeStruct((SUBLANES, N), F32))
        out_specs = (o_spec, pl.BlockSpec((SUBLANES, tn), lambda i, j, k: (0, j)))
    else:
        out_shape, out_specs = o_shape, o_spec
    return pl.pallas_call(body, grid=(M // tm, N // tn, nk), in_specs=in_specs, out_specs=out_specs,
                          out_shape=out_shape, scratch_shapes=[pltpu.VMEM((tm, tn), F32)] if nk > 1 else [],
                          compiler_params=_cp("parallel", "parallel", "arbitrary"), name=name)(a, b)


def _rms_fwd(x, g, *, name, tr=512):
    T, D = x.shape
    tr = min(tr, T)

    def body(x_ref, g_ref, h_ref):
        xv = x_ref[...]
        r = lax.rsqrt(jnp.mean(xv * xv, axis=-1, keepdims=True) + EPS)
        h_ref[...] = (xv * r * g_ref[...]).astype(BF16)

    return pl.pallas_call(body, grid=(T // tr,),
                          in_specs=[pl.BlockSpec((tr, D), lambda i: (i, 0)), pl.BlockSpec((1, D), lambda i: (0, 0))],
                          out_specs=pl.BlockSpec((tr, D), lambda i: (i, 0)),
                          out_shape=jax.ShapeDtypeStruct((T, D), BF16), compiler_params=_cp("parallel"), name=name)(x, g)


def _rms_bwd(x, g, dh, dres, *, name, tr=512):
    T, D = x.shape
    tr = min(tr, T)

    def body(x_ref, g_ref, dh_ref, dres_ref, dx_ref, dxb_ref, dg_ref):
        xv = x_ref[...]
        r = lax.rsqrt(jnp.mean(xv * xv, axis=-1, keepdims=True) + EPS)
        n = xv * r
        dh = dh_ref[...]
        dn = dh * g_ref[...]
        dx = dres_ref[...] + r * (dn - n * jnp.mean(dn * n, axis=-1, keepdims=True))
        dx_ref[...] = dx
        dxb_ref[...] = dx.astype(BF16)
        part = jnp.sum(dh * n, axis=0, keepdims=True)

        @pl.when(pl.program_id(0) == 0)
        def _():
            dg_ref[...] = part

        @pl.when(pl.program_id(0) > 0)
        def _():
            dg_ref[...] += part

    row = pl.BlockSpec((tr, D), lambda i: (i, 0))
    vec = pl.BlockSpec((1, D), lambda i: (0, 0))
    return pl.pallas_call(body, grid=(T // tr,), in_specs=[row, vec, row, row], out_specs=(row, row, vec),
                          out_shape=(jax.ShapeDtypeStruct((T, D), F32), jax.ShapeDtypeStruct((T, D), BF16),
                                     jax.ShapeDtypeStruct((1, D), F32)),
                          compiler_params=_cp("arbitrary"), name=name)(x, g, dh, dres)


def _final_loss(x, g, tgt, *, name, tr=512):
    T, D = x.shape
    tr = min(tr, T)

    def body(x_ref, g_ref, t_ref, loss_ref, dx_ref, dxb_ref, dg_ref):
        xv = x_ref[...]
        gv = g_ref[...]
        r = lax.rsqrt(jnp.mean(xv * xv, axis=-1, keepdims=True) + EPS)
        n = xv * r
        e = n * gv - t_ref[...]
        lpart = 0.5 * jnp.sum(jnp.mean(e * e, axis=-1, keepdims=True), axis=0, keepdims=True)
        dy = e * (1.0 / D)
        dn = dy * gv
        dx = r * (dn - n * jnp.mean(dn * n, axis=-1, keepdims=True))
        dx_ref[...] = dx
        dxb_ref[...] = dx.astype(BF16)
        gpart = jnp.sum(dy * n, axis=0, keepdims=True)

        @pl.when(pl.program_id(0) == 0)
        def _():
            dg_ref[...] = gpart
            loss_ref[...] = jnp.broadcast_to(lpart, (1, 128))

        @pl.when(pl.program_id(0) > 0)
        def _():
            dg_ref[...] += gpart
            loss_ref[...] += jnp.broadcast_to(lpart, (1, 128))

    row = pl.BlockSpec((tr, D), lambda i: (i, 0))
    vec = pl.BlockSpec((1, D), lambda i: (0, 0))
    return pl.pallas_call(body, grid=(T // tr,), in_specs=[row, vec, row],
                          out_specs=(pl.BlockSpec((1, 128), lambda i: (0, 0)), row, row, vec),
                          out_shape=(jax.ShapeDtypeStruct((1, 128), F32), jax.ShapeDtypeStruct((T, D), F32),
                                     jax.ShapeDtypeStruct((T, D), BF16), jax.ShapeDtypeStruct((1, D), F32)),
                          compiler_params=_cp("arbitrary"), name=name)(x, g, tgt)


def _ln_silu_fwd(u, g, b, *, name, tr=512):
    T, C = u.shape
    tr = min(tr, T)

    def body(u_ref, g_ref, b_ref, o_ref):
        uv = u_ref[...]
        mu = jnp.mean(uv, axis=-1, keepdims=True)
        xc = uv - mu
        r = lax.rsqrt(jnp.mean(xc * xc, axis=-1, keepdims=True) + EPS)
        y = xc * r * g_ref[...] + b_ref[...]
        o_ref[...] = (y * _sig(y)).astype(BF16)

    row = pl.BlockSpec((tr, C), lambda i: (i, 0))
    vec = pl.BlockSpec((1, C), lambda i: (0, 0))
    return pl.pallas_call(body, grid=(T // tr,), in_specs=[row, vec, vec], out_specs=row,
                          out_shape=jax.ShapeDtypeStruct((T, C), BF16), compiler_params=_cp("parallel"),
                          name=name)(u, g, b)


def _ln_silu_bwd(u, g, b, do, *, name, tr=512):
    T, C = u.shape
    tr = min(tr, T)

    def body(u_ref, g_ref, b_ref, do_ref, du_ref, dg_ref, db_ref):
        uv = u_ref[...]
        gv = g_ref[...]
        mu = jnp.mean(uv, axis=-1, keepdims=True)
        xc = uv - mu
        r = lax.rsqrt(jnp.mean(xc * xc, axis=-1, keepdims=True) + EPS)
        n = xc * r
        y = n * gv + b_ref[...]
        s = _sig(y)
        dy = do_ref[...] * (s * (1.0 + y * (1.0 - s)))
        dn = dy * gv
        du_ref[...] = r * (dn - jnp.mean(dn, axis=-1, keepdims=True) - n * jnp.mean(dn * n, axis=-1, keepdims=True))
        gpart = jnp.sum(dy * n, axis=0, keepdims=True)
        bpart = jnp.sum(dy, axis=0, keepdims=True)

        @pl.when(pl.program_id(0) == 0)
        def _():
            dg_ref[...] = gpart
            db_ref[...] = bpart

        @pl.when(pl.program_id(0) > 0)
        def _():
            dg_ref[...] += gpart
            db_ref[...] += bpart

    row = pl.BlockSpec((tr, C), lambda i: (i, 0))
    vec = pl.BlockSpec((1, C), lambda i: (0, 0))
    return pl.pallas_call(body, grid=(T // tr,), in_specs=[row, vec, vec, row], out_specs=(row, vec, vec),
                          out_shape=(jax.ShapeDtypeStruct((T, C), F32), jax.ShapeDtypeStruct((1, C), F32),
                                     jax.ShapeDtypeStruct((1, C), F32)),
                          compiler_params=_cp("arbitrary"), name=name)(u, g, b, do)


def _merge_fwd(z, ya, yb, *, off_sa, off_sb, name, tr=512):
    T, D = ya.shape
    tr = min(tr, T)
    assert off_sa % D == 0 and off_sb % D == 0

    def body(sa_ref, sb_ref, ya_ref, yb_ref, m_ref):
        m_ref[...] = (_sig(sa_ref[...].astype(F32)) * ya_ref[...]
                      + _sig(sb_ref[...].astype(F32)) * yb_ref[...]).astype(BF16)

    row = pl.BlockSpec((tr, D), lambda i: (i, 0))
    return pl.pallas_call(body, grid=(T // tr,),
                          in_specs=[pl.BlockSpec((tr, D), lambda i: (i, off_sa // D)),
                                    pl.BlockSpec((tr, D), lambda i: (i, off_sb // D)), row, row],
                          out_specs=row, out_shape=jax.ShapeDtypeStruct((T, D), BF16),
                          compiler_params=_cp("parallel"), name=name)(z, z, ya, yb)


def _columns_copy(stage_ref, dz_ref, row0, rows, col0, sem):
    dst = dz_ref.at[pl.ds(pl.multiple_of(row0, SUBLANES), rows),
                    pl.ds(pl.multiple_of(col0, 128), stage_ref.shape[1])]
    return pltpu.make_async_copy(stage_ref, dst, sem)


def _put_columns(stage_ref, dz_ref, row0, rows, col0, sem):
    cp = _columns_copy(stage_ref, dz_ref, row0, rows, col0, sem)
    cp.start()
    cp.wait()


def _merge_bwd(z, ya, yb, dm, dz, *, off_sa, off_sb, name, tr=512):
    T, D = ya.shape
    tr = min(tr, T)
    assert off_sb == off_sa + D

    def body(sa_ref, sb_ref, ya_ref, yb_ref, dm_ref, dz_in, dya_ref, dyb_ref, dz_ref, stage, sem):
        del dz_in
        dm = dm_ref[...]
        ga = _sig(sa_ref[...].astype(F32))
        gb = _sig(sb_ref[...].astype(F32))
        dya_ref[...] = (dm * ga).astype(BF16)
        dyb_ref[...] = (dm * gb).astype(BF16)
        stage[:, 0:D] = (dm * ya_ref[...] * ga * (1.0 - ga)).astype(BF16)
        stage[:, D:2 * D] = (dm * yb_ref[...] * gb * (1.0 - gb)).astype(BF16)
        _put_columns(stage, dz_ref, pl.program_id(0) * tr, tr, off_sa, sem)

    row = pl.BlockSpec((tr, D), lambda i: (i, 0))
    o = jax.ShapeDtypeStruct((T, D), BF16)
    return pl.pallas_call(body, grid=(T // tr,),
                          in_specs=[pl.BlockSpec((tr, D), lambda i: (i, off_sa // D)),
                                    pl.BlockSpec((tr, D), lambda i: (i, off_sb // D)), row, row, row, _ANY],
                          out_specs=(row, row, _ANY), out_shape=(o, o, jax.ShapeDtypeStruct(dz.shape, dz.dtype)),
                          scratch_shapes=[pltpu.VMEM((tr, 2 * D), BF16), pltpu.SemaphoreType.DMA],
                          input_output_aliases={5: 2},
                          compiler_params=_cp("parallel"), name=name)(z, z, ya, yb, dm, dz)


def _shift_rows(dst_ref, src_ref, r, total, back):
    for c0 in range(0, total - SUBLANES, CONV_CHUNK):
        n = min(CONV_CHUNK, total - SUBLANES - c0)
        if back:
            dst_ref[SUBLANES + c0:SUBLANES + c0 + n, :] = src_ref[SUBLANES + c0 - r:SUBLANES + c0 - r + n, :]
        else:
            dst_ref[c0:c0 + n, :] = src_ref[c0 + r:c0 + r + n, :]


def _tap_plan(K):
    if K <= SUBLANES:
        return [(0, [(s, K - 1 - s) for s in range(K)])]
    return [(r, [(SUBLANES * q, K - 1 - (SUBLANES * q + r)) for q in range(-(-K // SUBLANES)) if SUBLANES * q + r < K])
            for r in range(SUBLANES)]


def _conv_fwd(z, w, b, *, S, off_v, off_g, name, ct=256):
    T = z.shape[0]
    K, C = w.shape
    ct = min(ct, C)
    ch = min(CONV_CHUNK, S)
    glu = off_g is not None
    assert off_v % ct == 0 and (not glu or off_g % ct == 0)
    assert SUBLANES * ((K - 1) // SUBLANES) <= CONV_PAD - SUBLANES

    def body(*refs):
        if glu:
            v_ref, g_ref, w_ref, b_ref, o_ref, pad_ref, sh_ref = refs
        else:
            v_ref, w_ref, b_ref, o_ref, pad_ref, sh_ref = refs
        pad_ref[0:CONV_PAD, :] = jnp.zeros((CONV_PAD, ct), F32)
        if glu:
            pad_ref[CONV_PAD:CONV_PAD + S, :] = v_ref[...].astype(F32) * _sig(g_ref[...].astype(F32))
        else:
            pad_ref[CONV_PAD:CONV_PAD + S, :] = v_ref[...].astype(F32)
        for r, taps in _tap_plan(K):
            src = pad_ref
            if r > 0:
                _shift_rows(sh_ref, pad_ref, r, CONV_PAD + S, True)
                src = sh_ref
            for l0 in range(0, ct, 128):
                lanes = slice(l0, l0 + 128)
                for c in range(S // ch):
                    acc = None
                    for off, wrow in taps:
                        st = CONV_PAD + c * ch - off
                        term = w_ref[wrow:wrow + 1, lanes] * src[st:st + ch, lanes]
                        acc = term if acc is None else acc + term
                    rows = slice(c * ch, (c + 1) * ch)
                    if r == 0:
                        o_ref[rows, lanes] = acc + b_ref[:, lanes]
                    else:
                        o_ref[rows, lanes] += acc

    in_specs = [pl.BlockSpec((S, ct), lambda j, bb: (bb, off_v // ct + j))]
    args = [z]
    if glu:
        in_specs.append(pl.BlockSpec((S, ct), lambda j, bb: (bb, off_g // ct + j)))
        args.append(z)
    in_specs += [pl.BlockSpec((K, ct), lambda j, bb: (0, j)), pl.BlockSpec((1, ct), lambda j, bb: (0, j))]
    args += [w, b]
    return pl.pallas_call(body, grid=(C // ct, T // S), in_specs=in_specs,
                          out_specs=pl.BlockSpec((S, ct), lambda j, bb: (bb, j)),
                          out_shape=jax.ShapeDtypeStruct((T, C), F32),
                          scratch_shapes=[pltpu.VMEM((CONV_PAD + S, ct), F32), pltpu.VMEM((CONV_PAD + S, ct), F32)],
                          compiler_params=_cp("parallel", "parallel"), name=name)(*args)


def _conv_bwd(z, w, dy, dz, *, S, off_v, off_g, name, ct=256):
    T = z.shape[0]
    K, C = w.shape
    KP = -(-K // SUBLANES) * SUBLANES
    ct = min(ct, C)
    ch = min(CONV_CHUNK, S)
    glu = off_g is not None
    total = S + CONV_PAD

    def body(*refs):
        if glu:
            (v_ref, g_ref, w_ref, dy_ref, dz_in, dz_ref, dw_ref, db_ref,
             pad_ref, sh_ref, padb_ref, shb_ref, du_ref, stage_v, stage_g, sem) = refs
        else:
            (v_ref, w_ref, dy_ref, dz_in, dz_ref, dw_ref, db_ref,
             pad_ref, sh_ref, padb_ref, shb_ref, du_ref, stage_v, sem) = refs
        del dz_in
        j, bb = pl.program_id(0), pl.program_id(1)
        pad_ref[0:CONV_PAD, :] = jnp.zeros((CONV_PAD, ct), F32)
        if glu:
            pad_ref[CONV_PAD:total, :] = v_ref[...].astype(F32) * _sig(g_ref[...].astype(F32))
        else:
            pad_ref[CONV_PAD:total, :] = v_ref[...].astype(F32)
        padb_ref[0:S, :] = dy_ref[...]
        padb_ref[S:total, :] = jnp.zeros((CONV_PAD, ct), F32)

        @pl.when(bb == 0)
        def _():
            dw_ref[...] = jnp.zeros((KP, ct), F32)
            db_ref[...] = jnp.zeros((1, ct), F32)

        for r, taps in _tap_plan(K):
            u_src, d_src = pad_ref, padb_ref
            if r > 0:
                _shift_rows(sh_ref, pad_ref, r, total, True)
                _shift_rows(shb_ref, padb_ref, r, total, False)
                u_src, d_src = sh_ref, shb_ref
            for l0 in range(0, ct, 128):
                lanes = slice(l0, l0 + 128)
                for c in range(S // ch):
                    acc = None
                    for off, wrow in taps:
                        st = c * ch + off
                        term = w_ref[wrow:wrow + 1, lanes] * d_src[st:st + ch, lanes]
                        acc = term if acc is None else acc + term
                    rows = slice(c * ch, (c + 1) * ch)
                    if r == 0:
                        du_ref[rows, lanes] = acc
                    else:
                        du_ref[rows, lanes] += acc
                for off, wrow in taps:
                    acc = None
                    for c in range(S // ch):
                        st = CONV_PAD + c * ch - off
                        prod = padb_ref[c * ch:(c + 1) * ch, lanes] * u_src[st:st + ch, lanes]
                        acc = prod if acc is None else acc + prod
                    dw_ref[wrow:wrow + 1, lanes] += jnp.sum(acc, axis=0, keepdims=True)
        db_ref[...] += jnp.sum(dy_ref[...], axis=0, keepdims=True)
        for l0 in range(0, ct, 128):
            lanes = slice(l0, l0 + 128)
            for c in range(S // ch):
                rows = slice(c * ch, (c + 1) * ch)
                du = du_ref[rows, lanes]
                if glu:
                    sg = _sig(g_ref[rows, lanes].astype(F32))
                    stage_v[rows, lanes] = (du * sg).astype(BF16)
                    stage_g[rows, lanes] = (du * v_ref[rows, lanes].astype(F32) * sg * (1.0 - sg)).astype(BF16)
                else:
                    stage_v[rows, lanes] = du.astype(BF16)
        _put_columns(stage_v, dz_ref, bb * S, S, off_v + j * ct, sem)
        if glu:
            _put_columns(stage_g, dz_ref, bb * S, S, off_g + j * ct, sem)

    blk = lambda off: pl.BlockSpec((S, ct), lambda j, bb: (bb, off // ct + j))
    in_specs = [blk(off_v)]
    args = [z]
    if glu:
        in_specs.append(blk(off_g))
        args.append(z)
    in_specs += [pl.BlockSpec((K, ct), lambda j, bb: (0, j)), blk(0), _ANY]
    args += [w, dy, dz]
    out_shape = (jax.ShapeDtypeStruct(dz.shape, dz.dtype), jax.ShapeDtypeStruct((KP, C), F32),
                 jax.ShapeDtypeStruct((1, C), F32))
    out_specs = (_ANY, pl.BlockSpec((KP, ct), lambda j, bb: (0, j)), pl.BlockSpec((1, ct), lambda j, bb: (0, j)))
    padded = pltpu.VMEM((total, ct), F32)
    stage = pltpu.VMEM((S, ct), BF16)
    return pl.pallas_call(body, grid=(C // ct, T // S), in_specs=in_specs, out_specs=out_specs, out_shape=out_shape,
                          scratch_shapes=[padded, padded, padded, padded, pltpu.VMEM((S, ct), F32), stage]
                          + ([stage] if glu else []) + [pltpu.SemaphoreType.DMA],
                          input_output_aliases={len(args) - 1: 0},
                          compiler_params=_cp("parallel", "arbitrary"), name=name)(*args)


def _softplus_neg(lam):
    return jnp.maximum(-lam, 0.0) + jnp.log1p(jnp.exp(-jnp.abs(lam)))


def _neg_expm1(x):
    u = jnp.exp(x)
    um1 = u - 1.0
    lg = jnp.log(u)
    safe = jnp.where(lg == 0.0, 1.0, lg)
    em1 = jnp.where(um1 == 0.0, x, jnp.where(um1 == -1.0, -1.0, um1 * x / safe))
    return -em1


def _rglru_fwd(v, z, wa, wx, ba, bx, lam, *, S, off_gate, name, tt=1024):
    T, C = v.shape
    G, gw, _ = wa.shape
    tt = min(tt, S)
    nt = S // tt
    nlb = gw // 128
    assert gw % 128 == 0 and off_gate % 128 == 0

    def body(*refs):
        v_ref = refs[0]
        gate_refs = refs[1:1 + nlb]
        wa_ref, wx_ref, ba_ref, bx_ref, lam_ref, r_ref, i_ref, h_ref, hb_ref, a_s, b_s, carry_ref = refs[1 + nlb:]
        t = pl.program_id(2)

        @pl.when(t == 0)
        def _():
            carry_ref[...] = jnp.zeros((SUBLANES, gw), F32)

        vv = v_ref[...]
        vb = vv.astype(BF16)
        r = _sig(jnp.dot(vb, wa_ref[...], preferred_element_type=F32) + ba_ref[...])
        ig = _sig(jnp.dot(vb, wx_ref[...], preferred_element_type=F32) + bx_ref[...])
        log_a = -LRU_C * r * _softplus_neg(lam_ref[...])
        mult = jnp.sqrt(_neg_expm1(2.0 * log_a))
        start = jnp.logical_and(t == 0, lax.broadcasted_iota(jnp.int32, (tt, gw), 0) == 0)
        mult = jnp.where(start, 1.0, mult)
        r_ref[...] = r
        i_ref[...] = ig
        a_s[...] = jnp.exp(log_a)
        b_s[...] = mult * ig * vv
        row = lax.broadcasted_iota(jnp.int32, (SUBLANES, gw), 0)

        def step(i, carry):
            st = pl.multiple_of(i * SUBLANES, SUBLANES)
            A = a_s[pl.ds(st, SUBLANES), :]
            B = b_s[pl.ds(st, SUBLANES), :]
            for d in (1, 2, 4):
                m = row >= d
                Bn = jnp.where(m, A * pltpu.roll(B, d, 0) + B, B)
                A = jnp.where(m, A * pltpu.roll(A, d, 0), A)
                B = Bn
            h = B + A * carry
            h_ref[pl.ds(st, SUBLANES), :] = h
            return jnp.broadcast_to(h[SUBLANES - 1:SUBLANES, :], (SUBLANES, gw))

        carry_ref[...] = lax.fori_loop(0, tt // SUBLANES, step, carry_ref[...], unroll=2)
        for k in range(nlb):
            lanes = slice(k * 128, (k + 1) * 128)
            hb_ref[:, lanes] = (h_ref[:, lanes] * _gelu(gate_refs[k][...].astype(F32))).astype(BF16)

    blk = pl.BlockSpec((tt, gw), lambda g, bb, t: (bb * nt + t, g))
    gates = [pl.BlockSpec((tt, 128), lambda g, bb, t, k=k: (bb * nt + t, off_gate // 128 + g * nlb + k)) for k in range(nlb)]
    wsp = pl.BlockSpec((None, gw, gw), lambda g, bb, t: (g, 0, 0))
    vec = pl.BlockSpec((1, gw), lambda g, bb, t: (0, g))
    o = jax.ShapeDtypeStruct((T, C), F32)
    return pl.pallas_call(body, grid=(G, T // S, nt), in_specs=[blk] + gates + [wsp, wsp, vec, vec, vec],
                          out_specs=(blk, blk, blk, blk), out_shape=(o, o, o, jax.ShapeDtypeStruct((T, C), BF16)),
                          scratch_shapes=[pltpu.VMEM((tt, gw), F32), pltpu.VMEM((tt, gw), F32),
                                          pltpu.VMEM((SUBLANES, gw), F32)],
                          compiler_params=_cp("parallel", "parallel", "arbitrary"),
                          name=name)(v, *([z] * nlb), wa, wx, ba, bx, lam)


def _rglru_bwd(dhb, h, z, r, ig, v, wa, wx, lam, dz, *, S, off_gate, name, tt=1024):
    T, C = v.shape
    G, gw, _ = wa.shape
    hd = gw // HEADS_PER_GROUP
    tt = min(tt, S)
    nt = S // tt
    n_seq = T // S
    n_tiles = tt // SUBLANES
    nlb = gw // 128

    def body(*refs):
        dhb_ref, h_ref, hp_ref = refs[0:3]
        gate_refs = refs[3:3 + nlb]
        (r_ref, i_ref, v_ref, wa_ref, wx_ref, lam_ref, dz_in, dv_ref, dz_ref, dlam_ref, dba_ref, dbx_ref, dwa_ref, dwx_ref,
         dh_s, a_s, G_s, da_s, carry_ref, stage, acc_a, acc_x, sem) = refs[3 + nlb:]
        del dz_in
        g, bb, t = pl.program_id(0), pl.program_id(1), pl.program_id(2)
        tb = nt - 1 - t
        first = jnp.logical_and(bb == 0, t == 0)

        @pl.when(t == 0)
        def _():
            carry_ref[...] = jnp.zeros((SUBLANES, gw), F32)

        @pl.when(first)
        def _():
            dlam_ref[...] = jnp.zeros((SUBLANES, gw), F32)
            dba_ref[...] = jnp.zeros((SUBLANES, gw), F32)
            dbx_ref[...] = jnp.zeros((SUBLANES, gw), F32)
            acc_a[...] = jnp.zeros((gw, gw), F32)
            acc_x[...] = jnp.zeros((gw, gw), F32)

        rr, ig, vv = r_ref[...], i_ref[...], v_ref[...]
        lam_v = lam_ref[...]
        sp = _softplus_neg(lam_v)
        log_a = -LRU_C * rr * sp
        a = jnp.exp(log_a)
        a_s[...] = a
        for k in range(nlb):
            lanes = slice(k * 128, (k + 1) * 128)
            gate = gate_refs[k][...].astype(F32)
            dhb = dhb_ref[:, lanes]
            dh_s[:, lanes] = dhb * _gelu(gate)
            stage[:, lanes] = (dhb * h_ref[:, lanes] * _gelu_grad(gate)).astype(BF16)
        put = _columns_copy(stage, dz_ref, (bb * nt + tb) * tt, tt, off_gate + g * gw, sem)
        put.start()
        h_before = jnp.where(tb > 0, jnp.broadcast_to(hp_ref[SUBLANES - 1:SUBLANES, :], (SUBLANES, gw)), 0.0)
        row = lax.broadcasted_iota(jnp.int32, (SUBLANES, gw), 0)

        def step(k, qcarry):
            i = n_tiles - 1 - k
            st = pl.multiple_of(i * SUBLANES, SUBLANES)
            stp = pl.multiple_of(jnp.maximum(i - 1, 0) * SUBLANES, SUBLANES)
            A = a_s[pl.ds(st, SUBLANES), :]
            hv = h_ref[pl.ds(st, SUBLANES), :]
            hprev_tile = h_ref[pl.ds(stp, SUBLANES), :]
            dh = dh_s[pl.ds(st, SUBLANES), :]
            Aq = A
            Bq = A * dh
            for d in (1, 2, 4):
                m = row < SUBLANES - d
                Bn = jnp.where(m, Aq * pltpu.roll(Bq, SUBLANES - d, 0) + Bq, Bq)
                Aq = jnp.where(m, Aq * pltpu.roll(Aq, SUBLANES - d, 0), Aq)
                Bq = Bn
            q = Bq + Aq * qcarry
            qnext = jnp.where(row == SUBLANES - 1, qcarry, pltpu.roll(q, SUBLANES - 1, 0))
            gq = dh + qnext
            hlast = jnp.where(i > 0, jnp.broadcast_to(hprev_tile[SUBLANES - 1:SUBLANES, :], (SUBLANES, gw)), h_before)
            hprev = jnp.where(row == 0, hlast, pltpu.roll(hv, 1, 0))
            G_s[pl.ds(st, SUBLANES), :] = gq
            da_s[pl.ds(st, SUBLANES), :] = gq * hprev
            return jnp.broadcast_to(q[0:1, :], (SUBLANES, gw))

        carry_ref[...] = lax.fori_loop(0, n_tiles, step, carry_ref[...], unroll=2)

        Gv = G_s[...]
        mult_raw = jnp.sqrt(_neg_expm1(2.0 * log_a))
        start = jnp.logical_and(tb == 0, lax.broadcasted_iota(jnp.int32, (tt, gw), 0) == 0)
        mult = jnp.where(start, 1.0, mult_raw)
        dmult = jnp.where(start, 0.0, Gv * ig * vv)
        di = Gv * mult * vv
        dla = da_s[...] * a - dmult * (a * a) / jnp.where(start, 1.0, mult_raw)
        dpr = dla * (-LRU_C) * sp * rr * (1.0 - rr)
        dpi = di * ig * (1.0 - ig)
        dlam_ref[...] += jnp.broadcast_to(jnp.sum(dla * (-LRU_C) * rr, axis=0, keepdims=True) * (-_sig(-lam_v)), (SUBLANES, gw))
        dba_ref[...] += jnp.broadcast_to(jnp.sum(dpr, axis=0, keepdims=True), (SUBLANES, gw))
        dbx_ref[...] += jnp.broadcast_to(jnp.sum(dpi, axis=0, keepdims=True), (SUBLANES, gw))
        dprb, dpib = dpr.astype(BF16), dpi.astype(BF16)
        nt_dims = (((1,), (1,)), ((), ()))
        dv_ref[...] = (Gv * mult * ig
                       + lax.dot_general(dprb, wa_ref[...], nt_dims, preferred_element_type=F32)
                       + lax.dot_general(dpib, wx_ref[...], nt_dims, preferred_element_type=F32))
        tn_dims = (((0,), (0,)), ((), ()))
        vb = vv.astype(BF16)
        acc_a[...] += lax.dot_general(vb, dprb, tn_dims, preferred_element_type=F32)
        acc_x[...] += lax.dot_general(vb, dpib, tn_dims, preferred_element_type=F32)

        @pl.when(jnp.logical_and(bb == n_seq - 1, t == nt - 1))
        def _():
            for hh in range(HEADS_PER_GROUP):
                dwa_ref[hh] = acc_a[hh * hd:(hh + 1) * hd, hh * hd:(hh + 1) * hd]
                dwx_ref[hh] = acc_x[hh * hd:(hh + 1) * hd, hh * hd:(hh + 1) * hd]

        put.wait()

    rowblk = lambda g, bb, t: (bb * nt + nt - 1 - t, g)
    blk = pl.BlockSpec((tt, gw), rowblk)
    before = pl.BlockSpec((SUBLANES, gw), lambda g, bb, t: (jnp.maximum((bb * nt + nt - 1 - t) * n_tiles - 1, 0), g))
    gates = [pl.BlockSpec((tt, 128), lambda g, bb, t, k=k: (bb * nt + nt - 1 - t, off_gate // 128 + g * nlb + k))
             for k in range(nlb)]
    wsp = pl.BlockSpec((None, gw, gw), lambda g, bb, t: (g, 0, 0))
    vec = pl.BlockSpec((1, gw), lambda g, bb, t: (0, g))
    acc8 = pl.BlockSpec((SUBLANES, gw), lambda g, bb, t: (0, g))
    heads = pl.BlockSpec((HEADS_PER_GROUP, hd, hd), lambda g, bb, t: (g, 0, 0))
    o8 = jax.ShapeDtypeStruct((SUBLANES, C), F32)
    ow = jax.ShapeDtypeStruct((G * HEADS_PER_GROUP, hd, hd), F32)
    scr = pltpu.VMEM((tt, gw), F32)
    return pl.pallas_call(
        body, grid=(G, n_seq, nt),
        in_specs=[blk, blk, before] + gates + [blk, blk, blk, wsp, wsp, vec, _ANY],
        out_specs=(blk, _ANY, acc8, acc8, acc8, heads, heads),
        out_shape=(jax.ShapeDtypeStruct((T, C), F32), jax.ShapeDtypeStruct(dz.shape, dz.dtype), o8, o8, o8, ow, ow),
        scratch_shapes=[scr, scr, scr, scr, pltpu.VMEM((SUBLANES, gw), F32), pltpu.VMEM((tt, gw), BF16),
                        pltpu.VMEM((gw, gw), F32), pltpu.VMEM((gw, gw), F32), pltpu.SemaphoreType.DMA],
        input_output_aliases={9 + nlb: 1},
        compiler_params=_cp("parallel", "arbitrary", "arbitrary"),
        name=name)(dhb, h, h, *([z] * nlb), r, ig, v, wa, wx, lam, dz)


def _group_weights(w):
    H, hd, _ = w.shape
    G = H // HEADS_PER_GROUP
    eye = jnp.eye(HEADS_PER_GROUP, dtype=w.dtype)
    wg = jnp.einsum("ghij,hk->ghikj", w.reshape(G, HEADS_PER_GROUP, hd, hd), eye)
    return wg.reshape(G, HEADS_PER_GROUP * hd, HEADS_PER_GROUP * hd).astype(BF16)


def _layer_fwd(x, p, *, S, fetch=None):
    D = x.shape[1]
    Dc = p["conv_a_b"].shape[1]
    Dr = p["conv_b_b"].shape[1]
    offs = dict(va=0, ga=Dc, xb=2 * Dc, gb=2 * Dc + Dr, sa=2 * Dc + 2 * Dr, sb=2 * Dc + 2 * Dr + D)
    h = _rms_fwd(x, p["g_mix"], name="rms_mix_fwd")
    if fetch is not None:
        fetch("in", h)
    z = _mm_nn(h, p["w_in"], tm=2048, tn=p["w_in"].shape[2], bias=p["b_in"], out_dtype=BF16, a_resident=True,
               name="mm_in_fwd")
    if fetch is not None:
        fetch("mix", z)
    u1 = _conv_fwd(z, p["conv_a_w"], p["conv_a_b"], S=S, off_v=offs["va"], off_g=offs["ga"], name="conv_a_fwd")
    u2 = _ln_silu_fwd(u1, p["ln_g"], p["ln_b"], name="ln_silu_fwd")
    ya = _mm_nn(u2, p["w_a_out"], tm=1024, tn=1024, name="mm_a_out_fwd")
    v0 = _conv_fwd(z, p["conv_b_w"], p["conv_b_b"], S=S, off_v=offs["xb"], off_g=None, name="conv_b_fwd")
    r, ig, hs, hb = _rglru_fwd(v0, z, p["wg_a"], p["wg_x"], p["b_rg_a"], p["b_rg_x"], p["lam"], S=S, off_gate=offs["gb"],
                               name="rglru_fwd")
    yb = _mm_nn(hb, p["w_b_out"], tm=1024, tn=1024, name="mm_b_out_fwd")
    m = _merge_fwd(z, ya, yb, off_sa=offs["sa"], off_sb=offs["sb"], name="merge_fwd")
    x_mid = _mm_nn(m, p["w_o"], tm=1024, tn=1024, resid=x, name="mm_o_fwd")
    h2 = _rms_fwd(x_mid, p["g_mlp"], name="rms_mlp_fwd")
    if fetch is not None:
        fetch("mlp", h2)
    f = _mm_nn(h2, p["w_1"], tm=2048, tn=p["w_1"].shape[2], relu2=True, out_dtype=BF16, a_resident=True,
               name="mm_1_fwd")
    x_next = _mm_nn(f, p["w_2"], tm=512, tn=1024, resid=x_mid, name="mm_2_fwd")
    saved = dict(x=x, h=h, z=z, u1=u1, u2=u2, ya=ya, v0=v0, r=r, ig=ig, hs=hs, hb=hb, yb=yb, m=m,
                 x_mid=x_mid, h2=h2, f=f, offs=offs)
    return x_next, saved


def _layer_bwd_mlp(dx, p, sv, *, wdt, dep=None):
    dx, dxb = dx
    g = {}
    g["w_2"] = _mm_tn(sv["f"], dxb, tm=1024, tn=1024, tk=4096, out_dtype=wdt, name="mm_2_wgrad")
    dfp = _mm_nt(dxb, p["w_2"], tm=1024, tn=1024, tk=1024, mul_sqrt=sv["f"], out_dtype=BF16, dep=dep,
                 name="mm_2_dgrad")
    g["w_1"] = _mm_tn(sv["h2"], dfp, tm=1024, tn=512, tk=4096, out_blocks=p["w_1"].shape[0], out_dtype=wdt,
                      name="mm_1_wgrad")
    dh2 = _mm_nt(dfp, p["w_1"], tm=512, tn=1024, tk=512, whole_b=True, name="mm_1_dgrad")
    dx_mid, dx_mid_b, g["g_mlp"] = _rms_bwd(sv["x_mid"], p["g_mlp"], dh2, dx, name="rms_mlp_bwd")
    return (dx_mid, dx_mid_b), g


def _layer_bwd_mix(dx_mid, p, sv, *, S, wdt, dep=None, on_gate_grads=None, on_weight_grads=None):
    offs = sv["offs"]
    dx_mid, dx_mid_b = dx_mid
    g = {}
    g["w_o"] = _mm_tn(sv["m"], dx_mid_b, tm=1024, tn=1024, tk=4096, out_dtype=wdt, name="mm_o_wgrad")
    dm = _mm_nt(dx_mid_b, p["w_o"], tm=1024, tn=1024, tk=1024, dep=dep, name="mm_o_dgrad")
    dz = lax.empty(sv["z"].shape, BF16)
    dya, dyb, dz = _merge_bwd(sv["z"], sv["ya"], sv["yb"], dm, dz, off_sa=offs["sa"], off_sb=offs["sb"], name="merge_bwd")
    g["w_b_out"] = _mm_tn(sv["hb"], dyb, tm=768, tn=1024, tk=4096, out_dtype=wdt, name="mm_b_out_wgrad")
    dhb = _mm_nt(dyb, p["w_b_out"], tm=1024, tn=1536, tk=1024, name="mm_b_out_dgrad")
    dv0, dz, dlam, dba, dbx, g["w_rg_a"], g["w_rg_x"] = _rglru_bwd(
        dhb, sv["hs"], sv["z"], sv["r"], sv["ig"], sv["v0"], p["wg_a"], p["wg_x"], p["lam"], dz, S=S, off_gate=offs["gb"],
        name="rglru_bwd")
    g["lam"], g["b_rg_a"], g["b_rg_x"] = dlam[:1], dba[:1], dbx[:1]
    dep_gates = on_gate_grads(g) if on_gate_grads is not None else None
    dz, g["conv_b_w"], g["conv_b_b"] = _conv_bwd(sv["z"], p["conv_b_w"], dv0, dz, S=S, off_v=offs["xb"], off_g=None,
                                                 name="conv_b_bwd")
    g["w_a_out"] = _mm_tn(sv["u2"], dya, tm=1024, tn=1024, tk=4096, out_dtype=wdt, name="mm_a_out_wgrad")
    du2 = _mm_nt(dya, p["w_a_out"], tm=1024, tn=1024, tk=1024, dep=dep_gates, name="mm_a_out_dgrad")
    du1, g["ln_g"], g["ln_b"] = _ln_silu_bwd(sv["u1"], p["ln_g"], p["ln_b"], du2, name="ln_silu_bwd")
    dz, g["conv_a_w"], g["conv_a_b"] = _conv_bwd(sv["z"], p["conv_a_w"], du1, dz, S=S, off_v=offs["va"],
                                                 off_g=offs["ga"], name="conv_a_bwd")
    g["w_in"], db_in = _mm_tn(sv["h"], dz, tm=1024, tn=512, tk=4096, out_blocks=p["w_in"].shape[0], colsum=True,
                              out_dtype=wdt, name="mm_in_wgrad")
    g["b_in"] = db_in[:1]
    dep_in = on_weight_grads(g) if on_weight_grads is not None else None
    dh = _mm_nt(dz, p["w_in"], tm=256, tn=1024, tk=512, whole_b=True, dep=dep_in, name="mm_in_dgrad")
    dx_in, dx_in_b, g["g_mix"] = _rms_bwd(sv["x"], p["g_mix"], dh, dx_mid, name="rms_mix_bwd")
    return (dx_in, dx_in_b), g


def _layer_bwd(dx, p, sv, *, S, wdt=F32):
    dx_mid, g = _layer_bwd_mlp(dx, p, sv, wdt=wdt)
    dx_in, g2 = _layer_bwd_mix(dx_mid, p, sv, S=S, wdt=wdt)
    g.update(g2)
    return dx_in, g


def _local_step(x, tgt, layers, g_final, *, S, wdt=F32):
    saved = []
    for p in layers:
        x, sv = _layer_fwd(x, p, S=S)
        saved.append(sv)
    loss, dx, dxb, dg_final = _final_loss(x, g_final, tgt, name="final_loss")
    dx = (dx, dxb)
    grads = [None] * len(layers)
    for l in reversed(range(len(layers))):
        dx, grads[l] = _layer_bwd(dx, layers[l], saved[l], S=S, wdt=wdt)
    return loss, dx[0], grads, dg_final


_HBM = pl.BlockSpec(memory_space=pltpu.HBM)
_MESH = pl.DeviceIdType.MESH


_SEM = pl.BlockSpec(memory_space=pltpu.SEMAPHORE)
_ANY = pl.BlockSpec(memory_space=pl.ANY)
_FLIPS = [(dx, dy, dc) for dx in (0, 1) for dy in (0, 1) for dc in (0, 1)][1:]


def _place(shard, me_idx, dtype, *, name, dep=None):
    r, cc = shard.shape
    tr = 512 if r % 512 == 0 else r

    def body(me_ref, s_ref, *rest):
        del me_ref
        rest[-1][...] = s_ref[...].astype(dtype)

    in_specs, args = [pl.BlockSpec((tr, cc), lambda i, me: (i, 0))], [me_idx, shard]
    if dep is not None:
        in_specs.append(pl.BlockSpec(dep.shape, lambda i, me: (0, 0)))
        args.append(dep)
    grid_spec = pltpu.PrefetchScalarGridSpec(
        num_scalar_prefetch=1, grid=(r // tr,), in_specs=in_specs,
        out_specs=pl.BlockSpec((None, tr, cc), lambda i, me: (me[0], i, 0)))
    return pl.pallas_call(body, grid_spec=grid_spec, out_shape=jax.ShapeDtypeStruct((N_DEV, r, cc), dtype),
                          compiler_params=_cp("arbitrary"), name=name)(*args)


def _exchange_copies(srcs, lands, send_sems, recv_sems):
    x, y, c = lax.axis_index("x"), lax.axis_index("y"), lax.axis_index("c")
    me = 4 * x + 2 * y + c
    pairs = []
    for k, (dx, dy, dc) in enumerate(_FLIPS):
        peer = (1 - x if dx else x, 1 - y if dy else y, 1 - c if dc else c)
        pidx = 4 * peer[0] + 2 * peer[1] + peer[2]
        for a, land in enumerate(lands):
            src = land.at[me] if srcs is None else srcs[a].at[pidx]
            sem = k * len(lands) + a
            out = pltpu.make_async_remote_copy(src_ref=src, dst_ref=land.at[me], send_sem=send_sems.at[sem],
                                               recv_sem=recv_sems.at[sem], device_id=peer, device_id_type=_MESH)
            arrival = pltpu.make_async_remote_copy(src_ref=src, dst_ref=land.at[pidx], send_sem=send_sems.at[sem],
                                                   recv_sem=recv_sems.at[sem], device_id=peer, device_id_type=_MESH)
            pairs.append((out, arrival))
    return pairs


def _exchange_start(srcs, lands, *, name):
    n = len(lands)
    bufs = list(lands) if srcs is None else list(srcs) + list(lands)
    nb = len(bufs)

    def body(*refs):
        ins = refs[:nb]
        send_sems, recv_sems = refs[nb], refs[nb + 1]
        token = refs[-1]
        for out, _ in _exchange_copies(None if srcs is None else ins[:n], ins[nb - n:], send_sems, recv_sems):
            out.start()
        token[...] = jnp.zeros_like(token)

    sems = pltpu.SemaphoreType.DMA((len(_FLIPS) * n,))
    res = pl.pallas_call(
        body, name=name, in_specs=[_HBM] * nb,
        out_shape=(sems, sems, *[pltpu.HBM(b.shape, b.dtype) for b in bufs], jax.ShapeDtypeStruct((SUBLANES, 128), F32)),
        out_specs=(_SEM, _SEM, *[_HBM] * nb, pl.BlockSpec(memory_space=pltpu.VMEM)),
        input_output_aliases={i: 2 + i for i in range(nb)},
        compiler_params=pltpu.CompilerParams(has_side_effects=pltpu.SideEffectType.DATAFLOW_SIDE_EFFECTING),
    )(*[pltpu.with_memory_space_constraint(b, pltpu.HBM) for b in bufs])
    return res[0], res[1], list(res[2:2 + nb]), res[-1]


def _exchange_wait(send_sems, recv_sems, bufs, *, scatter, after, name):
    nb = len(bufs)
    n = nb // 2 if scatter else nb

    def body(*refs):
        ins = refs[:nb]
        for out, arrival in _exchange_copies(ins[:n] if scatter else None, ins[nb - n:], refs[nb], refs[nb + 1]):
            out.wait_send()
            arrival.wait_recv()

    extra = [] if after is None else [after]
    res = pl.pallas_call(
        body, name=name, in_specs=[_HBM] * nb + [_SEM, _SEM] + [_ANY] * len(extra),
        out_shape=tuple(pltpu.HBM(b.shape, b.dtype) for b in bufs), out_specs=tuple([_HBM] * nb),
        input_output_aliases={i: i for i in range(nb)},
        compiler_params=pltpu.CompilerParams(has_side_effects=pltpu.SideEffectType.DATAFLOW_SIDE_EFFECTING),
    )(*bufs, send_sems, recv_sems, *extra)
    return list(res)


def _adamw_math(w, g, m, v):
    m = ADAM_B1 * m + (1.0 - ADAM_B1) * g
    v = ADAM_B2 * v + (1.0 - ADAM_B2) * (g * g)
    m_hat = m / (1.0 - ADAM_B1 ** ADAM_STEP)
    v_hat = v / (1.0 - ADAM_B2 ** ADAM_STEP)
    delta = -ADAM_LR * (m_hat / (jnp.sqrt(v_hat) + ADAM_EPS) + ADAM_WD * w)
    return delta, m, v


def _adamw(w, m, v, parts, prev, layer, me_idx, *, name, own=None):
    L, r, cc = w.shape
    P = parts.shape[0]
    tr = 512 if r % 512 == 0 else r
    if prev is None:
        prev = tuple(lax.empty(w.shape, F32) for _ in range(4))

    def body(me_ref, w_ref, m_ref, v_ref, p_ref, *rest):
        g_ref, d_ref, nm_ref, nv_ref = rest[-4:]
        if own is None:
            g = p_ref[0].astype(F32)
            for q in range(1, P):
                g = g + p_ref[q].astype(F32)
        else:
            me = me_ref[0]
            g = rest[0][...].astype(F32)
            for q in range(P):
                g = g + jnp.where(q == me, 0.0, p_ref[q].astype(F32))
        d, nm, nv = _adamw_math(w_ref[...], g, m_ref[...], v_ref[...])
        g_ref[...] = g
        d_ref[...] = d
        nm_ref[...] = nm
        nv_ref[...] = nv

    blk = pl.BlockSpec((None, tr, cc), lambda i, me: (layer, i, 0))
    in_specs = [blk, blk, blk, pl.BlockSpec((P, tr, cc), lambda i, me: (0, i, 0))]
    args = [me_idx, w, m, v, parts]
    if own is not None:
        in_specs.append(pl.BlockSpec((None, tr, cc), lambda i, me: (me[0], i, 0)))
        args.append(own)
    first_prev = len(args)
    in_specs += [_ANY] * 4
    args += list(prev)
    grid_spec = pltpu.PrefetchScalarGridSpec(num_scalar_prefetch=1, grid=(r // tr,), in_specs=in_specs,
                                             out_specs=(blk, blk, blk, blk))
    o = jax.ShapeDtypeStruct(w.shape, F32)
    return pl.pallas_call(body, grid_spec=grid_spec, out_shape=(o, o, o, o),
                          input_output_aliases={first_prev + i: i for i in range(4)},
                          compiler_params=_cp("parallel"), name=name)(*args)


_SHARDED = ("w_in", "conv_a_w", "w_a_out", "conv_b_w", "w_b_out", "w_o", "w_1", "w_2")
_COL_SHARDED = ("w_in", "conv_a_w", "conv_b_w", "w_1")
_REPLICATED = ("g_mix", "b_in", "conv_a_b", "ln_g", "ln_b", "conv_b_b", "w_rg_a", "b_rg_a", "w_rg_x", "b_rg_x", "lam",
               "g_mlp")
_WEIGHTS = ("g_mix", "w_in", "b_in", "conv_a_w", "conv_a_b", "ln_g", "ln_b", "w_a_out", "conv_b_w", "conv_b_b", "w_rg_a",
            "b_rg_a", "w_rg_x", "b_rg_x", "lam", "w_b_out", "w_o", "g_mlp", "w_1", "w_2", "g_final")
_LANES = 128


def _cols_from_blocks(b):
    nb, K, n = b.shape
    return b.transpose(1, 0, 2).reshape(K, nb * n)


def _blocks_from_cols(w, K):
    n = w.shape[1] // N_DEV
    return w[:K].reshape(K, N_DEV, n).transpose(1, 0, 2)


def kernel(x, g_mix, w_in, b_in, conv_a_w, conv_a_b, ln_g, ln_b, w_a_out, conv_b_w, conv_b_b, w_rg_a, b_rg_a, w_rg_x, b_rg_x, lam, w_b_out, w_o, g_mlp, w_1, w_2, g_final, loss_target, m_g_mix, m_w_in, m_b_in, m_conv_a_w, m_conv_a_b, m_ln_g, m_ln_b, m_w_a_out, m_conv_b_w, m_conv_b_b, m_w_rg_a, m_b_rg_a, m_w_rg_x, m_b_rg_x, m_lam, m_w_b_out, m_w_o, m_g_mlp, m_w_1, m_w_2, m_g_final, v_g_mix, v_w_in, v_b_in, v_conv_a_w, v_conv_a_b, v_ln_g, v_ln_b, v_w_a_out, v_conv_b_w, v_conv_b_b, v_w_rg_a, v_b_rg_a, v_w_rg_x, v_b_rg_x, v_lam, v_w_b_out, v_w_o, v_g_mlp, v_w_1, v_w_2, v_g_final):
    W = dict(g_mix=g_mix, w_in=w_in, b_in=b_in, conv_a_w=conv_a_w, conv_a_b=conv_a_b, ln_g=ln_g, ln_b=ln_b,
             w_a_out=w_a_out, conv_b_w=conv_b_w, conv_b_b=conv_b_b, w_rg_a=w_rg_a, b_rg_a=b_rg_a, w_rg_x=w_rg_x,
             b_rg_x=b_rg_x, lam=lam, w_b_out=w_b_out, w_o=w_o, g_mlp=g_mlp, w_1=w_1, w_2=w_2, g_final=g_final)
    M = dict(g_mix=m_g_mix, w_in=m_w_in, b_in=m_b_in, conv_a_w=m_conv_a_w, conv_a_b=m_conv_a_b, ln_g=m_ln_g, ln_b=m_ln_b,
             w_a_out=m_w_a_out, conv_b_w=m_conv_b_w, conv_b_b=m_conv_b_b, w_rg_a=m_w_rg_a, b_rg_a=m_b_rg_a,
             w_rg_x=m_w_rg_x, b_rg_x=m_b_rg_x, lam=m_lam, w_b_out=m_w_b_out, w_o=m_w_o, g_mlp=m_g_mlp, w_1=m_w_1,
             w_2=m_w_2, g_final=m_g_final)
    V = dict(g_mix=v_g_mix, w_in=v_w_in, b_in=v_b_in, conv_a_w=v_conv_a_w, conv_a_b=v_conv_a_b, ln_g=v_ln_g, ln_b=v_ln_b,
             w_a_out=v_w_a_out, conv_b_w=v_conv_b_w, conv_b_b=v_conv_b_b, w_rg_a=v_w_rg_a, b_rg_a=v_b_rg_a,
             w_rg_x=v_w_rg_x, b_rg_x=v_b_rg_x, lam=v_lam, w_b_out=v_w_b_out, w_o=v_w_o, g_mlp=v_g_mlp, w_1=v_w_1,
             w_2=v_w_2, g_final=v_g_final)
    NB, S, D = x.shape
    L = g_mix.shape[0]
    hd = w_rg_a.shape[-1]
    me_idx = (4 * lax.axis_index("x") + 2 * lax.axis_index("y") + lax.axis_index("c")).astype(jnp.int32).reshape(1)

    stages = (("in", ("w_in",)), ("mix", ("conv_a_w", "w_a_out", "conv_b_w", "w_b_out", "w_o")), ("mlp", ("w_1", "w_2")))
    gathers = {}
    started = jnp.zeros((), F32)
    first = None
    for l in range(L):
        for stage, names in stages:
            lands = [_place(W[k][l], me_idx, F32 if k.startswith("conv") else BF16, dep=first, name="place_" + k)
                     for k in names]
            send_sems, recv_sems, bufs, token = _exchange_start(None, lands, name=f"weights_start_{stage}_{l}")
            gathers[l, stage] = (names, send_sems, recv_sems, bufs)
            started = started + token[0, 0]
            if first is None:
                first = token

    xt = x.reshape(NB * S, D)
    layers, saved = [], []
    for l in range(L):
        p = {k: W[k][l][None] for k in ("g_mix", "b_in", "conv_a_b", "ln_g", "ln_b", "conv_b_b", "b_rg_a", "b_rg_x",
                                         "lam", "g_mlp")}
        if l == 0:
            p["g_mix"] = p["g_mix"] + started
        p["wg_a"] = _group_weights(w_rg_a[l])
        p["wg_x"] = _group_weights(w_rg_x[l])

        def fetch(stage, after, l=l, p=p):
            names, send_sems, recv_sems, bufs = gathers[l, stage]
            bufs = _exchange_wait(send_sems, recv_sems, bufs, scatter=False, after=after,
                                  name=f"weights_wait_{stage}_{l}")
            for k, full in zip(names, bufs):
                if k in ("w_in", "w_1"):
                    p[k] = full
                elif k in _COL_SHARDED:
                    p[k] = _cols_from_blocks(full)
                else:
                    p[k] = full.reshape(-1, full.shape[-1])

        layers.append(p)
        xt, sv = _layer_fwd(xt, p, S=S, fetch=fetch)
        saved.append(sv)
    loss, dx, dxb, dg_final = _final_loss(xt, g_final[None], loss_target.reshape(NB * S, D), name="final_loss")
    dx = (dx, dxb)

    results = {k: None for k in _SHARDED}

    def scatter_start(l, names, g, tag):
        srcs = []
        for k in names:
            shard_shape = W[k].shape[1:]
            if k in ("w_in", "w_1"):
                srcs.append(g[k])
            elif k in _COL_SHARDED:
                srcs.append(_blocks_from_cols(g[k], shard_shape[0]).astype(BF16))
            else:
                srcs.append(g[k].reshape((N_DEV,) + shard_shape))
        lands = [lax.empty(s.shape, BF16) for s in srcs]
        send_sems, recv_sems, bufs, token = _exchange_start(srcs, lands, name=f"grads_start_{tag}_{l}")
        return (names, send_sems, recv_sems, bufs, tag), token

    def scatter_finish(l, flight, after):
        names, send_sems, recv_sems, bufs, tag = flight
        bufs = _exchange_wait(send_sems, recv_sems, bufs, scatter=True, after=after, name=f"grads_wait_{tag}_{l}")
        n = len(names)
        for k, own, land in zip(names, bufs[:n], bufs[n:]):
            results[k] = _adamw(W[k], M[k], V[k], land, results[k], l, me_idx, own=own, name="adamw_" + k)

    gate_names = ("w_rg_a", "w_rg_x")
    gate_results = {k: None for k in gate_names}

    def small_start(l, g):
        lands = [_place(g[k].reshape(-1, hd), me_idx, F32, name="place_gate_grad") for k in gate_names]
        send_sems, recv_sems, bufs, token = _exchange_start(None, lands, name=f"small_start_{l}")
        return (send_sems, recv_sems, bufs), token

    def small_finish(l, flight, after):
        send_sems, recv_sems, bufs = flight
        bufs = _exchange_wait(send_sems, recv_sems, bufs, scatter=False, after=after, name=f"small_wait_{l}")
        for k, g_all in zip(gate_names, bufs):
            gate_results[k] = _adamw(W[k].reshape(L, -1, hd), M[k].reshape(L, -1, hd), V[k].reshape(L, -1, hd), g_all,
                                     gate_results[k], l, me_idx, name="adamw_gate")

    grads = [None] * L
    in_flight = []
    dep = None
    for l in reversed(range(L)):
        dx_mid, g = _layer_bwd_mlp(dx, layers[l], saved[l], wdt=BF16, dep=dep)
        f_mlp, dep = scatter_start(l, ("w_2", "w_1"), g, "mlp")
        flights = [f_mlp]

        small = []

        def on_gate_grads(g_part, l=l, small=small):
            f_small, token = small_start(l, g_part)
            small.append(f_small)
            return token

        def on_weight_grads(g_part, l=l, flights=flights):
            f_mix, token = scatter_start(l, ("w_o", "w_b_out", "w_a_out", "conv_a_w", "conv_b_w", "w_in"), g_part, "mix")
            flights.append(f_mix)
            return token

        dx, g_mix_part = _layer_bwd_mix(dx_mid, layers[l], saved[l], S=S, wdt=BF16, dep=dep,
                                        on_gate_grads=on_gate_grads, on_weight_grads=on_weight_grads)
        dep = None
        g.update(g_mix_part)
        grads[l] = g
        for l_prev, fs, f_sm in in_flight:
            for flight in fs:
                scatter_finish(l_prev, flight, dx[0])
            small_finish(l_prev, f_sm, dx[0])
        in_flight = [(l, flights, small[0])]

    vec_names = tuple(k for k in _REPLICATED if k not in gate_names)
    n_vec = sum(W[k].shape[1] for k in vec_names)

    def vec_pack(rows, final, last):
        tail = jnp.concatenate([final.reshape(1, -1), jnp.broadcast_to(last.reshape(1, 1), (1, _LANES))], axis=1)
        tail = jnp.pad(tail, ((0, SUBLANES - L - 1), (0, n_vec - tail.shape[1])))
        return jnp.concatenate([rows, tail], axis=0)

    g_rows = jnp.concatenate([jnp.concatenate([grads[l][k] for k in vec_names], axis=1) for l in range(L)], axis=0)
    land = _place(vec_pack(g_rows, dg_final, loss[0, :1]), me_idx, F32, name="place_vectors")
    vec_send_sems, vec_recv_sems, vec_bufs, _ = _exchange_start(None, [land], name="vectors_start")
    for l_prev, fs, f_sm in in_flight:
        after = dx[0] if L == 1 else results["w_in"][0]
        for flight in fs:
            scatter_finish(l_prev, flight, after)
            after = results[flight[0][-1]][0]
        small_finish(l_prev, f_sm, after)
    out_g, out_d, out_m, out_v = {}, {}, {}, {}
    for k in _SHARDED:
        out_g[k], out_d[k], out_m[k], out_v[k] = results[k]
    for k in gate_names:
        out_g[k], out_d[k], out_m[k], out_v[k] = (a.reshape(W[k].shape) for a in gate_results[k])

    zero = jnp.zeros((1,), F32)
    (g_all,) = _exchange_wait(vec_send_sems, vec_recv_sems, vec_bufs, scatter=False, after=results["w_in"][0],
                              name="vectors_wait")
    vec_out = _adamw(vec_pack(jnp.concatenate([W[k] for k in vec_names], axis=1), g_final, zero)[None],
                     vec_pack(jnp.concatenate([M[k] for k in vec_names], axis=1), m_g_final, zero)[None],
                     vec_pack(jnp.concatenate([V[k] for k in vec_names], axis=1), v_g_final, zero)[None],
                     g_all, None, 0, me_idx, name="adamw_vectors")
    vec_out = [a[0] for a in vec_out]
    for res, arr in zip((out_g, out_d, out_m, out_v), vec_out):
        off = 0
        for k in vec_names:
            res[k] = arr[:L, off:off + W[k].shape[1]]
            off += W[k].shape[1]
        res["g_final"] = arr[L, :g_final.size]
    loss_out = vec_out[0][L, g_final.size]

    return (loss_out, dx[0].reshape(NB, S, D), *[out_g[k] for k in _WEIGHTS], *[out_d[k] for k in _WEIGHTS],
            *[out_m[k] for k in _WEIGHTS], *[out_v[k] for k in _WEIGHTS])
```

```python
import functools

import jax
import jax.numpy as jnp
from jax import lax
from jax.experimental import pallas as pl
from jax.experimental.pallas import tpu as pltpu

F32 = jnp.float32
BF16 = jnp.bfloat16

EPS = 1e-6
LRU_C = 8.0
N_RNN_HEADS = 16
HEADS_PER_GROUP = 4
N_DEV = 8
ADAM_LR, ADAM_B1, ADAM_B2, ADAM_EPS, ADAM_WD, ADAM_STEP = 0.001, 0.9, 0.999, 1e-08, 0.01, 10

VMEM_LIMIT_BYTES = 48 * 1024 * 1024
CONV_PAD = 32
CONV_CHUNK = 128
SUBLANES = 8


def _cp(*sem):
    return pltpu.CompilerParams(dimension_semantics=sem, vmem_limit_bytes=VMEM_LIMIT_BYTES)


def _sig(x):
    return 1.0 / (1.0 + jnp.exp(-x))


def _gelu(x):
    c = 0.7978845608028654
    return 0.5 * x * (1.0 + jnp.tanh(c * (x + 0.044715 * x * x * x)))


def _gelu_grad(x):
    c = 0.7978845608028654
    th = jnp.tanh(c * (x + 0.044715 * x * x * x))
    return 0.5 * (1.0 + th) + 0.5 * x * (1.0 - th * th) * c * (1.0 + 3.0 * 0.044715 * x * x)


def _mm_nn(a, b, *, tm, tn, name, bias=None, resid=None, relu2=False, out_dtype=F32, a_resident=False):
    M, K = a.shape
    blocked = b.ndim == 3
    N = b.shape[0] * b.shape[2] if blocked else b.shape[1]
    tm = min(tm, M)
    tn = min(tn, N)
    if blocked:
        assert tn == b.shape[2]
    n_extra = (bias is not None) + (resid is not None)

    def body(*refs):
        acc = jnp.dot(refs[0][...].astype(BF16), refs[1][...].astype(BF16), preferred_element_type=F32)
        k = 2
        if bias is not None:
            acc = acc + refs[k][...]
            k += 1
        if resid is not None:
            acc = acc + refs[k][...]
            k += 1
        if relu2:
            p = jnp.maximum(acc, 0.0)
            acc = p * p
        refs[k][...] = acc.astype(out_dtype)

    def spec(shape, index):
        return pl.BlockSpec(shape, (lambda i, j: index(j, i)) if a_resident else index)

    in_specs = [spec((tm, K), lambda j, i: (i, 0))]
    if blocked:
        in_specs.append(spec((None, K, tn), lambda j, i: (j, 0, 0)))
    else:
        in_specs.append(spec((K, tn), lambda j, i: (0, j)))
    args = [a, b]
    if bias is not None:
        in_specs.append(spec((1, tn), lambda j, i: (0, j)))
        args.append(bias)
    if resid is not None:
        in_specs.append(spec((tm, tn), lambda j, i: (i, j)))
        args.append(resid)
    out_specs = spec((tm, tn), lambda j, i: (i, j))
    out_shape = jax.ShapeDtypeStruct((M, N), out_dtype)
    del n_extra
    grid = (M // tm, N // tn) if a_resident else (N // tn, M // tm)
    return pl.pallas_call(body, grid=grid, in_specs=in_specs, out_specs=out_specs,
                          out_shape=out_shape, compiler_params=_cp("parallel", "parallel"), name=name)(*args)


def _mm_nt(a, b, *, tm, tn, tk, name, mul_sqrt=None, resid=None, out_dtype=F32, dep=None, whole_b=False):
    M, N = a.shape
    blocked = b.ndim == 3
    Kout = b.shape[1] if blocked else b.shape[0]
    tm = min(tm, M)
    tn = min(tn, Kout)
    tk = N if whole_b else (b.shape[2] if blocked else min(tk, N))
    nk = N // tk
    nt_dims = (((1,), (1,)), ((), ()))

    def body(*refs):
        acc_ref = refs[-1]
        kk = pl.program_id(2)
        if blocked and whole_b:
            n = b.shape[2]
            part = None
            for jb in range(b.shape[0]):
                pj = lax.dot_general(refs[0][:, jb * n:(jb + 1) * n].astype(BF16), refs[1][jb].astype(BF16), nt_dims,
                                     preferred_element_type=F32)
                part = pj if part is None else part + pj
        else:
            part = lax.dot_general(refs[0][...].astype(BF16), refs[1][...].astype(BF16), nt_dims,
                                   preferred_element_type=F32)

        def finish(acc):
            k = 2
            if mul_sqrt is not None:
                acc = acc * (2.0 * jnp.sqrt(refs[k][...].astype(F32)))
                k += 1
            if resid is not None:
                acc = acc + refs[k][...]
                k += 1
            if dep is not None:
                k += 1
            refs[k][...] = acc.astype(out_dtype)

        if nk == 1:
            finish(part)
            return

        @pl.when(kk == 0)
        def _():
            acc_ref[...] = part

        @pl.when(kk > 0)
        def _():
            acc_ref[...] += part

        @pl.when(kk == nk - 1)
        def _():
            finish(acc_ref[...])

    in_specs = [pl.BlockSpec((tm, tk), lambda i, j, k: (i, k))]
    if blocked and whole_b:
        in_specs.append(pl.BlockSpec((b.shape[0], tn, b.shape[2]), lambda i, j, k: (0, j, 0)))
    elif blocked:
        in_specs.append(pl.BlockSpec((None, tn, tk), lambda i, j, k: (k, j, 0)))
    else:
        in_specs.append(pl.BlockSpec((tn, tk), lambda i, j, k: (j, k)))
    args = [a, b]
    for extra in (mul_sqrt, resid):
        if extra is not None:
            in_specs.append(pl.BlockSpec((tm, tn), lambda i, j, k: (i, j)))
            args.append(extra)
    if dep is not None:
        in_specs.append(pl.BlockSpec(dep.shape, lambda i, j, k: (0, 0)))
        args.append(dep)
    return pl.pallas_call(body, grid=(M // tm, Kout // tn, nk), in_specs=in_specs,
                          out_specs=pl.BlockSpec((tm, tn), lambda i, j, k: (i, j)),
                          out_shape=jax.ShapeDtypeStruct((M, Kout), out_dtype),
                          scratch_shapes=[pltpu.VMEM((tm, tn), F32)] if nk > 1 else [],
                          compiler_params=_cp("parallel", "parallel", "arbitrary"), name=name)(*args)


def _mm_tn(a, b, *, tm, tn, tk, name, out_blocks=None, colsum=False, out_dtype=F32):
    T, M = a.shape
    N = b.shape[1]
    tm = min(tm, M)
    tk = min(tk, T)
    if out_blocks is not None:
        tn = N // out_blocks
        tm = M
    tn = min(tn, N)
    nk = T // tk
    if colsum:
        assert tm == M

    def body(*refs):
        a_ref, b_ref, o_ref, acc_ref = refs[0], refs[1], refs[2], refs[-1]
        kk = pl.program_id(2)
        bv = b_ref[...]
        part = lax.dot_general(a_ref[...].astype(BF16), bv.astype(BF16),
                               (((0,), (0,)), ((), ())), preferred_element_type=F32)

        if colsum:
            csum = jnp.broadcast_to(jnp.sum(bv.astype(F32), axis=0, keepdims=True), (SUBLANES, tn))

        if nk == 1:
            o_ref[...] = part.astype(out_dtype)
            if colsum:
                refs[3][...] = csum
            return

        @pl.when(kk == 0)
        def _():
            acc_ref[...] = part
            if colsum:
                refs[3][...] = csum

        @pl.when(kk > 0)
        def _():
            acc_ref[...] += part
            if colsum:
                refs[3][...] += csum

        @pl.when(kk == nk - 1)
        def _():
            o_ref[...] = acc_ref[...].astype(out_dtype)

    in_specs = [pl.BlockSpec((tk, tm), lambda i, j, k: (k, i)), pl.BlockSpec((tk, tn), lambda i, j, k: (k, j))]
    if out_blocks is not None:
        o_shape = jax.ShapeDtypeStruct((out_blocks, M, tn), out_dtype)
        o_spec = pl.BlockSpec((None, M, tn), lambda i, j, k: (j, 0, 0))
    else:
        o_shape = jax.ShapeDtypeStruct((M, N), out_dtype)
        o_spec = pl.BlockSpec((tm, tn), lambda i, j, k: (i, j))
    if colsum:
        out_shape = (o_shape, jax.ShapeDtypeStruct((SUBLANES, N), F32))
        out_specs = (o_spec, pl.BlockSpec((SUBLANES, tn), lambda i, j, k: (0, j)))
    else:
        out_shape, out_specs = o_shape, o_spec
    return pl.pallas_call(body, grid=(M // tm, N // tn, nk), in_specs=in_specs, out_specs=out_specs,
                          out_shape=out_shape, scratch_shapes=[pltpu.VMEM((tm, tn), F32)] if nk > 1 else [],
                          compiler_params=_cp("parallel", "parallel", "arbitrary"), name=name)(a, b)


def _rms_fwd(x, g, *, name, tr=512):
    T, D = x.shape
    tr = min(tr, T)

    def body(x_ref, g_ref, h_ref):
        xv = x_ref[...]
        r = lax.rsqrt(jnp.mean(xv * xv, axis=-1, keepdims=True) + EPS)
        h_ref[...] = (xv * r * g_ref[...]).astype(BF16)

    return pl.pallas_call(body, grid=(T // tr,),
                          in_specs=[pl.BlockSpec((tr, D), lambda i: (i, 0)), pl.BlockSpec((1, D), lambda i: (0, 0))],
                          out_specs=pl.BlockSpec((tr, D), lambda i: (i, 0)),
                          out_shape=jax.ShapeDtypeStruct((T, D), BF16), compiler_params=_cp("parallel"), name=name)(x, g)


def _rms_bwd(x, g, dh, dres, *, name, tr=512):
    T, D = x.shape
    tr = min(tr, T)

    def body(x_ref, g_ref, dh_ref, dres_ref, dx_ref, dxb_ref, dg_ref):
        xv = x_ref[...]
        r = lax.rsqrt(jnp.mean(xv * xv, axis=-1, keepdims=True) + EPS)
        n = xv * r
        dh = dh_ref[...]
        dn = dh * g_ref[...]
        dx = dres_ref[...] + r * (dn - n * jnp.mean(dn * n, axis=-1, keepdims=True))
        dx_ref[...] = dx
        dxb_ref[...] = dx.astype(BF16)
        part = jnp.sum(dh * n, axis=0, keepdims=True)

        @pl.when(pl.program_id(0) == 0)
        def _():
            dg_ref[...] = part

        @pl.when(pl.program_id(0) > 0)
        def _():
            dg_ref[...] += part

    row = pl.BlockSpec((tr, D), lambda i: (i, 0))
    vec = pl.BlockSpec((1, D), lambda i: (0, 0))
    return pl.pallas_call(body, grid=(T // tr,), in_specs=[row, vec, row, row], out_specs=(row, row, vec),
                          out_shape=(jax.ShapeDtypeStruct((T, D), F32), jax.ShapeDtypeStruct((T, D), BF16),
                                     jax.ShapeDtypeStruct((1, D), F32)),
                          compiler_params=_cp("arbitrary"), name=name)(x, g, dh, dres)


def _final_loss(x, g, tgt, *, name, tr=512):
    T, D = x.shape
    tr = min(tr, T)

    def body(x_ref, g_ref, t_ref, loss_ref, dx_ref, dxb_ref, dg_ref):
        xv = x_ref[...]
        gv = g_ref[...]
        r = lax.rsqrt(jnp.mean(xv * xv, axis=-1, keepdims=True) + EPS)
        n = xv * r
        e = n * gv - t_ref[...]
        lpart = 0.5 * jnp.sum(jnp.mean(e * e, axis=-1, keepdims=True), axis=0, keepdims=True)
        dy = e * (1.0 / D)
        dn = dy * gv
        dx = r * (dn - n * jnp.mean(dn * n, axis=-1, keepdims=True))
        dx_ref[...] = dx
        dxb_ref[...] = dx.astype(BF16)
        gpart = jnp.sum(dy * n, axis=0, keepdims=True)

        @pl.when(pl.program_id(0) == 0)
        def _():
            dg_ref[...] = gpart
            loss_ref[...] = jnp.broadcast_to(lpart, (1, 128))

        @pl.when(pl.program_id(0) > 0)
        def _():
            dg_ref[...] += gpart
            loss_ref[...] += jnp.broadcast_to(lpart, (1, 128))

    row = pl.BlockSpec((tr, D), lambda i: (i, 0))
    vec = pl.BlockSpec((1, D), lambda i: (0, 0))
    return pl.pallas_call(body, grid=(T // tr,), in_specs=[row, vec, row],
                          out_specs=(pl.BlockSpec((1, 128), lambda i: (0, 0)), row, row, vec),
                          out_shape=(jax.ShapeDtypeStruct((1, 128), F32), jax.ShapeDtypeStruct((T, D), F32),
                                     jax.ShapeDtypeStruct((T, D), BF16), jax.ShapeDtypeStruct((1, D), F32)),
                          compiler_params=_cp("arbitrary"), name=name)(x, g, tgt)


def _ln_silu_fwd(u, g, b, *, name, tr=512):
    T, C = u.shape
    tr = min(tr, T)

    def body(u_ref, g_ref, b_ref, o_ref):
        uv = u_ref[...]
        mu = jnp.mean(uv, axis=-1, keepdims=True)
        xc = uv - mu
        r = lax.rsqrt(jnp.mean(xc * xc, axis=-1, keepdims=True) + EPS)
        y = xc * r * g_ref[...] + b_ref[...]
        o_ref[...] = (y * _sig(y)).astype(BF16)

    row = pl.BlockSpec((tr, C), lambda i: (i, 0))
    vec = pl.BlockSpec((1, C), lambda i: (0, 0))
    return pl.pallas_call(body, grid=(T // tr,), in_specs=[row, vec, vec], out_specs=row,
                          out_shape=jax.ShapeDtypeStruct((T, C), BF16), compiler_params=_cp("parallel"),
                          name=name)(u, g, b)


def _ln_silu_bwd(u, g, b, do, *, name, tr=512):
    T, C = u.shape
    tr = min(tr, T)

    def body(u_ref, g_ref, b_ref, do_ref, du_ref, dg_ref, db_ref):
        uv = u_ref[...]
        gv = g_ref[...]
        mu = jnp.mean(uv, axis=-1, keepdims=True)
        xc = uv - mu
        r = lax.rsqrt(jnp.mean(xc * xc, axis=-1, keepdims=True) + EPS)
        n = xc * r
        y = n * gv + b_ref[...]
        s = _sig(y)
        dy = do_ref[...] * (s * (1.0 + y * (1.0 - s)))
        dn = dy * gv
        du_ref[...] = r * (dn - jnp.mean(dn, axis=-1, keepdims=True) - n * jnp.mean(dn * n, axis=-1, keepdims=True))
        gpart = jnp.sum(dy * n, axis=0, keepdims=True)
        bpart = jnp.sum(dy, axis=0, keepdims=True)

        @pl.when(pl.program_id(0) == 0)
        def _():
            dg_ref[...] = gpart
            db_ref[...] = bpart

        @pl.when(pl.program_id(0) > 0)
        def _():
            dg_ref[...] += gpart
            db_ref[...] += bpart

    row = pl.BlockSpec((tr, C), lambda i: (i, 0))
    vec = pl.BlockSpec((1, C), lambda i: (0, 0))
    return pl.pallas_call(body, grid=(T // tr,), in_specs=[row, vec, vec, row], out_specs=(row, vec, vec),
                          out_shape=(jax.ShapeDtypeStruct((T, C), F32), jax.ShapeDtypeStruct((1, C), F32),
                                     jax.ShapeDtypeStruct((1, C), F32)),
                          compiler_params=_cp("arbitrary"), name=name)(u, g, b, do)


def _merge_fwd(z, ya, yb, *, off_sa, off_sb, name, tr=512):
    T, D = ya.shape
    tr = min(tr, T)
    assert off_sa % D == 0 and off_sb % D == 0

    def body(sa_ref, sb_ref, ya_ref, yb_ref, m_ref):
        m_ref[...] = (_sig(sa_ref[...].astype(F32)) * ya_ref[...]
                      + _sig(sb_ref[...].astype(F32)) * yb_ref[...]).astype(BF16)

    row = pl.BlockSpec((tr, D), lambda i: (i, 0))
    return pl.pallas_call(body, grid=(T // tr,),
                          in_specs=[pl.BlockSpec((tr, D), lambda i: (i, off_sa // D)),
                                    pl.BlockSpec((tr, D), lambda i: (i, off_sb // D)), row, row],
                          out_specs=row, out_shape=jax.ShapeDtypeStruct((T, D), BF16),
                          compiler_params=_cp("parallel"), name=name)(z, z, ya, yb)


def _columns_copy(stage_ref, dz_ref, row0, rows, col0, sem):
    dst = dz_ref.at[pl.ds(pl.multiple_of(row0, SUBLANES), rows),
                    pl.ds(pl.multiple_of(col0, 128), stage_ref.shape[1])]
    return pltpu.make_async_copy(stage_ref, dst, sem)


def _put_columns(stage_ref, dz_ref, row0, rows, col0, sem):
    cp = _columns_copy(stage_ref, dz_ref, row0, rows, col0, sem)
    cp.start()
    cp.wait()


def _merge_bwd(z, ya, yb, dm, dz, *, off_sa, off_sb, name, tr=512):
    T, D = ya.shape
    tr = min(tr, T)
    assert off_sb == off_sa + D

    def body(sa_ref, sb_ref, ya_ref, yb_ref, dm_ref, dz_in, dya_ref, dyb_ref, dz_ref, stage, sem):
        del dz_in
        dm = dm_ref[...]
        ga = _sig(sa_ref[...].astype(F32))
        gb = _sig(sb_ref[...].astype(F32))
        dya_ref[...] = (dm * ga).astype(BF16)
        dyb_ref[...] = (dm * gb).astype(BF16)
        stage[:, 0:D] = (dm * ya_ref[...] * ga * (1.0 - ga)).astype(BF16)
        stage[:, D:2 * D] = (dm * yb_ref[...] * gb * (1.0 - gb)).astype(BF16)
        _put_columns(stage, dz_ref, pl.program_id(0) * tr, tr, off_sa, sem)

    row = pl.BlockSpec((tr, D), lambda i: (i, 0))
    o = jax.ShapeDtypeStruct((T, D), BF16)
    return pl.pallas_call(body, grid=(T // tr,),
                          in_specs=[pl.BlockSpec((tr, D), lambda i: (i, off_sa // D)),
                                    pl.BlockSpec((tr, D), lambda i: (i, off_sb // D)), row, row, row, _ANY],
                          out_specs=(row, row, _ANY), out_shape=(o, o, jax.ShapeDtypeStruct(dz.shape, dz.dtype)),
                          scratch_shapes=[pltpu.VMEM((tr, 2 * D), BF16), pltpu.SemaphoreType.DMA],
                          input_output_aliases={5: 2},
                          compiler_params=_cp("parallel"), name=name)(z, z, ya, yb, dm, dz)


def _shift_rows(dst_ref, src_ref, r, total, back):
    for c0 in range(0, total - SUBLANES, CONV_CHUNK):
        n = min(CONV_CHUNK, total - SUBLANES - c0)
        if back:
            dst_ref[SUBLANES + c0:SUBLANES + c0 + n, :] = src_ref[SUBLANES + c0 - r:SUBLANES + c0 - r + n, :]
        else:
            dst_ref[c0:c0 + n, :] = src_ref[c0 + r:c0 + r + n, :]


def _tap_plan(K):
    if K <= SUBLANES:
        return [(0, [(s, K - 1 - s) for s in range(K)])]
    return [(r, [(SUBLANES * q, K - 1 - (SUBLANES * q + r)) for q in range(-(-K // SUBLANES)) if SUBLANES * q + r < K])
            for r in range(SUBLANES)]


def _conv_fwd(z, w, b, *, S, off_v, off_g, name, ct=256):
    T = z.shape[0]
    K, C = w.shape
    ct = min(ct, C)
    ch = min(CONV_CHUNK, S)
    glu = off_g is not None
    assert off_v % ct == 0 and (not glu or off_g % ct == 0)
    assert SUBLANES * ((K - 1) // SUBLANES) <= CONV_PAD - SUBLANES

    def body(*refs):
        if glu:
            v_ref, g_ref, w_ref, b_ref, o_ref, pad_ref, sh_ref = refs
        else:
            v_ref, w_ref, b_ref, o_ref, pad_ref, sh_ref = refs
        pad_ref[0:CONV_PAD, :] = jnp.zeros((CONV_PAD, ct), F32)
        if glu:
            pad_ref[CONV_PAD:CONV_PAD + S, :] = v_ref[...].astype(F32) * _sig(g_ref[...].astype(F32))
        else:
            pad_ref[CONV_PAD:CONV_PAD + S, :] = v_ref[...].astype(F32)
        for r, taps in _tap_plan(K):
            src = pad_ref
            if r > 0:
                _shift_rows(sh_ref, pad_ref, r, CONV_PAD + S, True)
                src = sh_ref
            for l0 in range(0, ct, 128):
                lanes = slice(l0, l0 + 128)
                for c in range(S // ch):
                    acc = None
                    for off, wrow in taps:
                        st = CONV_PAD + c * ch - off
                        term = w_ref[wrow:wrow + 1, lanes] * src[st:st + ch, lanes]
                        acc = term if acc is None else acc + term
                    rows = slice(c * ch, (c + 1) * ch)
                    if r == 0:
                        o_ref[rows, lanes] = acc + b_ref[:, lanes]
                    else:
                        o_ref[rows, lanes] += acc

    in_specs = [pl.BlockSpec((S, ct), lambda j, bb: (bb, off_v // ct + j))]
    args = [z]
    if glu:
        in_specs.append(pl.BlockSpec((S, ct), lambda j, bb: (bb, off_g // ct + j)))
        args.append(z)
    in_specs += [pl.BlockSpec((K, ct), lambda j, bb: (0, j)), pl.BlockSpec((1, ct), lambda j, bb: (0, j))]
    args += [w, b]
    return pl.pallas_call(body, grid=(C // ct, T // S), in_specs=in_specs,
                          out_specs=pl.BlockSpec((S, ct), lambda j, bb: (bb, j)),
                          out_shape=jax.ShapeDtypeStruct((T, C), F32),
                          scratch_shapes=[pltpu.VMEM((CONV_PAD + S, ct), F32), pltpu.VMEM((CONV_PAD + S, ct), F32)],
                          compiler_params=_cp("parallel", "parallel"), name=name)(*args)


def _conv_bwd(z, w, dy, dz, *, S, off_v, off_g, name, ct=256):
    T = z.shape[0]
    K, C = w.shape
    KP = -(-K // SUBLANES) * SUBLANES
    ct = min(ct, C)
    ch = min(CONV_CHUNK, S)
    glu = off_g is not None
    total = S + CONV_PAD

    def body(*refs):
        if glu:
            (v_ref, g_ref, w_ref, dy_ref, dz_in, dz_ref, dw_ref, db_ref,
             pad_ref, sh_ref, padb_ref, shb_ref, du_ref, stage_v, stage_g, sem) = refs
        else:
            (v_ref, w_ref, dy_ref, dz_in, dz_ref, dw_ref, db_ref,
             pad_ref, sh_ref, padb_ref, shb_ref, du_ref, stage_v, sem) = refs
        del dz_in
        j, bb = pl.program_id(0), pl.program_id(1)
        pad_ref[0:CONV_PAD, :] = jnp.zeros((CONV_PAD, ct), F32)
        if glu:
            pad_ref[CONV_PAD:total, :] = v_ref[...].astype(F32) * _sig(g_ref[...].astype(F32))
        else:
            pad_ref[CONV_PAD:total, :] = v_ref[...].astype(F32)
        padb_ref[0:S, :] = dy_ref[...]
        padb_ref[S:total, :] = jnp.zeros((CONV_PAD, ct), F32)

        @pl.when(bb == 0)
        def _():
            dw_ref[...] = jnp.zeros((KP, ct), F32)
            db_ref[...] = jnp.zeros((1, ct), F32)

        for r, taps in _tap_plan(K):
            u_src, d_src = pad_ref, padb_ref
            if r > 0:
                _shift_rows(sh_ref, pad_ref, r, total, True)
                _shift_rows(shb_ref, padb_ref, r, total, False)
                u_src, d_src = sh_ref, shb_ref
            for l0 in range(0, ct, 128):
                lanes = slice(l0, l0 + 128)
                for c in range(S // ch):
                    acc = None
                    for off, wrow in taps:
                        st = c * ch + off
                        term = w_ref[wrow:wrow + 1, lanes] * d_src[st:st + ch, lanes]
                        acc = term if acc is None else acc + term
                    rows = slice(c * ch, (c + 1) * ch)
                    if r == 0:
                        du_ref[rows, lanes] = acc
                    else:
                        du_ref[rows, lanes] += acc
                for off, wrow in taps:
                    acc = None
                    for c in range(S // ch):
                        st = CONV_PAD + c * ch - off
                        prod = padb_ref[c * ch:(c + 1) * ch, lanes] * u_src[st:st + ch, lanes]
                        acc = prod if acc is None else acc + prod
                    dw_ref[wrow:wrow + 1, lanes] += jnp.sum(acc, axis=0, keepdims=True)
        db_ref[...] += jnp.sum(dy_ref[...], axis=0, keepdims=True)
        for l0 in range(0, ct, 128):
            lanes = slice(l0, l0 + 128)
            for c in range(S // ch):
                rows = slice(c * ch, (c + 1) * ch)
                du = du_ref[rows, lanes]
                if glu:
                    sg = _sig(g_ref[rows, lanes].astype(F32))
                    stage_v[rows, lanes] = (du * sg).astype(BF16)
                    stage_g[rows, lanes] = (du * v_ref[rows, lanes].astype(F32) * sg * (1.0 - sg)).astype(BF16)
                else:
                    stage_v[rows, lanes] = du.astype(BF16)
        _put_columns(stage_v, dz_ref, bb * S, S, off_v + j * ct, sem)
        if glu:
            _put_columns(stage_g, dz_ref, bb * S, S, off_g + j * ct, sem)

    blk = lambda off: pl.BlockSpec((S, ct), lambda j, bb: (bb, off // ct + j))
    in_specs = [blk(off_v)]
    args = [z]
    if glu:
        in_specs.append(blk(off_g))
        args.append(z)
    in_specs += [pl.BlockSpec((K, ct), lambda j, bb: (0, j)), blk(0), _ANY]
    args += [w, dy, dz]
    out_shape = (jax.ShapeDtypeStruct(dz.shape, dz.dtype), jax.ShapeDtypeStruct((KP, C), F32),
                 jax.ShapeDtypeStruct((1, C), F32))
    out_specs = (_ANY, pl.BlockSpec((KP, ct), lambda j, bb: (0, j)), pl.BlockSpec((1, ct), lambda j, bb: (0, j)))
    padded = pltpu.VMEM((total, ct), F32)
    stage = pltpu.VMEM((S, ct), BF16)
    return pl.pallas_call(body, grid=(C // ct, T // S), in_specs=in_specs, out_specs=out_specs, out_shape=out_shape,
                          scratch_shapes=[padded, padded, padded, padded, pltpu.VMEM((S, ct), F32), stage]
                          + ([stage] if glu else []) + [pltpu.SemaphoreType.DMA],
                          input_output_aliases={len(args) - 1: 0},
                          compiler_params=_cp("parallel", "arbitrary"), name=name)(*args)


def _softplus_neg(lam):
    return jnp.maximum(-lam, 0.0) + jnp.log1p(jnp.exp(-jnp.abs(lam)))


def _neg_expm1(x):
    u = jnp.exp(x)
    um1 = u - 1.0
    lg = jnp.log(u)
    safe = jnp.where(lg == 0.0, 1.0, lg)
    em1 = jnp.where(um1 == 0.0, x, jnp.where(um1 == -1.0, -1.0, um1 * x / safe))
    return -em1


def _rglru_fwd(v, z, wa, wx, ba, bx, lam, *, S, off_gate, name, tt=512):
    T, C = v.shape
    G, gw, _ = wa.shape
    tt = min(tt, S)
    nt = S // tt
    nlb = gw // 128
    assert gw % 128 == 0 and off_gate % 128 == 0

    def body(*refs):
        v_ref = refs[0]
        gate_refs = refs[1:1 + nlb]
        wa_ref, wx_ref, ba_ref, bx_ref, lam_ref, r_ref, i_ref, h_ref, hb_ref, a_s, b_s, carry_ref = refs[1 + nlb:]
        t = pl.program_id(2)

        @pl.when(t == 0)
        def _():
            carry_ref[...] = jnp.zeros((SUBLANES, gw), F32)

        vv = v_ref[...]
        vb = vv.astype(BF16)
        r = _sig(jnp.dot(vb, wa_ref[...], preferred_element_type=F32) + ba_ref[...])
        ig = _sig(jnp.dot(vb, wx_ref[...], preferred_element_type=F32) + bx_ref[...])
        log_a = -LRU_C * r * _softplus_neg(lam_ref[...])
        mult = jnp.sqrt(_neg_expm1(2.0 * log_a))
        start = jnp.logical_and(t == 0, lax.broadcasted_iota(jnp.int32, (tt, gw), 0) == 0)
        mult = jnp.where(start, 1.0, mult)
        r_ref[...] = r
        i_ref[...] = ig
        a_s[...] = jnp.exp(log_a)
        b_s[...] = mult * ig * vv
        row = lax.broadcasted_iota(jnp.int32, (SUBLANES, gw), 0)

        def step(i, carry):
            st = pl.multiple_of(i * SUBLANES, SUBLANES)
            A = a_s[pl.ds(st, SUBLANES), :]
            B = b_s[pl.ds(st, SUBLANES), :]
            for d in (1, 2, 4):
                m = row >= d
                Bn = jnp.where(m, A * pltpu.roll(B, d, 0) + B, B)
                A = jnp.where(m, A * pltpu.roll(A, d, 0), A)
                B = Bn
            h = B + A * carry
            h_ref[pl.ds(st, SUBLANES), :] = h
            return jnp.broadcast_to(h[SUBLANES - 1:SUBLANES, :], (SUBLANES, gw))

        carry_ref[...] = lax.fori_loop(0, tt // SUBLANES, step, carry_ref[...], unroll=2)
        for k in range(nlb):
            lanes = slice(k * 128, (k + 1) * 128)
            hb_ref[:, lanes] = (h_ref[:, lanes] * _gelu(gate_refs[k][...].astype(F32))).astype(BF16)

    blk = pl.BlockSpec((tt, gw), lambda g, bb, t: (bb * nt + t, g))
    gates = [pl.BlockSpec((tt, 128), lambda g, bb, t, k=k: (bb * nt + t, off_gate // 128 + g * nlb + k)) for k in range(nlb)]
    wsp = pl.BlockSpec((None, gw, gw), lambda g, bb, t: (g, 0, 0))
    vec = pl.BlockSpec((1, gw), lambda g, bb, t: (0, g))
    o = jax.ShapeDtypeStruct((T, C), F32)
    return pl.pallas_call(body, grid=(G, T // S, nt), in_specs=[blk] + gates + [wsp, wsp, vec, vec, vec],
                          out_specs=(blk, blk, blk, blk), out_shape=(o, o, o, jax.ShapeDtypeStruct((T, C), BF16)),
                          scratch_shapes=[pltpu.VMEM((tt, gw), F32), pltpu.VMEM((tt, gw), F32),
                                          pltpu.VMEM((SUBLANES, gw), F32)],
                          compiler_params=_cp("parallel", "parallel", "arbitrary"),
                          name=name)(v, *([z] * nlb), wa, wx, ba, bx, lam)


def _rglru_bwd(dhb, h, z, r, ig, v, wa, wx, lam, dz, *, S, off_gate, name, tt=512):
    T, C = v.shape
    G, gw, _ = wa.shape
    hd = gw // HEADS_PER_GROUP
    tt = min(tt, S)
    nt = S // tt
    n_seq = T // S
    n_tiles = tt // SUBLANES
    nlb = gw // 128

    def body(*refs):
        dhb_ref, h_ref, hp_ref = refs[0:3]
        gate_refs = refs[3:3 + nlb]
        (r_ref, i_ref, v_ref, wa_ref, wx_ref, lam_ref, dz_in, dv_ref, dz_ref, dlam_ref, dba_ref, dbx_ref, dwa_ref, dwx_ref,
         dh_s, a_s, G_s, da_s, carry_ref, stage, acc_a, acc_x, sem) = refs[3 + nlb:]
        del dz_in
        g, bb, t = pl.program_id(0), pl.program_id(1), pl.program_id(2)
        tb = nt - 1 - t
        first = jnp.logical_and(bb == 0, t == 0)

        @pl.when(t == 0)
        def _():
            carry_ref[...] = jnp.zeros((SUBLANES, gw), F32)

        @pl.when(first)
        def _():
            dlam_ref[...] = jnp.zeros((SUBLANES, gw), F32)
            dba_ref[...] = jnp.zeros((SUBLANES, gw), F32)
            dbx_ref[...] = jnp.zeros((SUBLANES, gw), F32)
            acc_a[...] = jnp.zeros((gw, gw), F32)
            acc_x[...] = jnp.zeros((gw, gw), F32)

        rr, ig, vv = r_ref[...], i_ref[...], v_ref[...]
        lam_v = lam_ref[...]
        sp = _softplus_neg(lam_v)
        log_a = -LRU_C * rr * sp
        a = jnp.exp(log_a)
        a_s[...] = a
        for k in range(nlb):
            lanes = slice(k * 128, (k + 1) * 128)
            gate = gate_refs[k][...].astype(F32)
            dhb = dhb_ref[:, lanes]
            dh_s[:, lanes] = dhb * _gelu(gate)
            stage[:, lanes] = (dhb * h_ref[:, lanes] * _gelu_grad(gate)).astype(BF16)
        put = _columns_copy(stage, dz_ref, (bb * nt + tb) * tt, tt, off_gate + g * gw, sem)
        put.start()
        h_before = jnp.where(tb > 0, jnp.broadcast_to(hp_ref[SUBLANES - 1:SUBLANES, :], (SUBLANES, gw)), 0.0)
        row = lax.broadcasted_iota(jnp.int32, (SUBLANES, gw), 0)

        def step(k, qcarry):
            i = n_tiles - 1 - k
            st = pl.multiple_of(i * SUBLANES, SUBLANES)
            stp = pl.multiple_of(jnp.maximum(i - 1, 0) * SUBLANES, SUBLANES)
            A = a_s[pl.ds(st, SUBLANES), :]
            hv = h_ref[pl.ds(st, SUBLANES), :]
            hprev_tile = h_ref[pl.ds(stp, SUBLANES), :]
            dh = dh_s[pl.ds(st, SUBLANES), :]
            Aq = A
            Bq = A * dh
            for d in (1, 2, 4):
                m = row < SUBLANES - d
                Bn = jnp.where(m, Aq * pltpu.roll(Bq, SUBLANES - d, 0) + Bq, Bq)
                Aq = jnp.where(m, Aq * pltpu.roll(Aq, SUBLANES - d, 0), Aq)
                Bq = Bn
            q = Bq + Aq * qcarry
            qnext = jnp.where(row == SUBLANES - 1, qcarry, pltpu.roll(q, SUBLANES - 1, 0))
            gq = dh + qnext
            hlast = jnp.where(i > 0, jnp.broadcast_to(hprev_tile[SUBLANES - 1:SUBLANES, :], (SUBLANES, gw)), h_before)
            hprev = jnp.where(row == 0, hlast, pltpu.roll(hv, 1, 0))
            G_s[pl.ds(st, SUBLANES), :] = gq
            da_s[pl.ds(st, SUBLANES), :] = gq * hprev
            return jnp.broadcast_to(q[0:1, :], (SUBLANES, gw))

        carry_ref[...] = lax.fori_loop(0, n_tiles, step, carry_ref[...], unroll=2)

        Gv = G_s[...]
        mult_raw = jnp.sqrt(_neg_expm1(2.0 * log_a))
        start = jnp.logical_and(tb == 0, lax.broadcasted_iota(jnp.int32, (tt, gw), 0) == 0)
        mult = jnp.where(start, 1.0, mult_raw)
        dmult = jnp.where(start, 0.0, Gv * ig * vv)
        di = Gv * mult * vv
        dla = da_s[...] * a - dmult * (a * a) / jnp.where(start, 1.0, mult_raw)
        dpr = dla * (-LRU_C) * sp * rr * (1.0 - rr)
        dpi = di * ig * (1.0 - ig)
        dlam_ref[...] += jnp.broadcast_to(jnp.sum(dla * (-LRU_C) * rr, axis=0, keepdims=True) * (-_sig(-lam_v)), (SUBLANES, gw))
        dba_ref[...] += jnp.broadcast_to(jnp.sum(dpr, axis=0, keepdims=True), (SUBLANES, gw))
        dbx_ref[...] += jnp.broadcast_to(jnp.sum(dpi, axis=0, keepdims=True), (SUBLANES, gw))
        dprb, dpib = dpr.astype(BF16), dpi.astype(BF16)
        nt_dims = (((1,), (1,)), ((), ()))
        dv_ref[...] = (Gv * mult * ig
                       + lax.dot_general(dprb, wa_ref[...], nt_dims, preferred_element_type=F32)
                       + lax.dot_general(dpib, wx_ref[...], nt_dims, preferred_element_type=F32))
        tn_dims = (((0,), (0,)), ((), ()))
        vb = vv.astype(BF16)
        acc_a[...] += lax.dot_general(vb, dprb, tn_dims, preferred_element_type=F32)
        acc_x[...] += lax.dot_general(vb, dpib, tn_dims, preferred_element_type=F32)

        @pl.when(jnp.logical_and(bb == n_seq - 1, t == nt - 1))
        def _():
            for hh in range(HEADS_PER_GROUP):
                dwa_ref[hh] = acc_a[hh * hd:(hh + 1) * hd, hh * hd:(hh + 1) * hd]
                dwx_ref[hh] = acc_x[hh * hd:(hh + 1) * hd, hh * hd:(hh + 1) * hd]

        put.wait()

    rowblk = lambda g, bb, t: (bb * nt + nt - 1 - t, g)
    blk = pl.BlockSpec((tt, gw), rowblk)
    before = pl.BlockSpec((SUBLANES, gw), lambda g, bb, t: (jnp.maximum((bb * nt + nt - 1 - t) * n_tiles - 1, 0), g))
    gates = [pl.BlockSpec((tt, 128), lambda g, bb, t, k=k: (bb * nt + nt - 1 - t, off_gate // 128 + g * nlb + k))
             for k in range(nlb)]
    wsp = pl.BlockSpec((None, gw, gw), lambda g, bb, t: (g, 0, 0))
    vec = pl.BlockSpec((1, gw), lambda g, bb, t: (0, g))
    acc8 = pl.BlockSpec((SUBLANES, gw), lambda g, bb, t: (0, g))
    heads = pl.BlockSpec((HEADS_PER_GROUP, hd, hd), lambda g, bb, t: (g, 0, 0))
    o8 = jax.ShapeDtypeStruct((SUBLANES, C), F32)
    ow = jax.ShapeDtypeStruct((G * HEADS_PER_GROUP, hd, hd), F32)
    scr = pltpu.VMEM((tt, gw), F32)
    return pl.pallas_call(
        body, grid=(G, n_seq, nt),
        in_specs=[blk, blk, before] + gates + [blk, blk, blk, wsp, wsp, vec, _ANY],
        out_specs=(blk, _ANY, acc8, acc8, acc8, heads, heads),
        out_shape=(jax.ShapeDtypeStruct((T, C), F32), jax.ShapeDtypeStruct(dz.shape, dz.dtype), o8, o8, o8, ow, ow),
        scratch_shapes=[scr, scr, scr, scr, pltpu.VMEM((SUBLANES, gw), F32), pltpu.VMEM((tt, gw), BF16),
                        pltpu.VMEM((gw, gw), F32), pltpu.VMEM((gw, gw), F32), pltpu.SemaphoreType.DMA],
        input_output_aliases={9 + nlb: 1},
        compiler_params=_cp("parallel", "arbitrary", "arbitrary"),
        name=name)(dhb, h, h, *([z] * nlb), r, ig, v, wa, wx, lam, dz)


def _group_weights(w):
    H, hd, _ = w.shape
    G = H // HEADS_PER_GROUP
    eye = jnp.eye(HEADS_PER_GROUP, dtype=w.dtype)
    wg = jnp.einsum("ghij,hk->ghikj", w.reshape(G, HEADS_PER_GROUP, hd, hd), eye)
    return wg.reshape(G, HEADS_PER_GROUP * hd, HEADS_PER_GROUP * hd).astype(BF16)


def _layer_fwd(x, p, *, S, fetch=None):
    D = x.shape[1]
    Dc = p["conv_a_b"].shape[1]
    Dr = p["conv_b_b"].shape[1]
    offs = dict(va=0, ga=Dc, xb=2 * Dc, gb=2 * Dc + Dr, sa=2 * Dc + 2 * Dr, sb=2 * Dc + 2 * Dr + D)
    h = _rms_fwd(x, p["g_mix"], name="rms_mix_fwd")
    if fetch is not None:
        fetch("in", h)
    z = _mm_nn(h, p["w_in"], tm=4096, tn=p["w_in"].shape[2], bias=p["b_in"], out_dtype=BF16, a_resident=True,
               name="mm_in_fwd")
    if fetch is not None:
        fetch("mix", z)
    u1 = _conv_fwd(z, p["conv_a_w"], p["conv_a_b"], S=S, off_v=offs["va"], off_g=offs["ga"], name="conv_a_fwd")
    u2 = _ln_silu_fwd(u1, p["ln_g"], p["ln_b"], name="ln_silu_fwd")
    ya = _mm_nn(u2, p["w_a_out"], tm=1024, tn=1024, name="mm_a_out_fwd")
    v0 = _conv_fwd(z, p["conv_b_w"], p["conv_b_b"], S=S, off_v=offs["xb"], off_g=None, name="conv_b_fwd")
    r, ig, hs, hb = _rglru_fwd(v0, z, p["wg_a"], p["wg_x"], p["b_rg_a"], p["b_rg_x"], p["lam"], S=S, off_gate=offs["gb"],
                               name="rglru_fwd")
    yb = _mm_nn(hb, p["w_b_out"], tm=1024, tn=1024, name="mm_b_out_fwd")
    m = _merge_fwd(z, ya, yb, off_sa=offs["sa"], off_sb=offs["sb"], name="merge_fwd")
    x_mid = _mm_nn(m, p["w_o"], tm=1024, tn=1024, resid=x, name="mm_o_fwd")
    h2 = _rms_fwd(x_mid, p["g_mlp"], name="rms_mlp_fwd")
    if fetch is not None:
        fetch("mlp", h2)
    f = _mm_nn(h2, p["w_1"], tm=4096, tn=p["w_1"].shape[2], relu2=True, out_dtype=BF16, a_resident=True,
               name="mm_1_fwd")
    x_next = _mm_nn(f, p["w_2"], tm=512, tn=1024, resid=x_mid, name="mm_2_fwd")
    saved = dict(x=x, h=h, z=z, u1=u1, u2=u2, ya=ya, v0=v0, r=r, ig=ig, hs=hs, hb=hb, yb=yb, m=m,
                 x_mid=x_mid, h2=h2, f=f, offs=offs)
    return x_next, saved


def _layer_bwd_mlp(dx, p, sv, *, wdt, dep=None):
    dx, dxb = dx
    g = {}
    g["w_2"] = _mm_tn(sv["f"], dxb, tm=1024, tn=1024, tk=4096, out_dtype=wdt, name="mm_2_wgrad")
    dfp = _mm_nt(dxb, p["w_2"], tm=1024, tn=1024, tk=1024, mul_sqrt=sv["f"], out_dtype=BF16, dep=dep,
                 name="mm_2_dgrad")
    g["w_1"] = _mm_tn(sv["h2"], dfp, tm=1024, tn=512, tk=4096, out_blocks=p["w_1"].shape[0], out_dtype=wdt,
                      name="mm_1_wgrad")
    dh2 = _mm_nt(dfp, p["w_1"], tm=512, tn=1024, tk=512, whole_b=True, name="mm_1_dgrad")
    dx_mid, dx_mid_b, g["g_mlp"] = _rms_bwd(sv["x_mid"], p["g_mlp"], dh2, dx, name="rms_mlp_bwd")
    return (dx_mid, dx_mid_b), g


def _layer_bwd_mix(dx_mid, p, sv, *, S, wdt, dep=None, on_gate_grads=None, on_weight_grads=None):
    offs = sv["offs"]
    dx_mid, dx_mid_b = dx_mid
    g = {}
    g["w_o"] = _mm_tn(sv["m"], dx_mid_b, tm=1024, tn=1024, tk=4096, out_dtype=wdt, name="mm_o_wgrad")
    dm = _mm_nt(dx_mid_b, p["w_o"], tm=1024, tn=1024, tk=1024, dep=dep, name="mm_o_dgrad")
    dz = lax.empty(sv["z"].shape, BF16)
    dya, dyb, dz = _merge_bwd(sv["z"], sv["ya"], sv["yb"], dm, dz, off_sa=offs["sa"], off_sb=offs["sb"], name="merge_bwd")
    g["w_b_out"] = _mm_tn(sv["hb"], dyb, tm=768, tn=1024, tk=4096, out_dtype=wdt, name="mm_b_out_wgrad")
    dhb = _mm_nt(dyb, p["w_b_out"], tm=1024, tn=1536, tk=1024, name="mm_b_out_dgrad")
    dv0, dz, dlam, dba, dbx, g["w_rg_a"], g["w_rg_x"] = _rglru_bwd(
        dhb, sv["hs"], sv["z"], sv["r"], sv["ig"], sv["v0"], p["wg_a"], p["wg_x"], p["lam"], dz, S=S, off_gate=offs["gb"],
        name="rglru_bwd")
    g["lam"], g["b_rg_a"], g["b_rg_x"] = dlam[:1], dba[:1], dbx[:1]
    dep_gates = on_gate_grads(g) if on_gate_grads is not None else None
    dz, g["conv_b_w"], g["conv_b_b"] = _conv_bwd(sv["z"], p["conv_b_w"], dv0, dz, S=S, off_v=offs["xb"], off_g=None,
                                                 name="conv_b_bwd")
    g["w_a_out"] = _mm_tn(sv["u2"], dya, tm=1024, tn=1024, tk=4096, out_dtype=wdt, name="mm_a_out_wgrad")
    du2 = _mm_nt(dya, p["w_a_out"], tm=1024, tn=1024, tk=1024, dep=dep_gates, name="mm_a_out_dgrad")
    du1, g["ln_g"], g["ln_b"] = _ln_silu_bwd(sv["u1"], p["ln_g"], p["ln_b"], du2, name="ln_silu_bwd")
    dz, g["conv_a_w"], g["conv_a_b"] = _conv_bwd(sv["z"], p["conv_a_w"], du1, dz, S=S, off_v=offs["va"],
                                                 off_g=offs["ga"], name="conv_a_bwd")
    g["w_in"], db_in = _mm_tn(sv["h"], dz, tm=1024, tn=512, tk=4096, out_blocks=p["w_in"].shape[0], colsum=True,
                              out_dtype=wdt, name="mm_in_wgrad")
    g["b_in"] = db_in[:1]
    dep_in = on_weight_grads(g) if on_weight_grads is not None else None
    dh = _mm_nt(dz, p["w_in"], tm=256, tn=1024, tk=512, whole_b=True, dep=dep_in, name="mm_in_dgrad")
    dx_in, dx_in_b, g["g_mix"] = _rms_bwd(sv["x"], p["g_mix"], dh, dx_mid, name="rms_mix_bwd")
    return (dx_in, dx_in_b), g


def _layer_bwd(dx, p, sv, *, S, wdt=F32):
    dx_mid, g = _layer_bwd_mlp(dx, p, sv, wdt=wdt)
    dx_in, g2 = _layer_bwd_mix(dx_mid, p, sv, S=S, wdt=wdt)
    g.update(g2)
    return dx_in, g


def _local_step(x, tgt, layers, g_final, *, S, wdt=F32):
    saved = []
    for p in layers:
        x, sv = _layer_fwd(x, p, S=S)
        saved.append(sv)
    loss, dx, dxb, dg_final = _final_loss(x, g_final, tgt, name="final_loss")
    dx = (dx, dxb)
    grads = [None] * len(layers)
    for l in reversed(range(len(layers))):
        dx, grads[l] = _layer_bwd(dx, layers[l], saved[l], S=S, wdt=wdt)
    return loss, dx[0], grads, dg_final


_HBM = pl.BlockSpec(memory_space=pltpu.HBM)
_MESH = pl.DeviceIdType.MESH


_SEM = pl.BlockSpec(memory_space=pltpu.SEMAPHORE)
_ANY = pl.BlockSpec(memory_space=pl.ANY)
_FLIPS = [(dx, dy, dc) for dx in (0, 1) for dy in (0, 1) for dc in (0, 1)][1:]


def _place(shard, me_idx, dtype, *, name, dep=None):
    r, cc = shard.shape
    tr = 512 if r % 512 == 0 else r

    def body(me_ref, s_ref, *rest):
        del me_ref
        rest[-1][...] = s_ref[...].astype(dtype)

    in_specs, args = [pl.BlockSpec((tr, cc), lambda i, me: (i, 0))], [me_idx, shard]
    if dep is not None:
        in_specs.append(pl.BlockSpec(dep.shape, lambda i, me: (0, 0)))
        args.append(dep)
    grid_spec = pltpu.PrefetchScalarGridSpec(
        num_scalar_prefetch=1, grid=(r // tr,), in_specs=in_specs,
        out_specs=pl.BlockSpec((None, tr, cc), lambda i, me: (me[0], i, 0)))
    return pl.pallas_call(body, grid_spec=grid_spec, out_shape=jax.ShapeDtypeStruct((N_DEV, r, cc), dtype),
                          compiler_params=_cp("arbitrary"), name=name)(*args)


def _exchange_copies(srcs, lands, send_sems, recv_sems):
    x, y, c = lax.axis_index("x"), lax.axis_index("y"), lax.axis_index("c")
    me = 4 * x + 2 * y + c
    pairs = []
    for k, (dx, dy, dc) in enumerate(_FLIPS):
        peer = (1 - x if dx else x, 1 - y if dy else y, 1 - c if dc else c)
        pidx = 4 * peer[0] + 2 * peer[1] + peer[2]
        for a, land in enumerate(lands):
            src = land.at[me] if srcs is None else srcs[a].at[pidx]
            sem = k * len(lands) + a
            out = pltpu.make_async_remote_copy(src_ref=src, dst_ref=land.at[me], send_sem=send_sems.at[sem],
                                               recv_sem=recv_sems.at[sem], device_id=peer, device_id_type=_MESH)
            arrival = pltpu.make_async_remote_copy(src_ref=src, dst_ref=land.at[pidx], send_sem=send_sems.at[sem],
                                                   recv_sem=recv_sems.at[sem], device_id=peer, device_id_type=_MESH)
            pairs.append((out, arrival))
    return pairs


def _exchange_start(srcs, lands, *, name):
    n = len(lands)
    bufs = list(lands) if srcs is None else list(srcs) + list(lands)
    nb = len(bufs)

    def body(*refs):
        ins = refs[:nb]
        send_sems, recv_sems = refs[nb], refs[nb + 1]
        token = refs[-1]
        for out, _ in _exchange_copies(None if srcs is None else ins[:n], ins[nb - n:], send_sems, recv_sems):
            out.start()
        token[...] = jnp.zeros_like(token)

    sems = pltpu.SemaphoreType.DMA((len(_FLIPS) * n,))
    res = pl.pallas_call(
        body, name=name, in_specs=[_HBM] * nb,
        out_shape=(sems, sems, *[pltpu.HBM(b.shape, b.dtype) for b in bufs], jax.ShapeDtypeStruct((SUBLANES, 128), F32)),
        out_specs=(_SEM, _SEM, *[_HBM] * nb, pl.BlockSpec(memory_space=pltpu.VMEM)),
        input_output_aliases={i: 2 + i for i in range(nb)},
        compiler_params=pltpu.CompilerParams(has_side_effects=pltpu.SideEffectType.DATAFLOW_SIDE_EFFECTING),
    )(*[pltpu.with_memory_space_constraint(b, pltpu.HBM) for b in bufs])
    return res[0], res[1], list(res[2:2 + nb]), res[-1]


def _exchange_wait(send_sems, recv_sems, bufs, *, scatter, after, name):
    nb = len(bufs)
    n = nb // 2 if scatter else nb

    def body(*refs):
        ins = refs[:nb]
        for out, arrival in _exchange_copies(ins[:n] if scatter else None, ins[nb - n:], refs[nb], refs[nb + 1]):
            out.wait_send()
            arrival.wait_recv()

    extra = [] if after is None else [after]
    res = pl.pallas_call(
        body, name=name, in_specs=[_HBM] * nb + [_SEM, _SEM] + [_ANY] * len(extra),
        out_shape=tuple(pltpu.HBM(b.shape, b.dtype) for b in bufs), out_specs=tuple([_HBM] * nb),
        input_output_aliases={i: i for i in range(nb)},
        compiler_params=pltpu.CompilerParams(has_side_effects=pltpu.SideEffectType.DATAFLOW_SIDE_EFFECTING),
    )(*bufs, send_sems, recv_sems, *extra)
    return list(res)


def _adamw_math(w, g, m, v):
    m = ADAM_B1 * m + (1.0 - ADAM_B1) * g
    v = ADAM_B2 * v + (1.0 - ADAM_B2) * (g * g)
    m_hat = m / (1.0 - ADAM_B1 ** ADAM_STEP)
    v_hat = v / (1.0 - ADAM_B2 ** ADAM_STEP)
    delta = -ADAM_LR * (m_hat / (jnp.sqrt(v_hat) + ADAM_EPS) + ADAM_WD * w)
    return delta, m, v


def _adamw(w, m, v, parts, prev, layer, me_idx, *, name, own=None):
    L, r, cc = w.shape
    P = parts.shape[0]
    tr = 512 if r % 512 == 0 else r
    if prev is None:
        prev = tuple(lax.empty(w.shape, F32) for _ in range(4))

    def body(me_ref, w_ref, m_ref, v_ref, p_ref, *rest):
        g_ref, d_ref, nm_ref, nv_ref = rest[-4:]
        if own is None:
            g = p_ref[0].astype(F32)
            for q in range(1, P):
                g = g + p_ref[q].astype(F32)
        else:
            me = me_ref[0]
            g = rest[0][...].astype(F32)
            for q in range(P):
                g = g + jnp.where(q == me, 0.0, p_ref[q].astype(F32))
        d, nm, nv = _adamw_math(w_ref[...], g, m_ref[...], v_ref[...])
        g_ref[...] = g
        d_ref[...] = d
        nm_ref[...] = nm
        nv_ref[...] = nv

    blk = pl.BlockSpec((None, tr, cc), lambda i, me: (layer, i, 0))
    in_specs = [blk, blk, blk, pl.BlockSpec((P, tr, cc), lambda i, me: (0, i, 0))]
    args = [me_idx, w, m, v, parts]
    if own is not None:
        in_specs.append(pl.BlockSpec((None, tr, cc), lambda i, me: (me[0], i, 0)))
        args.append(own)
    first_prev = len(args)
    in_specs += [_ANY] * 4
    args += list(prev)
    grid_spec = pltpu.PrefetchScalarGridSpec(num_scalar_prefetch=1, grid=(r // tr,), in_specs=in_specs,
                                             out_specs=(blk, blk, blk, blk))
    o = jax.ShapeDtypeStruct(w.shape, F32)
    return pl.pallas_call(body, grid_spec=grid_spec, out_shape=(o, o, o, o),
                          input_output_aliases={first_prev + i: i for i in range(4)},
                          compiler_params=_cp("parallel"), name=name)(*args)


_SHARDED = ("w_in", "conv_a_w", "w_a_out", "conv_b_w", "w_b_out", "w_o", "w_1", "w_2")
_COL_SHARDED = ("w_in", "conv_a_w", "conv_b_w", "w_1")
_REPLICATED = ("g_mix", "b_in", "conv_a_b", "ln_g", "ln_b", "conv_b_b", "w_rg_a", "b_rg_a", "w_rg_x", "b_rg_x", "lam",
               "g_mlp")
_WEIGHTS = ("g_mix", "w_in", "b_in", "conv_a_w", "conv_a_b", "ln_g", "ln_b", "w_a_out", "conv_b_w", "conv_b_b", "w_rg_a",
            "b_rg_a", "w_rg_x", "b_rg_x", "lam", "w_b_out", "w_o", "g_mlp", "w_1", "w_2", "g_final")
_LANES = 128


def _cols_from_blocks(b):
    nb, K, n = b.shape
    return b.transpose(1, 0, 2).reshape(K, nb * n)


def _blocks_from_cols(w, K):
    n = w.shape[1] // N_DEV
    return w[:K].reshape(K, N_DEV, n).transpose(1, 0, 2)


def kernel(x, g_mix, w_in, b_in, conv_a_w, conv_a_b, ln_g, ln_b, w_a_out, conv_b_w, conv_b_b, w_rg_a, b_rg_a, w_rg_x, b_rg_x, lam, w_b_out, w_o, g_mlp, w_1, w_2, g_final, loss_target, m_g_mix, m_w_in, m_b_in, m_conv_a_w, m_conv_a_b, m_ln_g, m_ln_b, m_w_a_out, m_conv_b_w, m_conv_b_b, m_w_rg_a, m_b_rg_a, m_w_rg_x, m_b_rg_x, m_lam, m_w_b_out, m_w_o, m_g_mlp, m_w_1, m_w_2, m_g_final, v_g_mix, v_w_in, v_b_in, v_conv_a_w, v_conv_a_b, v_ln_g, v_ln_b, v_w_a_out, v_conv_b_w, v_conv_b_b, v_w_rg_a, v_b_rg_a, v_w_rg_x, v_b_rg_x, v_lam, v_w_b_out, v_w_o, v_g_mlp, v_w_1, v_w_2, v_g_final):
    W = dict(g_mix=g_mix, w_in=w_in, b_in=b_in, conv_a_w=conv_a_w, conv_a_b=conv_a_b, ln_g=ln_g, ln_b=ln_b,
             w_a_out=w_a_out, conv_b_w=conv_b_w, conv_b_b=conv_b_b, w_rg_a=w_rg_a, b_rg_a=b_rg_a, w_rg_x=w_rg_x,
             b_rg_x=b_rg_x, lam=lam, w_b_out=w_b_out, w_o=w_o, g_mlp=g_mlp, w_1=w_1, w_2=w_2, g_final=g_final)
    M = dict(g_mix=m_g_mix, w_in=m_w_in, b_in=m_b_in, conv_a_w=m_conv_a_w, conv_a_b=m_conv_a_b, ln_g=m_ln_g, ln_b=m_ln_b,
             w_a_out=m_w_a_out, conv_b_w=m_conv_b_w, conv_b_b=m_conv_b_b, w_rg_a=m_w_rg_a, b_rg_a=m_b_rg_a,
             w_rg_x=m_w_rg_x, b_rg_x=m_b_rg_x, lam=m_lam, w_b_out=m_w_b_out, w_o=m_w_o, g_mlp=m_g_mlp, w_1=m_w_1,
             w_2=m_w_2, g_final=m_g_final)
    V = dict(g_mix=v_g_mix, w_in=v_w_in, b_in=v_b_in, conv_a_w=v_conv_a_w, conv_a_b=v_conv_a_b, ln_g=v_ln_g, ln_b=v_ln_b,
             w_a_out=v_w_a_out, conv_b_w=v_conv_b_w, conv_b_b=v_conv_b_b, w_rg_a=v_w_rg_a, b_rg_a=v_b_rg_a,
             w_rg_x=v_w_rg_x, b_rg_x=v_b_rg_x, lam=v_lam, w_b_out=v_w_b_out, w_o=v_w_o, g_mlp=v_g_mlp, w_1=v_w_1,
             w_2=v_w_2, g_final=v_g_final)
    NB, S, D = x.shape
    L = g_mix.shape[0]
    hd = w_rg_a.shape[-1]
    me_idx = (4 * lax.axis_index("x") + 2 * lax.axis_index("y") + lax.axis_index("c")).astype(jnp.int32).reshape(1)

    stages = (("in", ("w_in",)), ("mix", ("conv_a_w", "w_a_out", "conv_b_w", "w_b_out", "w_o")), ("mlp", ("w_1", "w_2")))
    gathers = {}
    started = jnp.zeros((), F32)
    first = None
    for l in range(L):
        for stage, names in stages:
            lands = [_place(W[k][l], me_idx, F32 if k.startswith("conv") else BF16, dep=first, name="place_" + k)
                     for k in names]
            send_sems, recv_sems, bufs, token = _exchange_start(None, lands, name=f"weights_start_{stage}_{l}")
            gathers[l, stage] = (names, send_sems, recv_sems, bufs)
            started = started + token[0, 0]
            if first is None:
                first = token

    xt = x.reshape(NB * S, D)
    layers, saved = [], []
    for l in range(L):
        p = {k: W[k][l][None] for k in ("g_mix", "b_in", "conv_a_b", "ln_g", "ln_b", "conv_b_b", "b_rg_a", "b_rg_x",
                                         "lam", "g_mlp")}
        if l == 0:
            p["g_mix"] = p["g_mix"] + started
        p["wg_a"] = _group_weights(w_rg_a[l])
        p["wg_x"] = _group_weights(w_rg_x[l])

        def fetch(stage, after, l=l, p=p):
            names, send_sems, recv_sems, bufs = gathers[l, stage]
            bufs = _exchange_wait(send_sems, recv_sems, bufs, scatter=False, after=after,
                                  name=f"weights_wait_{stage}_{l}")
            for k, full in zip(names, bufs):
                if k in ("w_in", "w_1"):
                    p[k] = full
                elif k in _COL_SHARDED:
                    p[k] = _cols_from_blocks(full)
                else:
                    p[k] = full.reshape(-1, full.shape[-1])

        layers.append(p)
        xt, sv = _layer_fwd(xt, p, S=S, fetch=fetch)
        saved.append(sv)
    loss, dx, dxb, dg_final = _final_loss(xt, g_final[None], loss_target.reshape(NB * S, D), name="final_loss")
    dx = (dx, dxb)

    results = {k: None for k in _SHARDED}

    def scatter_start(l, names, g, tag):
        srcs = []
        for k in names:
            shard_shape = W[k].shape[1:]
            if k in ("w_in", "w_1"):
                srcs.append(g[k])
            elif k in _COL_SHARDED:
                srcs.append(_blocks_from_cols(g[k], shard_shape[0]).astype(BF16))
            else:
                srcs.append(g[k].reshape((N_DEV,) + shard_shape))
        lands = [lax.empty(s.shape, BF16) for s in srcs]
        send_sems, recv_sems, bufs, token = _exchange_start(srcs, lands, name=f"grads_start_{tag}_{l}")
        return (names, send_sems, recv_sems, bufs, tag), token

    def scatter_finish(l, flight, after):
        names, send_sems, recv_sems, bufs, tag = flight
        bufs = _exchange_wait(send_sems, recv_sems, bufs, scatter=True, after=after, name=f"grads_wait_{tag}_{l}")
        n = len(names)
        for k, own, land in zip(names, bufs[:n], bufs[n:]):
            results[k] = _adamw(W[k], M[k], V[k], land, results[k], l, me_idx, own=own, name="adamw_" + k)

    gate_names = ("w_rg_a", "w_rg_x")
    gate_results = {k: None for k in gate_names}

    def small_start(l, g):
        lands = [_place(g[k].reshape(-1, hd), me_idx, F32, name="place_gate_grad") for k in gate_names]
        send_sems, recv_sems, bufs, token = _exchange_start(None, lands, name=f"small_start_{l}")
        return (send_sems, recv_sems, bufs), token

    def small_finish(l, flight, after):
        send_sems, recv_sems, bufs = flight
        bufs = _exchange_wait(send_sems, recv_sems, bufs, scatter=False, after=after, name=f"small_wait_{l}")
        for k, g_all in zip(gate_names, bufs):
            gate_results[k] = _adamw(W[k].reshape(L, -1, hd), M[k].reshape(L, -1, hd), V[k].reshape(L, -1, hd), g_all,
                                     gate_results[k], l, me_idx, name="adamw_gate")

    grads = [None] * L
    in_flight = []
    dep = None
    for l in reversed(range(L)):
        dx_mid, g = _layer_bwd_mlp(dx, layers[l], saved[l], wdt=BF16, dep=dep)
        f_mlp, dep = scatter_start(l, ("w_2", "w_1"), g, "mlp")
        flights = [f_mlp]

        small = []

        def on_gate_grads(g_part, l=l, small=small):
            f_small, token = small_start(l, g_part)
            small.append(f_small)
            return token

        def on_weight_grads(g_part, l=l, flights=flights):
            f_mix, token = scatter_start(l, ("w_o", "w_b_out", "w_a_out", "conv_a_w", "conv_b_w", "w_in"), g_part, "mix")
            flights.append(f_mix)
            return token

        dx, g_mix_part = _layer_bwd_mix(dx_mid, layers[l], saved[l], S=S, wdt=BF16, dep=dep,
                                        on_gate_grads=on_gate_grads, on_weight_grads=on_weight_grads)
        dep = None
        g.update(g_mix_part)
        grads[l] = g
        for l_prev, fs, f_sm in in_flight:
            for flight in fs:
                scatter_finish(l_prev, flight, dx[0])
            small_finish(l_prev, f_sm, dx[0])
        in_flight = [(l, flights, small[0])]

    vec_names = tuple(k for k in _REPLICATED if k not in gate_names)
    n_vec = sum(W[k].shape[1] for k in vec_names)

    def vec_pack(rows, final, last):
        tail = jnp.concatenate([final.reshape(1, -1), jnp.broadcast_to(last.reshape(1, 1), (1, _LANES))], axis=1)
        tail = jnp.pad(tail, ((0, SUBLANES - L - 1), (0, n_vec - tail.shape[1])))
        return jnp.concatenate([rows, tail], axis=0)

    g_rows = jnp.concatenate([jnp.concatenate([grads[l][k] for k in vec_names], axis=1) for l in range(L)], axis=0)
    land = _place(vec_pack(g_rows, dg_final, loss[0, :1]), me_idx, F32, name="place_vectors")
    vec_send_sems, vec_recv_sems, vec_bufs, _ = _exchange_start(None, [land], name="vectors_start")
    for l_prev, fs, f_sm in in_flight:
        after = dx[0] if L == 1 else results["w_in"][0]
        for flight in fs:
            scatter_finish(l_prev, flight, after)
            after = results[flight[0][-1]][0]
        small_finish(l_prev, f_sm, after)
    out_g, out_d, out_m, out_v = {}, {}, {}, {}
    for k in _SHARDED:
        out_g[k], out_d[k], out_m[k], out_v[k] = results[k]
    for k in gate_names:
        out_g[k], out_d[k], out_m[k], out_v[k] = (a.reshape(W[k].shape) for a in gate_results[k])

    zero = jnp.zeros((1,), F32)
    (g_all,) = _exchange_wait(vec_send_sems, vec_recv_sems, vec_bufs, scatter=False, after=results["w_in"][0],
                              name="vectors_wait")
    vec_out = _adamw(vec_pack(jnp.concatenate([W[k] for k in vec_names], axis=1), g_final, zero)[None],
                     vec_pack(jnp.concatenate([M[k] for k in vec_names], axis=1), m_g_final, zero)[None],
                     vec_pack(jnp.concatenate([V[k] for k in vec_names], axis=1), v_g_final, zero)[None],
                     g_all, None, 0, me_idx, name="adamw_vectors")
    vec_out = [a[0] for a in vec_out]
    for res, arr in zip((out_g, out_d, out_m, out_v), vec_out):
        off = 0
        for k in vec_names:
            res[k] = arr[:L, off:off + W[k].shape[1]]
            off += W[k].shape[1]
        res["g_final"] = arr[L, :g_final.size]
    loss_out = vec_out[0][L, g_final.size]

    return (loss_out, dx[0].reshape(NB, S, D), *[out_g[k] for k in _WEIGHTS], *[out_d[k] for k in _WEIGHTS],
            *[out_m[k] for k in _WEIGHTS], *[out_v[k] for k in _WEIGHTS])
```

```python
import functools

import jax
import jax.numpy as jnp
from jax import lax
from jax.experimental import pallas as pl
from jax.experimental.pallas import tpu as pltpu

F32 = jnp.float32
BF16 = jnp.bfloat16

EPS = 1e-6
LRU_C = 8.0
N_RNN_HEADS = 16
HEADS_PER_GROUP = 4
N_DEV = 8
ADAM_LR, ADAM_B1, ADAM_B2, ADAM_EPS, ADAM_WD, ADAM_STEP = 0.001, 0.9, 0.999, 1e-08, 0.01, 10

VMEM_LIMIT_BYTES = 48 * 1024 * 1024
CONV_PAD = 32
CONV_CHUNK = 128
SUBLANES = 8


def _cp(*sem):
    return pltpu.CompilerParams(dimension_semantics=sem, vmem_limit_bytes=VMEM_LIMIT_BYTES)


def _sig(x):
    return 1.0 / (1.0 + jnp.exp(-x))


def _gelu(x):
    c = 0.7978845608028654
    return 0.5 * x * (1.0 + jnp.tanh(c * (x + 0.044715 * x * x * x)))


def _gelu_grad(x):
    c = 0.7978845608028654
    th = jnp.tanh(c * (x + 0.044715 * x * x * x))
    return 0.5 * (1.0 + th) + 0.5 * x * (1.0 - th * th) * c * (1.0 + 3.0 * 0.044715 * x * x)


def _mm_nn(a, b, *, tm, tn, name, bias=None, resid=None, relu2=False, out_dtype=F32, a_resident=False):
    M, K = a.shape
    blocked = b.ndim == 3
    N = b.shape[0] * b.shape[2] if blocked else b.shape[1]
    tm = min(tm, M)
    tn = min(tn, N)
    if blocked:
        assert tn == b.shape[2]
    n_extra = (bias is not None) + (resid is not None)

    def body(*refs):
        acc = jnp.dot(refs[0][...].astype(BF16), refs[1][...].astype(BF16), preferred_element_type=F32)
        k = 2
        if bias is not None:
            acc = acc + refs[k][...]
            k += 1
        if resid is not None:
            acc = acc + refs[k][...]
            k += 1
        if relu2:
            p = jnp.maximum(acc, 0.0)
            acc = p * p
        refs[k][...] = acc.astype(out_dtype)

    def spec(shape, index):
        return pl.BlockSpec(shape, (lambda i, j: index(j, i)) if a_resident else index)

    in_specs = [spec((tm, K), lambda j, i: (i, 0))]
    if blocked:
        in_specs.append(spec((None, K, tn), lambda j, i: (j, 0, 0)))
    else:
        in_specs.append(spec((K, tn), lambda j, i: (0, j)))
    args = [a, b]
    if bias is not None:
        in_specs.append(spec((1, tn), lambda j, i: (0, j)))
        args.append(bias)
    if resid is not None:
        in_specs.append(spec((tm, tn), lambda j, i: (i, j)))
        args.append(resid)
    out_specs = spec((tm, tn), lambda j, i: (i, j))
    out_shape = jax.ShapeDtypeStruct((M, N), out_dtype)
    del n_extra
    grid = (M // tm, N // tn) if a_resident else (N // tn, M // tm)
    return pl.pallas_call(body, grid=grid, in_specs=in_specs, out_specs=out_specs,
                          out_shape=out_shape, compiler_params=_cp("parallel", "parallel"), name=name)(*args)


def _mm_nt(a, b, *, tm, tn, tk, name, mul_sqrt=None, resid=None, out_dtype=F32, dep=None, whole_b=False):
    M, N = a.shape
    blocked = b.ndim == 3
    Kout = b.shape[1] if blocked else b.shape[0]
    tm = min(tm, M)
    tn = min(tn, Kout)
    tk = N if whole_b else (b.shape[2] if blocked else min(tk, N))
    nk = N // tk
    nt_dims = (((1,), (1,)), ((), ()))

    def body(*refs):
        acc_ref = refs[-1]
        kk = pl.program_id(2)
        if blocked and whole_b:
            n = b.shape[2]
            part = None
            for jb in range(b.shape[0]):
                pj = lax.dot_general(refs[0][:, jb * n:(jb + 1) * n].astype(BF16), refs[1][jb].astype(BF16), nt_dims,
                                     preferred_element_type=F32)
                part = pj if part is None else part + pj
        else:
            part = lax.dot_general(refs[0][...].astype(BF16), refs[1][...].astype(BF16), nt_dims,
                                   preferred_element_type=F32)

        def finish(acc):
            k = 2
            if mul_sqrt is not None:
                acc = acc * (2.0 * jnp.sqrt(refs[k][...].astype(F32)))
                k += 1
            if resid is not None:
                acc = acc + refs[k][...]
                k += 1
            if dep is not None:
                k += 1
            refs[k][...] = acc.astype(out_dtype)

        if nk == 1:
            finish(part)
            return

        @pl.when(kk == 0)
        def _():
            acc_ref[...] = part

        @pl.when(kk > 0)
        def _():
            acc_ref[...] += part

        @pl.when(kk == nk - 1)
        def _():
            finish(acc_ref[...])

    in_specs = [pl.BlockSpec((tm, tk), lambda i, j, k: (i, k))]
    if blocked and whole_b:
        in_specs.append(pl.BlockSpec((b.shape[0], tn, b.shape[2]), lambda i, j, k: (0, j, 0)))
    elif blocked:
        in_specs.append(pl.BlockSpec((None, tn, tk), lambda i, j, k: (k, j, 0)))
    else:
        in_specs.append(pl.BlockSpec((tn, tk), lambda i, j, k: (j, k)))
    args = [a, b]
    for extra in (mul_sqrt, resid):
        if extra is not None:
            in_specs.append(pl.BlockSpec((tm, tn), lambda i, j, k: (i, j)))
            args.append(extra)
    if dep is not None:
        in_specs.append(pl.BlockSpec(dep.shape, lambda i, j, k: (0, 0)))
        args.append(dep)
    return pl.pallas_call(body, grid=(M // tm, Kout // tn, nk), in_specs=in_specs,
                          out_specs=pl.BlockSpec((tm, tn), lambda i, j, k: (i, j)),
                          out_shape=jax.ShapeDtypeStruct((M, Kout), out_dtype),
                          scratch_shapes=[pltpu.VMEM((tm, tn), F32)] if nk > 1 else [],
                          compiler_params=_cp("parallel", "parallel", "arbitrary"), name=name)(*args)


def _mm_tn(a, b, *, tm, tn, tk, name, out_blocks=None, colsum=False, out_dtype=F32):
    T, M = a.shape
    N = b.shape[1]
    tm = min(tm, M)
    tk = min(tk, T)
    if out_blocks is not None:
        tn = N // out_blocks
        tm = M
    tn = min(tn, N)
    nk = T // tk
    if colsum:
        assert tm == M

    def body(*refs):
        a_ref, b_ref, o_ref, acc_ref = refs[0], refs[1], refs[2], refs[-1]
        kk = pl.program_id(2)
        bv = b_ref[...]
        part = lax.dot_general(a_ref[...].astype(BF16), bv.astype(BF16),
                               (((0,), (0,)), ((), ())), preferred_element_type=F32)

        if colsum:
            csum = jnp.broadcast_to(jnp.sum(bv.astype(F32), axis=0, keepdims=True), (SUBLANES, tn))

        if nk == 1:
            o_ref[...] = part.astype(out_dtype)
            if colsum:
                refs[3][...] = csum
            return

        @pl.when(kk == 0)
        def _():
            acc_ref[...] = part
            if colsum:
                refs[3][...] = csum

        @pl.when(kk > 0)
        def _():
            acc_ref[...] += part
            if colsum:
                refs[3][...] += csum

        @pl.when(kk == nk - 1)
        def _():
            o_ref[...] = acc_ref[...].astype(out_dtype)

    in_specs = [pl.BlockSpec((tk, tm), lambda i, j, k: (k, i)), pl.BlockSpec((tk, tn), lambda i, j, k: (k, j))]
    if out_blocks is not None:
        o_shape = jax.ShapeDtypeStruct((out_blocks, M, tn), out_dtype)
        o_spec = pl.BlockSpec((None, M, tn), lambda i, j, k: (j, 0, 0))
    else:
        o_shape = jax.ShapeDtypeStruct((M, N), out_dtype)
        o_spec = pl.BlockSpec((tm, tn), lambda i, j, k: (i, j))
    if colsum:
        out_shape = (o_shape, jax.ShapeDtypeStruct((SUBLANES, N), F32))
        out_specs = (o_spec, pl.BlockSpec((SUBLANES, tn), lambda i, j, k: (0, j)))
    else:
        out_shape, out_specs = o_shape, o_spec
    return pl.pallas_call(body, grid=(M // tm, N // tn, nk), in_specs=in_specs, out_specs=out_specs,
                          out_shape=out_shape, scratch_shapes=[pltpu.VMEM((tm, tn), F32)] if nk > 1 else [],
                          compiler_params=_cp("parallel", "parallel", "arbitrary"), name=name)(a, b)


def _rms_fwd(x, g, *, name, tr=512):
    T, D = x.shape
    tr = min(tr, T)

    def body(x_ref, g_ref, h_ref):
        xv = x_ref[...]
        r = lax.rsqrt(jnp.mean(xv * xv, axis=-1, keepdims=True) + EPS)
        h_ref[...] = (xv * r * g_ref[...]).astype(BF16)

    return pl.pallas_call(body, grid=(T // tr,),
                          in_specs=[pl.BlockSpec((tr, D), lambda i: (i, 0)), pl.BlockSpec((1, D), lambda i: (0, 0))],
                          out_specs=pl.BlockSpec((tr, D), lambda i: (i, 0)),
                          out_shape=jax.ShapeDtypeStruct((T, D), BF16), compiler_params=_cp("parallel"), name=name)(x, g)


def _rms_bwd(x, g, dh, dres, *, name, tr=512):
    T, D = x.shape
    tr = min(tr, T)

    def body(x_ref, g_ref, dh_ref, dres_ref, dx_ref, dxb_ref, dg_ref):
        xv = x_ref[...]
        r = lax.rsqrt(jnp.mean(xv * xv, axis=-1, keepdims=True) + EPS)
        n = xv * r
        dh = dh_ref[...]
        dn = dh * g_ref[...]
        dx = dres_ref[...] + r * (dn - n * jnp.mean(dn * n, axis=-1, keepdims=True))
        dx_ref[...] = dx
        dxb_ref[...] = dx.astype(BF16)
        part = jnp.sum(dh * n, axis=0, keepdims=True)

        @pl.when(pl.program_id(0) == 0)
        def _():
            dg_ref[...] = part

        @pl.when(pl.program_id(0) > 0)
        def _():
            dg_ref[...] += part

    row = pl.BlockSpec((tr, D), lambda i: (i, 0))
    vec = pl.BlockSpec((1, D), lambda i: (0, 0))
    return pl.pallas_call(body, grid=(T // tr,), in_specs=[row, vec, row, row], out_specs=(row, row, vec),
                          out_shape=(jax.ShapeDtypeStruct((T, D), F32), jax.ShapeDtypeStruct((T, D), BF16),
                                     jax.ShapeDtypeStruct((1, D), F32)),
                          compiler_params=_cp("arbitrary"), name=name)(x, g, dh, dres)


def _final_loss(x, g, tgt, *, name, tr=512):
    T, D = x.shape
    tr = min(tr, T)

    def body(x_ref, g_ref, t_ref, loss_ref, dx_ref, dxb_ref, dg_ref):
        xv = x_ref[...]
        gv = g_ref[...]
        r = lax.rsqrt(jnp.mean(xv * xv, axis=-1, keepdims=True) + EPS)
        n = xv * r
        e = n * gv - t_ref[...]
        lpart = 0.5 * jnp.sum(jnp.mean(e * e, axis=-1, keepdims=True), axis=0, keepdims=True)
        dy = e * (1.0 / D)
        dn = dy * gv
        dx = r * (dn - n * jnp.mean(dn * n, axis=-1, keepdims=True))
        dx_ref[...] = dx
        dxb_ref[...] = dx.astype(BF16)
        gpart = jnp.sum(dy * n, axis=0, keepdims=True)

        @pl.when(pl.program_id(0) == 0)
        def _():
            dg_ref[...] = gpart
            loss_ref[...] = jnp.broadcast_to(lpart, (1, 128))

        @pl.when(pl.program_id(0) > 0)
        def _():
            dg_ref[...] += gpart
            loss_ref[...] += jnp.broadcast_to(lpart, (1, 128))

    row = pl.BlockSpec((tr, D), lambda i: (i, 0))
    vec = pl.BlockSpec((1, D), lambda i: (0, 0))
    return pl.pallas_call(body, grid=(T // tr,), in_specs=[row, vec, row],
                          out_specs=(pl.BlockSpec((1, 128), lambda i: (0, 0)), row, row, vec),
                          out_shape=(jax.ShapeDtypeStruct((1, 128), F32), jax.ShapeDtypeStruct((T, D), F32),
                                     jax.ShapeDtypeStruct((T, D), BF16), jax.ShapeDtypeStruct((1, D), F32)),
                          compiler_params=_cp("arbitrary"), name=name)(x, g, tgt)


def _ln_silu_fwd(u, g, b, *, name, tr=512):
    T, C = u.shape
    tr = min(tr, T)

    def body(u_ref, g_ref, b_ref, o_ref):
        uv = u_ref[...]
        mu = jnp.mean(uv, axis=-1, keepdims=True)
        xc = uv - mu
        r = lax.rsqrt(jnp.mean(xc * xc, axis=-1, keepdims=True) + EPS)
        y = xc * r * g_ref[...] + b_ref[...]
        o_ref[...] = (y * _sig(y)).astype(BF16)

    row = pl.BlockSpec((tr, C), lambda i: (i, 0))
    vec = pl.BlockSpec((1, C), lambda i: (0, 0))
    return pl.pallas_call(body, grid=(T // tr,), in_specs=[row, vec, vec], out_specs=row,
                          out_shape=jax.ShapeDtypeStruct((T, C), BF16), compiler_params=_cp("parallel"),
                          name=name)(u, g, b)


def _ln_silu_bwd(u, g, b, do, *, name, tr=512):
    T, C = u.shape
    tr = min(tr, T)

    def body(u_ref, g_ref, b_ref, do_ref, du_ref, dg_ref, db_ref):
        uv = u_ref[...]
        gv = g_ref[...]
        mu = jnp.mean(uv, axis=-1, keepdims=True)
        xc = uv - mu
        r = lax.rsqrt(jnp.mean(xc * xc, axis=-1, keepdims=True) + EPS)
        n = xc * r
        y = n * gv + b_ref[...]
        s = _sig(y)
        dy = do_ref[...] * (s * (1.0 + y * (1.0 - s)))
        dn = dy * gv
        du_ref[...] = r * (dn - jnp.mean(dn, axis=-1, keepdims=True) - n * jnp.mean(dn * n, axis=-1, keepdims=True))
        gpart = jnp.sum(dy * n, axis=0, keepdims=True)
        bpart = jnp.sum(dy, axis=0, keepdims=True)

        @pl.when(pl.program_id(0) == 0)
        def _():
            dg_ref[...] = gpart
            db_ref[...] = bpart

        @pl.when(pl.program_id(0) > 0)
        def _():
            dg_ref[...] += gpart
            db_ref[...] += bpart

    row = pl.BlockSpec((tr, C), lambda i: (i, 0))
    vec = pl.BlockSpec((1, C), lambda i: (0, 0))
    return pl.pallas_call(body, grid=(T // tr,), in_specs=[row, vec, vec, row], out_specs=(row, vec, vec),
                          out_shape=(jax.ShapeDtypeStruct((T, C), F32), jax.ShapeDtypeStruct((1, C), F32),
                                     jax.ShapeDtypeStruct((1, C), F32)),
                          compiler_params=_cp("arbitrary"), name=name)(u, g, b, do)


def _merge_fwd(z, ya, yb, *, off_sa, off_sb, name, tr=512):
    T, D = ya.shape
    tr = min(tr, T)
    assert off_sa % D == 0 and off_sb % D == 0

    def body(sa_ref, sb_ref, ya_ref, yb_ref, m_ref):
        m_ref[...] = (_sig(sa_ref[...].astype(F32)) * ya_ref[...]
                      + _sig(sb_ref[...].astype(F32)) * yb_ref[...]).astype(BF16)

    row = pl.BlockSpec((tr, D), lambda i: (i, 0))
    return pl.pallas_call(body, grid=(T // tr,),
                          in_specs=[pl.BlockSpec((tr, D), lambda i: (i, off_sa // D)),
                                    pl.BlockSpec((tr, D), lambda i: (i, off_sb // D)), row, row],
                          out_specs=row, out_shape=jax.ShapeDtypeStruct((T, D), BF16),
                          compiler_params=_cp("parallel"), name=name)(z, z, ya, yb)


def _columns_copy(stage_ref, dz_ref, row0, rows, col0, sem):
    dst = dz_ref.at[pl.ds(pl.multiple_of(row0, SUBLANES), rows),
                    pl.ds(pl.multiple_of(col0, 128), stage_ref.shape[1])]
    return pltpu.make_async_copy(stage_ref, dst, sem)


def _put_columns(stage_ref, dz_ref, row0, rows, col0, sem):
    cp = _columns_copy(stage_ref, dz_ref, row0, rows, col0, sem)
    cp.start()
    cp.wait()


def _merge_bwd(z, ya, yb, dm, dz, *, off_sa, off_sb, name, tr=512):
    T, D = ya.shape
    tr = min(tr, T)
    assert off_sb == off_sa + D

    def body(sa_ref, sb_ref, ya_ref, yb_ref, dm_ref, dz_in, dya_ref, dyb_ref, dz_ref, stage, sem):
        del dz_in
        dm = dm_ref[...]
        ga = _sig(sa_ref[...].astype(F32))
        gb = _sig(sb_ref[...].astype(F32))
        dya_ref[...] = (dm * ga).astype(BF16)
        dyb_ref[...] = (dm * gb).astype(BF16)
        stage[:, 0:D] = (dm * ya_ref[...] * ga * (1.0 - ga)).astype(BF16)
        stage[:, D:2 * D] = (dm * yb_ref[...] * gb * (1.0 - gb)).astype(BF16)
        _put_columns(stage, dz_ref, pl.program_id(0) * tr, tr, off_sa, sem)

    row = pl.BlockSpec((tr, D), lambda i: (i, 0))
    o = jax.ShapeDtypeStruct((T, D), BF16)
    return pl.pallas_call(body, grid=(T // tr,),
                          in_specs=[pl.BlockSpec((tr, D), lambda i: (i, off_sa // D)),
                                    pl.BlockSpec((tr, D), lambda i: (i, off_sb // D)), row, row, row, _ANY],
                          out_specs=(row, row, _ANY), out_shape=(o, o, jax.ShapeDtypeStruct(dz.shape, dz.dtype)),
                          scratch_shapes=[pltpu.VMEM((tr, 2 * D), BF16), pltpu.SemaphoreType.DMA],
                          input_output_aliases={5: 2},
                          compiler_params=_cp("parallel"), name=name)(z, z, ya, yb, dm, dz)


def _shift_rows(dst_ref, src_ref, r, total, back):
    for c0 in range(0, total - SUBLANES, CONV_CHUNK):
        n = min(CONV_CHUNK, total - SUBLANES - c0)
        if back:
            dst_ref[SUBLANES + c0:SUBLANES + c0 + n, :] = src_ref[SUBLANES + c0 - r:SUBLANES + c0 - r + n, :]
        else:
            dst_ref[c0:c0 + n, :] = src_ref[c0 + r:c0 + r + n, :]


def _tap_plan(K):
    if K <= SUBLANES:
        return [(0, [(s, K - 1 - s) for s in range(K)])]
    return [(r, [(SUBLANES * q, K - 1 - (SUBLANES * q + r)) for q in range(-(-K // SUBLANES)) if SUBLANES * q + r < K])
            for r in range(SUBLANES)]


def _conv_fwd(z, w, b, *, S, off_v, off_g, name, ct=256):
    T = z.shape[0]
    K, C = w.shape
    ct = min(ct, C)
    ch = min(CONV_CHUNK, S)
    glu = off_g is not None
    assert off_v % ct == 0 and (not glu or off_g % ct == 0)
    assert SUBLANES * ((K - 1) // SUBLANES) <= CONV_PAD - SUBLANES

    def body(*refs):
        if glu:
            v_ref, g_ref, w_ref, b_ref, o_ref, pad_ref, sh_ref = refs
        else:
            v_ref, w_ref, b_ref, o_ref, pad_ref, sh_ref = refs
        pad_ref[0:CONV_PAD, :] = jnp.zeros((CONV_PAD, ct), F32)
        if glu:
            pad_ref[CONV_PAD:CONV_PAD + S, :] = v_ref[...].astype(F32) * _sig(g_ref[...].astype(F32))
        else:
            pad_ref[CONV_PAD:CONV_PAD + S, :] = v_ref[...].astype(F32)
        for r, taps in _tap_plan(K):
            src = pad_ref
            if r > 0:
                _shift_rows(sh_ref, pad_ref, r, CONV_PAD + S, True)
                src = sh_ref
            for l0 in range(0, ct, 128):
                lanes = slice(l0, l0 + 128)
                for c in range(S // ch):
                    acc = None
                    for off, wrow in taps:
                        st = CONV_PAD + c * ch - off
                        term = w_ref[wrow:wrow + 1, lanes] * src[st:st + ch, lanes]
                        acc = term if acc is None else acc + term
                    rows = slice(c * ch, (c + 1) * ch)
                    if r == 0:
                        o_ref[rows, lanes] = acc + b_ref[:, lanes]
                    else:
                        o_ref[rows, lanes] += acc

    in_specs = [pl.BlockSpec((S, ct), lambda j, bb: (bb, off_v // ct + j))]
    args = [z]
    if glu:
        in_specs.append(pl.BlockSpec((S, ct), lambda j, bb: (bb, off_g // ct + j)))
        args.append(z)
    in_specs += [pl.BlockSpec((K, ct), lambda j, bb: (0, j)), pl.BlockSpec((1, ct), lambda j, bb: (0, j))]
    args += [w, b]
    return pl.pallas_call(body, grid=(C // ct, T // S), in_specs=in_specs,
                          out_specs=pl.BlockSpec((S, ct), lambda j, bb: (bb, j)),
                          out_shape=jax.ShapeDtypeStruct((T, C), F32),
                          scratch_shapes=[pltpu.VMEM((CONV_PAD + S, ct), F32), pltpu.VMEM((CONV_PAD + S, ct), F32)],
                          compiler_params=_cp("parallel", "parallel"), name=name)(*args)


def _conv_bwd(z, w, dy, dz, *, S, off_v, off_g, name, ct=256):
    T = z.shape[0]
    K, C = w.shape
    KP = -(-K // SUBLANES) * SUBLANES
    ct = min(ct, C)
    ch = min(CONV_CHUNK, S)
    glu = off_g is not None
    total = S + CONV_PAD

    def body(*refs):
        if glu:
            (v_ref, g_ref, w_ref, dy_ref, dz_in, dz_ref, dw_ref, db_ref,
             pad_ref, sh_ref, padb_ref, shb_ref, du_ref, stage_v, stage_g, sem) = refs
        else:
            (v_ref, w_ref, dy_ref, dz_in, dz_ref, dw_ref, db_ref,
             pad_ref, sh_ref, padb_ref, shb_ref, du_ref, stage_v, sem) = refs
        del dz_in
        j, bb = pl.program_id(0), pl.program_id(1)
        pad_ref[0:CONV_PAD, :] = jnp.zeros((CONV_PAD, ct), F32)
        if glu:
            pad_ref[CONV_PAD:total, :] = v_ref[...].astype(F32) * _sig(g_ref[...].astype(F32))
        else:
            pad_ref[CONV_PAD:total, :] = v_ref[...].astype(F32)
        padb_ref[0:S, :] = dy_ref[...]
        padb_ref[S:total, :] = jnp.zeros((CONV_PAD, ct), F32)

        @pl.when(bb == 0)
        def _():
            dw_ref[...] = jnp.zeros((KP, ct), F32)
            db_ref[...] = jnp.zeros((1, ct), F32)

        for r, taps in _tap_plan(K):
            u_src, d_src = pad_ref, padb_ref
            if r > 0:
                _shift_rows(sh_ref, pad_ref, r, total, True)
                _shift_rows(shb_ref, padb_ref, r, total, False)
                u_src, d_src = sh_ref, shb_ref
            for l0 in range(0, ct, 128):
                lanes = slice(l0, l0 + 128)
                for c in range(S // ch):
                    acc = None
                    for off, wrow in taps:
                        st = c * ch + off
                        term = w_ref[wrow:wrow + 1, lanes] * d_src[st:st + ch, lanes]
                        acc = term if acc is None else acc + term
                    rows = slice(c * ch, (c + 1) * ch)
                    if r == 0:
                        du_ref[rows, lanes] = acc
                    else:
                        du_ref[rows, lanes] += acc
                for off, wrow in taps:
                    acc = None
                    for c in range(S // ch):
                        st = CONV_PAD + c * ch - off
                        prod = padb_ref[c * ch:(c + 1) * ch, lanes] * u_src[st:st + ch, lanes]
                        acc = prod if acc is None else acc + prod
                    dw_ref[wrow:wrow + 1, lanes] += jnp.sum(acc, axis=0, keepdims=True)
        db_ref[...] += jnp.sum(dy_ref[...], axis=0, keepdims=True)
        for l0 in range(0, ct, 128):
            lanes = slice(l0, l0 + 128)
            for c in range(S // ch):
                rows = slice(c * ch, (c + 1) * ch)
                du = du_ref[rows, lanes]
                if glu:
                    sg = _sig(g_ref[rows, lanes].astype(F32))
                    stage_v[rows, lanes] = (du * sg).astype(BF16)
                    stage_g[rows, lanes] = (du * v_ref[rows, lanes].astype(F32) * sg * (1.0 - sg)).astype(BF16)
                else:
                    stage_v[rows, lanes] = du.astype(BF16)
        _put_columns(stage_v, dz_ref, bb * S, S, off_v + j * ct, sem)
        if glu:
            _put_columns(stage_g, dz_ref, bb * S, S, off_g + j * ct, sem)

    blk = lambda off: pl.BlockSpec((S, ct), lambda j, bb: (bb, off // ct + j))
    in_specs = [blk(off_v)]
    args = [z]
    if glu:
        in_specs.append(blk(off_g))
        args.append(z)
    in_specs += [pl.BlockSpec((K, ct), lambda j, bb: (0, j)), blk(0), _ANY]
    args += [w, dy, dz]
    out_shape = (jax.ShapeDtypeStruct(dz.shape, dz.dtype), jax.ShapeDtypeStruct((KP, C), F32),
                 jax.ShapeDtypeStruct((1, C), F32))
    out_specs = (_ANY, pl.BlockSpec((KP, ct), lambda j, bb: (0, j)), pl.BlockSpec((1, ct), lambda j, bb: (0, j)))
    padded = pltpu.VMEM((total, ct), F32)
    stage = pltpu.VMEM((S, ct), BF16)
    return pl.pallas_call(body, grid=(C // ct, T // S), in_specs=in_specs, out_specs=out_specs, out_shape=out_shape,
                          scratch_shapes=[padded, padded, padded, padded, pltpu.VMEM((S, ct), F32), stage]
                          + ([stage] if glu else []) + [pltpu.SemaphoreType.DMA],
                          input_output_aliases={len(args) - 1: 0},
                          compiler_params=_cp("parallel", "arbitrary"), name=name)(*args)


def _softplus_neg(lam):
    return jnp.maximum(-lam, 0.0) + jnp.log1p(jnp.exp(-jnp.abs(lam)))


def _neg_expm1(x):
    u = jnp.exp(x)
    um1 = u - 1.0
    lg = jnp.log(u)
    safe = jnp.where(lg == 0.0, 1.0, lg)
    em1 = jnp.where(um1 == 0.0, x, jnp.where(um1 == -1.0, -1.0, um1 * x / safe))
    return -em1


def _rglru_fwd(v, z, wa, wx, ba, bx, lam, *, S, off_gate, name, tt=512):
    T, C = v.shape
    G, gw, _ = wa.shape
    tt = min(tt, S)
    nt = S // tt
    nlb = gw // 128
    assert gw % 128 == 0 and off_gate % 128 == 0

    def body(*refs):
        v_ref = refs[0]
        gate_refs = refs[1:1 + nlb]
        wa_ref, wx_ref, ba_ref, bx_ref, lam_ref, r_ref, i_ref, h_ref, hb_ref, a_s, b_s, carry_ref = refs[1 + nlb:]
        t = pl.program_id(2)

        @pl.when(t == 0)
        def _():
            carry_ref[...] = jnp.zeros((SUBLANES, gw), F32)

        vv = v_ref[...]
        vb = vv.astype(BF16)
        r = _sig(jnp.dot(vb, wa_ref[...], preferred_element_type=F32) + ba_ref[...])
        ig = _sig(jnp.dot(vb, wx_ref[...], preferred_element_type=F32) + bx_ref[...])
        log_a = -LRU_C * r * _softplus_neg(lam_ref[...])
        mult = jnp.sqrt(_neg_expm1(2.0 * log_a))
        start = jnp.logical_and(t == 0, lax.broadcasted_iota(jnp.int32, (tt, gw), 0) == 0)
        mult = jnp.where(start, 1.0, mult)
        r_ref[...] = r
        i_ref[...] = ig
        a_s[...] = jnp.exp(log_a)
        b_s[...] = mult * ig * vv
        row = lax.broadcasted_iota(jnp.int32, (SUBLANES, gw), 0)

        def step(i, carry):
            st = pl.multiple_of(i * SUBLANES, SUBLANES)
            A = a_s[pl.ds(st, SUBLANES), :]
            B = b_s[pl.ds(st, SUBLANES), :]
            for d in (1, 2, 4):
                m = row >= d
                Bn = jnp.where(m, A * pltpu.roll(B, d, 0) + B, B)
                A = jnp.where(m, A * pltpu.roll(A, d, 0), A)
                B = Bn
            h = B + A * carry
            h_ref[pl.ds(st, SUBLANES), :] = h
            return jnp.broadcast_to(h[SUBLANES - 1:SUBLANES, :], (SUBLANES, gw))

        carry_ref[...] = lax.fori_loop(0, tt // SUBLANES, step, carry_ref[...], unroll=2)
        for k in range(nlb):
            lanes = slice(k * 128, (k + 1) * 128)
            hb_ref[:, lanes] = (h_ref[:, lanes] * _gelu(gate_refs[k][...].astype(F32))).astype(BF16)

    blk = pl.BlockSpec((tt, gw), lambda g, bb, t: (bb * nt + t, g))
    gates = [pl.BlockSpec((tt, 128), lambda g, bb, t, k=k: (bb * nt + t, off_gate // 128 + g * nlb + k)) for k in range(nlb)]
    wsp = pl.BlockSpec((None, gw, gw), lambda g, bb, t: (g, 0, 0))
    vec = pl.BlockSpec((1, gw), lambda g, bb, t: (0, g))
    o = jax.ShapeDtypeStruct((T, C), F32)
    return pl.pallas_call(body, grid=(G, T // S, nt), in_specs=[blk] + gates + [wsp, wsp, vec, vec, vec],
                          out_specs=(blk, blk, blk, blk), out_shape=(o, o, o, jax.ShapeDtypeStruct((T, C), BF16)),
                          scratch_shapes=[pltpu.VMEM((tt, gw), F32), pltpu.VMEM((tt, gw), F32),
                                          pltpu.VMEM((SUBLANES, gw), F32)],
                          compiler_params=_cp("parallel", "parallel", "arbitrary"),
                          name=name)(v, *([z] * nlb), wa, wx, ba, bx, lam)


def _rglru_bwd(dhb, h, z, r, ig, v, wa, wx, lam, dz, *, S, off_gate, name, tt=512):
    T, C = v.shape
    G, gw, _ = wa.shape
    hd = gw // HEADS_PER_GROUP
    tt = min(tt, S)
    nt = S // tt
    n_seq = T // S
    n_tiles = tt // SUBLANES
    nlb = gw // 128

    def body(*refs):
        dhb_ref, h_ref, hp_ref = refs[0:3]
        gate_refs = refs[3:3 + nlb]
        (r_ref, i_ref, v_ref, wa_ref, wx_ref, lam_ref, dz_in, dv_ref, dz_ref, dlam_ref, dba_ref, dbx_ref, dwa_ref, dwx_ref,
         dh_s, a_s, G_s, da_s, carry_ref, stage, acc_a, acc_x, sem) = refs[3 + nlb:]
        del dz_in
        g, bb, t = pl.program_id(0), pl.program_id(1), pl.program_id(2)
        tb = nt - 1 - t
        first = jnp.logical_and(bb == 0, t == 0)

        @pl.when(t == 0)
        def _():
            carry_ref[...] = jnp.zeros((SUBLANES, gw), F32)

        @pl.when(first)
        def _():
            dlam_ref[...] = jnp.zeros((SUBLANES, gw), F32)
            dba_ref[...] = jnp.zeros((SUBLANES, gw), F32)
            dbx_ref[...] = jnp.zeros((SUBLANES, gw), F32)
            acc_a[...] = jnp.zeros((gw, gw), F32)
            acc_x[...] = jnp.zeros((gw, gw), F32)

        rr, ig, vv = r_ref[...], i_ref[...], v_ref[...]
        lam_v = lam_ref[...]
        sp = _softplus_neg(lam_v)
        log_a = -LRU_C * rr * sp
        a = jnp.exp(log_a)
        a_s[...] = a
        for k in range(nlb):
            lanes = slice(k * 128, (k + 1) * 128)
            gate = gate_refs[k][...].astype(F32)
            dhb = dhb_ref[:, lanes]
            dh_s[:, lanes] = dhb * _gelu(gate)
            stage[:, lanes] = (dhb * h_ref[:, lanes] * _gelu_grad(gate)).astype(BF16)
        put = _columns_copy(stage, dz_ref, (bb * nt + tb) * tt, tt, off_gate + g * gw, sem)
        put.start()
        h_before = jnp.where(tb > 0, jnp.broadcast_to(hp_ref[SUBLANES - 1:SUBLANES, :], (SUBLANES, gw)), 0.0)
        row = lax.broadcasted_iota(jnp.int32, (SUBLANES, gw), 0)

        def step(k, qcarry):
            i = n_tiles - 1 - k
            st = pl.multiple_of(i * SUBLANES, SUBLANES)
            stp = pl.multiple_of(jnp.maximum(i - 1, 0) * SUBLANES, SUBLANES)
            A = a_s[pl.ds(st, SUBLANES), :]
            hv = h_ref[pl.ds(st, SUBLANES), :]
            hprev_tile = h_ref[pl.ds(stp, SUBLANES), :]
            dh = dh_s[pl.ds(st, SUBLANES), :]
            Aq = A
            Bq = A * dh
            for d in (1, 2, 4):
                m = row < SUBLANES - d
                Bn = jnp.where(m, Aq * pltpu.roll(Bq, SUBLANES - d, 0) + Bq, Bq)
                Aq = jnp.where(m, Aq * pltpu.roll(Aq, SUBLANES - d, 0), Aq)
                Bq = Bn
            q = Bq + Aq * qcarry
            qnext = jnp.where(row == SUBLANES - 1, qcarry, pltpu.roll(q, SUBLANES - 1, 0))
            gq = dh + qnext
            hlast = jnp.where(i > 0, jnp.broadcast_to(hprev_tile[SUBLANES - 1:SUBLANES, :], (SUBLANES, gw)), h_before)
            hprev = jnp.where(row == 0, hlast, pltpu.roll(hv, 1, 0))
            G_s[pl.ds(st, SUBLANES), :] = gq
            da_s[pl.ds(st, SUBLANES), :] = gq * hprev
            return jnp.broadcast_to(q[0:1, :], (SUBLANES, gw))

        carry_ref[...] = lax.fori_loop(0, n_tiles, step, carry_ref[...], unroll=2)

        Gv = G_s[...]
        mult_raw = jnp.sqrt(_neg_expm1(2.0 * log_a))
        start = jnp.logical_and(tb == 0, lax.broadcasted_iota(jnp.int32, (tt, gw), 0) == 0)
        mult = jnp.where(start, 1.0, mult_raw)
        dmult = jnp.where(start, 0.0, Gv * ig * vv)
        di = Gv * mult * vv
        dla = da_s[...] * a - dmult * (a * a) / jnp.where(start, 1.0, mult_raw)
        dpr = dla * (-LRU_C) * sp * rr * (1.0 - rr)
        dpi = di * ig * (1.0 - ig)
        dlam_ref[...] += jnp.broadcast_to(jnp.sum(dla * (-LRU_C) * rr, axis=0, keepdims=True) * (-_sig(-lam_v)), (SUBLANES, gw))
        dba_ref[...] += jnp.broadcast_to(jnp.sum(dpr, axis=0, keepdims=True), (SUBLANES, gw))
        dbx_ref[...] += jnp.broadcast_to(jnp.sum(dpi, axis=0, keepdims=True), (SUBLANES, gw))
        dprb, dpib = dpr.astype(BF16), dpi.astype(BF16)
        nt_dims = (((1,), (1,)), ((), ()))
        dv_ref[...] = (Gv * mult * ig
                       + lax.dot_general(dprb, wa_ref[...], nt_dims, preferred_element_type=F32)
                       + lax.dot_general(dpib, wx_ref[...], nt_dims, preferred_element_type=F32))
        tn_dims = (((0,), (0,)), ((), ()))
        vb = vv.astype(BF16)
        acc_a[...] += lax.dot_general(vb, dprb, tn_dims, preferred_element_type=F32)
        acc_x[...] += lax.dot_general(vb, dpib, tn_dims, preferred_element_type=F32)

        @pl.when(jnp.logical_and(bb == n_seq - 1, t == nt - 1))
        def _():
            for hh in range(HEADS_PER_GROUP):
                dwa_ref[hh] = acc_a[hh * hd:(hh + 1) * hd, hh * hd:(hh + 1) * hd]
                dwx_ref[hh] = acc_x[hh * hd:(hh + 1) * hd, hh * hd:(hh + 1) * hd]

        put.wait()

    rowblk = lambda g, bb, t: (bb * nt + nt - 1 - t, g)
    blk = pl.BlockSpec((tt, gw), rowblk)
    before = pl.BlockSpec((SUBLANES, gw), lambda g, bb, t: (jnp.maximum((bb * nt + nt - 1 - t) * n_tiles - 1, 0), g))
    gates = [pl.BlockSpec((tt, 128), lambda g, bb, t, k=k: (bb * nt + nt - 1 - t, off_gate // 128 + g * nlb + k))
             for k in range(nlb)]
    wsp = pl.BlockSpec((None, gw, gw), lambda g, bb, t: (g, 0, 0))
    vec = pl.BlockSpec((1, gw), lambda g, bb, t: (0, g))
    acc8 = pl.BlockSpec((SUBLANES, gw), lambda g, bb, t: (0, g))
    heads = pl.BlockSpec((HEADS_PER_GROUP, hd, hd), lambda g, bb, t: (g, 0, 0))
    o8 = jax.ShapeDtypeStruct((SUBLANES, C), F32)
    ow = jax.ShapeDtypeStruct((G * HEADS_PER_GROUP, hd, hd), F32)
    scr = pltpu.VMEM((tt, gw), F32)
    return pl.pallas_call(
        body, grid=(G, n_seq, nt),
        in_specs=[blk, blk, before] + gates + [blk, blk, blk, wsp, wsp, vec, _ANY],
        out_specs=(blk, _ANY, acc8, acc8, acc8, heads, heads),
        out_shape=(jax.ShapeDtypeStruct((T, C), F32), jax.ShapeDtypeStruct(dz.shape, dz.dtype), o8, o8, o8, ow, ow),
        scratch_shapes=[scr, scr, scr, scr, pltpu.VMEM((SUBLANES, gw), F32), pltpu.VMEM((tt, gw), BF16),
                        pltpu.VMEM((gw, gw), F32), pltpu.VMEM((gw, gw), F32), pltpu.SemaphoreType.DMA],
        input_output_aliases={9 + nlb: 1},
        compiler_params=_cp("parallel", "arbitrary", "arbitrary"),
        name=name)(dhb, h, h, *([z] * nlb), r, ig, v, wa, wx, lam, dz)


def _group_weights(w):
    H, hd, _ = w.shape
    G = H // HEADS_PER_GROUP
    eye = jnp.eye(HEADS_PER_GROUP, dtype=w.dtype)
    wg = jnp.einsum("ghij,hk->ghikj", w.reshape(G, HEADS_PER_GROUP, hd, hd), eye)
    return wg.reshape(G, HEADS_PER_GROUP * hd, HEADS_PER_GROUP * hd).astype(BF16)


def _layer_fwd(x, p, *, S, fetch=None):
    D = x.shape[1]
    Dc = p["conv_a_b"].shape[1]
    Dr = p["conv_b_b"].shape[1]
    offs = dict(va=0, ga=Dc, xb=2 * Dc, gb=2 * Dc + Dr, sa=2 * Dc + 2 * Dr, sb=2 * Dc + 2 * Dr + D)
    h = _rms_fwd(x, p["g_mix"], name="rms_mix_fwd")
    if fetch is not None:
        fetch("in", h)
    z = _mm_nn(h, p["w_in"], tm=4096, tn=p["w_in"].shape[2], bias=p["b_in"], out_dtype=BF16, a_resident=True,
               name="mm_in_fwd")
    if fetch is not None:
        fetch("mix", z)
    u1 = _conv_fwd(z, p["conv_a_w"], p["conv_a_b"], S=S, off_v=offs["va"], off_g=offs["ga"], name="conv_a_fwd")
    u2 = _ln_silu_fwd(u1, p["ln_g"], p["ln_b"], name="ln_silu_fwd")
    ya = _mm_nn(u2, p["w_a_out"], tm=1024, tn=1024, name="mm_a_out_fwd")
    v0 = _conv_fwd(z, p["conv_b_w"], p["conv_b_b"], S=S, off_v=offs["xb"], off_g=None, name="conv_b_fwd")
    r, ig, hs, hb = _rglru_fwd(v0, z, p["wg_a"], p["wg_x"], p["b_rg_a"], p["b_rg_x"], p["lam"], S=S, off_gate=offs["gb"],
                               name="rglru_fwd")
    yb = _mm_nn(hb, p["w_b_out"], tm=1024, tn=1024, name="mm_b_out_fwd")
    m = _merge_fwd(z, ya, yb, off_sa=offs["sa"], off_sb=offs["sb"], name="merge_fwd")
    x_mid = _mm_nn(m, p["w_o"], tm=1024, tn=1024, resid=x, name="mm_o_fwd")
    h2 = _rms_fwd(x_mid, p["g_mlp"], name="rms_mlp_fwd")
    if fetch is not None:
        fetch("mlp", h2)
    f = _mm_nn(h2, p["w_1"], tm=4096, tn=p["w_1"].shape[2], relu2=True, out_dtype=BF16, a_resident=True,
               name="mm_1_fwd")
    x_next = _mm_nn(f, p["w_2"], tm=512, tn=1024, resid=x_mid, name="mm_2_fwd")
    saved = dict(x=x, h=h, z=z, u1=u1, u2=u2, ya=ya, v0=v0, r=r, ig=ig, hs=hs, hb=hb, yb=yb, m=m,
                 x_mid=x_mid, h2=h2, f=f, offs=offs)
    return x_next, saved


def _layer_bwd_mlp(dx, p, sv, *, wdt, dep=None):
    dx, dxb = dx
    g = {}
    g["w_2"] = _mm_tn(sv["f"], dxb, tm=1024, tn=1024, tk=4096, out_dtype=wdt, name="mm_2_wgrad")
    dfp = _mm_nt(dxb, p["w_2"], tm=2048, tn=1024, tk=1024, mul_sqrt=sv["f"], out_dtype=BF16, dep=dep,
                 name="mm_2_dgrad")
    g["w_1"] = _mm_tn(sv["h2"], dfp, tm=1024, tn=512, tk=4096, out_blocks=p["w_1"].shape[0], out_dtype=wdt,
                      name="mm_1_wgrad")
    dh2 = _mm_nt(dfp, p["w_1"], tm=1024, tn=1024, tk=512, whole_b=True, name="mm_1_dgrad")
    dx_mid, dx_mid_b, g["g_mlp"] = _rms_bwd(sv["x_mid"], p["g_mlp"], dh2, dx, name="rms_mlp_bwd")
    return (dx_mid, dx_mid_b), g


def _layer_bwd_mix(dx_mid, p, sv, *, S, wdt, dep=None, on_gate_grads=None, on_weight_grads=None):
    offs = sv["offs"]
    dx_mid, dx_mid_b = dx_mid
    g = {}
    g["w_o"] = _mm_tn(sv["m"], dx_mid_b, tm=1024, tn=1024, tk=4096, out_dtype=wdt, name="mm_o_wgrad")
    dm = _mm_nt(dx_mid_b, p["w_o"], tm=1024, tn=1024, tk=1024, dep=dep, name="mm_o_dgrad")
    dz = lax.empty(sv["z"].shape, BF16)
    dya, dyb, dz = _merge_bwd(sv["z"], sv["ya"], sv["yb"], dm, dz, off_sa=offs["sa"], off_sb=offs["sb"], name="merge_bwd")
    g["w_b_out"] = _mm_tn(sv["hb"], dyb, tm=768, tn=1024, tk=4096, out_dtype=wdt, name="mm_b_out_wgrad")
    dhb = _mm_nt(dyb, p["w_b_out"], tm=1024, tn=1536, tk=1024, name="mm_b_out_dgrad")
    dv0, dz, dlam, dba, dbx, g["w_rg_a"], g["w_rg_x"] = _rglru_bwd(
        dhb, sv["hs"], sv["z"], sv["r"], sv["ig"], sv["v0"], p["wg_a"], p["wg_x"], p["lam"], dz, S=S, off_gate=offs["gb"],
        name="rglru_bwd")
    g["lam"], g["b_rg_a"], g["b_rg_x"] = dlam[:1], dba[:1], dbx[:1]
    dep_gates = on_gate_grads(g) if on_gate_grads is not None else None
    dz, g["conv_b_w"], g["conv_b_b"] = _conv_bwd(sv["z"], p["conv_b_w"], dv0, dz, S=S, off_v=offs["xb"], off_g=None,
                                                 name="conv_b_bwd")
    g["w_a_out"] = _mm_tn(sv["u2"], dya, tm=1024, tn=1024, tk=4096, out_dtype=wdt, name="mm_a_out_wgrad")
    du2 = _mm_nt(dya, p["w_a_out"], tm=1024, tn=1024, tk=1024, dep=dep_gates, name="mm_a_out_dgrad")
    du1, g["ln_g"], g["ln_b"] = _ln_silu_bwd(sv["u1"], p["ln_g"], p["ln_b"], du2, name="ln_silu_bwd")
    dz, g["conv_a_w"], g["conv_a_b"] = _conv_bwd(sv["z"], p["conv_a_w"], du1, dz, S=S, off_v=offs["va"],
                                                 off_g=offs["ga"], name="conv_a_bwd")
    g["w_in"], db_in = _mm_tn(sv["h"], dz, tm=1024, tn=512, tk=4096, out_blocks=p["w_in"].shape[0], colsum=True,
                              out_dtype=wdt, name="mm_in_wgrad")
    g["b_in"] = db_in[:1]
    dep_in = on_weight_grads(g) if on_weight_grads is not None else None
    dh = _mm_nt(dz, p["w_in"], tm=256, tn=1024, tk=512, whole_b=True, dep=dep_in, name="mm_in_dgrad")
    dx_in, dx_in_b, g["g_mix"] = _rms_bwd(sv["x"], p["g_mix"], dh, dx_mid, name="rms_mix_bwd")
    return (dx_in, dx_in_b), g


def _layer_bwd(dx, p, sv, *, S, wdt=F32):
    dx_mid, g = _layer_bwd_mlp(dx, p, sv, wdt=wdt)
    dx_in, g2 = _layer_bwd_mix(dx_mid, p, sv, S=S, wdt=wdt)
    g.update(g2)
    return dx_in, g


def _local_step(x, tgt, layers, g_final, *, S, wdt=F32):
    saved = []
    for p in layers:
        x, sv = _layer_fwd(x, p, S=S)
        saved.append(sv)
    loss, dx, dxb, dg_final = _final_loss(x, g_final, tgt, name="final_loss")
    dx = (dx, dxb)
    grads = [None] * len(layers)
    for l in reversed(range(len(layers))):
        dx, grads[l] = _layer_bwd(dx, layers[l], saved[l], S=S, wdt=wdt)
    return loss, dx[0], grads, dg_final


_HBM = pl.BlockSpec(memory_space=pltpu.HBM)
_MESH = pl.DeviceIdType.MESH


_SEM = pl.BlockSpec(memory_space=pltpu.SEMAPHORE)
_ANY = pl.BlockSpec(memory_space=pl.ANY)
_FLIPS = [(dx, dy, dc) for dx in (0, 1) for dy in (0, 1) for dc in (0, 1)][1:]


def _place(shard, me_idx, dtype, *, name, dep=None):
    r, cc = shard.shape
    tr = 512 if r % 512 == 0 else r

    def body(me_ref, s_ref, *rest):
        del me_ref
        rest[-1][...] = s_ref[...].astype(dtype)

    in_specs, args = [pl.BlockSpec((tr, cc), lambda i, me: (i, 0))], [me_idx, shard]
    if dep is not None:
        in_specs.append(pl.BlockSpec(dep.shape, lambda i, me: (0, 0)))
        args.append(dep)
    grid_spec = pltpu.PrefetchScalarGridSpec(
        num_scalar_prefetch=1, grid=(r // tr,), in_specs=in_specs,
        out_specs=pl.BlockSpec((None, tr, cc), lambda i, me: (me[0], i, 0)))
    return pl.pallas_call(body, grid_spec=grid_spec, out_shape=jax.ShapeDtypeStruct((N_DEV, r, cc), dtype),
                          compiler_params=_cp("arbitrary"), name=name)(*args)


def _exchange_copies(srcs, lands, send_sems, recv_sems):
    x, y, c = lax.axis_index("x"), lax.axis_index("y"), lax.axis_index("c")
    me = 4 * x + 2 * y + c
    pairs = []
    for k, (dx, dy, dc) in enumerate(_FLIPS):
        peer = (1 - x if dx else x, 1 - y if dy else y, 1 - c if dc else c)
        pidx = 4 * peer[0] + 2 * peer[1] + peer[2]
        for a, land in enumerate(lands):
            src = land.at[me] if srcs is None else srcs[a].at[pidx]
            sem = k * len(lands) + a
            out = pltpu.make_async_remote_copy(src_ref=src, dst_ref=land.at[me], send_sem=send_sems.at[sem],
                                               recv_sem=recv_sems.at[sem], device_id=peer, device_id_type=_MESH)
            arrival = pltpu.make_async_remote_copy(src_ref=src, dst_ref=land.at[pidx], send_sem=send_sems.at[sem],
                                                   recv_sem=recv_sems.at[sem], device_id=peer, device_id_type=_MESH)
            pairs.append((out, arrival))
    return pairs


def _exchange_start(srcs, lands, *, name):
    n = len(lands)
    bufs = list(lands) if srcs is None else list(srcs) + list(lands)
    nb = len(bufs)

    def body(*refs):
        ins = refs[:nb]
        send_sems, recv_sems = refs[nb], refs[nb + 1]
        token = refs[-1]
        for out, _ in _exchange_copies(None if srcs is None else ins[:n], ins[nb - n:], send_sems, recv_sems):
            out.start()
        token[...] = jnp.zeros_like(token)

    sems = pltpu.SemaphoreType.DMA((len(_FLIPS) * n,))
    res = pl.pallas_call(
        body, name=name, in_specs=[_HBM] * nb,
        out_shape=(sems, sems, *[pltpu.HBM(b.shape, b.dtype) for b in bufs], jax.ShapeDtypeStruct((SUBLANES, 128), F32)),
        out_specs=(_SEM, _SEM, *[_HBM] * nb, pl.BlockSpec(memory_space=pltpu.VMEM)),
        input_output_aliases={i: 2 + i for i in range(nb)},
        compiler_params=pltpu.CompilerParams(has_side_effects=pltpu.SideEffectType.DATAFLOW_SIDE_EFFECTING),
    )(*[pltpu.with_memory_space_constraint(b, pltpu.HBM) for b in bufs])
    return res[0], res[1], list(res[2:2 + nb]), res[-1]


def _exchange_wait(send_sems, recv_sems, bufs, *, scatter, after, name):
    nb = len(bufs)
    n = nb // 2 if scatter else nb

    def body(*refs):
        ins = refs[:nb]
        for out, arrival in _exchange_copies(ins[:n] if scatter else None, ins[nb - n:], refs[nb], refs[nb + 1]):
            out.wait_send()
            arrival.wait_recv()

    extra = [] if after is None else [after]
    res = pl.pallas_call(
        body, name=name, in_specs=[_HBM] * nb + [_SEM, _SEM] + [_ANY] * len(extra),
        out_shape=tuple(pltpu.HBM(b.shape, b.dtype) for b in bufs), out_specs=tuple([_HBM] * nb),
        input_output_aliases={i: i for i in range(nb)},
        compiler_params=pltpu.CompilerParams(has_side_effects=pltpu.SideEffectType.DATAFLOW_SIDE_EFFECTING),
    )(*bufs, send_sems, recv_sems, *extra)
    return list(res)


def _adamw_math(w, g, m, v):
    m = ADAM_B1 * m + (1.0 - ADAM_B1) * g
    v = ADAM_B2 * v + (1.0 - ADAM_B2) * (g * g)
    m_hat = m / (1.0 - ADAM_B1 ** ADAM_STEP)
    v_hat = v / (1.0 - ADAM_B2 ** ADAM_STEP)
    delta = -ADAM_LR * (m_hat / (jnp.sqrt(v_hat) + ADAM_EPS) + ADAM_WD * w)
    return delta, m, v


def _adamw(w, m, v, parts, prev, layer, me_idx, *, name, own=None):
    L, r, cc = w.shape
    P = parts.shape[0]
    tr = 512 if r % 512 == 0 else r
    if prev is None:
        prev = tuple(lax.empty(w.shape, F32) for _ in range(4))

    def body(me_ref, w_ref, m_ref, v_ref, p_ref, *rest):
        g_ref, d_ref, nm_ref, nv_ref = rest[-4:]
        if own is None:
            g = p_ref[0].astype(F32)
            for q in range(1, P):
                g = g + p_ref[q].astype(F32)
        else:
            me = me_ref[0]
            g = rest[0][...].astype(F32)
            for q in range(P):
                g = g + jnp.where(q == me, 0.0, p_ref[q].astype(F32))
        d, nm, nv = _adamw_math(w_ref[...], g, m_ref[...], v_ref[...])
        g_ref[...] = g
        d_ref[...] = d
        nm_ref[...] = nm
        nv_ref[...] = nv

    blk = pl.BlockSpec((None, tr, cc), lambda i, me: (layer, i, 0))
    in_specs = [blk, blk, blk, pl.BlockSpec((P, tr, cc), lambda i, me: (0, i, 0))]
    args = [me_idx, w, m, v, parts]
    if own is not None:
        in_specs.append(pl.BlockSpec((None, tr, cc), lambda i, me: (me[0], i, 0)))
        args.append(own)
    first_prev = len(args)
    in_specs += [_ANY] * 4
    args += list(prev)
    grid_spec = pltpu.PrefetchScalarGridSpec(num_scalar_prefetch=1, grid=(r // tr,), in_specs=in_specs,
                                             out_specs=(blk, blk, blk, blk))
    o = jax.ShapeDtypeStruct(w.shape, F32)
    return pl.pallas_call(body, grid_spec=grid_spec, out_shape=(o, o, o, o),
                          input_output_aliases={first_prev + i: i for i in range(4)},
                          compiler_params=_cp("parallel"), name=name)(*args)


_SHARDED = ("w_in", "conv_a_w", "w_a_out", "conv_b_w", "w_b_out", "w_o", "w_1", "w_2")
_COL_SHARDED = ("w_in", "conv_a_w", "conv_b_w", "w_1")
_REPLICATED = ("g_mix", "b_in", "conv_a_b", "ln_g", "ln_b", "conv_b_b", "w_rg_a", "b_rg_a", "w_rg_x", "b_rg_x", "lam",
               "g_mlp")
_WEIGHTS = ("g_mix", "w_in", "b_in", "conv_a_w", "conv_a_b", "ln_g", "ln_b", "w_a_out", "conv_b_w", "conv_b_b", "w_rg_a",
            "b_rg_a", "w_rg_x", "b_rg_x", "lam", "w_b_out", "w_o", "g_mlp", "w_1", "w_2", "g_final")
_LANES = 128


def _cols_from_blocks(b):
    nb, K, n = b.shape
    return b.transpose(1, 0, 2).reshape(K, nb * n)


def _blocks_from_cols(w, K):
    n = w.shape[1] // N_DEV
    return w[:K].reshape(K, N_DEV, n).transpose(1, 0, 2)


def kernel(x, g_mix, w_in, b_in, conv_a_w, conv_a_b, ln_g, ln_b, w_a_out, conv_b_w, conv_b_b, w_rg_a, b_rg_a, w_rg_x, b_rg_x, lam, w_b_out, w_o, g_mlp, w_1, w_2, g_final, loss_target, m_g_mix, m_w_in, m_b_in, m_conv_a_w, m_conv_a_b, m_ln_g, m_ln_b, m_w_a_out, m_conv_b_w, m_conv_b_b, m_w_rg_a, m_b_rg_a, m_w_rg_x, m_b_rg_x, m_lam, m_w_b_out, m_w_o, m_g_mlp, m_w_1, m_w_2, m_g_final, v_g_mix, v_w_in, v_b_in, v_conv_a_w, v_conv_a_b, v_ln_g, v_ln_b, v_w_a_out, v_conv_b_w, v_conv_b_b, v_w_rg_a, v_b_rg_a, v_w_rg_x, v_b_rg_x, v_lam, v_w_b_out, v_w_o, v_g_mlp, v_w_1, v_w_2, v_g_final):
    W = dict(g_mix=g_mix, w_in=w_in, b_in=b_in, conv_a_w=conv_a_w, conv_a_b=conv_a_b, ln_g=ln_g, ln_b=ln_b,
             w_a_out=w_a_out, conv_b_w=conv_b_w, conv_b_b=conv_b_b, w_rg_a=w_rg_a, b_rg_a=b_rg_a, w_rg_x=w_rg_x,
             b_rg_x=b_rg_x, lam=lam, w_b_out=w_b_out, w_o=w_o, g_mlp=g_mlp, w_1=w_1, w_2=w_2, g_final=g_final)
    M = dict(g_mix=m_g_mix, w_in=m_w_in, b_in=m_b_in, conv_a_w=m_conv_a_w, conv_a_b=m_conv_a_b, ln_g=m_ln_g, ln_b=m_ln_b,
             w_a_out=m_w_a_out, conv_b_w=m_conv_b_w, conv_b_b=m_conv_b_b, w_rg_a=m_w_rg_a, b_rg_a=m_b_rg_a,
             w_rg_x=m_w_rg_x, b_rg_x=m_b_rg_x, lam=m_lam, w_b_out=m_w_b_out, w_o=m_w_o, g_mlp=m_g_mlp, w_1=m_w_1,
             w_2=m_w_2, g_final=m_g_final)
    V = dict(g_mix=v_g_mix, w_in=v_w_in, b_in=v_b_in, conv_a_w=v_conv_a_w, conv_a_b=v_conv_a_b, ln_g=v_ln_g, ln_b=v_ln_b,
             w_a_out=v_w_a_out, conv_b_w=v_conv_b_w, conv_b_b=v_conv_b_b, w_rg_a=v_w_rg_a, b_rg_a=v_b_rg_a,
             w_rg_x=v_w_rg_x, b_rg_x=v_b_rg_x, lam=v_lam, w_b_out=v_w_b_out, w_o=v_w_o, g_mlp=v_g_mlp, w_1=v_w_1,
             w_2=v_w_2, g_final=v_g_final)
    NB, S, D = x.shape
    L = g_mix.shape[0]
    hd = w_rg_a.shape[-1]
    me_idx = (4 * lax.axis_index("x") + 2 * lax.axis_index("y") + lax.axis_index("c")).astype(jnp.int32).reshape(1)

    stages = (("in", ("w_in",)), ("mix", ("conv_a_w", "w_a_out", "conv_b_w", "w_b_out", "w_o")), ("mlp", ("w_1", "w_2")))
    gathers = {}
    started = jnp.zeros((), F32)
    first = None
    for l in range(L):
        for stage, names in stages:
            lands = [_place(W[k][l], me_idx, F32 if k.startswith("conv") else BF16, dep=first, name="place_" + k)
                     for k in names]
            send_sems, recv_sems, bufs, token = _exchange_start(None, lands, name=f"weights_start_{stage}_{l}")
            gathers[l, stage] = (names, send_sems, recv_sems, bufs)
            started = started + token[0, 0]
            if first is None:
                first = token

    xt = x.reshape(NB * S, D)
    layers, saved = [], []
    for l in range(L):
        p = {k: W[k][l][None] for k in ("g_mix", "b_in", "conv_a_b", "ln_g", "ln_b", "conv_b_b", "b_rg_a", "b_rg_x",
                                         "lam", "g_mlp")}
        if l == 0:
            p["g_mix"] = p["g_mix"] + started
        p["wg_a"] = _group_weights(w_rg_a[l])
        p["wg_x"] = _group_weights(w_rg_x[l])

        def fetch(stage, after, l=l, p=p):
            names, send_sems, recv_sems, bufs = gathers[l, stage]
            bufs = _exchange_wait(send_sems, recv_sems, bufs, scatter=False, after=after,
                                  name=f"weights_wait_{stage}_{l}")
            for k, full in zip(names, bufs):
                if k in ("w_in", "w_1"):
                    p[k] = full
                elif k in _COL_SHARDED:
                    p[k] = _cols_from_blocks(full)
                else:
                    p[k] = full.reshape(-1, full.shape[-1])

        layers.append(p)
        xt, sv = _layer_fwd(xt, p, S=S, fetch=fetch)
        saved.append(sv)
    loss, dx, dxb, dg_final = _final_loss(xt, g_final[None], loss_target.reshape(NB * S, D), name="final_loss")
    dx = (dx, dxb)

    results = {k: None for k in _SHARDED}

    def scatter_start(l, names, g, tag):
        srcs = []
        for k in names:
            shard_shape = W[k].shape[1:]
            if k in ("w_in", "w_1"):
                srcs.append(g[k])
            elif k in _COL_SHARDED:
                srcs.append(_blocks_from_cols(g[k], shard_shape[0]).astype(BF16))
            else:
                srcs.append(g[k].reshape((N_DEV,) + shard_shape))
        lands = [lax.empty(s.shape, BF16) for s in srcs]
        send_sems, recv_sems, bufs, token = _exchange_start(srcs, lands, name=f"grads_start_{tag}_{l}")
        return (names, send_sems, recv_sems, bufs, tag), token

    def scatter_finish(l, flight, after):
        names, send_sems, recv_sems, bufs, tag = flight
        bufs = _exchange_wait(send_sems, recv_sems, bufs, scatter=True, after=after, name=f"grads_wait_{tag}_{l}")
        n = len(names)
        for k, own, land in zip(names, bufs[:n], bufs[n:]):
            results[k] = _adamw(W[k], M[k], V[k], land, results[k], l, me_idx, own=own, name="adamw_" + k)

    gate_names = ("w_rg_a", "w_rg_x")
    gate_results = {k: None for k in gate_names}

    def small_start(l, g):
        lands = [_place(g[k].reshape(-1, hd), me_idx, F32, name="place_gate_grad") for k in gate_names]
        send_sems, recv_sems, bufs, token = _exchange_start(None, lands, name=f"small_start_{l}")
        return (send_sems, recv_sems, bufs), token

    def small_finish(l, flight, after):
        send_sems, recv_sems, bufs = flight
        bufs = _exchange_wait(send_sems, recv_sems, bufs, scatter=False, after=after, name=f"small_wait_{l}")
        for k, g_all in zip(gate_names, bufs):
            gate_results[k] = _adamw(W[k].reshape(L, -1, hd), M[k].reshape(L, -1, hd), V[k].reshape(L, -1, hd), g_all,
                                     gate_results[k], l, me_idx, name="adamw_gate")

    grads = [None] * L
    in_flight = []
    dep = None
    for l in reversed(range(L)):
        dx_mid, g = _layer_bwd_mlp(dx, layers[l], saved[l], wdt=BF16, dep=dep)
        f_mlp, dep = scatter_start(l, ("w_2", "w_1"), g, "mlp")
        flights = [f_mlp]

        small = []

        def on_gate_grads(g_part, l=l, small=small):
            f_small, token = small_start(l, g_part)
            small.append(f_small)
            return token

        def on_weight_grads(g_part, l=l, flights=flights):
            f_mix, token = scatter_start(l, ("w_o", "w_b_out", "w_a_out", "conv_a_w", "conv_b_w", "w_in"), g_part, "mix")
            flights.append(f_mix)
            return token

        dx, g_mix_part = _layer_bwd_mix(dx_mid, layers[l], saved[l], S=S, wdt=BF16, dep=dep,
                                        on_gate_grads=on_gate_grads, on_weight_grads=on_weight_grads)
        dep = None
        g.update(g_mix_part)
        grads[l] = g
        for l_prev, fs, f_sm in in_flight:
            for flight in fs:
                scatter_finish(l_prev, flight, dx[0])
            small_finish(l_prev, f_sm, dx[0])
        in_flight = [(l, flights, small[0])]

    vec_names = tuple(k for k in _REPLICATED if k not in gate_names)
    n_vec = sum(W[k].shape[1] for k in vec_names)

    def vec_pack(rows, final, last):
        tail = jnp.concatenate([final.reshape(1, -1), jnp.broadcast_to(last.reshape(1, 1), (1, _LANES))], axis=1)
        tail = jnp.pad(tail, ((0, SUBLANES - L - 1), (0, n_vec - tail.shape[1])))
        return jnp.concatenate([rows, tail], axis=0)

    g_rows = jnp.concatenate([jnp.concatenate([grads[l][k] for k in vec_names], axis=1) for l in range(L)], axis=0)
    land = _place(vec_pack(g_rows, dg_final, loss[0, :1]), me_idx, F32, name="place_vectors")
    vec_send_sems, vec_recv_sems, vec_bufs, _ = _exchange_start(None, [land], name="vectors_start")
    for l_prev, fs, f_sm in in_flight:
        after = dx[0] if L == 1 else results["w_in"][0]
        for flight in fs:
            scatter_finish(l_prev, flight, after)
            after = results[flight[0][-1]][0]
        small_finish(l_prev, f_sm, after)
    out_g, out_d, out_m, out_v = {}, {}, {}, {}
    for k in _SHARDED:
        out_g[k], out_d[k], out_m[k], out_v[k] = results[k]
    for k in gate_names:
        out_g[k], out_d[k], out_m[k], out_v[k] = (a.reshape(W[k].shape) for a in gate_results[k])

    zero = jnp.zeros((1,), F32)
    (g_all,) = _exchange_wait(vec_send_sems, vec_recv_sems, vec_bufs, scatter=False, after=results["w_in"][0],
                              name="vectors_wait")
    vec_out = _adamw(vec_pack(jnp.concatenate([W[k] for k in vec_names], axis=1), g_final, zero)[None],
                     vec_pack(jnp.concatenate([M[k] for k in vec_names], axis=1), m_g_final, zero)[None],
                     vec_pack(jnp.concatenate([V[k] for k in vec_names], axis=1), v_g_final, zero)[None],
                     g_all, None, 0, me_idx, name="adamw_vectors")
    vec_out = [a[0] for a in vec_out]
    for res, arr in zip((out_g, out_d, out_m, out_v), vec_out):
        off = 0
        for k in vec_names:
            res[k] = arr[:L, off:off + W[k].shape[1]]
            off += W[k].shape[1]
        res["g_final"] = arr[L, :g_final.size]
    loss_out = vec_out[0][L, g_final.size]

    return (loss_out, dx[0].reshape(NB, S, D), *[out_g[k] for k in _WEIGHTS], *[out_d[k] for k in _WEIGHTS],
            *[out_m[k] for k in _WEIGHTS], *[out_v[k] for k in _WEIGHTS])
```
